```python
import jax, jax.numpy as jnp
from jax import lax
import numpy as np

D_MODEL = 1024
BATCH = 8
SEQ = 4096
DEPTH = 1

D_FF = 2816
D_A = D_MODEL
D_B = D_MODEL
GROUP = 128
CONV_A = 31
CONV_B = 3
EPS = 1e-6
SPLITS = (D_A, 2 * D_A, 2 * D_A + D_B, 2 * D_A + 2 * D_B, 2 * D_A + 3 * D_B, 2 * D_A + 3 * D_B + D_MODEL)
D_IN = 2 * D_A + 3 * D_B + 2 * D_MODEL

kernel_name = "macaron_gated_conformer_shortconv_hybrid"


def rmsnorm(x, g):
    xf = x.astype(jnp.float32)
    y = xf * lax.rsqrt(jnp.mean(xf * xf, axis=-1, keepdims=True) + EPS)
    return (y * g.astype(jnp.float32)).astype(x.dtype)


def layernorm(x, g, b):
    xf = x.astype(jnp.float32)
    mu = jnp.mean(xf, axis=-1, keepdims=True)
    var = jnp.mean(jnp.square(xf - mu), axis=-1, keepdims=True)
    y = (xf - mu) * lax.rsqrt(var + EPS)
    return (y * g.astype(jnp.float32) + b.astype(jnp.float32)).astype(x.dtype)


def swiglu(x, w_gate, w_up, w_down):
    return (jax.nn.silu(x @ w_gate) * (x @ w_up)) @ w_down


def causal_depthwise_conv(x, w):
    k, c = w.shape
    return lax.conv_general_dilated(
        x, w[:, None, :].astype(x.dtype), window_strides=(1,), padding=((k - 1, 0),),
        dimension_numbers=("NWC", "WIO", "NWC"), feature_group_count=c)


def _fwd_setup_inputs(seed: int = 0) -> dict:
    key = jax.random.key(seed)
    ks = jax.random.split(key, 24)
    f32 = jnp.float32

    def nrm(k, shape, fan_in):
        return jax.random.normal(k, shape, f32) * (fan_in ** -0.5)

    def gain(k, shape):
        return 1.0 + 0.01 * jax.random.normal(k, shape, f32)

    L = DEPTH
    return {
        "x": jax.random.normal(ks[0], (BATCH, SEQ, D_MODEL), f32),
        "ffn1_norm": gain(ks[1], (L, D_MODEL)),
        "ffn1_w_gate": nrm(ks[2], (L, D_MODEL, D_FF), D_MODEL),
        "ffn1_w_up": nrm(ks[3], (L, D_MODEL, D_FF), D_MODEL),
        "ffn1_w_down": nrm(ks[4], (L, D_FF, D_MODEL), D_FF),
        "mix_norm": gain(ks[5], (L, D_MODEL)),
        "w_in": nrm(ks[6], (L, D_MODEL, D_IN), D_MODEL),
        "a_dw_w": nrm(ks[7], (L, CONV_A, D_A), CONV_A),
        "a_dw_b": 0.01 * jax.random.normal(ks[8], (L, D_A), f32),
        "a_ln_g": gain(ks[9], (L, D_A)),
        "a_ln_b": 0.01 * jax.random.normal(ks[10], (L, D_A), f32),
        "a_w_out": nrm(ks[11], (L, D_A, D_MODEL), D_A),
        "b_conv_w": nrm(ks[12], (L, CONV_B, D_B), CONV_B),
        "b_w_out": nrm(ks[13], (L, D_B, D_MODEL), D_B),
        "w_o": nrm(ks[14], (L, D_MODEL, D_MODEL), D_MODEL),
        "ffn2_norm": gain(ks[15], (L, D_MODEL)),
        "ffn2_w_gate": nrm(ks[16], (L, D_MODEL, D_FF), D_MODEL),
        "ffn2_w_up": nrm(ks[17], (L, D_MODEL, D_FF), D_MODEL),
        "ffn2_w_down": nrm(ks[18], (L, D_FF, D_MODEL), D_FF),
        "final_norm": gain(ks[19], (D_MODEL,)),
    }


def _fwd_reference(x, ffn1_norm, ffn1_w_gate, ffn1_w_up, ffn1_w_down, mix_norm, w_in,
              a_dw_w, a_dw_b, a_ln_g, a_ln_b, a_w_out, b_conv_w, b_w_out, w_o,
              ffn2_norm, ffn2_w_gate, ffn2_w_up, ffn2_w_down, final_norm):
    h = x
    for l in range(DEPTH):
        h = h + 0.5 * swiglu(rmsnorm(h, ffn1_norm[l]), ffn1_w_gate[l], ffn1_w_up[l], ffn1_w_down[l])

        u = rmsnorm(h, mix_norm[l])
        z = u @ w_in[l]
        a_val, a_gate, b_B, b_C, b_x, g_a, g_b = jnp.split(z, SPLITS, axis=-1)

        a = a_val * jax.nn.sigmoid(a_gate)
        a = causal_depthwise_conv(a, a_dw_w[l]) + a_dw_b[l]
        a = jax.nn.silu(layernorm(a, a_ln_g[l], a_ln_b[l]))
        y_a = a @ a_w_out[l]

        v = causal_depthwise_conv(b_C * b_x, b_conv_w[l])
        y_b = (b_B * v) @ b_w_out[l]

        m = jax.nn.sigmoid(g_a) * y_a + jax.nn.sigmoid(g_b) * y_b
        h = h + m @ w_o[l]

        h = h + 0.5 * swiglu(rmsnorm(h, ffn2_norm[l]), ffn2_w_gate[l], ffn2_w_up[l], ffn2_w_down[l])
    return rmsnorm(h, final_norm)


import jax as _jax
import jax.numpy as _jnp

TWIN_FORMAT = 'train_step'
FWD_PARAMS = ['x', 'ffn1_norm', 'ffn1_w_gate', 'ffn1_w_up', 'ffn1_w_down', 'mix_norm', 'w_in', 'a_dw_w', 'a_dw_b', 'a_ln_g', 'a_ln_b', 'a_w_out', 'b_conv_w', 'b_w_out', 'w_o', 'ffn2_norm', 'ffn2_w_gate', 'ffn2_w_up', 'ffn2_w_down', 'final_norm']
TWIN_WEIGHTS = ['ffn1_norm', 'ffn1_w_gate', 'ffn1_w_up', 'ffn1_w_down', 'mix_norm', 'w_in', 'a_dw_w', 'a_dw_b', 'a_ln_g', 'a_ln_b', 'a_w_out', 'b_conv_w', 'b_w_out', 'w_o', 'ffn2_norm', 'ffn2_w_gate', 'ffn2_w_up', 'ffn2_w_down', 'final_norm']
TWIN_DIFF_INPUT = 'x'
TWIN_INPUTS = ['x', 'ffn1_norm', 'ffn1_w_gate', 'ffn1_w_up', 'ffn1_w_down', 'mix_norm', 'w_in', 'a_dw_w', 'a_dw_b', 'a_ln_g', 'a_ln_b', 'a_w_out', 'b_conv_w', 'b_w_out', 'w_o', 'ffn2_norm', 'ffn2_w_gate', 'ffn2_w_up', 'ffn2_w_down', 'final_norm', 'loss_target', 'm_ffn1_norm', 'm_ffn1_w_gate', 'm_ffn1_w_up', 'm_ffn1_w_down', 'm_mix_norm', 'm_w_in', 'm_a_dw_w', 'm_a_dw_b', 'm_a_ln_g', 'm_a_ln_b', 'm_a_w_out', 'm_b_conv_w', 'm_b_w_out', 'm_w_o', 'm_ffn2_norm', 'm_ffn2_w_gate', 'm_ffn2_w_up', 'm_ffn2_w_down', 'm_final_norm', 'v_ffn1_norm', 'v_ffn1_w_gate', 'v_ffn1_w_up', 'v_ffn1_w_down', 'v_mix_norm', 'v_w_in', 'v_a_dw_w', 'v_a_dw_b', 'v_a_ln_g', 'v_a_ln_b', 'v_a_w_out', 'v_b_conv_w', 'v_b_w_out', 'v_w_o', 'v_ffn2_norm', 'v_ffn2_w_gate', 'v_ffn2_w_up', 'v_ffn2_w_down', 'v_final_norm']
TWIN_OUTPUTS = ['loss', 'grad_x', 'grad_ffn1_norm', 'grad_ffn1_w_gate', 'grad_ffn1_w_up', 'grad_ffn1_w_down', 'grad_mix_norm', 'grad_w_in', 'grad_a_dw_w', 'grad_a_dw_b', 'grad_a_ln_g', 'grad_a_ln_b', 'grad_a_w_out', 'grad_b_conv_w', 'grad_b_w_out', 'grad_w_o', 'grad_ffn2_norm', 'grad_ffn2_w_gate', 'grad_ffn2_w_up', 'grad_ffn2_w_down', 'grad_final_norm', 'delta_ffn1_norm', 'delta_ffn1_w_gate', 'delta_ffn1_w_up', 'delta_ffn1_w_down', 'delta_mix_norm', 'delta_w_in', 'delta_a_dw_w', 'delta_a_dw_b', 'delta_a_ln_g', 'delta_a_ln_b', 'delta_a_w_out', 'delta_b_conv_w', 'delta_b_w_out', 'delta_w_o', 'delta_ffn2_norm', 'delta_ffn2_w_gate', 'delta_ffn2_w_up', 'delta_ffn2_w_down', 'delta_final_norm', 'new_m_ffn1_norm', 'new_m_ffn1_w_gate', 'new_m_ffn1_w_up', 'new_m_ffn1_w_down', 'new_m_mix_norm', 'new_m_w_in', 'new_m_a_dw_w', 'new_m_a_dw_b', 'new_m_a_ln_g', 'new_m_a_ln_b', 'new_m_a_w_out', 'new_m_b_conv_w', 'new_m_b_w_out', 'new_m_w_o', 'new_m_ffn2_norm', 'new_m_ffn2_w_gate', 'new_m_ffn2_w_up', 'new_m_ffn2_w_down', 'new_m_final_norm', 'new_v_ffn1_norm', 'new_v_ffn1_w_gate', 'new_v_ffn1_w_up', 'new_v_ffn1_w_down', 'new_v_mix_norm', 'new_v_w_in', 'new_v_a_dw_w', 'new_v_a_dw_b', 'new_v_a_ln_g', 'new_v_a_ln_b', 'new_v_a_w_out', 'new_v_b_conv_w', 'new_v_b_w_out', 'new_v_w_o', 'new_v_ffn2_norm', 'new_v_ffn2_w_gate', 'new_v_ffn2_w_up', 'new_v_ffn2_w_down', 'new_v_final_norm']
TWIN_LEAF_KINDS = {'loss': 'loss', 'grad_x': 'grad_x', 'grad_ffn1_norm': 'grad_w', 'grad_ffn1_w_gate': 'grad_w', 'grad_ffn1_w_up': 'grad_w', 'grad_ffn1_w_down': 'grad_w', 'grad_mix_norm': 'grad_w', 'grad_w_in': 'grad_w', 'grad_a_dw_w': 'grad_w', 'grad_a_dw_b': 'grad_w', 'grad_a_ln_g': 'grad_w', 'grad_a_ln_b': 'grad_w', 'grad_a_w_out': 'grad_w', 'grad_b_conv_w': 'grad_w', 'grad_b_w_out': 'grad_w', 'grad_w_o': 'grad_w', 'grad_ffn2_norm': 'grad_w', 'grad_ffn2_w_gate': 'grad_w', 'grad_ffn2_w_up': 'grad_w', 'grad_ffn2_w_down': 'grad_w', 'grad_final_norm': 'grad_w', 'delta_ffn1_norm': 'delta_w', 'delta_ffn1_w_gate': 'delta_w', 'delta_ffn1_w_up': 'delta_w', 'delta_ffn1_w_down': 'delta_w', 'delta_mix_norm': 'delta_w', 'delta_w_in': 'delta_w', 'delta_a_dw_w': 'delta_w', 'delta_a_dw_b': 'delta_w', 'delta_a_ln_g': 'delta_w', 'delta_a_ln_b': 'delta_w', 'delta_a_w_out': 'delta_w', 'delta_b_conv_w': 'delta_w', 'delta_b_w_out': 'delta_w', 'delta_w_o': 'delta_w', 'delta_ffn2_norm': 'delta_w', 'delta_ffn2_w_gate': 'delta_w', 'delta_ffn2_w_up': 'delta_w', 'delta_ffn2_w_down': 'delta_w', 'delta_final_norm': 'delta_w', 'new_m_ffn1_norm': 'new_m', 'new_m_ffn1_w_gate': 'new_m', 'new_m_ffn1_w_up': 'new_m', 'new_m_ffn1_w_down': 'new_m', 'new_m_mix_norm': 'new_m', 'new_m_w_in': 'new_m', 'new_m_a_dw_w': 'new_m', 'new_m_a_dw_b': 'new_m', 'new_m_a_ln_g': 'new_m', 'new_m_a_ln_b': 'new_m', 'new_m_a_w_out': 'new_m', 'new_m_b_conv_w': 'new_m', 'new_m_b_w_out': 'new_m', 'new_m_w_o': 'new_m', 'new_m_ffn2_norm': 'new_m', 'new_m_ffn2_w_gate': 'new_m', 'new_m_ffn2_w_up': 'new_m', 'new_m_ffn2_w_down': 'new_m', 'new_m_final_norm': 'new_m', 'new_v_ffn1_norm': 'new_v', 'new_v_ffn1_w_gate': 'new_v', 'new_v_ffn1_w_up': 'new_v', 'new_v_ffn1_w_down': 'new_v', 'new_v_mix_norm': 'new_v', 'new_v_w_in': 'new_v', 'new_v_a_dw_w': 'new_v', 'new_v_a_dw_b': 'new_v', 'new_v_a_ln_g': 'new_v', 'new_v_a_ln_b': 'new_v', 'new_v_a_w_out': 'new_v', 'new_v_b_conv_w': 'new_v', 'new_v_b_w_out': 'new_v', 'new_v_w_o': 'new_v', 'new_v_ffn2_norm': 'new_v', 'new_v_ffn2_w_gate': 'new_v', 'new_v_ffn2_w_up': 'new_v', 'new_v_ffn2_w_down': 'new_v', 'new_v_final_norm': 'new_v'}


def _forward(args):
    return _fwd_reference(*[args[k] for k in FWD_PARAMS])


def _output_shape():
    out = _jax.eval_shape(lambda: _forward(_fwd_setup_inputs(0)))
    return out.shape, out.dtype

N_MICROBATCH = 1
ADAM_LR = 0.001
ADAM_B1 = 0.9
ADAM_B2 = 0.999
ADAM_EPS = 1e-08
ADAM_WD = 0.01
ADAM_STEP = 10
PER_EXAMPLE_BATCH_AXIS = {'x': 0, 'loss_target': 0}
SHARED_INPUTS = []
_WEIGHT_DTYPES = {'ffn1_norm': _jnp.float32, 'ffn1_w_gate': _jnp.float32, 'ffn1_w_up': _jnp.float32, 'ffn1_w_down': _jnp.float32, 'mix_norm': _jnp.float32, 'w_in': _jnp.float32, 'a_dw_w': _jnp.float32, 'a_dw_b': _jnp.float32, 'a_ln_g': _jnp.float32, 'a_ln_b': _jnp.float32, 'a_w_out': _jnp.float32, 'b_conv_w': _jnp.float32, 'b_w_out': _jnp.float32, 'w_o': _jnp.float32, 'ffn2_norm': _jnp.float32, 'ffn2_w_gate': _jnp.float32, 'ffn2_w_up': _jnp.float32, 'ffn2_w_down': _jnp.float32, 'final_norm': _jnp.float32}
MOMENT_SCALE = {'ffn1_norm': 9.049586e-02, 'ffn1_w_gate': 3.901146e-02, 'ffn1_w_up': 3.765221e-02, 'ffn1_w_down': 6.242842e-02, 'mix_norm': 1.492734e-01, 'w_in': 5.761486e-02, 'a_dw_w': 5.068497e-02, 'a_dw_b': 1.076283e-01, 'a_ln_g': 6.442208e-02, 'a_ln_b': 5.138925e-02, 'a_w_out': 4.895678e-02, 'b_conv_w': 8.108803e-02, 'b_w_out': 8.016387e-02, 'w_o': 9.369169e-02, 'ffn2_norm': 6.240813e-02, 'ffn2_w_gate': 2.625792e-02, 'ffn2_w_up': 2.547723e-02, 'ffn2_w_down': 4.214851e-02, 'final_norm': 3.189867e+01}


def _to_microbatches(a, axis):
    t = _jnp.moveaxis(a, axis, 0)
    t = t.reshape((N_MICROBATCH, t.shape[0] // N_MICROBATCH) + t.shape[1:])
    return _jnp.moveaxis(t, 1, axis + 1)


def setup_inputs(seed: int = 0) -> dict:
    inp = _fwd_setup_inputs(seed)
    key = _jax.random.fold_in(_jax.random.key(seed), 7919)
    shape, _ = _output_shape()
    out = dict(inp)
    out["loss_target"] = _jax.random.normal(_jax.random.fold_in(key, 0), shape, _jnp.float32)
    for i, name in enumerate(TWIN_WEIGHTS):
        w = inp[name].astype(_jnp.float32)
        if MOMENT_SCALE is None:
            s = _jnp.sqrt(_jnp.mean(_jnp.square(w)) + 1e-30)
        else:
            s = MOMENT_SCALE[name]
        km, kv = _jax.random.split(_jax.random.fold_in(key, i + 1))
        out[name] = w
        out["m_" + name] = s * _jax.random.normal(km, w.shape, _jnp.float32)
        out["v_" + name] = (s * s) * _jax.random.uniform(kv, w.shape, _jnp.float32, 0.5, 1.5)
    if N_MICROBATCH > 1:
        for name, axis in PER_EXAMPLE_BATCH_AXIS.items():
            out[name] = _to_microbatches(out[name], axis)
    return {'x': out['x'], 'ffn1_norm': out['ffn1_norm'], 'ffn1_w_gate': out['ffn1_w_gate'], 'ffn1_w_up': out['ffn1_w_up'], 'ffn1_w_down': out['ffn1_w_down'], 'mix_norm': out['mix_norm'], 'w_in': out['w_in'], 'a_dw_w': out['a_dw_w'], 'a_dw_b': out['a_dw_b'], 'a_ln_g': out['a_ln_g'], 'a_ln_b': out['a_ln_b'], 'a_w_out': out['a_w_out'], 'b_conv_w': out['b_conv_w'], 'b_w_out': out['b_w_out'], 'w_o': out['w_o'], 'ffn2_norm': out['ffn2_norm'], 'ffn2_w_gate': out['ffn2_w_gate'], 'ffn2_w_up': out['ffn2_w_up'], 'ffn2_w_down': out['ffn2_w_down'], 'final_norm': out['final_norm'], 'loss_target': out['loss_target'], 'm_ffn1_norm': out['m_ffn1_norm'], 'm_ffn1_w_gate': out['m_ffn1_w_gate'], 'm_ffn1_w_up': out['m_ffn1_w_up'], 'm_ffn1_w_down': out['m_ffn1_w_down'], 'm_mix_norm': out['m_mix_norm'], 'm_w_in': out['m_w_in'], 'm_a_dw_w': out['m_a_dw_w'], 'm_a_dw_b': out['m_a_dw_b'], 'm_a_ln_g': out['m_a_ln_g'], 'm_a_ln_b': out['m_a_ln_b'], 'm_a_w_out': out['m_a_w_out'], 'm_b_conv_w': out['m_b_conv_w'], 'm_b_w_out': out['m_b_w_out'], 'm_w_o': out['m_w_o'], 'm_ffn2_norm': out['m_ffn2_norm'], 'm_ffn2_w_gate': out['m_ffn2_w_gate'], 'm_ffn2_w_up': out['m_ffn2_w_up'], 'm_ffn2_w_down': out['m_ffn2_w_down'], 'm_final_norm': out['m_final_norm'], 'v_ffn1_norm': out['v_ffn1_norm'], 'v_ffn1_w_gate': out['v_ffn1_w_gate'], 'v_ffn1_w_up': out['v_ffn1_w_up'], 'v_ffn1_w_down': out['v_ffn1_w_down'], 'v_mix_norm': out['v_mix_norm'], 'v_w_in': out['v_w_in'], 'v_a_dw_w': out['v_a_dw_w'], 'v_a_dw_b': out['v_a_dw_b'], 'v_a_ln_g': out['v_a_ln_g'], 'v_a_ln_b': out['v_a_ln_b'], 'v_a_w_out': out['v_a_w_out'], 'v_b_conv_w': out['v_b_conv_w'], 'v_b_w_out': out['v_b_w_out'], 'v_w_o': out['v_w_o'], 'v_ffn2_norm': out['v_ffn2_norm'], 'v_ffn2_w_gate': out['v_ffn2_w_gate'], 'v_ffn2_w_up': out['v_ffn2_w_up'], 'v_ffn2_w_down': out['v_ffn2_w_down'], 'v_final_norm': out['v_final_norm']}


def _loss(weights, diff, rest, loss_target):
    with _jax.named_scope("forward"):
        args = {**rest, TWIN_DIFF_INPUT: diff, **{k: w.astype(_WEIGHT_DTYPES[k]) for k, w in weights.items()}}
        y = _forward(args)
    with _jax.named_scope("loss_head"):
        err = _jnp.square(y.astype(_jnp.float32) - loss_target)
        return 0.5 * _jnp.sum(_jnp.mean(err, axis=-1)) if err.ndim else 0.5 * err


def _adamw(w, g, m, v):
    m = ADAM_B1 * m + (1.0 - ADAM_B1) * g
    v = ADAM_B2 * v + (1.0 - ADAM_B2) * _jnp.square(g)
    m_hat = m / (1.0 - ADAM_B1 ** ADAM_STEP)
    v_hat = v / (1.0 - ADAM_B2 ** ADAM_STEP)
    delta = -ADAM_LR * (m_hat / (_jnp.sqrt(v_hat) + ADAM_EPS) + ADAM_WD * w)
    return delta, m, v


def reference(x, ffn1_norm, ffn1_w_gate, ffn1_w_up, ffn1_w_down, mix_norm, w_in, a_dw_w, a_dw_b, a_ln_g, a_ln_b, a_w_out, b_conv_w, b_w_out, w_o, ffn2_norm, ffn2_w_gate, ffn2_w_up, ffn2_w_down, final_norm, loss_target, m_ffn1_norm, m_ffn1_w_gate, m_ffn1_w_up, m_ffn1_w_down, m_mix_norm, m_w_in, m_a_dw_w, m_a_dw_b, m_a_ln_g, m_a_ln_b, m_a_w_out, m_b_conv_w, m_b_w_out, m_w_o, m_ffn2_norm, m_ffn2_w_gate, m_ffn2_w_up, m_ffn2_w_down, m_final_norm, v_ffn1_norm, v_ffn1_w_gate, v_ffn1_w_up, v_ffn1_w_down, v_mix_norm, v_w_in, v_a_dw_w, v_a_dw_b, v_a_ln_g, v_a_ln_b, v_a_w_out, v_b_conv_w, v_b_w_out, v_w_o, v_ffn2_norm, v_ffn2_w_gate, v_ffn2_w_up, v_ffn2_w_down, v_final_norm):
    given = dict(x=x, ffn1_norm=ffn1_norm, ffn1_w_gate=ffn1_w_gate, ffn1_w_up=ffn1_w_up, ffn1_w_down=ffn1_w_down, mix_norm=mix_norm, w_in=w_in, a_dw_w=a_dw_w, a_dw_b=a_dw_b, a_ln_g=a_ln_g, a_ln_b=a_ln_b, a_w_out=a_w_out, b_conv_w=b_conv_w, b_w_out=b_w_out, w_o=w_o, ffn2_norm=ffn2_norm, ffn2_w_gate=ffn2_w_gate, ffn2_w_up=ffn2_w_up, ffn2_w_down=ffn2_w_down, final_norm=final_norm, loss_target=loss_target, m_ffn1_norm=m_ffn1_norm, m_ffn1_w_gate=m_ffn1_w_gate, m_ffn1_w_up=m_ffn1_w_up, m_ffn1_w_down=m_ffn1_w_down, m_mix_norm=m_mix_norm, m_w_in=m_w_in, m_a_dw_w=m_a_dw_w, m_a_dw_b=m_a_dw_b, m_a_ln_g=m_a_ln_g, m_a_ln_b=m_a_ln_b, m_a_w_out=m_a_w_out, m_b_conv_w=m_b_conv_w, m_b_w_out=m_b_w_out, m_w_o=m_w_o, m_ffn2_norm=m_ffn2_norm, m_ffn2_w_gate=m_ffn2_w_gate, m_ffn2_w_up=m_ffn2_w_up, m_ffn2_w_down=m_ffn2_w_down, m_final_norm=m_final_norm, v_ffn1_norm=v_ffn1_norm, v_ffn1_w_gate=v_ffn1_w_gate, v_ffn1_w_up=v_ffn1_w_up, v_ffn1_w_down=v_ffn1_w_down, v_mix_norm=v_mix_norm, v_w_in=v_w_in, v_a_dw_w=v_a_dw_w, v_a_dw_b=v_a_dw_b, v_a_ln_g=v_a_ln_g, v_a_ln_b=v_a_ln_b, v_a_w_out=v_a_w_out, v_b_conv_w=v_b_conv_w, v_b_w_out=v_b_w_out, v_w_o=v_w_o, v_ffn2_norm=v_ffn2_norm, v_ffn2_w_gate=v_ffn2_w_gate, v_ffn2_w_up=v_ffn2_w_up, v_ffn2_w_down=v_ffn2_w_down, v_final_norm=v_final_norm)
    weights = {n: given[n] for n in TWIN_WEIGHTS}
    shared = {n: given[n] for n in SHARED_INPUTS}
    per_example = {n: given[n] for n in ['x']}
    grad_fn = _jax.value_and_grad(_loss, argnums=(0, 1))

    def one_microbatch(ex, loss_target):
        ex = dict(ex)
        diff = ex.pop(TWIN_DIFF_INPUT)
        return grad_fn(weights, diff, {**shared, **ex}, loss_target)

    if N_MICROBATCH == 1:
        loss, (grad_w, grad_x) = one_microbatch(per_example, given["loss_target"])
    else:
        def body(carry, xs):
            loss_sum, grad_sum = carry
            l_k, (gw_k, gx_k) = one_microbatch(xs[0], xs[1])
            with _jax.named_scope("update"):
                return (loss_sum + l_k, _jax.tree.map(_jnp.add, grad_sum, gw_k)), gx_k

        init = (_jnp.zeros((), _jnp.float32), _jax.tree.map(_jnp.zeros_like, weights))
        (loss, grad_w), grad_x = _jax.lax.scan(body, init, (per_example, given["loss_target"]))
    with _jax.named_scope("update"):
        delta_w, new_m, new_v = {}, {}, {}
        for n in TWIN_WEIGHTS:
            delta_w[n], new_m[n], new_v[n] = _adamw(weights[n], grad_w[n], given["m_" + n], given["v_" + n])
    return (loss, grad_x, *[grad_w[n] for n in TWIN_WEIGHTS], *[delta_w[n] for n in TWIN_WEIGHTS],
            *[new_m[n] for n in TWIN_WEIGHTS], *[new_v[n] for n in TWIN_WEIGHTS])
```

```python
import functools

import jax
import jax.numpy as jnp
from jax import lax
from jax.experimental import pallas as pl
from jax.experimental.pallas import tpu as pltpu

F32 = jnp.float32
BF16 = jnp.bfloat16
EPS = 1e-6
NS = 4
HALO = 32
MESH = pl.DeviceIdType.MESH
ANY = pl.BlockSpec(memory_space=pl.ANY)

ADAM_LR = 0.001
ADAM_B1 = 0.9
ADAM_B2 = 0.999
ADAM_EPS = 1e-08
ADAM_WD = 0.01
ADAM_STEP = 10


def _cparams(n_axes, vmem_mb):
    return pltpu.CompilerParams(dimension_semantics=("arbitrary",) * n_axes, vmem_limit_bytes=vmem_mb << 20)


def _tile(n, t):
    return t if n % t == 0 else n


def _dot(a, b):
    return jnp.dot(a, b, preferred_element_type=F32)


def _dot_nt(a, b):
    return lax.dot_general(a, b, (((1,), (1,)), ((), ())), preferred_element_type=F32)


def _dot_tn(a, b):
    return lax.dot_general(a, b, (((0,), (0,)), ((), ())), preferred_element_type=F32)


def _sigmoid(x):
    return jax.nn.sigmoid(x)


def _rms_fwd(x, g):
    r = lax.rsqrt(jnp.mean(x * x, axis=-1, keepdims=True) + EPS)
    return x * r * g


def _rms_bwd(x, g, dn):
    r = lax.rsqrt(jnp.mean(x * x, axis=-1, keepdims=True) + EPS)
    xr = x * r
    dg = jnp.sum(dn * xr, axis=0, keepdims=True)
    w = dn * g
    dx = r * w - xr * (r * r) * jnp.mean(x * w, axis=-1, keepdims=True)
    return dx, dg


def _ffn_fwd(h, g, wg, wu, wd, name):
    S, D = h.shape
    Fs = wg.shape[2]
    ts = _tile(S, 512)

    def body(h_ref, g_ref, wg_ref, wu_ref, wd_ref, ho_ref, n_ref, gp_ref, up_ref, nscr, acc):
        s = pl.program_id(1)

        @pl.when(s == 0)
        def _():
            n = _rms_fwd(h_ref[...], g_ref[...]).astype(BF16)
            nscr[...] = n
            n_ref[...] = n
            acc[...] = jnp.zeros_like(acc)

        n = nscr[...]
        gp = _dot(n, wg_ref[0])
        up = _dot(n, wu_ref[0])
        gp_ref[0] = gp.astype(BF16)
        up_ref[0] = up.astype(BF16)
        a = (gp * _sigmoid(gp) * up).astype(BF16)
        acc[...] += _dot(a, wd_ref[0])

        @pl.when(s == NS - 1)
        def _():
            ho_ref[...] = h_ref[...] + 0.5 * acc[...]

    return pl.pallas_call(
        body, name=name, grid=(S // ts, NS),
        in_specs=[pl.BlockSpec((ts, D), lambda i, s: (i, 0)), pl.BlockSpec((1, D), lambda i, s: (0, 0)),
                  pl.BlockSpec((1, D, Fs), lambda i, s: (s, 0, 0)), pl.BlockSpec((1, D, Fs), lambda i, s: (s, 0, 0)),
                  pl.BlockSpec((1, Fs, D), lambda i, s: (s, 0, 0))],
        out_specs=[pl.BlockSpec((ts, D), lambda i, s: (i, 0)), pl.BlockSpec((ts, D), lambda i, s: (i, 0)),
                   pl.BlockSpec((1, ts, Fs), lambda i, s: (s, i, 0)), pl.BlockSpec((1, ts, Fs), lambda i, s: (s, i, 0))],
        out_shape=[jax.ShapeDtypeStruct((S, D), F32), jax.ShapeDtypeStruct((S, D), BF16),
                   jax.ShapeDtypeStruct((NS, S, Fs), BF16), jax.ShapeDtypeStruct((NS, S, Fs), BF16)],
        scratch_shapes=[pltpu.VMEM((ts, D), BF16), pltpu.VMEM((ts, D), F32)],
        compiler_params=_cparams(2, 48),
    )(h, g, wg, wu, wd)


def _ffn_bwd(dh, h, g, gp, up, wg, wu, wd, name):
    S, D = h.shape
    Fs = wg.shape[2]
    ts = _tile(S, 512)

    def body(dh_ref, h_ref, g_ref, gp_ref, up_ref, wg_ref, wu_ref, wd_ref,
             dhi_ref, dgp_ref, dup_ref, a_ref, do_ref, dg_ref, doscr, acc):
        i = pl.program_id(0)
        s = pl.program_id(1)

        @pl.when(s == 0)
        def _():
            d = (0.5 * dh_ref[...]).astype(BF16)
            doscr[...] = d
            do_ref[...] = d
            acc[...] = jnp.zeros_like(acc)

        @pl.when(jnp.logical_and(i == 0, s == 0))
        def _():
            dg_ref[...] = jnp.zeros_like(dg_ref)

        da = _dot_nt(doscr[...], wd_ref[0])
        gf = gp_ref[0].astype(F32)
        uf = up_ref[0].astype(F32)
        sg = _sigmoid(gf)
        si = gf * sg
        dgp = (da * uf * (sg * (1.0 + gf * (1.0 - sg)))).astype(BF16)
        dup = (da * si).astype(BF16)
        a_ref[0] = (si * uf).astype(BF16)
        dgp_ref[0] = dgp
        dup_ref[0] = dup
        acc[...] += _dot_nt(dgp, wg_ref[0]) + _dot_nt(dup, wu_ref[0])

        @pl.when(s == NS - 1)
        def _():
            dx, dg = _rms_bwd(h_ref[...], g_ref[...], acc[...])
            dhi_ref[...] = dh_ref[...] + dx
            dg_ref[...] += dg

    tok = pl.BlockSpec((ts, D), lambda i, s: (i, 0))
    hid = pl.BlockSpec((1, ts, Fs), lambda i, s: (s, i, 0))
    row = pl.BlockSpec((1, D), lambda i, s: (0, 0))
    return pl.pallas_call(
        body, name=name, grid=(S // ts, NS),
        in_specs=[tok, tok, row, hid, hid,
                  pl.BlockSpec((1, D, Fs), lambda i, s: (s, 0, 0)), pl.BlockSpec((1, D, Fs), lambda i, s: (s, 0, 0)),
                  pl.BlockSpec((1, Fs, D), lambda i, s: (s, 0, 0))],
        out_specs=[tok, hid, hid, hid, tok, row],
        out_shape=[jax.ShapeDtypeStruct((S, D), F32), jax.ShapeDtypeStruct((NS, S, Fs), BF16),
                   jax.ShapeDtypeStruct((NS, S, Fs), BF16), jax.ShapeDtypeStruct((NS, S, Fs), BF16),
                   jax.ShapeDtypeStruct((S, D), BF16), jax.ShapeDtypeStruct((1, D), F32)],
        scratch_shapes=[pltpu.VMEM((ts, D), BF16), pltpu.VMEM((ts, D), F32)],
        compiler_params=_cparams(2, 48),
    )(dh, h, g, gp, up, wg, wu, wd)


def _tn(x, y, nb, x_block, x_map, y_block, y_map, o_shape, o_block, o_map, name):
    x3, y3, o3 = len(x_block) == 3, len(y_block) == 3, len(o_block) == 3
    ts = x_block[1] if x3 else x_block[0]
    S = x.shape[1] if x3 else x.shape[0]

    def body(x_ref, y_ref, o_ref):
        @pl.when(pl.program_id(1) == 0)
        def _():
            o_ref[...] = jnp.zeros_like(o_ref)

        xv = x_ref[0] if x3 else x_ref[...]
        yv = y_ref[0] if y3 else y_ref[...]
        p = _dot_tn(xv, yv)
        if o3:
            o_ref[0] += p
        else:
            o_ref[...] += p

    return pl.pallas_call(
        body, name=name, grid=(nb, S // ts),
        in_specs=[pl.BlockSpec(x_block, x_map), pl.BlockSpec(y_block, y_map)],
        out_specs=pl.BlockSpec(o_block, o_map),
        out_shape=jax.ShapeDtypeStruct(o_shape, F32),
        compiler_params=_cparams(2, 48),
    )(x, y)


def _ffn_wgrads(n, dgp, dup, a, do, tag):
    S, D = n.shape
    Fs = dgp.shape[2]
    ts = _tile(S, 512)
    tok = ((ts, D), lambda s, k: (k, 0))
    hid = ((1, ts, Fs), lambda s, k: (s, k, 0))
    dwg = _tn(n, dgp, NS, *tok, *hid, (NS, D, Fs), (1, D, Fs), lambda s, k: (s, 0, 0), tag + "_dwg")
    dwu = _tn(n, dup, NS, *tok, *hid, (NS, D, Fs), (1, D, Fs), lambda s, k: (s, 0, 0), tag + "_dwu")
    dwd = _tn(a, do, NS, *hid, *tok, (NS, Fs, D), (1, Fs, D), lambda s, k: (s, 0, 0), tag + "_dwd")
    return dwg, dwu, dwd


def _mix_in_fwd(h, g, win):
    S, D = h.shape
    NG = win.shape[1] // D
    ts = _tile(S, 512)

    def body(h_ref, g_ref, w_ref, u_ref, z_ref, uscr):
        @pl.when(pl.program_id(1) == 0)
        def _():
            u = _rms_fwd(h_ref[...], g_ref[...]).astype(BF16)
            uscr[...] = u
            u_ref[...] = u

        z_ref[0] = _dot(uscr[...], w_ref[...]).astype(BF16)

    return pl.pallas_call(
        body, name="mix_in_fwd", grid=(S // ts, NG),
        in_specs=[pl.BlockSpec((ts, D), lambda i, k: (i, 0)), pl.BlockSpec((1, D), lambda i, k: (0, 0)),
                  pl.BlockSpec((D, D), lambda i, k: (0, k))],
        out_specs=[pl.BlockSpec((ts, D), lambda i, k: (i, 0)), pl.BlockSpec((1, ts, D), lambda i, k: (k, i, 0))],
        out_shape=[jax.ShapeDtypeStruct((S, D), BF16), jax.ShapeDtypeStruct((NG, S, D), BF16)],
        scratch_shapes=[pltpu.VMEM((ts, D), BF16)],
        compiler_params=_cparams(2, 40),
    )(h, g, win)


def _conv_fwd(z, wa, ba, wb):
    _, S, D = z.shape
    _, KA, CB = wa.shape
    KB = wb.shape[1]
    ts = _tile(S, 512)
    r = ts // HALO
    CH = min(128, ts)

    def body(z_ref, zh_ref, wa_ref, ba_ref, wb_ref, a1_ref, q_ref, sa, sb):
        keep = (pl.program_id(1) > 0).astype(F32)
        sa[HALO:HALO + ts, :] = z_ref[0].astype(F32) * _sigmoid(z_ref[1].astype(F32))
        sa[0:HALO, :] = zh_ref[0].astype(F32) * _sigmoid(zh_ref[1].astype(F32)) * keep
        sb[HALO:HALO + ts, :] = z_ref[3].astype(F32) * z_ref[4].astype(F32)
        sb[0:HALO, :] = zh_ref[3].astype(F32) * zh_ref[4].astype(F32) * keep
        wak = [wa_ref[0, k:k + 1, :] for k in range(KA)]
        wbk = [wb_ref[0, k:k + 1, :] for k in range(KB)]
        for c0 in range(0, ts, CH):
            acc = jnp.broadcast_to(ba_ref[...], (CH, CB))
            for k in range(KA):
                o = c0 + HALO - (KA - 1) + k
                acc = acc + wak[k] * sa[o:o + CH, :]
            a1_ref[c0:c0 + CH, :] = acc
            v = jnp.zeros((CH, CB), F32)
            for k in range(KB):
                o = c0 + HALO - (KB - 1) + k
                v = v + wbk[k] * sb[o:o + CH, :]
            q_ref[c0:c0 + CH, :] = (z_ref[2, c0:c0 + CH, :].astype(F32) * v).astype(BF16)

    return pl.pallas_call(
        body, name="conv_fwd", grid=(D // CB, S // ts),
        in_specs=[pl.BlockSpec((5, ts, CB), lambda j, i: (0, i, j)),
                  pl.BlockSpec((5, HALO, CB), lambda j, i: (0, jnp.maximum(i * r - 1, 0), j)),
                  pl.BlockSpec((1, KA, CB), lambda j, i: (j, 0, 0)), pl.BlockSpec((1, CB), lambda j, i: (0, j)),
                  pl.BlockSpec((1, KB, CB), lambda j, i: (j, 0, 0))],
        out_specs=[pl.BlockSpec((ts, CB), lambda j, i: (i, j)), pl.BlockSpec((ts, CB), lambda j, i: (i, j))],
        out_shape=[jax.ShapeDtypeStruct((S, D), F32), jax.ShapeDtypeStruct((S, D), BF16)],
        scratch_shapes=[pltpu.VMEM((HALO + ts, CB), F32), pltpu.VMEM((HALO + ts, CB), F32)],
        compiler_params=_cparams(2, 32),
    )(z, z, wa, ba, wb)


def _ln_stats(a1):
    mu = jnp.mean(a1, axis=-1, keepdims=True)
    xc = a1 - mu
    rstd = lax.rsqrt(jnp.mean(xc * xc, axis=-1, keepdims=True) + EPS)
    return xc * rstd, rstd


def _mix_out_fwd(h1, a1, q, z, lng, lnb, wa, wb, wo):
    S, D = h1.shape
    ts = _tile(S, 256)

    def body(h_ref, a1_ref, q_ref, ga_ref, gb_ref, lng_ref, lnb_ref, wa_ref, wb_ref, wo_ref,
             h2_ref, a3_ref, m_ref, ya_ref, yb_ref):
        xhat, _ = _ln_stats(a1_ref[...])
        a2 = xhat * lng_ref[...] + lnb_ref[...]
        a3 = (a2 * _sigmoid(a2)).astype(BF16)
        a3_ref[...] = a3
        ya = _dot(a3, wa_ref[...])
        yb = _dot(q_ref[...], wb_ref[...])
        ya_ref[...] = ya.astype(BF16)
        yb_ref[...] = yb.astype(BF16)
        m = (_sigmoid(ga_ref[0].astype(F32)) * ya + _sigmoid(gb_ref[0].astype(F32)) * yb).astype(BF16)
        m_ref[...] = m
        h2_ref[...] = h_ref[...] + _dot(m, wo_ref[...])

    tok = pl.BlockSpec((ts, D), lambda i: (i, 0))
    row = pl.BlockSpec((1, D), lambda i: (0, 0))
    mat = pl.BlockSpec((D, D), lambda i: (0, 0))
    return pl.pallas_call(
        body, name="mix_out_fwd", grid=(S // ts,),
        in_specs=[tok, tok, tok, pl.BlockSpec((1, ts, D), lambda i: (5, i, 0)), pl.BlockSpec((1, ts, D), lambda i: (6, i, 0)),
                  row, row, mat, mat, mat],
        out_specs=[tok, tok, tok, tok, tok],
        out_shape=[jax.ShapeDtypeStruct((S, D), F32)] + [jax.ShapeDtypeStruct((S, D), BF16)] * 4,
        compiler_params=_cparams(1, 56),
    )(h1, a1, q, z, z, lng, lnb, wa, wb, wo)


def _mix_out_bwd(dh2, a1, z, ya, yb, lng, lnb, wa, wb, wo):
    S, D = dh2.shape
    ts = _tile(S, 256)

    def body(dh_ref, a1_ref, ga_ref, gb_ref, ya_ref, yb_ref, lng_ref, lnb_ref, wa_ref, wb_ref, wo_ref,
             da1_ref, dq_ref, dga_ref, dgb_ref, dya_ref, dyb_ref, dhb_ref, dlg_ref, dlb_ref):
        @pl.when(pl.program_id(0) == 0)
        def _():
            dlg_ref[...] = jnp.zeros_like(dlg_ref)
            dlb_ref[...] = jnp.zeros_like(dlb_ref)

        dhb = dh_ref[...].astype(BF16)
        dhb_ref[...] = dhb
        dm = _dot_nt(dhb, wo_ref[...])
        sa = _sigmoid(ga_ref[0].astype(F32))
        sb = _sigmoid(gb_ref[0].astype(F32))
        dga_ref[...] = (dm * ya_ref[...].astype(F32) * sa * (1.0 - sa)).astype(BF16)
        dgb_ref[...] = (dm * yb_ref[...].astype(F32) * sb * (1.0 - sb)).astype(BF16)
        dya = (sa * dm).astype(BF16)
        dyb = (sb * dm).astype(BF16)
        dya_ref[...] = dya
        dyb_ref[...] = dyb
        dq_ref[...] = _dot_nt(dyb, wb_ref[...])
        da3 = _dot_nt(dya, wa_ref[...])
        xhat, rstd = _ln_stats(a1_ref[...])
        a2 = xhat * lng_ref[...] + lnb_ref[...]
        sg = _sigmoid(a2)
        da2 = da3 * (sg * (1.0 + a2 * (1.0 - sg)))
        dlg_ref[...] += jnp.sum(da2 * xhat, axis=0, keepdims=True)
        dlb_ref[...] += jnp.sum(da2, axis=0, keepdims=True)
        dxh = da2 * lng_ref[...]
        da1_ref[...] = rstd * (dxh - jnp.mean(dxh, axis=-1, keepdims=True)
                               - xhat * jnp.mean(dxh * xhat, axis=-1, keepdims=True))

    tok = pl.BlockSpec((ts, D), lambda i: (i, 0))
    row = pl.BlockSpec((1, D), lambda i: (0, 0))
    mat = pl.BlockSpec((D, D), lambda i: (0, 0))
    return pl.pallas_call(
        body, name="mix_out_bwd", grid=(S // ts,),
        in_specs=[tok, tok, pl.BlockSpec((1, ts, D), lambda i: (5, i, 0)), pl.BlockSpec((1, ts, D), lambda i: (6, i, 0)),
                  tok, tok, row, row, mat, mat, mat],
        out_specs=[tok, tok, tok, tok, tok, tok, tok, row, row],
        out_shape=[jax.ShapeDtypeStruct((S, D), F32), jax.ShapeDtypeStruct((S, D), F32)]
        + [jax.ShapeDtypeStruct((S, D), BF16)] * 5 + [jax.ShapeDtypeStruct((1, D), F32)] * 2,
        compiler_params=_cparams(1, 56),
    )(dh2, a1, z, z, ya, yb, lng, lnb, wa, wb, wo)


def _conv_bwd(z, da1, dq, dga, dgb, wa, wb):
    NG, S, D = z.shape
    _, KA, CB = wa.shape
    KB = wb.shape[1]
    ts = _tile(S, 512)
    r = ts // HALO
    nt = S // ts
    CH = min(128, ts)
    last_halo = S // HALO - 1

    def body(z_ref, zp_ref, zn_ref, da1_ref, da1n_ref, dq_ref, dqn_ref, dga_ref, dgb_ref, wa_ref, wb_ref,
             dz_ref, dwa_ref, dba_ref, dwb_ref, sa0, sd, sp, sv, acca, accb):
        i = pl.program_id(1)
        prev = (i > 0).astype(F32)
        nxt = (i < nt - 1).astype(F32)

        @pl.when(i == 0)
        def _():
            acca[...] = jnp.zeros_like(acca)
            accb[...] = jnp.zeros_like(accb)
            dba_ref[...] = jnp.zeros_like(dba_ref)

        sa0[HALO:HALO + ts, :] = z_ref[0].astype(F32) * _sigmoid(z_ref[1].astype(F32))
        sa0[0:HALO, :] = zp_ref[0].astype(F32) * _sigmoid(zp_ref[1].astype(F32)) * prev
        sp[HALO:HALO + ts, :] = z_ref[3].astype(F32) * z_ref[4].astype(F32)
        sp[0:HALO, :] = zp_ref[3].astype(F32) * zp_ref[4].astype(F32) * prev
        sd[0:ts, :] = da1_ref[...]
        sd[ts:ts + HALO, :] = da1n_ref[...] * nxt
        sv[0:ts, :] = dq_ref[...] * z_ref[2].astype(F32)
        sv[ts:ts + HALO, :] = dqn_ref[...] * zn_ref[2].astype(F32) * nxt
        dba_ref[...] += jnp.sum(da1_ref[...], axis=0, keepdims=True)
        wak = [wa_ref[0, k:k + 1, :] for k in range(KA)]
        wbk = [wb_ref[0, k:k + 1, :] for k in range(KB)]
        for c0 in range(0, ts, CH):
            rows = slice(c0, c0 + CH)
            d1 = sd[rows, :]
            da0 = jnp.zeros((CH, CB), F32)
            for k in range(KA):
                o = c0 + (KA - 1) - k
                da0 = da0 + wak[k] * sd[o:o + CH, :]
                o = c0 + HALO - (KA - 1) + k
                acca[k] += jnp.sum((d1 * sa0[o:o + CH, :]).reshape(CH // 8, 8, CB), axis=0)
            val = z_ref[0, rows, :].astype(F32)
            sg = _sigmoid(z_ref[1, rows, :].astype(F32))
            dz_ref[0, rows, :] = (da0 * sg).astype(BF16)
            dz_ref[1, rows, :] = (da0 * val * sg * (1.0 - sg)).astype(BF16)
            dv = sv[rows, :]
            v = jnp.zeros((CH, CB), F32)
            dp = jnp.zeros((CH, CB), F32)
            for k in range(KB):
                o = c0 + HALO - (KB - 1) + k
                pw = sp[o:o + CH, :]
                v = v + wbk[k] * pw
                accb[k] += jnp.sum((dv * pw).reshape(CH // 8, 8, CB), axis=0)
                o = c0 + (KB - 1) - k
                dp = dp + wbk[k] * sv[o:o + CH, :]
            dz_ref[2, rows, :] = (dq_ref[rows, :] * v).astype(BF16)
            dz_ref[3, rows, :] = (dp * z_ref[4, rows, :].astype(F32)).astype(BF16)
            dz_ref[4, rows, :] = (dp * z_ref[3, rows, :].astype(F32)).astype(BF16)
        dz_ref[5] = dga_ref[...]
        dz_ref[6] = dgb_ref[...]

        @pl.when(i == nt - 1)
        def _():
            dwa_ref[0] = jnp.sum(acca[...], axis=1)
            dwb_ref[0] = jnp.sum(accb[...], axis=1)

    zt = pl.BlockSpec((5, ts, CB), lambda j, i: (0, i, j))
    zp = pl.BlockSpec((5, HALO, CB), lambda j, i: (0, jnp.maximum(i * r - 1, 0), j))
    zn = pl.BlockSpec((5, HALO, CB), lambda j, i: (0, jnp.minimum((i + 1) * r, last_halo), j))
    tok = pl.BlockSpec((ts, CB), lambda j, i: (i, j))
    tokn = pl.BlockSpec((HALO, CB), lambda j, i: (jnp.minimum((i + 1) * r, last_halo), j))
    return pl.pallas_call(
        body, name="conv_bwd", grid=(D // CB, nt),
        in_specs=[zt, zp, zn, tok, tokn, tok, tokn, tok, tok,
                  pl.BlockSpec((1, KA, CB), lambda j, i: (j, 0, 0)), pl.BlockSpec((1, KB, CB), lambda j, i: (j, 0, 0))],
        out_specs=[pl.BlockSpec((NG, ts, CB), lambda j, i: (0, i, j)), pl.BlockSpec((1, KA, CB), lambda j, i: (j, 0, 0)),
                   pl.BlockSpec((1, CB), lambda j, i: (0, j)), pl.BlockSpec((1, KB, CB), lambda j, i: (j, 0, 0))],
        out_shape=[jax.ShapeDtypeStruct((NG, S, D), BF16), jax.ShapeDtypeStruct((D // CB, KA, CB), F32),
                   jax.ShapeDtypeStruct((1, D), F32), jax.ShapeDtypeStruct((D // CB, KB, CB), F32)],
        scratch_shapes=[pltpu.VMEM((HALO + ts, CB), F32), pltpu.VMEM((ts + HALO, CB), F32),
                        pltpu.VMEM((HALO + ts, CB), F32), pltpu.VMEM((ts + HALO, CB), F32),
                        pltpu.VMEM((KA, 8, CB), F32), pltpu.VMEM((KB, 8, CB), F32)],
        compiler_params=_cparams(2, 40),
    )(z, z, z, da1, da1, dq, dq, dga, dgb, wa, wb)


def _mix_in_bwd(dh2, h1, g, dz, win):
    S, D = h1.shape
    NG = dz.shape[0]
    ts = _tile(S, 512)

    def body(dh_ref, h_ref, g_ref, dz_ref, w_ref, dhi_ref, dg_ref, acc):
        i = pl.program_id(0)
        k = pl.program_id(1)

        @pl.when(k == 0)
        def _():
            acc[...] = jnp.zeros_like(acc)

        @pl.when(jnp.logical_and(i == 0, k == 0))
        def _():
            dg_ref[...] = jnp.zeros_like(dg_ref)

        acc[...] += _dot_nt(dz_ref[0], w_ref[...])

        @pl.when(k == NG - 1)
        def _():
            dx, dg = _rms_bwd(h_ref[...], g_ref[...], acc[...])
            dhi_ref[...] = dh_ref[...] + dx
            dg_ref[...] += dg

    tok = pl.BlockSpec((ts, D), lambda i, k: (i, 0))
    row = pl.BlockSpec((1, D), lambda i, k: (0, 0))
    return pl.pallas_call(
        body, name="mix_in_bwd", grid=(S // ts, NG),
        in_specs=[tok, tok, row, pl.BlockSpec((1, ts, D), lambda i, k: (k, i, 0)), pl.BlockSpec((D, D), lambda i, k: (0, k))],
        out_specs=[tok, row],
        out_shape=[jax.ShapeDtypeStruct((S, D), F32), jax.ShapeDtypeStruct((1, D), F32)],
        scratch_shapes=[pltpu.VMEM((ts, D), F32)],
        compiler_params=_cparams(2, 40),
    )(dh2, h1, g, dz, win)


def _loss_head(h3, t, g):
    S, D = h3.shape
    ts = _tile(S, 512)

    def body(h_ref, t_ref, g_ref, dh_ref, dg_ref, loss_ref):
        @pl.when(pl.program_id(0) == 0)
        def _():
            dg_ref[...] = jnp.zeros_like(dg_ref)
            loss_ref[...] = jnp.zeros_like(loss_ref)

        x = h_ref[...]
        err = _rms_fwd(x, g_ref[...]) - t_ref[...]
        loss_ref[...] += (0.5 / D) * jnp.sum(err * err)
        dx, dg = _rms_bwd(x, g_ref[...], err * (1.0 / D))
        dh_ref[...] = dx
        dg_ref[...] += dg

    tok = pl.BlockSpec((ts, D), lambda i: (i, 0))
    row = pl.BlockSpec((1, D), lambda i: (0, 0))
    return pl.pallas_call(
        body, name="loss_head", grid=(S // ts,),
        in_specs=[tok, tok, row],
        out_specs=[tok, row, pl.BlockSpec((8, 128), lambda i: (0, 0))],
        out_shape=[jax.ShapeDtypeStruct((S, D), F32), jax.ShapeDtypeStruct((1, D), F32), jax.ShapeDtypeStruct((8, 128), F32)],
        compiler_params=_cparams(1, 40),
    )(h3, t, g)


def _place():
    x, y, c = lax.axis_index("x"), lax.axis_index("y"), lax.axis_index("c")
    chips = [(1 - x, y), (x, 1 - y), (1 - x, 1 - y)]
    return x, y, c, chips


def _gather_weights(shards):
    n = len(shards)

    def body(*refs):
        ins, outs = refs[:n], refs[n:2 * n]
        send_sems, recv_sems, local_sems = refs[2 * n:]
        x, y, c, chips = _place()
        p = 2 * x + y
        sibling = (x, y, 1 - c)

        def rows(a, pc):
            half = ins[a].shape[0] // 2
            return pl.ds(pc * half, half)

        def copy(a, k, q, pc, to, src=None):
            dst = outs[a].at[q, rows(a, pc)]
            return pltpu.make_async_remote_copy(src_ref=dst if src is None else src, dst_ref=dst,
                                                send_sem=send_sems.at[a, k], recv_sem=recv_sems.at[a, k],
                                                device_id=to, device_id_type=MESH)

        mine = [pltpu.make_async_copy(ins[a], outs[a].at[p], local_sems.at[a]) for a in range(n)]
        first, passed = [], []
        for a in range(n):
            mine[a].start()
            for j, chip in enumerate(chips):
                cp = copy(a, j, p, c, (*chip, c), src=ins[a].at[rows(a, c)])
                cp.start()
                first.append(cp)
        for a in range(n):
            for j, (qx, qy) in enumerate(chips):
                q = 2 * qx + qy
                copy(a, j, q, c, sibling).wait_recv()
                cp = copy(a, 3 + j, q, c, sibling)
                cp.start()
                passed.append(cp)
        for a in range(n):
            for j, (qx, qy) in enumerate(chips):
                copy(a, 3 + j, 2 * qx + qy, 1 - c, sibling).wait_recv()
        for cp in first + passed:
            cp.wait_send()
        for cp in mine:
            cp.wait()

    return pl.pallas_call(
        body, name="gather_weights",
        in_specs=[ANY] * n, out_specs=[ANY] * n,
        out_shape=[jax.ShapeDtypeStruct((NS,) + s.shape, s.dtype) for s in shards],
        scratch_shapes=[pltpu.SemaphoreType.DMA((n, 6)), pltpu.SemaphoreType.DMA((n, 6)), pltpu.SemaphoreType.DMA((n,))],
    )(*shards)


def _send_halves_to_sibling(grads):
    n = len(grads)

    def body(*refs):
        ins, outs = refs[:n], refs[n:2 * n]
        send_sems, recv_sems = refs[2 * n:]
        x, y, c, _ = _place()
        cps = []
        for a in range(n):
            half = ins[a].shape[1] // 2
            cp = pltpu.make_async_remote_copy(src_ref=ins[a].at[:, pl.ds((1 - c) * half, half)], dst_ref=outs[a],
                                              send_sem=send_sems.at[a], recv_sem=recv_sems.at[a],
                                              device_id=(x, y, 1 - c), device_id_type=MESH)
            cp.start()
            cps.append(cp)
        for cp in cps:
            cp.wait()

    return pl.pallas_call(
        body, name="grads_to_sibling",
        in_specs=[ANY] * n, out_specs=[ANY] * n,
        out_shape=[jax.ShapeDtypeStruct((NS, g.shape[1] // 2) + g.shape[2:], g.dtype) for g in grads],
        scratch_shapes=[pltpu.SemaphoreType.DMA((n,)), pltpu.SemaphoreType.DMA((n,))],
    )(*grads)


def _scatter_to_chips(sums):
    n = len(sums)

    def body(*refs):
        ins, outs = refs[:n], refs[n:2 * n]
        send_sems, recv_sems = refs[2 * n:]
        x, y, c, chips = _place()
        cps = []
        for a in range(n):
            for j, (qx, qy) in enumerate(chips):
                cp = pltpu.make_async_remote_copy(src_ref=ins[a].at[2 * qx + qy], dst_ref=outs[a].at[j],
                                                  send_sem=send_sems.at[a, j], recv_sem=recv_sems.at[a, j],
                                                  device_id=(qx, qy, c), device_id_type=MESH)
                cp.start()
                cps.append(cp)
        for cp in cps:
            cp.wait()

    return pl.pallas_call(
        body, name="grads_to_chips",
        in_specs=[ANY] * n, out_specs=[ANY] * n,
        out_shape=[jax.ShapeDtypeStruct((3,) + s.shape[1:], s.dtype) for s in sums],
        scratch_shapes=[pltpu.SemaphoreType.DMA((n, 3)), pltpu.SemaphoreType.DMA((n, 3))],
    )(*sums)


def _swap_halves(grads):
    n = len(grads)

    def body(*refs):
        outs = refs[n:2 * n]
        send_sems, recv_sems = refs[2 * n:]
        x, y, c, _ = _place()
        cps = []
        for a in range(n):
            half = outs[a].shape[0] // 2
            mine = outs[a].at[pl.ds(c * half, half)]
            cp = pltpu.make_async_remote_copy(src_ref=mine, dst_ref=mine, send_sem=send_sems.at[a], recv_sem=recv_sems.at[a],
                                              device_id=(x, y, 1 - c), device_id_type=MESH)
            cp.start()
            cps.append(cp)
        for a, cp in enumerate(cps):
            cp.wait_send()
            half = outs[a].shape[0] // 2
            theirs = outs[a].at[pl.ds((1 - c) * half, half)]
            pltpu.make_async_remote_copy(src_ref=theirs, dst_ref=theirs, send_sem=send_sems.at[a], recv_sem=recv_sems.at[a],
                                         device_id=(x, y, 1 - c), device_id_type=MESH).wait_recv()

    return pl.pallas_call(
        body, name="grads_swap_halves",
        in_specs=[ANY] * n, out_specs=[ANY] * n,
        out_shape=[jax.ShapeDtypeStruct(g.shape, g.dtype) for g in grads],
        input_output_aliases={a: a for a in range(n)},
        scratch_shapes=[pltpu.SemaphoreType.DMA((n,)), pltpu.SemaphoreType.DMA((n,))],
    )(*grads)


def _allreduce_small(v):
    R, C = v.shape
    N = 8

    def body(v_ref, out_ref, gath, send_sems, recv_sems, local_sem):
        x, y, c, chips = _place()
        me, sibling = (x, y, c), (x, y, 1 - c)

        def rows(px, py, pc):
            return gath.at[pl.ds((4 * px + 2 * py + pc) * R, R), :]

        def copy(k, block, to, src=None):
            return pltpu.make_async_remote_copy(src_ref=rows(*block) if src is None else src, dst_ref=rows(*block),
                                                send_sem=send_sems.at[k], recv_sem=recv_sems.at[k],
                                                device_id=to, device_id_type=MESH)

        mine = pltpu.make_async_copy(v_ref, rows(*me), local_sem)
        mine.start()
        first = [copy(0, me, sibling, src=v_ref)]
        first += [copy(1 + j, me, (*chip, c), src=v_ref) for j, chip in enumerate(chips)]
        for cp in first:
            cp.start()
        passed = [copy(4 + j, (*chip, c), sibling) for j, chip in enumerate(chips)]
        for j, chip in enumerate(chips):
            copy(1 + j, (*chip, c), me).wait_recv()
            passed[j].start()
        copy(0, sibling, me).wait_recv()
        for j, chip in enumerate(chips):
            copy(4 + j, (*chip, 1 - c), me).wait_recv()
        for cp in first + passed:
            cp.wait_send()
        mine.wait()
        acc = gath[0:R, :]
        for d in range(1, N):
            acc = acc + gath[d * R:(d + 1) * R, :]
        out_ref[...] = acc

    return pl.pallas_call(
        body, name="allreduce_small",
        in_specs=[pl.BlockSpec(memory_space=pltpu.VMEM)], out_specs=pl.BlockSpec(memory_space=pltpu.VMEM),
        out_shape=jax.ShapeDtypeStruct((R, C), F32),
        scratch_shapes=[pltpu.VMEM((N * R, C), F32), pltpu.SemaphoreType.DMA((7,)), pltpu.SemaphoreType.DMA((7,)),
                        pltpu.SemaphoreType.DMA],
    )(v)


def _row_tile(half):
    for t in (256, 176, 128, 64, 32, 16, 8):
        if half % t == 0:
            return t
    return half


def _add_sibling(place, g, got, name):
    _, R, C = g.shape
    half = R // 2
    tr = _row_tile(half)
    nr = half // tr

    def body(pc_ref, g_ref, got_ref, b_ref, f_ref):
        s = g_ref[0] + got_ref[0]
        b_ref[0] = s.astype(BF16)

        @pl.when(pl.program_id(1) == pc_ref[0])
        def _():
            f_ref[...] = s

    return pl.pallas_call(
        body, name=name,
        grid_spec=pltpu.PrefetchScalarGridSpec(
            num_scalar_prefetch=1, grid=(nr, NS),
            in_specs=[pl.BlockSpec((1, tr, C), lambda r, q, pc: (q, pc[1] * nr + r, 0)),
                      pl.BlockSpec((1, tr, C), lambda r, q, pc: (q, r, 0))],
            out_specs=[pl.BlockSpec((1, tr, C), lambda r, q, pc: (q, r, 0)), pl.BlockSpec((tr, C), lambda r, q, pc: (r, 0))]),
        out_shape=[jax.ShapeDtypeStruct((NS, half, C), BF16), jax.ShapeDtypeStruct((half, C), F32)],
        compiler_params=_cparams(2, 32),
    )(place, g, got)


def _add_chips(place, own, got, name):
    half, C = own.shape
    tr = _row_tile(half)
    nr = half // tr

    def body(pc_ref, own_ref, got_ref, o_ref):
        o_ref[...] = ((own_ref[...] + got_ref[0].astype(F32)) + got_ref[1].astype(F32)) + got_ref[2].astype(F32)

    return pl.pallas_call(
        body, name=name,
        grid_spec=pltpu.PrefetchScalarGridSpec(
            num_scalar_prefetch=1, grid=(nr,),
            in_specs=[pl.BlockSpec((tr, C), lambda r, pc: (r, 0)), pl.BlockSpec((3, tr, C), lambda r, pc: (0, r, 0))],
            out_specs=pl.BlockSpec((tr, C), lambda r, pc: (pc[1] * nr + r, 0))),
        out_shape=jax.ShapeDtypeStruct((2 * half, C), F32),
        compiler_params=_cparams(1, 32),
    )(place, own, got)


def _adamw(w, g, m, v, name):
    R, C = w.shape
    tr = _row_tile(R)
    c1 = 1.0 - ADAM_B1 ** ADAM_STEP
    c2 = 1.0 - ADAM_B2 ** ADAM_STEP

    def body(w_ref, g_ref, m_ref, v_ref, d_ref, mo_ref, vo_ref):
        gv = g_ref[...]
        mn = ADAM_B1 * m_ref[...] + (1.0 - ADAM_B1) * gv
        vn = ADAM_B2 * v_ref[...] + (1.0 - ADAM_B2) * (gv * gv)
        mo_ref[...] = mn
        vo_ref[...] = vn
        d_ref[...] = -ADAM_LR * ((mn / c1) / (jnp.sqrt(vn / c2) + ADAM_EPS) + ADAM_WD * w_ref[...])

    blk = pl.BlockSpec((tr, C), lambda r: (r, 0))
    return pl.pallas_call(
        body, name=name, grid=(R // tr,),
        in_specs=[blk] * 4, out_specs=[blk] * 3,
        out_shape=[jax.ShapeDtypeStruct((R, C), F32)] * 3,
        compiler_params=_cparams(1, 48),
    )(w, g, m, v)


def kernel(x, ffn1_norm, ffn1_w_gate, ffn1_w_up, ffn1_w_down, mix_norm, w_in, a_dw_w, a_dw_b, a_ln_g, a_ln_b, a_w_out, b_conv_w, b_w_out, w_o, ffn2_norm, ffn2_w_gate, ffn2_w_up, ffn2_w_down, final_norm, loss_target, m_ffn1_norm, m_ffn1_w_gate, m_ffn1_w_up, m_ffn1_w_down, m_mix_norm, m_w_in, m_a_dw_w, m_a_dw_b, m_a_ln_g, m_a_ln_b, m_a_w_out, m_b_conv_w, m_b_w_out, m_w_o, m_ffn2_norm, m_ffn2_w_gate, m_ffn2_w_up, m_ffn2_w_down, m_final_norm, v_ffn1_norm, v_ffn1_w_gate, v_ffn1_w_up, v_ffn1_w_down, v_mix_norm, v_w_in, v_a_dw_w, v_a_dw_b, v_a_ln_g, v_a_ln_b, v_a_w_out, v_b_conv_w, v_b_w_out, v_w_o, v_ffn2_norm, v_ffn2_w_gate, v_ffn2_w_up, v_ffn2_w_down, v_final_norm):
    names = ["ffn1_norm", "ffn1_w_gate", "ffn1_w_up", "ffn1_w_down", "mix_norm", "w_in", "a_dw_w", "a_dw_b", "a_ln_g",
             "a_ln_b", "a_w_out", "b_conv_w", "b_w_out", "w_o", "ffn2_norm", "ffn2_w_gate", "ffn2_w_up", "ffn2_w_down",
             "final_norm"]
    W = dict(zip(names, [ffn1_norm, ffn1_w_gate, ffn1_w_up, ffn1_w_down, mix_norm, w_in, a_dw_w, a_dw_b, a_ln_g, a_ln_b,
                         a_w_out, b_conv_w, b_w_out, w_o, ffn2_norm, ffn2_w_gate, ffn2_w_up, ffn2_w_down, final_norm]))
    M = dict(zip(names, [m_ffn1_norm, m_ffn1_w_gate, m_ffn1_w_up, m_ffn1_w_down, m_mix_norm, m_w_in, m_a_dw_w, m_a_dw_b,
                         m_a_ln_g, m_a_ln_b, m_a_w_out, m_b_conv_w, m_b_w_out, m_w_o, m_ffn2_norm, m_ffn2_w_gate,
                         m_ffn2_w_up, m_ffn2_w_down, m_final_norm]))
    V = dict(zip(names, [v_ffn1_norm, v_ffn1_w_gate, v_ffn1_w_up, v_ffn1_w_down, v_mix_norm, v_w_in, v_a_dw_w, v_a_dw_b,
                         v_a_ln_g, v_a_ln_b, v_a_w_out, v_b_conv_w, v_b_w_out, v_w_o, v_ffn2_norm, v_ffn2_w_gate,
                         v_ffn2_w_up, v_ffn2_w_down, v_final_norm]))
    big = ["ffn1_w_gate", "ffn1_w_up", "ffn1_w_down", "w_in", "a_w_out", "b_w_out", "w_o",
           "ffn2_w_gate", "ffn2_w_up", "ffn2_w_down"]
    vecs = ["ffn1_norm", "mix_norm", "a_dw_b", "a_ln_g", "a_ln_b", "ffn2_norm", "final_norm"]

    S, D = x.shape[1], x.shape[2]
    CB = D // NS
    KA, KB = a_dw_w.shape[1], b_conv_w.shape[1]
    px, py, pc = lax.axis_index("x"), lax.axis_index("y"), lax.axis_index("c")
    chip = 2 * px + py
    place = jnp.stack([chip, pc]).astype(jnp.int32)
    h0 = x.reshape(S, D)
    tgt = loss_target.reshape(S, D)
    row = lambda n: W[n].reshape(1, D)

    pad = lambda a, r: jnp.concatenate([a, jnp.zeros((r - a.shape[0], a.shape[1]), F32)], axis=0)
    full = _gather_weights([W[n][0].astype(BF16) for n in big] + [pad(a_dw_w[0], 32), pad(b_conv_w[0], 16)])
    Wf = dict(zip(big, full[:len(big)]))
    wa_taps, wb_taps = full[len(big)][:, :KA], full[len(big) + 1][:, :KB]
    win = jnp.transpose(Wf["w_in"], (1, 0, 2)).reshape(D, -1)
    wa_out = Wf["a_w_out"].reshape(D, D)
    wb_out = Wf["b_w_out"].reshape(D, D)
    wo = Wf["w_o"].reshape(D, D)

    h1, n1, gp1, up1 = _ffn_fwd(h0, row("ffn1_norm"), Wf["ffn1_w_gate"], Wf["ffn1_w_up"], Wf["ffn1_w_down"], "ffn1_fwd")
    u, z = _mix_in_fwd(h1, row("mix_norm"), win)
    a1, q = _conv_fwd(z, wa_taps, row("a_dw_b"), wb_taps)
    h2, a3, mm, ya, yb = _mix_out_fwd(h1, a1, q, z, row("a_ln_g"), row("a_ln_b"), wa_out, wb_out, wo)
    h3, n2, gp2, up2 = _ffn_fwd(h2, row("ffn2_norm"), Wf["ffn2_w_gate"], Wf["ffn2_w_up"], Wf["ffn2_w_down"], "ffn2_fwd")
    dh3, d_final, loss_part = _loss_head(h3, tgt, row("final_norm"))
    loss = lax.psum(loss_part[0, 0], ("x", "y", "c"))

    G = {}
    dh2, dgp2, dup2, act2, do2, d_ffn2 = _ffn_bwd(dh3, h2, row("ffn2_norm"), gp2, up2,
                                                  Wf["ffn2_w_gate"], Wf["ffn2_w_up"], Wf["ffn2_w_down"], "ffn2_bwd")
    G["ffn2_w_gate"], G["ffn2_w_up"], G["ffn2_w_down"] = _ffn_wgrads(n2, dgp2, dup2, act2, do2, "ffn2")
    da1, dq, dga, dgb, dya, dyb, dh2b, d_lng, d_lnb = _mix_out_bwd(dh2, a1, z, ya, yb, row("a_ln_g"), row("a_ln_b"),
                                                                   wa_out, wb_out, wo)
    ts = _tile(S, 512)
    sq = ((ts, D), lambda s, k: (k, 0))
    G["a_w_out"] = _tn(a3, dya, 1, *sq, *sq, (D, D), (D, D), lambda s, k: (0, 0), "a_w_out_grad").reshape(NS, CB, D)
    G["b_w_out"] = _tn(q, dyb, 1, *sq, *sq, (D, D), (D, D), lambda s, k: (0, 0), "b_w_out_grad").reshape(NS, CB, D)
    G["w_o"] = _tn(mm, dh2b, 1, *sq, *sq, (D, D), (D, D), lambda s, k: (0, 0), "w_o_grad").reshape(NS, CB, D)
    dz, d_wa, d_ba, d_wb = _conv_bwd(z, da1, dq, dga, dgb, wa_taps, wb_taps)
    dh1, d_mix = _mix_in_bwd(dh2, h1, row("mix_norm"), dz, win)
    NG = dz.shape[0]
    d_win = _tn(u, dz, NG, *sq, (1, ts, D), lambda s, k: (s, k, 0), (D, NG * D), (D, D), lambda s, k: (0, s), "w_in_grad")
    G["w_in"] = jnp.transpose(d_win.reshape(D, NS, -1), (1, 0, 2))
    dx, dgp1, dup1, act1, do1, d_ffn1 = _ffn_bwd(dh1, h0, row("ffn1_norm"), gp1, up1,
                                                 Wf["ffn1_w_gate"], Wf["ffn1_w_up"], Wf["ffn1_w_down"], "ffn1_bwd")
    G["ffn1_w_gate"], G["ffn1_w_up"], G["ffn1_w_down"] = _ffn_wgrads(n1, dgp1, dup1, act1, do1, "ffn1")

    glist = [G[n] for n in big]
    from_sibling = _send_halves_to_sibling(glist)
    sums = [_add_sibling(place, g, got, n + "_chip_sum") for n, g, got in zip(big, glist, from_sibling)]
    landed = _scatter_to_chips([s[0] for s in sums])
    halves = [_add_chips(place, s[1], got, n + "_total") for n, s, got in zip(big, sums, landed)]
    totals = dict(zip(big, _swap_halves(halves)))

    taps_a = jnp.transpose(d_wa, (1, 0, 2)).reshape(KA, D)
    taps_b = jnp.transpose(d_wb, (1, 0, 2)).reshape(KB, D)
    rows_a = -(-KA // 8) * 8
    rows_b = -(-KB // 8) * 8
    vec_grads = {"ffn1_norm": d_ffn1, "mix_norm": d_mix, "a_dw_b": d_ba, "a_ln_g": d_lng, "a_ln_b": d_lnb,
                 "ffn2_norm": d_ffn2, "final_norm": d_final}
    packed = jnp.concatenate([pad(jnp.concatenate([vec_grads[n] for n in vecs], axis=0), 8),
                              pad(taps_a, rows_a), pad(taps_b, rows_b)], axis=0)
    small = _allreduce_small(packed)
    g_vecs = small[0:8]
    g_taps = lax.dynamic_slice_in_dim(small[8:], chip * CB, CB, axis=1)

    def pack_vecs(P):
        return pad(jnp.concatenate([P[n].reshape(1, D) for n in vecs], axis=0), 8)

    def pack_taps(P):
        return jnp.concatenate([pad(P["a_dw_w"][0], rows_a), pad(P["b_conv_w"][0], rows_b)], axis=0)

    dv_, mv_, vv_ = _adamw(pack_vecs(W), g_vecs, pack_vecs(M), pack_vecs(V), "adamw_vectors")
    dt_, mt_, vt_ = _adamw(pack_taps(W), g_taps, pack_taps(M), pack_taps(V), "adamw_taps")

    grads, deltas, new_m, new_v = {}, {}, {}, {}
    for n in big:
        shp = W[n].shape
        g2 = totals[n]
        d_, m_, v_ = _adamw(W[n][0], g2, M[n][0], V[n][0], n + "_adamw")
        grads[n], deltas[n], new_m[n], new_v[n] = g2.reshape(shp), d_.reshape(shp), m_.reshape(shp), v_.reshape(shp)
    for i, n in enumerate(vecs):
        shp = W[n].shape
        grads[n] = g_vecs[i].reshape(shp)
        deltas[n], new_m[n], new_v[n] = dv_[i].reshape(shp), mv_[i].reshape(shp), vv_[i].reshape(shp)
    for n, lo, k in (("a_dw_w", 0, KA), ("b_conv_w", rows_a, KB)):
        shp = W[n].shape
        grads[n] = g_taps[lo:lo + k].reshape(shp)
        deltas[n], new_m[n], new_v[n] = (dt_[lo:lo + k].reshape(shp), mt_[lo:lo + k].reshape(shp),
                                         vt_[lo:lo + k].reshape(shp))

    return (loss, dx.reshape(x.shape), *[grads[n] for n in names], *[deltas[n] for n in names],
            *[new_m[n] for n in names], *[new_v[n] for n in names])
```

```python
import functools

import jax
import jax.numpy as jnp
from jax import lax
from jax.experimental import pallas as pl
from jax.experimental.pallas import tpu as pltpu

F32 = jnp.float32
BF16 = jnp.bfloat16
EPS = 1e-6
NS = 4
HALO = 32
MESH = pl.DeviceIdType.MESH
ANY = pl.BlockSpec(memory_space=pl.ANY)

ADAM_LR = 0.001
ADAM_B1 = 0.9
ADAM_B2 = 0.999
ADAM_EPS = 1e-08
ADAM_WD = 0.01
ADAM_STEP = 10


def _cparams(n_axes, vmem_mb):
    return pltpu.CompilerParams(dimension_semantics=("arbitrary",) * n_axes, vmem_limit_bytes=vmem_mb << 20)


def _tile(n, t):
    return t if n % t == 0 else n


def _row_tile(n, cap=256):
    for t in (256, 176, 128, 64, 32, 16, 8):
        if t <= cap and n % t == 0:
            return t
    return n


def _dot(a, b):
    return jnp.dot(a, b, preferred_element_type=F32)


def _dot_nt(a, b):
    return lax.dot_general(a, b, (((1,), (1,)), ((), ())), preferred_element_type=F32)


def _dot_tn(a, b):
    return lax.dot_general(a, b, (((0,), (0,)), ((), ())), preferred_element_type=F32)


def _sigmoid(x):
    return jax.nn.sigmoid(x)


def _rms_fwd(x, g):
    r = lax.rsqrt(jnp.mean(x * x, axis=-1, keepdims=True) + EPS)
    return x * r * g


def _rms_bwd(x, g, dn):
    r = lax.rsqrt(jnp.mean(x * x, axis=-1, keepdims=True) + EPS)
    xr = x * r
    dg = jnp.sum(dn * xr, axis=0, keepdims=True)
    w = dn * g
    dx = r * w - xr * (r * r) * jnp.mean(x * w, axis=-1, keepdims=True)
    return dx, dg


def _place():
    x, y, c = lax.axis_index("x"), lax.axis_index("y"), lax.axis_index("c")
    chips = [(1 - x, y), (x, 1 - y), (1 - x, 1 - y)]
    return x, y, c, chips


def _quarter_shape(full_shape, kind):
    r, c = full_shape
    return (r // NS, c) if kind == "rows" else (r, c // NS)


def _half_of_quarter(ref, kind, q, pc):
    qr, qc = _quarter_shape(ref.shape, kind)
    h = qr // 2
    if kind == "rows":
        return ref.at[pl.ds(q * qr + pc * h, h), :]
    return ref.at[pl.ds(pc * h, h), pl.ds(q * qc, qc)]


def _quarter(ref, kind, q):
    qr, qc = _quarter_shape(ref.shape, kind)
    if kind == "rows":
        return ref.at[pl.ds(q * qr, qr), :]
    return ref.at[:, pl.ds(q * qc, qc)]


def _rows_half(ref, pc):
    h = ref.shape[0] // 2
    return ref.at[pl.ds(pc * h, h)]


class _Gather:
    def __init__(self, quarters, kinds):
        self.ins = list(quarters)
        self.kinds = list(kinds)
        n = len(self.ins)
        self.out_shape = [jax.ShapeDtypeStruct((NS * a.shape[0], a.shape[1]) if k == "rows" else (a.shape[0], NS * a.shape[1]),
                                               a.dtype) for a, k in zip(self.ins, self.kinds)]
        self.scratch = [pltpu.SemaphoreType.DMA((n, 6)), pltpu.SemaphoreType.DMA((n, 6)), pltpu.SemaphoreType.DMA((n,))]
        self.aliases = {}

    def _copy(self, outs, sems, a, k, q, pc, to, src=None):
        dst = _half_of_quarter(outs[a], self.kinds[a], q, pc)
        return pltpu.make_async_remote_copy(src_ref=dst if src is None else src, dst_ref=dst,
                                            send_sem=sems[0].at[a, k], recv_sem=sems[1].at[a, k],
                                            device_id=to, device_id_type=MESH)

    def _mine(self, ins, outs, sems, a, p):
        return pltpu.make_async_copy(ins[a], _quarter(outs[a], self.kinds[a], p), sems[2].at[a])

    def start(self, ins, outs, sems):
        x, y, c, chips = _place()
        p = 2 * x + y
        for a in range(len(ins)):
            self._mine(ins, outs, sems, a, p).start()
            for j, chip in enumerate(chips):
                self._copy(outs, sems, a, j, p, c, (*chip, c), src=_rows_half(ins[a], c)).start()

    def finish(self, ins, outs, sems):
        x, y, c, chips = _place()
        p = 2 * x + y
        sibling = (x, y, 1 - c)
        n = len(ins)
        for a in range(n):
            for j, (qx, qy) in enumerate(chips):
                q = 2 * qx + qy
                self._copy(outs, sems, a, j, q, c, sibling).wait_recv()
                self._copy(outs, sems, a, 3 + j, q, c, sibling).start()
        for a in range(n):
            for j, (qx, qy) in enumerate(chips):
                q = 2 * qx + qy
                self._copy(outs, sems, a, 3 + j, q, 1 - c, sibling).wait_recv()
                self._copy(outs, sems, a, j, p, c, (qx, qy, c), src=_rows_half(ins[a], c)).wait_send()
                self._copy(outs, sems, a, 3 + j, q, c, sibling).wait_send()
            self._mine(ins, outs, sems, a, p).wait()


class _ToSibling:
    def __init__(self, grads, kinds):
        self.ins = list(grads)
        self.kinds = list(kinds)
        n = len(self.ins)
        self.out_shape = []
        for g, k in zip(self.ins, self.kinds):
            qr, qc = _quarter_shape(g.shape, k)
            self.out_shape.append(jax.ShapeDtypeStruct((NS, qr // 2, qc), g.dtype))
        self.scratch = [pltpu.SemaphoreType.DMA((n, NS)), pltpu.SemaphoreType.DMA((n, NS))]
        self.aliases = {}

    def _copies(self, ins, outs, sems):
        x, y, c, _ = _place()
        return [pltpu.make_async_remote_copy(src_ref=_half_of_quarter(ins[a], self.kinds[a], q, 1 - c), dst_ref=outs[a].at[q],
                                             send_sem=sems[0].at[a, q], recv_sem=sems[1].at[a, q],
                                             device_id=(x, y, 1 - c), device_id_type=MESH)
                for a in range(len(ins)) for q in range(NS)]

    def start(self, ins, outs, sems):
        for cp in self._copies(ins, outs, sems):
            cp.start()

    def finish(self, ins, outs, sems):
        for cp in self._copies(ins, outs, sems):
            cp.wait()


class _ToChips:
    def __init__(self, sums):
        self.ins = list(sums)
        n = len(self.ins)
        self.out_shape = [jax.ShapeDtypeStruct((3,) + s.shape[1:], s.dtype) for s in self.ins]
        self.scratch = [pltpu.SemaphoreType.DMA((n, 3)), pltpu.SemaphoreType.DMA((n, 3))]
        self.aliases = {}

    def _copies(self, ins, outs, sems):
        x, y, c, chips = _place()
        return [pltpu.make_async_remote_copy(src_ref=ins[a].at[2 * qx + qy], dst_ref=outs[a].at[j],
                                             send_sem=sems[0].at[a, j], recv_sem=sems[1].at[a, j],
                                             device_id=(qx, qy, c), device_id_type=MESH)
                for a in range(len(ins)) for j, (qx, qy) in enumerate(chips)]

    def start(self, ins, outs, sems):
        for cp in self._copies(ins, outs, sems):
            cp.start()

    def finish(self, ins, outs, sems):
        for cp in self._copies(ins, outs, sems):
            cp.wait()


class _SwapHalves:
    def __init__(self, quarters):
        self.ins = list(quarters)
        n = len(self.ins)
        self.out_shape = [jax.ShapeDtypeStruct(g.shape, g.dtype) for g in self.ins]
        self.scratch = [pltpu.SemaphoreType.DMA((n,)), pltpu.SemaphoreType.DMA((n,))]
        self.aliases = {a: a for a in range(n)}

    def _copy(self, outs, sems, a, pc):
        x, y, c, _ = _place()
        rows = _rows_half(outs[a], pc)
        return pltpu.make_async_remote_copy(src_ref=rows, dst_ref=rows, send_sem=sems[0].at[a], recv_sem=sems[1].at[a],
                                            device_id=(x, y, 1 - c), device_id_type=MESH)

    def start(self, ins, outs, sems):
        c = lax.axis_index("c")
        for a in range(len(outs)):
            self._copy(outs, sems, a, c).start()

    def finish(self, ins, outs, sems):
        c = lax.axis_index("c")
        for a in range(len(outs)):
            self._copy(outs, sems, a, c).wait_send()
            self._copy(outs, sems, a, 1 - c).wait_recv()


def _call(name, grid, compute, in_specs, out_specs, out_shape, scratch, vmem_mb, args, jobs=()):
    n_in, n_out, n_scr = len(in_specs), len(out_specs), len(scratch)
    ji = [len(j.ins) for j in jobs]
    jo = [len(j.out_shape) for j in jobs]
    js = [len(j.scratch) for j in jobs]

    def body(*refs):
        pos = [0]

        def take(k):
            r = refs[pos[0]:pos[0] + k]
            pos[0] += k
            return r

        ins, jins = take(n_in), [take(k) for k in ji]
        outs, jouts = take(n_out), [take(k) for k in jo]
        scr, jscr = take(n_scr), [take(k) for k in js]
        if jobs and grid:
            ids = [pl.program_id(a) for a in range(len(grid))]
            first = functools.reduce(jnp.logical_and, [i == 0 for i in ids])
            last = functools.reduce(jnp.logical_and, [i == g - 1 for i, g in zip(ids, grid)])

            @pl.when(first)
            def _():
                for j, a, b, c in zip(jobs, jins, jouts, jscr):
                    j.start(a, b, c)
        elif jobs:
            for j, a, b, c in zip(jobs, jins, jouts, jscr):
                j.start(a, b, c)
        compute(ins, outs, scr)
        if jobs and grid:
            @pl.when(last)
            def _():
                for j, a, b, c in zip(jobs, jins, jouts, jscr):
                    j.finish(a, b, c)
        elif jobs:
            for j, a, b, c in zip(jobs, jins, jouts, jscr):
                j.finish(a, b, c)

    aliases = {}
    in_off, out_off = n_in, n_out
    for j, a, b in zip(jobs, ji, jo):
        for s, d in j.aliases.items():
            aliases[in_off + s] = out_off + d
        in_off += a
        out_off += b
    res = pl.pallas_call(
        body, name=name, grid=grid,
        in_specs=list(in_specs) + [ANY] * sum(ji), out_specs=list(out_specs) + [ANY] * sum(jo),
        out_shape=list(out_shape) + [s for j in jobs for s in j.out_shape],
        scratch_shapes=list(scratch) + [s for j in jobs for s in j.scratch],
        input_output_aliases=aliases, compiler_params=_cparams(len(grid), vmem_mb),
    )(*args, *[a for j in jobs for a in j.ins])
    res = list(res)
    main, rest, jres = res[:n_out], res[n_out:], []
    for k in jo:
        jres.append(rest[:k])
        rest = rest[k:]
    return main, jres


def _exchange(name, jobs):
    return _call(name, (), lambda ins, outs, scr: None, [], [], [], [], 16, [], jobs)[1]


def _allreduce_small(v):
    R, C = v.shape
    N = 8

    def body(v_ref, out_ref, gath, send_sems, recv_sems, local_sem):
        x, y, c, chips = _place()
        me, sibling = (x, y, c), (x, y, 1 - c)

        def rows(px, py, pc):
            return gath.at[pl.ds((4 * px + 2 * py + pc) * R, R), :]

        def copy(k, block, to, src=None):
            return pltpu.make_async_remote_copy(src_ref=rows(*block) if src is None else src, dst_ref=rows(*block),
                                                send_sem=send_sems.at[k], recv_sem=recv_sems.at[k],
                                                device_id=to, device_id_type=MESH)

        mine = pltpu.make_async_copy(v_ref, rows(*me), local_sem)
        mine.start()
        first = [copy(0, me, sibling, src=v_ref)]
        first += [copy(1 + j, me, (*chip, c), src=v_ref) for j, chip in enumerate(chips)]
        for cp in first:
            cp.start()
        passed = [copy(4 + j, (*chip, c), sibling) for j, chip in enumerate(chips)]
        for j, chip in enumerate(chips):
            copy(1 + j, (*chip, c), me).wait_recv()
            passed[j].start()
        copy(0, sibling, me).wait_recv()
        for j, chip in enumerate(chips):
            copy(4 + j, (*chip, 1 - c), me).wait_recv()
        for cp in first + passed:
            cp.wait_send()
        mine.wait()
        acc = gath[0:R, :]
        for d in range(1, N):
            acc = acc + gath[d * R:(d + 1) * R, :]
        out_ref[...] = acc

    return pl.pallas_call(
        body, name="allreduce_small",
        in_specs=[pl.BlockSpec(memory_space=pltpu.VMEM)], out_specs=pl.BlockSpec(memory_space=pltpu.VMEM),
        out_shape=jax.ShapeDtypeStruct((R, C), F32),
        scratch_shapes=[pltpu.VMEM((N * R, C), F32), pltpu.SemaphoreType.DMA((7,)), pltpu.SemaphoreType.DMA((7,)),
                        pltpu.SemaphoreType.DMA],
    )(v)


def _ffn_fwd(h, g, wg, wu, wd, name, jobs=()):
    S, D = h.shape
    F = wg.shape[0]
    ts = _tile(S, 512)
    fb = _tile(F, F // 2)
    nf = F // fb

    def compute(ins, outs, scr):
        h_ref, g_ref, wg_ref, wu_ref, wd_ref = ins
        ho_ref, n_ref, gp_ref, up_ref = outs
        nscr, acc = scr
        s = pl.program_id(1)

        @pl.when(s == 0)
        def _():
            n = _rms_fwd(h_ref[...], g_ref[...]).astype(BF16)
            nscr[...] = n
            n_ref[...] = n
            acc[...] = jnp.zeros_like(acc)

        n = nscr[...]
        gp = _dot_nt(n, wg_ref[...])
        up = _dot_nt(n, wu_ref[...])
        gp_ref[...] = gp.astype(BF16)
        up_ref[...] = up.astype(BF16)
        a = (gp * _sigmoid(gp) * up).astype(BF16)
        acc[...] += _dot(a, wd_ref[...])

        @pl.when(s == nf - 1)
        def _():
            ho_ref[...] = h_ref[...] + 0.5 * acc[...]

    tok = pl.BlockSpec((ts, D), lambda i, s: (i, 0))
    wsp = pl.BlockSpec((fb, D), lambda i, s: (s, 0))
    hid = pl.BlockSpec((ts, fb), lambda i, s: (i, s))
    return _call(name, (S // ts, nf), compute,
                 [tok, pl.BlockSpec((1, D), lambda i, s: (0, 0)), wsp, wsp, wsp], [tok, tok, hid, hid],
                 [jax.ShapeDtypeStruct((S, D), F32), jax.ShapeDtypeStruct((S, D), BF16),
                  jax.ShapeDtypeStruct((S, F), BF16), jax.ShapeDtypeStruct((S, F), BF16)],
                 [pltpu.VMEM((ts, D), BF16), pltpu.VMEM((ts, D), F32)], 56, [h, g, wg, wu, wd], jobs)


def _ffn_bwd(dh, h, g, gp, up, wg, wu, wd, name, jobs=()):
    S, D = h.shape
    F = wg.shape[0]
    ts = _tile(S, 1024)
    fb = _tile(F, 256)
    nf = F // fb

    def compute(ins, outs, scr):
        dh_ref, h_ref, g_ref, gp_ref, up_ref, wg_ref, wu_ref, wd_ref = ins
        dhi_ref, dgp_ref, dup_ref, a_ref, do_ref, dg_ref = outs
        doscr, acc = scr
        i = pl.program_id(0)
        s = pl.program_id(1)

        @pl.when(s == 0)
        def _():
            d = (0.5 * dh_ref[...]).astype(BF16)
            doscr[...] = d
            do_ref[...] = d
            acc[...] = jnp.zeros_like(acc)

        @pl.when(jnp.logical_and(i == 0, s == 0))
        def _():
            dg_ref[...] = jnp.zeros_like(dg_ref)

        da = _dot_nt(doscr[...], wd_ref[...])
        gf = gp_ref[...].astype(F32)
        uf = up_ref[...].astype(F32)
        sg = _sigmoid(gf)
        si = gf * sg
        dgp = (da * uf * (sg * (1.0 + gf * (1.0 - sg)))).astype(BF16)
        dup = (da * si).astype(BF16)
        a_ref[...] = (si * uf).astype(BF16)
        dgp_ref[...] = dgp
        dup_ref[...] = dup
        acc[...] += _dot(dgp, wg_ref[...]) + _dot(dup, wu_ref[...])

        @pl.when(s == nf - 1)
        def _():
            dx, dg = _rms_bwd(h_ref[...], g_ref[...], acc[...])
            dhi_ref[...] = dh_ref[...] + dx
            dg_ref[...] += dg

    tok = pl.BlockSpec((ts, D), lambda i, s: (i, 0))
    hid = pl.BlockSpec((ts, fb), lambda i, s: (i, s))
    row = pl.BlockSpec((1, D), lambda i, s: (0, 0))
    wsp = pl.BlockSpec((fb, D), lambda i, s: (s, 0))
    return _call(name, (S // ts, nf), compute, [tok, tok, row, hid, hid, wsp, wsp, wsp], [tok, hid, hid, hid, tok, row],
                 [jax.ShapeDtypeStruct((S, D), F32), jax.ShapeDtypeStruct((S, F), BF16), jax.ShapeDtypeStruct((S, F), BF16),
                  jax.ShapeDtypeStruct((S, F), BF16), jax.ShapeDtypeStruct((S, D), BF16), jax.ShapeDtypeStruct((1, D), F32)],
                 [pltpu.VMEM((ts, D), BF16), pltpu.VMEM((ts, D), F32)], 56, [dh, h, g, gp, up, wg, wu, wd], jobs)


def _ffn_wgrad(hid, tok, name, jobs=()):
    S, D = tok.shape
    F = hid.shape[1]
    fb = _tile(F, F // 2)
    tk = _tile(S, 1024)

    def compute(ins, outs, scr):
        @pl.when(pl.program_id(1) == 0)
        def _():
            outs[0][...] = jnp.zeros_like(outs[0])

        outs[0][...] += _dot_tn(ins[0][...], ins[1][...])

    main, jres = _call(name, (F // fb, S // tk), compute,
                       [pl.BlockSpec((tk, fb), lambda j, k: (k, j)), pl.BlockSpec((tk, D), lambda j, k: (k, 0))],
                       [pl.BlockSpec((fb, D), lambda j, k: (j, 0))], [jax.ShapeDtypeStruct((F, D), F32)], [], 48,
                       [hid, tok], jobs)
    return main[0], jres


def _mix_in_fwd(h, g, win, jobs=()):
    S, D = h.shape
    NG = win.shape[1] // D
    ts = _tile(S, 512)

    def compute(ins, outs, scr):
        h_ref, g_ref, w_ref = ins
        u_ref, z_ref = outs

        @pl.when(pl.program_id(1) == 0)
        def _():
            u = _rms_fwd(h_ref[...], g_ref[...]).astype(BF16)
            scr[0][...] = u
            u_ref[...] = u

        z_ref[0] = _dot(scr[0][...], w_ref[...]).astype(BF16)

    return _call("mix_in_fwd", (S // ts, NG), compute,
                 [pl.BlockSpec((ts, D), lambda i, k: (i, 0)), pl.BlockSpec((1, D), lambda i, k: (0, 0)),
                  pl.BlockSpec((D, D), lambda i, k: (0, k))],
                 [pl.BlockSpec((ts, D), lambda i, k: (i, 0)), pl.BlockSpec((1, ts, D), lambda i, k: (k, i, 0))],
                 [jax.ShapeDtypeStruct((S, D), BF16), jax.ShapeDtypeStruct((NG, S, D), BF16)],
                 [pltpu.VMEM((ts, D), BF16)], 40, [h, g, win], jobs)


def _conv_fwd(z, wa, ba, wb, jobs=()):
    _, S, D = z.shape
    _, KA, CB = wa.shape
    KB = wb.shape[1]
    ts = _tile(S, 512)
    r = ts // HALO
    CH = min(128, ts)

    def compute(ins, outs, scr):
        z_ref, zh_ref, wa_ref, ba_ref, wb_ref = ins
        a1_ref, q_ref = outs
        sa, sb = scr
        keep = (pl.program_id(1) > 0).astype(F32)
        sa[HALO:HALO + ts, :] = z_ref[0].astype(F32) * _sigmoid(z_ref[1].astype(F32))
        sa[0:HALO, :] = zh_ref[0].astype(F32) * _sigmoid(zh_ref[1].astype(F32)) * keep
        sb[HALO:HALO + ts, :] = z_ref[3].astype(F32) * z_ref[4].astype(F32)
        sb[0:HALO, :] = zh_ref[3].astype(F32) * zh_ref[4].astype(F32) * keep
        wak = [wa_ref[0, k:k + 1, :] for k in range(KA)]
        wbk = [wb_ref[0, k:k + 1, :] for k in range(KB)]
        for c0 in range(0, ts, CH):
            acc = jnp.broadcast_to(ba_ref[...], (CH, CB))
            for k in range(KA):
                o = c0 + HALO - (KA - 1) + k
                acc = acc + wak[k] * sa[o:o + CH, :]
            a1_ref[c0:c0 + CH, :] = acc
            v = jnp.zeros((CH, CB), F32)
            for k in range(KB):
                o = c0 + HALO - (KB - 1) + k
                v = v + wbk[k] * sb[o:o + CH, :]
            q_ref[c0:c0 + CH, :] = (z_ref[2, c0:c0 + CH, :].astype(F32) * v).astype(BF16)

    return _call("conv_fwd", (D // CB, S // ts), compute,
                 [pl.BlockSpec((5, ts, CB), lambda j, i: (0, i, j)),
                  pl.BlockSpec((5, HALO, CB), lambda j, i: (0, jnp.maximum(i * r - 1, 0), j)),
                  pl.BlockSpec((1, KA, CB), lambda j, i: (j, 0, 0)), pl.BlockSpec((1, CB), lambda j, i: (0, j)),
                  pl.BlockSpec((1, KB, CB), lambda j, i: (j, 0, 0))],
                 [pl.BlockSpec((ts, CB), lambda j, i: (i, j)), pl.BlockSpec((ts, CB), lambda j, i: (i, j))],
                 [jax.ShapeDtypeStruct((S, D), F32), jax.ShapeDtypeStruct((S, D), BF16)],
                 [pltpu.VMEM((HALO + ts, CB), F32), pltpu.VMEM((HALO + ts, CB), F32)], 32, [z, z, wa, ba, wb], jobs)


def _ln_stats(a1):
    mu = jnp.mean(a1, axis=-1, keepdims=True)
    xc = a1 - mu
    rstd = lax.rsqrt(jnp.mean(xc * xc, axis=-1, keepdims=True) + EPS)
    return xc * rstd, rstd


def _mix_out_fwd(h1, a1, q, z, lng, lnb, wa, wb, wo):
    S, D = h1.shape
    ts = _tile(S, 256)

    def compute(ins, outs, scr):
        h_ref, a1_ref, q_ref, ga_ref, gb_ref, lng_ref, lnb_ref, wa_ref, wb_ref, wo_ref = ins
        h2_ref, a3_ref, m_ref, ya_ref, yb_ref = outs
        xhat, _ = _ln_stats(a1_ref[...])
        a2 = xhat * lng_ref[...] + lnb_ref[...]
        a3 = (a2 * _sigmoid(a2)).astype(BF16)
        a3_ref[...] = a3
        ya = _dot(a3, wa_ref[...])
        yb = _dot(q_ref[...], wb_ref[...])
        ya_ref[...] = ya.astype(BF16)
        yb_ref[...] = yb.astype(BF16)
        m = (_sigmoid(ga_ref[0].astype(F32)) * ya + _sigmoid(gb_ref[0].astype(F32)) * yb).astype(BF16)
        m_ref[...] = m
        h2_ref[...] = h_ref[...] + _dot(m, wo_ref[...])

    tok = pl.BlockSpec((ts, D), lambda i: (i, 0))
    row = pl.BlockSpec((1, D), lambda i: (0, 0))
    mat = pl.BlockSpec((D, D), lambda i: (0, 0))
    return _call("mix_out_fwd", (S // ts,), compute,
                 [tok, tok, tok, pl.BlockSpec((1, ts, D), lambda i: (5, i, 0)), pl.BlockSpec((1, ts, D), lambda i: (6, i, 0)),
                  row, row, mat, mat, mat], [tok] * 5,
                 [jax.ShapeDtypeStruct((S, D), F32)] + [jax.ShapeDtypeStruct((S, D), BF16)] * 4,
                 [], 56, [h1, a1, q, z, z, lng, lnb, wa, wb, wo])[0]


def _mix_out_bwd(dh2, a1, z, ya, yb, lng, lnb, wa, wb, wo, jobs=()):
    S, D = dh2.shape
    ts = _tile(S, 256)

    def compute(ins, outs, scr):
        dh_ref, a1_ref, ga_ref, gb_ref, ya_ref, yb_ref, lng_ref, lnb_ref, wa_ref, wb_ref, wo_ref = ins
        da1_ref, dq_ref, dga_ref, dgb_ref, dya_ref, dyb_ref, dhb_ref, dlg_ref, dlb_ref = outs

        @pl.when(pl.program_id(0) == 0)
        def _():
            dlg_ref[...] = jnp.zeros_like(dlg_ref)
            dlb_ref[...] = jnp.zeros_like(dlb_ref)

        dhb = dh_ref[...].astype(BF16)
        dhb_ref[...] = dhb
        dm = _dot_nt(dhb, wo_ref[...])
        sa = _sigmoid(ga_ref[0].astype(F32))
        sb = _sigmoid(gb_ref[0].astype(F32))
        dga_ref[...] = (dm * ya_ref[...].astype(F32) * sa * (1.0 - sa)).astype(BF16)
        dgb_ref[...] = (dm * yb_ref[...].astype(F32) * sb * (1.0 - sb)).astype(BF16)
        dya = (sa * dm).astype(BF16)
        dyb = (sb * dm).astype(BF16)
        dya_ref[...] = dya
        dyb_ref[...] = dyb
        dq_ref[...] = _dot_nt(dyb, wb_ref[...])
        da3 = _dot_nt(dya, wa_ref[...])
        xhat, rstd = _ln_stats(a1_ref[...])
        a2 = xhat * lng_ref[...] + lnb_ref[...]
        sg = _sigmoid(a2)
        da2 = da3 * (sg * (1.0 + a2 * (1.0 - sg)))
        dlg_ref[...] += jnp.sum(da2 * xhat, axis=0, keepdims=True)
        dlb_ref[...] += jnp.sum(da2, axis=0, keepdims=True)
        dxh = da2 * lng_ref[...]
        da1_ref[...] = rstd * (dxh - jnp.mean(dxh, axis=-1, keepdims=True)
                               - xhat * jnp.mean(dxh * xhat, axis=-1, keepdims=True))

    tok = pl.BlockSpec((ts, D), lambda i: (i, 0))
    row = pl.BlockSpec((1, D), lambda i: (0, 0))
    mat = pl.BlockSpec((D, D), lambda i: (0, 0))
    return _call("mix_out_bwd", (S // ts,), compute,
                 [tok, tok, pl.BlockSpec((1, ts, D), lambda i: (5, i, 0)), pl.BlockSpec((1, ts, D), lambda i: (6, i, 0)),
                  tok, tok, row, row, mat, mat, mat], [tok] * 7 + [row, row],
                 [jax.ShapeDtypeStruct((S, D), F32), jax.ShapeDtypeStruct((S, D), F32)]
                 + [jax.ShapeDtypeStruct((S, D), BF16)] * 5 + [jax.ShapeDtypeStruct((1, D), F32)] * 2,
                 [], 56, [dh2, a1, z, z, ya, yb, lng, lnb, wa, wb, wo], jobs)


def _mixer_wgrads(a3, dya, q, dyb, mm, dhb, jobs=()):
    S, D = a3.shape
    tk = _tile(S, 512)

    def compute(ins, outs, scr):
        @pl.when(pl.program_id(0) == 0)
        def _():
            for o in outs:
                o[...] = jnp.zeros_like(o)

        for t in range(3):
            outs[t][...] += _dot_tn(ins[2 * t][...], ins[2 * t + 1][...])

    tok = pl.BlockSpec((tk, D), lambda k: (k, 0))
    return _call("mixer_wgrads", (S // tk,), compute, [tok] * 6, [pl.BlockSpec((D, D), lambda k: (0, 0))] * 3,
                 [jax.ShapeDtypeStruct((D, D), F32)] * 3, [], 56, [a3, dya, q, dyb, mm, dhb], jobs)


def _conv_bwd(z, da1, dq, dga, dgb, wa, wb, jobs=()):
    NG, S, D = z.shape
    _, KA, CB = wa.shape
    KB = wb.shape[1]
    ts = _tile(S, 512)
    r = ts // HALO
    nt = S // ts
    CH = min(128, ts)
    last_halo = S // HALO - 1

    def compute(ins, outs, scr):
        z_ref, zp_ref, zn_ref, da1_ref, da1n_ref, dq_ref, dqn_ref, dga_ref, dgb_ref, wa_ref, wb_ref = ins
        dz_ref, dwa_ref, dba_ref, dwb_ref = outs
        sa0, sd, sp, sv, acca, accb = scr
        i = pl.program_id(1)
        prev = (i > 0).astype(F32)
        nxt = (i < nt - 1).astype(F32)

        @pl.when(i == 0)
        def _():
            acca[...] = jnp.zeros_like(acca)
            accb[...] = jnp.zeros_like(accb)
            dba_ref[...] = jnp.zeros_like(dba_ref)

        sa0[HALO:HALO + ts, :] = z_ref[0].astype(F32) * _sigmoid(z_ref[1].astype(F32))
        sa0[0:HALO, :] = zp_ref[0].astype(F32) * _sigmoid(zp_ref[1].astype(F32)) * prev
        sp[HALO:HALO + ts, :] = z_ref[3].astype(F32) * z_ref[4].astype(F32)
        sp[0:HALO, :] = zp_ref[3].astype(F32) * zp_ref[4].astype(F32) * prev
        sd[0:ts, :] = da1_ref[...]
        sd[ts:ts + HALO, :] = da1n_ref[...] * nxt
        sv[0:ts, :] = dq_ref[...] * z_ref[2].astype(F32)
        sv[ts:ts + HALO, :] = dqn_ref[...] * zn_ref[2].astype(F32) * nxt
        dba_ref[...] += jnp.sum(da1_ref[...], axis=0, keepdims=True)
        wak = [wa_ref[0, k:k + 1, :] for k in range(KA)]
        wbk = [wb_ref[0, k:k + 1, :] for k in range(KB)]
        for c0 in range(0, ts, CH):
            rows = slice(c0, c0 + CH)
            d1 = sd[rows, :]
            da0 = jnp.zeros((CH, CB), F32)
            for k in range(KA):
                o = c0 + (KA - 1) - k
                da0 = da0 + wak[k] * sd[o:o + CH, :]
                o = c0 + HALO - (KA - 1) + k
                acca[k] += jnp.sum((d1 * sa0[o:o + CH, :]).reshape(CH // 8, 8, CB), axis=0)
            val = z_ref[0, rows, :].astype(F32)
            sg = _sigmoid(z_ref[1, rows, :].astype(F32))
            dz_ref[0, rows, :] = (da0 * sg).astype(BF16)
            dz_ref[1, rows, :] = (da0 * val * sg * (1.0 - sg)).astype(BF16)
            dv = sv[rows, :]
            v = jnp.zeros((CH, CB), F32)
            dp = jnp.zeros((CH, CB), F32)
            for k in range(KB):
                o = c0 + HALO - (KB - 1) + k
                pw = sp[o:o + CH, :]
                v = v + wbk[k] * pw
                accb[k] += jnp.sum((dv * pw).reshape(CH // 8, 8, CB), axis=0)
                o = c0 + (KB - 1) - k
                dp = dp + wbk[k] * sv[o:o + CH, :]
            dz_ref[2, rows, :] = (dq_ref[rows, :] * v).astype(BF16)
            dz_ref[3, rows, :] = (dp * z_ref[4, rows, :].astype(F32)).astype(BF16)
            dz_ref[4, rows, :] = (dp * z_ref[3, rows, :].astype(F32)).astype(BF16)
        dz_ref[5] = dga_ref[...]
        dz_ref[6] = dgb_ref[...]

        @pl.when(i == nt - 1)
        def _():
            dwa_ref[0] = jnp.sum(acca[...], axis=1)
            dwb_ref[0] = jnp.sum(accb[...], axis=1)

    zt = pl.BlockSpec((5, ts, CB), lambda j, i: (0, i, j))
    zp = pl.BlockSpec((5, HALO, CB), lambda j, i: (0, jnp.maximum(i * r - 1, 0), j))
    zn = pl.BlockSpec((5, HALO, CB), lambda j, i: (0, jnp.minimum((i + 1) * r, last_halo), j))
    tok = pl.BlockSpec((ts, CB), lambda j, i: (i, j))
    tokn = pl.BlockSpec((HALO, CB), lambda j, i: (jnp.minimum((i + 1) * r, last_halo), j))
    return _call("conv_bwd", (D // CB, nt), compute,
                 [zt, zp, zn, tok, tokn, tok, tokn, tok, tok,
                  pl.BlockSpec((1, KA, CB), lambda j, i: (j, 0, 0)), pl.BlockSpec((1, KB, CB), lambda j, i: (j, 0, 0))],
                 [pl.BlockSpec((NG, ts, CB), lambda j, i: (0, i, j)), pl.BlockSpec((1, KA, CB), lambda j, i: (j, 0, 0)),
                  pl.BlockSpec((1, CB), lambda j, i: (0, j)), pl.BlockSpec((1, KB, CB), lambda j, i: (j, 0, 0))],
                 [jax.ShapeDtypeStruct((NG, S, D), BF16), jax.ShapeDtypeStruct((D // CB, KA, CB), F32),
                  jax.ShapeDtypeStruct((1, D), F32), jax.ShapeDtypeStruct((D // CB, KB, CB), F32)],
                 [pltpu.VMEM((HALO + ts, CB), F32), pltpu.VMEM((ts + HALO, CB), F32),
                  pltpu.VMEM((HALO + ts, CB), F32), pltpu.VMEM((ts + HALO, CB), F32),
                  pltpu.VMEM((KA, 8, CB), F32), pltpu.VMEM((KB, 8, CB), F32)],
                 40, [z, z, z, da1, da1, dq, dq, dga, dgb, wa, wb], jobs)


def _mix_in_bwd(dh2, h1, g, dz, win, jobs=()):
    S, D = h1.shape
    NG = dz.shape[0]
    ts = _tile(S, 512)

    def compute(ins, outs, scr):
        dh_ref, h_ref, g_ref, dz_ref, w_ref = ins
        dhi_ref, dg_ref = outs
        acc = scr[0]
        i = pl.program_id(0)
        k = pl.program_id(1)

        @pl.when(k == 0)
        def _():
            acc[...] = jnp.zeros_like(acc)

        @pl.when(jnp.logical_and(i == 0, k == 0))
        def _():
            dg_ref[...] = jnp.zeros_like(dg_ref)

        acc[...] += _dot_nt(dz_ref[0], w_ref[...])

        @pl.when(k == NG - 1)
        def _():
            dx, dg = _rms_bwd(h_ref[...], g_ref[...], acc[...])
            dhi_ref[...] = dh_ref[...] + dx
            dg_ref[...] += dg

    tok = pl.BlockSpec((ts, D), lambda i, k: (i, 0))
    row = pl.BlockSpec((1, D), lambda i, k: (0, 0))
    return _call("mix_in_bwd", (S // ts, NG), compute,
                 [tok, tok, row, pl.BlockSpec((1, ts, D), lambda i, k: (k, i, 0)), pl.BlockSpec((D, D), lambda i, k: (0, k))],
                 [tok, row], [jax.ShapeDtypeStruct((S, D), F32), jax.ShapeDtypeStruct((1, D), F32)],
                 [pltpu.VMEM((ts, D), F32)], 40, [dh2, h1, g, dz, win], jobs)


def _w_in_grad(u, dz, jobs=()):
    S, D = u.shape
    NG = dz.shape[0]
    tk = _tile(S, 1024)

    def compute(ins, outs, scr):
        @pl.when(pl.program_id(1) == 0)
        def _():
            outs[0][...] = jnp.zeros_like(outs[0])

        outs[0][...] += _dot_tn(ins[0][...], ins[1][0])

    return _call("w_in_grad", (NG, S // tk), compute,
                 [pl.BlockSpec((tk, D), lambda j, k: (k, 0)), pl.BlockSpec((1, tk, D), lambda j, k: (j, k, 0))],
                 [pl.BlockSpec((D, D), lambda j, k: (0, j))], [jax.ShapeDtypeStruct((D, NG * D), F32)], [], 40, [u, dz], jobs)


def _loss_head(h3, t, g):
    S, D = h3.shape
    ts = _tile(S, 512)

    def compute(ins, outs, scr):
        h_ref, t_ref, g_ref = ins
        dh_ref, dg_ref, loss_ref = outs

        @pl.when(pl.program_id(0) == 0)
        def _():
            dg_ref[...] = jnp.zeros_like(dg_ref)
            loss_ref[...] = jnp.zeros_like(loss_ref)

        x = h_ref[...]
        err = _rms_fwd(x, g_ref[...]) - t_ref[...]
        loss_ref[...] += (0.5 / D) * jnp.sum(err * err)
        dx, dg = _rms_bwd(x, g_ref[...], err * (1.0 / D))
        dh_ref[...] = dx
        dg_ref[...] += dg

    tok = pl.BlockSpec((ts, D), lambda i: (i, 0))
    row = pl.BlockSpec((1, D), lambda i: (0, 0))
    return _call("loss_head", (S // ts,), compute, [tok, tok, row], [tok, row, pl.BlockSpec((8, 128), lambda i: (0, 0))],
                 [jax.ShapeDtypeStruct((S, D), F32), jax.ShapeDtypeStruct((1, D), F32), jax.ShapeDtypeStruct((8, 128), F32)],
                 [], 40, [h3, t, g])[0]


def _chip_sums(place, grads, got, kind, name):
    n = len(grads)
    qr, qc = _quarter_shape(grads[0].shape, kind)
    h = qr // 2
    tr = _row_tile(h)
    nr = h // tr

    def body(pc_ref, *refs):
        g_refs, got_refs, b_refs, f_refs = refs[:n], refs[n:2 * n], refs[2 * n:3 * n], refs[3 * n:]
        own = pl.program_id(1) == pc_ref[0]
        for a in range(n):
            s = g_refs[a][...] + got_refs[a][0]
            b_refs[a][0] = s.astype(BF16)

            @pl.when(own)
            def _():
                f_refs[a][...] = s

    if kind == "rows":
        gspec = pl.BlockSpec((tr, qc), lambda r, q, pc: (q * (2 * nr) + pc[1] * nr + r, 0))
    else:
        gspec = pl.BlockSpec((tr, qc), lambda r, q, pc: (pc[1] * nr + r, q))
    lspec = pl.BlockSpec((1, tr, qc), lambda r, q, pc: (q, r, 0))
    res = pl.pallas_call(
        body, name=name,
        grid_spec=pltpu.PrefetchScalarGridSpec(
            num_scalar_prefetch=1, grid=(nr, NS), in_specs=[gspec] * n + [lspec] * n,
            out_specs=[lspec] * n + [pl.BlockSpec((tr, qc), lambda r, q, pc: (r, 0))] * n),
        out_shape=[jax.ShapeDtypeStruct((NS, h, qc), BF16)] * n + [jax.ShapeDtypeStruct((h, qc), F32)] * n,
        compiler_params=_cparams(2, 48),
    )(place, *grads, *got)
    return res[:n], res[n:]


def _totals(place, own, got, name):
    n = len(own)
    h, qc = own[0].shape
    tr = _row_tile(h)
    nr = h // tr

    def body(pc_ref, *refs):
        own_refs, got_refs, o_refs = refs[:n], refs[n:2 * n], refs[2 * n:]
        for a in range(n):
            g = got_refs[a]
            o_refs[a][...] = ((own_refs[a][...] + g[0].astype(F32)) + g[1].astype(F32)) + g[2].astype(F32)

    return pl.pallas_call(
        body, name=name,
        grid_spec=pltpu.PrefetchScalarGridSpec(
            num_scalar_prefetch=1, grid=(nr,),
            in_specs=[pl.BlockSpec((tr, qc), lambda r, pc: (r, 0))] * n + [pl.BlockSpec((3, tr, qc), lambda r, pc: (0, r, 0))] * n,
            out_specs=[pl.BlockSpec((tr, qc), lambda r, pc: (pc[1] * nr + r, 0))] * n),
        out_shape=[jax.ShapeDtypeStruct((2 * h, qc), F32)] * n,
        compiler_params=_cparams(1, 48),
    )(place, *own, *got)


def _adamw(ws, gs, ms, vs, name):
    n = len(ws)
    R, C = ws[0].shape
    tr = _row_tile(R, (36 << 20) // (7 * 2 * 4 * n * C))
    c1 = 1.0 - ADAM_B1 ** ADAM_STEP
    c2 = 1.0 - ADAM_B2 ** ADAM_STEP

    def body(*refs):
        w_refs, g_refs, m_refs, v_refs = refs[:n], refs[n:2 * n], refs[2 * n:3 * n], refs[3 * n:4 * n]
        d_refs, mo_refs, vo_refs = refs[4 * n:5 * n], refs[5 * n:6 * n], refs[6 * n:]
        for a in range(n):
            gv = g_refs[a][...]
            mn = ADAM_B1 * m_refs[a][...] + (1.0 - ADAM_B1) * gv
            vn = ADAM_B2 * v_refs[a][...] + (1.0 - ADAM_B2) * (gv * gv)
            mo_refs[a][...] = mn
            vo_refs[a][...] = vn
            d_refs[a][...] = -ADAM_LR * ((mn / c1) / (jnp.sqrt(vn / c2) + ADAM_EPS) + ADAM_WD * w_refs[a][...])

    blk = pl.BlockSpec((tr, C), lambda r: (r, 0))
    res = pl.pallas_call(
        body, name=name, grid=(R // tr,),
        in_specs=[blk] * (4 * n), out_specs=[blk] * (3 * n),
        out_shape=[jax.ShapeDtypeStruct((R, C), F32)] * (3 * n),
        compiler_params=_cparams(1, 56),
    )(*ws, *gs, *ms, *vs)
    return res[:n], res[n:2 * n], res[2 * n:]


def kernel(x, ffn1_norm, ffn1_w_gate, ffn1_w_up, ffn1_w_down, mix_norm, w_in, a_dw_w, a_dw_b, a_ln_g, a_ln_b, a_w_out, b_conv_w, b_w_out, w_o, ffn2_norm, ffn2_w_gate, ffn2_w_up, ffn2_w_down, final_norm, loss_target, m_ffn1_norm, m_ffn1_w_gate, m_ffn1_w_up, m_ffn1_w_down, m_mix_norm, m_w_in, m_a_dw_w, m_a_dw_b, m_a_ln_g, m_a_ln_b, m_a_w_out, m_b_conv_w, m_b_w_out, m_w_o, m_ffn2_norm, m_ffn2_w_gate, m_ffn2_w_up, m_ffn2_w_down, m_final_norm, v_ffn1_norm, v_ffn1_w_gate, v_ffn1_w_up, v_ffn1_w_down, v_mix_norm, v_w_in, v_a_dw_w, v_a_dw_b, v_a_ln_g, v_a_ln_b, v_a_w_out, v_b_conv_w, v_b_w_out, v_w_o, v_ffn2_norm, v_ffn2_w_gate, v_ffn2_w_up, v_ffn2_w_down, v_final_norm):
    names = ["ffn1_norm", "ffn1_w_gate", "ffn1_w_up", "ffn1_w_down", "mix_norm", "w_in", "a_dw_w", "a_dw_b", "a_ln_g",
             "a_ln_b", "a_w_out", "b_conv_w", "b_w_out", "w_o", "ffn2_norm", "ffn2_w_gate", "ffn2_w_up", "ffn2_w_down",
             "final_norm"]
    W = dict(zip(names, [ffn1_norm, ffn1_w_gate, ffn1_w_up, ffn1_w_down, mix_norm, w_in, a_dw_w, a_dw_b, a_ln_g, a_ln_b,
                         a_w_out, b_conv_w, b_w_out, w_o, ffn2_norm, ffn2_w_gate, ffn2_w_up, ffn2_w_down, final_norm]))
    M = dict(zip(names, [m_ffn1_norm, m_ffn1_w_gate, m_ffn1_w_up, m_ffn1_w_down, m_mix_norm, m_w_in, m_a_dw_w, m_a_dw_b,
                         m_a_ln_g, m_a_ln_b, m_a_w_out, m_b_conv_w, m_b_w_out, m_w_o, m_ffn2_norm, m_ffn2_w_gate,
                         m_ffn2_w_up, m_ffn2_w_down, m_final_norm]))
    V = dict(zip(names, [v_ffn1_norm, v_ffn1_w_gate, v_ffn1_w_up, v_ffn1_w_down, v_mix_norm, v_w_in, v_a_dw_w, v_a_dw_b,
                         v_a_ln_g, v_a_ln_b, v_a_w_out, v_b_conv_w, v_b_w_out, v_w_o, v_ffn2_norm, v_ffn2_w_gate,
                         v_ffn2_w_up, v_ffn2_w_down, v_final_norm]))
    transposed = ("ffn1_w_gate", "ffn1_w_up", "ffn2_w_gate", "ffn2_w_up")
    vecs = ["ffn1_norm", "mix_norm", "a_dw_b", "a_ln_g", "a_ln_b", "ffn2_norm", "final_norm"]
    ffn1 = ["ffn1_w_gate", "ffn1_w_up", "ffn1_w_down"]
    ffn2 = ["ffn2_w_gate", "ffn2_w_up", "ffn2_w_down"]
    outp = ["a_w_out", "b_w_out", "w_o"]

    S, D = x.shape[1], x.shape[2]
    CB = D // NS
    KA, KB = a_dw_w.shape[1], b_conv_w.shape[1]
    px, py, pc = lax.axis_index("x"), lax.axis_index("y"), lax.axis_index("c")
    chip = 2 * px + py
    place = jnp.stack([chip, pc]).astype(jnp.int32)
    h0 = x.reshape(S, D)
    tgt = loss_target.reshape(S, D)
    row = lambda n: W[n].reshape(1, D)
    pad = lambda a, r: jnp.concatenate([a, jnp.zeros((r - a.shape[0], a.shape[1]), F32)], axis=0)

    def quarter(P, n):
        return jnp.transpose(P[n][0]) if n in transposed else P[n][0]

    def unquarter(a, n):
        return (jnp.transpose(a) if n in transposed else a).reshape(W[n].shape)

    wq = {n: quarter(W, n).astype(BF16) for n in ffn1 + ffn2 + outp + ["w_in"]}

    f1 = _exchange("gather_ffn1", [_Gather([wq[n] for n in ffn1], ["rows"] * 3)])[0]
    g_in = _Gather([wq["w_in"], pad(a_dw_w[0], 32), pad(b_conv_w[0], 16)], ["cols", "rows", "rows"])
    (h1, n1, gp1, up1), ((win, taps_a, taps_b),) = _ffn_fwd(h0, row("ffn1_norm"), *f1, "ffn1_fwd", [g_in])
    wa_taps = taps_a.reshape(NS, 32, CB)[:, :KA]
    wb_taps = taps_b.reshape(NS, 16, CB)[:, :KB]
    g_out = _Gather([wq[n] for n in outp] + [wq["ffn2_w_gate"]], ["rows"] * 4)
    (u, z), ((wa_out, wb_out, wo, f2g),) = _mix_in_fwd(h1, row("mix_norm"), win, [g_out])
    g_f2 = _Gather([wq["ffn2_w_up"], wq["ffn2_w_down"]], ["rows"] * 2)
    (a1, q), ((f2u, f2d),) = _conv_fwd(z, wa_taps, row("a_dw_b"), wb_taps, [g_f2])
    h2, a3, mm, ya, yb = _mix_out_fwd(h1, a1, q, z, row("a_ln_g"), row("a_ln_b"), wa_out, wb_out, wo)
    (h3, n2, gp2, up2), _ = _ffn_fwd(h2, row("ffn2_norm"), f2g, f2u, f2d, "ffn2_fwd")
    dh3, d_final, loss_part = _loss_head(h3, tgt, row("final_norm"))
    loss = lax.psum(loss_part[0, 0], ("x", "y", "c"))

    (dh2, dgp2, dup2, act2, do2, d_ffn2), _ = _ffn_bwd(dh3, h2, row("ffn2_norm"), gp2, up2, f2g, f2u, f2d, "ffn2_bwd")
    g2 = [_ffn_wgrad(dgp2, n2, "ffn2_dwg")[0], _ffn_wgrad(dup2, n2, "ffn2_dwu")[0], _ffn_wgrad(act2, do2, "ffn2_dwd")[0]]
    (da1, dq, dga, dgb, dya, dyb, dh2b, d_lng, d_lnb), (got,) = _mix_out_bwd(
        dh2, a1, z, ya, yb, row("a_ln_g"), row("a_ln_b"), wa_out, wb_out, wo, [_ToSibling(g2, ["rows"] * 3)])
    wire2, own2 = _chip_sums(place, g2, got, "rows", "ffn2_chip_sums")
    (dz, d_wa, d_ba, d_wb), (got,) = _conv_bwd(z, da1, dq, dga, dgb, wa_taps, wb_taps, [_ToChips(wire2)])
    half2 = _totals(place, own2, got, "ffn2_totals")
    (g_win,), (tot2,) = _w_in_grad(u, dz, [_SwapHalves(half2)])
    (dh1, d_mix), (got,) = _mix_in_bwd(dh2, h1, row("mix_norm"), dz, win, [_ToSibling([g_win], ["cols"])])
    wire_in, own_in = _chip_sums(place, [g_win], got, "cols", "w_in_chip_sum")
    (dx, dgp1, dup1, act1, do1, d_ffn1), (got,) = _ffn_bwd(dh1, h0, row("ffn1_norm"), gp1, up1, *f1, "ffn1_bwd",
                                                           [_ToChips(wire_in)])
    half_in = _totals(place, own_in, got, "w_in_total")
    g1g, (tot_in,) = _ffn_wgrad(dgp1, n1, "ffn1_dwg", [_SwapHalves(half_in)])
    g1u, (got_g,) = _ffn_wgrad(dup1, n1, "ffn1_dwu", [_ToSibling([g1g], ["rows"])])
    g1d, (got_u,) = _ffn_wgrad(act1, do1, "ffn1_dwd", [_ToSibling([g1u], ["rows"])])
    go, (got_d,) = _mixer_wgrads(a3, dya, q, dyb, mm, dh2b, [_ToSibling([g1d], ["rows"])])
    g1 = [g1g, g1u, g1d]
    wire1, own1 = _chip_sums(place, g1, got_g + got_u + got_d, "rows", "ffn1_chip_sums")
    (got,) = _exchange("mixer_grads_to_sibling", [_ToSibling(go, ["rows"] * 3)])
    wire_o, own_o = _chip_sums(place, go, got, "rows", "mixer_chip_sums")
    got1, got_o = _exchange("last_grads_to_chips", [_ToChips(wire1), _ToChips(wire_o)])
    half1 = _totals(place, own1, got1, "ffn1_totals")
    half_o = _totals(place, own_o, got_o, "mixer_totals")
    tot1, tot_o = _exchange("last_grads_swap", [_SwapHalves(half1), _SwapHalves(half_o)])
    totals = dict(zip(ffn2 + ["w_in"] + ffn1 + outp, list(tot2) + list(tot_in) + list(tot1) + list(tot_o)))

    taps_ga = jnp.transpose(d_wa, (1, 0, 2)).reshape(KA, D)
    taps_gb = jnp.transpose(d_wb, (1, 0, 2)).reshape(KB, D)
    rows_a = -(-KA // 8) * 8
    rows_b = -(-KB // 8) * 8
    vec_grads = {"ffn1_norm": d_ffn1, "mix_norm": d_mix, "a_dw_b": d_ba, "a_ln_g": d_lng, "a_ln_b": d_lnb,
                 "ffn2_norm": d_ffn2, "final_norm": d_final}
    packed = jnp.concatenate([pad(jnp.concatenate([vec_grads[n] for n in vecs], axis=0), 8),
                              pad(taps_ga, rows_a), pad(taps_gb, rows_b)], axis=0)
    small = _allreduce_small(packed)
    g_vecs = small[0:8]
    g_taps = lax.dynamic_slice_in_dim(small[8:], chip * CB, CB, axis=1)

    def pack_vecs(P):
        return pad(jnp.concatenate([P[n].reshape(1, D) for n in vecs], axis=0), 8)

    def pack_taps(P):
        return jnp.concatenate([pad(P["a_dw_w"][0], rows_a), pad(P["b_conv_w"][0], rows_b)], axis=0)

    (dv_,), (mv_,), (vv_,) = _adamw([pack_vecs(W)], [g_vecs], [pack_vecs(M)], [pack_vecs(V)], "adamw_vectors")
    (dt_,), (mt_,), (vt_,) = _adamw([pack_taps(W)], [g_taps], [pack_taps(M)], [pack_taps(V)], "adamw_taps")

    grads, deltas, new_m, new_v = {}, {}, {}, {}
    for group, tag in ((ffn1 + ffn2, "ffn"), (["w_in"], "w_in"), (outp, "mixer")):
        ds, ms, vs = _adamw([quarter(W, n) for n in group], [totals[n] for n in group], [quarter(M, n) for n in group],
                            [quarter(V, n) for n in group], tag + "_adamw")
        for n, d_, m_, v_ in zip(group, ds, ms, vs):
            grads[n], deltas[n], new_m[n], new_v[n] = (unquarter(totals[n], n), unquarter(d_, n), unquarter(m_, n),
                                                       unquarter(v_, n))
    for i, n in enumerate(vecs):
        shp = W[n].shape
        grads[n] = g_vecs[i].reshape(shp)
        deltas[n], new_m[n], new_v[n] = dv_[i].reshape(shp), mv_[i].reshape(shp), vv_[i].reshape(shp)
    for n, lo, k in (("a_dw_w", 0, KA), ("b_conv_w", rows_a, KB)):
        shp = W[n].shape
        grads[n] = g_taps[lo:lo + k].reshape(shp)
        deltas[n], new_m[n], new_v[n] = (dt_[lo:lo + k].reshape(shp), mt_[lo:lo + k].reshape(shp),
                                         vt_[lo:lo + k].reshape(shp))

    return (loss, dx.reshape(x.shape), *[grads[n] for n in names], *[deltas[n] for n in names],
            *[new_m[n] for n in names], *[new_v[n] for n in names])
```

```python
import functools

import jax
import jax.numpy as jnp
from jax import lax
from jax.experimental import pallas as pl
from jax.experimental.pallas import tpu as pltpu

F32 = jnp.float32
BF16 = jnp.bfloat16
EPS = 1e-6
NS = 4
HALO = 32
MESH = pl.DeviceIdType.MESH
ANY = pl.BlockSpec(memory_space=pl.ANY)

ADAM_LR = 0.001
ADAM_B1 = 0.9
ADAM_B2 = 0.999
ADAM_EPS = 1e-08
ADAM_WD = 0.01
ADAM_STEP = 10


def _cparams(n_axes, vmem_mb):
    return pltpu.CompilerParams(dimension_semantics=("arbitrary",) * n_axes, vmem_limit_bytes=vmem_mb << 20)


def _tile(n, t):
    return t if n % t == 0 else n


def _resident(shape):
    return pl.BlockSpec(shape, lambda *_: (0,) * len(shape), pipeline_mode=pl.Buffered(1))


def _row_tile(n, cap=256):
    for t in (256, 176, 128, 64, 32, 16, 8):
        if t <= cap and n % t == 0:
            return t
    return n


def _dot(a, b):
    return jnp.dot(a, b, preferred_element_type=F32)


def _dot_nt(a, b):
    return lax.dot_general(a, b, (((1,), (1,)), ((), ())), preferred_element_type=F32)


def _dot_tn(a, b):
    return lax.dot_general(a, b, (((0,), (0,)), ((), ())), preferred_element_type=F32)


def _sigmoid(x):
    return jax.nn.sigmoid(x)


def _rms_fwd(x, g):
    r = lax.rsqrt(jnp.mean(x * x, axis=-1, keepdims=True) + EPS)
    return x * r * g


def _rms_bwd(x, g, dn):
    r = lax.rsqrt(jnp.mean(x * x, axis=-1, keepdims=True) + EPS)
    xr = x * r
    dg = jnp.sum(dn * xr, axis=0, keepdims=True)
    w = dn * g
    dx = r * w - xr * (r * r) * jnp.mean(x * w, axis=-1, keepdims=True)
    return dx, dg


def _place():
    x, y, c = lax.axis_index("x"), lax.axis_index("y"), lax.axis_index("c")
    chips = [(1 - x, y), (x, 1 - y), (1 - x, 1 - y)]
    return x, y, c, chips


def _quarter_shape(full_shape, kind):
    r, c = full_shape
    return (r // NS, c) if kind == "rows" else (r, c // NS)


def _half_of_quarter(ref, kind, q, pc):
    qr, qc = _quarter_shape(ref.shape, kind)
    h = qr // 2
    if kind == "rows":
        return ref.at[pl.ds(q * qr + pc * h, h), :]
    return ref.at[pl.ds(pc * h, h), pl.ds(q * qc, qc)]


def _quarter(ref, kind, q):
    qr, qc = _quarter_shape(ref.shape, kind)
    if kind == "rows":
        return ref.at[pl.ds(q * qr, qr), :]
    return ref.at[:, pl.ds(q * qc, qc)]


def _rows_half(ref, pc):
    h = ref.shape[0] // 2
    return ref.at[pl.ds(pc * h, h)]


class _Gather:
    def __init__(self, quarters, kinds):
        self.ins = list(quarters)
        self.kinds = list(kinds)
        n = len(self.ins)
        self.out_shape = [jax.ShapeDtypeStruct((NS * a.shape[0], a.shape[1]) if k == "rows" else (a.shape[0], NS * a.shape[1]),
                                               a.dtype) for a, k in zip(self.ins, self.kinds)]
        self.scratch = [pltpu.SemaphoreType.DMA((n, 6)), pltpu.SemaphoreType.DMA((n, 6)), pltpu.SemaphoreType.DMA((n,))]
        self.aliases = {}

    def _copy(self, outs, sems, a, k, q, pc, to, src=None):
        dst = _half_of_quarter(outs[a], self.kinds[a], q, pc)
        return pltpu.make_async_remote_copy(src_ref=dst if src is None else src, dst_ref=dst,
                                            send_sem=sems[0].at[a, k], recv_sem=sems[1].at[a, k],
                                            device_id=to, device_id_type=MESH)

    def _mine(self, ins, outs, sems, a, p):
        return pltpu.make_async_copy(ins[a], _quarter(outs[a], self.kinds[a], p), sems[2].at[a])

    def start(self, ins, outs, sems):
        x, y, c, chips = _place()
        p = 2 * x + y
        for a in range(len(ins)):
            self._mine(ins, outs, sems, a, p).start()
            for j, chip in enumerate(chips):
                self._copy(outs, sems, a, j, p, c, (*chip, c), src=_rows_half(ins[a], c)).start()

    def finish(self, ins, outs, sems):
        x, y, c, chips = _place()
        p = 2 * x + y
        sibling = (x, y, 1 - c)
        n = len(ins)
        for a in range(n):
            for j, (qx, qy) in enumerate(chips):
                q = 2 * qx + qy
                self._copy(outs, sems, a, j, q, c, sibling).wait_recv()
                self._copy(outs, sems, a, 3 + j, q, c, sibling).start()
        for a in range(n):
            for j, (qx, qy) in enumerate(chips):
                q = 2 * qx + qy
                self._copy(outs, sems, a, 3 + j, q, 1 - c, sibling).wait_recv()
                self._copy(outs, sems, a, j, p, c, (qx, qy, c), src=_rows_half(ins[a], c)).wait_send()
                self._copy(outs, sems, a, 3 + j, q, c, sibling).wait_send()
            self._mine(ins, outs, sems, a, p).wait()


class _ToSibling:
    def __init__(self, grads, kinds):
        self.ins = list(grads)
        self.kinds = list(kinds)
        n = len(self.ins)
        self.out_shape = []
        for g, k in zip(self.ins, self.kinds):
            qr, qc = _quarter_shape(g.shape, k)
            self.out_shape.append(jax.ShapeDtypeStruct((NS, qr // 2, qc), g.dtype))
        self.scratch = [pltpu.SemaphoreType.DMA((n, NS)), pltpu.SemaphoreType.DMA((n, NS))]
        self.aliases = {}

    def _copies(self, ins, outs, sems):
        x, y, c, _ = _place()
        return [pltpu.make_async_remote_copy(src_ref=_half_of_quarter(ins[a], self.kinds[a], q, 1 - c), dst_ref=outs[a].at[q],
                                             send_sem=sems[0].at[a, q], recv_sem=sems[1].at[a, q],
                                             device_id=(x, y, 1 - c), device_id_type=MESH)
                for a in range(len(ins)) for q in range(NS)]

    def start(self, ins, outs, sems):
        for cp in self._copies(ins, outs, sems):
            cp.start()

    def finish(self, ins, outs, sems):
        for cp in self._copies(ins, outs, sems):
            cp.wait()


class _ToChips:
    def __init__(self, sums):
        self.ins = list(sums)
        n = len(self.ins)
        self.out_shape = [jax.ShapeDtypeStruct((3,) + s.shape[1:], s.dtype) for s in self.ins]
        self.scratch = [pltpu.SemaphoreType.DMA((n, 3)), pltpu.SemaphoreType.DMA((n, 3))]
        self.aliases = {}

    def _copies(self, ins, outs, sems):
        x, y, c, chips = _place()
        return [pltpu.make_async_remote_copy(src_ref=ins[a].at[2 * qx + qy], dst_ref=outs[a].at[j],
                                             send_sem=sems[0].at[a, j], recv_sem=sems[1].at[a, j],
                                             device_id=(qx, qy, c), device_id_type=MESH)
                for a in range(len(ins)) for j, (qx, qy) in enumerate(chips)]

    def start(self, ins, outs, sems):
        for cp in self._copies(ins, outs, sems):
            cp.start()

    def finish(self, ins, outs, sems):
        for cp in self._copies(ins, outs, sems):
            cp.wait()


class _SwapHalves:
    def __init__(self, quarters):
        self.ins = list(quarters)
        n = len(self.ins)
        self.out_shape = [jax.ShapeDtypeStruct(g.shape, g.dtype) for g in self.ins]
        self.scratch = [pltpu.SemaphoreType.DMA((n,)), pltpu.SemaphoreType.DMA((n,))]
        self.aliases = {a: a for a in range(n)}

    def _copy(self, outs, sems, a, pc):
        x, y, c, _ = _place()
        rows = _rows_half(outs[a], pc)
        return pltpu.make_async_remote_copy(src_ref=rows, dst_ref=rows, send_sem=sems[0].at[a], recv_sem=sems[1].at[a],
                                            device_id=(x, y, 1 - c), device_id_type=MESH)

    def start(self, ins, outs, sems):
        c = lax.axis_index("c")
        for a in range(len(outs)):
            self._copy(outs, sems, a, c).start()

    def finish(self, ins, outs, sems):
        c = lax.axis_index("c")
        for a in range(len(outs)):
            self._copy(outs, sems, a, c).wait_send()
            self._copy(outs, sems, a, 1 - c).wait_recv()


def _call(name, grid, compute, in_specs, out_specs, out_shape, scratch, vmem_mb, args, jobs=()):
    n_in, n_out, n_scr = len(in_specs), len(out_specs), len(scratch)
    ji = [len(j.ins) for j in jobs]
    jo = [len(j.out_shape) for j in jobs]
    js = [len(j.scratch) for j in jobs]

    def body(*refs):
        pos = [0]

        def take(k):
            r = refs[pos[0]:pos[0] + k]
            pos[0] += k
            return r

        ins, jins = take(n_in), [take(k) for k in ji]
        outs, jouts = take(n_out), [take(k) for k in jo]
        scr, jscr = take(n_scr), [take(k) for k in js]
        if jobs and grid:
            ids = [pl.program_id(a) for a in range(len(grid))]
            first = functools.reduce(jnp.logical_and, [i == 0 for i in ids])
            last = functools.reduce(jnp.logical_and, [i == g - 1 for i, g in zip(ids, grid)])

            @pl.when(first)
            def _():
                for j, a, b, c in zip(jobs, jins, jouts, jscr):
                    j.start(a, b, c)
        elif jobs:
            for j, a, b, c in zip(jobs, jins, jouts, jscr):
                j.start(a, b, c)
        compute(ins, outs, scr)
        if jobs and grid:
            @pl.when(last)
            def _():
                for j, a, b, c in zip(jobs, jins, jouts, jscr):
                    j.finish(a, b, c)
        elif jobs:
            for j, a, b, c in zip(jobs, jins, jouts, jscr):
                j.finish(a, b, c)

    aliases = {}
    in_off, out_off = n_in, n_out
    for j, a, b in zip(jobs, ji, jo):
        for s, d in j.aliases.items():
            aliases[in_off + s] = out_off + d
        in_off += a
        out_off += b
    res = pl.pallas_call(
        body, name=name, grid=grid,
        in_specs=list(in_specs) + [ANY] * sum(ji), out_specs=list(out_specs) + [ANY] * sum(jo),
        out_shape=list(out_shape) + [s for j in jobs for s in j.out_shape],
        scratch_shapes=list(scratch) + [s for j in jobs for s in j.scratch],
        input_output_aliases=aliases, compiler_params=_cparams(len(grid), vmem_mb),
    )(*args, *[a for j in jobs for a in j.ins])
    res = list(res)
    main, rest, jres = res[:n_out], res[n_out:], []
    for k in jo:
        jres.append(rest[:k])
        rest = rest[k:]
    return main, jres


def _exchange(name, jobs):
    return _call(name, (), lambda ins, outs, scr: None, [], [], [], [], 16, [], jobs)[1]


def _allreduce_small(v):
    R, C = v.shape
    N = 8

    def body(v_ref, out_ref, gath, send_sems, recv_sems, local_sem):
        x, y, c, chips = _place()
        me, sibling = (x, y, c), (x, y, 1 - c)

        def rows(px, py, pc):
            return gath.at[pl.ds((4 * px + 2 * py + pc) * R, R), :]

        def copy(k, block, to, src=None):
            return pltpu.make_async_remote_copy(src_ref=rows(*block) if src is None else src, dst_ref=rows(*block),
                                                send_sem=send_sems.at[k], recv_sem=recv_sems.at[k],
                                                device_id=to, device_id_type=MESH)

        mine = pltpu.make_async_copy(v_ref, rows(*me), local_sem)
        mine.start()
        first = [copy(0, me, sibling, src=v_ref)]
        first += [copy(1 + j, me, (*chip, c), src=v_ref) for j, chip in enumerate(chips)]
        for cp in first:
            cp.start()
        passed = [copy(4 + j, (*chip, c), sibling) for j, chip in enumerate(chips)]
        for j, chip in enumerate(chips):
            copy(1 + j, (*chip, c), me).wait_recv()
            passed[j].start()
        copy(0, sibling, me).wait_recv()
        for j, chip in enumerate(chips):
            copy(4 + j, (*chip, 1 - c), me).wait_recv()
        for cp in first + passed:
            cp.wait_send()
        mine.wait()
        acc = gath[0:R, :]
        for d in range(1, N):
            acc = acc + gath[d * R:(d + 1) * R, :]
        out_ref[...] = acc

    return pl.pallas_call(
        body, name="allreduce_small",
        in_specs=[pl.BlockSpec(memory_space=pltpu.VMEM)], out_specs=pl.BlockSpec(memory_space=pltpu.VMEM),
        out_shape=jax.ShapeDtypeStruct((R, C), F32),
        scratch_shapes=[pltpu.VMEM((N * R, C), F32), pltpu.SemaphoreType.DMA((7,)), pltpu.SemaphoreType.DMA((7,)),
                        pltpu.SemaphoreType.DMA],
    )(v)


def _ffn_fwd(h, g, wg, wu, wd, name, jobs=()):
    S, D = h.shape
    F = wg.shape[0]
    ts = _tile(S, 512)
    fb = _tile(F, F // 2)
    nf = F // fb

    def compute(ins, outs, scr):
        h_ref, g_ref, wg_ref, wu_ref, wd_ref = ins
        ho_ref, n_ref, gp_ref, up_ref = outs
        nscr, acc = scr
        s = pl.program_id(1)

        @pl.when(s == 0)
        def _():
            n = _rms_fwd(h_ref[...], g_ref[...]).astype(BF16)
            nscr[...] = n
            n_ref[...] = n
            acc[...] = jnp.zeros_like(acc)

        n = nscr[...]
        gp = _dot_nt(n, wg_ref[...])
        up = _dot_nt(n, wu_ref[...])
        gp_ref[...] = gp.astype(BF16)
        up_ref[...] = up.astype(BF16)
        a = (gp * _sigmoid(gp) * up).astype(BF16)
        acc[...] += _dot(a, wd_ref[...])

        @pl.when(s == nf - 1)
        def _():
            ho_ref[...] = h_ref[...] + 0.5 * acc[...]

    tok = pl.BlockSpec((ts, D), lambda i, s: (i, 0))
    wsp = pl.BlockSpec((fb, D), lambda i, s: (s, 0))
    hid = pl.BlockSpec((ts, fb), lambda i, s: (i, s))
    return _call(name, (S // ts, nf), compute,
                 [tok, pl.BlockSpec((1, D), lambda i, s: (0, 0)), wsp, wsp, wsp], [tok, tok, hid, hid],
                 [jax.ShapeDtypeStruct((S, D), F32), jax.ShapeDtypeStruct((S, D), BF16),
                  jax.ShapeDtypeStruct((S, F), BF16), jax.ShapeDtypeStruct((S, F), BF16)],
                 [pltpu.VMEM((ts, D), BF16), pltpu.VMEM((ts, D), F32)], 56, [h, g, wg, wu, wd], jobs)


def _ffn_bwd(dh, h, g, gp, up, wg, wu, wd, name, jobs=()):
    S, D = h.shape
    F = wg.shape[0]
    ts = _tile(S, 1024)
    fb = _tile(F, 256)
    nf = F // fb

    def compute(ins, outs, scr):
        dh_ref, h_ref, g_ref, gp_ref, up_ref, wg_ref, wu_ref, wd_ref = ins
        dhi_ref, dgp_ref, dup_ref, a_ref, do_ref, dg_ref = outs
        doscr, acc = scr
        i = pl.program_id(0)
        s = pl.program_id(1)

        @pl.when(s == 0)
        def _():
            d = (0.5 * dh_ref[...]).astype(BF16)
            doscr[...] = d
            do_ref[...] = d
            acc[...] = jnp.zeros_like(acc)

        @pl.when(jnp.logical_and(i == 0, s == 0))
        def _():
            dg_ref[...] = jnp.zeros_like(dg_ref)

        da = _dot_nt(doscr[...], wd_ref[...])
        gf = gp_ref[...].astype(F32)
        uf = up_ref[...].astype(F32)
        sg = _sigmoid(gf)
        si = gf * sg
        dgp = (da * uf * (sg * (1.0 + gf * (1.0 - sg)))).astype(BF16)
        dup = (da * si).astype(BF16)
        a_ref[...] = (si * uf).astype(BF16)
        dgp_ref[...] = dgp
        dup_ref[...] = dup
        acc[...] += _dot(dgp, wg_ref[...]) + _dot(dup, wu_ref[...])

        @pl.when(s == nf - 1)
        def _():
            dx, dg = _rms_bwd(h_ref[...], g_ref[...], acc[...])
            dhi_ref[...] = dh_ref[...] + dx
            dg_ref[...] += dg

    tok = pl.BlockSpec((ts, D), lambda i, s: (i, 0))
    hid = pl.BlockSpec((ts, fb), lambda i, s: (i, s))
    row = pl.BlockSpec((1, D), lambda i, s: (0, 0))
    wsp = pl.BlockSpec((fb, D), lambda i, s: (s, 0))
    return _call(name, (S // ts, nf), compute, [tok, tok, row, hid, hid, wsp, wsp, wsp], [tok, hid, hid, hid, tok, row],
                 [jax.ShapeDtypeStruct((S, D), F32), jax.ShapeDtypeStruct((S, F), BF16), jax.ShapeDtypeStruct((S, F), BF16),
                  jax.ShapeDtypeStruct((S, F), BF16), jax.ShapeDtypeStruct((S, D), BF16), jax.ShapeDtypeStruct((1, D), F32)],
                 [pltpu.VMEM((ts, D), BF16), pltpu.VMEM((ts, D), F32)], 56, [dh, h, g, gp, up, wg, wu, wd], jobs)


def _ffn_wgrad(hid, tok, name, jobs=()):
    S, D = tok.shape
    F = hid.shape[1]
    fb = _tile(F, F // 2)

    def compute(ins, outs, scr):
        outs[0][...] = _dot_tn(ins[0][...], ins[1][...])

    main, jres = _call(name, (F // fb,), compute,
                       [pl.BlockSpec((S, fb), lambda j: (0, j)), _resident(tok.shape)],
                       [pl.BlockSpec((fb, D), lambda j: (j, 0))], [jax.ShapeDtypeStruct((F, D), F32)], [], 56,
                       [hid, tok], jobs)
    return main[0], jres


def _mix_in_fwd(h, g, win, jobs=()):
    S, D = h.shape
    NG = win.shape[1] // D
    ts = _tile(S, 512)

    def compute(ins, outs, scr):
        h_ref, g_ref, w_ref = ins
        u_ref, z_ref = outs
        u = _rms_fwd(h_ref[...], g_ref[...]).astype(BF16)
        u_ref[...] = u
        for k in range(NG):
            z_ref[k] = _dot(u, w_ref[:, k * D:(k + 1) * D]).astype(BF16)

    return _call("mix_in_fwd", (S // ts,), compute,
                 [pl.BlockSpec((ts, D), lambda i: (i, 0)), pl.BlockSpec((1, D), lambda i: (0, 0)), _resident(win.shape)],
                 [pl.BlockSpec((ts, D), lambda i: (i, 0)), pl.BlockSpec((NG, ts, D), lambda i: (0, i, 0))],
                 [jax.ShapeDtypeStruct((S, D), BF16), jax.ShapeDtypeStruct((NG, S, D), BF16)],
                 [], 48, [h, g, win], jobs)


SUBLANES = 8


def _shifted_copies(s):
    n = s.shape[1] - SUBLANES
    for r in range(1, SUBLANES):
        s[r, 0:n, :] = s[0, r:r + n, :]


def _window(s, o, rows):
    r = o % SUBLANES
    return s[r, o - r:o - r + rows, :]


def _conv_fwd(z, wa, ba, wb, jobs=()):
    _, S, D = z.shape
    _, KA, CB = wa.shape
    KB = wb.shape[1]
    ts = _tile(S, 512)
    r = ts // HALO
    CH = min(64, ts)

    def compute(ins, outs, scr):
        z_ref, zh_ref, wa_ref, ba_ref, wb_ref = ins
        a1_ref, q_ref = outs
        sa, sb = scr
        keep = (pl.program_id(1) > 0).astype(F32)
        sa[0, HALO:HALO + ts, :] = z_ref[0].astype(F32) * _sigmoid(z_ref[1].astype(F32))
        sa[0, 0:HALO, :] = zh_ref[0].astype(F32) * _sigmoid(zh_ref[1].astype(F32)) * keep
        _shifted_copies(sa)
        sb[HALO:HALO + ts, :] = z_ref[3].astype(F32) * z_ref[4].astype(F32)
        sb[0:HALO, :] = zh_ref[3].astype(F32) * zh_ref[4].astype(F32) * keep
        wak = [wa_ref[0, k:k + 1, :] for k in range(KA)]
        wbk = [wb_ref[0, k:k + 1, :] for k in range(KB)]
        for c0 in range(0, ts, CH):
            acc = jnp.broadcast_to(ba_ref[...], (CH, CB))
            for k in range(KA):
                acc = acc + wak[k] * _window(sa, c0 + HALO - (KA - 1) + k, CH)
            a1_ref[c0:c0 + CH, :] = acc
            v = jnp.zeros((CH, CB), F32)
            for k in range(KB):
                o = c0 + HALO - (KB - 1) + k
                v = v + wbk[k] * sb[o:o + CH, :]
            q_ref[c0:c0 + CH, :] = (z_ref[2, c0:c0 + CH, :].astype(F32) * v).astype(BF16)

    return _call("conv_fwd", (D // CB, S // ts), compute,
                 [pl.BlockSpec((5, ts, CB), lambda j, i: (0, i, j)),
                  pl.BlockSpec((5, HALO, CB), lambda j, i: (0, jnp.maximum(i * r - 1, 0), j)),
                  pl.BlockSpec((1, KA, CB), lambda j, i: (j, 0, 0)), pl.BlockSpec((1, CB), lambda j, i: (0, j)),
                  pl.BlockSpec((1, KB, CB), lambda j, i: (j, 0, 0))],
                 [pl.BlockSpec((ts, CB), lambda j, i: (i, j)), pl.BlockSpec((ts, CB), lambda j, i: (i, j))],
                 [jax.ShapeDtypeStruct((S, D), F32), jax.ShapeDtypeStruct((S, D), BF16)],
                 [pltpu.VMEM((SUBLANES, HALO + ts, CB), F32), pltpu.VMEM((HALO + ts, CB), F32)], 32, [z, z, wa, ba, wb], jobs)


def _ln_stats(a1):
    mu = jnp.mean(a1, axis=-1, keepdims=True)
    xc = a1 - mu
    rstd = lax.rsqrt(jnp.mean(xc * xc, axis=-1, keepdims=True) + EPS)
    return xc * rstd, rstd


def _mix_out_fwd(h1, a1, q, z, lng, lnb, wa, wb, wo):
    S, D = h1.shape
    ts = _tile(S, 512)

    def compute(ins, outs, scr):
        h_ref, a1_ref, q_ref, ga_ref, gb_ref, lng_ref, lnb_ref, wa_ref, wb_ref, wo_ref = ins
        h2_ref, a3_ref, m_ref, ya_ref, yb_ref = outs
        xhat, _ = _ln_stats(a1_ref[...])
        a2 = xhat * lng_ref[...] + lnb_ref[...]
        a3 = (a2 * _sigmoid(a2)).astype(BF16)
        a3_ref[...] = a3
        ya = _dot(a3, wa_ref[...])
        yb = _dot(q_ref[...], wb_ref[...])
        ya_ref[...] = ya.astype(BF16)
        yb_ref[...] = yb.astype(BF16)
        m = (_sigmoid(ga_ref[0].astype(F32)) * ya + _sigmoid(gb_ref[0].astype(F32)) * yb).astype(BF16)
        m_ref[...] = m
        h2_ref[...] = h_ref[...] + _dot(m, wo_ref[...])

    tok = pl.BlockSpec((ts, D), lambda i: (i, 0))
    row = pl.BlockSpec((1, D), lambda i: (0, 0))
    mat = _resident((D, D))
    return _call("mix_out_fwd", (S // ts,), compute,
                 [tok, tok, tok, pl.BlockSpec((1, ts, D), lambda i: (5, i, 0)), pl.BlockSpec((1, ts, D), lambda i: (6, i, 0)),
                  row, row, mat, mat, mat], [tok] * 5,
                 [jax.ShapeDtypeStruct((S, D), F32)] + [jax.ShapeDtypeStruct((S, D), BF16)] * 4,
                 [], 56, [h1, a1, q, z, z, lng, lnb, wa, wb, wo])[0]


def _mix_out_bwd(dh2, a1, z, ya, yb, lng, lnb, wa, wb, wo, jobs=()):
    S, D = dh2.shape
    ts = _tile(S, 256)

    def compute(ins, outs, scr):
        dh_ref, a1_ref, ga_ref, gb_ref, ya_ref, yb_ref, lng_ref, lnb_ref, wa_ref, wb_ref, wo_ref = ins
        da1_ref, dq_ref, dga_ref, dgb_ref, dya_ref, dyb_ref, dhb_ref, dlg_ref, dlb_ref = outs

        @pl.when(pl.program_id(0) == 0)
        def _():
            dlg_ref[...] = jnp.zeros_like(dlg_ref)
            dlb_ref[...] = jnp.zeros_like(dlb_ref)

        dhb = dh_ref[...].astype(BF16)
        dhb_ref[...] = dhb
        dm = _dot_nt(dhb, wo_ref[...])
        sa = _sigmoid(ga_ref[0].astype(F32))
        sb = _sigmoid(gb_ref[0].astype(F32))
        dga_ref[...] = (dm * ya_ref[...].astype(F32) * sa * (1.0 - sa)).astype(BF16)
        dgb_ref[...] = (dm * yb_ref[...].astype(F32) * sb * (1.0 - sb)).astype(BF16)
        dya = (sa * dm).astype(BF16)
        dyb = (sb * dm).astype(BF16)
        dya_ref[...] = dya
        dyb_ref[...] = dyb
        dq_ref[...] = _dot_nt(dyb, wb_ref[...])
        da3 = _dot_nt(dya, wa_ref[...])
        xhat, rstd = _ln_stats(a1_ref[...])
        a2 = xhat * lng_ref[...] + lnb_ref[...]
        sg = _sigmoid(a2)
        da2 = da3 * (sg * (1.0 + a2 * (1.0 - sg)))
        dlg_ref[...] += jnp.sum(da2 * xhat, axis=0, keepdims=True)
        dlb_ref[...] += jnp.sum(da2, axis=0, keepdims=True)
        dxh = da2 * lng_ref[...]
        da1_ref[...] = rstd * (dxh - jnp.mean(dxh, axis=-1, keepdims=True)
                               - xhat * jnp.mean(dxh * xhat, axis=-1, keepdims=True))

    tok = pl.BlockSpec((ts, D), lambda i: (i, 0))
    row = pl.BlockSpec((1, D), lambda i: (0, 0))
    mat = _resident((D, D))
    return _call("mix_out_bwd", (S // ts,), compute,
                 [tok, tok, pl.BlockSpec((1, ts, D), lambda i: (5, i, 0)), pl.BlockSpec((1, ts, D), lambda i: (6, i, 0)),
                  tok, tok, row, row, mat, mat, mat], [tok] * 7 + [row, row],
                 [jax.ShapeDtypeStruct((S, D), F32), jax.ShapeDtypeStruct((S, D), F32)]
                 + [jax.ShapeDtypeStruct((S, D), BF16)] * 5 + [jax.ShapeDtypeStruct((1, D), F32)] * 2,
                 [], 56, [dh2, a1, z, z, ya, yb, lng, lnb, wa, wb, wo], jobs)


def _mixer_wgrads(a3, dya, q, dyb, mm, dhb, jobs=()):
    S, D = a3.shape
    tk = _tile(S, 512)

    def compute(ins, outs, scr):
        @pl.when(pl.program_id(0) == 0)
        def _():
            for o in outs:
                o[...] = jnp.zeros_like(o)

        for t in range(3):
            outs[t][...] += _dot_tn(ins[2 * t][...], ins[2 * t + 1][...])

    tok = pl.BlockSpec((tk, D), lambda k: (k, 0))
    return _call("mixer_wgrads", (S // tk,), compute, [tok] * 6, [pl.BlockSpec((D, D), lambda k: (0, 0))] * 3,
                 [jax.ShapeDtypeStruct((D, D), F32)] * 3, [], 56, [a3, dya, q, dyb, mm, dhb], jobs)


def _conv_bwd(z, da1, dq, dga, dgb, wa, wb, jobs=()):
    NG, S, D = z.shape
    _, KA, CB = wa.shape
    KB = wb.shape[1]
    ts = _tile(S, 512)
    r = ts // HALO
    nt = S // ts
    CH = min(64, ts)
    last_halo = S // HALO - 1

    def compute(ins, outs, scr):
        z_ref, zp_ref, zn_ref, da1_ref, da1n_ref, dq_ref, dqn_ref, dga_ref, dgb_ref, wa_ref, wb_ref = ins
        dz_ref, dwa_ref, dba_ref, dwb_ref = outs
        sa0, sd, sp, sv, acca, accb = scr
        i = pl.program_id(1)
        prev = (i > 0).astype(F32)
        nxt = (i < nt - 1).astype(F32)

        @pl.when(i == 0)
        def _():
            acca[...] = jnp.zeros_like(acca)
            accb[...] = jnp.zeros_like(accb)
            dba_ref[...] = jnp.zeros_like(dba_ref)

        sa0[0, HALO:HALO + ts, :] = z_ref[0].astype(F32) * _sigmoid(z_ref[1].astype(F32))
        sa0[0, 0:HALO, :] = zp_ref[0].astype(F32) * _sigmoid(zp_ref[1].astype(F32)) * prev
        _shifted_copies(sa0)
        sp[HALO:HALO + ts, :] = z_ref[3].astype(F32) * z_ref[4].astype(F32)
        sp[0:HALO, :] = zp_ref[3].astype(F32) * zp_ref[4].astype(F32) * prev
        sd[0, 0:ts, :] = da1_ref[...]
        sd[0, ts:ts + HALO, :] = da1n_ref[...] * nxt
        _shifted_copies(sd)
        sv[0:ts, :] = dq_ref[...] * z_ref[2].astype(F32)
        sv[ts:ts + HALO, :] = dqn_ref[...] * zn_ref[2].astype(F32) * nxt
        dba_ref[...] += jnp.sum(da1_ref[...], axis=0, keepdims=True)
        wak = [wa_ref[0, k:k + 1, :] for k in range(KA)]
        wbk = [wb_ref[0, k:k + 1, :] for k in range(KB)]
        for c0 in range(0, ts, CH):
            rows = slice(c0, c0 + CH)
            d1 = sd[0, rows, :]
            da0 = jnp.zeros((CH, CB), F32)
            for k in range(KA):
                da0 = da0 + wak[k] * _window(sd, c0 + (KA - 1) - k, CH)
                a0w = _window(sa0, c0 + HALO - (KA - 1) + k, CH)
                acca[k] += jnp.sum((d1 * a0w).reshape(CH // 8, 8, CB), axis=0)
            val = z_ref[0, rows, :].astype(F32)
            sg = _sigmoid(z_ref[1, rows, :].astype(F32))
            dz_ref[0, rows, :] = (da0 * sg).astype(BF16)
            dz_ref[1, rows, :] = (da0 * val * sg * (1.0 - sg)).astype(BF16)
            dv = sv[rows, :]
            v = jnp.zeros((CH, CB), F32)
            dp = jnp.zeros((CH, CB), F32)
            for k in range(KB):
                o = c0 + HALO - (KB - 1) + k
                pw = sp[o:o + CH, :]
                v = v + wbk[k] * pw
                accb[k] += jnp.sum((dv * pw).reshape(CH // 8, 8, CB), axis=0)
                o = c0 + (KB - 1) - k
                dp = dp + wbk[k] * sv[o:o + CH, :]
            dz_ref[2, rows, :] = (dq_ref[rows, :] * v).astype(BF16)
            dz_ref[3, rows, :] = (dp * z_ref[4, rows, :].astype(F32)).astype(BF16)
            dz_ref[4, rows, :] = (dp * z_ref[3, rows, :].astype(F32)).astype(BF16)
        dz_ref[5] = dga_ref[...]
        dz_ref[6] = dgb_ref[...]

        @pl.when(i == nt - 1)
        def _():
            dwa_ref[0] = jnp.sum(acca[...], axis=1)
            dwb_ref[0] = jnp.sum(accb[...], axis=1)

    zt = pl.BlockSpec((5, ts, CB), lambda j, i: (0, i, j))
    zp = pl.BlockSpec((5, HALO, CB), lambda j, i: (0, jnp.maximum(i * r - 1, 0), j))
    zn = pl.BlockSpec((5, HALO, CB), lambda j, i: (0, jnp.minimum((i + 1) * r, last_halo), j))
    tok = pl.BlockSpec((ts, CB), lambda j, i: (i, j))
    tokn = pl.BlockSpec((HALO, CB), lambda j, i: (jnp.minimum((i + 1) * r, last_halo), j))
    return _call("conv_bwd", (D // CB, nt), compute,
                 [zt, zp, zn, tok, tokn, tok, tokn, tok, tok,
                  pl.BlockSpec((1, KA, CB), lambda j, i: (j, 0, 0)), pl.BlockSpec((1, KB, CB), lambda j, i: (j, 0, 0))],
                 [pl.BlockSpec((NG, ts, CB), lambda j, i: (0, i, j)), pl.BlockSpec((1, KA, CB), lambda j, i: (j, 0, 0)),
                  pl.BlockSpec((1, CB), lambda j, i: (0, j)), pl.BlockSpec((1, KB, CB), lambda j, i: (j, 0, 0))],
                 [jax.ShapeDtypeStruct((NG, S, D), BF16), jax.ShapeDtypeStruct((D // CB, KA, CB), F32),
                  jax.ShapeDtypeStruct((1, D), F32), jax.ShapeDtypeStruct((D // CB, KB, CB), F32)],
                 [pltpu.VMEM((SUBLANES, HALO + ts, CB), F32), pltpu.VMEM((SUBLANES, ts + HALO, CB), F32),
                  pltpu.VMEM((HALO + ts, CB), F32), pltpu.VMEM((ts + HALO, CB), F32),
                  pltpu.VMEM((KA, 8, CB), F32), pltpu.VMEM((KB, 8, CB), F32)],
                 40, [z, z, z, da1, da1, dq, dq, dga, dgb, wa, wb], jobs)


def _mix_in_bwd(dh2, h1, g, dz, win, jobs=()):
    S, D = h1.shape
    NG = dz.shape[0]
    ts = _tile(S, 512)

    def compute(ins, outs, scr):
        dh_ref, h_ref, g_ref, dz_ref, w_ref = ins
        dhi_ref, dg_ref = outs

        @pl.when(pl.program_id(0) == 0)
        def _():
            dg_ref[...] = jnp.zeros_like(dg_ref)

        du = _dot_nt(dz_ref[0], w_ref[:, 0:D])
        for k in range(1, NG):
            du = du + _dot_nt(dz_ref[k], w_ref[:, k * D:(k + 1) * D])
        dx, dg = _rms_bwd(h_ref[...], g_ref[...], du)
        dhi_ref[...] = dh_ref[...] + dx
        dg_ref[...] += dg

    tok = pl.BlockSpec((ts, D), lambda i: (i, 0))
    row = pl.BlockSpec((1, D), lambda i: (0, 0))
    return _call("mix_in_bwd", (S // ts,), compute,
                 [tok, tok, row, pl.BlockSpec((NG, ts, D), lambda i: (0, i, 0)), _resident(win.shape)],
                 [tok, row], [jax.ShapeDtypeStruct((S, D), F32), jax.ShapeDtypeStruct((1, D), F32)],
                 [], 56, [dh2, h1, g, dz, win], jobs)


def _w_in_grad(u, dz, jobs=()):
    S, D = u.shape
    NG = dz.shape[0]

    def compute(ins, outs, scr):
        outs[0][...] = _dot_tn(ins[0][...], ins[1][0])

    return _call("w_in_grad", (NG,), compute,
                 [_resident(u.shape), pl.BlockSpec((1, S, D), lambda j: (j, 0, 0))],
                 [pl.BlockSpec((D, D), lambda j: (0, j))], [jax.ShapeDtypeStruct((D, NG * D), F32)], [], 48, [u, dz], jobs)


def _loss_head(h3, t, g):
    S, D = h3.shape
    ts = _tile(S, 512)

    def compute(ins, outs, scr):
        h_ref, t_ref, g_ref = ins
        dh_ref, dg_ref, loss_ref = outs

        @pl.when(pl.program_id(0) == 0)
        def _():
            dg_ref[...] = jnp.zeros_like(dg_ref)
            loss_ref[...] = jnp.zeros_like(loss_ref)

        x = h_ref[...]
        err = _rms_fwd(x, g_ref[...]) - t_ref[...]
        loss_ref[...] += (0.5 / D) * jnp.sum(err * err)
        dx, dg = _rms_bwd(x, g_ref[...], err * (1.0 / D))
        dh_ref[...] = dx
        dg_ref[...] += dg

    tok = pl.BlockSpec((ts, D), lambda i: (i, 0))
    row = pl.BlockSpec((1, D), lambda i: (0, 0))
    return _call("loss_head", (S // ts,), compute, [tok, tok, row], [tok, row, pl.BlockSpec((8, 128), lambda i: (0, 0))],
                 [jax.ShapeDtypeStruct((S, D), F32), jax.ShapeDtypeStruct((1, D), F32), jax.ShapeDtypeStruct((8, 128), F32)],
                 [], 40, [h3, t, g])[0]


def _chip_sums(place, grads, got, kind, name):
    n = len(grads)
    qr, qc = _quarter_shape(grads[0].shape, kind)
    h = qr // 2
    tr = _row_tile(h)
    nr = h // tr

    def body(pc_ref, *refs):
        g_refs, got_refs, b_refs, f_refs = refs[:n], refs[n:2 * n], refs[2 * n:3 * n], refs[3 * n:]
        own = pl.program_id(1) == pc_ref[0]
        for a in range(n):
            s = g_refs[a][...] + got_refs[a][0]
            b_refs[a][0] = s.astype(BF16)

            @pl.when(own)
            def _():
                f_refs[a][...] = s

    if kind == "rows":
        gspec = pl.BlockSpec((tr, qc), lambda r, q, pc: (q * (2 * nr) + pc[1] * nr + r, 0))
    else:
        gspec = pl.BlockSpec((tr, qc), lambda r, q, pc: (pc[1] * nr + r, q))
    lspec = pl.BlockSpec((1, tr, qc), lambda r, q, pc: (q, r, 0))
    res = pl.pallas_call(
        body, name=name,
        grid_spec=pltpu.PrefetchScalarGridSpec(
            num_scalar_prefetch=1, grid=(nr, NS), in_specs=[gspec] * n + [lspec] * n,
            out_specs=[lspec] * n + [pl.BlockSpec((tr, qc), lambda r, q, pc: (r, 0))] * n),
        out_shape=[jax.ShapeDtypeStruct((NS, h, qc), BF16)] * n + [jax.ShapeDtypeStruct((h, qc), F32)] * n,
        compiler_params=_cparams(2, 48),
    )(place, *grads, *got)
    return res[:n], res[n:]


def _totals(place, own, got, name):
    n = len(own)
    h, qc = own[0].shape
    tr = _row_tile(h)
    nr = h // tr

    def body(pc_ref, *refs):
        own_refs, got_refs, o_refs = refs[:n], refs[n:2 * n], refs[2 * n:]
        for a in range(n):
            g = got_refs[a]
            o_refs[a][...] = ((own_refs[a][...] + g[0].astype(F32)) + g[1].astype(F32)) + g[2].astype(F32)

    return pl.pallas_call(
        body, name=name,
        grid_spec=pltpu.PrefetchScalarGridSpec(
            num_scalar_prefetch=1, grid=(nr,),
            in_specs=[pl.BlockSpec((tr, qc), lambda r, pc: (r, 0))] * n + [pl.BlockSpec((3, tr, qc), lambda r, pc: (0, r, 0))] * n,
            out_specs=[pl.BlockSpec((tr, qc), lambda r, pc: (pc[1] * nr + r, 0))] * n),
        out_shape=[jax.ShapeDtypeStruct((2 * h, qc), F32)] * n,
        compiler_params=_cparams(1, 48),
    )(place, *own, *got)


def _adamw(ws, gs, ms, vs, name):
    n = len(ws)
    R, C = ws[0].shape
    tr = _row_tile(R, (36 << 20) // (7 * 2 * 4 * n * C))
    c1 = 1.0 - ADAM_B1 ** ADAM_STEP
    c2 = 1.0 - ADAM_B2 ** ADAM_STEP

    def body(*refs):
        w_refs, g_refs, m_refs, v_refs = refs[:n], refs[n:2 * n], refs[2 * n:3 * n], refs[3 * n:4 * n]
        d_refs, mo_refs, vo_refs = refs[4 * n:5 * n], refs[5 * n:6 * n], refs[6 * n:]
        for a in range(n):
            gv = g_refs[a][...]
            mn = ADAM_B1 * m_refs[a][...] + (1.0 - ADAM_B1) * gv
            vn = ADAM_B2 * v_refs[a][...] + (1.0 - ADAM_B2) * (gv * gv)
            mo_refs[a][...] = mn
            vo_refs[a][...] = vn
            d_refs[a][...] = -ADAM_LR * ((mn / c1) / (jnp.sqrt(vn / c2) + ADAM_EPS) + ADAM_WD * w_refs[a][...])

    blk = pl.BlockSpec((tr, C), lambda r: (r, 0))
    res = pl.pallas_call(
        body, name=name, grid=(R // tr,),
        in_specs=[blk] * (4 * n), out_specs=[blk] * (3 * n),
        out_shape=[jax.ShapeDtypeStruct((R, C), F32)] * (3 * n),
        compiler_params=_cparams(1, 56),
    )(*ws, *gs, *ms, *vs)
    return res[:n], res[n:2 * n], res[2 * n:]


def kernel(x, ffn1_norm, ffn1_w_gate, ffn1_w_up, ffn1_w_down, mix_norm, w_in, a_dw_w, a_dw_b, a_ln_g, a_ln_b, a_w_out, b_conv_w, b_w_out, w_o, ffn2_norm, ffn2_w_gate, ffn2_w_up, ffn2_w_down, final_norm, loss_target, m_ffn1_norm, m_ffn1_w_gate, m_ffn1_w_up, m_ffn1_w_down, m_mix_norm, m_w_in, m_a_dw_w, m_a_dw_b, m_a_ln_g, m_a_ln_b, m_a_w_out, m_b_conv_w, m_b_w_out, m_w_o, m_ffn2_norm, m_ffn2_w_gate, m_ffn2_w_up, m_ffn2_w_down, m_final_norm, v_ffn1_norm, v_ffn1_w_gate, v_ffn1_w_up, v_ffn1_w_down, v_mix_norm, v_w_in, v_a_dw_w, v_a_dw_b, v_a_ln_g, v_a_ln_b, v_a_w_out, v_b_conv_w, v_b_w_out, v_w_o, v_ffn2_norm, v_ffn2_w_gate, v_ffn2_w_up, v_ffn2_w_down, v_final_norm):
    names = ["ffn1_norm", "ffn1_w_gate", "ffn1_w_up", "ffn1_w_down", "mix_norm", "w_in", "a_dw_w", "a_dw_b", "a_ln_g",
             "a_ln_b", "a_w_out", "b_conv_w", "b_w_out", "w_o", "ffn2_norm", "ffn2_w_gate", "ffn2_w_up", "ffn2_w_down",
             "final_norm"]
    W = dict(zip(names, [ffn1_norm, ffn1_w_gate, ffn1_w_up, ffn1_w_down, mix_norm, w_in, a_dw_w, a_dw_b, a_ln_g, a_ln_b,
                         a_w_out, b_conv_w, b_w_out, w_o, ffn2_norm, ffn2_w_gate, ffn2_w_up, ffn2_w_down, final_norm]))
    M = dict(zip(names, [m_ffn1_norm, m_ffn1_w_gate, m_ffn1_w_up, m_ffn1_w_down, m_mix_norm, m_w_in, m_a_dw_w, m_a_dw_b,
                         m_a_ln_g, m_a_ln_b, m_a_w_out, m_b_conv_w, m_b_w_out, m_w_o, m_ffn2_norm, m_ffn2_w_gate,
                         m_ffn2_w_up, m_ffn2_w_down, m_final_norm]))
    V = dict(zip(names, [v_ffn1_norm, v_ffn1_w_gate, v_ffn1_w_up, v_ffn1_w_down, v_mix_norm, v_w_in, v_a_dw_w, v_a_dw_b,
                         v_a_ln_g, v_a_ln_b, v_a_w_out, v_b_conv_w, v_b_w_out, v_w_o, v_ffn2_norm, v_ffn2_w_gate,
                         v_ffn2_w_up, v_ffn2_w_down, v_final_norm]))
    transposed = ("ffn1_w_gate", "ffn1_w_up", "ffn2_w_gate", "ffn2_w_up")
    vecs = ["ffn1_norm", "mix_norm", "a_dw_b", "a_ln_g", "a_ln_b", "ffn2_norm", "final_norm"]
    ffn1 = ["ffn1_w_gate", "ffn1_w_up", "ffn1_w_down"]
    ffn2 = ["ffn2_w_gate", "ffn2_w_up", "ffn2_w_down"]
    outp = ["a_w_out", "b_w_out", "w_o"]

    S, D = x.shape[1], x.shape[2]
    CB = D // NS
    KA, KB = a_dw_w.shape[1], b_conv_w.shape[1]
    px, py, pc = lax.axis_index("x"), lax.axis_index("y"), lax.axis_index("c")
    chip = 2 * px + py
    place = jnp.stack([chip, pc]).astype(jnp.int32)
    h0 = x.reshape(S, D)
    tgt = loss_target.reshape(S, D)
    row = lambda n: W[n].reshape(1, D)
    pad = lambda a, r: jnp.concatenate([a, jnp.zeros((r - a.shape[0], a.shape[1]), F32)], axis=0)

    def quarter(P, n):
        return jnp.transpose(P[n][0]) if n in transposed else P[n][0]

    def unquarter(a, n):
        return (jnp.transpose(a) if n in transposed else a).reshape(W[n].shape)

    wq = {n: quarter(W, n).astype(BF16) for n in ffn1 + ffn2 + outp + ["w_in"]}

    f1 = _exchange("gather_ffn1", [_Gather([wq[n] for n in ffn1], ["rows"] * 3)])[0]
    g_in = _Gather([wq["w_in"], pad(a_dw_w[0], 32), pad(b_conv_w[0], 16)], ["cols", "rows", "rows"])
    (h1, n1, gp1, up1), ((win, taps_a, taps_b),) = _ffn_fwd(h0, row("ffn1_norm"), *f1, "ffn1_fwd", [g_in])
    wa_taps = taps_a.reshape(NS, 32, CB)[:, :KA]
    wb_taps = taps_b.reshape(NS, 16, CB)[:, :KB]
    g_out = _Gather([wq[n] for n in outp] + [wq["ffn2_w_gate"]], ["rows"] * 4)
    (u, z), ((wa_out, wb_out, wo, f2g),) = _mix_in_fwd(h1, row("mix_norm"), win, [g_out])
    g_f2 = _Gather([wq["ffn2_w_up"], wq["ffn2_w_down"]], ["rows"] * 2)
    (a1, q), ((f2u, f2d),) = _conv_fwd(z, wa_taps, row("a_dw_b"), wb_taps, [g_f2])
    h2, a3, mm, ya, yb = _mix_out_fwd(h1, a1, q, z, row("a_ln_g"), row("a_ln_b"), wa_out, wb_out, wo)
    (h3, n2, gp2, up2), _ = _ffn_fwd(h2, row("ffn2_norm"), f2g, f2u, f2d, "ffn2_fwd")
    dh3, d_final, loss_part = _loss_head(h3, tgt, row("final_norm"))
    loss = lax.psum(loss_part[0, 0], ("x", "y", "c"))

    (dh2, dgp2, dup2, act2, do2, d_ffn2), _ = _ffn_bwd(dh3, h2, row("ffn2_norm"), gp2, up2, f2g, f2u, f2d, "ffn2_bwd")
    g2 = [_ffn_wgrad(dgp2, n2, "ffn2_dwg")[0], _ffn_wgrad(dup2, n2, "ffn2_dwu")[0], _ffn_wgrad(act2, do2, "ffn2_dwd")[0]]
    (da1, dq, dga, dgb, dya, dyb, dh2b, d_lng, d_lnb), (got,) = _mix_out_bwd(
        dh2, a1, z, ya, yb, row("a_ln_g"), row("a_ln_b"), wa_out, wb_out, wo, [_ToSibling(g2, ["rows"] * 3)])
    wire2, own2 = _chip_sums(place, g2, got, "rows", "ffn2_chip_sums")
    (dz, d_wa, d_ba, d_wb), (got,) = _conv_bwd(z, da1, dq, dga, dgb, wa_taps, wb_taps, [_ToChips(wire2)])
    half2 = _totals(place, own2, got, "ffn2_totals")
    (g_win,), (tot2,) = _w_in_grad(u, dz, [_SwapHalves(half2)])
    (dh1, d_mix), (got,) = _mix_in_bwd(dh2, h1, row("mix_norm"), dz, win, [_ToSibling([g_win], ["cols"])])
    wire_in, own_in = _chip_sums(place, [g_win], got, "cols", "w_in_chip_sum")
    (dx, dgp1, dup1, act1, do1, d_ffn1), (got,) = _ffn_bwd(dh1, h0, row("ffn1_norm"), gp1, up1, *f1, "ffn1_bwd",
                                                           [_ToChips(wire_in)])
    half_in = _totals(place, own_in, got, "w_in_total")
    go, (tot_in,) = _mixer_wgrads(a3, dya, q, dyb, mm, dh2b, [_SwapHalves(half_in)])
    g1g, (got_o,) = _ffn_wgrad(dgp1, n1, "ffn1_dwg", [_ToSibling(go, ["rows"] * 3)])
    wire_o, own_o = _chip_sums(place, go, got_o, "rows", "mixer_chip_sums")
    g1u, (got_g, land_o) = _ffn_wgrad(dup1, n1, "ffn1_dwu", [_ToSibling([g1g], ["rows"]), _ToChips(wire_o)])
    wire_g, own_g = _chip_sums(place, [g1g], got_g, "rows", "ffn1_dwg_chip_sum")
    half_o = _totals(place, own_o, land_o, "mixer_totals")
    g1d, (got_u, land_g, tot_o) = _ffn_wgrad(act1, do1, "ffn1_dwd",
                                             [_ToSibling([g1u], ["rows"]), _ToChips(wire_g), _SwapHalves(half_o)])
    wire_u, own_u = _chip_sums(place, [g1u], got_u, "rows", "ffn1_dwu_chip_sum")
    half_g = _totals(place, own_g, land_g, "ffn1_dwg_total")
    got_d, land_u, tot_g = _exchange("tail_exchange_1", [_ToSibling([g1d], ["rows"]), _ToChips(wire_u), _SwapHalves(half_g)])
    wire_d, own_d = _chip_sums(place, [g1d], got_d, "rows", "ffn1_dwd_chip_sum")
    half_u = _totals(place, own_u, land_u, "ffn1_dwu_total")
    land_d, tot_u = _exchange("tail_exchange_2", [_ToChips(wire_d), _SwapHalves(half_u)])
    half_d = _totals(place, own_d, land_d, "ffn1_dwd_total")
    (tot_d,) = _exchange("tail_exchange_3", [_SwapHalves(half_d)])
    tot1 = [tot_g[0], tot_u[0], tot_d[0]]
    totals = dict(zip(ffn2 + ["w_in"] + ffn1 + outp, list(tot2) + list(tot_in) + tot1 + list(tot_o)))

    taps_ga = jnp.transpose(d_wa, (1, 0, 2)).reshape(KA, D)
    taps_gb = jnp.transpose(d_wb, (1, 0, 2)).reshape(KB, D)
    rows_a = -(-KA // 8) * 8
    rows_b = -(-KB // 8) * 8
    vec_grads = {"ffn1_norm": d_ffn1, "mix_norm": d_mix, "a_dw_b": d_ba, "a_ln_g": d_lng, "a_ln_b": d_lnb,
                 "ffn2_norm": d_ffn2, "final_norm": d_final}
    packed = jnp.concatenate([pad(jnp.concatenate([vec_grads[n] for n in vecs], axis=0), 8),
                              pad(taps_ga, rows_a), pad(taps_gb, rows_b)], axis=0)
    small = _allreduce_small(packed)
    g_vecs = small[0:8]
    g_taps = lax.dynamic_slice_in_dim(small[8:], chip * CB, CB, axis=1)

    def pack_vecs(P):
        return pad(jnp.concatenate([P[n].reshape(1, D) for n in vecs], axis=0), 8)

    def pack_taps(P):
        return jnp.concatenate([pad(P["a_dw_w"][0], rows_a), pad(P["b_conv_w"][0], rows_b)], axis=0)

    (dv_,), (mv_,), (vv_,) = _adamw([pack_vecs(W)], [g_vecs], [pack_vecs(M)], [pack_vecs(V)], "adamw_vectors")
    (dt_,), (mt_,), (vt_,) = _adamw([pack_taps(W)], [g_taps], [pack_taps(M)], [pack_taps(V)], "adamw_taps")

    grads, deltas, new_m, new_v = {}, {}, {}, {}
    for group, tag in ((ffn1 + ffn2, "ffn"), (["w_in"], "w_in"), (outp, "mixer")):
        ds, ms, vs = _adamw([quarter(W, n) for n in group], [totals[n] for n in group], [quarter(M, n) for n in group],
                            [quarter(V, n) for n in group], tag + "_adamw")
        for n, d_, m_, v_ in zip(group, ds, ms, vs):
            grads[n], deltas[n], new_m[n], new_v[n] = (unquarter(totals[n], n), unquarter(d_, n), unquarter(m_, n),
                                                       unquarter(v_, n))
    for i, n in enumerate(vecs):
        shp = W[n].shape
        grads[n] = g_vecs[i].reshape(shp)
        deltas[n], new_m[n], new_v[n] = dv_[i].reshape(shp), mv_[i].reshape(shp), vv_[i].reshape(shp)
    for n, lo, k in (("a_dw_w", 0, KA), ("b_conv_w", rows_a, KB)):
        shp = W[n].shape
        grads[n] = g_taps[lo:lo + k].reshape(shp)
        deltas[n], new_m[n], new_v[n] = (dt_[lo:lo + k].reshape(shp), mt_[lo:lo + k].reshape(shp),
                                         vt_[lo:lo + k].reshape(shp))

    return (loss, dx.reshape(x.shape), *[grads[n] for n in names], *[deltas[n] for n in names],
            *[new_m[n] for n in names], *[new_v[n] for n in names])
```

```python
import functools

import jax
import jax.numpy as jnp
from jax import lax
from jax.experimental import pallas as pl
from jax.experimental.pallas import tpu as pltpu

F32 = jnp.float32
BF16 = jnp.bfloat16
EPS = 1e-6
NS = 4
HALO = 32
MESH = pl.DeviceIdType.MESH
ANY = pl.BlockSpec(memory_space=pl.ANY)

ADAM_LR = 0.001
ADAM_B1 = 0.9
ADAM_B2 = 0.999
ADAM_EPS = 1e-08
ADAM_WD = 0.01
ADAM_STEP = 10


def _cparams(n_axes, vmem_mb):
    return pltpu.CompilerParams(dimension_semantics=("arbitrary",) * n_axes, vmem_limit_bytes=vmem_mb << 20)


def _tile(n, t):
    return t if n % t == 0 else n


def _resident(shape):
    return pl.BlockSpec(shape, lambda *_: (0,) * len(shape), pipeline_mode=pl.Buffered(1))


def _row_tile(n, cap=256):
    for t in (256, 176, 128, 64, 32, 16, 8):
        if t <= cap and n % t == 0:
            return t
    return n


def _dot(a, b):
    return jnp.dot(a, b, preferred_element_type=F32)


def _dot_nt(a, b):
    return lax.dot_general(a, b, (((1,), (1,)), ((), ())), preferred_element_type=F32)


def _dot_tn(a, b):
    return lax.dot_general(a, b, (((0,), (0,)), ((), ())), preferred_element_type=F32)


def _sigmoid(x):
    return jax.nn.sigmoid(x)


def _rms_fwd(x, g):
    r = lax.rsqrt(jnp.mean(x * x, axis=-1, keepdims=True) + EPS)
    return x * r * g


def _rms_bwd(x, g, dn):
    r = lax.rsqrt(jnp.mean(x * x, axis=-1, keepdims=True) + EPS)
    xr = x * r
    dg = jnp.sum(dn * xr, axis=0, keepdims=True)
    w = dn * g
    dx = r * w - xr * (r * r) * jnp.mean(x * w, axis=-1, keepdims=True)
    return dx, dg


def _place():
    x, y, c = lax.axis_index("x"), lax.axis_index("y"), lax.axis_index("c")
    chips = [(1 - x, y), (x, 1 - y), (1 - x, 1 - y)]
    return x, y, c, chips


def _quarter_shape(full_shape, kind):
    r, c = full_shape
    return (r // NS, c) if kind == "rows" else (r, c // NS)


def _half_of_quarter(ref, kind, q, pc):
    qr, qc = _quarter_shape(ref.shape, kind)
    h = qr // 2
    if kind == "rows":
        return ref.at[pl.ds(q * qr + pc * h, h), :]
    return ref.at[pl.ds(pc * h, h), pl.ds(q * qc, qc)]


def _quarter(ref, kind, q):
    qr, qc = _quarter_shape(ref.shape, kind)
    if kind == "rows":
        return ref.at[pl.ds(q * qr, qr), :]
    return ref.at[:, pl.ds(q * qc, qc)]


def _rows_half(ref, pc):
    h = ref.shape[0] // 2
    return ref.at[pl.ds(pc * h, h)]


class _Gather:
    def __init__(self, quarters, kinds):
        self.ins = list(quarters)
        self.kinds = list(kinds)
        n = len(self.ins)
        self.out_shape = [jax.ShapeDtypeStruct((NS * a.shape[0], a.shape[1]) if k == "rows" else (a.shape[0], NS * a.shape[1]),
                                               a.dtype) for a, k in zip(self.ins, self.kinds)]
        self.scratch = [pltpu.SemaphoreType.DMA((n, 6)), pltpu.SemaphoreType.DMA((n, 6)), pltpu.SemaphoreType.DMA((n,))]
        self.aliases = {}

    def _copy(self, outs, sems, a, k, q, pc, to, src=None):
        dst = _half_of_quarter(outs[a], self.kinds[a], q, pc)
        return pltpu.make_async_remote_copy(src_ref=dst if src is None else src, dst_ref=dst,
                                            send_sem=sems[0].at[a, k], recv_sem=sems[1].at[a, k],
                                            device_id=to, device_id_type=MESH)

    def _mine(self, ins, outs, sems, a, p):
        return pltpu.make_async_copy(ins[a], _quarter(outs[a], self.kinds[a], p), sems[2].at[a])

    def start(self, ins, outs, sems):
        x, y, c, chips = _place()
        p = 2 * x + y
        for a in range(len(ins)):
            self._mine(ins, outs, sems, a, p).start()
            for j, chip in enumerate(chips):
                self._copy(outs, sems, a, j, p, c, (*chip, c), src=_rows_half(ins[a], c)).start()

    def finish(self, ins, outs, sems):
        x, y, c, chips = _place()
        p = 2 * x + y
        sibling = (x, y, 1 - c)
        n = len(ins)
        for a in range(n):
            for j, (qx, qy) in enumerate(chips):
                q = 2 * qx + qy
                self._copy(outs, sems, a, j, q, c, sibling).wait_recv()
                self._copy(outs, sems, a, 3 + j, q, c, sibling).start()
        for a in range(n):
            for j, (qx, qy) in enumerate(chips):
                q = 2 * qx + qy
                self._copy(outs, sems, a, 3 + j, q, 1 - c, sibling).wait_recv()
                self._copy(outs, sems, a, j, p, c, (qx, qy, c), src=_rows_half(ins[a], c)).wait_send()
                self._copy(outs, sems, a, 3 + j, q, c, sibling).wait_send()
            self._mine(ins, outs, sems, a, p).wait()


class _ToSibling:
    def __init__(self, grads, kinds):
        self.ins = list(grads)
        self.kinds = list(kinds)
        n = len(self.ins)
        self.out_shape = []
        for g, k in zip(self.ins, self.kinds):
            qr, qc = _quarter_shape(g.shape, k)
            self.out_shape.append(jax.ShapeDtypeStruct((NS, qr // 2, qc), g.dtype))
        self.scratch = [pltpu.SemaphoreType.DMA((n, NS)), pltpu.SemaphoreType.DMA((n, NS))]
        self.aliases = {}

    def _copies(self, ins, outs, sems):
        x, y, c, _ = _place()
        return [pltpu.make_async_remote_copy(src_ref=_half_of_quarter(ins[a], self.kinds[a], q, 1 - c), dst_ref=outs[a].at[q],
                                             send_sem=sems[0].at[a, q], recv_sem=sems[1].at[a, q],
                                             device_id=(x, y, 1 - c), device_id_type=MESH)
                for a in range(len(ins)) for q in range(NS)]

    def start(self, ins, outs, sems):
        for cp in self._copies(ins, outs, sems):
            cp.start()

    def finish(self, ins, outs, sems):
        for cp in self._copies(ins, outs, sems):
            cp.wait()


class _ToChips:
    def __init__(self, sums, which=(0, 1, 2)):
        self.ins = list(sums)
        self.which = tuple(which)
        n, m = len(self.ins), len(self.which)
        self.out_shape = [jax.ShapeDtypeStruct((m,) + s.shape[1:], s.dtype) for s in self.ins]
        self.scratch = [pltpu.SemaphoreType.DMA((n, m)), pltpu.SemaphoreType.DMA((n, m))]
        self.aliases = {}

    def _copies(self, ins, outs, sems):
        x, y, c, chips = _place()
        return [pltpu.make_async_remote_copy(src_ref=ins[a].at[2 * chips[j][0] + chips[j][1]], dst_ref=outs[a].at[k],
                                             send_sem=sems[0].at[a, k], recv_sem=sems[1].at[a, k],
                                             device_id=(*chips[j], c), device_id_type=MESH)
                for a in range(len(ins)) for k, j in enumerate(self.which)]

    def start(self, ins, outs, sems):
        for cp in self._copies(ins, outs, sems):
            cp.start()

    def finish(self, ins, outs, sems):
        for cp in self._copies(ins, outs, sems):
            cp.wait()


class _SwapHalves:
    def __init__(self, quarters):
        self.ins = list(quarters)
        n = len(self.ins)
        self.out_shape = [jax.ShapeDtypeStruct(g.shape, g.dtype) for g in self.ins]
        self.scratch = [pltpu.SemaphoreType.DMA((n,)), pltpu.SemaphoreType.DMA((n,))]
        self.aliases = {a: a for a in range(n)}

    def _copy(self, outs, sems, a, pc):
        x, y, c, _ = _place()
        rows = _rows_half(outs[a], pc)
        return pltpu.make_async_remote_copy(src_ref=rows, dst_ref=rows, send_sem=sems[0].at[a], recv_sem=sems[1].at[a],
                                            device_id=(x, y, 1 - c), device_id_type=MESH)

    def start(self, ins, outs, sems):
        c = lax.axis_index("c")
        for a in range(len(outs)):
            self._copy(outs, sems, a, c).start()

    def finish(self, ins, outs, sems):
        c = lax.axis_index("c")
        for a in range(len(outs)):
            self._copy(outs, sems, a, c).wait_send()
            self._copy(outs, sems, a, 1 - c).wait_recv()


def _call(name, grid, compute, in_specs, out_specs, out_shape, scratch, vmem_mb, args, jobs=()):
    n_in, n_out, n_scr = len(in_specs), len(out_specs), len(scratch)
    ji = [len(j.ins) for j in jobs]
    jo = [len(j.out_shape) for j in jobs]
    js = [len(j.scratch) for j in jobs]

    def body(*refs):
        pos = [0]

        def take(k):
            r = refs[pos[0]:pos[0] + k]
            pos[0] += k
            return r

        ins, jins = take(n_in), [take(k) for k in ji]
        outs, jouts = take(n_out), [take(k) for k in jo]
        scr, jscr = take(n_scr), [take(k) for k in js]
        if jobs and grid:
            ids = [pl.program_id(a) for a in range(len(grid))]
            first = functools.reduce(jnp.logical_and, [i == 0 for i in ids])
            last = functools.reduce(jnp.logical_and, [i == g - 1 for i, g in zip(ids, grid)])

            @pl.when(first)
            def _():
                for j, a, b, c in zip(jobs, jins, jouts, jscr):
                    j.start(a, b, c)
        elif jobs:
            for j, a, b, c in zip(jobs, jins, jouts, jscr):
                j.start(a, b, c)
        compute(ins, outs, scr)
        if jobs and grid:
            @pl.when(last)
            def _():
                for j, a, b, c in zip(jobs, jins, jouts, jscr):
                    j.finish(a, b, c)
        elif jobs:
            for j, a, b, c in zip(jobs, jins, jouts, jscr):
                j.finish(a, b, c)

    aliases = {}
    in_off, out_off = n_in, n_out
    for j, a, b in zip(jobs, ji, jo):
        for s, d in j.aliases.items():
            aliases[in_off + s] = out_off + d
        in_off += a
        out_off += b
    res = pl.pallas_call(
        body, name=name, grid=grid,
        in_specs=list(in_specs) + [ANY] * sum(ji), out_specs=list(out_specs) + [ANY] * sum(jo),
        out_shape=list(out_shape) + [s for j in jobs for s in j.out_shape],
        scratch_shapes=list(scratch) + [s for j in jobs for s in j.scratch],
        input_output_aliases=aliases, compiler_params=_cparams(len(grid), vmem_mb),
    )(*args, *[a for j in jobs for a in j.ins])
    res = list(res)
    main, rest, jres = res[:n_out], res[n_out:], []
    for k in jo:
        jres.append(rest[:k])
        rest = rest[k:]
    return main, jres


def _exchange(name, jobs):
    return _call(name, (), lambda ins, outs, scr: None, [], [], [], [], 16, [], jobs)[1]


def _allreduce_small(v):
    R, C = v.shape
    N = 8

    def body(v_ref, out_ref, gath, send_sems, recv_sems, local_sem):
        x, y, c, chips = _place()
        me, sibling = (x, y, c), (x, y, 1 - c)

        def rows(px, py, pc):
            return gath.at[pl.ds((4 * px + 2 * py + pc) * R, R), :]

        def copy(k, block, to, src=None):
            return pltpu.make_async_remote_copy(src_ref=rows(*block) if src is None else src, dst_ref=rows(*block),
                                                send_sem=send_sems.at[k], recv_sem=recv_sems.at[k],
                                                device_id=to, device_id_type=MESH)

        mine = pltpu.make_async_copy(v_ref, rows(*me), local_sem)
        mine.start()
        first = [copy(0, me, sibling, src=v_ref)]
        first += [copy(1 + j, me, (*chip, c), src=v_ref) for j, chip in enumerate(chips)]
        for cp in first:
            cp.start()
        passed = [copy(4 + j, (*chip, c), sibling) for j, chip in enumerate(chips)]
        for j, chip in enumerate(chips):
            copy(1 + j, (*chip, c), me).wait_recv()
            passed[j].start()
        copy(0, sibling, me).wait_recv()
        for j, chip in enumerate(chips):
            copy(4 + j, (*chip, 1 - c), me).wait_recv()
        for cp in first + passed:
            cp.wait_send()
        mine.wait()
        acc = gath[0:R, :]
        for d in range(1, N):
            acc = acc + gath[d * R:(d + 1) * R, :]
        out_ref[...] = acc

    return pl.pallas_call(
        body, name="allreduce_small",
        in_specs=[pl.BlockSpec(memory_space=pltpu.VMEM)], out_specs=pl.BlockSpec(memory_space=pltpu.VMEM),
        out_shape=jax.ShapeDtypeStruct((R, C), F32),
        scratch_shapes=[pltpu.VMEM((N * R, C), F32), pltpu.SemaphoreType.DMA((7,)), pltpu.SemaphoreType.DMA((7,)),
                        pltpu.SemaphoreType.DMA],
    )(v)


def _ffn_fwd(h, g, wg, wu, wd, name, jobs=()):
    S, D = h.shape
    F = wg.shape[0]
    ts = _tile(S, 512)
    fb = _tile(F, F // 2)
    nf = F // fb

    def compute(ins, outs, scr):
        h_ref, g_ref, wg_ref, wu_ref, wd_ref = ins
        ho_ref, n_ref, gp_ref, up_ref = outs
        nscr, acc = scr
        s = pl.program_id(1)

        @pl.when(s == 0)
        def _():
            n = _rms_fwd(h_ref[...], g_ref[...]).astype(BF16)
            nscr[...] = n
            n_ref[...] = n
            acc[...] = jnp.zeros_like(acc)

        n = nscr[...]
        gp = _dot_nt(n, wg_ref[...])
        up = _dot_nt(n, wu_ref[...])
        gp_ref[...] = gp.astype(BF16)
        up_ref[...] = up.astype(BF16)
        a = (gp * _sigmoid(gp) * up).astype(BF16)
        acc[...] += _dot(a, wd_ref[...])

        @pl.when(s == nf - 1)
        def _():
            ho_ref[...] = h_ref[...] + 0.5 * acc[...]

    tok = pl.BlockSpec((ts, D), lambda i, s: (i, 0))
    wsp = pl.BlockSpec((fb, D), lambda i, s: (s, 0))
    hid = pl.BlockSpec((ts, fb), lambda i, s: (i, s))
    return _call(name, (S // ts, nf), compute,
                 [tok, pl.BlockSpec((1, D), lambda i, s: (0, 0)), wsp, wsp, wsp], [tok, tok, hid, hid],
                 [jax.ShapeDtypeStruct((S, D), F32), jax.ShapeDtypeStruct((S, D), BF16),
                  jax.ShapeDtypeStruct((S, F), BF16), jax.ShapeDtypeStruct((S, F), BF16)],
                 [pltpu.VMEM((ts, D), BF16), pltpu.VMEM((ts, D), F32)], 56, [h, g, wg, wu, wd], jobs)


def _ffn_bwd_hidden(dh, gp, up, wd, name, jobs=()):
    S, D = dh.shape
    F = wd.shape[0]
    ts = _tile(S, 512)
    fb = _tile(F, F // 2)

    def compute(ins, outs, scr):
        dh_ref, gp_ref, up_ref, wd_ref = ins
        dgp_ref, dup_ref, a_ref, do_ref = outs

        @pl.when(pl.program_id(1) == 0)
        def _():
            d = (0.5 * dh_ref[...]).astype(BF16)
            scr[0][...] = d
            do_ref[...] = d

        da = _dot_nt(scr[0][...], wd_ref[...])
        gf = gp_ref[...].astype(F32)
        uf = up_ref[...].astype(F32)
        sg = _sigmoid(gf)
        si = gf * sg
        dgp_ref[...] = (da * uf * (sg * (1.0 + gf * (1.0 - sg)))).astype(BF16)
        dup_ref[...] = (da * si).astype(BF16)
        a_ref[...] = (si * uf).astype(BF16)

    tok = pl.BlockSpec((ts, D), lambda i, s: (i, 0))
    hid = pl.BlockSpec((ts, fb), lambda i, s: (i, s))
    return _call(name, (S // ts, F // fb), compute, [tok, hid, hid, pl.BlockSpec((fb, D), lambda i, s: (s, 0))],
                 [hid, hid, hid, tok], [jax.ShapeDtypeStruct((S, F), BF16)] * 3 + [jax.ShapeDtypeStruct((S, D), BF16)],
                 [pltpu.VMEM((ts, D), BF16)], 56, [dh, gp, up, wd], jobs)


def _ffn_bwd_input(dh, h, g, dgp, dup, wg, wu, name, jobs=()):
    S, D = h.shape
    F = wg.shape[0]
    ts = _tile(S, 512)

    def compute(ins, outs, scr):
        dh_ref, h_ref, g_ref, dgp_ref, dup_ref, wg_ref, wu_ref = ins
        dhi_ref, dg_ref = outs

        @pl.when(pl.program_id(0) == 0)
        def _():
            dg_ref[...] = jnp.zeros_like(dg_ref)

        dn = _dot(dgp_ref[...], wg_ref[...]) + _dot(dup_ref[...], wu_ref[...])
        dx, dg = _rms_bwd(h_ref[...], g_ref[...], dn)
        dhi_ref[...] = dh_ref[...] + dx
        dg_ref[...] += dg

    tok = pl.BlockSpec((ts, D), lambda i: (i, 0))
    hid = pl.BlockSpec((ts, F), lambda i: (i, 0))
    row = pl.BlockSpec((1, D), lambda i: (0, 0))
    return _call(name, (S // ts,), compute, [tok, tok, row, hid, hid, _resident((F, D)), _resident((F, D))], [tok, row],
                 [jax.ShapeDtypeStruct((S, D), F32), jax.ShapeDtypeStruct((1, D), F32)], [], 56,
                 [dh, h, g, dgp, dup, wg, wu], jobs)


def _ffn_wgrad(hid, tok, name, jobs=()):
    S, D = tok.shape
    F = hid.shape[1]
    fb = _tile(F, F // 2)

    def compute(ins, outs, scr):
        outs[0][...] = _dot_tn(ins[0][...], ins[1][...])

    main, jres = _call(name, (F // fb,), compute,
                       [pl.BlockSpec((S, fb), lambda j: (0, j)), _resident(tok.shape)],
                       [pl.BlockSpec((fb, D), lambda j: (j, 0))], [jax.ShapeDtypeStruct((F, D), F32)], [], 56,
                       [hid, tok], jobs)
    return main[0], jres


def _mix_in_fwd(h, g, win, jobs=()):
    S, D = h.shape
    NG = win.shape[1] // D
    ts = _tile(S, 512)

    def compute(ins, outs, scr):
        h_ref, g_ref, w_ref = ins
        u_ref, z_ref = outs
        u = _rms_fwd(h_ref[...], g_ref[...]).astype(BF16)
        u_ref[...] = u
        for k in range(NG):
            z_ref[k] = _dot(u, w_ref[:, k * D:(k + 1) * D]).astype(BF16)

    return _call("mix_in_fwd", (S // ts,), compute,
                 [pl.BlockSpec((ts, D), lambda i: (i, 0)), pl.BlockSpec((1, D), lambda i: (0, 0)), _resident(win.shape)],
                 [pl.BlockSpec((ts, D), lambda i: (i, 0)), pl.BlockSpec((NG, ts, D), lambda i: (0, i, 0))],
                 [jax.ShapeDtypeStruct((S, D), BF16), jax.ShapeDtypeStruct((NG, S, D), BF16)],
                 [], 48, [h, g, win], jobs)


SUBLANES = 8


def _shifted_copies(s):
    n = s.shape[1] - SUBLANES
    for r in range(1, SUBLANES):
        s[r, 0:n, :] = s[0, r:r + n, :]


def _window(s, o, rows):
    r = o % SUBLANES
    return s[r, o - r:o - r + rows, :]


def _conv_fwd(z, wa, ba, wb, jobs=()):
    _, S, D = z.shape
    _, KA, CB = wa.shape
    KB = wb.shape[1]
    ts = _tile(S, 512)
    r = ts // HALO
    CH = min(64, ts)

    def compute(ins, outs, scr):
        z_ref, zh_ref, wa_ref, ba_ref, wb_ref = ins
        a1_ref, q_ref = outs
        sa, sb = scr
        keep = (pl.program_id(1) > 0).astype(F32)
        sa[0, HALO:HALO + ts, :] = z_ref[0].astype(F32) * _sigmoid(z_ref[1].astype(F32))
        sa[0, 0:HALO, :] = zh_ref[0].astype(F32) * _sigmoid(zh_ref[1].astype(F32)) * keep
        _shifted_copies(sa)
        sb[HALO:HALO + ts, :] = z_ref[3].astype(F32) * z_ref[4].astype(F32)
        sb[0:HALO, :] = zh_ref[3].astype(F32) * zh_ref[4].astype(F32) * keep
        wak = [wa_ref[0, k:k + 1, :] for k in range(KA)]
        wbk = [wb_ref[0, k:k + 1, :] for k in range(KB)]
        for c0 in range(0, ts, CH):
            acc = jnp.broadcast_to(ba_ref[...], (CH, CB))
            for k in range(KA):
                acc = acc + wak[k] * _window(sa, c0 + HALO - (KA - 1) + k, CH)
            a1_ref[c0:c0 + CH, :] = acc
            v = jnp.zeros((CH, CB), F32)
            for k in range(KB):
                o = c0 + HALO - (KB - 1) + k
                v = v + wbk[k] * sb[o:o + CH, :]
            q_ref[c0:c0 + CH, :] = (z_ref[2, c0:c0 + CH, :].astype(F32) * v).astype(BF16)

    return _call("conv_fwd", (D // CB, S // ts), compute,
                 [pl.BlockSpec((5, ts, CB), lambda j, i: (0, i, j)),
                  pl.BlockSpec((5, HALO, CB), lambda j, i: (0, jnp.maximum(i * r - 1, 0), j)),
                  pl.BlockSpec((1, KA, CB), lambda j, i: (j, 0, 0)), pl.BlockSpec((1, CB), lambda j, i: (0, j)),
                  pl.BlockSpec((1, KB, CB), lambda j, i: (j, 0, 0))],
                 [pl.BlockSpec((ts, CB), lambda j, i: (i, j)), pl.BlockSpec((ts, CB), lambda j, i: (i, j))],
                 [jax.ShapeDtypeStruct((S, D), F32), jax.ShapeDtypeStruct((S, D), BF16)],
                 [pltpu.VMEM((SUBLANES, HALO + ts, CB), F32), pltpu.VMEM((HALO + ts, CB), F32)], 32, [z, z, wa, ba, wb], jobs)


def _ln_stats(a1):
    mu = jnp.mean(a1, axis=-1, keepdims=True)
    xc = a1 - mu
    rstd = lax.rsqrt(jnp.mean(xc * xc, axis=-1, keepdims=True) + EPS)
    return xc * rstd, rstd


def _mix_out_fwd(h1, a1, q, z, lng, lnb, wa, wb, wo):
    S, D = h1.shape
    ts = _tile(S, 512)

    def compute(ins, outs, scr):
        h_ref, a1_ref, q_ref, ga_ref, gb_ref, lng_ref, lnb_ref, wa_ref, wb_ref, wo_ref = ins
        h2_ref, a3_ref, m_ref, ya_ref, yb_ref = outs
        xhat, _ = _ln_stats(a1_ref[...])
        a2 = xhat * lng_ref[...] + lnb_ref[...]
        a3 = (a2 * _sigmoid(a2)).astype(BF16)
        a3_ref[...] = a3
        ya = _dot(a3, wa_ref[...])
        yb = _dot(q_ref[...], wb_ref[...])
        ya_ref[...] = ya.astype(BF16)
        yb_ref[...] = yb.astype(BF16)
        m = (_sigmoid(ga_ref[0].astype(F32)) * ya + _sigmoid(gb_ref[0].astype(F32)) * yb).astype(BF16)
        m_ref[...] = m
        h2_ref[...] = h_ref[...] + _dot(m, wo_ref[...])

    tok = pl.BlockSpec((ts, D), lambda i: (i, 0))
    row = pl.BlockSpec((1, D), lambda i: (0, 0))
    mat = _resident((D, D))
    return _call("mix_out_fwd", (S // ts,), compute,
                 [tok, tok, tok, pl.BlockSpec((1, ts, D), lambda i: (5, i, 0)), pl.BlockSpec((1, ts, D), lambda i: (6, i, 0)),
                  row, row, mat, mat, mat], [tok] * 5,
                 [jax.ShapeDtypeStruct((S, D), F32)] + [jax.ShapeDtypeStruct((S, D), BF16)] * 4,
                 [], 56, [h1, a1, q, z, z, lng, lnb, wa, wb, wo])[0]


def _mix_out_bwd(dh2, a1, z, ya, yb, lng, lnb, wa, wb, wo, jobs=()):
    S, D = dh2.shape
    ts = _tile(S, 512)

    def compute(ins, outs, scr):
        dh_ref, a1_ref, ga_ref, gb_ref, ya_ref, yb_ref, lng_ref, lnb_ref, wa_ref, wb_ref, wo_ref = ins
        da1_ref, dq_ref, dga_ref, dgb_ref, dya_ref, dyb_ref, dhb_ref, dlg_ref, dlb_ref = outs

        @pl.when(pl.program_id(0) == 0)
        def _():
            dlg_ref[...] = jnp.zeros_like(dlg_ref)
            dlb_ref[...] = jnp.zeros_like(dlb_ref)

        dhb = dh_ref[...].astype(BF16)
        dhb_ref[...] = dhb
        dm = _dot_nt(dhb, wo_ref[...])
        sa = _sigmoid(ga_ref[0].astype(F32))
        sb = _sigmoid(gb_ref[0].astype(F32))
        dga_ref[...] = (dm * ya_ref[...].astype(F32) * sa * (1.0 - sa)).astype(BF16)
        dgb_ref[...] = (dm * yb_ref[...].astype(F32) * sb * (1.0 - sb)).astype(BF16)
        dya = (sa * dm).astype(BF16)
        dyb = (sb * dm).astype(BF16)
        dya_ref[...] = dya
        dyb_ref[...] = dyb
        dq_ref[...] = _dot_nt(dyb, wb_ref[...])
        da3 = _dot_nt(dya, wa_ref[...])
        xhat, rstd = _ln_stats(a1_ref[...])
        a2 = xhat * lng_ref[...] + lnb_ref[...]
        sg = _sigmoid(a2)
        da2 = da3 * (sg * (1.0 + a2 * (1.0 - sg)))
        dlg_ref[...] += jnp.sum(da2 * xhat, axis=0, keepdims=True)
        dlb_ref[...] += jnp.sum(da2, axis=0, keepdims=True)
        dxh = da2 * lng_ref[...]
        da1_ref[...] = rstd * (dxh - jnp.mean(dxh, axis=-1, keepdims=True)
                               - xhat * jnp.mean(dxh * xhat, axis=-1, keepdims=True))

    tok = pl.BlockSpec((ts, D), lambda i: (i, 0))
    row = pl.BlockSpec((1, D), lambda i: (0, 0))
    mat = _resident((D, D))
    return _call("mix_out_bwd", (S // ts,), compute,
                 [tok, tok, pl.BlockSpec((1, ts, D), lambda i: (5, i, 0)), pl.BlockSpec((1, ts, D), lambda i: (6, i, 0)),
                  tok, tok, row, row, mat, mat, mat], [tok] * 7 + [row, row],
                 [jax.ShapeDtypeStruct((S, D), F32), jax.ShapeDtypeStruct((S, D), F32)]
                 + [jax.ShapeDtypeStruct((S, D), BF16)] * 5 + [jax.ShapeDtypeStruct((1, D), F32)] * 2,
                 [], 56, [dh2, a1, z, z, ya, yb, lng, lnb, wa, wb, wo], jobs)


def _mixer_wgrads(a3, dya, q, dyb, mm, dhb, jobs=()):
    S, D = a3.shape
    tk = _tile(S, 512)

    def compute(ins, outs, scr):
        @pl.when(pl.program_id(0) == 0)
        def _():
            for o in outs:
                o[...] = jnp.zeros_like(o)

        for t in range(3):
            outs[t][...] += _dot_tn(ins[2 * t][...], ins[2 * t + 1][...])

    tok = pl.BlockSpec((tk, D), lambda k: (k, 0))
    return _call("mixer_wgrads", (S // tk,), compute, [tok] * 6, [pl.BlockSpec((D, D), lambda k: (0, 0))] * 3,
                 [jax.ShapeDtypeStruct((D, D), F32)] * 3, [], 56, [a3, dya, q, dyb, mm, dhb], jobs)


def _conv_bwd(z, da1, dq, dga, dgb, wa, wb, jobs=()):
    NG, S, D = z.shape
    _, KA, CB = wa.shape
    KB = wb.shape[1]
    ts = _tile(S, 512)
    r = ts // HALO
    nt = S // ts
    CH = min(64, ts)
    last_halo = S // HALO - 1

    def compute(ins, outs, scr):
        z_ref, zp_ref, zn_ref, da1_ref, da1n_ref, dq_ref, dqn_ref, dga_ref, dgb_ref, wa_ref, wb_ref = ins
        dz_ref, dwa_ref, dba_ref, dwb_ref = outs
        sa0, sd, sp, sv, acca, accb = scr
        i = pl.program_id(1)
        prev = (i > 0).astype(F32)
        nxt = (i < nt - 1).astype(F32)

        @pl.when(i == 0)
        def _():
            acca[...] = jnp.zeros_like(acca)
            accb[...] = jnp.zeros_like(accb)
            dba_ref[...] = jnp.zeros_like(dba_ref)

        sa0[0, HALO:HALO + ts, :] = z_ref[0].astype(F32) * _sigmoid(z_ref[1].astype(F32))
        sa0[0, 0:HALO, :] = zp_ref[0].astype(F32) * _sigmoid(zp_ref[1].astype(F32)) * prev
        _shifted_copies(sa0)
        sp[HALO:HALO + ts, :] = z_ref[3].astype(F32) * z_ref[4].astype(F32)
        sp[0:HALO, :] = zp_ref[3].astype(F32) * zp_ref[4].astype(F32) * prev
        sd[0, 0:ts, :] = da1_ref[...]
        sd[0, ts:ts + HALO, :] = da1n_ref[...] * nxt
        _shifted_copies(sd)
        sv[0:ts, :] = dq_ref[...] * z_ref[2].astype(F32)
        sv[ts:ts + HALO, :] = dqn_ref[...] * zn_ref[2].astype(F32) * nxt
        dba_ref[...] += jnp.sum(da1_ref[...], axis=0, keepdims=True)
        wak = [wa_ref[0, k:k + 1, :] for k in range(KA)]
        wbk = [wb_ref[0, k:k + 1, :] for k in range(KB)]
        for c0 in range(0, ts, CH):
            rows = slice(c0, c0 + CH)
            d1 = sd[0, rows, :]
            da0 = jnp.zeros((CH, CB), F32)
            for k in range(KA):
                da0 = da0 + wak[k] * _window(sd, c0 + (KA - 1) - k, CH)
                a0w = _window(sa0, c0 + HALO - (KA - 1) + k, CH)
                acca[k] += jnp.sum((d1 * a0w).reshape(CH // 8, 8, CB), axis=0)
            val = z_ref[0, rows, :].astype(F32)
            sg = _sigmoid(z_ref[1, rows, :].astype(F32))
            dz_ref[0, rows, :] = (da0 * sg).astype(BF16)
            dz_ref[1, rows, :] = (da0 * val * sg * (1.0 - sg)).astype(BF16)
            dv = sv[rows, :]
            v = jnp.zeros((CH, CB), F32)
            dp = jnp.zeros((CH, CB), F32)
            for k in range(KB):
                o = c0 + HALO - (KB - 1) + k
                pw = sp[o:o + CH, :]
                v = v + wbk[k] * pw
                accb[k] += jnp.sum((dv * pw).reshape(CH // 8, 8, CB), axis=0)
                o = c0 + (KB - 1) - k
                dp = dp + wbk[k] * sv[o:o + CH, :]
            dz_ref[2, rows, :] = (dq_ref[rows, :] * v).astype(BF16)
            dz_ref[3, rows, :] = (dp * z_ref[4, rows, :].astype(F32)).astype(BF16)
            dz_ref[4, rows, :] = (dp * z_ref[3, rows, :].astype(F32)).astype(BF16)
        dz_ref[5] = dga_ref[...]
        dz_ref[6] = dgb_ref[...]

        @pl.when(i == nt - 1)
        def _():
            dwa_ref[0] = jnp.sum(acca[...], axis=1)
            dwb_ref[0] = jnp.sum(accb[...], axis=1)

    zt = pl.BlockSpec((5, ts, CB), lambda j, i: (0, i, j))
    zp = pl.BlockSpec((5, HALO, CB), lambda j, i: (0, jnp.maximum(i * r - 1, 0), j))
    zn = pl.BlockSpec((5, HALO, CB), lambda j, i: (0, jnp.minimum((i + 1) * r, last_halo), j))
    tok = pl.BlockSpec((ts, CB), lambda j, i: (i, j))
    tokn = pl.BlockSpec((HALO, CB), lambda j, i: (jnp.minimum((i + 1) * r, last_halo), j))
    return _call("conv_bwd", (D // CB, nt), compute,
                 [zt, zp, zn, tok, tokn, tok, tokn, tok, tok,
                  pl.BlockSpec((1, KA, CB), lambda j, i: (j, 0, 0)), pl.BlockSpec((1, KB, CB), lambda j, i: (j, 0, 0))],
                 [pl.BlockSpec((NG, ts, CB), lambda j, i: (0, i, j)), pl.BlockSpec((1, KA, CB), lambda j, i: (j, 0, 0)),
                  pl.BlockSpec((1, CB), lambda j, i: (0, j)), pl.BlockSpec((1, KB, CB), lambda j, i: (j, 0, 0))],
                 [jax.ShapeDtypeStruct((NG, S, D), BF16), jax.ShapeDtypeStruct((D // CB, KA, CB), F32),
                  jax.ShapeDtypeStruct((1, D), F32), jax.ShapeDtypeStruct((D // CB, KB, CB), F32)],
                 [pltpu.VMEM((SUBLANES, HALO + ts, CB), F32), pltpu.VMEM((SUBLANES, ts + HALO, CB), F32),
                  pltpu.VMEM((HALO + ts, CB), F32), pltpu.VMEM((ts + HALO, CB), F32),
                  pltpu.VMEM((KA, 8, CB), F32), pltpu.VMEM((KB, 8, CB), F32)],
                 40, [z, z, z, da1, da1, dq, dq, dga, dgb, wa, wb], jobs)


def _mix_in_bwd(dh2, h1, g, dz, win, jobs=()):
    S, D = h1.shape
    NG = dz.shape[0]
    ts = _tile(S, 512)

    def compute(ins, outs, scr):
        dh_ref, h_ref, g_ref, dz_ref, w_ref = ins
        dhi_ref, dg_ref = outs

        @pl.when(pl.program_id(0) == 0)
        def _():
            dg_ref[...] = jnp.zeros_like(dg_ref)

        du = _dot_nt(dz_ref[0], w_ref[:, 0:D])
        for k in range(1, NG):
            du = du + _dot_nt(dz_ref[k], w_ref[:, k * D:(k + 1) * D])
        dx, dg = _rms_bwd(h_ref[...], g_ref[...], du)
        dhi_ref[...] = dh_ref[...] + dx
        dg_ref[...] += dg

    tok = pl.BlockSpec((ts, D), lambda i: (i, 0))
    row = pl.BlockSpec((1, D), lambda i: (0, 0))
    return _call("mix_in_bwd", (S // ts,), compute,
                 [tok, tok, row, pl.BlockSpec((NG, ts, D), lambda i: (0, i, 0)), _resident(win.shape)],
                 [tok, row], [jax.ShapeDtypeStruct((S, D), F32), jax.ShapeDtypeStruct((1, D), F32)],
                 [], 56, [dh2, h1, g, dz, win], jobs)


def _w_in_grad(u, dz, jobs=()):
    S, D = u.shape
    NG = dz.shape[0]

    def compute(ins, outs, scr):
        outs[0][...] = _dot_tn(ins[0][...], ins[1][0])

    return _call("w_in_grad", (NG,), compute,
                 [_resident(u.shape), pl.BlockSpec((1, S, D), lambda j: (j, 0, 0))],
                 [pl.BlockSpec((D, D), lambda j: (0, j))], [jax.ShapeDtypeStruct((D, NG * D), F32)], [], 48, [u, dz], jobs)


def _loss_head(h3, t, g):
    S, D = h3.shape
    ts = _tile(S, 512)

    def compute(ins, outs, scr):
        h_ref, t_ref, g_ref = ins
        dh_ref, dg_ref, loss_ref = outs

        @pl.when(pl.program_id(0) == 0)
        def _():
            dg_ref[...] = jnp.zeros_like(dg_ref)
            loss_ref[...] = jnp.zeros_like(loss_ref)

        x = h_ref[...]
        err = _rms_fwd(x, g_ref[...]) - t_ref[...]
        loss_ref[...] += (0.5 / D) * jnp.sum(err * err)
        dx, dg = _rms_bwd(x, g_ref[...], err * (1.0 / D))
        dh_ref[...] = dx
        dg_ref[...] += dg

    tok = pl.BlockSpec((ts, D), lambda i: (i, 0))
    row = pl.BlockSpec((1, D), lambda i: (0, 0))
    return _call("loss_head", (S // ts,), compute, [tok, tok, row], [tok, row, pl.BlockSpec((8, 128), lambda i: (0, 0))],
                 [jax.ShapeDtypeStruct((S, D), F32), jax.ShapeDtypeStruct((1, D), F32), jax.ShapeDtypeStruct((8, 128), F32)],
                 [], 40, [h3, t, g])[0]


def _chip_sums(place, grads, got, kind, name):
    n = len(grads)
    qr, qc = _quarter_shape(grads[0].shape, kind)
    h = qr // 2
    tr = _row_tile(h)
    nr = h // tr

    def body(pc_ref, *refs):
        g_refs, got_refs, b_refs, f_refs = refs[:n], refs[n:2 * n], refs[2 * n:3 * n], refs[3 * n:]
        own = pl.program_id(1) == pc_ref[0]
        for a in range(n):
            s = g_refs[a][...] + got_refs[a][0]
            b_refs[a][0] = s.astype(BF16)

            @pl.when(own)
            def _():
                f_refs[a][...] = s

    if kind == "rows":
        gspec = pl.BlockSpec((tr, qc), lambda r, q, pc: (q * (2 * nr) + pc[1] * nr + r, 0))
    else:
        gspec = pl.BlockSpec((tr, qc), lambda r, q, pc: (pc[1] * nr + r, q))
    lspec = pl.BlockSpec((1, tr, qc), lambda r, q, pc: (q, r, 0))
    res = pl.pallas_call(
        body, name=name,
        grid_spec=pltpu.PrefetchScalarGridSpec(
            num_scalar_prefetch=1, grid=(nr, NS), in_specs=[gspec] * n + [lspec] * n,
            out_specs=[lspec] * n + [pl.BlockSpec((tr, qc), lambda r, q, pc: (r, 0))] * n),
        out_shape=[jax.ShapeDtypeStruct((NS, h, qc), BF16)] * n + [jax.ShapeDtypeStruct((h, qc), F32)] * n,
        compiler_params=_cparams(2, 48),
    )(place, *grads, *got)
    return res[:n], res[n:]


def _totals(place, own, got, name):
    n = len(own)
    h, qc = own[0].shape
    tr = _row_tile(h)
    nr = h // tr
    got = [list(g) if isinstance(g, (list, tuple)) else [g] for g in got]
    m = len(got[0])

    def body(pc_ref, *refs):
        own_refs, got_refs, o_refs = refs[:n], refs[n:n + n * m], refs[n + n * m:]
        for a in range(n):
            acc = own_refs[a][...]
            for g in got_refs[a * m:(a + 1) * m]:
                for k in range(g.shape[0]):
                    acc = acc + g[k].astype(F32)
            o_refs[a][...] = acc

    lands = [pl.BlockSpec((g.shape[0], tr, qc), lambda r, pc: (0, r, 0)) for gs in got for g in gs]
    return pl.pallas_call(
        body, name=name,
        grid_spec=pltpu.PrefetchScalarGridSpec(
            num_scalar_prefetch=1, grid=(nr,),
            in_specs=[pl.BlockSpec((tr, qc), lambda r, pc: (r, 0))] * n + lands,
            out_specs=[pl.BlockSpec((tr, qc), lambda r, pc: (pc[1] * nr + r, 0))] * n),
        out_shape=[jax.ShapeDtypeStruct((2 * h, qc), F32)] * n,
        compiler_params=_cparams(1, 48),
    )(place, *own, *[g for gs in got for g in gs])


def _adamw(ws, gs, ms, vs, name):
    n = len(ws)
    R, C = ws[0].shape
    tr = _row_tile(R, (36 << 20) // (7 * 2 * 4 * n * C))
    c1 = 1.0 - ADAM_B1 ** ADAM_STEP
    c2 = 1.0 - ADAM_B2 ** ADAM_STEP

    def body(*refs):
        w_refs, g_refs, m_refs, v_refs = refs[:n], refs[n:2 * n], refs[2 * n:3 * n], refs[3 * n:4 * n]
        d_refs, mo_refs, vo_refs = refs[4 * n:5 * n], refs[5 * n:6 * n], refs[6 * n:]
        for a in range(n):
            gv = g_refs[a][...]
            mn = ADAM_B1 * m_refs[a][...] + (1.0 - ADAM_B1) * gv
            vn = ADAM_B2 * v_refs[a][...] + (1.0 - ADAM_B2) * (gv * gv)
            mo_refs[a][...] = mn
            vo_refs[a][...] = vn
            d_refs[a][...] = -ADAM_LR * ((mn / c1) / (jnp.sqrt(vn / c2) + ADAM_EPS) + ADAM_WD * w_refs[a][...])

    blk = pl.BlockSpec((tr, C), lambda r: (r, 0))
    res = pl.pallas_call(
        body, name=name, grid=(R // tr,),
        in_specs=[blk] * (4 * n), out_specs=[blk] * (3 * n),
        out_shape=[jax.ShapeDtypeStruct((R, C), F32)] * (3 * n),
        compiler_params=_cparams(1, 56),
    )(*ws, *gs, *ms, *vs)
    return res[:n], res[n:2 * n], res[2 * n:]


def kernel(x, ffn1_norm, ffn1_w_gate, ffn1_w_up, ffn1_w_down, mix_norm, w_in, a_dw_w, a_dw_b, a_ln_g, a_ln_b, a_w_out, b_conv_w, b_w_out, w_o, ffn2_norm, ffn2_w_gate, ffn2_w_up, ffn2_w_down, final_norm, loss_target, m_ffn1_norm, m_ffn1_w_gate, m_ffn1_w_up, m_ffn1_w_down, m_mix_norm, m_w_in, m_a_dw_w, m_a_dw_b, m_a_ln_g, m_a_ln_b, m_a_w_out, m_b_conv_w, m_b_w_out, m_w_o, m_ffn2_norm, m_ffn2_w_gate, m_ffn2_w_up, m_ffn2_w_down, m_final_norm, v_ffn1_norm, v_ffn1_w_gate, v_ffn1_w_up, v_ffn1_w_down, v_mix_norm, v_w_in, v_a_dw_w, v_a_dw_b, v_a_ln_g, v_a_ln_b, v_a_w_out, v_b_conv_w, v_b_w_out, v_w_o, v_ffn2_norm, v_ffn2_w_gate, v_ffn2_w_up, v_ffn2_w_down, v_final_norm):
    names = ["ffn1_norm", "ffn1_w_gate", "ffn1_w_up", "ffn1_w_down", "mix_norm", "w_in", "a_dw_w", "a_dw_b", "a_ln_g",
             "a_ln_b", "a_w_out", "b_conv_w", "b_w_out", "w_o", "ffn2_norm", "ffn2_w_gate", "ffn2_w_up", "ffn2_w_down",
             "final_norm"]
    W = dict(zip(names, [ffn1_norm, ffn1_w_gate, ffn1_w_up, ffn1_w_down, mix_norm, w_in, a_dw_w, a_dw_b, a_ln_g, a_ln_b,
                         a_w_out, b_conv_w, b_w_out, w_o, ffn2_norm, ffn2_w_gate, ffn2_w_up, ffn2_w_down, final_norm]))
    M = dict(zip(names, [m_ffn1_norm, m_ffn1_w_gate, m_ffn1_w_up, m_ffn1_w_down, m_mix_norm, m_w_in, m_a_dw_w, m_a_dw_b,
                         m_a_ln_g, m_a_ln_b, m_a_w_out, m_b_conv_w, m_b_w_out, m_w_o, m_ffn2_norm, m_ffn2_w_gate,
                         m_ffn2_w_up, m_ffn2_w_down, m_final_norm]))
    V = dict(zip(names, [v_ffn1_norm, v_ffn1_w_gate, v_ffn1_w_up, v_ffn1_w_down, v_mix_norm, v_w_in, v_a_dw_w, v_a_dw_b,
                         v_a_ln_g, v_a_ln_b, v_a_w_out, v_b_conv_w, v_b_w_out, v_w_o, v_ffn2_norm, v_ffn2_w_gate,
                         v_ffn2_w_up, v_ffn2_w_down, v_final_norm]))
    transposed = ("ffn1_w_gate", "ffn1_w_up", "ffn2_w_gate", "ffn2_w_up")
    vecs = ["ffn1_norm", "mix_norm", "a_dw_b", "a_ln_g", "a_ln_b", "ffn2_norm", "final_norm"]
    ffn1 = ["ffn1_w_gate", "ffn1_w_up", "ffn1_w_down"]
    ffn2 = ["ffn2_w_gate", "ffn2_w_up", "ffn2_w_down"]
    outp = ["a_w_out", "b_w_out", "w_o"]

    S, D = x.shape[1], x.shape[2]
    CB = D // NS
    KA, KB = a_dw_w.shape[1], b_conv_w.shape[1]
    px, py, pc = lax.axis_index("x"), lax.axis_index("y"), lax.axis_index("c")
    chip = 2 * px + py
    place = jnp.stack([chip, pc]).astype(jnp.int32)
    h0 = x.reshape(S, D)
    tgt = loss_target.reshape(S, D)
    row = lambda n: W[n].reshape(1, D)
    pad = lambda a, r: jnp.concatenate([a, jnp.zeros((r - a.shape[0], a.shape[1]), F32)], axis=0)

    def quarter(P, n):
        return jnp.transpose(P[n][0]) if n in transposed else P[n][0]

    def unquarter(a, n):
        return (jnp.transpose(a) if n in transposed else a).reshape(W[n].shape)

    wq = {n: quarter(W, n).astype(BF16) for n in ffn1 + ffn2 + outp + ["w_in"]}

    f1 = _exchange("gather_ffn1", [_Gather([wq[n] for n in ffn1], ["rows"] * 3)])[0]
    g_in = _Gather([wq["w_in"], pad(a_dw_w[0], 32), pad(b_conv_w[0], 16)], ["cols", "rows", "rows"])
    (h1, n1, gp1, up1), ((win, taps_a, taps_b),) = _ffn_fwd(h0, row("ffn1_norm"), *f1, "ffn1_fwd", [g_in])
    wa_taps = taps_a.reshape(NS, 32, CB)[:, :KA]
    wb_taps = taps_b.reshape(NS, 16, CB)[:, :KB]
    g_out = _Gather([wq[n] for n in outp] + [wq["ffn2_w_gate"]], ["rows"] * 4)
    (u, z), ((wa_out, wb_out, wo, f2g),) = _mix_in_fwd(h1, row("mix_norm"), win, [g_out])
    g_f2 = _Gather([wq["ffn2_w_up"], wq["ffn2_w_down"]], ["rows"] * 2)
    (a1, q), ((f2u, f2d),) = _conv_fwd(z, wa_taps, row("a_dw_b"), wb_taps, [g_f2])
    h2, a3, mm, ya, yb = _mix_out_fwd(h1, a1, q, z, row("a_ln_g"), row("a_ln_b"), wa_out, wb_out, wo)
    (h3, n2, gp2, up2), _ = _ffn_fwd(h2, row("ffn2_norm"), f2g, f2u, f2d, "ffn2_fwd")
    dh3, d_final, loss_part = _loss_head(h3, tgt, row("final_norm"))
    loss = lax.psum(loss_part[0, 0], ("x", "y", "c"))

    (dgp2, dup2, act2, do2), _ = _ffn_bwd_hidden(dh3, gp2, up2, f2d, "ffn2_bwd_hidden")
    (dh2, d_ffn2), _ = _ffn_bwd_input(dh3, h2, row("ffn2_norm"), dgp2, dup2, f2g, f2u, "ffn2_bwd_input")
    g2 = [_ffn_wgrad(dgp2, n2, "ffn2_dwg")[0], _ffn_wgrad(dup2, n2, "ffn2_dwu")[0], _ffn_wgrad(act2, do2, "ffn2_dwd")[0]]
    (da1, dq, dga, dgb, dya, dyb, dh2b, d_lng, d_lnb), (got,) = _mix_out_bwd(
        dh2, a1, z, ya, yb, row("a_ln_g"), row("a_ln_b"), wa_out, wb_out, wo, [_ToSibling(g2, ["rows"] * 3)])
    wire2, own2 = _chip_sums(place, g2, got, "rows", "ffn2_chip_sums")
    (dz, d_wa, d_ba, d_wb), (got,) = _conv_bwd(z, da1, dq, dga, dgb, wa_taps, wb_taps, [_ToChips(wire2)])
    half2 = _totals(place, own2, got, "ffn2_totals")
    (g_win,), (tot2,) = _w_in_grad(u, dz, [_SwapHalves(half2)])
    (dh1, d_mix), (got,) = _mix_in_bwd(dh2, h1, row("mix_norm"), dz, win, [_ToSibling([g_win], ["cols"])])
    wire_in, own_in = _chip_sums(place, [g_win], got, "cols", "w_in_chip_sum")
    (dgp1, dup1, act1, do1), (near_in,) = _ffn_bwd_hidden(dh1, gp1, up1, f1[2], "ffn1_bwd_hidden",
                                                          [_ToChips(wire_in, (0, 1))])
    (dx, d_ffn1), _ = _ffn_bwd_input(dh1, h0, row("ffn1_norm"), dgp1, dup1, f1[0], f1[1], "ffn1_bwd_input")
    go, (far_in,) = _mixer_wgrads(a3, dya, q, dyb, mm, dh2b, [_ToChips(wire_in, (2,))])
    half_in = _totals(place, own_in, [[near_in[0], far_in[0]]], "w_in_total")
    g1g, (tot_in, got_o) = _ffn_wgrad(dgp1, n1, "ffn1_dwg", [_SwapHalves(half_in), _ToSibling(go, ["rows"] * 3)])
    wire_o, own_o = _chip_sums(place, go, got_o, "rows", "mixer_chip_sums")
    g1u, (got_g, land_o) = _ffn_wgrad(dup1, n1, "ffn1_dwu", [_ToSibling([g1g], ["rows"]), _ToChips(wire_o)])
    wire_g, own_g = _chip_sums(place, [g1g], got_g, "rows", "ffn1_dwg_chip_sum")
    half_o = _totals(place, own_o, land_o, "mixer_totals")
    g1d, (got_u, land_g, tot_o) = _ffn_wgrad(act1, do1, "ffn1_dwd",
                                             [_ToSibling([g1u], ["rows"]), _ToChips(wire_g), _SwapHalves(half_o)])
    wire_u, own_u = _chip_sums(place, [g1u], got_u, "rows", "ffn1_dwu_chip_sum")
    half_g = _totals(place, own_g, land_g, "ffn1_dwg_total")
    got_d, land_u, tot_g = _exchange("tail_exchange_1", [_ToSibling([g1d], ["rows"]), _ToChips(wire_u), _SwapHalves(half_g)])
    wire_d, own_d = _chip_sums(place, [g1d], got_d, "rows", "ffn1_dwd_chip_sum")
    half_u = _totals(place, own_u, land_u, "ffn1_dwu_total")
    land_d, tot_u = _exchange("tail_exchange_2", [_ToChips(wire_d), _SwapHalves(half_u)])
    half_d = _totals(place, own_d, land_d, "ffn1_dwd_total")
    (tot_d,) = _exchange("tail_exchange_3", [_SwapHalves(half_d)])
    tot1 = [tot_g[0], tot_u[0], tot_d[0]]
    totals = dict(zip(ffn2 + ["w_in"] + ffn1 + outp, list(tot2) + list(tot_in) + tot1 + list(tot_o)))

    taps_ga = jnp.transpose(d_wa, (1, 0, 2)).reshape(KA, D)
    taps_gb = jnp.transpose(d_wb, (1, 0, 2)).reshape(KB, D)
    rows_a = -(-KA // 8) * 8
    rows_b = -(-KB // 8) * 8
    vec_grads = {"ffn1_norm": d_ffn1, "mix_norm": d_mix, "a_dw_b": d_ba, "a_ln_g": d_lng, "a_ln_b": d_lnb,
                 "ffn2_norm": d_ffn2, "final_norm": d_final}
    packed = jnp.concatenate([pad(jnp.concatenate([vec_grads[n] for n in vecs], axis=0), 8),
                              pad(taps_ga, rows_a), pad(taps_gb, rows_b)], axis=0)
    small = _allreduce_small(packed)
    g_vecs = small[0:8]
    g_taps = lax.dynamic_slice_in_dim(small[8:], chip * CB, CB, axis=1)

    def pack_vecs(P):
        return pad(jnp.concatenate([P[n].reshape(1, D) for n in vecs], axis=0), 8)

    def pack_taps(P):
        return jnp.concatenate([pad(P["a_dw_w"][0], rows_a), pad(P["b_conv_w"][0], rows_b)], axis=0)

    (dv_,), (mv_,), (vv_,) = _adamw([pack_vecs(W)], [g_vecs], [pack_vecs(M)], [pack_vecs(V)], "adamw_vectors")
    (dt_,), (mt_,), (vt_,) = _adamw([pack_taps(W)], [g_taps], [pack_taps(M)], [pack_taps(V)], "adamw_taps")

    grads, deltas, new_m, new_v = {}, {}, {}, {}
    for group, tag in ((ffn1 + ffn2, "ffn"), (["w_in"], "w_in"), (outp, "mixer")):
        ds, ms, vs = _adamw([quarter(W, n) for n in group], [totals[n] for n in group], [quarter(M, n) for n in group],
                            [quarter(V, n) for n in group], tag + "_adamw")
        for n, d_, m_, v_ in zip(group, ds, ms, vs):
            grads[n], deltas[n], new_m[n], new_v[n] = (unquarter(totals[n], n), unquarter(d_, n), unquarter(m_, n),
                                                       unquarter(v_, n))
    for i, n in enumerate(vecs):
        shp = W[n].shape
        grads[n] = g_vecs[i].reshape(shp)
        deltas[n], new_m[n], new_v[n] = dv_[i].reshape(shp), mv_[i].reshape(shp), vv_[i].reshape(shp)
    for n, lo, k in (("a_dw_w", 0, KA), ("b_conv_w", rows_a, KB)):
        shp = W[n].shape
        grads[n] = g_taps[lo:lo + k].reshape(shp)
        deltas[n], new_m[n], new_v[n] = (dt_[lo:lo + k].reshape(shp), mt_[lo:lo + k].reshape(shp),
                                         vt_[lo:lo + k].reshape(shp))

    return (loss, dx.reshape(x.shape), *[grads[n] for n in names], *[deltas[n] for n in names],
            *[new_m[n] for n in names], *[new_v[n] for n in names])
```

```python
import functools

import jax
import jax.numpy as jnp
from jax import lax
from jax.experimental import pallas as pl
from jax.experimental.pallas import tpu as pltpu

F32 = jnp.float32
BF16 = jnp.bfloat16
EPS = 1e-6
NS = 4
HALO = 32
MESH = pl.DeviceIdType.MESH
IN_HBM = pl.BlockSpec(memory_space=pltpu.HBM)

ADAM_LR = 0.001
ADAM_B1 = 0.9
ADAM_B2 = 0.999
ADAM_EPS = 1e-08
ADAM_WD = 0.01
ADAM_STEP = 10


def _cparams(n_axes, vmem_mb):
    return pltpu.CompilerParams(dimension_semantics=("arbitrary",) * n_axes, vmem_limit_bytes=vmem_mb << 20)


def _tile(n, t):
    return t if n % t == 0 else n


def _resident(shape):
    return pl.BlockSpec(shape, lambda *_: (0,) * len(shape), pipeline_mode=pl.Buffered(1))


def _row_tile(n, cap=256):
    for t in (256, 176, 128, 64, 32, 16, 8):
        if t <= cap and n % t == 0:
            return t
    return n


def _dot(a, b):
    return jnp.dot(a, b, preferred_element_type=F32)


def _dot_nt(a, b):
    return lax.dot_general(a, b, (((1,), (1,)), ((), ())), preferred_element_type=F32)


def _dot_tn(a, b):
    return lax.dot_general(a, b, (((0,), (0,)), ((), ())), preferred_element_type=F32)


def _sigmoid(x):
    return jax.nn.sigmoid(x)


def _rms_fwd(x, g):
    r = lax.rsqrt(jnp.mean(x * x, axis=-1, keepdims=True) + EPS)
    return x * r * g


def _rms_bwd(x, g, dn):
    r = lax.rsqrt(jnp.mean(x * x, axis=-1, keepdims=True) + EPS)
    xr = x * r
    dg = jnp.sum(dn * xr, axis=0, keepdims=True)
    w = dn * g
    dx = r * w - xr * (r * r) * jnp.mean(x * w, axis=-1, keepdims=True)
    return dx, dg


def _place():
    x, y, c = lax.axis_index("x"), lax.axis_index("y"), lax.axis_index("c")
    chips = [(1 - x, y), (x, 1 - y), (1 - x, 1 - y)]
    return x, y, c, chips


def _quarter_shape(full_shape, kind):
    r, c = full_shape
    return (r // NS, c) if kind == "rows" else (r, c // NS)


def _half_of_quarter(ref, kind, q, pc):
    qr, qc = _quarter_shape(ref.shape, kind)
    h = qr // 2
    if kind == "rows":
        return ref.at[pl.ds(q * qr + pc * h, h), :]
    return ref.at[pl.ds(pc * h, h), pl.ds(q * qc, qc)]


def _quarter(ref, kind, q):
    qr, qc = _quarter_shape(ref.shape, kind)
    if kind == "rows":
        return ref.at[pl.ds(q * qr, qr), :]
    return ref.at[:, pl.ds(q * qc, qc)]


def _rows_half(ref, pc):
    h = ref.shape[0] // 2
    return ref.at[pl.ds(pc * h, h)]


class _Gather:
    def __init__(self, quarters, kinds):
        self.ins = list(quarters)
        self.kinds = list(kinds)
        n = len(self.ins)
        self.out_shape = [jax.ShapeDtypeStruct((NS * a.shape[0], a.shape[1]) if k == "rows" else (a.shape[0], NS * a.shape[1]),
                                               a.dtype) for a, k in zip(self.ins, self.kinds)]
        self.scratch = [pltpu.SemaphoreType.DMA((n, 6)), pltpu.SemaphoreType.DMA((n, 6)), pltpu.SemaphoreType.DMA((n,))]
        self.aliases = {}

    def _copy(self, outs, sems, a, k, q, pc, to, src=None):
        dst = _half_of_quarter(outs[a], self.kinds[a], q, pc)
        return pltpu.make_async_remote_copy(src_ref=dst if src is None else src, dst_ref=dst,
                                            send_sem=sems[0].at[a, k], recv_sem=sems[1].at[a, k],
                                            device_id=to, device_id_type=MESH)

    def _mine(self, ins, outs, sems, a, p):
        return pltpu.make_async_copy(ins[a], _quarter(outs[a], self.kinds[a], p), sems[2].at[a])

    def start(self, ins, outs, sems):
        x, y, c, chips = _place()
        p = 2 * x + y
        for a in range(len(ins)):
            self._mine(ins, outs, sems, a, p).start()
            for j, chip in enumerate(chips):
                self._copy(outs, sems, a, j, p, c, (*chip, c), src=_rows_half(ins[a], c)).start()

    def finish(self, ins, outs, sems):
        x, y, c, chips = _place()
        p = 2 * x + y
        sibling = (x, y, 1 - c)
        n = len(ins)
        for a in range(n):
            for j, (qx, qy) in enumerate(chips):
                q = 2 * qx + qy
                self._copy(outs, sems, a, j, q, c, sibling).wait_recv()
                self._copy(outs, sems, a, 3 + j, q, c, sibling).start()
        for a in range(n):
            for j, (qx, qy) in enumerate(chips):
                q = 2 * qx + qy
                self._copy(outs, sems, a, 3 + j, q, 1 - c, sibling).wait_recv()
                self._copy(outs, sems, a, j, p, c, (qx, qy, c), src=_rows_half(ins[a], c)).wait_send()
                self._copy(outs, sems, a, 3 + j, q, c, sibling).wait_send()
            self._mine(ins, outs, sems, a, p).wait()


class _ToSibling:
    def __init__(self, grads, kinds):
        self.ins = list(grads)
        self.kinds = list(kinds)
        n = len(self.ins)
        self.out_shape = []
        for g, k in zip(self.ins, self.kinds):
            qr, qc = _quarter_shape(g.shape, k)
            self.out_shape.append(jax.ShapeDtypeStruct((NS, qr // 2, qc), g.dtype))
        self.scratch = [pltpu.SemaphoreType.DMA((n, NS)), pltpu.SemaphoreType.DMA((n, NS))]
        self.aliases = {}

    def _copies(self, ins, outs, sems):
        x, y, c, _ = _place()
        return [pltpu.make_async_remote_copy(src_ref=_half_of_quarter(ins[a], self.kinds[a], q, 1 - c), dst_ref=outs[a].at[q],
                                             send_sem=sems[0].at[a, q], recv_sem=sems[1].at[a, q],
                                             device_id=(x, y, 1 - c), device_id_type=MESH)
                for a in range(len(ins)) for q in range(NS)]

    def start(self, ins, outs, sems):
        for cp in self._copies(ins, outs, sems):
            cp.start()

    def finish(self, ins, outs, sems):
        for cp in self._copies(ins, outs, sems):
            cp.wait()


class _ToChips:
    def __init__(self, sums, which=(0, 1, 2)):
        self.ins = list(sums)
        self.which = tuple(which)
        n, m = len(self.ins), len(self.which)
        self.out_shape = [jax.ShapeDtypeStruct((m,) + s.shape[1:], s.dtype) for s in self.ins]
        self.scratch = [pltpu.SemaphoreType.DMA((n, m)), pltpu.SemaphoreType.DMA((n, m))]
        self.aliases = {}

    def _copies(self, ins, outs, sems):
        x, y, c, chips = _place()
        return [pltpu.make_async_remote_copy(src_ref=ins[a].at[2 * chips[j][0] + chips[j][1]], dst_ref=outs[a].at[k],
                                             send_sem=sems[0].at[a, k], recv_sem=sems[1].at[a, k],
                                             device_id=(*chips[j], c), device_id_type=MESH)
                for a in range(len(ins)) for k, j in enumerate(self.which)]

    def start(self, ins, outs, sems):
        for cp in self._copies(ins, outs, sems):
            cp.start()

    def finish(self, ins, outs, sems):
        for cp in self._copies(ins, outs, sems):
            cp.wait()


class _SwapHalves:
    def __init__(self, quarters):
        self.ins = list(quarters)
        n = len(self.ins)
        self.out_shape = [jax.ShapeDtypeStruct(g.shape, g.dtype) for g in self.ins]
        self.scratch = [pltpu.SemaphoreType.DMA((n,)), pltpu.SemaphoreType.DMA((n,))]
        self.aliases = {a: a for a in range(n)}

    def _copy(self, outs, sems, a, pc):
        x, y, c, _ = _place()
        rows = _rows_half(outs[a], pc)
        return pltpu.make_async_remote_copy(src_ref=rows, dst_ref=rows, send_sem=sems[0].at[a], recv_sem=sems[1].at[a],
                                            device_id=(x, y, 1 - c), device_id_type=MESH)

    def start(self, ins, outs, sems):
        c = lax.axis_index("c")
        for a in range(len(outs)):
            self._copy(outs, sems, a, c).start()

    def finish(self, ins, outs, sems):
        c = lax.axis_index("c")
        for a in range(len(outs)):
            self._copy(outs, sems, a, c).wait_send()
            self._copy(outs, sems, a, 1 - c).wait_recv()


def _call(name, grid, compute, in_specs, out_specs, out_shape, scratch, vmem_mb, args, jobs=()):
    n_in, n_out, n_scr = len(in_specs), len(out_specs), len(scratch)
    ji = [len(j.ins) for j in jobs]
    jo = [len(j.out_shape) for j in jobs]
    js = [len(j.scratch) for j in jobs]

    def body(*refs):
        pos = [0]

        def take(k):
            r = refs[pos[0]:pos[0] + k]
            pos[0] += k
            return r

        ins, jins = take(n_in), [take(k) for k in ji]
        outs, jouts = take(n_out), [take(k) for k in jo]
        scr, jscr = take(n_scr), [take(k) for k in js]
        if jobs and grid:
            ids = [pl.program_id(a) for a in range(len(grid))]
            first = functools.reduce(jnp.logical_and, [i == 0 for i in ids])
            last = functools.reduce(jnp.logical_and, [i == g - 1 for i, g in zip(ids, grid)])

            @pl.when(first)
            def _():
                for j, a, b, c in zip(jobs, jins, jouts, jscr):
                    j.start(a, b, c)
        elif jobs:
            for j, a, b, c in zip(jobs, jins, jouts, jscr):
                j.start(a, b, c)
        compute(ins, outs, scr)
        if jobs and grid:
            @pl.when(last)
            def _():
                for j, a, b, c in zip(jobs, jins, jouts, jscr):
                    j.finish(a, b, c)
        elif jobs:
            for j, a, b, c in zip(jobs, jins, jouts, jscr):
                j.finish(a, b, c)

    aliases = {}
    in_off, out_off = n_in, n_out
    for j, a, b in zip(jobs, ji, jo):
        for s, d in j.aliases.items():
            aliases[in_off + s] = out_off + d
        in_off += a
        out_off += b
    res = pl.pallas_call(
        body, name=name, grid=grid,
        in_specs=list(in_specs) + [IN_HBM] * sum(ji), out_specs=list(out_specs) + [IN_HBM] * sum(jo),
        out_shape=list(out_shape) + [pltpu.HBM(s.shape, s.dtype) for j in jobs for s in j.out_shape],
        scratch_shapes=list(scratch) + [s for j in jobs for s in j.scratch],
        input_output_aliases=aliases, compiler_params=_cparams(len(grid), vmem_mb),
    )(*args, *[a for j in jobs for a in j.ins])
    res = list(res)
    main, rest, jres = res[:n_out], res[n_out:], []
    for k in jo:
        jres.append(rest[:k])
        rest = rest[k:]
    return main, jres


def _exchange(name, jobs):
    return _call(name, (), lambda ins, outs, scr: None, [], [], [], [], 16, [], jobs)[1]


def _small_rows(ka, kb):
    first_a = 8
    first_b = first_a + -(-ka // 8) * 8
    return first_a, first_b, first_b + -(-kb // 8) * 8


def _allreduce_small(vecs, taps_a, taps_b):
    n = len(vecs)
    C = vecs[0].shape[1]
    NQ, KA, CB = taps_a.shape
    KB = taps_b.shape[1]
    first_a, first_b, R = _small_rows(KA, KB)
    N = 8

    def body(*refs):
        vec_refs = refs[:n]
        ta_ref, tb_ref, out_ref, v_ref, gath, send_sems, recv_sems, local_sem = refs[n:]
        v_ref[...] = jnp.zeros_like(v_ref)
        for i, r in enumerate(vec_refs):
            v_ref[i:i + 1, :] = r[...]
        for q in range(NQ):
            v_ref[first_a:first_a + KA, q * CB:(q + 1) * CB] = ta_ref[q]
            v_ref[first_b:first_b + KB, q * CB:(q + 1) * CB] = tb_ref[q]
        x, y, c, chips = _place()
        me, sibling = (x, y, c), (x, y, 1 - c)

        def rows(px, py, pc):
            return gath.at[pl.ds((4 * px + 2 * py + pc) * R, R), :]

        def copy(k, block, to, src=None):
            return pltpu.make_async_remote_copy(src_ref=rows(*block) if src is None else src, dst_ref=rows(*block),
                                                send_sem=send_sems.at[k], recv_sem=recv_sems.at[k],
                                                device_id=to, device_id_type=MESH)

        mine = pltpu.make_async_copy(v_ref, rows(*me), local_sem)
        mine.start()
        first = [copy(0, me, sibling, src=v_ref)]
        first += [copy(1 + j, me, (*chip, c), src=v_ref) for j, chip in enumerate(chips)]
        for cp in first:
            cp.start()
        passed = [copy(4 + j, (*chip, c), sibling) for j, chip in enumerate(chips)]
        for j, chip in enumerate(chips):
            copy(1 + j, (*chip, c), me).wait_recv()
            passed[j].start()
        copy(0, sibling, me).wait_recv()
        for j, chip in enumerate(chips):
            copy(4 + j, (*chip, 1 - c), me).wait_recv()
        for cp in first + passed:
            cp.wait_send()
        mine.wait()
        acc = gath[0:R, :]
        for d in range(1, N):
            acc = acc + gath[d * R:(d + 1) * R, :]
        out_ref[...] = acc

    vmem = pl.BlockSpec(memory_space=pltpu.VMEM)
    return pl.pallas_call(
        body, name="allreduce_small",
        in_specs=[vmem] * (n + 2), out_specs=vmem,
        out_shape=jax.ShapeDtypeStruct((R, C), F32),
        scratch_shapes=[pltpu.VMEM((R, C), F32), pltpu.VMEM((N * R, C), F32), pltpu.SemaphoreType.DMA((7,)),
                        pltpu.SemaphoreType.DMA((7,)), pltpu.SemaphoreType.DMA],
    )(*vecs, taps_a, taps_b)


def _adamw_math(w, g, m, v):
    c1 = 1.0 - ADAM_B1 ** ADAM_STEP
    c2 = 1.0 - ADAM_B2 ** ADAM_STEP
    mn = ADAM_B1 * m + (1.0 - ADAM_B1) * g
    vn = ADAM_B2 * v + (1.0 - ADAM_B2) * (g * g)
    return -ADAM_LR * ((mn / c1) / (jnp.sqrt(vn / c2) + ADAM_EPS) + ADAM_WD * w), mn, vn


def _small_adamw(place, small, vec_wmv, tap_wmv):
    n = len(vec_wmv)
    D = small.shape[1]
    CB = tap_wmv[0][0].shape[2]
    ks = [t[0].shape[1] for t in tap_wmv]
    firsts = _small_rows(*ks)[:2]

    def body(place_ref, small_ref, *refs):
        ins, outs = refs[:3 * (n + 2)], refs[3 * (n + 2):]
        chip = place_ref[0]
        for i in range(n):
            g = small_ref[i:i + 1, :]
            d, mn, vn = _adamw_math(ins[3 * i][...], g, ins[3 * i + 1][...], ins[3 * i + 2][...])
            for o, val in zip(outs[4 * i:4 * i + 4], (g, d, mn, vn)):
                o[...] = val
        for t, (row0, k) in enumerate(zip(firsts, ks)):
            g = jnp.zeros((k, CB), F32)
            for q in range(D // CB):
                g = g + jnp.where(chip == q, small_ref[row0:row0 + k, q * CB:(q + 1) * CB], 0.0)
            w_ref, m_ref, v_ref = ins[3 * (n + t):3 * (n + t) + 3]
            d, mn, vn = _adamw_math(w_ref[0], g, m_ref[0], v_ref[0])
            for o, val in zip(outs[4 * (n + t):4 * (n + t) + 4], (g, d, mn, vn)):
                o[0] = val

    flat = [a for wmv in list(vec_wmv) + list(tap_wmv) for a in wmv]
    shapes = [jax.ShapeDtypeStruct(wmv[0].shape, F32) for wmv in list(vec_wmv) + list(tap_wmv) for _ in range(4)]
    vmem = pl.BlockSpec(memory_space=pltpu.VMEM)
    res = pl.pallas_call(
        body, name="small_adamw",
        in_specs=[pl.BlockSpec(memory_space=pltpu.SMEM)] + [vmem] * (1 + len(flat)), out_specs=[vmem] * len(shapes),
        out_shape=shapes,
    )(place, small, *flat)
    return [res[4 * i:4 * i + 4] for i in range(n + 2)]


def _ffn_fwd(h, g, wg, wu, wd, name, jobs=()):
    S, D = h.shape
    F = wg.shape[0]
    ts = _tile(S, 512)
    fb = _tile(F, F // 2)
    nf = F // fb

    def compute(ins, outs, scr):
        h_ref, g_ref, wg_ref, wu_ref, wd_ref = ins
        ho_ref, n_ref, gp_ref, up_ref = outs
        x = h_ref[...]
        n = _rms_fwd(x, g_ref[...]).astype(BF16)
        n_ref[...] = n
        acc = None
        for j in range(nf):
            cols = slice(j * fb, (j + 1) * fb)
            gp = _dot_nt(n, wg_ref[cols, :])
            up = _dot_nt(n, wu_ref[cols, :])
            gp_ref[:, cols] = gp.astype(BF16)
            up_ref[:, cols] = up.astype(BF16)
            part = _dot((gp * _sigmoid(gp) * up).astype(BF16), wd_ref[cols, :])
            acc = part if acc is None else acc + part
        ho_ref[...] = x + 0.5 * acc

    tok = pl.BlockSpec((ts, D), lambda i: (i, 0))
    wsp = _resident((F, D))
    hid = pl.BlockSpec((ts, F), lambda i: (i, 0))
    return _call(name, (S // ts,), compute, [tok, pl.BlockSpec((1, D), lambda i: (0, 0)), wsp, wsp, wsp], [tok, tok, hid, hid],
                 [jax.ShapeDtypeStruct((S, D), F32), jax.ShapeDtypeStruct((S, D), BF16),
                  jax.ShapeDtypeStruct((S, F), BF16), jax.ShapeDtypeStruct((S, F), BF16)],
                 [], 56, [h, g, wg, wu, wd], jobs)


def _ffn_bwd_hidden(do, gp, up, wd, name, jobs=()):
    S, D = do.shape
    F = wd.shape[0]
    ts = _tile(S, 512)
    fb = _tile(F, F // 2)

    def compute(ins, outs, scr):
        do_ref, gp_ref, up_ref, wd_ref = ins
        dgp_ref, dup_ref, a_ref = outs
        da = _dot_nt(do_ref[...], wd_ref[...])
        gf = gp_ref[...].astype(F32)
        uf = up_ref[...].astype(F32)
        sg = _sigmoid(gf)
        si = gf * sg
        dgp_ref[...] = (da * uf * (sg * (1.0 + gf * (1.0 - sg)))).astype(BF16)
        dup_ref[...] = (da * si).astype(BF16)
        a_ref[...] = (si * uf).astype(BF16)

    tok = pl.BlockSpec((ts, D), lambda s, i: (i, 0))
    hid = pl.BlockSpec((ts, fb), lambda s, i: (i, s))
    return _call(name, (F // fb, S // ts), compute, [tok, hid, hid, pl.BlockSpec((fb, D), lambda s, i: (s, 0))],
                 [hid, hid, hid], [jax.ShapeDtypeStruct((S, F), BF16)] * 3, [], 56, [do, gp, up, wd], jobs)


def _ffn_bwd_input(dh, h, g, dgp, dup, wg, wu, name, jobs=()):
    S, D = h.shape
    F = wg.shape[0]
    ts = _tile(S, 512)

    def compute(ins, outs, scr):
        dh_ref, h_ref, g_ref, dgp_ref, dup_ref, wg_ref, wu_ref = ins
        dhi_ref, dg_ref = outs

        @pl.when(pl.program_id(0) == 0)
        def _():
            dg_ref[...] = jnp.zeros_like(dg_ref)

        dn = _dot(dgp_ref[...], wg_ref[...]) + _dot(dup_ref[...], wu_ref[...])
        dx, dg = _rms_bwd(h_ref[...], g_ref[...], dn)
        dhi_ref[...] = dh_ref[...] + dx
        dg_ref[...] += dg

    tok = pl.BlockSpec((ts, D), lambda i: (i, 0))
    hid = pl.BlockSpec((ts, F), lambda i: (i, 0))
    row = pl.BlockSpec((1, D), lambda i: (0, 0))
    return _call(name, (S // ts,), compute, [tok, tok, row, hid, hid, _resident((F, D)), _resident((F, D))], [tok, row],
                 [jax.ShapeDtypeStruct((S, D), F32), jax.ShapeDtypeStruct((1, D), F32)], [], 56,
                 [dh, h, g, dgp, dup, wg, wu], jobs)


def _ffn_wgrad(hid, tok, name, jobs=()):
    S, D = tok.shape
    F = hid.shape[1]
    fb = _tile(F, F // 2)

    def compute(ins, outs, scr):
        outs[0][...] = _dot_tn(ins[0][...], ins[1][...])

    main, jres = _call(name, (F // fb,), compute,
                       [pl.BlockSpec((S, fb), lambda j: (0, j)), _resident(tok.shape)],
                       [pl.BlockSpec((fb, D), lambda j: (j, 0))], [jax.ShapeDtypeStruct((F, D), F32)], [], 56,
                       [hid, tok], jobs)
    return main[0], jres


def _mix_in_fwd(h, g, win, jobs=()):
    S, D = h.shape
    NG = win.shape[1] // D
    ts = _tile(S, 512)

    def compute(ins, outs, scr):
        h_ref, g_ref, w_ref = ins
        u_ref, z_ref = outs
        u = _rms_fwd(h_ref[...], g_ref[...]).astype(BF16)
        u_ref[...] = u
        for k in range(NG):
            z_ref[k] = _dot(u, w_ref[:, k * D:(k + 1) * D]).astype(BF16)

    return _call("mix_in_fwd", (S // ts,), compute,
                 [pl.BlockSpec((ts, D), lambda i: (i, 0)), pl.BlockSpec((1, D), lambda i: (0, 0)), _resident(win.shape)],
                 [pl.BlockSpec((ts, D), lambda i: (i, 0)), pl.BlockSpec((NG, ts, D), lambda i: (0, i, 0))],
                 [jax.ShapeDtypeStruct((S, D), BF16), jax.ShapeDtypeStruct((NG, S, D), BF16)],
                 [], 48, [h, g, win], jobs)


SUBLANES = 8


def _shifted_copies(s):
    n = s.shape[1] - SUBLANES
    for r in range(1, SUBLANES):
        s[r, 0:n, :] = s[0, r:r + n, :]


def _window(s, o, rows):
    r = o % SUBLANES
    return s[r, o - r:o - r + rows, :]


def _conv_fwd(z, wa, ba, wb, jobs=()):
    _, S, D = z.shape
    _, KA, CB = wa.shape
    KB = wb.shape[1]
    ts = _tile(S, 512)
    r = ts // HALO
    CH = min(64, ts)

    def compute(ins, outs, scr):
        z_ref, zh_ref, wa_ref, ba_ref, wb_ref = ins
        a1_ref, q_ref = outs
        sa, sb = scr
        keep = (pl.program_id(1) > 0).astype(F32)
        sa[0, HALO:HALO + ts, :] = z_ref[0].astype(F32) * _sigmoid(z_ref[1].astype(F32))
        sa[0, 0:HALO, :] = zh_ref[0].astype(F32) * _sigmoid(zh_ref[1].astype(F32)) * keep
        _shifted_copies(sa)
        sb[HALO:HALO + ts, :] = z_ref[3].astype(F32) * z_ref[4].astype(F32)
        sb[0:HALO, :] = zh_ref[3].astype(F32) * zh_ref[4].astype(F32) * keep
        wak = [wa_ref[0, k:k + 1, :] for k in range(KA)]
        wbk = [wb_ref[0, k:k + 1, :] for k in range(KB)]
        for c0 in range(0, ts, CH):
            acc = jnp.broadcast_to(ba_ref[...], (CH, CB))
            for k in range(KA):
                acc = acc + wak[k] * _window(sa, c0 + HALO - (KA - 1) + k, CH)
            a1_ref[c0:c0 + CH, :] = acc
            v = jnp.zeros((CH, CB), F32)
            for k in range(KB):
                o = c0 + HALO - (KB - 1) + k
                v = v + wbk[k] * sb[o:o + CH, :]
            q_ref[c0:c0 + CH, :] = (z_ref[2, c0:c0 + CH, :].astype(F32) * v).astype(BF16)

    return _call("conv_fwd", (D // CB, S // ts), compute,
                 [pl.BlockSpec((5, ts, CB), lambda j, i: (0, i, j)),
                  pl.BlockSpec((5, HALO, CB), lambda j, i: (0, jnp.maximum(i * r - 1, 0), j)),
                  pl.BlockSpec((1, KA, CB), lambda j, i: (j, 0, 0)), pl.BlockSpec((1, CB), lambda j, i: (0, j)),
                  pl.BlockSpec((1, KB, CB), lambda j, i: (j, 0, 0))],
                 [pl.BlockSpec((ts, CB), lambda j, i: (i, j)), pl.BlockSpec((ts, CB), lambda j, i: (i, j))],
                 [jax.ShapeDtypeStruct((S, D), F32), jax.ShapeDtypeStruct((S, D), BF16)],
                 [pltpu.VMEM((SUBLANES, HALO + ts, CB), F32), pltpu.VMEM((HALO + ts, CB), F32)], 32, [z, z, wa, ba, wb], jobs)


def _ln_stats(a1):
    mu = jnp.mean(a1, axis=-1, keepdims=True)
    xc = a1 - mu
    rstd = lax.rsqrt(jnp.mean(xc * xc, axis=-1, keepdims=True) + EPS)
    return xc * rstd, rstd


def _mix_out_fwd(h1, a1, q, z, lng, lnb, wa, wb, wo):
    S, D = h1.shape
    ts = _tile(S, 512)

    def compute(ins, outs, scr):
        h_ref, a1_ref, q_ref, ga_ref, gb_ref, lng_ref, lnb_ref, wa_ref, wb_ref, wo_ref = ins
        h2_ref, a3_ref, m_ref, ya_ref, yb_ref = outs
        xhat, _ = _ln_stats(a1_ref[...])
        a2 = xhat * lng_ref[...] + lnb_ref[...]
        a3 = (a2 * _sigmoid(a2)).astype(BF16)
        a3_ref[...] = a3
        ya = _dot(a3, wa_ref[...])
        yb = _dot(q_ref[...], wb_ref[...])
        ya_ref[...] = ya.astype(BF16)
        yb_ref[...] = yb.astype(BF16)
        m = (_sigmoid(ga_ref[0].astype(F32)) * ya + _sigmoid(gb_ref[0].astype(F32)) * yb).astype(BF16)
        m_ref[...] = m
        h2_ref[...] = h_ref[...] + _dot(m, wo_ref[...])

    tok = pl.BlockSpec((ts, D), lambda i: (i, 0))
    row = pl.BlockSpec((1, D), lambda i: (0, 0))
    mat = _resident((D, D))
    return _call("mix_out_fwd", (S // ts,), compute,
                 [tok, tok, tok, pl.BlockSpec((1, ts, D), lambda i: (5, i, 0)), pl.BlockSpec((1, ts, D), lambda i: (6, i, 0)),
                  row, row, mat, mat, mat], [tok] * 5,
                 [jax.ShapeDtypeStruct((S, D), F32)] + [jax.ShapeDtypeStruct((S, D), BF16)] * 4,
                 [], 56, [h1, a1, q, z, z, lng, lnb, wa, wb, wo])[0]


def _mix_out_bwd(dh2, a1, z, ya, yb, lng, lnb, wa, wb, wo, jobs=()):
    S, D = dh2.shape
    ts = _tile(S, 512)

    def compute(ins, outs, scr):
        dh_ref, a1_ref, ga_ref, gb_ref, ya_ref, yb_ref, lng_ref, lnb_ref, wa_ref, wb_ref, wo_ref = ins
        da1_ref, dq_ref, dga_ref, dgb_ref, dya_ref, dyb_ref, dhb_ref, dlg_ref, dlb_ref = outs

        @pl.when(pl.program_id(0) == 0)
        def _():
            dlg_ref[...] = jnp.zeros_like(dlg_ref)
            dlb_ref[...] = jnp.zeros_like(dlb_ref)

        dhb = dh_ref[...].astype(BF16)
        dhb_ref[...] = dhb
        dm = _dot_nt(dhb, wo_ref[...])
        sa = _sigmoid(ga_ref[0].astype(F32))
        sb = _sigmoid(gb_ref[0].astype(F32))
        dga_ref[...] = (dm * ya_ref[...].astype(F32) * sa * (1.0 - sa)).astype(BF16)
        dgb_ref[...] = (dm * yb_ref[...].astype(F32) * sb * (1.0 - sb)).astype(BF16)
        dya = (sa * dm).astype(BF16)
        dyb = (sb * dm).astype(BF16)
        dya_ref[...] = dya
        dyb_ref[...] = dyb
        dq_ref[...] = _dot_nt(dyb, wb_ref[...])
        da3 = _dot_nt(dya, wa_ref[...])
        xhat, rstd = _ln_stats(a1_ref[...])
        a2 = xhat * lng_ref[...] + lnb_ref[...]
        sg = _sigmoid(a2)
        da2 = da3 * (sg * (1.0 + a2 * (1.0 - sg)))
        dlg_ref[...] += jnp.sum(da2 * xhat, axis=0, keepdims=True)
        dlb_ref[...] += jnp.sum(da2, axis=0, keepdims=True)
        dxh = da2 * lng_ref[...]
        da1_ref[...] = rstd * (dxh - jnp.mean(dxh, axis=-1, keepdims=True)
                               - xhat * jnp.mean(dxh * xhat, axis=-1, keepdims=True))

    tok = pl.BlockSpec((ts, D), lambda i: (i, 0))
    row = pl.BlockSpec((1, D), lambda i: (0, 0))
    mat = _resident((D, D))
    return _call("mix_out_bwd", (S // ts,), compute,
                 [tok, tok, pl.BlockSpec((1, ts, D), lambda i: (5, i, 0)), pl.BlockSpec((1, ts, D), lambda i: (6, i, 0)),
                  tok, tok, row, row, mat, mat, mat], [tok] * 7 + [row, row],
                 [jax.ShapeDtypeStruct((S, D), F32), jax.ShapeDtypeStruct((S, D), F32)]
                 + [jax.ShapeDtypeStruct((S, D), BF16)] * 5 + [jax.ShapeDtypeStruct((1, D), F32)] * 2,
                 [], 56, [dh2, a1, z, z, ya, yb, lng, lnb, wa, wb, wo], jobs)


def _mixer_wgrads(a3, dya, q, dyb, mm, dhb, jobs=()):
    S, D = a3.shape
    tk = _tile(S, 512)

    def compute(ins, outs, scr):
        @pl.when(pl.program_id(0) == 0)
        def _():
            for o in outs:
                o[...] = jnp.zeros_like(o)

        for t in range(3):
            outs[t][...] += _dot_tn(ins[2 * t][...], ins[2 * t + 1][...])

    tok = pl.BlockSpec((tk, D), lambda k: (k, 0))
    return _call("mixer_wgrads", (S // tk,), compute, [tok] * 6, [pl.BlockSpec((D, D), lambda k: (0, 0))] * 3,
                 [jax.ShapeDtypeStruct((D, D), F32)] * 3, [], 56, [a3, dya, q, dyb, mm, dhb], jobs)


def _conv_bwd(z, da1, dq, dga, dgb, wa, wb, jobs=()):
    NG, S, D = z.shape
    _, KA, CB = wa.shape
    KB = wb.shape[1]
    ts = _tile(S, 512)
    r = ts // HALO
    nt = S // ts
    CH = min(64, ts)
    last_halo = S // HALO - 1

    def compute(ins, outs, scr):
        z_ref, zp_ref, zn_ref, da1_ref, da1n_ref, dq_ref, dqn_ref, dga_ref, dgb_ref, wa_ref, wb_ref = ins
        dz_ref, dwa_ref, dba_ref, dwb_ref = outs
        sa0, sd, sp, sv, acca, accb = scr
        i = pl.program_id(1)
        prev = (i > 0).astype(F32)
        nxt = (i < nt - 1).astype(F32)

        @pl.when(i == 0)
        def _():
            acca[...] = jnp.zeros_like(acca)
            accb[...] = jnp.zeros_like(accb)
            dba_ref[...] = jnp.zeros_like(dba_ref)

        sa0[0, HALO:HALO + ts, :] = z_ref[0].astype(F32) * _sigmoid(z_ref[1].astype(F32))
        sa0[0, 0:HALO, :] = zp_ref[0].astype(F32) * _sigmoid(zp_ref[1].astype(F32)) * prev
        _shifted_copies(sa0)
        sp[HALO:HALO + ts, :] = z_ref[3].astype(F32) * z_ref[4].astype(F32)
        sp[0:HALO, :] = zp_ref[3].astype(F32) * zp_ref[4].astype(F32) * prev
        sd[0, 0:ts, :] = da1_ref[...]
        sd[0, ts:ts + HALO, :] = da1n_ref[...] * nxt
        _shifted_copies(sd)
        sv[0:ts, :] = dq_ref[...] * z_ref[2].astype(F32)
        sv[ts:ts + HALO, :] = dqn_ref[...] * zn_ref[2].astype(F32) * nxt
        dba_ref[...] += jnp.sum(da1_ref[...], axis=0, keepdims=True)
        wak = [wa_ref[0, k:k + 1, :] for k in range(KA)]
        wbk = [wb_ref[0, k:k + 1, :] for k in range(KB)]
        for c0 in range(0, ts, CH):
            rows = slice(c0, c0 + CH)
            d1 = sd[0, rows, :]
            da0 = jnp.zeros((CH, CB), F32)
            for k in range(KA):
                da0 = da0 + wak[k] * _window(sd, c0 + (KA - 1) - k, CH)
                a0w = _window(sa0, c0 + HALO - (KA - 1) + k, CH)
                acca[k] += jnp.sum((d1 * a0w).reshape(CH // 8, 8, CB), axis=0)
            val = z_ref[0, rows, :].astype(F32)
            sg = _sigmoid(z_ref[1, rows, :].astype(F32))
            dz_ref[0, rows, :] = (da0 * sg).astype(BF16)
            dz_ref[1, rows, :] = (da0 * val * sg * (1.0 - sg)).astype(BF16)
            dv = sv[rows, :]
            v = jnp.zeros((CH, CB), F32)
            dp = jnp.zeros((CH, CB), F32)
            for k in range(KB):
                o = c0 + HALO - (KB - 1) + k
                pw = sp[o:o + CH, :]
                v = v + wbk[k] * pw
                accb[k] += jnp.sum((dv * pw).reshape(CH // 8, 8, CB), axis=0)
                o = c0 + (KB - 1) - k
                dp = dp + wbk[k] * sv[o:o + CH, :]
            dz_ref[2, rows, :] = (dq_ref[rows, :] * v).astype(BF16)
            dz_ref[3, rows, :] = (dp * z_ref[4, rows, :].astype(F32)).astype(BF16)
            dz_ref[4, rows, :] = (dp * z_ref[3, rows, :].astype(F32)).astype(BF16)
        dz_ref[5] = dga_ref[...]
        dz_ref[6] = dgb_ref[...]

        @pl.when(i == nt - 1)
        def _():
            dwa_ref[0] = jnp.sum(acca[...], axis=1)
            dwb_ref[0] = jnp.sum(accb[...], axis=1)

    zt = pl.BlockSpec((5, ts, CB), lambda j, i: (0, i, j))
    zp = pl.BlockSpec((5, HALO, CB), lambda j, i: (0, jnp.maximum(i * r - 1, 0), j))
    zn = pl.BlockSpec((5, HALO, CB), lambda j, i: (0, jnp.minimum((i + 1) * r, last_halo), j))
    tok = pl.BlockSpec((ts, CB), lambda j, i: (i, j))
    tokn = pl.BlockSpec((HALO, CB), lambda j, i: (jnp.minimum((i + 1) * r, last_halo), j))
    return _call("conv_bwd", (D // CB, nt), compute,
                 [zt, zp, zn, tok, tokn, tok, tokn, tok, tok,
                  pl.BlockSpec((1, KA, CB), lambda j, i: (j, 0, 0)), pl.BlockSpec((1, KB, CB), lambda j, i: (j, 0, 0))],
                 [pl.BlockSpec((NG, ts, CB), lambda j, i: (0, i, j)), pl.BlockSpec((1, KA, CB), lambda j, i: (j, 0, 0)),
                  pl.BlockSpec((1, CB), lambda j, i: (0, j)), pl.BlockSpec((1, KB, CB), lambda j, i: (j, 0, 0))],
                 [jax.ShapeDtypeStruct((NG, S, D), BF16), jax.ShapeDtypeStruct((D // CB, KA, CB), F32),
                  jax.ShapeDtypeStruct((1, D), F32), jax.ShapeDtypeStruct((D // CB, KB, CB), F32)],
                 [pltpu.VMEM((SUBLANES, HALO + ts, CB), F32), pltpu.VMEM((SUBLANES, ts + HALO, CB), F32),
                  pltpu.VMEM((HALO + ts, CB), F32), pltpu.VMEM((ts + HALO, CB), F32),
                  pltpu.VMEM((KA, 8, CB), F32), pltpu.VMEM((KB, 8, CB), F32)],
                 40, [z, z, z, da1, da1, dq, dq, dga, dgb, wa, wb], jobs)


def _mix_in_bwd(dh2, h1, g, dz, win, jobs=()):
    S, D = h1.shape
    NG = dz.shape[0]
    ts = _tile(S, 512)

    def compute(ins, outs, scr):
        dh_ref, h_ref, g_ref, dz_ref, w_ref = ins
        dhi_ref, dg_ref, do_ref = outs

        @pl.when(pl.program_id(0) == 0)
        def _():
            dg_ref[...] = jnp.zeros_like(dg_ref)

        du = _dot_nt(dz_ref[0], w_ref[:, 0:D])
        for k in range(1, NG):
            du = du + _dot_nt(dz_ref[k], w_ref[:, k * D:(k + 1) * D])
        dx, dg = _rms_bwd(h_ref[...], g_ref[...], du)
        dhi = dh_ref[...] + dx
        dhi_ref[...] = dhi
        do_ref[...] = (0.5 * dhi).astype(BF16)
        dg_ref[...] += dg

    tok = pl.BlockSpec((ts, D), lambda i: (i, 0))
    row = pl.BlockSpec((1, D), lambda i: (0, 0))
    return _call("mix_in_bwd", (S // ts,), compute,
                 [tok, tok, row, pl.BlockSpec((NG, ts, D), lambda i: (0, i, 0)), _resident(win.shape)],
                 [tok, row, tok],
                 [jax.ShapeDtypeStruct((S, D), F32), jax.ShapeDtypeStruct((1, D), F32), jax.ShapeDtypeStruct((S, D), BF16)],
                 [], 56, [dh2, h1, g, dz, win], jobs)


def _w_in_grad(u, dz, jobs=()):
    S, D = u.shape
    NG = dz.shape[0]

    def compute(ins, outs, scr):
        outs[0][...] = _dot_tn(ins[0][...], ins[1][0])

    return _call("w_in_grad", (NG,), compute,
                 [_resident(u.shape), pl.BlockSpec((1, S, D), lambda j: (j, 0, 0))],
                 [pl.BlockSpec((D, D), lambda j: (0, j))], [jax.ShapeDtypeStruct((D, NG * D), F32)], [], 48, [u, dz], jobs)


def _loss_head(h3, t, g):
    S, D = h3.shape
    ts = _tile(S, 512)

    def compute(ins, outs, scr):
        h_ref, t_ref, g_ref = ins
        dh_ref, dg_ref, loss_ref, do_ref = outs

        @pl.when(pl.program_id(0) == 0)
        def _():
            dg_ref[...] = jnp.zeros_like(dg_ref)
            loss_ref[...] = jnp.zeros_like(loss_ref)

        x = h_ref[...]
        err = _rms_fwd(x, g_ref[...]) - t_ref[...]
        loss_ref[...] += (0.5 / D) * jnp.sum(err * err)
        dx, dg = _rms_bwd(x, g_ref[...], err * (1.0 / D))
        dh_ref[...] = dx
        do_ref[...] = (0.5 * dx).astype(BF16)
        dg_ref[...] += dg

    tok = pl.BlockSpec((ts, D), lambda i: (i, 0))
    row = pl.BlockSpec((1, D), lambda i: (0, 0))
    return _call("loss_head", (S // ts,), compute, [tok, tok, row],
                 [tok, row, pl.BlockSpec((8, 128), lambda i: (0, 0)), tok],
                 [jax.ShapeDtypeStruct((S, D), F32), jax.ShapeDtypeStruct((1, D), F32), jax.ShapeDtypeStruct((8, 128), F32),
                  jax.ShapeDtypeStruct((S, D), BF16)], [], 40, [h3, t, g])[0]


def _chip_sums(place, grads, got, kind, name):
    n = len(grads)
    qr, qc = _quarter_shape(grads[0].shape, kind)
    h = qr // 2
    tr = _row_tile(h)
    nr = h // tr

    def body(pc_ref, *refs):
        g_refs, got_refs, b_refs, f_refs = refs[:n], refs[n:2 * n], refs[2 * n:3 * n], refs[3 * n:]
        own = pl.program_id(1) == pc_ref[0]
        for a in range(n):
            s = g_refs[a][...] + got_refs[a][0]
            b_refs[a][0] = s.astype(BF16)

            @pl.when(own)
            def _():
                f_refs[a][...] = s

    if kind == "rows":
        gspec = pl.BlockSpec((tr, qc), lambda r, q, pc: (q * (2 * nr) + pc[1] * nr + r, 0))
    else:
        gspec = pl.BlockSpec((tr, qc), lambda r, q, pc: (pc[1] * nr + r, q))
    lspec = pl.BlockSpec((1, tr, qc), lambda r, q, pc: (q, r, 0))
    res = pl.pallas_call(
        body, name=name,
        grid_spec=pltpu.PrefetchScalarGridSpec(
            num_scalar_prefetch=1, grid=(nr, NS), in_specs=[gspec] * n + [lspec] * n,
            out_specs=[lspec] * n + [pl.BlockSpec((tr, qc), lambda r, q, pc: (r, 0))] * n),
        out_shape=[jax.ShapeDtypeStruct((NS, h, qc), BF16)] * n + [jax.ShapeDtypeStruct((h, qc), F32)] * n,
        compiler_params=_cparams(2, 48),
    )(place, *grads, *got)
    return res[:n], res[n:]


def _totals(place, own, got, name):
    n = len(own)
    h, qc = own[0].shape
    tr = _row_tile(h)
    nr = h // tr
    got = [list(g) if isinstance(g, (list, tuple)) else [g] for g in got]
    m = len(got[0])

    def body(pc_ref, *refs):
        own_refs, got_refs, o_refs = refs[:n], refs[n:n + n * m], refs[n + n * m:]
        for a in range(n):
            acc = own_refs[a][...]
            for g in got_refs[a * m:(a + 1) * m]:
                for k in range(g.shape[0]):
                    acc = acc + g[k].astype(F32)
            o_refs[a][...] = acc

    lands = [pl.BlockSpec((g.shape[0], tr, qc), lambda r, pc: (0, r, 0)) for gs in got for g in gs]
    return pl.pallas_call(
        body, name=name,
        grid_spec=pltpu.PrefetchScalarGridSpec(
            num_scalar_prefetch=1, grid=(nr,),
            in_specs=[pl.BlockSpec((tr, qc), lambda r, pc: (r, 0))] * n + lands,
            out_specs=[pl.BlockSpec((tr, qc), lambda r, pc: (pc[1] * nr + r, 0))] * n),
        out_shape=[jax.ShapeDtypeStruct((2 * h, qc), F32)] * n,
        compiler_params=_cparams(1, 48),
    )(place, *own, *[g for gs in got for g in gs])


def _adamw(ws, gs, ms, vs, name):
    n = len(ws)
    R, C = ws[0].shape
    tr = _row_tile(R, (36 << 20) // (7 * 2 * 4 * n * C))

    def body(*refs):
        w_refs, g_refs, m_refs, v_refs = refs[:n], refs[n:2 * n], refs[2 * n:3 * n], refs[3 * n:4 * n]
        d_refs, mo_refs, vo_refs = refs[4 * n:5 * n], refs[5 * n:6 * n], refs[6 * n:]
        for a in range(n):
            d_refs[a][...], mo_refs[a][...], vo_refs[a][...] = _adamw_math(w_refs[a][...], g_refs[a][...], m_refs[a][...],
                                                                         v_refs[a][...])

    blk = pl.BlockSpec((tr, C), lambda r: (r, 0))
    res = pl.pallas_call(
        body, name=name, grid=(R // tr,),
        in_specs=[blk] * (4 * n), out_specs=[blk] * (3 * n),
        out_shape=[jax.ShapeDtypeStruct((R, C), F32)] * (3 * n),
        compiler_params=_cparams(1, 56),
    )(*ws, *gs, *ms, *vs)
    return res[:n], res[n:2 * n], res[2 * n:]


def kernel(x, ffn1_norm, ffn1_w_gate, ffn1_w_up, ffn1_w_down, mix_norm, w_in, a_dw_w, a_dw_b, a_ln_g, a_ln_b, a_w_out, b_conv_w, b_w_out, w_o, ffn2_norm, ffn2_w_gate, ffn2_w_up, ffn2_w_down, final_norm, loss_target, m_ffn1_norm, m_ffn1_w_gate, m_ffn1_w_up, m_ffn1_w_down, m_mix_norm, m_w_in, m_a_dw_w, m_a_dw_b, m_a_ln_g, m_a_ln_b, m_a_w_out, m_b_conv_w, m_b_w_out, m_w_o, m_ffn2_norm, m_ffn2_w_gate, m_ffn2_w_up, m_ffn2_w_down, m_final_norm, v_ffn1_norm, v_ffn1_w_gate, v_ffn1_w_up, v_ffn1_w_down, v_mix_norm, v_w_in, v_a_dw_w, v_a_dw_b, v_a_ln_g, v_a_ln_b, v_a_w_out, v_b_conv_w, v_b_w_out, v_w_o, v_ffn2_norm, v_ffn2_w_gate, v_ffn2_w_up, v_ffn2_w_down, v_final_norm):
    names = ["ffn1_norm", "ffn1_w_gate", "ffn1_w_up", "ffn1_w_down", "mix_norm", "w_in", "a_dw_w", "a_dw_b", "a_ln_g",
             "a_ln_b", "a_w_out", "b_conv_w", "b_w_out", "w_o", "ffn2_norm", "ffn2_w_gate", "ffn2_w_up", "ffn2_w_down",
             "final_norm"]
    W = dict(zip(names, [ffn1_norm, ffn1_w_gate, ffn1_w_up, ffn1_w_down, mix_norm, w_in, a_dw_w, a_dw_b, a_ln_g, a_ln_b,
                         a_w_out, b_conv_w, b_w_out, w_o, ffn2_norm, ffn2_w_gate, ffn2_w_up, ffn2_w_down, final_norm]))
    M = dict(zip(names, [m_ffn1_norm, m_ffn1_w_gate, m_ffn1_w_up, m_ffn1_w_down, m_mix_norm, m_w_in, m_a_dw_w, m_a_dw_b,
                         m_a_ln_g, m_a_ln_b, m_a_w_out, m_b_conv_w, m_b_w_out, m_w_o, m_ffn2_norm, m_ffn2_w_gate,
                         m_ffn2_w_up, m_ffn2_w_down, m_final_norm]))
    V = dict(zip(names, [v_ffn1_norm, v_ffn1_w_gate, v_ffn1_w_up, v_ffn1_w_down, v_mix_norm, v_w_in, v_a_dw_w, v_a_dw_b,
                         v_a_ln_g, v_a_ln_b, v_a_w_out, v_b_conv_w, v_b_w_out, v_w_o, v_ffn2_norm, v_ffn2_w_gate,
                         v_ffn2_w_up, v_ffn2_w_down, v_final_norm]))
    transposed = ("ffn1_w_gate", "ffn1_w_up", "ffn2_w_gate", "ffn2_w_up")
    vecs = ["ffn1_norm", "mix_norm", "a_dw_b", "a_ln_g", "a_ln_b", "ffn2_norm", "final_norm"]
    ffn1 = ["ffn1_w_gate", "ffn1_w_up", "ffn1_w_down"]
    ffn2 = ["ffn2_w_gate", "ffn2_w_up", "ffn2_w_down"]
    outp = ["a_w_out", "b_w_out", "w_o"]

    S, D = x.shape[1], x.shape[2]
    CB = D // NS
    KA, KB = a_dw_w.shape[1], b_conv_w.shape[1]
    px, py, pc = lax.axis_index("x"), lax.axis_index("y"), lax.axis_index("c")
    chip = 2 * px + py
    place = jnp.stack([chip, pc]).astype(jnp.int32)
    h0 = x.reshape(S, D)
    tgt = loss_target.reshape(S, D)
    row = lambda n: pltpu.with_memory_space_constraint(W[n].reshape(1, D), pltpu.HBM)
    pad = lambda a, r: jnp.concatenate([a, jnp.zeros((r - a.shape[0], a.shape[1]), F32)], axis=0)

    def quarter(P, n):
        return jnp.transpose(P[n][0]) if n in transposed else P[n][0]

    def unquarter(a, n):
        return (jnp.transpose(a) if n in transposed else a).reshape(W[n].shape)

    wq = {n: quarter(W, n).astype(BF16) for n in ffn1 + ffn2 + outp + ["w_in"]}

    f1 = _exchange("gather_ffn1", [_Gather([wq[n] for n in ffn1], ["rows"] * 3)])[0]
    g_in = _Gather([wq["w_in"], pad(a_dw_w[0], 32), pad(b_conv_w[0], 16)], ["cols", "rows", "rows"])
    (h1, n1, gp1, up1), ((win, taps_a, taps_b),) = _ffn_fwd(h0, row("ffn1_norm"), *f1, "ffn1_fwd", [g_in])
    wa_taps = taps_a.reshape(NS, 32, CB)[:, :KA]
    wb_taps = taps_b.reshape(NS, 16, CB)[:, :KB]
    g_out = _Gather([wq[n] for n in outp] + [wq["ffn2_w_gate"]], ["rows"] * 4)
    (u, z), ((wa_out, wb_out, wo, f2g),) = _mix_in_fwd(h1, row("mix_norm"), win, [g_out])
    g_f2 = _Gather([wq["ffn2_w_up"], wq["ffn2_w_down"]], ["rows"] * 2)
    (a1, q), ((f2u, f2d),) = _conv_fwd(z, wa_taps, row("a_dw_b"), wb_taps, [g_f2])
    h2, a3, mm, ya, yb = _mix_out_fwd(h1, a1, q, z, row("a_ln_g"), row("a_ln_b"), wa_out, wb_out, wo)
    (h3, n2, gp2, up2), _ = _ffn_fwd(h2, row("ffn2_norm"), f2g, f2u, f2d, "ffn2_fwd")
    dh3, d_final, loss_part, do2 = _loss_head(h3, tgt, row("final_norm"))
    loss = lax.psum(loss_part[0, 0], ("x", "y", "c"))

    (dgp2, dup2, act2), _ = _ffn_bwd_hidden(do2, gp2, up2, f2d, "ffn2_bwd_hidden")
    (dh2, d_ffn2), _ = _ffn_bwd_input(dh3, h2, row("ffn2_norm"), dgp2, dup2, f2g, f2u, "ffn2_bwd_input")
    g2 = [_ffn_wgrad(dgp2, n2, "ffn2_dwg")[0], _ffn_wgrad(dup2, n2, "ffn2_dwu")[0], _ffn_wgrad(act2, do2, "ffn2_dwd")[0]]
    (da1, dq, dga, dgb, dya, dyb, dh2b, d_lng, d_lnb), (got,) = _mix_out_bwd(
        dh2, a1, z, ya, yb, row("a_ln_g"), row("a_ln_b"), wa_out, wb_out, wo, [_ToSibling(g2, ["rows"] * 3)])
    wire2, own2 = _chip_sums(place, g2, got, "rows", "ffn2_chip_sums")
    (dz, d_wa, d_ba, d_wb), (got,) = _conv_bwd(z, da1, dq, dga, dgb, wa_taps, wb_taps, [_ToChips(wire2)])
    half2 = _totals(place, own2, got, "ffn2_totals")
    (g_win,), (tot2,) = _w_in_grad(u, dz, [_SwapHalves(half2)])
    (dh1, d_mix, do1), (got,) = _mix_in_bwd(dh2, h1, row("mix_norm"), dz, win, [_ToSibling([g_win], ["cols"])])
    wire_in, own_in = _chip_sums(place, [g_win], got, "cols", "w_in_chip_sum")
    (dgp1, dup1, act1), (near_in,) = _ffn_bwd_hidden(do1, gp1, up1, f1[2], "ffn1_bwd_hidden", [_ToChips(wire_in, (0, 1))])
    (dx, d_ffn1), _ = _ffn_bwd_input(dh1, h0, row("ffn1_norm"), dgp1, dup1, f1[0], f1[1], "ffn1_bwd_input")
    go, (far_in,) = _mixer_wgrads(a3, dya, q, dyb, mm, dh2b, [_ToChips(wire_in, (2,))])
    half_in = _totals(place, own_in, [[near_in[0], far_in[0]]], "w_in_total")
    g1g, (tot_in, got_o) = _ffn_wgrad(dgp1, n1, "ffn1_dwg", [_SwapHalves(half_in), _ToSibling(go, ["rows"] * 3)])
    wire_o, own_o = _chip_sums(place, go, got_o, "rows", "mixer_chip_sums")
    g1u, (got_g, land_o) = _ffn_wgrad(dup1, n1, "ffn1_dwu", [_ToSibling([g1g], ["rows"]), _ToChips(wire_o)])
    wire_g, own_g = _chip_sums(place, [g1g], got_g, "rows", "ffn1_dwg_chip_sum")
    half_o = _totals(place, own_o, land_o, "mixer_totals")
    g1d, (got_u, land_g, tot_o) = _ffn_wgrad(act1, do1, "ffn1_dwd",
                                             [_ToSibling([g1u], ["rows"]), _ToChips(wire_g), _SwapHalves(half_o)])
    wire_u, own_u = _chip_sums(place, [g1u], got_u, "rows", "ffn1_dwu_chip_sum")
    half_g = _totals(place, own_g, land_g, "ffn1_dwg_total")
    got_d, land_u, tot_g = _exchange("tail_exchange_1", [_ToSibling([g1d], ["rows"]), _ToChips(wire_u), _SwapHalves(half_g)])
    wire_d, own_d = _chip_sums(place, [g1d], got_d, "rows", "ffn1_dwd_chip_sum")
    half_u = _totals(place, own_u, land_u, "ffn1_dwu_total")
    land_d, tot_u = _exchange("tail_exchange_2", [_ToChips(wire_d), _SwapHalves(half_u)])
    half_d = _totals(place, own_d, land_d, "ffn1_dwd_total")
    (tot_d,) = _exchange("tail_exchange_3", [_SwapHalves(half_d)])
    tot1 = [tot_g[0], tot_u[0], tot_d[0]]
    totals = dict(zip(ffn2 + ["w_in"] + ffn1 + outp, list(tot2) + list(tot_in) + tot1 + list(tot_o)))

    vec_grads = {"ffn1_norm": d_ffn1, "mix_norm": d_mix, "a_dw_b": d_ba, "a_ln_g": d_lng, "a_ln_b": d_lnb,
                 "ffn2_norm": d_ffn2, "final_norm": d_final}
    small = _allreduce_small([vec_grads[n] for n in vecs], d_wa, d_wb)
    taps = ["a_dw_w", "b_conv_w"]
    small_out = _small_adamw(place, small, [[P[n].reshape(1, D) for P in (W, M, V)] for n in vecs],
                             [[P[n] for P in (W, M, V)] for n in taps])

    grads, deltas, new_m, new_v = {}, {}, {}, {}
    for n, (g_, d_, m_, v_) in zip(vecs + taps, small_out):
        shp = W[n].shape
        grads[n], deltas[n], new_m[n], new_v[n] = g_.reshape(shp), d_.reshape(shp), m_.reshape(shp), v_.reshape(shp)
    for group, tag in ((ffn1 + ffn2, "ffn"), (["w_in"], "w_in"), (outp, "mixer")):
        ds, ms, vs = _adamw([quarter(W, n) for n in group], [totals[n] for n in group], [quarter(M, n) for n in group],
                            [quarter(V, n) for n in group], tag + "_adamw")
        for n, d_, m_, v_ in zip(group, ds, ms, vs):
            grads[n], deltas[n], new_m[n], new_v[n] = (unquarter(totals[n], n), unquarter(d_, n), unquarter(m_, n),
                                                       unquarter(v_, n))
    return (loss, dx.reshape(x.shape), *[grads[n] for n in names], *[deltas[n] for n in names],
            *[new_m[n] for n in names], *[new_v[n] for n in names])
```

```python
import functools

import jax
import jax.numpy as jnp
from jax import lax
from jax.experimental import pallas as pl
from jax.experimental.pallas import tpu as pltpu

F32 = jnp.float32
BF16 = jnp.bfloat16
EPS = 1e-6
NS = 4
HALO = 32
MESH = pl.DeviceIdType.MESH
IN_HBM = pl.BlockSpec(memory_space=pltpu.HBM)

ADAM_LR = 0.001
ADAM_B1 = 0.9
ADAM_B2 = 0.999
ADAM_EPS = 1e-08
ADAM_WD = 0.01
ADAM_STEP = 10


def _cparams(n_axes, vmem_mb):
    return pltpu.CompilerParams(dimension_semantics=("arbitrary",) * n_axes, vmem_limit_bytes=vmem_mb << 20)


def _tile(n, t):
    return t if n % t == 0 else n


def _resident(shape):
    return pl.BlockSpec(shape, lambda *_: (0,) * len(shape), pipeline_mode=pl.Buffered(1))


def _row_tile(n, cap=256):
    for t in (256, 176, 128, 64, 32, 16, 8):
        if t <= cap and n % t == 0:
            return t
    return n


def _dot(a, b):
    return jnp.dot(a, b, preferred_element_type=F32)


def _dot_nt(a, b):
    return lax.dot_general(a, b, (((1,), (1,)), ((), ())), preferred_element_type=F32)


def _dot_tn(a, b):
    return lax.dot_general(a, b, (((0,), (0,)), ((), ())), preferred_element_type=F32)


def _sigmoid(x):
    return jax.nn.sigmoid(x)


def _rms_fwd(x, g):
    r = lax.rsqrt(jnp.mean(x * x, axis=-1, keepdims=True) + EPS)
    return x * r * g


def _rms_bwd(x, g, dn):
    r = lax.rsqrt(jnp.mean(x * x, axis=-1, keepdims=True) + EPS)
    xr = x * r
    dg = jnp.sum(dn * xr, axis=0, keepdims=True)
    w = dn * g
    dx = r * w - xr * (r * r) * jnp.mean(x * w, axis=-1, keepdims=True)
    return dx, dg


def _place():
    x, y, c = lax.axis_index("x"), lax.axis_index("y"), lax.axis_index("c")
    chips = [(1 - x, y), (x, 1 - y), (1 - x, 1 - y)]
    return x, y, c, chips


def _quarter_shape(full_shape, kind):
    r, c = full_shape
    return (r // NS, c) if kind == "rows" else (r, c // NS)


def _half_of_quarter(ref, kind, q, pc):
    qr, qc = _quarter_shape(ref.shape, kind)
    h = qr // 2
    if kind == "rows":
        return ref.at[pl.ds(q * qr + pc * h, h), :]
    return ref.at[pl.ds(pc * h, h), pl.ds(q * qc, qc)]


def _quarter(ref, kind, q):
    qr, qc = _quarter_shape(ref.shape, kind)
    if kind == "rows":
        return ref.at[pl.ds(q * qr, qr), :]
    return ref.at[:, pl.ds(q * qc, qc)]


def _rows_half(ref, pc):
    h = ref.shape[0] // 2
    return ref.at[pl.ds(pc * h, h)]


class _Gather:
    def __init__(self, quarters, kinds):
        self.ins = list(quarters)
        self.kinds = list(kinds)
        n = len(self.ins)
        self.out_shape = [jax.ShapeDtypeStruct((NS * a.shape[0], a.shape[1]) if k == "rows" else (a.shape[0], NS * a.shape[1]),
                                               a.dtype) for a, k in zip(self.ins, self.kinds)]
        self.scratch = [pltpu.SemaphoreType.DMA((n, 6)), pltpu.SemaphoreType.DMA((n, 6)), pltpu.SemaphoreType.DMA((n,))]
        self.aliases = {}

    def _copy(self, outs, sems, a, k, q, pc, to, src=None):
        dst = _half_of_quarter(outs[a], self.kinds[a], q, pc)
        return pltpu.make_async_remote_copy(src_ref=dst if src is None else src, dst_ref=dst,
                                            send_sem=sems[0].at[a, k], recv_sem=sems[1].at[a, k],
                                            device_id=to, device_id_type=MESH)

    def _mine(self, ins, outs, sems, a, p):
        return pltpu.make_async_copy(ins[a], _quarter(outs[a], self.kinds[a], p), sems[2].at[a])

    def start(self, ins, outs, sems):
        x, y, c, chips = _place()
        p = 2 * x + y
        for a in range(len(ins)):
            self._mine(ins, outs, sems, a, p).start()
            for j, chip in enumerate(chips):
                self._copy(outs, sems, a, j, p, c, (*chip, c), src=_rows_half(ins[a], c)).start()

    def relay(self, ins, outs, sems):
        x, y, c, chips = _place()
        sibling = (x, y, 1 - c)
        for a in range(len(ins)):
            for j, (qx, qy) in enumerate(chips):
                q = 2 * qx + qy
                self._copy(outs, sems, a, j, q, c, sibling).wait_recv()
                self._copy(outs, sems, a, 3 + j, q, c, sibling).start()

    def finish(self, ins, outs, sems):
        x, y, c, chips = _place()
        p = 2 * x + y
        sibling = (x, y, 1 - c)
        n = len(ins)
        for a in range(n):
            for j, (qx, qy) in enumerate(chips):
                q = 2 * qx + qy
                self._copy(outs, sems, a, 3 + j, q, 1 - c, sibling).wait_recv()
                self._copy(outs, sems, a, j, p, c, (qx, qy, c), src=_rows_half(ins[a], c)).wait_send()
                self._copy(outs, sems, a, 3 + j, q, c, sibling).wait_send()
            self._mine(ins, outs, sems, a, p).wait()


class _ToSibling:
    def __init__(self, grads, kinds):
        self.ins = list(grads)
        self.kinds = list(kinds)
        n = len(self.ins)
        self.out_shape = []
        for g, k in zip(self.ins, self.kinds):
            qr, qc = _quarter_shape(g.shape, k)
            self.out_shape.append(jax.ShapeDtypeStruct((NS, qr // 2, qc), g.dtype))
        self.scratch = [pltpu.SemaphoreType.DMA((n, NS)), pltpu.SemaphoreType.DMA((n, NS))]
        self.aliases = {}

    def _copies(self, ins, outs, sems):
        x, y, c, _ = _place()
        return [pltpu.make_async_remote_copy(src_ref=_half_of_quarter(ins[a], self.kinds[a], q, 1 - c), dst_ref=outs[a].at[q],
                                             send_sem=sems[0].at[a, q], recv_sem=sems[1].at[a, q],
                                             device_id=(x, y, 1 - c), device_id_type=MESH)
                for a in range(len(ins)) for q in range(NS)]

    def start(self, ins, outs, sems):
        for cp in self._copies(ins, outs, sems):
            cp.start()

    def finish(self, ins, outs, sems):
        for cp in self._copies(ins, outs, sems):
            cp.wait()


class _ToChips:
    def __init__(self, sums, which=(0, 1, 2)):
        self.ins = list(sums)
        self.which = tuple(which)
        n, m = len(self.ins), len(self.which)
        self.out_shape = [jax.ShapeDtypeStruct((m,) + s.shape[1:], s.dtype) for s in self.ins]
        self.scratch = [pltpu.SemaphoreType.DMA((n, m)), pltpu.SemaphoreType.DMA((n, m))]
        self.aliases = {}

    def _copies(self, ins, outs, sems):
        x, y, c, chips = _place()
        return [pltpu.make_async_remote_copy(src_ref=ins[a].at[2 * chips[j][0] + chips[j][1]], dst_ref=outs[a].at[k],
                                             send_sem=sems[0].at[a, k], recv_sem=sems[1].at[a, k],
                                             device_id=(*chips[j], c), device_id_type=MESH)
                for a in range(len(ins)) for k, j in enumerate(self.which)]

    def start(self, ins, outs, sems):
        for cp in self._copies(ins, outs, sems):
            cp.start()

    def finish(self, ins, outs, sems):
        for cp in self._copies(ins, outs, sems):
            cp.wait()


class _SwapHalves:
    def __init__(self, quarters):
        self.ins = list(quarters)
        n = len(self.ins)
        self.out_shape = [jax.ShapeDtypeStruct(g.shape, g.dtype) for g in self.ins]
        self.scratch = [pltpu.SemaphoreType.DMA((n,)), pltpu.SemaphoreType.DMA((n,))]
        self.aliases = {a: a for a in range(n)}

    def _copy(self, outs, sems, a, pc):
        x, y, c, _ = _place()
        rows = _rows_half(outs[a], pc)
        return pltpu.make_async_remote_copy(src_ref=rows, dst_ref=rows, send_sem=sems[0].at[a], recv_sem=sems[1].at[a],
                                            device_id=(x, y, 1 - c), device_id_type=MESH)

    def start(self, ins, outs, sems):
        c = lax.axis_index("c")
        for a in range(len(outs)):
            self._copy(outs, sems, a, c).start()

    def finish(self, ins, outs, sems):
        c = lax.axis_index("c")
        for a in range(len(outs)):
            self._copy(outs, sems, a, c).wait_send()
            self._copy(outs, sems, a, 1 - c).wait_recv()


def _call(name, grid, compute, in_specs, out_specs, out_shape, scratch, vmem_mb, args, jobs=()):
    n_in, n_out, n_scr = len(in_specs), len(out_specs), len(scratch)
    ji = [len(j.ins) for j in jobs]
    jo = [len(j.out_shape) for j in jobs]
    js = [len(j.scratch) for j in jobs]

    def body(*refs):
        pos = [0]

        def take(k):
            r = refs[pos[0]:pos[0] + k]
            pos[0] += k
            return r

        ins, jins = take(n_in), [take(k) for k in ji]
        outs, jouts = take(n_out), [take(k) for k in jo]
        scr, jscr = take(n_scr), [take(k) for k in js]
        if jobs and grid:
            ids = [pl.program_id(a) for a in range(len(grid))]
            first = functools.reduce(jnp.logical_and, [i == 0 for i in ids])
            last = functools.reduce(jnp.logical_and, [i == g - 1 for i, g in zip(ids, grid)])

            @pl.when(first)
            def _():
                for j, a, b, c in zip(jobs, jins, jouts, jscr):
                    j.start(a, b, c)

            @pl.when(last)
            def _():
                for j, a, b, c in zip(jobs, jins, jouts, jscr):
                    if hasattr(j, "relay"):
                        j.relay(a, b, c)
        elif jobs:
            for j, a, b, c in zip(jobs, jins, jouts, jscr):
                j.start(a, b, c)
            for j, a, b, c in zip(jobs, jins, jouts, jscr):
                if hasattr(j, "relay"):
                    j.relay(a, b, c)
        compute(ins, outs, scr)
        if jobs and grid:
            @pl.when(last)
            def _():
                for j, a, b, c in zip(jobs, jins, jouts, jscr):
                    j.finish(a, b, c)
        elif jobs:
            for j, a, b, c in zip(jobs, jins, jouts, jscr):
                j.finish(a, b, c)

    aliases = {}
    in_off, out_off = n_in, n_out
    for j, a, b in zip(jobs, ji, jo):
        for s, d in j.aliases.items():
            aliases[in_off + s] = out_off + d
        in_off += a
        out_off += b
    res = pl.pallas_call(
        body, name=name, grid=grid,
        in_specs=list(in_specs) + [IN_HBM] * sum(ji), out_specs=list(out_specs) + [IN_HBM] * sum(jo),
        out_shape=list(out_shape) + [pltpu.HBM(s.shape, s.dtype) for j in jobs for s in j.out_shape],
        scratch_shapes=list(scratch) + [s for j in jobs for s in j.scratch],
        input_output_aliases=aliases, compiler_params=_cparams(len(grid), vmem_mb),
    )(*args, *[a for j in jobs for a in j.ins])
    res = list(res)
    main, rest, jres = res[:n_out], res[n_out:], []
    for k in jo:
        jres.append(rest[:k])
        rest = rest[k:]
    return main, jres


def _exchange(name, jobs):
    return _call(name, (), lambda ins, outs, scr: None, [], [], [], [], 16, [], jobs)[1]


def _small_rows(ka, kb):
    first_a = 8
    first_b = first_a + -(-ka // 8) * 8
    return first_a, first_b, first_b + -(-kb // 8) * 8


LOSS_ROW = 7


def _allreduce_small(vecs, taps_a, taps_b, loss_part):
    n = len(vecs)
    C = vecs[0].shape[1]
    NQ, KA, CB = taps_a.shape
    KB = taps_b.shape[1]
    first_a, first_b, R = _small_rows(KA, KB)
    assert n <= LOSS_ROW < first_a
    N = 8

    def body(*refs):
        vec_refs = refs[:n]
        ta_ref, tb_ref, loss_ref, out_ref, v_ref, gath, send_sems, recv_sems, local_sem = refs[n:]
        v_ref[...] = jnp.zeros_like(v_ref)
        v_ref[LOSS_ROW:LOSS_ROW + 1, 0:loss_ref.shape[1]] = loss_ref[0:1, :]
        for i, r in enumerate(vec_refs):
            v_ref[i:i + 1, :] = r[...]
        for q in range(NQ):
            v_ref[first_a:first_a + KA, q * CB:(q + 1) * CB] = ta_ref[q]
            v_ref[first_b:first_b + KB, q * CB:(q + 1) * CB] = tb_ref[q]
        x, y, c, chips = _place()
        me, sibling = (x, y, c), (x, y, 1 - c)

        def rows(px, py, pc):
            return gath.at[pl.ds((4 * px + 2 * py + pc) * R, R), :]

        def copy(k, block, to, src=None):
            return pltpu.make_async_remote_copy(src_ref=rows(*block) if src is None else src, dst_ref=rows(*block),
                                                send_sem=send_sems.at[k], recv_sem=recv_sems.at[k],
                                                device_id=to, device_id_type=MESH)

        mine = pltpu.make_async_copy(v_ref, rows(*me), local_sem)
        mine.start()
        first = [copy(0, me, sibling, src=v_ref)]
        first += [copy(1 + j, me, (*chip, c), src=v_ref) for j, chip in enumerate(chips)]
        for cp in first:
            cp.start()
        passed = [copy(4 + j, (*chip, c), sibling) for j, chip in enumerate(chips)]
        for j, chip in enumerate(chips):
            copy(1 + j, (*chip, c), me).wait_recv()
            passed[j].start()
        copy(0, sibling, me).wait_recv()
        for j, chip in enumerate(chips):
            copy(4 + j, (*chip, 1 - c), me).wait_recv()
        for cp in first + passed:
            cp.wait_send()
        mine.wait()
        acc = gath[0:R, :]
        for d in range(1, N):
            acc = acc + gath[d * R:(d + 1) * R, :]
        out_ref[...] = acc

    vmem = pl.BlockSpec(memory_space=pltpu.VMEM)
    return pl.pallas_call(
        body, name="allreduce_small",
        in_specs=[vmem] * (n + 3), out_specs=vmem,
        out_shape=jax.ShapeDtypeStruct((R, C), F32),
        scratch_shapes=[pltpu.VMEM((R, C), F32), pltpu.VMEM((N * R, C), F32), pltpu.SemaphoreType.DMA((7,)),
                        pltpu.SemaphoreType.DMA((7,)), pltpu.SemaphoreType.DMA],
    )(*vecs, taps_a, taps_b, loss_part)


def _adamw_math(w, g, m, v):
    c1 = 1.0 - ADAM_B1 ** ADAM_STEP
    c2 = 1.0 - ADAM_B2 ** ADAM_STEP
    mn = ADAM_B1 * m + (1.0 - ADAM_B1) * g
    vn = ADAM_B2 * v + (1.0 - ADAM_B2) * (g * g)
    return -ADAM_LR * ((mn / c1) / (jnp.sqrt(vn / c2) + ADAM_EPS) + ADAM_WD * w), mn, vn


def _small_adamw(place, small, vec_wmv, tap_wmv):
    n = len(vec_wmv)
    D = small.shape[1]
    CB = tap_wmv[0][0].shape[2]
    ks = [t[0].shape[1] for t in tap_wmv]
    firsts = _small_rows(*ks)[:2]

    def body(place_ref, small_ref, *refs):
        ins, outs = refs[:3 * (n + 2)], refs[3 * (n + 2):]
        chip = place_ref[0]
        for i in range(n):
            g = small_ref[i:i + 1, :]
            d, mn, vn = _adamw_math(ins[3 * i][...], g, ins[3 * i + 1][...], ins[3 * i + 2][...])
            for o, val in zip(outs[4 * i:4 * i + 4], (g, d, mn, vn)):
                o[...] = val
        for t, (row0, k) in enumerate(zip(firsts, ks)):
            g = jnp.zeros((k, CB), F32)
            for q in range(D // CB):
                g = g + jnp.where(chip == q, small_ref[row0:row0 + k, q * CB:(q + 1) * CB], 0.0)
            w_ref, m_ref, v_ref = ins[3 * (n + t):3 * (n + t) + 3]
            d, mn, vn = _adamw_math(w_ref[0], g, m_ref[0], v_ref[0])
            for o, val in zip(outs[4 * (n + t):4 * (n + t) + 4], (g, d, mn, vn)):
                o[0] = val

    flat = [a for wmv in list(vec_wmv) + list(tap_wmv) for a in wmv]
    shapes = [jax.ShapeDtypeStruct(wmv[0].shape, F32) for wmv in list(vec_wmv) + list(tap_wmv) for _ in range(4)]
    vmem = pl.BlockSpec(memory_space=pltpu.VMEM)
    res = pl.pallas_call(
        body, name="small_adamw",
        in_specs=[pl.BlockSpec(memory_space=pltpu.SMEM)] + [vmem] * (1 + len(flat)), out_specs=[vmem] * len(shapes),
        out_shape=shapes,
    )(place, small, *flat)
    return [res[4 * i:4 * i + 4] for i in range(n + 2)]


def _ffn_fwd(h, g, wg, wu, wd, name, jobs=(), head=None):
    S, D = h.shape
    F = wg.shape[0]
    ts = _tile(S, 512)
    fb = _tile(F, F // 2)
    nf = F // fb

    def compute(ins, outs, scr):
        h_ref, g_ref, wg_ref, wu_ref, wd_ref = ins[:5]
        n_ref, gp_ref, up_ref = outs[-3:]
        x = h_ref[...]
        n = _rms_fwd(x, g_ref[...]).astype(BF16)
        n_ref[...] = n
        acc = None
        for j in range(nf):
            cols = slice(j * fb, (j + 1) * fb)
            gp = _dot_nt(n, wg_ref[cols, :])
            up = _dot_nt(n, wu_ref[cols, :])
            gp_ref[:, cols] = gp.astype(BF16)
            up_ref[:, cols] = up.astype(BF16)
            part = _dot((gp * _sigmoid(gp) * up).astype(BF16), wd_ref[cols, :])
            acc = part if acc is None else acc + part
        ho = x + 0.5 * acc
        if head is None:
            outs[0][...] = ho
            return
        t_ref, gf_ref = ins[5:]
        dh_ref, do_ref, dgf_ref, loss_ref = outs[:4]

        @pl.when(pl.program_id(0) == 0)
        def _():
            dgf_ref[...] = jnp.zeros_like(dgf_ref)
            loss_ref[...] = jnp.zeros_like(loss_ref)

        err = _rms_fwd(ho, gf_ref[...]) - t_ref[...]
        loss_ref[...] += (0.5 / D) * jnp.sum(err * err)
        dx, dg = _rms_bwd(ho, gf_ref[...], err * (1.0 / D))
        dh_ref[...] = dx
        do_ref[...] = (0.5 * dx).astype(BF16)
        dgf_ref[...] += dg

    tok = pl.BlockSpec((ts, D), lambda i: (i, 0))
    row = pl.BlockSpec((1, D), lambda i: (0, 0))
    wsp = _resident((F, D))
    hid = pl.BlockSpec((ts, F), lambda i: (i, 0))
    saved = [jax.ShapeDtypeStruct((S, D), BF16), jax.ShapeDtypeStruct((S, F), BF16), jax.ShapeDtypeStruct((S, F), BF16)]
    if head is None:
        return _call(name, (S // ts,), compute, [tok, row, wsp, wsp, wsp], [tok, tok, hid, hid],
                     [jax.ShapeDtypeStruct((S, D), F32)] + saved, [], 56, [h, g, wg, wu, wd], jobs)
    return _call(name, (S // ts,), compute, [tok, row, wsp, wsp, wsp, tok, row],
                 [tok, tok, row, pl.BlockSpec((8, 128), lambda i: (0, 0)), tok, hid, hid],
                 [jax.ShapeDtypeStruct((S, D), F32), jax.ShapeDtypeStruct((S, D), BF16), jax.ShapeDtypeStruct((1, D), F32),
                  jax.ShapeDtypeStruct((8, 128), F32)] + saved, [], 60, [h, g, wg, wu, wd, *head], jobs)


def _ffn_bwd_hidden(do, gp, up, wd, name, jobs=()):
    S, D = do.shape
    F = wd.shape[0]
    ts = _tile(S, 512)
    fb = _tile(F, F // 2)

    def compute(ins, outs, scr):
        do_ref, gp_ref, up_ref, wd_ref = ins
        dgp_ref, dup_ref, a_ref = outs
        da = _dot_nt(do_ref[...], wd_ref[...])
        gf = gp_ref[...].astype(F32)
        uf = up_ref[...].astype(F32)
        sg = _sigmoid(gf)
        si = gf * sg
        dgp_ref[...] = (da * uf * (sg * (1.0 + gf * (1.0 - sg)))).astype(BF16)
        dup_ref[...] = (da * si).astype(BF16)
        a_ref[...] = (si * uf).astype(BF16)

    tok = pl.BlockSpec((ts, D), lambda s, i: (i, 0))
    hid = pl.BlockSpec((ts, fb), lambda s, i: (i, s))
    return _call(name, (F // fb, S // ts), compute, [tok, hid, hid, pl.BlockSpec((fb, D), lambda s, i: (s, 0))],
                 [hid, hid, hid], [jax.ShapeDtypeStruct((S, F), BF16)] * 3, [], 56, [do, gp, up, wd], jobs)


def _ffn_bwd_input(dh, h, g, dgp, dup, wg, wu, name, jobs=()):
    S, D = h.shape
    F = wg.shape[0]
    ts = _tile(S, 512)

    def compute(ins, outs, scr):
        dh_ref, h_ref, g_ref, dgp_ref, dup_ref, wg_ref, wu_ref = ins
        dhi_ref, dg_ref = outs

        @pl.when(pl.program_id(0) == 0)
        def _():
            dg_ref[...] = jnp.zeros_like(dg_ref)

        dn = _dot(dgp_ref[...], wg_ref[...]) + _dot(dup_ref[...], wu_ref[...])
        dx, dg = _rms_bwd(h_ref[...], g_ref[...], dn)
        dhi_ref[...] = dh_ref[...] + dx
        dg_ref[...] += dg

    tok = pl.BlockSpec((ts, D), lambda i: (i, 0))
    hid = pl.BlockSpec((ts, F), lambda i: (i, 0))
    row = pl.BlockSpec((1, D), lambda i: (0, 0))
    return _call(name, (S // ts,), compute, [tok, tok, row, hid, hid, _resident((F, D)), _resident((F, D))], [tok, row],
                 [jax.ShapeDtypeStruct((S, D), F32), jax.ShapeDtypeStruct((1, D), F32)], [], 56,
                 [dh, h, g, dgp, dup, wg, wu], jobs)


def _ffn_wgrad(hid, tok, name, jobs=()):
    S, D = tok.shape
    F = hid.shape[1]
    fb = _tile(F, F // 2)

    def compute(ins, outs, scr):
        outs[0][...] = _dot_tn(ins[0][...], ins[1][...])

    main, jres = _call(name, (F // fb,), compute,
                       [pl.BlockSpec((S, fb), lambda j: (0, j)), _resident(tok.shape)],
                       [pl.BlockSpec((fb, D), lambda j: (j, 0))], [jax.ShapeDtypeStruct((F, D), F32)], [], 56,
                       [hid, tok], jobs)
    return main[0], jres


def _mix_in_fwd(h, g, win, jobs=()):
    S, D = h.shape
    NG = win.shape[1] // D
    ts = _tile(S, 512)

    def compute(ins, outs, scr):
        h_ref, g_ref, w_ref = ins
        u_ref, z_ref = outs
        u = _rms_fwd(h_ref[...], g_ref[...]).astype(BF16)
        u_ref[...] = u
        for k in range(NG):
            z_ref[k] = _dot(u, w_ref[:, k * D:(k + 1) * D]).astype(BF16)

    return _call("mix_in_fwd", (S // ts,), compute,
                 [pl.BlockSpec((ts, D), lambda i: (i, 0)), pl.BlockSpec((1, D), lambda i: (0, 0)), _resident(win.shape)],
                 [pl.BlockSpec((ts, D), lambda i: (i, 0)), pl.BlockSpec((NG, ts, D), lambda i: (0, i, 0))],
                 [jax.ShapeDtypeStruct((S, D), BF16), jax.ShapeDtypeStruct((NG, S, D), BF16)],
                 [], 48, [h, g, win], jobs)


SUBLANES = 8


def _shifted_copies(s):
    n = s.shape[1] - SUBLANES
    for r in range(1, SUBLANES):
        s[r, 0:n, :] = s[0, r:r + n, :]


def _window(s, o, rows):
    r = o % SUBLANES
    return s[r, o - r:o - r + rows, :]


def _conv_fwd(z, wa, ba, wb, jobs=()):
    _, S, D = z.shape
    _, KA, CB = wa.shape
    KB = wb.shape[1]
    ts = _tile(S, 512)
    r = ts // HALO
    CH = min(64, ts)

    def compute(ins, outs, scr):
        z_ref, zh_ref, wa_ref, ba_ref, wb_ref = ins
        a1_ref, q_ref = outs
        sa, sb = scr
        keep = (pl.program_id(1) > 0).astype(F32)
        sa[0, HALO:HALO + ts, :] = z_ref[0].astype(F32) * _sigmoid(z_ref[1].astype(F32))
        sa[0, 0:HALO, :] = zh_ref[0].astype(F32) * _sigmoid(zh_ref[1].astype(F32)) * keep
        _shifted_copies(sa)
        sb[HALO:HALO + ts, :] = z_ref[3].astype(F32) * z_ref[4].astype(F32)
        sb[0:HALO, :] = zh_ref[3].astype(F32) * zh_ref[4].astype(F32) * keep
        wak = [wa_ref[0, k:k + 1, :] for k in range(KA)]
        wbk = [wb_ref[0, k:k + 1, :] for k in range(KB)]
        for c0 in range(0, ts, CH):
            acc = jnp.broadcast_to(ba_ref[...], (CH, CB))
            for k in range(KA):
                acc = acc + wak[k] * _window(sa, c0 + HALO - (KA - 1) + k, CH)
            a1_ref[c0:c0 + CH, :] = acc
            v = jnp.zeros((CH, CB), F32)
            for k in range(KB):
                o = c0 + HALO - (KB - 1) + k
                v = v + wbk[k] * sb[o:o + CH, :]
            q_ref[c0:c0 + CH, :] = (z_ref[2, c0:c0 + CH, :].astype(F32) * v).astype(BF16)

    return _call("conv_fwd", (D // CB, S // ts), compute,
                 [pl.BlockSpec((5, ts, CB), lambda j, i: (0, i, j)),
                  pl.BlockSpec((5, HALO, CB), lambda j, i: (0, jnp.maximum(i * r - 1, 0), j)),
                  pl.BlockSpec((1, KA, CB), lambda j, i: (j, 0, 0)), pl.BlockSpec((1, CB), lambda j, i: (0, j)),
                  pl.BlockSpec((1, KB, CB), lambda j, i: (j, 0, 0))],
                 [pl.BlockSpec((ts, CB), lambda j, i: (i, j)), pl.BlockSpec((ts, CB), lambda j, i: (i, j))],
                 [jax.ShapeDtypeStruct((S, D), F32), jax.ShapeDtypeStruct((S, D), BF16)],
                 [pltpu.VMEM((SUBLANES, HALO + ts, CB), F32), pltpu.VMEM((HALO + ts, CB), F32)], 32, [z, z, wa, ba, wb], jobs)


def _ln_stats(a1):
    mu = jnp.mean(a1, axis=-1, keepdims=True)
    xc = a1 - mu
    rstd = lax.rsqrt(jnp.mean(xc * xc, axis=-1, keepdims=True) + EPS)
    return xc * rstd, rstd


def _mix_out_fwd(h1, a1, q, z, lng, lnb, wa, wb, wo):
    S, D = h1.shape
    ts = _tile(S, 512)

    def compute(ins, outs, scr):
        h_ref, a1_ref, q_ref, ga_ref, gb_ref, lng_ref, lnb_ref, wa_ref, wb_ref, wo_ref = ins
        h2_ref, a3_ref, m_ref, ya_ref, yb_ref = outs
        xhat, _ = _ln_stats(a1_ref[...])
        a2 = xhat * lng_ref[...] + lnb_ref[...]
        a3 = (a2 * _sigmoid(a2)).astype(BF16)
        a3_ref[...] = a3
        ya = _dot(a3, wa_ref[...])
        yb = _dot(q_ref[...], wb_ref[...])
        ya_ref[...] = ya.astype(BF16)
        yb_ref[...] = yb.astype(BF16)
        m = (_sigmoid(ga_ref[0].astype(F32)) * ya + _sigmoid(gb_ref[0].astype(F32)) * yb).astype(BF16)
        m_ref[...] = m
        h2_ref[...] = h_ref[...] + _dot(m, wo_ref[...])

    tok = pl.BlockSpec((ts, D), lambda i: (i, 0))
    row = pl.BlockSpec((1, D), lambda i: (0, 0))
    mat = _resident((D, D))
    return _call("mix_out_fwd", (S // ts,), compute,
                 [tok, tok, tok, pl.BlockSpec((1, ts, D), lambda i: (5, i, 0)), pl.BlockSpec((1, ts, D), lambda i: (6, i, 0)),
                  row, row, mat, mat, mat], [tok] * 5,
                 [jax.ShapeDtypeStruct((S, D), F32)] + [jax.ShapeDtypeStruct((S, D), BF16)] * 4,
                 [], 56, [h1, a1, q, z, z, lng, lnb, wa, wb, wo])[0]


def _mix_out_bwd(dh2, a1, z, ya, yb, lng, lnb, wa, wb, wo, jobs=()):
    S, D = dh2.shape
    ts = _tile(S, 512)

    def compute(ins, outs, scr):
        dh_ref, a1_ref, ga_ref, gb_ref, ya_ref, yb_ref, lng_ref, lnb_ref, wa_ref, wb_ref, wo_ref = ins
        da1_ref, dq_ref, dga_ref, dgb_ref, dya_ref, dyb_ref, dhb_ref, dlg_ref, dlb_ref = outs

        @pl.when(pl.program_id(0) == 0)
        def _():
            dlg_ref[...] = jnp.zeros_like(dlg_ref)
            dlb_ref[...] = jnp.zeros_like(dlb_ref)

        dhb = dh_ref[...].astype(BF16)
        dhb_ref[...] = dhb
        dm = _dot_nt(dhb, wo_ref[...])
        sa = _sigmoid(ga_ref[0].astype(F32))
        sb = _sigmoid(gb_ref[0].astype(F32))
        dga_ref[...] = (dm * ya_ref[...].astype(F32) * sa * (1.0 - sa)).astype(BF16)
        dgb_ref[...] = (dm * yb_ref[...].astype(F32) * sb * (1.0 - sb)).astype(BF16)
        dya = (sa * dm).astype(BF16)
        dyb = (sb * dm).astype(BF16)
        dya_ref[...] = dya
        dyb_ref[...] = dyb
        dq_ref[...] = _dot_nt(dyb, wb_ref[...])
        da3 = _dot_nt(dya, wa_ref[...])
        xhat, rstd = _ln_stats(a1_ref[...])
        a2 = xhat * lng_ref[...] + lnb_ref[...]
        sg = _sigmoid(a2)
        da2 = da3 * (sg * (1.0 + a2 * (1.0 - sg)))
        dlg_ref[...] += jnp.sum(da2 * xhat, axis=0, keepdims=True)
        dlb_ref[...] += jnp.sum(da2, axis=0, keepdims=True)
        dxh = da2 * lng_ref[...]
        da1_ref[...] = rstd * (dxh - jnp.mean(dxh, axis=-1, keepdims=True)
                               - xhat * jnp.mean(dxh * xhat, axis=-1, keepdims=True))

    tok = pl.BlockSpec((ts, D), lambda i: (i, 0))
    row = pl.BlockSpec((1, D), lambda i: (0, 0))
    mat = _resident((D, D))
    return _call("mix_out_bwd", (S // ts,), compute,
                 [tok, tok, pl.BlockSpec((1, ts, D), lambda i: (5, i, 0)), pl.BlockSpec((1, ts, D), lambda i: (6, i, 0)),
                  tok, tok, row, row, mat, mat, mat], [tok] * 7 + [row, row],
                 [jax.ShapeDtypeStruct((S, D), F32), jax.ShapeDtypeStruct((S, D), F32)]
                 + [jax.ShapeDtypeStruct((S, D), BF16)] * 5 + [jax.ShapeDtypeStruct((1, D), F32)] * 2,
                 [], 56, [dh2, a1, z, z, ya, yb, lng, lnb, wa, wb, wo], jobs)


def _mixer_wgrads(a3, dya, q, dyb, mm, dhb, jobs=()):
    S, D = a3.shape
    tk = _tile(S, 512)

    def compute(ins, outs, scr):
        @pl.when(pl.program_id(0) == 0)
        def _():
            for o in outs:
                o[...] = jnp.zeros_like(o)

        for t in range(3):
            outs[t][...] += _dot_tn(ins[2 * t][...], ins[2 * t + 1][...])

    tok = pl.BlockSpec((tk, D), lambda k: (k, 0))
    return _call("mixer_wgrads", (S // tk,), compute, [tok] * 6, [pl.BlockSpec((D, D), lambda k: (0, 0))] * 3,
                 [jax.ShapeDtypeStruct((D, D), F32)] * 3, [], 56, [a3, dya, q, dyb, mm, dhb], jobs)


def _conv_bwd(z, da1, dq, dga, dgb, wa, wb, jobs=()):
    NG, S, D = z.shape
    _, KA, CB = wa.shape
    KB = wb.shape[1]
    ts = _tile(S, 512)
    r = ts // HALO
    nt = S // ts
    CH = min(64, ts)
    last_halo = S // HALO - 1

    def compute(ins, outs, scr):
        z_ref, zp_ref, zn_ref, da1_ref, da1n_ref, dq_ref, dqn_ref, dga_ref, dgb_ref, wa_ref, wb_ref = ins
        dz_ref, dwa_ref, dba_ref, dwb_ref = outs
        sa0, sd, sp, sv, acca, accb = scr
        i = pl.program_id(1)
        prev = (i > 0).astype(F32)
        nxt = (i < nt - 1).astype(F32)

        @pl.when(i == 0)
        def _():
            acca[...] = jnp.zeros_like(acca)
            accb[...] = jnp.zeros_like(accb)
            dba_ref[...] = jnp.zeros_like(dba_ref)

        sa0[0, HALO:HALO + ts, :] = z_ref[0].astype(F32) * _sigmoid(z_ref[1].astype(F32))
        sa0[0, 0:HALO, :] = zp_ref[0].astype(F32) * _sigmoid(zp_ref[1].astype(F32)) * prev
        _shifted_copies(sa0)
        sp[HALO:HALO + ts, :] = z_ref[3].astype(F32) * z_ref[4].astype(F32)
        sp[0:HALO, :] = zp_ref[3].astype(F32) * zp_ref[4].astype(F32) * prev
        sd[0, 0:ts, :] = da1_ref[...]
        sd[0, ts:ts + HALO, :] = da1n_ref[...] * nxt
        _shifted_copies(sd)
        sv[0:ts, :] = dq_ref[...] * z_ref[2].astype(F32)
        sv[ts:ts + HALO, :] = dqn_ref[...] * zn_ref[2].astype(F32) * nxt
        dba_ref[...] += jnp.sum(da1_ref[...], axis=0, keepdims=True)
        wak = [wa_ref[0, k:k + 1, :] for k in range(KA)]
        wbk = [wb_ref[0, k:k + 1, :] for k in range(KB)]
        for c0 in range(0, ts, CH):
            rows = slice(c0, c0 + CH)
            d1 = sd[0, rows, :]
            da0 = jnp.zeros((CH, CB), F32)
            for k in range(KA):
                da0 = da0 + wak[k] * _window(sd, c0 + (KA - 1) - k, CH)
                a0w = _window(sa0, c0 + HALO - (KA - 1) + k, CH)
                acca[k] += jnp.sum((d1 * a0w).reshape(CH // 8, 8, CB), axis=0)
            val = z_ref[0, rows, :].astype(F32)
            sg = _sigmoid(z_ref[1, rows, :].astype(F32))
            dz_ref[0, rows, :] = (da0 * sg).astype(BF16)
            dz_ref[1, rows, :] = (da0 * val * sg * (1.0 - sg)).astype(BF16)
            dv = sv[rows, :]
            v = jnp.zeros((CH, CB), F32)
            dp = jnp.zeros((CH, CB), F32)
            for k in range(KB):
                o = c0 + HALO - (KB - 1) + k
                pw = sp[o:o + CH, :]
                v = v + wbk[k] * pw
                accb[k] += jnp.sum((dv * pw).reshape(CH // 8, 8, CB), axis=0)
                o = c0 + (KB - 1) - k
                dp = dp + wbk[k] * sv[o:o + CH, :]
            dz_ref[2, rows, :] = (dq_ref[rows, :] * v).astype(BF16)
            dz_ref[3, rows, :] = (dp * z_ref[4, rows, :].astype(F32)).astype(BF16)
            dz_ref[4, rows, :] = (dp * z_ref[3, rows, :].astype(F32)).astype(BF16)
        dz_ref[5] = dga_ref[...]
        dz_ref[6] = dgb_ref[...]

        @pl.when(i == nt - 1)
        def _():
            dwa_ref[0] = jnp.sum(acca[...], axis=1)
            dwb_ref[0] = jnp.sum(accb[...], axis=1)

    zt = pl.BlockSpec((5, ts, CB), lambda j, i: (0, i, j))
    zp = pl.BlockSpec((5, HALO, CB), lambda j, i: (0, jnp.maximum(i * r - 1, 0), j))
    zn = pl.BlockSpec((5, HALO, CB), lambda j, i: (0, jnp.minimum((i + 1) * r, last_halo), j))
    tok = pl.BlockSpec((ts, CB), lambda j, i: (i, j))
    tokn = pl.BlockSpec((HALO, CB), lambda j, i: (jnp.minimum((i + 1) * r, last_halo), j))
    return _call("conv_bwd", (D // CB, nt), compute,
                 [zt, zp, zn, tok, tokn, tok, tokn, tok, tok,
                  pl.BlockSpec((1, KA, CB), lambda j, i: (j, 0, 0)), pl.BlockSpec((1, KB, CB), lambda j, i: (j, 0, 0))],
                 [pl.BlockSpec((NG, ts, CB), lambda j, i: (0, i, j)), pl.BlockSpec((1, KA, CB), lambda j, i: (j, 0, 0)),
                  pl.BlockSpec((1, CB), lambda j, i: (0, j)), pl.BlockSpec((1, KB, CB), lambda j, i: (j, 0, 0))],
                 [jax.ShapeDtypeStruct((NG, S, D), BF16), jax.ShapeDtypeStruct((D // CB, KA, CB), F32),
                  jax.ShapeDtypeStruct((1, D), F32), jax.ShapeDtypeStruct((D // CB, KB, CB), F32)],
                 [pltpu.VMEM((SUBLANES, HALO + ts, CB), F32), pltpu.VMEM((SUBLANES, ts + HALO, CB), F32),
                  pltpu.VMEM((HALO + ts, CB), F32), pltpu.VMEM((ts + HALO, CB), F32),
                  pltpu.VMEM((KA, 8, CB), F32), pltpu.VMEM((KB, 8, CB), F32)],
                 40, [z, z, z, da1, da1, dq, dq, dga, dgb, wa, wb], jobs)


def _mix_in_bwd(dh2, h1, g, dz, win, jobs=()):
    S, D = h1.shape
    NG = dz.shape[0]
    ts = _tile(S, 512)

    def compute(ins, outs, scr):
        dh_ref, h_ref, g_ref, dz_ref, w_ref = ins
        dhi_ref, dg_ref, do_ref = outs

        @pl.when(pl.program_id(0) == 0)
        def _():
            dg_ref[...] = jnp.zeros_like(dg_ref)

        du = _dot_nt(dz_ref[0], w_ref[:, 0:D])
        for k in range(1, NG):
            du = du + _dot_nt(dz_ref[k], w_ref[:, k * D:(k + 1) * D])
        dx, dg = _rms_bwd(h_ref[...], g_ref[...], du)
        dhi = dh_ref[...] + dx
        dhi_ref[...] = dhi
        do_ref[...] = (0.5 * dhi).astype(BF16)
        dg_ref[...] += dg

    tok = pl.BlockSpec((ts, D), lambda i: (i, 0))
    row = pl.BlockSpec((1, D), lambda i: (0, 0))
    return _call("mix_in_bwd", (S // ts,), compute,
                 [tok, tok, row, pl.BlockSpec((NG, ts, D), lambda i: (0, i, 0)), _resident(win.shape)],
                 [tok, row, tok],
                 [jax.ShapeDtypeStruct((S, D), F32), jax.ShapeDtypeStruct((1, D), F32), jax.ShapeDtypeStruct((S, D), BF16)],
                 [], 56, [dh2, h1, g, dz, win], jobs)


def _w_in_grad(u, dz, jobs=()):
    S, D = u.shape
    NG = dz.shape[0]

    def compute(ins, outs, scr):
        outs[0][...] = _dot_tn(ins[0][...], ins[1][0])

    return _call("w_in_grad", (NG,), compute,
                 [_resident(u.shape), pl.BlockSpec((1, S, D), lambda j: (j, 0, 0))],
                 [pl.BlockSpec((D, D), lambda j: (0, j))], [jax.ShapeDtypeStruct((D, NG * D), F32)], [], 48, [u, dz], jobs)


def _chip_sums(place, grads, got, kind, name):
    n = len(grads)
    qr, qc = _quarter_shape(grads[0].shape, kind)
    h = qr // 2
    tr = _row_tile(h)
    nr = h // tr

    def body(pc_ref, *refs):
        g_refs, got_refs, b_refs, f_refs = refs[:n], refs[n:2 * n], refs[2 * n:3 * n], refs[3 * n:]
        own = pl.program_id(1) == pc_ref[0]
        for a in range(n):
            s = g_refs[a][...] + got_refs[a][0]
            b_refs[a][0] = s.astype(BF16)

            @pl.when(own)
            def _():
                f_refs[a][...] = s

    if kind == "rows":
        gspec = pl.BlockSpec((tr, qc), lambda r, q, pc: (q * (2 * nr) + pc[1] * nr + r, 0))
    else:
        gspec = pl.BlockSpec((tr, qc), lambda r, q, pc: (pc[1] * nr + r, q))
    lspec = pl.BlockSpec((1, tr, qc), lambda r, q, pc: (q, r, 0))
    res = pl.pallas_call(
        body, name=name,
        grid_spec=pltpu.PrefetchScalarGridSpec(
            num_scalar_prefetch=1, grid=(nr, NS), in_specs=[gspec] * n + [lspec] * n,
            out_specs=[lspec] * n + [pl.BlockSpec((tr, qc), lambda r, q, pc: (r, 0))] * n),
        out_shape=[jax.ShapeDtypeStruct((NS, h, qc), BF16)] * n + [jax.ShapeDtypeStruct((h, qc), F32)] * n,
        compiler_params=_cparams(2, 48),
    )(place, *grads, *got)
    return res[:n], res[n:]


def _totals(place, own, got, name):
    n = len(own)
    h, qc = own[0].shape
    tr = _row_tile(h)
    nr = h // tr
    got = [list(g) if isinstance(g, (list, tuple)) else [g] for g in got]
    m = len(got[0])

    def body(pc_ref, *refs):
        own_refs, got_refs, o_refs = refs[:n], refs[n:n + n * m], refs[n + n * m:]
        for a in range(n):
            acc = own_refs[a][...]
            for g in got_refs[a * m:(a + 1) * m]:
                for k in range(g.shape[0]):
                    acc = acc + g[k].astype(F32)
            o_refs[a][...] = acc

    lands = [pl.BlockSpec((g.shape[0], tr, qc), lambda r, pc: (0, r, 0)) for gs in got for g in gs]
    return pl.pallas_call(
        body, name=name,
        grid_spec=pltpu.PrefetchScalarGridSpec(
            num_scalar_prefetch=1, grid=(nr,),
            in_specs=[pl.BlockSpec((tr, qc), lambda r, pc: (r, 0))] * n + lands,
            out_specs=[pl.BlockSpec((tr, qc), lambda r, pc: (pc[1] * nr + r, 0))] * n),
        out_shape=[jax.ShapeDtypeStruct((2 * h, qc), F32)] * n,
        compiler_params=_cparams(1, 48),
    )(place, *own, *[g for gs in got for g in gs])


def _adamw(ws, gs, ms, vs, name):
    n = len(ws)
    R, C = ws[0].shape
    tr = _row_tile(R, (36 << 20) // (7 * 2 * 4 * n * C))

    def body(*refs):
        w_refs, g_refs, m_refs, v_refs = refs[:n], refs[n:2 * n], refs[2 * n:3 * n], refs[3 * n:4 * n]
        d_refs, mo_refs, vo_refs = refs[4 * n:5 * n], refs[5 * n:6 * n], refs[6 * n:]
        for a in range(n):
            d_refs[a][...], mo_refs[a][...], vo_refs[a][...] = _adamw_math(w_refs[a][...], g_refs[a][...], m_refs[a][...],
                                                                         v_refs[a][...])

    blk = pl.BlockSpec((tr, C), lambda r: (r, 0))
    res = pl.pallas_call(
        body, name=name, grid=(R // tr,),
        in_specs=[blk] * (4 * n), out_specs=[blk] * (3 * n),
        out_shape=[jax.ShapeDtypeStruct((R, C), F32)] * (3 * n),
        compiler_params=_cparams(1, 56),
    )(*ws, *gs, *ms, *vs)
    return res[:n], res[n:2 * n], res[2 * n:]


def kernel(x, ffn1_norm, ffn1_w_gate, ffn1_w_up, ffn1_w_down, mix_norm, w_in, a_dw_w, a_dw_b, a_ln_g, a_ln_b, a_w_out, b_conv_w, b_w_out, w_o, ffn2_norm, ffn2_w_gate, ffn2_w_up, ffn2_w_down, final_norm, loss_target, m_ffn1_norm, m_ffn1_w_gate, m_ffn1_w_up, m_ffn1_w_down, m_mix_norm, m_w_in, m_a_dw_w, m_a_dw_b, m_a_ln_g, m_a_ln_b, m_a_w_out, m_b_conv_w, m_b_w_out, m_w_o, m_ffn2_norm, m_ffn2_w_gate, m_ffn2_w_up, m_ffn2_w_down, m_final_norm, v_ffn1_norm, v_ffn1_w_gate, v_ffn1_w_up, v_ffn1_w_down, v_mix_norm, v_w_in, v_a_dw_w, v_a_dw_b, v_a_ln_g, v_a_ln_b, v_a_w_out, v_b_conv_w, v_b_w_out, v_w_o, v_ffn2_norm, v_ffn2_w_gate, v_ffn2_w_up, v_ffn2_w_down, v_final_norm):
    names = ["ffn1_norm", "ffn1_w_gate", "ffn1_w_up", "ffn1_w_down", "mix_norm", "w_in", "a_dw_w", "a_dw_b", "a_ln_g",
             "a_ln_b", "a_w_out", "b_conv_w", "b_w_out", "w_o", "ffn2_norm", "ffn2_w_gate", "ffn2_w_up", "ffn2_w_down",
             "final_norm"]
    W = dict(zip(names, [ffn1_norm, ffn1_w_gate, ffn1_w_up, ffn1_w_down, mix_norm, w_in, a_dw_w, a_dw_b, a_ln_g, a_ln_b,
                         a_w_out, b_conv_w, b_w_out, w_o, ffn2_norm, ffn2_w_gate, ffn2_w_up, ffn2_w_down, final_norm]))
    M = dict(zip(names, [m_ffn1_norm, m_ffn1_w_gate, m_ffn1_w_up, m_ffn1_w_down, m_mix_norm, m_w_in, m_a_dw_w, m_a_dw_b,
                         m_a_ln_g, m_a_ln_b, m_a_w_out, m_b_conv_w, m_b_w_out, m_w_o, m_ffn2_norm, m_ffn2_w_gate,
                         m_ffn2_w_up, m_ffn2_w_down, m_final_norm]))
    V = dict(zip(names, [v_ffn1_norm, v_ffn1_w_gate, v_ffn1_w_up, v_ffn1_w_down, v_mix_norm, v_w_in, v_a_dw_w, v_a_dw_b,
                         v_a_ln_g, v_a_ln_b, v_a_w_out, v_b_conv_w, v_b_w_out, v_w_o, v_ffn2_norm, v_ffn2_w_gate,
                         v_ffn2_w_up, v_ffn2_w_down, v_final_norm]))
    transposed = ("ffn1_w_gate", "ffn1_w_up", "ffn2_w_gate", "ffn2_w_up")
    vecs = ["ffn1_norm", "mix_norm", "a_dw_b", "a_ln_g", "a_ln_b", "ffn2_norm", "final_norm"]
    ffn1 = ["ffn1_w_gate", "ffn1_w_up", "ffn1_w_down"]
    ffn2 = ["ffn2_w_gate", "ffn2_w_up", "ffn2_w_down"]
    outp = ["a_w_out", "b_w_out", "w_o"]

    S, D = x.shape[1], x.shape[2]
    CB = D // NS
    KA, KB = a_dw_w.shape[1], b_conv_w.shape[1]
    px, py, pc = lax.axis_index("x"), lax.axis_index("y"), lax.axis_index("c")
    chip = 2 * px + py
    place = jnp.stack([chip, pc]).astype(jnp.int32)
    h0 = x.reshape(S, D)
    tgt = loss_target.reshape(S, D)
    row = lambda n: pltpu.with_memory_space_constraint(W[n].reshape(1, D), pltpu.HBM)
    pad = lambda a, r: jnp.concatenate([a, jnp.zeros((r - a.shape[0], a.shape[1]), F32)], axis=0)

    def quarter(P, n):
        return jnp.transpose(P[n][0]) if n in transposed else P[n][0]

    def unquarter(a, n):
        return (jnp.transpose(a) if n in transposed else a).reshape(W[n].shape)

    wq = {n: quarter(W, n).astype(BF16) for n in ffn1 + ffn2 + outp + ["w_in"]}

    f1 = _exchange("gather_ffn1", [_Gather([wq[n] for n in ffn1], ["rows"] * 3)])[0]
    g_in = _Gather([wq["w_in"], pad(a_dw_w[0], 32), pad(b_conv_w[0], 16)], ["cols", "rows", "rows"])
    (h1, n1, gp1, up1), ((win, taps_a, taps_b),) = _ffn_fwd(h0, row("ffn1_norm"), *f1, "ffn1_fwd", [g_in])
    wa_taps = taps_a.reshape(NS, 32, CB)[:, :KA]
    wb_taps = taps_b.reshape(NS, 16, CB)[:, :KB]
    g_out = _Gather([wq[n] for n in outp] + [wq["ffn2_w_gate"]], ["rows"] * 4)
    (u, z), ((wa_out, wb_out, wo, f2g),) = _mix_in_fwd(h1, row("mix_norm"), win, [g_out])
    g_f2 = _Gather([wq["ffn2_w_up"], wq["ffn2_w_down"]], ["rows"] * 2)
    (a1, q), ((f2u, f2d),) = _conv_fwd(z, wa_taps, row("a_dw_b"), wb_taps, [g_f2])
    h2, a3, mm, ya, yb = _mix_out_fwd(h1, a1, q, z, row("a_ln_g"), row("a_ln_b"), wa_out, wb_out, wo)
    (dh3, do2, d_final, loss_part, n2, gp2, up2), _ = _ffn_fwd(h2, row("ffn2_norm"), f2g, f2u, f2d, "ffn2_fwd_loss",
                                                               head=(tgt, row("final_norm")))

    (dgp2, dup2, act2), _ = _ffn_bwd_hidden(do2, gp2, up2, f2d, "ffn2_bwd_hidden")
    (dh2, d_ffn2), _ = _ffn_bwd_input(dh3, h2, row("ffn2_norm"), dgp2, dup2, f2g, f2u, "ffn2_bwd_input")
    g2 = [_ffn_wgrad(dgp2, n2, "ffn2_dwg")[0], _ffn_wgrad(dup2, n2, "ffn2_dwu")[0], _ffn_wgrad(act2, do2, "ffn2_dwd")[0]]
    (da1, dq, dga, dgb, dya, dyb, dh2b, d_lng, d_lnb), (got,) = _mix_out_bwd(
        dh2, a1, z, ya, yb, row("a_ln_g"), row("a_ln_b"), wa_out, wb_out, wo, [_ToSibling(g2, ["rows"] * 3)])
    wire2, own2 = _chip_sums(place, g2, got, "rows", "ffn2_chip_sums")
    (dz, d_wa, d_ba, d_wb), (got,) = _conv_bwd(z, da1, dq, dga, dgb, wa_taps, wb_taps, [_ToChips(wire2)])
    half2 = _totals(place, own2, got, "ffn2_totals")
    (g_win,), (tot2,) = _w_in_grad(u, dz, [_SwapHalves(half2)])
    (dh1, d_mix, do1), (got,) = _mix_in_bwd(dh2, h1, row("mix_norm"), dz, win, [_ToSibling([g_win], ["cols"])])
    wire_in, own_in = _chip_sums(place, [g_win], got, "cols", "w_in_chip_sum")
    (dgp1, dup1, act1), (near_in,) = _ffn_bwd_hidden(do1, gp1, up1, f1[2], "ffn1_bwd_hidden", [_ToChips(wire_in, (0, 1))])
    (dx, d_ffn1), _ = _ffn_bwd_input(dh1, h0, row("ffn1_norm"), dgp1, dup1, f1[0], f1[1], "ffn1_bwd_input")
    go, (far_in,) = _mixer_wgrads(a3, dya, q, dyb, mm, dh2b, [_ToChips(wire_in, (2,))])
    half_in = _totals(place, own_in, [[near_in[0], far_in[0]]], "w_in_total")
    g1g, (tot_in, got_o) = _ffn_wgrad(dgp1, n1, "ffn1_dwg", [_SwapHalves(half_in), _ToSibling(go, ["rows"] * 3)])
    wire_o, own_o = _chip_sums(place, go, got_o, "rows", "mixer_chip_sums")
    g1u, (got_g, land_o) = _ffn_wgrad(dup1, n1, "ffn1_dwu", [_ToSibling([g1g], ["rows"]), _ToChips(wire_o)])
    wire_g, own_g = _chip_sums(place, [g1g], got_g, "rows", "ffn1_dwg_chip_sum")
    half_o = _totals(place, own_o, land_o, "mixer_totals")
    g1d, (got_u, land_g, tot_o) = _ffn_wgrad(act1, do1, "ffn1_dwd",
                                             [_ToSibling([g1u], ["rows"]), _ToChips(wire_g), _SwapHalves(half_o)])
    wire_u, own_u = _chip_sums(place, [g1u], got_u, "rows", "ffn1_dwu_chip_sum")
    half_g = _totals(place, own_g, land_g, "ffn1_dwg_total")
    got_d, land_u, tot_g = _exchange("tail_exchange_1", [_ToSibling([g1d], ["rows"]), _ToChips(wire_u), _SwapHalves(half_g)])
    wire_d, own_d = _chip_sums(place, [g1d], got_d, "rows", "ffn1_dwd_chip_sum")
    half_u = _totals(place, own_u, land_u, "ffn1_dwu_total")
    land_d, tot_u = _exchange("tail_exchange_2", [_ToChips(wire_d), _SwapHalves(half_u)])
    half_d = _totals(place, own_d, land_d, "ffn1_dwd_total")
    (tot_d,) = _exchange("tail_exchange_3", [_SwapHalves(half_d)])
    tot1 = [tot_g[0], tot_u[0], tot_d[0]]
    totals = dict(zip(ffn2 + ["w_in"] + ffn1 + outp, list(tot2) + list(tot_in) + tot1 + list(tot_o)))

    vec_grads = {"ffn1_norm": d_ffn1, "mix_norm": d_mix, "a_dw_b": d_ba, "a_ln_g": d_lng, "a_ln_b": d_lnb,
                 "ffn2_norm": d_ffn2, "final_norm": d_final}
    small = _allreduce_small([vec_grads[n] for n in vecs], d_wa, d_wb, loss_part)
    loss = small[LOSS_ROW, 0]
    taps = ["a_dw_w", "b_conv_w"]
    small_out = _small_adamw(place, small, [[P[n].reshape(1, D) for P in (W, M, V)] for n in vecs],
                             [[P[n] for P in (W, M, V)] for n in taps])

    grads, deltas, new_m, new_v = {}, {}, {}, {}
    for n, (g_, d_, m_, v_) in zip(vecs + taps, small_out):
        shp = W[n].shape
        grads[n], deltas[n], new_m[n], new_v[n] = g_.reshape(shp), d_.reshape(shp), m_.reshape(shp), v_.reshape(shp)
    for group, tag in ((ffn1 + ffn2, "ffn"), (["w_in"], "w_in"), (outp, "mixer")):
        ds, ms, vs = _adamw([quarter(W, n) for n in group], [totals[n] for n in group], [quarter(M, n) for n in group],
                            [quarter(V, n) for n in group], tag + "_adamw")
        for n, d_, m_, v_ in zip(group, ds, ms, vs):
            grads[n], deltas[n], new_m[n], new_v[n] = (unquarter(totals[n], n), unquarter(d_, n), unquarter(m_, n),
                                                       unquarter(v_, n))
    return (loss, dx.reshape(x.shape), *[grads[n] for n in names], *[deltas[n] for n in names],
            *[new_m[n] for n in names], *[new_v[n] for n in names])
```

```python
import functools

import jax
import jax.numpy as jnp
from jax import lax
from jax.experimental import pallas as pl
from jax.experimental.pallas import tpu as pltpu

F32 = jnp.float32
BF16 = jnp.bfloat16
EPS = 1e-6
NS = 4
HALO = 32
MESH = pl.DeviceIdType.MESH
IN_HBM = pl.BlockSpec(memory_space=pltpu.HBM)

ADAM_LR = 0.001
ADAM_B1 = 0.9
ADAM_B2 = 0.999
ADAM_EPS = 1e-08
ADAM_WD = 0.01
ADAM_STEP = 10


def _cparams(n_axes, vmem_mb):
    return pltpu.CompilerParams(dimension_semantics=("arbitrary",) * n_axes, vmem_limit_bytes=vmem_mb << 20)


def _tile(n, t):
    return t if n % t == 0 else n


def _resident(shape):
    return pl.BlockSpec(shape, lambda *_: (0,) * len(shape), pipeline_mode=pl.Buffered(1))


def _row_tile(n, cap=256):
    for t in (256, 176, 128, 64, 32, 16, 8):
        if t <= cap and n % t == 0:
            return t
    return n


def _dot(a, b):
    return jnp.dot(a, b, preferred_element_type=F32)


def _dot_nt(a, b):
    return lax.dot_general(a, b, (((1,), (1,)), ((), ())), preferred_element_type=F32)


def _dot_tn(a, b):
    return lax.dot_general(a, b, (((0,), (0,)), ((), ())), preferred_element_type=F32)


def _sigmoid(x):
    return jax.nn.sigmoid(x)


def _rms_fwd(x, g):
    r = lax.rsqrt(jnp.mean(x * x, axis=-1, keepdims=True) + EPS)
    return x * r * g


def _rms_bwd(x, g, dn):
    r = lax.rsqrt(jnp.mean(x * x, axis=-1, keepdims=True) + EPS)
    xr = x * r
    dg = jnp.sum(dn * xr, axis=0, keepdims=True)
    w = dn * g
    dx = r * w - xr * (r * r) * jnp.mean(x * w, axis=-1, keepdims=True)
    return dx, dg


def _place():
    x, y, c = lax.axis_index("x"), lax.axis_index("y"), lax.axis_index("c")
    chips = [(1 - x, y), (x, 1 - y), (1 - x, 1 - y)]
    return x, y, c, chips


def _quarter_shape(full_shape, kind):
    r, c = full_shape
    return (r // NS, c) if kind == "rows" else (r, c // NS)


def _half_of_quarter(ref, kind, q, pc):
    qr, qc = _quarter_shape(ref.shape, kind)
    h = qr // 2
    if kind == "rows":
        return ref.at[pl.ds(q * qr + pc * h, h), :]
    return ref.at[pl.ds(pc * h, h), pl.ds(q * qc, qc)]


def _quarter(ref, kind, q):
    qr, qc = _quarter_shape(ref.shape, kind)
    if kind == "rows":
        return ref.at[pl.ds(q * qr, qr), :]
    return ref.at[:, pl.ds(q * qc, qc)]


def _rows_half(ref, pc):
    h = ref.shape[0] // 2
    return ref.at[pl.ds(pc * h, h)]


class _Gather:
    def __init__(self, quarters, kinds):
        self.ins = list(quarters)
        self.kinds = list(kinds)
        n = len(self.ins)
        self.out_shape = [jax.ShapeDtypeStruct((NS * a.shape[0], a.shape[1]) if k == "rows" else (a.shape[0], NS * a.shape[1]),
                                               a.dtype) for a, k in zip(self.ins, self.kinds)]
        self.scratch = [pltpu.SemaphoreType.DMA((n, 6)), pltpu.SemaphoreType.DMA((n, 6)), pltpu.SemaphoreType.DMA((n,))]
        self.aliases = {}

    def _copy(self, outs, sems, a, k, q, pc, to, src=None):
        dst = _half_of_quarter(outs[a], self.kinds[a], q, pc)
        return pltpu.make_async_remote_copy(src_ref=dst if src is None else src, dst_ref=dst,
                                            send_sem=sems[0].at[a, k], recv_sem=sems[1].at[a, k],
                                            device_id=to, device_id_type=MESH)

    def _mine(self, ins, outs, sems, a, p):
        return pltpu.make_async_copy(ins[a], _quarter(outs[a], self.kinds[a], p), sems[2].at[a])

    def start(self, ins, outs, sems):
        x, y, c, chips = _place()
        p = 2 * x + y
        for a in range(len(ins)):
            self._mine(ins, outs, sems, a, p).start()
            for j, chip in enumerate(chips):
                self._copy(outs, sems, a, j, p, c, (*chip, c), src=_rows_half(ins[a], c)).start()

    def relay(self, ins, outs, sems):
        x, y, c, chips = _place()
        sibling = (x, y, 1 - c)
        for a in range(len(ins)):
            for j, (qx, qy) in enumerate(chips):
                q = 2 * qx + qy
                self._copy(outs, sems, a, j, q, c, sibling).wait_recv()
                self._copy(outs, sems, a, 3 + j, q, c, sibling).start()

    def finish(self, ins, outs, sems):
        x, y, c, chips = _place()
        p = 2 * x + y
        sibling = (x, y, 1 - c)
        n = len(ins)
        for a in range(n):
            for j, (qx, qy) in enumerate(chips):
                q = 2 * qx + qy
                self._copy(outs, sems, a, 3 + j, q, 1 - c, sibling).wait_recv()
                self._copy(outs, sems, a, j, p, c, (qx, qy, c), src=_rows_half(ins[a], c)).wait_send()
                self._copy(outs, sems, a, 3 + j, q, c, sibling).wait_send()
            self._mine(ins, outs, sems, a, p).wait()


class _ToSibling:
    def __init__(self, grads, kinds):
        self.ins = list(grads)
        self.kinds = list(kinds)
        n = len(self.ins)
        self.out_shape = []
        for g, k in zip(self.ins, self.kinds):
            qr, qc = _quarter_shape(g.shape, k)
            self.out_shape.append(jax.ShapeDtypeStruct((NS, qr // 2, qc), g.dtype))
        self.scratch = [pltpu.SemaphoreType.DMA((n, NS)), pltpu.SemaphoreType.DMA((n, NS))]
        self.aliases = {}

    def _copies(self, ins, outs, sems):
        x, y, c, _ = _place()
        return [pltpu.make_async_remote_copy(src_ref=_half_of_quarter(ins[a], self.kinds[a], q, 1 - c), dst_ref=outs[a].at[q],
                                             send_sem=sems[0].at[a, q], recv_sem=sems[1].at[a, q],
                                             device_id=(x, y, 1 - c), device_id_type=MESH)
                for a in range(len(ins)) for q in range(NS)]

    def start(self, ins, outs, sems):
        for cp in self._copies(ins, outs, sems):
            cp.start()

    def finish(self, ins, outs, sems):
        for cp in self._copies(ins, outs, sems):
            cp.wait()


class _ToChips:
    def __init__(self, sums, which=(0, 1, 2)):
        self.ins = list(sums)
        self.which = tuple(which)
        n, m = len(self.ins), len(self.which)
        self.out_shape = [jax.ShapeDtypeStruct((m,) + s.shape[1:], s.dtype) for s in self.ins]
        self.scratch = [pltpu.SemaphoreType.DMA((n, m)), pltpu.SemaphoreType.DMA((n, m))]
        self.aliases = {}

    def _copies(self, ins, outs, sems):
        x, y, c, chips = _place()
        return [pltpu.make_async_remote_copy(src_ref=ins[a].at[2 * chips[j][0] + chips[j][1]], dst_ref=outs[a].at[k],
                                             send_sem=sems[0].at[a, k], recv_sem=sems[1].at[a, k],
                                             device_id=(*chips[j], c), device_id_type=MESH)
                for a in range(len(ins)) for k, j in enumerate(self.which)]

    def start(self, ins, outs, sems):
        for cp in self._copies(ins, outs, sems):
            cp.start()

    def finish(self, ins, outs, sems):
        for cp in self._copies(ins, outs, sems):
            cp.wait()


class _SwapHalves:
    def __init__(self, quarters):
        self.ins = list(quarters)
        n = len(self.ins)
        self.out_shape = [jax.ShapeDtypeStruct(g.shape, g.dtype) for g in self.ins]
        self.scratch = [pltpu.SemaphoreType.DMA((n,)), pltpu.SemaphoreType.DMA((n,))]
        self.aliases = {a: a for a in range(n)}

    def _copy(self, outs, sems, a, pc):
        x, y, c, _ = _place()
        rows = _rows_half(outs[a], pc)
        return pltpu.make_async_remote_copy(src_ref=rows, dst_ref=rows, send_sem=sems[0].at[a], recv_sem=sems[1].at[a],
                                            device_id=(x, y, 1 - c), device_id_type=MESH)

    def start(self, ins, outs, sems):
        c = lax.axis_index("c")
        for a in range(len(outs)):
            self._copy(outs, sems, a, c).start()

    def finish(self, ins, outs, sems):
        c = lax.axis_index("c")
        for a in range(len(outs)):
            self._copy(outs, sems, a, c).wait_send()
            self._copy(outs, sems, a, 1 - c).wait_recv()


def _call(name, grid, compute, in_specs, out_specs, out_shape, scratch, vmem_mb, args, jobs=()):
    n_in, n_out, n_scr = len(in_specs), len(out_specs), len(scratch)
    ji = [len(j.ins) for j in jobs]
    jo = [len(j.out_shape) for j in jobs]
    js = [len(j.scratch) for j in jobs]

    def body(*refs):
        pos = [0]

        def take(k):
            r = refs[pos[0]:pos[0] + k]
            pos[0] += k
            return r

        ins, jins = take(n_in), [take(k) for k in ji]
        outs, jouts = take(n_out), [take(k) for k in jo]
        scr, jscr = take(n_scr), [take(k) for k in js]
        if jobs and grid:
            ids = [pl.program_id(a) for a in range(len(grid))]
            first = functools.reduce(jnp.logical_and, [i == 0 for i in ids])
            last = functools.reduce(jnp.logical_and, [i == g - 1 for i, g in zip(ids, grid)])

            @pl.when(first)
            def _():
                for j, a, b, c in zip(jobs, jins, jouts, jscr):
                    j.start(a, b, c)

            @pl.when(last)
            def _():
                for j, a, b, c in zip(jobs, jins, jouts, jscr):
                    if hasattr(j, "relay"):
                        j.relay(a, b, c)
        elif jobs:
            for j, a, b, c in zip(jobs, jins, jouts, jscr):
                j.start(a, b, c)
            for j, a, b, c in zip(jobs, jins, jouts, jscr):
                if hasattr(j, "relay"):
                    j.relay(a, b, c)
        compute(ins, outs, scr)
        if jobs and grid:
            @pl.when(last)
            def _():
                for j, a, b, c in zip(jobs, jins, jouts, jscr):
                    j.finish(a, b, c)
        elif jobs:
            for j, a, b, c in zip(jobs, jins, jouts, jscr):
                j.finish(a, b, c)

    aliases = {}
    in_off, out_off = n_in, n_out
    for j, a, b in zip(jobs, ji, jo):
        for s, d in j.aliases.items():
            aliases[in_off + s] = out_off + d
        in_off += a
        out_off += b
    res = pl.pallas_call(
        body, name=name, grid=grid,
        in_specs=list(in_specs) + [IN_HBM] * sum(ji), out_specs=list(out_specs) + [IN_HBM] * sum(jo),
        out_shape=list(out_shape) + [pltpu.HBM(s.shape, s.dtype) for j in jobs for s in j.out_shape],
        scratch_shapes=list(scratch) + [s for j in jobs for s in j.scratch],
        input_output_aliases=aliases, compiler_params=_cparams(len(grid), vmem_mb),
    )(*args, *[a for j in jobs for a in j.ins])
    res = list(res)
    main, rest, jres = res[:n_out], res[n_out:], []
    for k in jo:
        jres.append(rest[:k])
        rest = rest[k:]
    return main, jres


def _exchange(name, jobs):
    return _call(name, (), lambda ins, outs, scr: None, [], [], [], [], 16, [], jobs)[1]


def _small_rows(ka, kb):
    first_a = 8
    first_b = first_a + -(-ka // 8) * 8
    return first_a, first_b, first_b + -(-kb // 8) * 8


LOSS_ROW = 7


def _allreduce_small(vecs, taps_a, taps_b, loss_part):
    n = len(vecs)
    C = vecs[0].shape[1]
    NQ, KA, CB = taps_a.shape
    KB = taps_b.shape[1]
    first_a, first_b, R = _small_rows(KA, KB)
    assert n <= LOSS_ROW < first_a
    N = 8

    def body(*refs):
        vec_refs = refs[:n]
        ta_ref, tb_ref, loss_ref, out_ref, v_ref, gath, send_sems, recv_sems, local_sem = refs[n:]
        v_ref[...] = jnp.zeros_like(v_ref)
        v_ref[LOSS_ROW:LOSS_ROW + 1, 0:loss_ref.shape[1]] = loss_ref[0:1, :]
        for i, r in enumerate(vec_refs):
            v_ref[i:i + 1, :] = r[...]
        for q in range(NQ):
            v_ref[first_a:first_a + KA, q * CB:(q + 1) * CB] = ta_ref[q]
            v_ref[first_b:first_b + KB, q * CB:(q + 1) * CB] = tb_ref[q]
        x, y, c, chips = _place()
        me, sibling = (x, y, c), (x, y, 1 - c)

        def rows(px, py, pc):
            return gath.at[pl.ds((4 * px + 2 * py + pc) * R, R), :]

        def copy(k, block, to, src=None):
            return pltpu.make_async_remote_copy(src_ref=rows(*block) if src is None else src, dst_ref=rows(*block),
                                                send_sem=send_sems.at[k], recv_sem=recv_sems.at[k],
                                                device_id=to, device_id_type=MESH)

        mine = pltpu.make_async_copy(v_ref, rows(*me), local_sem)
        mine.start()
        first = [copy(0, me, sibling, src=v_ref)]
        first += [copy(1 + j, me, (*chip, c), src=v_ref) for j, chip in enumerate(chips)]
        for cp in first:
            cp.start()
        passed = [copy(4 + j, (*chip, c), sibling) for j, chip in enumerate(chips)]
        for j, chip in enumerate(chips):
            copy(1 + j, (*chip, c), me).wait_recv()
            passed[j].start()
        copy(0, sibling, me).wait_recv()
        for j, chip in enumerate(chips):
            copy(4 + j, (*chip, 1 - c), me).wait_recv()
        for cp in first + passed:
            cp.wait_send()
        mine.wait()
        acc = gath[0:R, :]
        for d in range(1, N):
            acc = acc + gath[d * R:(d + 1) * R, :]
        out_ref[...] = acc

    vmem = pl.BlockSpec(memory_space=pltpu.VMEM)
    return pl.pallas_call(
        body, name="allreduce_small",
        in_specs=[vmem] * (n + 3), out_specs=vmem,
        out_shape=jax.ShapeDtypeStruct((R, C), F32),
        scratch_shapes=[pltpu.VMEM((R, C), F32), pltpu.VMEM((N * R, C), F32), pltpu.SemaphoreType.DMA((7,)),
                        pltpu.SemaphoreType.DMA((7,)), pltpu.SemaphoreType.DMA],
    )(*vecs, taps_a, taps_b, loss_part)


def _adamw_math(w, g, m, v):
    c1 = 1.0 - ADAM_B1 ** ADAM_STEP
    c2 = 1.0 - ADAM_B2 ** ADAM_STEP
    mn = ADAM_B1 * m + (1.0 - ADAM_B1) * g
    vn = ADAM_B2 * v + (1.0 - ADAM_B2) * (g * g)
    return -ADAM_LR * ((mn / c1) / (jnp.sqrt(vn / c2) + ADAM_EPS) + ADAM_WD * w), mn, vn


def _small_adamw(place, small, vec_wmv, tap_wmv):
    n = len(vec_wmv)
    D = small.shape[1]
    CB = tap_wmv[0][0].shape[2]
    ks = [t[0].shape[1] for t in tap_wmv]
    firsts = _small_rows(*ks)[:2]

    def body(place_ref, small_ref, *refs):
        ins, outs = refs[:3 * (n + 2)], refs[3 * (n + 2):]
        chip = place_ref[0]
        for i in range(n):
            g = small_ref[i:i + 1, :]
            d, mn, vn = _adamw_math(ins[3 * i][...], g, ins[3 * i + 1][...], ins[3 * i + 2][...])
            for o, val in zip(outs[4 * i:4 * i + 4], (g, d, mn, vn)):
                o[...] = val
        for t, (row0, k) in enumerate(zip(firsts, ks)):
            g = jnp.zeros((k, CB), F32)
            for q in range(D // CB):
                g = g + jnp.where(chip == q, small_ref[row0:row0 + k, q * CB:(q + 1) * CB], 0.0)
            w_ref, m_ref, v_ref = ins[3 * (n + t):3 * (n + t) + 3]
            d, mn, vn = _adamw_math(w_ref[0], g, m_ref[0], v_ref[0])
            for o, val in zip(outs[4 * (n + t):4 * (n + t) + 4], (g, d, mn, vn)):
                o[0] = val

    flat = [a for wmv in list(vec_wmv) + list(tap_wmv) for a in wmv]
    shapes = [jax.ShapeDtypeStruct(wmv[0].shape, F32) for wmv in list(vec_wmv) + list(tap_wmv) for _ in range(4)]
    vmem = pl.BlockSpec(memory_space=pltpu.VMEM)
    res = pl.pallas_call(
        body, name="small_adamw",
        in_specs=[pl.BlockSpec(memory_space=pltpu.SMEM)] + [vmem] * (1 + len(flat)), out_specs=[vmem] * len(shapes),
        out_shape=shapes,
    )(place, small, *flat)
    return [res[4 * i:4 * i + 4] for i in range(n + 2)]


def _ffn_fwd(h, g, wg, wu, wd, name, jobs=(), head=None):
    S, D = h.shape
    F = wg.shape[0]
    ts = _tile(S, 512)
    fb = _tile(F, F // 2)
    nf = F // fb

    def compute(ins, outs, scr):
        h_ref, g_ref, wg_ref, wu_ref, wd_ref = ins[:5]
        n_ref, gp_ref, up_ref = outs[-3:]
        x = h_ref[...]
        n = _rms_fwd(x, g_ref[...]).astype(BF16)
        n_ref[...] = n
        acc = None
        for j in range(nf):
            cols = slice(j * fb, (j + 1) * fb)
            gp = _dot_nt(n, wg_ref[cols, :])
            up = _dot_nt(n, wu_ref[cols, :])
            gp_ref[:, cols] = gp.astype(BF16)
            up_ref[:, cols] = up.astype(BF16)
            part = _dot((gp * _sigmoid(gp) * up).astype(BF16), wd_ref[cols, :])
            acc = part if acc is None else acc + part
        ho = x + 0.5 * acc
        if head is None:
            outs[0][...] = ho
            return
        t_ref, gf_ref = ins[5:]
        dh_ref, do_ref, dgf_ref, loss_ref = outs[:4]

        @pl.when(pl.program_id(0) == 0)
        def _():
            dgf_ref[...] = jnp.zeros_like(dgf_ref)
            loss_ref[...] = jnp.zeros_like(loss_ref)

        err = _rms_fwd(ho, gf_ref[...]) - t_ref[...]
        loss_ref[...] += (0.5 / D) * jnp.sum(err * err)
        dx, dg = _rms_bwd(ho, gf_ref[...], err * (1.0 / D))
        dh_ref[...] = dx
        do_ref[...] = (0.5 * dx).astype(BF16)
        dgf_ref[...] += dg

    tok = pl.BlockSpec((ts, D), lambda i: (i, 0))
    row = pl.BlockSpec((1, D), lambda i: (0, 0))
    wsp = _resident((F, D))
    hid = pl.BlockSpec((ts, F), lambda i: (i, 0))
    saved = [jax.ShapeDtypeStruct((S, D), BF16), jax.ShapeDtypeStruct((S, F), BF16), jax.ShapeDtypeStruct((S, F), BF16)]
    if head is None:
        return _call(name, (S // ts,), compute, [tok, row, wsp, wsp, wsp], [tok, tok, hid, hid],
                     [jax.ShapeDtypeStruct((S, D), F32)] + saved, [], 56, [h, g, wg, wu, wd], jobs)
    return _call(name, (S // ts,), compute, [tok, row, wsp, wsp, wsp, tok, row],
                 [tok, tok, row, pl.BlockSpec((8, 128), lambda i: (0, 0)), tok, hid, hid],
                 [jax.ShapeDtypeStruct((S, D), F32), jax.ShapeDtypeStruct((S, D), BF16), jax.ShapeDtypeStruct((1, D), F32),
                  jax.ShapeDtypeStruct((8, 128), F32)] + saved, [], 60, [h, g, wg, wu, wd, *head], jobs)


def _ffn_bwd_hidden(do, gp, up, wd, name, jobs=()):
    S, D = do.shape
    F = wd.shape[0]
    ts = _tile(S, 512)
    fb = _tile(F, F // 2)

    def compute(ins, outs, scr):
        do_ref, gp_ref, up_ref, wd_ref = ins
        dgp_ref, dup_ref, a_ref = outs
        da = _dot_nt(do_ref[...], wd_ref[...])
        gf = gp_ref[...].astype(F32)
        uf = up_ref[...].astype(F32)
        sg = _sigmoid(gf)
        si = gf * sg
        dgp_ref[...] = (da * uf * (sg * (1.0 + gf * (1.0 - sg)))).astype(BF16)
        dup_ref[...] = (da * si).astype(BF16)
        a_ref[...] = (si * uf).astype(BF16)

    tok = pl.BlockSpec((ts, D), lambda s, i: (i, 0))
    hid = pl.BlockSpec((ts, fb), lambda s, i: (i, s))
    return _call(name, (F // fb, S // ts), compute, [tok, hid, hid, pl.BlockSpec((fb, D), lambda s, i: (s, 0))],
                 [hid, hid, hid], [jax.ShapeDtypeStruct((S, F), BF16)] * 3, [], 56, [do, gp, up, wd], jobs)


def _ffn_bwd_input(dh, h, g, dgp, dup, wg, wu, name, jobs=()):
    S, D = h.shape
    F = wg.shape[0]
    ts = _tile(S, 512)

    def compute(ins, outs, scr):
        dh_ref, h_ref, g_ref, dgp_ref, dup_ref, wg_ref, wu_ref = ins
        dhi_ref, dg_ref = outs

        @pl.when(pl.program_id(0) == 0)
        def _():
            dg_ref[...] = jnp.zeros_like(dg_ref)

        dn = _dot(dgp_ref[...], wg_ref[...]) + _dot(dup_ref[...], wu_ref[...])
        dx, dg = _rms_bwd(h_ref[...], g_ref[...], dn)
        dhi_ref[...] = dh_ref[...] + dx
        dg_ref[...] += dg

    tok = pl.BlockSpec((ts, D), lambda i: (i, 0))
    hid = pl.BlockSpec((ts, F), lambda i: (i, 0))
    row = pl.BlockSpec((1, D), lambda i: (0, 0))
    return _call(name, (S // ts,), compute, [tok, tok, row, hid, hid, _resident((F, D)), _resident((F, D))], [tok, row],
                 [jax.ShapeDtypeStruct((S, D), F32), jax.ShapeDtypeStruct((1, D), F32)], [], 56,
                 [dh, h, g, dgp, dup, wg, wu], jobs)


def _ffn_wgrad(hid, tok, name, jobs=()):
    S, D = tok.shape
    F = hid.shape[1]
    fb = _tile(F, F // 2)

    def compute(ins, outs, scr):
        outs[0][...] = _dot_tn(ins[0][...], ins[1][...])

    main, jres = _call(name, (F // fb,), compute,
                       [pl.BlockSpec((S, fb), lambda j: (0, j)), _resident(tok.shape)],
                       [pl.BlockSpec((fb, D), lambda j: (j, 0))], [jax.ShapeDtypeStruct((F, D), F32)], [], 56,
                       [hid, tok], jobs)
    return main[0], jres


def _mix_in_fwd(h, g, win, jobs=()):
    S, D = h.shape
    NG = win.shape[1] // D
    ts = _tile(S, 512)

    def compute(ins, outs, scr):
        h_ref, g_ref, w_ref = ins
        u_ref, z_ref = outs
        u = _rms_fwd(h_ref[...], g_ref[...]).astype(BF16)
        u_ref[...] = u
        for k in range(NG):
            z_ref[k] = _dot(u, w_ref[:, k * D:(k + 1) * D]).astype(BF16)

    return _call("mix_in_fwd", (S // ts,), compute,
                 [pl.BlockSpec((ts, D), lambda i: (i, 0)), pl.BlockSpec((1, D), lambda i: (0, 0)), _resident(win.shape)],
                 [pl.BlockSpec((ts, D), lambda i: (i, 0)), pl.BlockSpec((NG, ts, D), lambda i: (0, i, 0))],
                 [jax.ShapeDtypeStruct((S, D), BF16), jax.ShapeDtypeStruct((NG, S, D), BF16)],
                 [], 48, [h, g, win], jobs)


SUBLANES = 8


def _shifted_copies(s):
    n = s.shape[1] - SUBLANES
    for r in range(1, SUBLANES):
        s[r, 0:n, :] = s[0, r:r + n, :]


def _window(s, o, rows):
    r = o % SUBLANES
    return s[r, o - r:o - r + rows, :]


def _conv_fwd(z, wa, ba, wb, jobs=()):
    _, S, D = z.shape
    _, KA, CB = wa.shape
    KB = wb.shape[1]
    ts = _tile(S, 1024)
    r = ts // HALO
    CH = min(64, ts)

    def compute(ins, outs, scr):
        z_ref, zh_ref, wa_ref, ba_ref, wb_ref = ins
        a1_ref, q_ref = outs
        sa, sb = scr
        keep = (pl.program_id(1) > 0).astype(F32)
        sa[0, HALO:HALO + ts, :] = z_ref[0].astype(F32) * _sigmoid(z_ref[1].astype(F32))
        sa[0, 0:HALO, :] = zh_ref[0].astype(F32) * _sigmoid(zh_ref[1].astype(F32)) * keep
        _shifted_copies(sa)
        sb[HALO:HALO + ts, :] = z_ref[3].astype(F32) * z_ref[4].astype(F32)
        sb[0:HALO, :] = zh_ref[3].astype(F32) * zh_ref[4].astype(F32) * keep
        wak = [wa_ref[0, k:k + 1, :] for k in range(KA)]
        wbk = [wb_ref[0, k:k + 1, :] for k in range(KB)]
        for c0 in range(0, ts, CH):
            acc = jnp.broadcast_to(ba_ref[...], (CH, CB))
            for k in range(KA):
                acc = acc + wak[k] * _window(sa, c0 + HALO - (KA - 1) + k, CH)
            a1_ref[c0:c0 + CH, :] = acc
            v = jnp.zeros((CH, CB), F32)
            for k in range(KB):
                o = c0 + HALO - (KB - 1) + k
                v = v + wbk[k] * sb[o:o + CH, :]
            q_ref[c0:c0 + CH, :] = (z_ref[2, c0:c0 + CH, :].astype(F32) * v).astype(BF16)

    return _call("conv_fwd", (D // CB, S // ts), compute,
                 [pl.BlockSpec((5, ts, CB), lambda j, i: (0, i, j)),
                  pl.BlockSpec((5, HALO, CB), lambda j, i: (0, jnp.maximum(i * r - 1, 0), j)),
                  pl.BlockSpec((1, KA, CB), lambda j, i: (j, 0, 0)), pl.BlockSpec((1, CB), lambda j, i: (0, j)),
                  pl.BlockSpec((1, KB, CB), lambda j, i: (j, 0, 0))],
                 [pl.BlockSpec((ts, CB), lambda j, i: (i, j)), pl.BlockSpec((ts, CB), lambda j, i: (i, j))],
                 [jax.ShapeDtypeStruct((S, D), F32), jax.ShapeDtypeStruct((S, D), BF16)],
                 [pltpu.VMEM((SUBLANES, HALO + ts, CB), F32), pltpu.VMEM((HALO + ts, CB), F32)], 40, [z, z, wa, ba, wb], jobs)


def _ln_stats(a1):
    mu = jnp.mean(a1, axis=-1, keepdims=True)
    xc = a1 - mu
    rstd = lax.rsqrt(jnp.mean(xc * xc, axis=-1, keepdims=True) + EPS)
    return xc * rstd, rstd


def _mix_out_fwd(h1, a1, q, z, lng, lnb, wa, wb, wo):
    S, D = h1.shape
    ts = _tile(S, 512)

    def compute(ins, outs, scr):
        h_ref, a1_ref, q_ref, ga_ref, gb_ref, lng_ref, lnb_ref, wa_ref, wb_ref, wo_ref = ins
        h2_ref, a3_ref, m_ref, ya_ref, yb_ref = outs
        xhat, _ = _ln_stats(a1_ref[...])
        a2 = xhat * lng_ref[...] + lnb_ref[...]
        a3 = (a2 * _sigmoid(a2)).astype(BF16)
        a3_ref[...] = a3
        ya = _dot(a3, wa_ref[...])
        yb = _dot(q_ref[...], wb_ref[...])
        ya_ref[...] = ya.astype(BF16)
        yb_ref[...] = yb.astype(BF16)
        m = (_sigmoid(ga_ref[0].astype(F32)) * ya + _sigmoid(gb_ref[0].astype(F32)) * yb).astype(BF16)
        m_ref[...] = m
        h2_ref[...] = h_ref[...] + _dot(m, wo_ref[...])

    tok = pl.BlockSpec((ts, D), lambda i: (i, 0))
    row = pl.BlockSpec((1, D), lambda i: (0, 0))
    mat = _resident((D, D))
    return _call("mix_out_fwd", (S // ts,), compute,
                 [tok, tok, tok, pl.BlockSpec((1, ts, D), lambda i: (5, i, 0)), pl.BlockSpec((1, ts, D), lambda i: (6, i, 0)),
                  row, row, mat, mat, mat], [tok] * 5,
                 [jax.ShapeDtypeStruct((S, D), F32)] + [jax.ShapeDtypeStruct((S, D), BF16)] * 4,
                 [], 56, [h1, a1, q, z, z, lng, lnb, wa, wb, wo])[0]


def _mix_out_bwd(dh2, a1, z, ya, yb, lng, lnb, wa, wb, wo, jobs=()):
    S, D = dh2.shape
    ts = _tile(S, 512)

    def compute(ins, outs, scr):
        dh_ref, a1_ref, ga_ref, gb_ref, ya_ref, yb_ref, lng_ref, lnb_ref, wa_ref, wb_ref, wo_ref = ins
        da1_ref, dq_ref, dga_ref, dgb_ref, dya_ref, dyb_ref, dhb_ref, dlg_ref, dlb_ref = outs

        @pl.when(pl.program_id(0) == 0)
        def _():
            dlg_ref[...] = jnp.zeros_like(dlg_ref)
            dlb_ref[...] = jnp.zeros_like(dlb_ref)

        dhb = dh_ref[...].astype(BF16)
        dhb_ref[...] = dhb
        dm = _dot_nt(dhb, wo_ref[...])
        sa = _sigmoid(ga_ref[0].astype(F32))
        sb = _sigmoid(gb_ref[0].astype(F32))
        dga_ref[...] = (dm * ya_ref[...].astype(F32) * sa * (1.0 - sa)).astype(BF16)
        dgb_ref[...] = (dm * yb_ref[...].astype(F32) * sb * (1.0 - sb)).astype(BF16)
        dya = (sa * dm).astype(BF16)
        dyb = (sb * dm).astype(BF16)
        dya_ref[...] = dya
        dyb_ref[...] = dyb
        dq_ref[...] = _dot_nt(dyb, wb_ref[...]).astype(BF16)
        da3 = _dot_nt(dya, wa_ref[...])
        xhat, rstd = _ln_stats(a1_ref[...])
        a2 = xhat * lng_ref[...] + lnb_ref[...]
        sg = _sigmoid(a2)
        da2 = da3 * (sg * (1.0 + a2 * (1.0 - sg)))
        dlg_ref[...] += jnp.sum(da2 * xhat, axis=0, keepdims=True)
        dlb_ref[...] += jnp.sum(da2, axis=0, keepdims=True)
        dxh = da2 * lng_ref[...]
        da1_ref[...] = (rstd * (dxh - jnp.mean(dxh, axis=-1, keepdims=True)
                                - xhat * jnp.mean(dxh * xhat, axis=-1, keepdims=True))).astype(BF16)

    tok = pl.BlockSpec((ts, D), lambda i: (i, 0))
    row = pl.BlockSpec((1, D), lambda i: (0, 0))
    mat = _resident((D, D))
    return _call("mix_out_bwd", (S // ts,), compute,
                 [tok, tok, pl.BlockSpec((1, ts, D), lambda i: (5, i, 0)), pl.BlockSpec((1, ts, D), lambda i: (6, i, 0)),
                  tok, tok, row, row, mat, mat, mat], [tok] * 7 + [row, row],
                 [jax.ShapeDtypeStruct((S, D), BF16)] * 7 + [jax.ShapeDtypeStruct((1, D), F32)] * 2,
                 [], 56, [dh2, a1, z, z, ya, yb, lng, lnb, wa, wb, wo], jobs)


def _mixer_wgrads(a3, dya, q, dyb, mm, dhb, jobs=()):
    S, D = a3.shape
    tk = _tile(S, 512)

    def compute(ins, outs, scr):
        @pl.when(pl.program_id(0) == 0)
        def _():
            for o in outs:
                o[...] = jnp.zeros_like(o)

        for t in range(3):
            outs[t][...] += _dot_tn(ins[2 * t][...], ins[2 * t + 1][...])

    tok = pl.BlockSpec((tk, D), lambda k: (k, 0))
    return _call("mixer_wgrads", (S // tk,), compute, [tok] * 6, [pl.BlockSpec((D, D), lambda k: (0, 0))] * 3,
                 [jax.ShapeDtypeStruct((D, D), F32)] * 3, [], 56, [a3, dya, q, dyb, mm, dhb], jobs)


def _conv_bwd(z, da1, dq, dga, dgb, wa, wb, jobs=()):
    NG, S, D = z.shape
    _, KA, CB = wa.shape
    KB = wb.shape[1]
    ts = _tile(S, 1024)
    r = ts // HALO
    nt = S // ts
    CH = min(64, ts)
    last_halo = S // HALO - 1

    def compute(ins, outs, scr):
        z_ref, zp_ref, zn_ref, da1_ref, da1n_ref, dq_ref, dqn_ref, dga_ref, dgb_ref, wa_ref, wb_ref = ins
        dz_ref, dwa_ref, dba_ref, dwb_ref = outs
        sa0, sd, sp, sv, acca, accb = scr
        i = pl.program_id(1)
        prev = (i > 0).astype(F32)
        nxt = (i < nt - 1).astype(F32)

        @pl.when(i == 0)
        def _():
            acca[...] = jnp.zeros_like(acca)
            accb[...] = jnp.zeros_like(accb)
            dba_ref[...] = jnp.zeros_like(dba_ref)

        sa0[0, HALO:HALO + ts, :] = z_ref[0].astype(F32) * _sigmoid(z_ref[1].astype(F32))
        sa0[0, 0:HALO, :] = zp_ref[0].astype(F32) * _sigmoid(zp_ref[1].astype(F32)) * prev
        _shifted_copies(sa0)
        sp[HALO:HALO + ts, :] = z_ref[3].astype(F32) * z_ref[4].astype(F32)
        sp[0:HALO, :] = zp_ref[3].astype(F32) * zp_ref[4].astype(F32) * prev
        sd[0, 0:ts, :] = da1_ref[...].astype(F32)
        sd[0, ts:ts + HALO, :] = da1n_ref[...].astype(F32) * nxt
        _shifted_copies(sd)
        sv[0:ts, :] = dq_ref[...].astype(F32) * z_ref[2].astype(F32)
        sv[ts:ts + HALO, :] = dqn_ref[...].astype(F32) * zn_ref[2].astype(F32) * nxt
        dba_ref[...] += jnp.sum(sd[0, 0:ts, :], axis=0, keepdims=True)
        wak = [wa_ref[0, k:k + 1, :] for k in range(KA)]
        wbk = [wb_ref[0, k:k + 1, :] for k in range(KB)]
        for c0 in range(0, ts, CH):
            rows = slice(c0, c0 + CH)
            d1 = sd[0, rows, :]
            da0 = jnp.zeros((CH, CB), F32)
            for k in range(KA):
                da0 = da0 + wak[k] * _window(sd, c0 + (KA - 1) - k, CH)
                a0w = _window(sa0, c0 + HALO - (KA - 1) + k, CH)
                acca[k] += jnp.sum((d1 * a0w).reshape(CH // 8, 8, CB), axis=0)
            val = z_ref[0, rows, :].astype(F32)
            sg = _sigmoid(z_ref[1, rows, :].astype(F32))
            dz_ref[0, rows, :] = (da0 * sg).astype(BF16)
            dz_ref[1, rows, :] = (da0 * val * sg * (1.0 - sg)).astype(BF16)
            dv = sv[rows, :]
            v = jnp.zeros((CH, CB), F32)
            dp = jnp.zeros((CH, CB), F32)
            for k in range(KB):
                o = c0 + HALO - (KB - 1) + k
                pw = sp[o:o + CH, :]
                v = v + wbk[k] * pw
                accb[k] += jnp.sum((dv * pw).reshape(CH // 8, 8, CB), axis=0)
                o = c0 + (KB - 1) - k
                dp = dp + wbk[k] * sv[o:o + CH, :]
            dz_ref[2, rows, :] = (dq_ref[rows, :].astype(F32) * v).astype(BF16)
            dz_ref[3, rows, :] = (dp * z_ref[4, rows, :].astype(F32)).astype(BF16)
            dz_ref[4, rows, :] = (dp * z_ref[3, rows, :].astype(F32)).astype(BF16)
        dz_ref[5] = dga_ref[...]
        dz_ref[6] = dgb_ref[...]

        @pl.when(i == nt - 1)
        def _():
            dwa_ref[0] = jnp.sum(acca[...], axis=1)
            dwb_ref[0] = jnp.sum(accb[...], axis=1)

    zt = pl.BlockSpec((5, ts, CB), lambda j, i: (0, i, j))
    zp = pl.BlockSpec((5, HALO, CB), lambda j, i: (0, jnp.maximum(i * r - 1, 0), j))
    zn = pl.BlockSpec((5, HALO, CB), lambda j, i: (0, jnp.minimum((i + 1) * r, last_halo), j))
    tok = pl.BlockSpec((ts, CB), lambda j, i: (i, j))
    tokn = pl.BlockSpec((HALO, CB), lambda j, i: (jnp.minimum((i + 1) * r, last_halo), j))
    return _call("conv_bwd", (D // CB, nt), compute,
                 [zt, zp, zn, tok, tokn, tok, tokn, tok, tok,
                  pl.BlockSpec((1, KA, CB), lambda j, i: (j, 0, 0)), pl.BlockSpec((1, KB, CB), lambda j, i: (j, 0, 0))],
                 [pl.BlockSpec((NG, ts, CB), lambda j, i: (0, i, j)), pl.BlockSpec((1, KA, CB), lambda j, i: (j, 0, 0)),
                  pl.BlockSpec((1, CB), lambda j, i: (0, j)), pl.BlockSpec((1, KB, CB), lambda j, i: (j, 0, 0))],
                 [jax.ShapeDtypeStruct((NG, S, D), BF16), jax.ShapeDtypeStruct((D // CB, KA, CB), F32),
                  jax.ShapeDtypeStruct((1, D), F32), jax.ShapeDtypeStruct((D // CB, KB, CB), F32)],
                 [pltpu.VMEM((SUBLANES, HALO + ts, CB), F32), pltpu.VMEM((SUBLANES, ts + HALO, CB), F32),
                  pltpu.VMEM((HALO + ts, CB), F32), pltpu.VMEM((ts + HALO, CB), F32),
                  pltpu.VMEM((KA, 8, CB), F32), pltpu.VMEM((KB, 8, CB), F32)],
                 48, [z, z, z, da1, da1, dq, dq, dga, dgb, wa, wb], jobs)


def _mix_in_bwd(dh2, h1, g, dz, win, jobs=()):
    S, D = h1.shape
    NG = dz.shape[0]
    ts = _tile(S, 512)

    def compute(ins, outs, scr):
        dh_ref, h_ref, g_ref, dz_ref, w_ref = ins
        dhi_ref, dg_ref, do_ref = outs

        @pl.when(pl.program_id(0) == 0)
        def _():
            dg_ref[...] = jnp.zeros_like(dg_ref)

        du = _dot_nt(dz_ref[0], w_ref[:, 0:D])
        for k in range(1, NG):
            du = du + _dot_nt(dz_ref[k], w_ref[:, k * D:(k + 1) * D])
        dx, dg = _rms_bwd(h_ref[...], g_ref[...], du)
        dhi = dh_ref[...] + dx
        dhi_ref[...] = dhi
        do_ref[...] = (0.5 * dhi).astype(BF16)
        dg_ref[...] += dg

    tok = pl.BlockSpec((ts, D), lambda i: (i, 0))
    row = pl.BlockSpec((1, D), lambda i: (0, 0))
    return _call("mix_in_bwd", (S // ts,), compute,
                 [tok, tok, row, pl.BlockSpec((NG, ts, D), lambda i: (0, i, 0)), _resident(win.shape)],
                 [tok, row, tok],
                 [jax.ShapeDtypeStruct((S, D), F32), jax.ShapeDtypeStruct((1, D), F32), jax.ShapeDtypeStruct((S, D), BF16)],
                 [], 56, [dh2, h1, g, dz, win], jobs)


def _w_in_grad(u, dz, jobs=()):
    S, D = u.shape
    NG = dz.shape[0]

    def compute(ins, outs, scr):
        outs[0][...] = _dot_tn(ins[0][...], ins[1][0])

    return _call("w_in_grad", (NG,), compute,
                 [_resident(u.shape), pl.BlockSpec((1, S, D), lambda j: (j, 0, 0))],
                 [pl.BlockSpec((D, D), lambda j: (0, j))], [jax.ShapeDtypeStruct((D, NG * D), F32)], [], 48, [u, dz], jobs)


def _chip_sums(place, grads, got, kind, name):
    n = len(grads)
    qr, qc = _quarter_shape(grads[0].shape, kind)
    h = qr // 2
    tr = _row_tile(h)
    nr = h // tr

    def body(pc_ref, *refs):
        g_refs, got_refs, b_refs, f_refs = refs[:n], refs[n:2 * n], refs[2 * n:3 * n], refs[3 * n:]
        own = pl.program_id(1) == pc_ref[0]
        for a in range(n):
            s = g_refs[a][...] + got_refs[a][0]
            b_refs[a][0] = s.astype(BF16)

            @pl.when(own)
            def _():
                f_refs[a][...] = s

    if kind == "rows":
        gspec = pl.BlockSpec((tr, qc), lambda r, q, pc: (q * (2 * nr) + pc[1] * nr + r, 0))
    else:
        gspec = pl.BlockSpec((tr, qc), lambda r, q, pc: (pc[1] * nr + r, q))
    lspec = pl.BlockSpec((1, tr, qc), lambda r, q, pc: (q, r, 0))
    res = pl.pallas_call(
        body, name=name,
        grid_spec=pltpu.PrefetchScalarGridSpec(
            num_scalar_prefetch=1, grid=(nr, NS), in_specs=[gspec] * n + [lspec] * n,
            out_specs=[lspec] * n + [pl.BlockSpec((tr, qc), lambda r, q, pc: (r, 0))] * n),
        out_shape=[jax.ShapeDtypeStruct((NS, h, qc), BF16)] * n + [jax.ShapeDtypeStruct((h, qc), F32)] * n,
        compiler_params=_cparams(2, 48),
    )(place, *grads, *got)
    return res[:n], res[n:]


def _totals(place, own, got, name):
    n = len(own)
    h, qc = own[0].shape
    tr = _row_tile(h)
    nr = h // tr
    got = [list(g) if isinstance(g, (list, tuple)) else [g] for g in got]
    m = len(got[0])

    def body(pc_ref, *refs):
        own_refs, got_refs, o_refs = refs[:n], refs[n:n + n * m], refs[n + n * m:]
        for a in range(n):
            acc = own_refs[a][...]
            for g in got_refs[a * m:(a + 1) * m]:
                for k in range(g.shape[0]):
                    acc = acc + g[k].astype(F32)
            o_refs[a][...] = acc

    lands = [pl.BlockSpec((g.shape[0], tr, qc), lambda r, pc: (0, r, 0)) for gs in got for g in gs]
    return pl.pallas_call(
        body, name=name,
        grid_spec=pltpu.PrefetchScalarGridSpec(
            num_scalar_prefetch=1, grid=(nr,),
            in_specs=[pl.BlockSpec((tr, qc), lambda r, pc: (r, 0))] * n + lands,
            out_specs=[pl.BlockSpec((tr, qc), lambda r, pc: (pc[1] * nr + r, 0))] * n),
        out_shape=[jax.ShapeDtypeStruct((2 * h, qc), F32)] * n,
        compiler_params=_cparams(1, 48),
    )(place, *own, *[g for gs in got for g in gs])


def _adamw(ws, gs, ms, vs, name):
    n = len(ws)
    R, C = ws[0].shape
    tr = _row_tile(R, (36 << 20) // (7 * 2 * 4 * n * C))

    def body(*refs):
        w_refs, g_refs, m_refs, v_refs = refs[:n], refs[n:2 * n], refs[2 * n:3 * n], refs[3 * n:4 * n]
        d_refs, mo_refs, vo_refs = refs[4 * n:5 * n], refs[5 * n:6 * n], refs[6 * n:]
        for a in range(n):
            d_refs[a][...], mo_refs[a][...], vo_refs[a][...] = _adamw_math(w_refs[a][...], g_refs[a][...], m_refs[a][...],
                                                                         v_refs[a][...])

    blk = pl.BlockSpec((tr, C), lambda r: (r, 0))
    res = pl.pallas_call(
        body, name=name, grid=(R // tr,),
        in_specs=[blk] * (4 * n), out_specs=[blk] * (3 * n),
        out_shape=[jax.ShapeDtypeStruct((R, C), F32)] * (3 * n),
        compiler_params=_cparams(1, 56),
    )(*ws, *gs, *ms, *vs)
    return res[:n], res[n:2 * n], res[2 * n:]


def kernel(x, ffn1_norm, ffn1_w_gate, ffn1_w_up, ffn1_w_down, mix_norm, w_in, a_dw_w, a_dw_b, a_ln_g, a_ln_b, a_w_out, b_conv_w, b_w_out, w_o, ffn2_norm, ffn2_w_gate, ffn2_w_up, ffn2_w_down, final_norm, loss_target, m_ffn1_norm, m_ffn1_w_gate, m_ffn1_w_up, m_ffn1_w_down, m_mix_norm, m_w_in, m_a_dw_w, m_a_dw_b, m_a_ln_g, m_a_ln_b, m_a_w_out, m_b_conv_w, m_b_w_out, m_w_o, m_ffn2_norm, m_ffn2_w_gate, m_ffn2_w_up, m_ffn2_w_down, m_final_norm, v_ffn1_norm, v_ffn1_w_gate, v_ffn1_w_up, v_ffn1_w_down, v_mix_norm, v_w_in, v_a_dw_w, v_a_dw_b, v_a_ln_g, v_a_ln_b, v_a_w_out, v_b_conv_w, v_b_w_out, v_w_o, v_ffn2_norm, v_ffn2_w_gate, v_ffn2_w_up, v_ffn2_w_down, v_final_norm):
    names = ["ffn1_norm", "ffn1_w_gate", "ffn1_w_up", "ffn1_w_down", "mix_norm", "w_in", "a_dw_w", "a_dw_b", "a_ln_g",
             "a_ln_b", "a_w_out", "b_conv_w", "b_w_out", "w_o", "ffn2_norm", "ffn2_w_gate", "ffn2_w_up", "ffn2_w_down",
             "final_norm"]
    W = dict(zip(names, [ffn1_norm, ffn1_w_gate, ffn1_w_up, ffn1_w_down, mix_norm, w_in, a_dw_w, a_dw_b, a_ln_g, a_ln_b,
                         a_w_out, b_conv_w, b_w_out, w_o, ffn2_norm, ffn2_w_gate, ffn2_w_up, ffn2_w_down, final_norm]))
    M = dict(zip(names, [m_ffn1_norm, m_ffn1_w_gate, m_ffn1_w_up, m_ffn1_w_down, m_mix_norm, m_w_in, m_a_dw_w, m_a_dw_b,
                         m_a_ln_g, m_a_ln_b, m_a_w_out, m_b_conv_w, m_b_w_out, m_w_o, m_ffn2_norm, m_ffn2_w_gate,
                         m_ffn2_w_up, m_ffn2_w_down, m_final_norm]))
    V = dict(zip(names, [v_ffn1_norm, v_ffn1_w_gate, v_ffn1_w_up, v_ffn1_w_down, v_mix_norm, v_w_in, v_a_dw_w, v_a_dw_b,
                         v_a_ln_g, v_a_ln_b, v_a_w_out, v_b_conv_w, v_b_w_out, v_w_o, v_ffn2_norm, v_ffn2_w_gate,
                         v_ffn2_w_up, v_ffn2_w_down, v_final_norm]))
    transposed = ("ffn1_w_gate", "ffn1_w_up", "ffn2_w_gate", "ffn2_w_up")
    vecs = ["ffn1_norm", "mix_norm", "a_dw_b", "a_ln_g", "a_ln_b", "ffn2_norm", "final_norm"]
    ffn1 = ["ffn1_w_gate", "ffn1_w_up", "ffn1_w_down"]
    ffn2 = ["ffn2_w_gate", "ffn2_w_up", "ffn2_w_down"]
    outp = ["a_w_out", "b_w_out", "w_o"]

    S, D = x.shape[1], x.shape[2]
    CB = D // NS
    KA, KB = a_dw_w.shape[1], b_conv_w.shape[1]
    px, py, pc = lax.axis_index("x"), lax.axis_index("y"), lax.axis_index("c")
    chip = 2 * px + py
    place = jnp.stack([chip, pc]).astype(jnp.int32)
    h0 = x.reshape(S, D)
    tgt = loss_target.reshape(S, D)
    row = lambda n: pltpu.with_memory_space_constraint(W[n].reshape(1, D), pltpu.HBM)
    pad = lambda a, r: jnp.concatenate([a, jnp.zeros((r - a.shape[0], a.shape[1]), F32)], axis=0)

    def quarter(P, n):
        return jnp.transpose(P[n][0]) if n in transposed else P[n][0]

    def unquarter(a, n):
        return (jnp.transpose(a) if n in transposed else a).reshape(W[n].shape)

    wq = {n: quarter(W, n).astype(BF16) for n in ffn1 + ffn2 + outp + ["w_in"]}

    f1 = _exchange("gather_ffn1", [_Gather([wq[n] for n in ffn1], ["rows"] * 3)])[0]
    g_in = _Gather([wq["w_in"], pad(a_dw_w[0], 32), pad(b_conv_w[0], 16)], ["cols", "rows", "rows"])
    (h1, n1, gp1, up1), ((win, taps_a, taps_b),) = _ffn_fwd(h0, row("ffn1_norm"), *f1, "ffn1_fwd", [g_in])
    wa_taps = taps_a.reshape(NS, 32, CB)[:, :KA]
    wb_taps = taps_b.reshape(NS, 16, CB)[:, :KB]
    g_out = _Gather([wq[n] for n in outp] + [wq["ffn2_w_gate"]], ["rows"] * 4)
    (u, z), ((wa_out, wb_out, wo, f2g),) = _mix_in_fwd(h1, row("mix_norm"), win, [g_out])
    g_f2 = _Gather([wq["ffn2_w_up"], wq["ffn2_w_down"]], ["rows"] * 2)
    (a1, q), ((f2u, f2d),) = _conv_fwd(z, wa_taps, row("a_dw_b"), wb_taps, [g_f2])
    h2, a3, mm, ya, yb = _mix_out_fwd(h1, a1, q, z, row("a_ln_g"), row("a_ln_b"), wa_out, wb_out, wo)
    (dh3, do2, d_final, loss_part, n2, gp2, up2), _ = _ffn_fwd(h2, row("ffn2_norm"), f2g, f2u, f2d, "ffn2_fwd_loss",
                                                               head=(tgt, row("final_norm")))

    (dgp2, dup2, act2), _ = _ffn_bwd_hidden(do2, gp2, up2, f2d, "ffn2_bwd_hidden")
    (dh2, d_ffn2), _ = _ffn_bwd_input(dh3, h2, row("ffn2_norm"), dgp2, dup2, f2g, f2u, "ffn2_bwd_input")
    g2 = [_ffn_wgrad(dgp2, n2, "ffn2_dwg")[0], _ffn_wgrad(dup2, n2, "ffn2_dwu")[0], _ffn_wgrad(act2, do2, "ffn2_dwd")[0]]
    (da1, dq, dga, dgb, dya, dyb, dh2b, d_lng, d_lnb), (got,) = _mix_out_bwd(
        dh2, a1, z, ya, yb, row("a_ln_g"), row("a_ln_b"), wa_out, wb_out, wo, [_ToSibling(g2, ["rows"] * 3)])
    wire2, own2 = _chip_sums(place, g2, got, "rows", "ffn2_chip_sums")
    (dz, d_wa, d_ba, d_wb), (got,) = _conv_bwd(z, da1, dq, dga, dgb, wa_taps, wb_taps, [_ToChips(wire2)])
    half2 = _totals(place, own2, got, "ffn2_totals")
    (g_win,), (tot2,) = _w_in_grad(u, dz, [_SwapHalves(half2)])
    (dh1, d_mix, do1), (got,) = _mix_in_bwd(dh2, h1, row("mix_norm"), dz, win, [_ToSibling([g_win], ["cols"])])
    wire_in, own_in = _chip_sums(place, [g_win], got, "cols", "w_in_chip_sum")
    (dgp1, dup1, act1), (near_in,) = _ffn_bwd_hidden(do1, gp1, up1, f1[2], "ffn1_bwd_hidden", [_ToChips(wire_in, (0, 1))])
    (dx, d_ffn1), _ = _ffn_bwd_input(dh1, h0, row("ffn1_norm"), dgp1, dup1, f1[0], f1[1], "ffn1_bwd_input")
    go, (far_in,) = _mixer_wgrads(a3, dya, q, dyb, mm, dh2b, [_ToChips(wire_in, (2,))])
    half_in = _totals(place, own_in, [[near_in[0], far_in[0]]], "w_in_total")
    g1g, (tot_in, got_o) = _ffn_wgrad(dgp1, n1, "ffn1_dwg", [_SwapHalves(half_in), _ToSibling(go, ["rows"] * 3)])
    wire_o, own_o = _chip_sums(place, go, got_o, "rows", "mixer_chip_sums")
    g1u, (got_g, land_o) = _ffn_wgrad(dup1, n1, "ffn1_dwu", [_ToSibling([g1g], ["rows"]), _ToChips(wire_o)])
    wire_g, own_g = _chip_sums(place, [g1g], got_g, "rows", "ffn1_dwg_chip_sum")
    half_o = _totals(place, own_o, land_o, "mixer_totals")
    g1d, (got_u, land_g, tot_o) = _ffn_wgrad(act1, do1, "ffn1_dwd",
                                             [_ToSibling([g1u], ["rows"]), _ToChips(wire_g), _SwapHalves(half_o)])
    wire_u, own_u = _chip_sums(place, [g1u], got_u, "rows", "ffn1_dwu_chip_sum")
    half_g = _totals(place, own_g, land_g, "ffn1_dwg_total")
    got_d, land_u, tot_g = _exchange("tail_exchange_1", [_ToSibling([g1d], ["rows"]), _ToChips(wire_u), _SwapHalves(half_g)])
    wire_d, own_d = _chip_sums(place, [g1d], got_d, "rows", "ffn1_dwd_chip_sum")
    half_u = _totals(place, own_u, land_u, "ffn1_dwu_total")
    land_d, tot_u = _exchange("tail_exchange_2", [_ToChips(wire_d), _SwapHalves(half_u)])
    half_d = _totals(place, own_d, land_d, "ffn1_dwd_total")
    (tot_d,) = _exchange("tail_exchange_3", [_SwapHalves(half_d)])
    tot1 = [tot_g[0], tot_u[0], tot_d[0]]
    totals = dict(zip(ffn2 + ["w_in"] + ffn1 + outp, list(tot2) + list(tot_in) + tot1 + list(tot_o)))

    vec_grads = {"ffn1_norm": d_ffn1, "mix_norm": d_mix, "a_dw_b": d_ba, "a_ln_g": d_lng, "a_ln_b": d_lnb,
                 "ffn2_norm": d_ffn2, "final_norm": d_final}
    small = _allreduce_small([vec_grads[n] for n in vecs], d_wa, d_wb, loss_part)
    loss = small[LOSS_ROW, 0]
    taps = ["a_dw_w", "b_conv_w"]
    small_out = _small_adamw(place, small, [[P[n].reshape(1, D) for P in (W, M, V)] for n in vecs],
                             [[P[n] for P in (W, M, V)] for n in taps])

    grads, deltas, new_m, new_v = {}, {}, {}, {}
    for n, (g_, d_, m_, v_) in zip(vecs + taps, small_out):
        shp = W[n].shape
        grads[n], deltas[n], new_m[n], new_v[n] = g_.reshape(shp), d_.reshape(shp), m_.reshape(shp), v_.reshape(shp)
    for group, tag in ((ffn1 + ffn2, "ffn"), (["w_in"], "w_in"), (outp, "mixer")):
        ds, ms, vs = _adamw([quarter(W, n) for n in group], [totals[n] for n in group], [quarter(M, n) for n in group],
                            [quarter(V, n) for n in group], tag + "_adamw")
        for n, d_, m_, v_ in zip(group, ds, ms, vs):
            grads[n], deltas[n], new_m[n], new_v[n] = (unquarter(totals[n], n), unquarter(d_, n), unquarter(m_, n),
                                                       unquarter(v_, n))
    return (loss, dx.reshape(x.shape), *[grads[n] for n in names], *[deltas[n] for n in names],
            *[new_m[n] for n in names], *[new_v[n] for n in names])
```

```python
import functools

import jax
import jax.numpy as jnp
from jax import lax
from jax.experimental import pallas as pl
from jax.experimental.pallas import tpu as pltpu

F32 = jnp.float32
BF16 = jnp.bfloat16
EPS = 1e-6
NS = 4
HALO = 32
MESH = pl.DeviceIdType.MESH
IN_HBM = pl.BlockSpec(memory_space=pltpu.HBM)

ADAM_LR = 0.001
ADAM_B1 = 0.9
ADAM_B2 = 0.999
ADAM_EPS = 1e-08
ADAM_WD = 0.01
ADAM_STEP = 10


def _cparams(n_axes, vmem_mb):
    return pltpu.CompilerParams(dimension_semantics=("arbitrary",) * n_axes, vmem_limit_bytes=vmem_mb << 20)


def _tile(n, t):
    return t if n % t == 0 else n


def _row_parts(rows, n=2):
    if rows % (16 * n):
        return [slice(0, rows)]
    return [slice(p * (rows // n), (p + 1) * (rows // n)) for p in range(n)]


def _resident(shape):
    return pl.BlockSpec(shape, lambda *_: (0,) * len(shape), pipeline_mode=pl.Buffered(1))


def _row_tile(n, cap=256):
    for t in (256, 176, 128, 64, 32, 16, 8):
        if t <= cap and n % t == 0:
            return t
    return n


def _dot(a, b):
    return jnp.dot(a, b, preferred_element_type=F32)


def _dot_nt(a, b):
    return lax.dot_general(a, b, (((1,), (1,)), ((), ())), preferred_element_type=F32)


def _dot_tn(a, b):
    return lax.dot_general(a, b, (((0,), (0,)), ((), ())), preferred_element_type=F32)


def _sigmoid(x):
    return jax.nn.sigmoid(x)


def _rms_fwd(x, g):
    r = lax.rsqrt(jnp.mean(x * x, axis=-1, keepdims=True) + EPS)
    return x * r * g


def _rms_bwd(x, g, dn):
    r = lax.rsqrt(jnp.mean(x * x, axis=-1, keepdims=True) + EPS)
    xr = x * r
    dg = jnp.sum(dn * xr, axis=0, keepdims=True)
    w = dn * g
    dx = r * w - xr * (r * r) * jnp.mean(x * w, axis=-1, keepdims=True)
    return dx, dg


def _place():
    x, y, c = lax.axis_index("x"), lax.axis_index("y"), lax.axis_index("c")
    chips = [(1 - x, y), (x, 1 - y), (1 - x, 1 - y)]
    return x, y, c, chips


def _quarter_shape(full_shape, kind):
    r, c = full_shape
    return (r // NS, c) if kind == "rows" else (r, c // NS)


def _half_of_quarter(ref, kind, q, pc):
    qr, qc = _quarter_shape(ref.shape, kind)
    h = qr // 2
    if kind == "rows":
        return ref.at[pl.ds(q * qr + pc * h, h), :]
    return ref.at[pl.ds(pc * h, h), pl.ds(q * qc, qc)]


def _quarter(ref, kind, q):
    qr, qc = _quarter_shape(ref.shape, kind)
    if kind == "rows":
        return ref.at[pl.ds(q * qr, qr), :]
    return ref.at[:, pl.ds(q * qc, qc)]


def _rows_half(ref, pc):
    h = ref.shape[0] // 2
    return ref.at[pl.ds(pc * h, h)]


class _Gather:
    def __init__(self, quarters, kinds):
        self.ins = list(quarters)
        self.kinds = list(kinds)
        n = len(self.ins)
        self.out_shape = [jax.ShapeDtypeStruct((NS * a.shape[0], a.shape[1]) if k == "rows" else (a.shape[0], NS * a.shape[1]),
                                               a.dtype) for a, k in zip(self.ins, self.kinds)]
        self.scratch = [pltpu.SemaphoreType.DMA((n, 6)), pltpu.SemaphoreType.DMA((n, 6)), pltpu.SemaphoreType.DMA((n,))]
        self.aliases = {}

    def _copy(self, outs, sems, a, k, q, pc, to, src=None):
        dst = _half_of_quarter(outs[a], self.kinds[a], q, pc)
        return pltpu.make_async_remote_copy(src_ref=dst if src is None else src, dst_ref=dst,
                                            send_sem=sems[0].at[a, k], recv_sem=sems[1].at[a, k],
                                            device_id=to, device_id_type=MESH)

    def _mine(self, ins, outs, sems, a, p):
        return pltpu.make_async_copy(ins[a], _quarter(outs[a], self.kinds[a], p), sems[2].at[a])

    def start(self, ins, outs, sems):
        x, y, c, chips = _place()
        p = 2 * x + y
        for a in range(len(ins)):
            self._mine(ins, outs, sems, a, p).start()
            for j, chip in enumerate(chips):
                self._copy(outs, sems, a, j, p, c, (*chip, c), src=_rows_half(ins[a], c)).start()

    def relay(self, ins, outs, sems):
        x, y, c, chips = _place()
        sibling = (x, y, 1 - c)
        for a in range(len(ins)):
            for j, (qx, qy) in enumerate(chips):
                q = 2 * qx + qy
                self._copy(outs, sems, a, j, q, c, sibling).wait_recv()
                self._copy(outs, sems, a, 3 + j, q, c, sibling).start()

    def finish(self, ins, outs, sems):
        x, y, c, chips = _place()
        p = 2 * x + y
        sibling = (x, y, 1 - c)
        n = len(ins)
        for a in range(n):
            for j, (qx, qy) in enumerate(chips):
                q = 2 * qx + qy
                self._copy(outs, sems, a, 3 + j, q, 1 - c, sibling).wait_recv()
                self._copy(outs, sems, a, j, p, c, (qx, qy, c), src=_rows_half(ins[a], c)).wait_send()
                self._copy(outs, sems, a, 3 + j, q, c, sibling).wait_send()
            self._mine(ins, outs, sems, a, p).wait()


class _ToSibling:
    def __init__(self, grads, kinds):
        self.ins = list(grads)
        self.kinds = list(kinds)
        n = len(self.ins)
        self.out_shape = []
        for g, k in zip(self.ins, self.kinds):
            qr, qc = _quarter_shape(g.shape, k)
            self.out_shape.append(jax.ShapeDtypeStruct((NS, qr // 2, qc), g.dtype))
        self.scratch = [pltpu.SemaphoreType.DMA((n, NS)), pltpu.SemaphoreType.DMA((n, NS))]
        self.aliases = {}

    def _copies(self, ins, outs, sems):
        x, y, c, _ = _place()
        return [pltpu.make_async_remote_copy(src_ref=_half_of_quarter(ins[a], self.kinds[a], q, 1 - c), dst_ref=outs[a].at[q],
                                             send_sem=sems[0].at[a, q], recv_sem=sems[1].at[a, q],
                                             device_id=(x, y, 1 - c), device_id_type=MESH)
                for a in range(len(ins)) for q in range(NS)]

    def start(self, ins, outs, sems):
        for cp in self._copies(ins, outs, sems):
            cp.start()

    def finish(self, ins, outs, sems):
        for cp in self._copies(ins, outs, sems):
            cp.wait()


class _ToChips:
    def __init__(self, sums, which=(0, 1, 2)):
        self.ins = list(sums)
        self.which = tuple(which)
        n, m = len(self.ins), len(self.which)
        self.out_shape = [jax.ShapeDtypeStruct((m,) + s.shape[1:], s.dtype) for s in self.ins]
        self.scratch = [pltpu.SemaphoreType.DMA((n, m)), pltpu.SemaphoreType.DMA((n, m))]
        self.aliases = {}

    def _copies(self, ins, outs, sems):
        x, y, c, chips = _place()
        return [pltpu.make_async_remote_copy(src_ref=ins[a].at[2 * chips[j][0] + chips[j][1]], dst_ref=outs[a].at[k],
                                             send_sem=sems[0].at[a, k], recv_sem=sems[1].at[a, k],
                                             device_id=(*chips[j], c), device_id_type=MESH)
                for a in range(len(ins)) for k, j in enumerate(self.which)]

    def start(self, ins, outs, sems):
        for cp in self._copies(ins, outs, sems):
            cp.start()

    def finish(self, ins, outs, sems):
        for cp in self._copies(ins, outs, sems):
            cp.wait()


class _SwapHalves:
    def __init__(self, quarters):
        self.ins = list(quarters)
        n = len(self.ins)
        self.out_shape = [jax.ShapeDtypeStruct(g.shape, g.dtype) for g in self.ins]
        self.scratch = [pltpu.SemaphoreType.DMA((n,)), pltpu.SemaphoreType.DMA((n,))]
        self.aliases = {a: a for a in range(n)}

    def _copy(self, outs, sems, a, pc):
        x, y, c, _ = _place()
        rows = _rows_half(outs[a], pc)
        return pltpu.make_async_remote_copy(src_ref=rows, dst_ref=rows, send_sem=sems[0].at[a], recv_sem=sems[1].at[a],
                                            device_id=(x, y, 1 - c), device_id_type=MESH)

    def start(self, ins, outs, sems):
        c = lax.axis_index("c")
        for a in range(len(outs)):
            self._copy(outs, sems, a, c).start()

    def finish(self, ins, outs, sems):
        c = lax.axis_index("c")
        for a in range(len(outs)):
            self._copy(outs, sems, a, c).wait_send()
            self._copy(outs, sems, a, 1 - c).wait_recv()


def _call(name, grid, compute, in_specs, out_specs, out_shape, scratch, vmem_mb, args, jobs=()):
    n_in, n_out, n_scr = len(in_specs), len(out_specs), len(scratch)
    ji = [len(j.ins) for j in jobs]
    jo = [len(j.out_shape) for j in jobs]
    js = [len(j.scratch) for j in jobs]

    def body(*refs):
        pos = [0]

        def take(k):
            r = refs[pos[0]:pos[0] + k]
            pos[0] += k
            return r

        ins, jins = take(n_in), [take(k) for k in ji]
        outs, jouts = take(n_out), [take(k) for k in jo]
        scr, jscr = take(n_scr), [take(k) for k in js]
        if jobs and grid:
            ids = [pl.program_id(a) for a in range(len(grid))]
            first = functools.reduce(jnp.logical_and, [i == 0 for i in ids])
            last = functools.reduce(jnp.logical_and, [i == g - 1 for i, g in zip(ids, grid)])

            @pl.when(first)
            def _():
                for j, a, b, c in zip(jobs, jins, jouts, jscr):
                    j.start(a, b, c)

            @pl.when(last)
            def _():
                for j, a, b, c in zip(jobs, jins, jouts, jscr):
                    if hasattr(j, "relay"):
                        j.relay(a, b, c)
        elif jobs:
            for j, a, b, c in zip(jobs, jins, jouts, jscr):
                j.start(a, b, c)
            for j, a, b, c in zip(jobs, jins, jouts, jscr):
                if hasattr(j, "relay"):
                    j.relay(a, b, c)
        compute(ins, outs, scr)
        if jobs and grid:
            @pl.when(last)
            def _():
                for j, a, b, c in zip(jobs, jins, jouts, jscr):
                    j.finish(a, b, c)
        elif jobs:
            for j, a, b, c in zip(jobs, jins, jouts, jscr):
                j.finish(a, b, c)

    aliases = {}
    in_off, out_off = n_in, n_out
    for j, a, b in zip(jobs, ji, jo):
        for s, d in j.aliases.items():
            aliases[in_off + s] = out_off + d
        in_off += a
        out_off += b
    res = pl.pallas_call(
        body, name=name, grid=grid,
        in_specs=list(in_specs) + [IN_HBM] * sum(ji), out_specs=list(out_specs) + [IN_HBM] * sum(jo),
        out_shape=list(out_shape) + [pltpu.HBM(s.shape, s.dtype) for j in jobs for s in j.out_shape],
        scratch_shapes=list(scratch) + [s for j in jobs for s in j.scratch],
        input_output_aliases=aliases, compiler_params=_cparams(len(grid), vmem_mb),
    )(*args, *[a for j in jobs for a in j.ins])
    res = list(res)
    main, rest, jres = res[:n_out], res[n_out:], []
    for k in jo:
        jres.append(rest[:k])
        rest = rest[k:]
    return main, jres


def _exchange(name, jobs):
    return _call(name, (), lambda ins, outs, scr: None, [], [], [], [], 16, [], jobs)[1]


def _small_rows(ka, kb):
    first_a = 8
    first_b = first_a + -(-ka // 8) * 8
    return first_a, first_b, first_b + -(-kb // 8) * 8


LOSS_ROW = 7


def _allreduce_small(vecs, taps_a, taps_b, loss_part):
    n = len(vecs)
    C = vecs[0].shape[1]
    NQ, KA, CB = taps_a.shape
    KB = taps_b.shape[1]
    first_a, first_b, R = _small_rows(KA, KB)
    assert n <= LOSS_ROW < first_a
    N = 8

    def body(*refs):
        vec_refs = refs[:n]
        ta_ref, tb_ref, loss_ref, out_ref, v_ref, gath, send_sems, recv_sems, local_sem = refs[n:]
        v_ref[...] = jnp.zeros_like(v_ref)
        v_ref[LOSS_ROW:LOSS_ROW + 1, 0:loss_ref.shape[1]] = loss_ref[0:1, :]
        for i, r in enumerate(vec_refs):
            v_ref[i:i + 1, :] = r[...]
        for q in range(NQ):
            v_ref[first_a:first_a + KA, q * CB:(q + 1) * CB] = ta_ref[q]
            v_ref[first_b:first_b + KB, q * CB:(q + 1) * CB] = tb_ref[q]
        x, y, c, chips = _place()
        me, sibling = (x, y, c), (x, y, 1 - c)

        def rows(px, py, pc):
            return gath.at[pl.ds((4 * px + 2 * py + pc) * R, R), :]

        def copy(k, block, to, src=None):
            return pltpu.make_async_remote_copy(src_ref=rows(*block) if src is None else src, dst_ref=rows(*block),
                                                send_sem=send_sems.at[k], recv_sem=recv_sems.at[k],
                                                device_id=to, device_id_type=MESH)

        mine = pltpu.make_async_copy(v_ref, rows(*me), local_sem)
        mine.start()
        first = [copy(0, me, sibling, src=v_ref)]
        first += [copy(1 + j, me, (*chip, c), src=v_ref) for j, chip in enumerate(chips)]
        for cp in first:
            cp.start()
        passed = [copy(4 + j, (*chip, c), sibling) for j, chip in enumerate(chips)]
        for j, chip in enumerate(chips):
            copy(1 + j, (*chip, c), me).wait_recv()
            passed[j].start()
        copy(0, sibling, me).wait_recv()
        for j, chip in enumerate(chips):
            copy(4 + j, (*chip, 1 - c), me).wait_recv()
        for cp in first + passed:
            cp.wait_send()
        mine.wait()
        acc = gath[0:R, :]
        for d in range(1, N):
            acc = acc + gath[d * R:(d + 1) * R, :]
        out_ref[...] = acc

    vmem = pl.BlockSpec(memory_space=pltpu.VMEM)
    return pl.pallas_call(
        body, name="allreduce_small",
        in_specs=[vmem] * (n + 3), out_specs=vmem,
        out_shape=jax.ShapeDtypeStruct((R, C), F32),
        scratch_shapes=[pltpu.VMEM((R, C), F32), pltpu.VMEM((N * R, C), F32), pltpu.SemaphoreType.DMA((7,)),
                        pltpu.SemaphoreType.DMA((7,)), pltpu.SemaphoreType.DMA],
    )(*vecs, taps_a, taps_b, loss_part)


def _adamw_math(w, g, m, v):
    c1 = 1.0 - ADAM_B1 ** ADAM_STEP
    c2 = 1.0 - ADAM_B2 ** ADAM_STEP
    mn = ADAM_B1 * m + (1.0 - ADAM_B1) * g
    vn = ADAM_B2 * v + (1.0 - ADAM_B2) * (g * g)
    return -ADAM_LR * ((mn / c1) / (jnp.sqrt(vn / c2) + ADAM_EPS) + ADAM_WD * w), mn, vn


def _small_adamw(place, small, vec_wmv, tap_wmv):
    n = len(vec_wmv)
    D = small.shape[1]
    CB = tap_wmv[0][0].shape[2]
    ks = [t[0].shape[1] for t in tap_wmv]
    firsts = _small_rows(*ks)[:2]

    def body(place_ref, small_ref, *refs):
        ins, outs = refs[:3 * (n + 2)], refs[3 * (n + 2):]
        chip = place_ref[0]
        for i in range(n):
            g = small_ref[i:i + 1, :]
            d, mn, vn = _adamw_math(ins[3 * i][...], g, ins[3 * i + 1][...], ins[3 * i + 2][...])
            for o, val in zip(outs[4 * i:4 * i + 4], (g, d, mn, vn)):
                o[...] = val
        for t, (row0, k) in enumerate(zip(firsts, ks)):
            g = jnp.zeros((k, CB), F32)
            for q in range(D // CB):
                g = g + jnp.where(chip == q, small_ref[row0:row0 + k, q * CB:(q + 1) * CB], 0.0)
            w_ref, m_ref, v_ref = ins[3 * (n + t):3 * (n + t) + 3]
            d, mn, vn = _adamw_math(w_ref[0], g, m_ref[0], v_ref[0])
            for o, val in zip(outs[4 * (n + t):4 * (n + t) + 4], (g, d, mn, vn)):
                o[0] = val

    flat = [a for wmv in list(vec_wmv) + list(tap_wmv) for a in wmv]
    shapes = [jax.ShapeDtypeStruct(wmv[0].shape, F32) for wmv in list(vec_wmv) + list(tap_wmv) for _ in range(4)]
    vmem = pl.BlockSpec(memory_space=pltpu.VMEM)
    res = pl.pallas_call(
        body, name="small_adamw",
        in_specs=[pl.BlockSpec(memory_space=pltpu.SMEM)] + [vmem] * (1 + len(flat)), out_specs=[vmem] * len(shapes),
        out_shape=shapes,
    )(place, small, *flat)
    return [res[4 * i:4 * i + 4] for i in range(n + 2)]


def _ffn_fwd(h, g, wg, wu, wd, name, jobs=(), head=None):
    S, D = h.shape
    F = wg.shape[0]
    ts = _tile(S, 512)
    fb = _tile(F, F // 2)
    nf = F // fb

    def compute(ins, outs, scr):
        h_ref, g_ref, wg_ref, wu_ref, wd_ref = ins[:5]
        n_ref, gp_ref, up_ref = outs[-3:]
        x = h_ref[...]
        n = _rms_fwd(x, g_ref[...]).astype(BF16)
        n_ref[...] = n
        acc = None
        for j in range(nf):
            cols = slice(j * fb, (j + 1) * fb)
            gp = _dot_nt(n, wg_ref[cols, :])
            up = _dot_nt(n, wu_ref[cols, :])
            gp_ref[:, cols] = gp.astype(BF16)
            up_ref[:, cols] = up.astype(BF16)
            part = _dot((gp * _sigmoid(gp) * up).astype(BF16), wd_ref[cols, :])
            acc = part if acc is None else acc + part
        ho = x + 0.5 * acc
        if head is None:
            outs[0][...] = ho
            return
        t_ref, gf_ref = ins[5:]
        dh_ref, do_ref, dgf_ref, loss_ref = outs[:4]

        @pl.when(pl.program_id(0) == 0)
        def _():
            dgf_ref[...] = jnp.zeros_like(dgf_ref)
            loss_ref[...] = jnp.zeros_like(loss_ref)

        err = _rms_fwd(ho, gf_ref[...]) - t_ref[...]
        loss_ref[...] += (0.5 / D) * jnp.sum(err * err)
        dx, dg = _rms_bwd(ho, gf_ref[...], err * (1.0 / D))
        dh_ref[...] = dx
        do_ref[...] = (0.5 * dx).astype(BF16)
        dgf_ref[...] += dg

    tok = pl.BlockSpec((ts, D), lambda i: (i, 0))
    row = pl.BlockSpec((1, D), lambda i: (0, 0))
    wsp = _resident((F, D))
    hid = pl.BlockSpec((ts, F), lambda i: (i, 0))
    saved = [jax.ShapeDtypeStruct((S, D), BF16), jax.ShapeDtypeStruct((S, F), BF16), jax.ShapeDtypeStruct((S, F), BF16)]
    if head is None:
        return _call(name, (S // ts,), compute, [tok, row, wsp, wsp, wsp], [tok, tok, hid, hid],
                     [jax.ShapeDtypeStruct((S, D), F32)] + saved, [], 56, [h, g, wg, wu, wd], jobs)
    return _call(name, (S // ts,), compute, [tok, row, wsp, wsp, wsp, tok, row],
                 [tok, tok, row, pl.BlockSpec((8, 128), lambda i: (0, 0)), tok, hid, hid],
                 [jax.ShapeDtypeStruct((S, D), F32), jax.ShapeDtypeStruct((S, D), BF16), jax.ShapeDtypeStruct((1, D), F32),
                  jax.ShapeDtypeStruct((8, 128), F32)] + saved, [], 60, [h, g, wg, wu, wd, *head], jobs)


def _ffn_bwd_hidden(do, gp, up, wd, name, jobs=()):
    S, D = do.shape
    F = wd.shape[0]
    ts = _tile(S, 1024)
    fb = _tile(F, F // 2)
    def compute(ins, outs, scr):
        do_ref, gp_ref, up_ref, wd_ref = ins
        dgp_ref, dup_ref, a_ref = outs
        parts = _row_parts(ts, 4)
        das = [_dot_nt(do_ref[rows, :], wd_ref[...]) for rows in parts]
        for rows, da in zip(parts, das):
            gf = gp_ref[rows, :].astype(F32)
            uf = up_ref[rows, :].astype(F32)
            sg = _sigmoid(gf)
            si = gf * sg
            dgp_ref[rows, :] = (da * uf * (sg * (1.0 + gf * (1.0 - sg)))).astype(BF16)
            dup_ref[rows, :] = (da * si).astype(BF16)
            a_ref[rows, :] = (si * uf).astype(BF16)

    tok = pl.BlockSpec((ts, D), lambda s, i: (i, 0))
    hid = pl.BlockSpec((ts, fb), lambda s, i: (i, s))
    return _call(name, (F // fb, S // ts), compute, [tok, hid, hid, pl.BlockSpec((fb, D), lambda s, i: (s, 0))],
                 [hid, hid, hid], [jax.ShapeDtypeStruct((S, F), BF16)] * 3, [], 56, [do, gp, up, wd], jobs)


def _ffn_bwd_input(dh, h, g, dgp, dup, wg, wu, name, jobs=()):
    S, D = h.shape
    F = wg.shape[0]
    ts = _tile(S, 512)

    def compute(ins, outs, scr):
        dh_ref, h_ref, g_ref, dgp_ref, dup_ref, wg_ref, wu_ref = ins
        dhi_ref, dg_ref = outs

        @pl.when(pl.program_id(0) == 0)
        def _():
            dg_ref[...] = jnp.zeros_like(dg_ref)

        dn = _dot(dgp_ref[...], wg_ref[...]) + _dot(dup_ref[...], wu_ref[...])
        dx, dg = _rms_bwd(h_ref[...], g_ref[...], dn)
        dhi_ref[...] = dh_ref[...] + dx
        dg_ref[...] += dg

    tok = pl.BlockSpec((ts, D), lambda i: (i, 0))
    hid = pl.BlockSpec((ts, F), lambda i: (i, 0))
    row = pl.BlockSpec((1, D), lambda i: (0, 0))
    return _call(name, (S // ts,), compute, [tok, tok, row, hid, hid, _resident((F, D)), _resident((F, D))], [tok, row],
                 [jax.ShapeDtypeStruct((S, D), F32), jax.ShapeDtypeStruct((1, D), F32)], [], 56,
                 [dh, h, g, dgp, dup, wg, wu], jobs)


def _ffn_wgrad(hid, tok, name, jobs=()):
    S, D = tok.shape
    F = hid.shape[1]
    fb = _tile(F, F // 2)

    def compute(ins, outs, scr):
        outs[0][...] = _dot_tn(ins[0][...], ins[1][...])

    main, jres = _call(name, (F // fb,), compute,
                       [pl.BlockSpec((S, fb), lambda j: (0, j)), _resident(tok.shape)],
                       [pl.BlockSpec((fb, D), lambda j: (j, 0))], [jax.ShapeDtypeStruct((F, D), F32)], [], 56,
                       [hid, tok], jobs)
    return main[0], jres


def _mix_in_fwd(h, g, win, jobs=()):
    S, D = h.shape
    NG = win.shape[1] // D
    ts = _tile(S, 512)

    def compute(ins, outs, scr):
        h_ref, g_ref, w_ref = ins
        u_ref, z_ref = outs
        u = _rms_fwd(h_ref[...], g_ref[...]).astype(BF16)
        u_ref[...] = u
        for k in range(NG):
            z_ref[k] = _dot(u, w_ref[:, k * D:(k + 1) * D]).astype(BF16)

    return _call("mix_in_fwd", (S // ts,), compute,
                 [pl.BlockSpec((ts, D), lambda i: (i, 0)), pl.BlockSpec((1, D), lambda i: (0, 0)), _resident(win.shape)],
                 [pl.BlockSpec((ts, D), lambda i: (i, 0)), pl.BlockSpec((NG, ts, D), lambda i: (0, i, 0))],
                 [jax.ShapeDtypeStruct((S, D), BF16), jax.ShapeDtypeStruct((NG, S, D), BF16)],
                 [], 48, [h, g, win], jobs)


SUBLANES = 8


def _shifted_copies(s):
    n = s.shape[1] - SUBLANES
    for r in range(1, SUBLANES):
        s[r, 0:n, :] = s[0, r:r + n, :]


def _window(s, o, rows):
    r = o % SUBLANES
    return s[r, o - r:o - r + rows, :]


def _conv_fwd(z, wa, ba, wb, jobs=()):
    _, S, D = z.shape
    _, KA, CB = wa.shape
    KB = wb.shape[1]
    ts = _tile(S, 1024)
    r = ts // HALO
    CH = min(64, ts)

    def compute(ins, outs, scr):
        z_ref, zh_ref, wa_ref, ba_ref, wb_ref = ins
        a1_ref, q_ref = outs
        sa, sb = scr
        keep = (pl.program_id(1) > 0).astype(F32)
        sa[0, HALO:HALO + ts, :] = z_ref[0].astype(F32) * _sigmoid(z_ref[1].astype(F32))
        sa[0, 0:HALO, :] = zh_ref[0].astype(F32) * _sigmoid(zh_ref[1].astype(F32)) * keep
        _shifted_copies(sa)
        sb[HALO:HALO + ts, :] = z_ref[3].astype(F32) * z_ref[4].astype(F32)
        sb[0:HALO, :] = zh_ref[3].astype(F32) * zh_ref[4].astype(F32) * keep
        wak = [wa_ref[0, k:k + 1, :] for k in range(KA)]
        wbk = [wb_ref[0, k:k + 1, :] for k in range(KB)]
        for c0 in range(0, ts, CH):
            acc = jnp.broadcast_to(ba_ref[...], (CH, CB))
            for k in range(KA):
                acc = acc + wak[k] * _window(sa, c0 + HALO - (KA - 1) + k, CH)
            a1_ref[c0:c0 + CH, :] = acc
            v = jnp.zeros((CH, CB), F32)
            for k in range(KB):
                o = c0 + HALO - (KB - 1) + k
                v = v + wbk[k] * sb[o:o + CH, :]
            q_ref[c0:c0 + CH, :] = (z_ref[2, c0:c0 + CH, :].astype(F32) * v).astype(BF16)

    return _call("conv_fwd", (D // CB, S // ts), compute,
                 [pl.BlockSpec((5, ts, CB), lambda j, i: (0, i, j)),
                  pl.BlockSpec((5, HALO, CB), lambda j, i: (0, jnp.maximum(i * r - 1, 0), j)),
                  pl.BlockSpec((1, KA, CB), lambda j, i: (j, 0, 0)), pl.BlockSpec((1, CB), lambda j, i: (0, j)),
                  pl.BlockSpec((1, KB, CB), lambda j, i: (j, 0, 0))],
                 [pl.BlockSpec((ts, CB), lambda j, i: (i, j)), pl.BlockSpec((ts, CB), lambda j, i: (i, j))],
                 [jax.ShapeDtypeStruct((S, D), F32), jax.ShapeDtypeStruct((S, D), BF16)],
                 [pltpu.VMEM((SUBLANES, HALO + ts, CB), F32), pltpu.VMEM((HALO + ts, CB), F32)], 40, [z, z, wa, ba, wb], jobs)


def _ln_stats(a1):
    mu = jnp.mean(a1, axis=-1, keepdims=True)
    xc = a1 - mu
    rstd = lax.rsqrt(jnp.mean(xc * xc, axis=-1, keepdims=True) + EPS)
    return xc * rstd, rstd


def _mix_out_fwd(h1, a1, q, z, lng, lnb, wa, wb, wo):
    S, D = h1.shape
    ts = _tile(S, 512)

    def compute(ins, outs, scr):
        h_ref, a1_ref, q_ref, ga_ref, gb_ref, lng_ref, lnb_ref, wa_ref, wb_ref, wo_ref = ins
        h2_ref, a3_ref, m_ref, ya_ref, yb_ref = outs
        xhat, _ = _ln_stats(a1_ref[...])
        a2 = xhat * lng_ref[...] + lnb_ref[...]
        a3 = (a2 * _sigmoid(a2)).astype(BF16)
        a3_ref[...] = a3
        ya = _dot(a3, wa_ref[...])
        yb = _dot(q_ref[...], wb_ref[...])
        ya_ref[...] = ya.astype(BF16)
        yb_ref[...] = yb.astype(BF16)
        m = (_sigmoid(ga_ref[0].astype(F32)) * ya + _sigmoid(gb_ref[0].astype(F32)) * yb).astype(BF16)
        m_ref[...] = m
        h2_ref[...] = h_ref[...] + _dot(m, wo_ref[...])

    tok = pl.BlockSpec((ts, D), lambda i: (i, 0))
    row = pl.BlockSpec((1, D), lambda i: (0, 0))
    mat = _resident((D, D))
    return _call("mix_out_fwd", (S // ts,), compute,
                 [tok, tok, tok, pl.BlockSpec((1, ts, D), lambda i: (5, i, 0)), pl.BlockSpec((1, ts, D), lambda i: (6, i, 0)),
                  row, row, mat, mat, mat], [tok] * 5,
                 [jax.ShapeDtypeStruct((S, D), F32)] + [jax.ShapeDtypeStruct((S, D), BF16)] * 4,
                 [], 56, [h1, a1, q, z, z, lng, lnb, wa, wb, wo])[0]


def _mix_out_bwd(dh2, a1, z, ya, yb, lng, lnb, wa, wb, wo, jobs=()):
    S, D = dh2.shape
    ts = _tile(S, 512)

    def compute(ins, outs, scr):
        dh_ref, a1_ref, ga_ref, gb_ref, ya_ref, yb_ref, lng_ref, lnb_ref, wa_ref, wb_ref, wo_ref = ins
        da1_ref, dq_ref, dga_ref, dgb_ref, dya_ref, dyb_ref, dhb_ref, dlg_ref, dlb_ref = outs

        @pl.when(pl.program_id(0) == 0)
        def _():
            dlg_ref[...] = jnp.zeros_like(dlg_ref)
            dlb_ref[...] = jnp.zeros_like(dlb_ref)

        for rows in _row_parts(ts):
            dhb = dh_ref[rows, :].astype(BF16)
            dhb_ref[rows, :] = dhb
            dm = _dot_nt(dhb, wo_ref[...])
            sa = _sigmoid(ga_ref[0, rows, :].astype(F32))
            sb = _sigmoid(gb_ref[0, rows, :].astype(F32))
            dga_ref[rows, :] = (dm * ya_ref[rows, :].astype(F32) * sa * (1.0 - sa)).astype(BF16)
            dgb_ref[rows, :] = (dm * yb_ref[rows, :].astype(F32) * sb * (1.0 - sb)).astype(BF16)
            dya = (sa * dm).astype(BF16)
            dyb = (sb * dm).astype(BF16)
            dya_ref[rows, :] = dya
            dyb_ref[rows, :] = dyb
            dq_ref[rows, :] = _dot_nt(dyb, wb_ref[...]).astype(BF16)
            da3 = _dot_nt(dya, wa_ref[...])
            xhat, rstd = _ln_stats(a1_ref[rows, :])
            a2 = xhat * lng_ref[...] + lnb_ref[...]
            sg = _sigmoid(a2)
            da2 = da3 * (sg * (1.0 + a2 * (1.0 - sg)))
            dlg_ref[...] += jnp.sum(da2 * xhat, axis=0, keepdims=True)
            dlb_ref[...] += jnp.sum(da2, axis=0, keepdims=True)
            dxh = da2 * lng_ref[...]
            da1_ref[rows, :] = (rstd * (dxh - jnp.mean(dxh, axis=-1, keepdims=True)
                                        - xhat * jnp.mean(dxh * xhat, axis=-1, keepdims=True))).astype(BF16)

    tok = pl.BlockSpec((ts, D), lambda i: (i, 0))
    row = pl.BlockSpec((1, D), lambda i: (0, 0))
    mat = _resident((D, D))
    return _call("mix_out_bwd", (S // ts,), compute,
                 [tok, tok, pl.BlockSpec((1, ts, D), lambda i: (5, i, 0)), pl.BlockSpec((1, ts, D), lambda i: (6, i, 0)),
                  tok, tok, row, row, mat, mat, mat], [tok] * 7 + [row, row],
                 [jax.ShapeDtypeStruct((S, D), BF16)] * 7 + [jax.ShapeDtypeStruct((1, D), F32)] * 2,
                 [], 56, [dh2, a1, z, z, ya, yb, lng, lnb, wa, wb, wo], jobs)


def _mixer_wgrads(a3, dya, q, dyb, mm, dhb, jobs=()):
    S, D = a3.shape
    tk = _tile(S, 512)

    def compute(ins, outs, scr):
        @pl.when(pl.program_id(0) == 0)
        def _():
            for o in outs:
                o[...] = jnp.zeros_like(o)

        for t in range(3):
            outs[t][...] += _dot_tn(ins[2 * t][...], ins[2 * t + 1][...])

    tok = pl.BlockSpec((tk, D), lambda k: (k, 0))
    return _call("mixer_wgrads", (S // tk,), compute, [tok] * 6, [pl.BlockSpec((D, D), lambda k: (0, 0))] * 3,
                 [jax.ShapeDtypeStruct((D, D), F32)] * 3, [], 56, [a3, dya, q, dyb, mm, dhb], jobs)


def _conv_bwd(z, da1, dq, dga, dgb, wa, wb, jobs=()):
    NG, S, D = z.shape
    _, KA, CB = wa.shape
    KB = wb.shape[1]
    ts = _tile(S, 1024)
    r = ts // HALO
    nt = S // ts
    CH = min(64, ts)
    last_halo = S // HALO - 1

    def compute(ins, outs, scr):
        z_ref, zp_ref, zn_ref, da1_ref, da1n_ref, dq_ref, dqn_ref, dga_ref, dgb_ref, wa_ref, wb_ref = ins
        dz_ref, dwa_ref, dba_ref, dwb_ref = outs
        sa0, sd, sp, sv, acca, accb = scr
        i = pl.program_id(1)
        prev = (i > 0).astype(F32)
        nxt = (i < nt - 1).astype(F32)

        @pl.when(i == 0)
        def _():
            acca[...] = jnp.zeros_like(acca)
            accb[...] = jnp.zeros_like(accb)
            dba_ref[...] = jnp.zeros_like(dba_ref)

        sa0[0, HALO:HALO + ts, :] = z_ref[0].astype(F32) * _sigmoid(z_ref[1].astype(F32))
        sa0[0, 0:HALO, :] = zp_ref[0].astype(F32) * _sigmoid(zp_ref[1].astype(F32)) * prev
        _shifted_copies(sa0)
        sp[HALO:HALO + ts, :] = z_ref[3].astype(F32) * z_ref[4].astype(F32)
        sp[0:HALO, :] = zp_ref[3].astype(F32) * zp_ref[4].astype(F32) * prev
        sd[0, 0:ts, :] = da1_ref[...].astype(F32)
        sd[0, ts:ts + HALO, :] = da1n_ref[...].astype(F32) * nxt
        _shifted_copies(sd)
        sv[0:ts, :] = dq_ref[...].astype(F32) * z_ref[2].astype(F32)
        sv[ts:ts + HALO, :] = dqn_ref[...].astype(F32) * zn_ref[2].astype(F32) * nxt
        dba_ref[...] += jnp.sum(sd[0, 0:ts, :], axis=0, keepdims=True)
        wak = [wa_ref[0, k:k + 1, :] for k in range(KA)]
        wbk = [wb_ref[0, k:k + 1, :] for k in range(KB)]
        for c0 in range(0, ts, CH):
            rows = slice(c0, c0 + CH)
            d1 = sd[0, rows, :]
            da0 = jnp.zeros((CH, CB), F32)
            for k in range(KA):
                da0 = da0 + wak[k] * _window(sd, c0 + (KA - 1) - k, CH)
                a0w = _window(sa0, c0 + HALO - (KA - 1) + k, CH)
                acca[k] += jnp.sum((d1 * a0w).reshape(CH // 8, 8, CB), axis=0)
            val = z_ref[0, rows, :].astype(F32)
            sg = _sigmoid(z_ref[1, rows, :].astype(F32))
            dz_ref[0, rows, :] = (da0 * sg).astype(BF16)
            dz_ref[1, rows, :] = (da0 * val * sg * (1.0 - sg)).astype(BF16)
            dv = sv[rows, :]
            v = jnp.zeros((CH, CB), F32)
            dp = jnp.zeros((CH, CB), F32)
            for k in range(KB):
                o = c0 + HALO - (KB - 1) + k
                pw = sp[o:o + CH, :]
                v = v + wbk[k] * pw
                accb[k] += jnp.sum((dv * pw).reshape(CH // 8, 8, CB), axis=0)
                o = c0 + (KB - 1) - k
                dp = dp + wbk[k] * sv[o:o + CH, :]
            dz_ref[2, rows, :] = (dq_ref[rows, :].astype(F32) * v).astype(BF16)
            dz_ref[3, rows, :] = (dp * z_ref[4, rows, :].astype(F32)).astype(BF16)
            dz_ref[4, rows, :] = (dp * z_ref[3, rows, :].astype(F32)).astype(BF16)
        dz_ref[5] = dga_ref[...]
        dz_ref[6] = dgb_ref[...]

        @pl.when(i == nt - 1)
        def _():
            dwa_ref[0] = jnp.sum(acca[...], axis=1)
            dwb_ref[0] = jnp.sum(accb[...], axis=1)

    zt = pl.BlockSpec((5, ts, CB), lambda j, i: (0, i, j))
    zp = pl.BlockSpec((5, HALO, CB), lambda j, i: (0, jnp.maximum(i * r - 1, 0), j))
    zn = pl.BlockSpec((5, HALO, CB), lambda j, i: (0, jnp.minimum((i + 1) * r, last_halo), j))
    tok = pl.BlockSpec((ts, CB), lambda j, i: (i, j))
    tokn = pl.BlockSpec((HALO, CB), lambda j, i: (jnp.minimum((i + 1) * r, last_halo), j))
    return _call("conv_bwd", (D // CB, nt), compute,
                 [zt, zp, zn, tok, tokn, tok, tokn, tok, tok,
                  pl.BlockSpec((1, KA, CB), lambda j, i: (j, 0, 0)), pl.BlockSpec((1, KB, CB), lambda j, i: (j, 0, 0))],
                 [pl.BlockSpec((NG, ts, CB), lambda j, i: (0, i, j)), pl.BlockSpec((1, KA, CB), lambda j, i: (j, 0, 0)),
                  pl.BlockSpec((1, CB), lambda j, i: (0, j)), pl.BlockSpec((1, KB, CB), lambda j, i: (j, 0, 0))],
                 [jax.ShapeDtypeStruct((NG, S, D), BF16), jax.ShapeDtypeStruct((D // CB, KA, CB), F32),
                  jax.ShapeDtypeStruct((1, D), F32), jax.ShapeDtypeStruct((D // CB, KB, CB), F32)],
                 [pltpu.VMEM((SUBLANES, HALO + ts, CB), F32), pltpu.VMEM((SUBLANES, ts + HALO, CB), F32),
                  pltpu.VMEM((HALO + ts, CB), F32), pltpu.VMEM((ts + HALO, CB), F32),
                  pltpu.VMEM((KA, 8, CB), F32), pltpu.VMEM((KB, 8, CB), F32)],
                 48, [z, z, z, da1, da1, dq, dq, dga, dgb, wa, wb], jobs)


def _mix_in_bwd(dh2, h1, g, dz, win, jobs=()):
    S, D = h1.shape
    NG = dz.shape[0]
    ts = _tile(S, 512)

    def compute(ins, outs, scr):
        dh_ref, h_ref, g_ref, dz_ref, w_ref = ins
        dhi_ref, dg_ref, do_ref = outs

        @pl.when(pl.program_id(0) == 0)
        def _():
            dg_ref[...] = jnp.zeros_like(dg_ref)

        du = _dot_nt(dz_ref[0], w_ref[:, 0:D])
        for k in range(1, NG):
            du = du + _dot_nt(dz_ref[k], w_ref[:, k * D:(k + 1) * D])
        dx, dg = _rms_bwd(h_ref[...], g_ref[...], du)
        dhi = dh_ref[...] + dx
        dhi_ref[...] = dhi
        do_ref[...] = (0.5 * dhi).astype(BF16)
        dg_ref[...] += dg

    tok = pl.BlockSpec((ts, D), lambda i: (i, 0))
    row = pl.BlockSpec((1, D), lambda i: (0, 0))
    return _call("mix_in_bwd", (S // ts,), compute,
                 [tok, tok, row, pl.BlockSpec((NG, ts, D), lambda i: (0, i, 0)), _resident(win.shape)],
                 [tok, row, tok],
                 [jax.ShapeDtypeStruct((S, D), F32), jax.ShapeDtypeStruct((1, D), F32), jax.ShapeDtypeStruct((S, D), BF16)],
                 [], 56, [dh2, h1, g, dz, win], jobs)


def _w_in_grad(u, dz, jobs=()):
    S, D = u.shape
    NG = dz.shape[0]

    def compute(ins, outs, scr):
        outs[0][...] = _dot_tn(ins[0][...], ins[1][0])

    return _call("w_in_grad", (NG,), compute,
                 [_resident(u.shape), pl.BlockSpec((1, S, D), lambda j: (j, 0, 0))],
                 [pl.BlockSpec((D, D), lambda j: (0, j))], [jax.ShapeDtypeStruct((D, NG * D), F32)], [], 48, [u, dz], jobs)


def _chip_sums(place, grads, got, kind, name):
    n = len(grads)
    qr, qc = _quarter_shape(grads[0].shape, kind)
    h = qr // 2
    tr = _row_tile(h)
    nr = h // tr

    def body(pc_ref, *refs):
        g_refs, got_refs, b_refs, f_refs = refs[:n], refs[n:2 * n], refs[2 * n:3 * n], refs[3 * n:]
        own = pl.program_id(1) == pc_ref[0]
        for a in range(n):
            s = g_refs[a][...] + got_refs[a][0]
            b_refs[a][0] = s.astype(BF16)

            @pl.when(own)
            def _():
                f_refs[a][...] = s

    if kind == "rows":
        gspec = pl.BlockSpec((tr, qc), lambda r, q, pc: (q * (2 * nr) + pc[1] * nr + r, 0))
    else:
        gspec = pl.BlockSpec((tr, qc), lambda r, q, pc: (pc[1] * nr + r, q))
    lspec = pl.BlockSpec((1, tr, qc), lambda r, q, pc: (q, r, 0))
    res = pl.pallas_call(
        body, name=name,
        grid_spec=pltpu.PrefetchScalarGridSpec(
            num_scalar_prefetch=1, grid=(nr, NS), in_specs=[gspec] * n + [lspec] * n,
            out_specs=[lspec] * n + [pl.BlockSpec((tr, qc), lambda r, q, pc: (r, 0))] * n),
        out_shape=[jax.ShapeDtypeStruct((NS, h, qc), BF16)] * n + [jax.ShapeDtypeStruct((h, qc), F32)] * n,
        compiler_params=_cparams(2, 48),
    )(place, *grads, *got)
    return res[:n], res[n:]


def _totals(place, own, got, name):
    n = len(own)
    h, qc = own[0].shape
    tr = _row_tile(h)
    nr = h // tr
    got = [list(g) if isinstance(g, (list, tuple)) else [g] for g in got]
    m = len(got[0])

    def body(pc_ref, *refs):
        own_refs, got_refs, o_refs = refs[:n], refs[n:n + n * m], refs[n + n * m:]
        for a in range(n):
            acc = own_refs[a][...]
            for g in got_refs[a * m:(a + 1) * m]:
                for k in range(g.shape[0]):
                    acc = acc + g[k].astype(F32)
            o_refs[a][...] = acc

    lands = [pl.BlockSpec((g.shape[0], tr, qc), lambda r, pc: (0, r, 0)) for gs in got for g in gs]
    return pl.pallas_call(
        body, name=name,
        grid_spec=pltpu.PrefetchScalarGridSpec(
            num_scalar_prefetch=1, grid=(nr,),
            in_specs=[pl.BlockSpec((tr, qc), lambda r, pc: (r, 0))] * n + lands,
            out_specs=[pl.BlockSpec((tr, qc), lambda r, pc: (pc[1] * nr + r, 0))] * n),
        out_shape=[jax.ShapeDtypeStruct((2 * h, qc), F32)] * n,
        compiler_params=_cparams(1, 48),
    )(place, *own, *[g for gs in got for g in gs])


def _adamw(ws, gs, ms, vs, name):
    n = len(ws)
    R, C = ws[0].shape
    tr = _row_tile(R, (36 << 20) // (7 * 2 * 4 * n * C))

    def body(*refs):
        w_refs, g_refs, m_refs, v_refs = refs[:n], refs[n:2 * n], refs[2 * n:3 * n], refs[3 * n:4 * n]
        d_refs, mo_refs, vo_refs = refs[4 * n:5 * n], refs[5 * n:6 * n], refs[6 * n:]
        for a in range(n):
            d_refs[a][...], mo_refs[a][...], vo_refs[a][...] = _adamw_math(w_refs[a][...], g_refs[a][...], m_refs[a][...],
                                                                         v_refs[a][...])

    blk = pl.BlockSpec((tr, C), lambda r: (r, 0))
    res = pl.pallas_call(
        body, name=name, grid=(R // tr,),
        in_specs=[blk] * (4 * n), out_specs=[blk] * (3 * n),
        out_shape=[jax.ShapeDtypeStruct((R, C), F32)] * (3 * n),
        compiler_params=_cparams(1, 56),
    )(*ws, *gs, *ms, *vs)
    return res[:n], res[n:2 * n], res[2 * n:]


def kernel(x, ffn1_norm, ffn1_w_gate, ffn1_w_up, ffn1_w_down, mix_norm, w_in, a_dw_w, a_dw_b, a_ln_g, a_ln_b, a_w_out, b_conv_w, b_w_out, w_o, ffn2_norm, ffn2_w_gate, ffn2_w_up, ffn2_w_down, final_norm, loss_target, m_ffn1_norm, m_ffn1_w_gate, m_ffn1_w_up, m_ffn1_w_down, m_mix_norm, m_w_in, m_a_dw_w, m_a_dw_b, m_a_ln_g, m_a_ln_b, m_a_w_out, m_b_conv_w, m_b_w_out, m_w_o, m_ffn2_norm, m_ffn2_w_gate, m_ffn2_w_up, m_ffn2_w_down, m_final_norm, v_ffn1_norm, v_ffn1_w_gate, v_ffn1_w_up, v_ffn1_w_down, v_mix_norm, v_w_in, v_a_dw_w, v_a_dw_b, v_a_ln_g, v_a_ln_b, v_a_w_out, v_b_conv_w, v_b_w_out, v_w_o, v_ffn2_norm, v_ffn2_w_gate, v_ffn2_w_up, v_ffn2_w_down, v_final_norm):
    names = ["ffn1_norm", "ffn1_w_gate", "ffn1_w_up", "ffn1_w_down", "mix_norm", "w_in", "a_dw_w", "a_dw_b", "a_ln_g",
             "a_ln_b", "a_w_out", "b_conv_w", "b_w_out", "w_o", "ffn2_norm", "ffn2_w_gate", "ffn2_w_up", "ffn2_w_down",
             "final_norm"]
    W = dict(zip(names, [ffn1_norm, ffn1_w_gate, ffn1_w_up, ffn1_w_down, mix_norm, w_in, a_dw_w, a_dw_b, a_ln_g, a_ln_b,
                         a_w_out, b_conv_w, b_w_out, w_o, ffn2_norm, ffn2_w_gate, ffn2_w_up, ffn2_w_down, final_norm]))
    M = dict(zip(names, [m_ffn1_norm, m_ffn1_w_gate, m_ffn1_w_up, m_ffn1_w_down, m_mix_norm, m_w_in, m_a_dw_w, m_a_dw_b,
                         m_a_ln_g, m_a_ln_b, m_a_w_out, m_b_conv_w, m_b_w_out, m_w_o, m_ffn2_norm, m_ffn2_w_gate,
                         m_ffn2_w_up, m_ffn2_w_down, m_final_norm]))
    V = dict(zip(names, [v_ffn1_norm, v_ffn1_w_gate, v_ffn1_w_up, v_ffn1_w_down, v_mix_norm, v_w_in, v_a_dw_w, v_a_dw_b,
                         v_a_ln_g, v_a_ln_b, v_a_w_out, v_b_conv_w, v_b_w_out, v_w_o, v_ffn2_norm, v_ffn2_w_gate,
                         v_ffn2_w_up, v_ffn2_w_down, v_final_norm]))
    transposed = ("ffn1_w_gate", "ffn1_w_up", "ffn2_w_gate", "ffn2_w_up")
    vecs = ["ffn1_norm", "mix_norm", "a_dw_b", "a_ln_g", "a_ln_b", "ffn2_norm", "final_norm"]
    ffn1 = ["ffn1_w_gate", "ffn1_w_up", "ffn1_w_down"]
    ffn2 = ["ffn2_w_gate", "ffn2_w_up", "ffn2_w_down"]
    outp = ["a_w_out", "b_w_out", "w_o"]

    S, D = x.shape[1], x.shape[2]
    CB = D // NS
    KA, KB = a_dw_w.shape[1], b_conv_w.shape[1]
    px, py, pc = lax.axis_index("x"), lax.axis_index("y"), lax.axis_index("c")
    chip = 2 * px + py
    place = jnp.stack([chip, pc]).astype(jnp.int32)
    h0 = x.reshape(S, D)
    tgt = loss_target.reshape(S, D)
    row = lambda n: pltpu.with_memory_space_constraint(W[n].reshape(1, D), pltpu.HBM)
    pad = lambda a, r: jnp.concatenate([a, jnp.zeros((r - a.shape[0], a.shape[1]), F32)], axis=0)

    def quarter(P, n):
        return jnp.transpose(P[n][0]) if n in transposed else P[n][0]

    def unquarter(a, n):
        return (jnp.transpose(a) if n in transposed else a).reshape(W[n].shape)

    wq = {n: quarter(W, n).astype(BF16) for n in ffn1 + ffn2 + outp + ["w_in"]}

    f1 = _exchange("gather_ffn1", [_Gather([wq[n] for n in ffn1], ["rows"] * 3)])[0]
    g_in = _Gather([wq["w_in"], pad(a_dw_w[0], 32), pad(b_conv_w[0], 16)], ["cols", "rows", "rows"])
    (h1, n1, gp1, up1), ((win, taps_a, taps_b),) = _ffn_fwd(h0, row("ffn1_norm"), *f1, "ffn1_fwd", [g_in])
    wa_taps = taps_a.reshape(NS, 32, CB)[:, :KA]
    wb_taps = taps_b.reshape(NS, 16, CB)[:, :KB]
    g_out = _Gather([wq[n] for n in outp] + [wq["ffn2_w_gate"]], ["rows"] * 4)
    (u, z), ((wa_out, wb_out, wo, f2g),) = _mix_in_fwd(h1, row("mix_norm"), win, [g_out])
    g_f2 = _Gather([wq["ffn2_w_up"], wq["ffn2_w_down"]], ["rows"] * 2)
    (a1, q), ((f2u, f2d),) = _conv_fwd(z, wa_taps, row("a_dw_b"), wb_taps, [g_f2])
    h2, a3, mm, ya, yb = _mix_out_fwd(h1, a1, q, z, row("a_ln_g"), row("a_ln_b"), wa_out, wb_out, wo)
    (dh3, do2, d_final, loss_part, n2, gp2, up2), _ = _ffn_fwd(h2, row("ffn2_norm"), f2g, f2u, f2d, "ffn2_fwd_loss",
                                                               head=(tgt, row("final_norm")))

    (dgp2, dup2, act2), _ = _ffn_bwd_hidden(do2, gp2, up2, f2d, "ffn2_bwd_hidden")
    (dh2, d_ffn2), _ = _ffn_bwd_input(dh3, h2, row("ffn2_norm"), dgp2, dup2, f2g, f2u, "ffn2_bwd_input")
    g2 = [_ffn_wgrad(dgp2, n2, "ffn2_dwg")[0], _ffn_wgrad(dup2, n2, "ffn2_dwu")[0], _ffn_wgrad(act2, do2, "ffn2_dwd")[0]]
    (da1, dq, dga, dgb, dya, dyb, dh2b, d_lng, d_lnb), (got,) = _mix_out_bwd(
        dh2, a1, z, ya, yb, row("a_ln_g"), row("a_ln_b"), wa_out, wb_out, wo, [_ToSibling(g2, ["rows"] * 3)])
    wire2, own2 = _chip_sums(place, g2, got, "rows", "ffn2_chip_sums")
    (dz, d_wa, d_ba, d_wb), (got,) = _conv_bwd(z, da1, dq, dga, dgb, wa_taps, wb_taps, [_ToChips(wire2)])
    half2 = _totals(place, own2, got, "ffn2_totals")
    (g_win,), (tot2,) = _w_in_grad(u, dz, [_SwapHalves(half2)])
    (dh1, d_mix, do1), (got,) = _mix_in_bwd(dh2, h1, row("mix_norm"), dz, win, [_ToSibling([g_win], ["cols"])])
    wire_in, own_in = _chip_sums(place, [g_win], got, "cols", "w_in_chip_sum")
    (dgp1, dup1, act1), (near_in,) = _ffn_bwd_hidden(do1, gp1, up1, f1[2], "ffn1_bwd_hidden", [_ToChips(wire_in, (0, 1))])
    (dx, d_ffn1), _ = _ffn_bwd_input(dh1, h0, row("ffn1_norm"), dgp1, dup1, f1[0], f1[1], "ffn1_bwd_input")
    go, (far_in,) = _mixer_wgrads(a3, dya, q, dyb, mm, dh2b, [_ToChips(wire_in, (2,))])
    half_in = _totals(place, own_in, [[near_in[0], far_in[0]]], "w_in_total")
    g1g, (tot_in, got_o) = _ffn_wgrad(dgp1, n1, "ffn1_dwg", [_SwapHalves(half_in), _ToSibling(go, ["rows"] * 3)])
    wire_o, own_o = _chip_sums(place, go, got_o, "rows", "mixer_chip_sums")
    g1u, (got_g, land_o) = _ffn_wgrad(dup1, n1, "ffn1_dwu", [_ToSibling([g1g], ["rows"]), _ToChips(wire_o)])
    wire_g, own_g = _chip_sums(place, [g1g], got_g, "rows", "ffn1_dwg_chip_sum")
    half_o = _totals(place, own_o, land_o, "mixer_totals")
    g1d, (got_u, land_g, tot_o) = _ffn_wgrad(act1, do1, "ffn1_dwd",
                                             [_ToSibling([g1u], ["rows"]), _ToChips(wire_g), _SwapHalves(half_o)])
    wire_u, own_u = _chip_sums(place, [g1u], got_u, "rows", "ffn1_dwu_chip_sum")
    half_g = _totals(place, own_g, land_g, "ffn1_dwg_total")
    got_d, land_u, tot_g = _exchange("tail_exchange_1", [_ToSibling([g1d], ["rows"]), _ToChips(wire_u), _SwapHalves(half_g)])
    wire_d, own_d = _chip_sums(place, [g1d], got_d, "rows", "ffn1_dwd_chip_sum")
    half_u = _totals(place, own_u, land_u, "ffn1_dwu_total")
    land_d, tot_u = _exchange("tail_exchange_2", [_ToChips(wire_d), _SwapHalves(half_u)])
    half_d = _totals(place, own_d, land_d, "ffn1_dwd_total")
    (tot_d,) = _exchange("tail_exchange_3", [_SwapHalves(half_d)])
    tot1 = [tot_g[0], tot_u[0], tot_d[0]]
    totals = dict(zip(ffn2 + ["w_in"] + ffn1 + outp, list(tot2) + list(tot_in) + tot1 + list(tot_o)))

    vec_grads = {"ffn1_norm": d_ffn1, "mix_norm": d_mix, "a_dw_b": d_ba, "a_ln_g": d_lng, "a_ln_b": d_lnb,
                 "ffn2_norm": d_ffn2, "final_norm": d_final}
    small = _allreduce_small([vec_grads[n] for n in vecs], d_wa, d_wb, loss_part)
    loss = small[LOSS_ROW, 0]
    taps = ["a_dw_w", "b_conv_w"]
    small_out = _small_adamw(place, small, [[P[n].reshape(1, D) for P in (W, M, V)] for n in vecs],
                             [[P[n] for P in (W, M, V)] for n in taps])

    grads, deltas, new_m, new_v = {}, {}, {}, {}
    for n, (g_, d_, m_, v_) in zip(vecs + taps, small_out):
        shp = W[n].shape
        grads[n], deltas[n], new_m[n], new_v[n] = g_.reshape(shp), d_.reshape(shp), m_.reshape(shp), v_.reshape(shp)
    for group, tag in ((ffn1 + ffn2, "ffn"), (["w_in"], "w_in"), (outp, "mixer")):
        ds, ms, vs = _adamw([quarter(W, n) for n in group], [totals[n] for n in group], [quarter(M, n) for n in group],
                            [quarter(V, n) for n in group], tag + "_adamw")
        for n, d_, m_, v_ in zip(group, ds, ms, vs):
            grads[n], deltas[n], new_m[n], new_v[n] = (unquarter(totals[n], n), unquarter(d_, n), unquarter(m_, n),
                                                       unquarter(v_, n))
    return (loss, dx.reshape(x.shape), *[grads[n] for n in names], *[deltas[n] for n in names],
            *[new_m[n] for n in names], *[new_v[n] for n in names])
```

```python
import functools

import jax
import jax.numpy as jnp
from jax import lax
from jax.experimental import pallas as pl
from jax.experimental.pallas import tpu as pltpu

F32 = jnp.float32
BF16 = jnp.bfloat16
EPS = 1e-6
NS = 4
HALO = 32
MESH = pl.DeviceIdType.MESH
IN_HBM = pl.BlockSpec(memory_space=pltpu.HBM)

ADAM_LR = 0.001
ADAM_B1 = 0.9
ADAM_B2 = 0.999
ADAM_EPS = 1e-08
ADAM_WD = 0.01
ADAM_STEP = 10


def _cparams(n_axes, vmem_mb):
    return pltpu.CompilerParams(dimension_semantics=("arbitrary",) * n_axes, vmem_limit_bytes=vmem_mb << 20)


def _tile(n, t):
    return t if n % t == 0 else n


def _row_parts(rows, n=2):
    if rows % (16 * n):
        return [slice(0, rows)]
    return [slice(p * (rows // n), (p + 1) * (rows // n)) for p in range(n)]


def _resident(shape):
    return pl.BlockSpec(shape, lambda *_: (0,) * len(shape), pipeline_mode=pl.Buffered(1))


def _row_tile(n, cap=256):
    for t in (256, 176, 128, 64, 32, 16, 8):
        if t <= cap and n % t == 0:
            return t
    return n


def _dot(a, b):
    return jnp.dot(a, b, preferred_element_type=F32)


def _dot_nt(a, b):
    return lax.dot_general(a, b, (((1,), (1,)), ((), ())), preferred_element_type=F32)


def _dot_tn(a, b):
    return lax.dot_general(a, b, (((0,), (0,)), ((), ())), preferred_element_type=F32)


def _sigmoid(x):
    return jax.nn.sigmoid(x)


def _rms_fwd(x, g):
    r = lax.rsqrt(jnp.mean(x * x, axis=-1, keepdims=True) + EPS)
    return x * r * g


def _rms_bwd(x, g, dn):
    r = lax.rsqrt(jnp.mean(x * x, axis=-1, keepdims=True) + EPS)
    xr = x * r
    dg = jnp.sum(dn * xr, axis=0, keepdims=True)
    w = dn * g
    dx = r * w - xr * (r * r) * jnp.mean(x * w, axis=-1, keepdims=True)
    return dx, dg


def _place():
    x, y, c = lax.axis_index("x"), lax.axis_index("y"), lax.axis_index("c")
    chips = [(1 - x, y), (x, 1 - y), (1 - x, 1 - y)]
    return x, y, c, chips


def _quarter_shape(full_shape, kind):
    r, c = full_shape
    return (r // NS, c) if kind == "rows" else (r, c // NS)


def _half_of_quarter(ref, kind, q, pc):
    qr, qc = _quarter_shape(ref.shape, kind)
    h = qr // 2
    if kind == "rows":
        return ref.at[pl.ds(q * qr + pc * h, h), :]
    return ref.at[pl.ds(pc * h, h), pl.ds(q * qc, qc)]


def _quarter(ref, kind, q):
    qr, qc = _quarter_shape(ref.shape, kind)
    if kind == "rows":
        return ref.at[pl.ds(q * qr, qr), :]
    return ref.at[:, pl.ds(q * qc, qc)]


def _rows_half(ref, pc):
    h = ref.shape[0] // 2
    return ref.at[pl.ds(pc * h, h)]


class _Gather:
    def __init__(self, quarters, kinds):
        self.ins = list(quarters)
        self.kinds = list(kinds)
        n = len(self.ins)
        self.out_shape = [jax.ShapeDtypeStruct((NS * a.shape[0], a.shape[1]) if k == "rows" else (a.shape[0], NS * a.shape[1]),
                                               a.dtype) for a, k in zip(self.ins, self.kinds)]
        self.scratch = [pltpu.SemaphoreType.DMA((n, 6)), pltpu.SemaphoreType.DMA((n, 6)), pltpu.SemaphoreType.DMA((n,))]
        self.aliases = {}

    def _copy(self, outs, sems, a, k, q, pc, to, src=None):
        dst = _half_of_quarter(outs[a], self.kinds[a], q, pc)
        return pltpu.make_async_remote_copy(src_ref=dst if src is None else src, dst_ref=dst,
                                            send_sem=sems[0].at[a, k], recv_sem=sems[1].at[a, k],
                                            device_id=to, device_id_type=MESH)

    def _mine(self, ins, outs, sems, a, p):
        return pltpu.make_async_copy(ins[a], _quarter(outs[a], self.kinds[a], p), sems[2].at[a])

    def start(self, ins, outs, sems):
        x, y, c, chips = _place()
        p = 2 * x + y
        for a in range(len(ins)):
            self._mine(ins, outs, sems, a, p).start()
            for j, chip in enumerate(chips):
                self._copy(outs, sems, a, j, p, c, (*chip, c), src=_rows_half(ins[a], c)).start()

    def relay(self, ins, outs, sems):
        x, y, c, chips = _place()
        sibling = (x, y, 1 - c)
        for a in range(len(ins)):
            for j, (qx, qy) in enumerate(chips):
                q = 2 * qx + qy
                self._copy(outs, sems, a, j, q, c, sibling).wait_recv()
                self._copy(outs, sems, a, 3 + j, q, c, sibling).start()

    def finish(self, ins, outs, sems):
        x, y, c, chips = _place()
        p = 2 * x + y
        sibling = (x, y, 1 - c)
        n = len(ins)
        for a in range(n):
            for j, (qx, qy) in enumerate(chips):
                q = 2 * qx + qy
                self._copy(outs, sems, a, 3 + j, q, 1 - c, sibling).wait_recv()
                self._copy(outs, sems, a, j, p, c, (qx, qy, c), src=_rows_half(ins[a], c)).wait_send()
                self._copy(outs, sems, a, 3 + j, q, c, sibling).wait_send()
            self._mine(ins, outs, sems, a, p).wait()


class _ToSibling:
    def __init__(self, grads, kinds):
        self.ins = list(grads)
        self.kinds = list(kinds)
        n = len(self.ins)
        self.out_shape = []
        for g, k in zip(self.ins, self.kinds):
            qr, qc = _quarter_shape(g.shape, k)
            self.out_shape.append(jax.ShapeDtypeStruct((NS, qr // 2, qc), g.dtype))
        self.scratch = [pltpu.SemaphoreType.DMA((n, NS)), pltpu.SemaphoreType.DMA((n, NS))]
        self.aliases = {}

    def _copies(self, ins, outs, sems):
        x, y, c, _ = _place()
        return [pltpu.make_async_remote_copy(src_ref=_half_of_quarter(ins[a], self.kinds[a], q, 1 - c), dst_ref=outs[a].at[q],
                                             send_sem=sems[0].at[a, q], recv_sem=sems[1].at[a, q],
                                             device_id=(x, y, 1 - c), device_id_type=MESH)
                for a in range(len(ins)) for q in range(NS)]

    def start(self, ins, outs, sems):
        for cp in self._copies(ins, outs, sems):
            cp.start()

    def finish(self, ins, outs, sems):
        for cp in self._copies(ins, outs, sems):
            cp.wait()


class _ToChips:
    def __init__(self, sums, which=(0, 1, 2)):
        self.ins = list(sums)
        self.which = tuple(which)
        n, m = len(self.ins), len(self.which)
        self.out_shape = [jax.ShapeDtypeStruct((m,) + s.shape[1:], s.dtype) for s in self.ins]
        self.scratch = [pltpu.SemaphoreType.DMA((n, m)), pltpu.SemaphoreType.DMA((n, m))]
        self.aliases = {}

    def _copies(self, ins, outs, sems):
        x, y, c, chips = _place()
        return [pltpu.make_async_remote_copy(src_ref=ins[a].at[2 * chips[j][0] + chips[j][1]], dst_ref=outs[a].at[k],
                                             send_sem=sems[0].at[a, k], recv_sem=sems[1].at[a, k],
                                             device_id=(*chips[j], c), device_id_type=MESH)
                for a in range(len(ins)) for k, j in enumerate(self.which)]

    def start(self, ins, outs, sems):
        for cp in self._copies(ins, outs, sems):
            cp.start()

    def finish(self, ins, outs, sems):
        for cp in self._copies(ins, outs, sems):
            cp.wait()


class _SwapHalves:
    def __init__(self, quarters):
        self.ins = list(quarters)
        n = len(self.ins)
        self.out_shape = [jax.ShapeDtypeStruct(g.shape, g.dtype) for g in self.ins]
        self.scratch = [pltpu.SemaphoreType.DMA((n,)), pltpu.SemaphoreType.DMA((n,))]
        self.aliases = {a: a for a in range(n)}

    def _copy(self, outs, sems, a, pc):
        x, y, c, _ = _place()
        rows = _rows_half(outs[a], pc)
        return pltpu.make_async_remote_copy(src_ref=rows, dst_ref=rows, send_sem=sems[0].at[a], recv_sem=sems[1].at[a],
                                            device_id=(x, y, 1 - c), device_id_type=MESH)

    def start(self, ins, outs, sems):
        c = lax.axis_index("c")
        for a in range(len(outs)):
            self._copy(outs, sems, a, c).start()

    def finish(self, ins, outs, sems):
        c = lax.axis_index("c")
        for a in range(len(outs)):
            self._copy(outs, sems, a, c).wait_send()
            self._copy(outs, sems, a, 1 - c).wait_recv()


def _call(name, grid, compute, in_specs, out_specs, out_shape, scratch, vmem_mb, args, jobs=()):
    n_in, n_out, n_scr = len(in_specs), len(out_specs), len(scratch)
    ji = [len(j.ins) for j in jobs]
    jo = [len(j.out_shape) for j in jobs]
    js = [len(j.scratch) for j in jobs]

    def body(*refs):
        pos = [0]

        def take(k):
            r = refs[pos[0]:pos[0] + k]
            pos[0] += k
            return r

        ins, jins = take(n_in), [take(k) for k in ji]
        outs, jouts = take(n_out), [take(k) for k in jo]
        scr, jscr = take(n_scr), [take(k) for k in js]
        if jobs and grid:
            ids = [pl.program_id(a) for a in range(len(grid))]
            first = functools.reduce(jnp.logical_and, [i == 0 for i in ids])
            last = functools.reduce(jnp.logical_and, [i == g - 1 for i, g in zip(ids, grid)])

            @pl.when(first)
            def _():
                for j, a, b, c in zip(jobs, jins, jouts, jscr):
                    j.start(a, b, c)

            @pl.when(last)
            def _():
                for j, a, b, c in zip(jobs, jins, jouts, jscr):
                    if hasattr(j, "relay"):
                        j.relay(a, b, c)
        elif jobs:
            for j, a, b, c in zip(jobs, jins, jouts, jscr):
                j.start(a, b, c)
            for j, a, b, c in zip(jobs, jins, jouts, jscr):
                if hasattr(j, "relay"):
                    j.relay(a, b, c)
        compute(ins, outs, scr)
        if jobs and grid:
            @pl.when(last)
            def _():
                for j, a, b, c in zip(jobs, jins, jouts, jscr):
                    j.finish(a, b, c)
        elif jobs:
            for j, a, b, c in zip(jobs, jins, jouts, jscr):
                j.finish(a, b, c)

    aliases = {}
    in_off, out_off = n_in, n_out
    for j, a, b in zip(jobs, ji, jo):
        for s, d in j.aliases.items():
            aliases[in_off + s] = out_off + d
        in_off += a
        out_off += b
    res = pl.pallas_call(
        body, name=name, grid=grid,
        in_specs=list(in_specs) + [IN_HBM] * sum(ji), out_specs=list(out_specs) + [IN_HBM] * sum(jo),
        out_shape=list(out_shape) + [pltpu.HBM(s.shape, s.dtype) for j in jobs for s in j.out_shape],
        scratch_shapes=list(scratch) + [s for j in jobs for s in j.scratch],
        input_output_aliases=aliases, compiler_params=_cparams(len(grid), vmem_mb),
    )(*args, *[a for j in jobs for a in j.ins])
    res = list(res)
    main, rest, jres = res[:n_out], res[n_out:], []
    for k in jo:
        jres.append(rest[:k])
        rest = rest[k:]
    return main, jres


def _exchange(name, jobs):
    return _call(name, (), lambda ins, outs, scr: None, [], [], [], [], 16, [], jobs)[1]


def _small_rows(ka, kb):
    first_a = 8
    first_b = first_a + -(-ka // 8) * 8
    return first_a, first_b, first_b + -(-kb // 8) * 8


LOSS_ROW = 7


def _allreduce_small(vecs, taps_a, taps_b, loss_part):
    n = len(vecs)
    C = vecs[0].shape[1]
    NQ, KA, CB = taps_a.shape
    KB = taps_b.shape[1]
    first_a, first_b, R = _small_rows(KA, KB)
    assert n <= LOSS_ROW < first_a
    N = 8

    def body(*refs):
        vec_refs = refs[:n]
        ta_ref, tb_ref, loss_ref, out_ref, v_ref, gath, send_sems, recv_sems, local_sem = refs[n:]
        v_ref[...] = jnp.zeros_like(v_ref)
        v_ref[LOSS_ROW:LOSS_ROW + 1, 0:loss_ref.shape[1]] = loss_ref[0:1, :]
        for i, r in enumerate(vec_refs):
            v_ref[i:i + 1, :] = r[...]
        for q in range(NQ):
            v_ref[first_a:first_a + KA, q * CB:(q + 1) * CB] = ta_ref[q]
            v_ref[first_b:first_b + KB, q * CB:(q + 1) * CB] = tb_ref[q]
        x, y, c, chips = _place()
        me, sibling = (x, y, c), (x, y, 1 - c)

        def rows(px, py, pc):
            return gath.at[pl.ds((4 * px + 2 * py + pc) * R, R), :]

        def copy(k, block, to, src=None):
            return pltpu.make_async_remote_copy(src_ref=rows(*block) if src is None else src, dst_ref=rows(*block),
                                                send_sem=send_sems.at[k], recv_sem=recv_sems.at[k],
                                                device_id=to, device_id_type=MESH)

        mine = pltpu.make_async_copy(v_ref, rows(*me), local_sem)
        mine.start()
        first = [copy(0, me, sibling, src=v_ref)]
        first += [copy(1 + j, me, (*chip, c), src=v_ref) for j, chip in enumerate(chips)]
        for cp in first:
            cp.start()
        passed = [copy(4 + j, (*chip, c), sibling) for j, chip in enumerate(chips)]
        for j, chip in enumerate(chips):
            copy(1 + j, (*chip, c), me).wait_recv()
            passed[j].start()
        copy(0, sibling, me).wait_recv()
        for j, chip in enumerate(chips):
            copy(4 + j, (*chip, 1 - c), me).wait_recv()
        for cp in first + passed:
            cp.wait_send()
        mine.wait()
        acc = gath[0:R, :]
        for d in range(1, N):
            acc = acc + gath[d * R:(d + 1) * R, :]
        out_ref[...] = acc

    vmem = pl.BlockSpec(memory_space=pltpu.VMEM)
    return pl.pallas_call(
        body, name="allreduce_small",
        in_specs=[vmem] * (n + 3), out_specs=vmem,
        out_shape=jax.ShapeDtypeStruct((R, C), F32),
        scratch_shapes=[pltpu.VMEM((R, C), F32), pltpu.VMEM((N * R, C), F32), pltpu.SemaphoreType.DMA((7,)),
                        pltpu.SemaphoreType.DMA((7,)), pltpu.SemaphoreType.DMA],
    )(*vecs, taps_a, taps_b, loss_part)


def _adamw_math(w, g, m, v):
    c1 = 1.0 - ADAM_B1 ** ADAM_STEP
    c2 = 1.0 - ADAM_B2 ** ADAM_STEP
    mn = ADAM_B1 * m + (1.0 - ADAM_B1) * g
    vn = ADAM_B2 * v + (1.0 - ADAM_B2) * (g * g)
    return -ADAM_LR * ((mn / c1) / (jnp.sqrt(vn / c2) + ADAM_EPS) + ADAM_WD * w), mn, vn


def _small_adamw(place, small, vec_wmv, tap_wmv):
    n = len(vec_wmv)
    D = small.shape[1]
    CB = tap_wmv[0][0].shape[2]
    ks = [t[0].shape[1] for t in tap_wmv]
    firsts = _small_rows(*ks)[:2]

    def body(place_ref, small_ref, *refs):
        ins, outs = refs[:3 * (n + 2)], refs[3 * (n + 2):]
        chip = place_ref[0]
        for i in range(n):
            g = small_ref[i:i + 1, :]
            d, mn, vn = _adamw_math(ins[3 * i][...], g, ins[3 * i + 1][...], ins[3 * i + 2][...])
            for o, val in zip(outs[4 * i:4 * i + 4], (g, d, mn, vn)):
                o[...] = val
        for t, (row0, k) in enumerate(zip(firsts, ks)):
            g = jnp.zeros((k, CB), F32)
            for q in range(D // CB):
                g = g + jnp.where(chip == q, small_ref[row0:row0 + k, q * CB:(q + 1) * CB], 0.0)
            w_ref, m_ref, v_ref = ins[3 * (n + t):3 * (n + t) + 3]
            d, mn, vn = _adamw_math(w_ref[0], g, m_ref[0], v_ref[0])
            for o, val in zip(outs[4 * (n + t):4 * (n + t) + 4], (g, d, mn, vn)):
                o[0] = val

    flat = [a for wmv in list(vec_wmv) + list(tap_wmv) for a in wmv]
    shapes = [jax.ShapeDtypeStruct(wmv[0].shape, F32) for wmv in list(vec_wmv) + list(tap_wmv) for _ in range(4)]
    vmem = pl.BlockSpec(memory_space=pltpu.VMEM)
    res = pl.pallas_call(
        body, name="small_adamw",
        in_specs=[pl.BlockSpec(memory_space=pltpu.SMEM)] + [vmem] * (1 + len(flat)), out_specs=[vmem] * len(shapes),
        out_shape=shapes,
    )(place, small, *flat)
    return [res[4 * i:4 * i + 4] for i in range(n + 2)]


def _ffn_fwd(h, g, wg, wu, wd, name, jobs=(), head=None):
    S, D = h.shape
    F = wg.shape[0]
    ts = _tile(S, 512)
    fb = _tile(F, F // 2)
    nf = F // fb

    def compute(ins, outs, scr):
        h_ref, g_ref, wg_ref, wu_ref, wd_ref = ins[:5]
        n_ref, gp_ref, up_ref = outs[-3:]
        x = h_ref[...]
        n = _rms_fwd(x, g_ref[...]).astype(BF16)
        n_ref[...] = n
        acc = None
        for j in range(nf):
            cols = slice(j * fb, (j + 1) * fb)
            gp = _dot_nt(n, wg_ref[cols, :])
            up = _dot_nt(n, wu_ref[cols, :])
            gp_ref[:, cols] = gp.astype(BF16)
            up_ref[:, cols] = up.astype(BF16)
            part = _dot((gp * _sigmoid(gp) * up).astype(BF16), wd_ref[cols, :])
            acc = part if acc is None else acc + part
        ho = x + 0.5 * acc
        if head is None:
            outs[0][...] = ho
            return
        t_ref, gf_ref = ins[5:]
        dh_ref, do_ref, dgf_ref, loss_ref = outs[:4]

        @pl.when(pl.program_id(0) == 0)
        def _():
            dgf_ref[...] = jnp.zeros_like(dgf_ref)
            loss_ref[...] = jnp.zeros_like(loss_ref)

        err = _rms_fwd(ho, gf_ref[...]) - t_ref[...]
        loss_ref[...] += (0.5 / D) * jnp.sum(err * err)
        dx, dg = _rms_bwd(ho, gf_ref[...], err * (1.0 / D))
        dh_ref[...] = dx
        do_ref[...] = (0.5 * dx).astype(BF16)
        dgf_ref[...] += dg

    tok = pl.BlockSpec((ts, D), lambda i: (i, 0))
    row = pl.BlockSpec((1, D), lambda i: (0, 0))
    wsp = _resident((F, D))
    hid = pl.BlockSpec((ts, F), lambda i: (i, 0))
    saved = [jax.ShapeDtypeStruct((S, D), BF16), jax.ShapeDtypeStruct((S, F), BF16), jax.ShapeDtypeStruct((S, F), BF16)]
    if head is None:
        return _call(name, (S // ts,), compute, [tok, row, wsp, wsp, wsp], [tok, tok, hid, hid],
                     [jax.ShapeDtypeStruct((S, D), F32)] + saved, [], 56, [h, g, wg, wu, wd], jobs)
    return _call(name, (S // ts,), compute, [tok, row, wsp, wsp, wsp, tok, row],
                 [tok, tok, row, pl.BlockSpec((8, 128), lambda i: (0, 0)), tok, hid, hid],
                 [jax.ShapeDtypeStruct((S, D), F32), jax.ShapeDtypeStruct((S, D), BF16), jax.ShapeDtypeStruct((1, D), F32),
                  jax.ShapeDtypeStruct((8, 128), F32)] + saved, [], 60, [h, g, wg, wu, wd, *head], jobs)


def _ffn_bwd_hidden(do, gp, up, wd, name, jobs=()):
    S, D = do.shape
    F = wd.shape[0]
    ts = _tile(S, 1024)
    fb = _tile(F, F // 2)
    def compute(ins, outs, scr):
        do_ref, gp_ref, up_ref, wd_ref = ins
        dgp_ref, dup_ref, a_ref = outs
        parts = _row_parts(ts, 4)
        das = [_dot_nt(do_ref[rows, :], wd_ref[...]) for rows in parts]
        for rows, da in zip(parts, das):
            gf = gp_ref[rows, :].astype(F32)
            uf = up_ref[rows, :].astype(F32)
            sg = _sigmoid(gf)
            si = gf * sg
            dgp_ref[rows, :] = (da * uf * (sg * (1.0 + gf * (1.0 - sg)))).astype(BF16)
            dup_ref[rows, :] = (da * si).astype(BF16)
            a_ref[rows, :] = (si * uf).astype(BF16)

    tok = pl.BlockSpec((ts, D), lambda s, i: (i, 0))
    hid = pl.BlockSpec((ts, fb), lambda s, i: (i, s))
    return _call(name, (F // fb, S // ts), compute, [tok, hid, hid, pl.BlockSpec((fb, D), lambda s, i: (s, 0))],
                 [hid, hid, hid], [jax.ShapeDtypeStruct((S, F), BF16)] * 3, [], 56, [do, gp, up, wd], jobs)


def _ffn_bwd_input(dh, h, g, dgp, dup, wg, wu, name, jobs=()):
    S, D = h.shape
    F = wg.shape[0]
    ts = _tile(S, 512)

    def compute(ins, outs, scr):
        dh_ref, h_ref, g_ref, dgp_ref, dup_ref, wg_ref, wu_ref = ins
        dhi_ref, dg_ref = outs

        @pl.when(pl.program_id(0) == 0)
        def _():
            dg_ref[...] = jnp.zeros_like(dg_ref)

        dn = _dot(dgp_ref[...], wg_ref[...]) + _dot(dup_ref[...], wu_ref[...])
        dx, dg = _rms_bwd(h_ref[...], g_ref[...], dn)
        dhi_ref[...] = dh_ref[...] + dx
        dg_ref[...] += dg

    tok = pl.BlockSpec((ts, D), lambda i: (i, 0))
    hid = pl.BlockSpec((ts, F), lambda i: (i, 0))
    row = pl.BlockSpec((1, D), lambda i: (0, 0))
    return _call(name, (S // ts,), compute, [tok, tok, row, hid, hid, _resident((F, D)), _resident((F, D))], [tok, row],
                 [jax.ShapeDtypeStruct((S, D), F32), jax.ShapeDtypeStruct((1, D), F32)], [], 56,
                 [dh, h, g, dgp, dup, wg, wu], jobs)


def _ffn_wgrad(hid, tok, name, jobs=()):
    S, D = tok.shape
    F = hid.shape[1]
    fb = _tile(F, F // 2)

    def compute(ins, outs, scr):
        outs[0][...] = _dot_tn(ins[0][...], ins[1][...])

    main, jres = _call(name, (F // fb,), compute,
                       [pl.BlockSpec((S, fb), lambda j: (0, j)), _resident(tok.shape)],
                       [pl.BlockSpec((fb, D), lambda j: (j, 0))], [jax.ShapeDtypeStruct((F, D), F32)], [], 56,
                       [hid, tok], jobs)
    return main[0], jres


def _w_in_pieces(D, cq, ng):
    groups = []
    for k in range(ng):
        lo, hi, pieces = k * D, (k + 1) * D, []
        while lo < hi:
            q = lo // cq
            w = min(hi, (q + 1) * cq) - lo
            pieces.append((q, lo - q * cq, w, lo - k * D))
            lo += w
        groups.append(pieces)
    return groups


def _mix_in_fwd(h, g, win, jobs=()):
    S, D = h.shape
    NG = win.shape[0] * win.shape[2] // D
    pieces = _w_in_pieces(D, win.shape[2], NG)
    ts = _tile(S, 512)

    def compute(ins, outs, scr):
        h_ref, g_ref, w_ref = ins
        u_ref, z_ref = outs
        u = _rms_fwd(h_ref[...], g_ref[...]).astype(BF16)
        u_ref[...] = u
        for k in range(NG):
            for q, c0, w, d0 in pieces[k]:
                z_ref[k, :, d0:d0 + w] = _dot(u, w_ref[q, :, c0:c0 + w]).astype(BF16)

    return _call("mix_in_fwd", (S // ts,), compute,
                 [pl.BlockSpec((ts, D), lambda i: (i, 0)), pl.BlockSpec((1, D), lambda i: (0, 0)), _resident(win.shape)],
                 [pl.BlockSpec((ts, D), lambda i: (i, 0)), pl.BlockSpec((NG, ts, D), lambda i: (0, i, 0))],
                 [jax.ShapeDtypeStruct((S, D), BF16), jax.ShapeDtypeStruct((NG, S, D), BF16)],
                 [], 48, [h, g, win], jobs)


SUBLANES = 8


def _shifted_copies(s):
    n = s.shape[1] - SUBLANES
    for r in range(1, SUBLANES):
        s[r, 0:n, :] = s[0, r:r + n, :]


def _window(s, o, rows):
    r = o % SUBLANES
    return s[r, o - r:o - r + rows, :]


def _conv_fwd(z, wa, ba, wb, jobs=()):
    _, S, D = z.shape
    _, KA, CB = wa.shape
    KB = wb.shape[1]
    ts = _tile(S, 1024)
    r = ts // HALO
    CH = min(64, ts)

    def compute(ins, outs, scr):
        z_ref, zh_ref, wa_ref, ba_ref, wb_ref = ins
        a1_ref, q_ref = outs
        sa, sb = scr
        keep = (pl.program_id(1) > 0).astype(F32)
        sa[0, HALO:HALO + ts, :] = z_ref[0].astype(F32) * _sigmoid(z_ref[1].astype(F32))
        sa[0, 0:HALO, :] = zh_ref[0].astype(F32) * _sigmoid(zh_ref[1].astype(F32)) * keep
        _shifted_copies(sa)
        sb[HALO:HALO + ts, :] = z_ref[3].astype(F32) * z_ref[4].astype(F32)
        sb[0:HALO, :] = zh_ref[3].astype(F32) * zh_ref[4].astype(F32) * keep
        wak = [wa_ref[0, k:k + 1, :] for k in range(KA)]
        wbk = [wb_ref[0, k:k + 1, :] for k in range(KB)]
        for c0 in range(0, ts, CH):
            acc = jnp.broadcast_to(ba_ref[...], (CH, CB))
            for k in range(KA):
                acc = acc + wak[k] * _window(sa, c0 + HALO - (KA - 1) + k, CH)
            a1_ref[c0:c0 + CH, :] = acc
            v = jnp.zeros((CH, CB), F32)
            for k in range(KB):
                o = c0 + HALO - (KB - 1) + k
                v = v + wbk[k] * sb[o:o + CH, :]
            q_ref[c0:c0 + CH, :] = (z_ref[2, c0:c0 + CH, :].astype(F32) * v).astype(BF16)

    return _call("conv_fwd", (D // CB, S // ts), compute,
                 [pl.BlockSpec((5, ts, CB), lambda j, i: (0, i, j)),
                  pl.BlockSpec((5, HALO, CB), lambda j, i: (0, jnp.maximum(i * r - 1, 0), j)),
                  pl.BlockSpec((1, KA, CB), lambda j, i: (j, 0, 0)), pl.BlockSpec((1, CB), lambda j, i: (0, j)),
                  pl.BlockSpec((1, KB, CB), lambda j, i: (j, 0, 0))],
                 [pl.BlockSpec((ts, CB), lambda j, i: (i, j)), pl.BlockSpec((ts, CB), lambda j, i: (i, j))],
                 [jax.ShapeDtypeStruct((S, D), F32), jax.ShapeDtypeStruct((S, D), BF16)],
                 [pltpu.VMEM((SUBLANES, HALO + ts, CB), F32), pltpu.VMEM((HALO + ts, CB), F32)], 40, [z, z, wa, ba, wb], jobs)


def _ln_stats(a1):
    mu = jnp.mean(a1, axis=-1, keepdims=True)
    xc = a1 - mu
    rstd = lax.rsqrt(jnp.mean(xc * xc, axis=-1, keepdims=True) + EPS)
    return xc * rstd, rstd


def _mix_out_fwd(h1, a1, q, z, lng, lnb, wa, wb, wo, jobs=()):
    S, D = h1.shape
    ts = _tile(S, 512)

    def compute(ins, outs, scr):
        h_ref, a1_ref, q_ref, ga_ref, gb_ref, lng_ref, lnb_ref, wa_ref, wb_ref, wo_ref = ins
        h2_ref, a3_ref, m_ref, ya_ref, yb_ref = outs
        xhat, _ = _ln_stats(a1_ref[...])
        a2 = xhat * lng_ref[...] + lnb_ref[...]
        a3 = (a2 * _sigmoid(a2)).astype(BF16)
        a3_ref[...] = a3
        ya = _dot(a3, wa_ref[...])
        yb = _dot(q_ref[...], wb_ref[...])
        ya_ref[...] = ya.astype(BF16)
        yb_ref[...] = yb.astype(BF16)
        m = (_sigmoid(ga_ref[0].astype(F32)) * ya + _sigmoid(gb_ref[0].astype(F32)) * yb).astype(BF16)
        m_ref[...] = m
        h2_ref[...] = h_ref[...] + _dot(m, wo_ref[...])

    tok = pl.BlockSpec((ts, D), lambda i: (i, 0))
    row = pl.BlockSpec((1, D), lambda i: (0, 0))
    mat = _resident((D, D))
    return _call("mix_out_fwd", (S // ts,), compute,
                 [tok, tok, tok, pl.BlockSpec((1, ts, D), lambda i: (5, i, 0)), pl.BlockSpec((1, ts, D), lambda i: (6, i, 0)),
                  row, row, mat, mat, mat], [tok] * 5,
                 [jax.ShapeDtypeStruct((S, D), F32)] + [jax.ShapeDtypeStruct((S, D), BF16)] * 4,
                 [], 56, [h1, a1, q, z, z, lng, lnb, wa, wb, wo], jobs)


def _mix_out_bwd(dh2, a1, z, ya, yb, lng, lnb, wa, wb, wo, jobs=()):
    S, D = dh2.shape
    ts = _tile(S, 512)

    def compute(ins, outs, scr):
        dh_ref, a1_ref, ga_ref, gb_ref, ya_ref, yb_ref, lng_ref, lnb_ref, wa_ref, wb_ref, wo_ref = ins
        da1_ref, dq_ref, dga_ref, dgb_ref, dya_ref, dyb_ref, dhb_ref, dlg_ref, dlb_ref = outs

        @pl.when(pl.program_id(0) == 0)
        def _():
            dlg_ref[...] = jnp.zeros_like(dlg_ref)
            dlb_ref[...] = jnp.zeros_like(dlb_ref)

        for rows in _row_parts(ts):
            dhb = dh_ref[rows, :].astype(BF16)
            dhb_ref[rows, :] = dhb
            dm = _dot_nt(dhb, wo_ref[...])
            sa = _sigmoid(ga_ref[0, rows, :].astype(F32))
            sb = _sigmoid(gb_ref[0, rows, :].astype(F32))
            dga_ref[rows, :] = (dm * ya_ref[rows, :].astype(F32) * sa * (1.0 - sa)).astype(BF16)
            dgb_ref[rows, :] = (dm * yb_ref[rows, :].astype(F32) * sb * (1.0 - sb)).astype(BF16)
            dya = (sa * dm).astype(BF16)
            dyb = (sb * dm).astype(BF16)
            dya_ref[rows, :] = dya
            dyb_ref[rows, :] = dyb
            dq_ref[rows, :] = _dot_nt(dyb, wb_ref[...]).astype(BF16)
            da3 = _dot_nt(dya, wa_ref[...])
            xhat, rstd = _ln_stats(a1_ref[rows, :])
            a2 = xhat * lng_ref[...] + lnb_ref[...]
            sg = _sigmoid(a2)
            da2 = da3 * (sg * (1.0 + a2 * (1.0 - sg)))
            dlg_ref[...] += jnp.sum(da2 * xhat, axis=0, keepdims=True)
            dlb_ref[...] += jnp.sum(da2, axis=0, keepdims=True)
            dxh = da2 * lng_ref[...]
            da1_ref[rows, :] = (rstd * (dxh - jnp.mean(dxh, axis=-1, keepdims=True)
                                        - xhat * jnp.mean(dxh * xhat, axis=-1, keepdims=True))).astype(BF16)

    tok = pl.BlockSpec((ts, D), lambda i: (i, 0))
    row = pl.BlockSpec((1, D), lambda i: (0, 0))
    mat = _resident((D, D))
    return _call("mix_out_bwd", (S // ts,), compute,
                 [tok, tok, pl.BlockSpec((1, ts, D), lambda i: (5, i, 0)), pl.BlockSpec((1, ts, D), lambda i: (6, i, 0)),
                  tok, tok, row, row, mat, mat, mat], [tok] * 7 + [row, row],
                 [jax.ShapeDtypeStruct((S, D), BF16)] * 7 + [jax.ShapeDtypeStruct((1, D), F32)] * 2,
                 [], 56, [dh2, a1, z, z, ya, yb, lng, lnb, wa, wb, wo], jobs)


def _mixer_wgrads(a3, dya, q, dyb, mm, dhb, jobs=()):
    S, D = a3.shape
    tk = _tile(S, 512)

    def compute(ins, outs, scr):
        @pl.when(pl.program_id(0) == 0)
        def _():
            for o in outs:
                o[...] = jnp.zeros_like(o)

        for t in range(3):
            outs[t][...] += _dot_tn(ins[2 * t][...], ins[2 * t + 1][...])

    tok = pl.BlockSpec((tk, D), lambda k: (k, 0))
    return _call("mixer_wgrads", (S // tk,), compute, [tok] * 6, [pl.BlockSpec((D, D), lambda k: (0, 0))] * 3,
                 [jax.ShapeDtypeStruct((D, D), F32)] * 3, [], 56, [a3, dya, q, dyb, mm, dhb], jobs)


def _conv_bwd(z, da1, dq, dga, dgb, wa, wb, jobs=()):
    NG, S, D = z.shape
    _, KA, CB = wa.shape
    KB = wb.shape[1]
    ts = _tile(S, 1024)
    r = ts // HALO
    nt = S // ts
    CH = min(64, ts)
    last_halo = S // HALO - 1

    def compute(ins, outs, scr):
        z_ref, zp_ref, zn_ref, da1_ref, da1n_ref, dq_ref, dqn_ref, dga_ref, dgb_ref, wa_ref, wb_ref = ins
        dz_ref, dwa_ref, dba_ref, dwb_ref = outs
        sa0, sd, sp, sv, acca, accb = scr
        i = pl.program_id(1)
        prev = (i > 0).astype(F32)
        nxt = (i < nt - 1).astype(F32)

        @pl.when(i == 0)
        def _():
            acca[...] = jnp.zeros_like(acca)
            accb[...] = jnp.zeros_like(accb)
            dba_ref[...] = jnp.zeros_like(dba_ref)

        sa0[0, HALO:HALO + ts, :] = z_ref[0].astype(F32) * _sigmoid(z_ref[1].astype(F32))
        sa0[0, 0:HALO, :] = zp_ref[0].astype(F32) * _sigmoid(zp_ref[1].astype(F32)) * prev
        _shifted_copies(sa0)
        sp[HALO:HALO + ts, :] = z_ref[3].astype(F32) * z_ref[4].astype(F32)
        sp[0:HALO, :] = zp_ref[3].astype(F32) * zp_ref[4].astype(F32) * prev
        sd[0, 0:ts, :] = da1_ref[...].astype(F32)
        sd[0, ts:ts + HALO, :] = da1n_ref[...].astype(F32) * nxt
        _shifted_copies(sd)
        sv[0:ts, :] = dq_ref[...].astype(F32) * z_ref[2].astype(F32)
        sv[ts:ts + HALO, :] = dqn_ref[...].astype(F32) * zn_ref[2].astype(F32) * nxt
        dba_ref[...] += jnp.sum(sd[0, 0:ts, :], axis=0, keepdims=True)
        wak = [wa_ref[0, k:k + 1, :] for k in range(KA)]
        wbk = [wb_ref[0, k:k + 1, :] for k in range(KB)]
        for c0 in range(0, ts, CH):
            rows = slice(c0, c0 + CH)
            d1 = sd[0, rows, :]
            da0 = jnp.zeros((CH, CB), F32)
            for k in range(KA):
                da0 = da0 + wak[k] * _window(sd, c0 + (KA - 1) - k, CH)
                a0w = _window(sa0, c0 + HALO - (KA - 1) + k, CH)
                acca[k] += jnp.sum((d1 * a0w).reshape(CH // 8, 8, CB), axis=0)
            val = z_ref[0, rows, :].astype(F32)
            sg = _sigmoid(z_ref[1, rows, :].astype(F32))
            dz_ref[0, rows, :] = (da0 * sg).astype(BF16)
            dz_ref[1, rows, :] = (da0 * val * sg * (1.0 - sg)).astype(BF16)
            dv = sv[rows, :]
            v = jnp.zeros((CH, CB), F32)
            dp = jnp.zeros((CH, CB), F32)
            for k in range(KB):
                o = c0 + HALO - (KB - 1) + k
                pw = sp[o:o + CH, :]
                v = v + wbk[k] * pw
                accb[k] += jnp.sum((dv * pw).reshape(CH // 8, 8, CB), axis=0)
                o = c0 + (KB - 1) - k
                dp = dp + wbk[k] * sv[o:o + CH, :]
            dz_ref[2, rows, :] = (dq_ref[rows, :].astype(F32) * v).astype(BF16)
            dz_ref[3, rows, :] = (dp * z_ref[4, rows, :].astype(F32)).astype(BF16)
            dz_ref[4, rows, :] = (dp * z_ref[3, rows, :].astype(F32)).astype(BF16)
        dz_ref[5] = dga_ref[...]
        dz_ref[6] = dgb_ref[...]

        @pl.when(i == nt - 1)
        def _():
            dwa_ref[0] = jnp.sum(acca[...], axis=1)
            dwb_ref[0] = jnp.sum(accb[...], axis=1)

    zt = pl.BlockSpec((5, ts, CB), lambda j, i: (0, i, j))
    zp = pl.BlockSpec((5, HALO, CB), lambda j, i: (0, jnp.maximum(i * r - 1, 0), j))
    zn = pl.BlockSpec((5, HALO, CB), lambda j, i: (0, jnp.minimum((i + 1) * r, last_halo), j))
    tok = pl.BlockSpec((ts, CB), lambda j, i: (i, j))
    tokn = pl.BlockSpec((HALO, CB), lambda j, i: (jnp.minimum((i + 1) * r, last_halo), j))
    return _call("conv_bwd", (D // CB, nt), compute,
                 [zt, zp, zn, tok, tokn, tok, tokn, tok, tok,
                  pl.BlockSpec((1, KA, CB), lambda j, i: (j, 0, 0)), pl.BlockSpec((1, KB, CB), lambda j, i: (j, 0, 0))],
                 [pl.BlockSpec((NG, ts, CB), lambda j, i: (0, i, j)), pl.BlockSpec((1, KA, CB), lambda j, i: (j, 0, 0)),
                  pl.BlockSpec((1, CB), lambda j, i: (0, j)), pl.BlockSpec((1, KB, CB), lambda j, i: (j, 0, 0))],
                 [jax.ShapeDtypeStruct((NG, S, D), BF16), jax.ShapeDtypeStruct((D // CB, KA, CB), F32),
                  jax.ShapeDtypeStruct((1, D), F32), jax.ShapeDtypeStruct((D // CB, KB, CB), F32)],
                 [pltpu.VMEM((SUBLANES, HALO + ts, CB), F32), pltpu.VMEM((SUBLANES, ts + HALO, CB), F32),
                  pltpu.VMEM((HALO + ts, CB), F32), pltpu.VMEM((ts + HALO, CB), F32),
                  pltpu.VMEM((KA, 8, CB), F32), pltpu.VMEM((KB, 8, CB), F32)],
                 48, [z, z, z, da1, da1, dq, dq, dga, dgb, wa, wb], jobs)


def _mix_in_bwd(dh2, h1, g, dz, win, jobs=()):
    S, D = h1.shape
    NG = dz.shape[0]
    pieces = _w_in_pieces(D, win.shape[2], NG)
    ts = _tile(S, 512)

    def compute(ins, outs, scr):
        dh_ref, h_ref, g_ref, dz_ref, w_ref = ins
        dhi_ref, dg_ref, do_ref = outs

        @pl.when(pl.program_id(0) == 0)
        def _():
            dg_ref[...] = jnp.zeros_like(dg_ref)

        du = None
        for k in range(NG):
            for q, c0, w, d0 in pieces[k]:
                part = _dot_nt(dz_ref[k, :, d0:d0 + w], w_ref[q, :, c0:c0 + w])
                du = part if du is None else du + part
        dx, dg = _rms_bwd(h_ref[...], g_ref[...], du)
        dhi = dh_ref[...] + dx
        dhi_ref[...] = dhi
        do_ref[...] = (0.5 * dhi).astype(BF16)
        dg_ref[...] += dg

    tok = pl.BlockSpec((ts, D), lambda i: (i, 0))
    row = pl.BlockSpec((1, D), lambda i: (0, 0))
    return _call("mix_in_bwd", (S // ts,), compute,
                 [tok, tok, row, pl.BlockSpec((NG, ts, D), lambda i: (0, i, 0)), _resident(win.shape)],
                 [tok, row, tok],
                 [jax.ShapeDtypeStruct((S, D), F32), jax.ShapeDtypeStruct((1, D), F32), jax.ShapeDtypeStruct((S, D), BF16)],
                 [], 56, [dh2, h1, g, dz, win], jobs)


def _w_in_grad(u, dz, jobs=()):
    S, D = u.shape
    NG = dz.shape[0]

    def compute(ins, outs, scr):
        outs[0][...] = _dot_tn(ins[0][...], ins[1][0])

    return _call("w_in_grad", (NG,), compute,
                 [_resident(u.shape), pl.BlockSpec((1, S, D), lambda j: (j, 0, 0))],
                 [pl.BlockSpec((D, D), lambda j: (0, j))], [jax.ShapeDtypeStruct((D, NG * D), F32)], [], 48, [u, dz], jobs)


def _chip_sums(place, grads, got, kind, name):
    n = len(grads)
    qr, qc = _quarter_shape(grads[0].shape, kind)
    h = qr // 2
    tr = _row_tile(h)
    nr = h // tr

    def body(pc_ref, *refs):
        g_refs, got_refs, b_refs, f_refs = refs[:n], refs[n:2 * n], refs[2 * n:3 * n], refs[3 * n:]
        own = pl.program_id(1) == pc_ref[0]
        for a in range(n):
            s = g_refs[a][...] + got_refs[a][0]
            b_refs[a][0] = s.astype(BF16)

            @pl.when(own)
            def _():
                f_refs[a][...] = s

    if kind == "rows":
        gspec = pl.BlockSpec((tr, qc), lambda r, q, pc: (q * (2 * nr) + pc[1] * nr + r, 0))
    else:
        gspec = pl.BlockSpec((tr, qc), lambda r, q, pc: (pc[1] * nr + r, q))
    lspec = pl.BlockSpec((1, tr, qc), lambda r, q, pc: (q, r, 0))
    res = pl.pallas_call(
        body, name=name,
        grid_spec=pltpu.PrefetchScalarGridSpec(
            num_scalar_prefetch=1, grid=(nr, NS), in_specs=[gspec] * n + [lspec] * n,
            out_specs=[lspec] * n + [pl.BlockSpec((tr, qc), lambda r, q, pc: (r, 0))] * n),
        out_shape=[jax.ShapeDtypeStruct((NS, h, qc), BF16)] * n + [jax.ShapeDtypeStruct((h, qc), F32)] * n,
        compiler_params=_cparams(2, 48),
    )(place, *grads, *got)
    return res[:n], res[n:]


def _totals(place, own, got, name):
    n = len(own)
    h, qc = own[0].shape
    tr = _row_tile(h)
    nr = h // tr
    got = [list(g) if isinstance(g, (list, tuple)) else [g] for g in got]
    m = len(got[0])

    def body(pc_ref, *refs):
        own_refs, got_refs, o_refs = refs[:n], refs[n:n + n * m], refs[n + n * m:]
        for a in range(n):
            acc = own_refs[a][...]
            for g in got_refs[a * m:(a + 1) * m]:
                for k in range(g.shape[0]):
                    acc = acc + g[k].astype(F32)
            o_refs[a][...] = acc

    lands = [pl.BlockSpec((g.shape[0], tr, qc), lambda r, pc: (0, r, 0)) for gs in got for g in gs]
    return pl.pallas_call(
        body, name=name,
        grid_spec=pltpu.PrefetchScalarGridSpec(
            num_scalar_prefetch=1, grid=(nr,),
            in_specs=[pl.BlockSpec((tr, qc), lambda r, pc: (r, 0))] * n + lands,
            out_specs=[pl.BlockSpec((tr, qc), lambda r, pc: (pc[1] * nr + r, 0))] * n),
        out_shape=[jax.ShapeDtypeStruct((2 * h, qc), F32)] * n,
        compiler_params=_cparams(1, 48),
    )(place, *own, *[g for gs in got for g in gs])


def _adamw(ws, gs, ms, vs, name):
    n = len(ws)
    R, C = ws[0].shape
    tr = _row_tile(R, (36 << 20) // (7 * 2 * 4 * n * C))

    def body(*refs):
        w_refs, g_refs, m_refs, v_refs = refs[:n], refs[n:2 * n], refs[2 * n:3 * n], refs[3 * n:4 * n]
        d_refs, mo_refs, vo_refs = refs[4 * n:5 * n], refs[5 * n:6 * n], refs[6 * n:]
        for a in range(n):
            d_refs[a][...], mo_refs[a][...], vo_refs[a][...] = _adamw_math(w_refs[a][...], g_refs[a][...], m_refs[a][...],
                                                                         v_refs[a][...])

    blk = pl.BlockSpec((tr, C), lambda r: (r, 0))
    res = pl.pallas_call(
        body, name=name, grid=(R // tr,),
        in_specs=[blk] * (4 * n), out_specs=[blk] * (3 * n),
        out_shape=[jax.ShapeDtypeStruct((R, C), F32)] * (3 * n),
        compiler_params=_cparams(1, 56),
    )(*ws, *gs, *ms, *vs)
    return res[:n], res[n:2 * n], res[2 * n:]


def kernel(x, ffn1_norm, ffn1_w_gate, ffn1_w_up, ffn1_w_down, mix_norm, w_in, a_dw_w, a_dw_b, a_ln_g, a_ln_b, a_w_out, b_conv_w, b_w_out, w_o, ffn2_norm, ffn2_w_gate, ffn2_w_up, ffn2_w_down, final_norm, loss_target, m_ffn1_norm, m_ffn1_w_gate, m_ffn1_w_up, m_ffn1_w_down, m_mix_norm, m_w_in, m_a_dw_w, m_a_dw_b, m_a_ln_g, m_a_ln_b, m_a_w_out, m_b_conv_w, m_b_w_out, m_w_o, m_ffn2_norm, m_ffn2_w_gate, m_ffn2_w_up, m_ffn2_w_down, m_final_norm, v_ffn1_norm, v_ffn1_w_gate, v_ffn1_w_up, v_ffn1_w_down, v_mix_norm, v_w_in, v_a_dw_w, v_a_dw_b, v_a_ln_g, v_a_ln_b, v_a_w_out, v_b_conv_w, v_b_w_out, v_w_o, v_ffn2_norm, v_ffn2_w_gate, v_ffn2_w_up, v_ffn2_w_down, v_final_norm):
    names = ["ffn1_norm", "ffn1_w_gate", "ffn1_w_up", "ffn1_w_down", "mix_norm", "w_in", "a_dw_w", "a_dw_b", "a_ln_g",
             "a_ln_b", "a_w_out", "b_conv_w", "b_w_out", "w_o", "ffn2_norm", "ffn2_w_gate", "ffn2_w_up", "ffn2_w_down",
             "final_norm"]
    W = dict(zip(names, [ffn1_norm, ffn1_w_gate, ffn1_w_up, ffn1_w_down, mix_norm, w_in, a_dw_w, a_dw_b, a_ln_g, a_ln_b,
                         a_w_out, b_conv_w, b_w_out, w_o, ffn2_norm, ffn2_w_gate, ffn2_w_up, ffn2_w_down, final_norm]))
    M = dict(zip(names, [m_ffn1_norm, m_ffn1_w_gate, m_ffn1_w_up, m_ffn1_w_down, m_mix_norm, m_w_in, m_a_dw_w, m_a_dw_b,
                         m_a_ln_g, m_a_ln_b, m_a_w_out, m_b_conv_w, m_b_w_out, m_w_o, m_ffn2_norm, m_ffn2_w_gate,
                         m_ffn2_w_up, m_ffn2_w_down, m_final_norm]))
    V = dict(zip(names, [v_ffn1_norm, v_ffn1_w_gate, v_ffn1_w_up, v_ffn1_w_down, v_mix_norm, v_w_in, v_a_dw_w, v_a_dw_b,
                         v_a_ln_g, v_a_ln_b, v_a_w_out, v_b_conv_w, v_b_w_out, v_w_o, v_ffn2_norm, v_ffn2_w_gate,
                         v_ffn2_w_up, v_ffn2_w_down, v_final_norm]))
    transposed = ("ffn1_w_gate", "ffn1_w_up", "ffn2_w_gate", "ffn2_w_up")
    vecs = ["ffn1_norm", "mix_norm", "a_dw_b", "a_ln_g", "a_ln_b", "ffn2_norm", "final_norm"]
    ffn1 = ["ffn1_w_gate", "ffn1_w_up", "ffn1_w_down"]
    ffn2 = ["ffn2_w_gate", "ffn2_w_up", "ffn2_w_down"]
    outp = ["a_w_out", "b_w_out", "w_o"]

    S, D = x.shape[1], x.shape[2]
    CB = D // NS
    KA, KB = a_dw_w.shape[1], b_conv_w.shape[1]
    px, py, pc = lax.axis_index("x"), lax.axis_index("y"), lax.axis_index("c")
    chip = 2 * px + py
    place = jnp.stack([chip, pc]).astype(jnp.int32)
    h0 = x.reshape(S, D)
    tgt = loss_target.reshape(S, D)
    row = lambda n: pltpu.with_memory_space_constraint(W[n].reshape(1, D), pltpu.HBM)
    pad = lambda a, r: jnp.concatenate([a, jnp.zeros((r - a.shape[0], a.shape[1]), F32)], axis=0)

    def quarter(P, n):
        return jnp.transpose(P[n][0]) if n in transposed else P[n][0]

    def unquarter(a, n):
        return (jnp.transpose(a) if n in transposed else a).reshape(W[n].shape)

    wq = {n: quarter(W, n).astype(BF16) for n in ffn1 + ffn2 + outp + ["w_in"]}

    f1 = _exchange("gather_ffn1", [_Gather([wq[n] for n in ffn1], ["rows"] * 3)])[0]
    g_in = _Gather([wq["w_in"], pad(a_dw_w[0], 32), pad(b_conv_w[0], 16)], ["rows"] * 3)
    (h1, n1, gp1, up1), ((win, taps_a, taps_b),) = _ffn_fwd(h0, row("ffn1_norm"), *f1, "ffn1_fwd", [g_in])
    win = win.reshape(NS, D, -1)
    wa_taps = taps_a.reshape(NS, 32, CB)[:, :KA]
    wb_taps = taps_b.reshape(NS, 16, CB)[:, :KB]
    g_out = _Gather([wq[n] for n in outp], ["rows"] * 3)
    (u, z), ((wa_out, wb_out, wo),) = _mix_in_fwd(h1, row("mix_norm"), win, [g_out])
    g_f2 = _Gather([wq["ffn2_w_gate"], wq["ffn2_w_up"]], ["rows"] * 2)
    (a1, q), ((f2g, f2u),) = _conv_fwd(z, wa_taps, row("a_dw_b"), wb_taps, [g_f2])
    (h2, a3, mm, ya, yb), ((f2d,),) = _mix_out_fwd(h1, a1, q, z, row("a_ln_g"), row("a_ln_b"), wa_out, wb_out, wo,
                                                   [_Gather([wq["ffn2_w_down"]], ["rows"])])
    (dh3, do2, d_final, loss_part, n2, gp2, up2), _ = _ffn_fwd(h2, row("ffn2_norm"), f2g, f2u, f2d, "ffn2_fwd_loss",
                                                               head=(tgt, row("final_norm")))

    (dgp2, dup2, act2), _ = _ffn_bwd_hidden(do2, gp2, up2, f2d, "ffn2_bwd_hidden")
    (dh2, d_ffn2), _ = _ffn_bwd_input(dh3, h2, row("ffn2_norm"), dgp2, dup2, f2g, f2u, "ffn2_bwd_input")
    g2 = [_ffn_wgrad(dgp2, n2, "ffn2_dwg")[0], _ffn_wgrad(dup2, n2, "ffn2_dwu")[0], _ffn_wgrad(act2, do2, "ffn2_dwd")[0]]
    (da1, dq, dga, dgb, dya, dyb, dh2b, d_lng, d_lnb), (got,) = _mix_out_bwd(
        dh2, a1, z, ya, yb, row("a_ln_g"), row("a_ln_b"), wa_out, wb_out, wo, [_ToSibling(g2, ["rows"] * 3)])
    wire2, own2 = _chip_sums(place, g2, got, "rows", "ffn2_chip_sums")
    (dz, d_wa, d_ba, d_wb), (got,) = _conv_bwd(z, da1, dq, dga, dgb, wa_taps, wb_taps, [_ToChips(wire2)])
    half2 = _totals(place, own2, got, "ffn2_totals")
    (g_win,), (tot2,) = _w_in_grad(u, dz, [_SwapHalves(half2)])
    (dh1, d_mix, do1), (got,) = _mix_in_bwd(dh2, h1, row("mix_norm"), dz, win, [_ToSibling([g_win], ["cols"])])
    wire_in, own_in = _chip_sums(place, [g_win], got, "cols", "w_in_chip_sum")
    (dgp1, dup1, act1), (near_in,) = _ffn_bwd_hidden(do1, gp1, up1, f1[2], "ffn1_bwd_hidden", [_ToChips(wire_in, (0, 1))])
    (dx, d_ffn1), _ = _ffn_bwd_input(dh1, h0, row("ffn1_norm"), dgp1, dup1, f1[0], f1[1], "ffn1_bwd_input")
    go, (far_in,) = _mixer_wgrads(a3, dya, q, dyb, mm, dh2b, [_ToChips(wire_in, (2,))])
    half_in = _totals(place, own_in, [[near_in[0], far_in[0]]], "w_in_total")
    g1g, (tot_in, got_o) = _ffn_wgrad(dgp1, n1, "ffn1_dwg", [_SwapHalves(half_in), _ToSibling(go, ["rows"] * 3)])
    wire_o, own_o = _chip_sums(place, go, got_o, "rows", "mixer_chip_sums")
    g1u, (got_g, land_o) = _ffn_wgrad(dup1, n1, "ffn1_dwu", [_ToSibling([g1g], ["rows"]), _ToChips(wire_o)])
    wire_g, own_g = _chip_sums(place, [g1g], got_g, "rows", "ffn1_dwg_chip_sum")
    half_o = _totals(place, own_o, land_o, "mixer_totals")
    g1d, (got_u, land_g, tot_o) = _ffn_wgrad(act1, do1, "ffn1_dwd",
                                             [_ToSibling([g1u], ["rows"]), _ToChips(wire_g), _SwapHalves(half_o)])
    wire_u, own_u = _chip_sums(place, [g1u], got_u, "rows", "ffn1_dwu_chip_sum")
    half_g = _totals(place, own_g, land_g, "ffn1_dwg_total")
    got_d, land_u, tot_g = _exchange("tail_exchange_1", [_ToSibling([g1d], ["rows"]), _ToChips(wire_u), _SwapHalves(half_g)])
    wire_d, own_d = _chip_sums(place, [g1d], got_d, "rows", "ffn1_dwd_chip_sum")
    half_u = _totals(place, own_u, land_u, "ffn1_dwu_total")
    land_d, tot_u = _exchange("tail_exchange_2", [_ToChips(wire_d), _SwapHalves(half_u)])
    half_d = _totals(place, own_d, land_d, "ffn1_dwd_total")
    (tot_d,) = _exchange("tail_exchange_3", [_SwapHalves(half_d)])
    tot1 = [tot_g[0], tot_u[0], tot_d[0]]
    totals = dict(zip(ffn2 + ["w_in"] + ffn1 + outp, list(tot2) + list(tot_in) + tot1 + list(tot_o)))

    vec_grads = {"ffn1_norm": d_ffn1, "mix_norm": d_mix, "a_dw_b": d_ba, "a_ln_g": d_lng, "a_ln_b": d_lnb,
                 "ffn2_norm": d_ffn2, "final_norm": d_final}
    small = _allreduce_small([vec_grads[n] for n in vecs], d_wa, d_wb, loss_part)
    loss = small[LOSS_ROW, 0]
    taps = ["a_dw_w", "b_conv_w"]
    small_out = _small_adamw(place, small, [[P[n].reshape(1, D) for P in (W, M, V)] for n in vecs],
                             [[P[n] for P in (W, M, V)] for n in taps])

    grads, deltas, new_m, new_v = {}, {}, {}, {}
    for n, (g_, d_, m_, v_) in zip(vecs + taps, small_out):
        shp = W[n].shape
        grads[n], deltas[n], new_m[n], new_v[n] = g_.reshape(shp), d_.reshape(shp), m_.reshape(shp), v_.reshape(shp)
    for group, tag in ((ffn1 + ffn2, "ffn"), (["w_in"], "w_in"), (outp, "mixer")):
        ds, ms, vs = _adamw([quarter(W, n) for n in group], [totals[n] for n in group], [quarter(M, n) for n in group],
                            [quarter(V, n) for n in group], tag + "_adamw")
        for n, d_, m_, v_ in zip(group, ds, ms, vs):
            grads[n], deltas[n], new_m[n], new_v[n] = (unquarter(totals[n], n), unquarter(d_, n), unquarter(m_, n),
                                                       unquarter(v_, n))
    return (loss, dx.reshape(x.shape), *[grads[n] for n in names], *[deltas[n] for n in names],
            *[new_m[n] for n in names], *[new_v[n] for n in names])
```

```python
import functools

import jax
import jax.numpy as jnp
from jax import lax
from jax.experimental import pallas as pl
from jax.experimental.pallas import tpu as pltpu

F32 = jnp.float32
BF16 = jnp.bfloat16
EPS = 1e-6
NS = 4
HALO = 32
MESH = pl.DeviceIdType.MESH
IN_HBM = pl.BlockSpec(memory_space=pltpu.HBM)

ADAM_LR = 0.001
ADAM_B1 = 0.9
ADAM_B2 = 0.999
ADAM_EPS = 1e-08
ADAM_WD = 0.01
ADAM_STEP = 10


def _cparams(n_axes, vmem_mb):
    return pltpu.CompilerParams(dimension_semantics=("arbitrary",) * n_axes, vmem_limit_bytes=vmem_mb << 20)


def _tile(n, t):
    return t if n % t == 0 else n


def _row_parts(rows, n=2):
    if rows % (16 * n):
        return [slice(0, rows)]
    return [slice(p * (rows // n), (p + 1) * (rows // n)) for p in range(n)]


def _resident(shape):
    return pl.BlockSpec(shape, lambda *_: (0,) * len(shape), pipeline_mode=pl.Buffered(1))


def _row_tile(n, cap=256):
    for t in (256, 176, 128, 64, 32, 16, 8):
        if t <= cap and n % t == 0:
            return t
    return n


def _dot(a, b):
    return jnp.dot(a, b, preferred_element_type=F32)


def _dot_nt(a, b):
    return lax.dot_general(a, b, (((1,), (1,)), ((), ())), preferred_element_type=F32)


def _dot_tn(a, b):
    return lax.dot_general(a, b, (((0,), (0,)), ((), ())), preferred_element_type=F32)


def _sigmoid(x):
    return jax.nn.sigmoid(x)


def _rms_fwd(x, g):
    r = lax.rsqrt(jnp.mean(x * x, axis=-1, keepdims=True) + EPS)
    return x * r * g


def _rms_bwd(x, g, dn):
    r = lax.rsqrt(jnp.mean(x * x, axis=-1, keepdims=True) + EPS)
    xr = x * r
    dg = jnp.sum(dn * xr, axis=0, keepdims=True)
    w = dn * g
    dx = r * w - xr * (r * r) * jnp.mean(x * w, axis=-1, keepdims=True)
    return dx, dg


def _place():
    x, y, c = lax.axis_index("x"), lax.axis_index("y"), lax.axis_index("c")
    chips = [(1 - x, y), (x, 1 - y), (1 - x, 1 - y)]
    return x, y, c, chips


def _quarter_shape(full_shape, kind):
    r, c = full_shape
    return (r // NS, c) if kind == "rows" else (r, c // NS)


def _half_of_quarter(ref, kind, q, pc):
    qr, qc = _quarter_shape(ref.shape, kind)
    h = qr // 2
    if kind == "rows":
        return ref.at[pl.ds(q * qr + pc * h, h), :]
    return ref.at[pl.ds(pc * h, h), pl.ds(q * qc, qc)]


def _quarter(ref, kind, q):
    qr, qc = _quarter_shape(ref.shape, kind)
    if kind == "rows":
        return ref.at[pl.ds(q * qr, qr), :]
    return ref.at[:, pl.ds(q * qc, qc)]


def _rows_half(ref, pc):
    h = ref.shape[0] // 2
    return ref.at[pl.ds(pc * h, h)]


class _Gather:
    def __init__(self, quarters, kinds):
        self.ins = list(quarters)
        self.kinds = list(kinds)
        n = len(self.ins)
        self.out_shape = [jax.ShapeDtypeStruct((NS * a.shape[0], a.shape[1]) if k == "rows" else (a.shape[0], NS * a.shape[1]),
                                               a.dtype) for a, k in zip(self.ins, self.kinds)]
        self.scratch = [pltpu.SemaphoreType.DMA((n, 6)), pltpu.SemaphoreType.DMA((n, 6)), pltpu.SemaphoreType.DMA((n,))]
        self.aliases = {}

    def _copy(self, outs, sems, a, k, q, pc, to, src=None):
        dst = _half_of_quarter(outs[a], self.kinds[a], q, pc)
        return pltpu.make_async_remote_copy(src_ref=dst if src is None else src, dst_ref=dst,
                                            send_sem=sems[0].at[a, k], recv_sem=sems[1].at[a, k],
                                            device_id=to, device_id_type=MESH)

    def _mine(self, ins, outs, sems, a, p):
        return pltpu.make_async_copy(ins[a], _quarter(outs[a], self.kinds[a], p), sems[2].at[a])

    def start(self, ins, outs, sems):
        x, y, c, chips = _place()
        p = 2 * x + y
        for a in range(len(ins)):
            self._mine(ins, outs, sems, a, p).start()
            for j, chip in enumerate(chips):
                self._copy(outs, sems, a, j, p, c, (*chip, c), src=_rows_half(ins[a], c)).start()

    def relay(self, ins, outs, sems):
        x, y, c, chips = _place()
        sibling = (x, y, 1 - c)
        for a in range(len(ins)):
            for j, (qx, qy) in enumerate(chips):
                q = 2 * qx + qy
                self._copy(outs, sems, a, j, q, c, sibling).wait_recv()
                self._copy(outs, sems, a, 3 + j, q, c, sibling).start()

    def finish(self, ins, outs, sems):
        x, y, c, chips = _place()
        p = 2 * x + y
        sibling = (x, y, 1 - c)
        n = len(ins)
        for a in range(n):
            for j, (qx, qy) in enumerate(chips):
                q = 2 * qx + qy
                self._copy(outs, sems, a, 3 + j, q, 1 - c, sibling).wait_recv()
                self._copy(outs, sems, a, j, p, c, (qx, qy, c), src=_rows_half(ins[a], c)).wait_send()
                self._copy(outs, sems, a, 3 + j, q, c, sibling).wait_send()
            self._mine(ins, outs, sems, a, p).wait()


class _ToSibling:
    def __init__(self, grads, kinds):
        self.ins = list(grads)
        self.kinds = list(kinds)
        n = len(self.ins)
        self.out_shape = []
        for g, k in zip(self.ins, self.kinds):
            qr, qc = _quarter_shape(g.shape, k)
            self.out_shape.append(jax.ShapeDtypeStruct((NS, qr // 2, qc), g.dtype))
        self.scratch = [pltpu.SemaphoreType.DMA((n, NS)), pltpu.SemaphoreType.DMA((n, NS))]
        self.aliases = {}

    def _copies(self, ins, outs, sems):
        x, y, c, _ = _place()
        return [pltpu.make_async_remote_copy(src_ref=_half_of_quarter(ins[a], self.kinds[a], q, 1 - c), dst_ref=outs[a].at[q],
                                             send_sem=sems[0].at[a, q], recv_sem=sems[1].at[a, q],
                                             device_id=(x, y, 1 - c), device_id_type=MESH)
                for a in range(len(ins)) for q in range(NS)]

    def start(self, ins, outs, sems):
        for cp in self._copies(ins, outs, sems):
            cp.start()

    def finish(self, ins, outs, sems):
        for cp in self._copies(ins, outs, sems):
            cp.wait()


class _ToChips:
    def __init__(self, sums, which=(0, 1, 2)):
        self.ins = list(sums)
        self.which = tuple(which)
        n, m = len(self.ins), len(self.which)
        self.out_shape = [jax.ShapeDtypeStruct((m,) + s.shape[1:], s.dtype) for s in self.ins]
        self.scratch = [pltpu.SemaphoreType.DMA((n, m)), pltpu.SemaphoreType.DMA((n, m))]
        self.aliases = {}

    def _copies(self, ins, outs, sems):
        x, y, c, chips = _place()
        return [pltpu.make_async_remote_copy(src_ref=ins[a].at[2 * chips[j][0] + chips[j][1]], dst_ref=outs[a].at[k],
                                             send_sem=sems[0].at[a, k], recv_sem=sems[1].at[a, k],
                                             device_id=(*chips[j], c), device_id_type=MESH)
                for a in range(len(ins)) for k, j in enumerate(self.which)]

    def start(self, ins, outs, sems):
        for cp in self._copies(ins, outs, sems):
            cp.start()

    def finish(self, ins, outs, sems):
        for cp in self._copies(ins, outs, sems):
            cp.wait()


class _SwapHalves:
    def __init__(self, quarters):
        self.ins = list(quarters)
        n = len(self.ins)
        self.out_shape = [jax.ShapeDtypeStruct(g.shape, g.dtype) for g in self.ins]
        self.scratch = [pltpu.SemaphoreType.DMA((n,)), pltpu.SemaphoreType.DMA((n,))]
        self.aliases = {a: a for a in range(n)}

    def _copy(self, outs, sems, a, pc):
        x, y, c, _ = _place()
        rows = _rows_half(outs[a], pc)
        return pltpu.make_async_remote_copy(src_ref=rows, dst_ref=rows, send_sem=sems[0].at[a], recv_sem=sems[1].at[a],
                                            device_id=(x, y, 1 - c), device_id_type=MESH)

    def start(self, ins, outs, sems):
        c = lax.axis_index("c")
        for a in range(len(outs)):
            self._copy(outs, sems, a, c).start()

    def finish(self, ins, outs, sems):
        c = lax.axis_index("c")
        for a in range(len(outs)):
            self._copy(outs, sems, a, c).wait_send()
            self._copy(outs, sems, a, 1 - c).wait_recv()


def _call(name, grid, compute, in_specs, out_specs, out_shape, scratch, vmem_mb, args, jobs=()):
    n_in, n_out, n_scr = len(in_specs), len(out_specs), len(scratch)
    ji = [len(j.ins) for j in jobs]
    jo = [len(j.out_shape) for j in jobs]
    js = [len(j.scratch) for j in jobs]

    def body(*refs):
        pos = [0]

        def take(k):
            r = refs[pos[0]:pos[0] + k]
            pos[0] += k
            return r

        ins, jins = take(n_in), [take(k) for k in ji]
        outs, jouts = take(n_out), [take(k) for k in jo]
        scr, jscr = take(n_scr), [take(k) for k in js]
        if jobs and grid:
            ids = [pl.program_id(a) for a in range(len(grid))]
            first = functools.reduce(jnp.logical_and, [i == 0 for i in ids])
            last = functools.reduce(jnp.logical_and, [i == g - 1 for i, g in zip(ids, grid)])

            @pl.when(first)
            def _():
                for j, a, b, c in zip(jobs, jins, jouts, jscr):
                    j.start(a, b, c)

            @pl.when(last)
            def _():
                for j, a, b, c in zip(jobs, jins, jouts, jscr):
                    if hasattr(j, "relay"):
                        j.relay(a, b, c)
        elif jobs:
            for j, a, b, c in zip(jobs, jins, jouts, jscr):
                j.start(a, b, c)
            for j, a, b, c in zip(jobs, jins, jouts, jscr):
                if hasattr(j, "relay"):
                    j.relay(a, b, c)
        compute(ins, outs, scr)
        if jobs and grid:
            @pl.when(last)
            def _():
                for j, a, b, c in zip(jobs, jins, jouts, jscr):
                    j.finish(a, b, c)
        elif jobs:
            for j, a, b, c in zip(jobs, jins, jouts, jscr):
                j.finish(a, b, c)

    aliases = {}
    in_off, out_off = n_in, n_out
    for j, a, b in zip(jobs, ji, jo):
        for s, d in j.aliases.items():
            aliases[in_off + s] = out_off + d
        in_off += a
        out_off += b
    res = pl.pallas_call(
        body, name=name, grid=grid,
        in_specs=list(in_specs) + [IN_HBM] * sum(ji), out_specs=list(out_specs) + [IN_HBM] * sum(jo),
        out_shape=list(out_shape) + [pltpu.HBM(s.shape, s.dtype) for j in jobs for s in j.out_shape],
        scratch_shapes=list(scratch) + [s for j in jobs for s in j.scratch],
        input_output_aliases=aliases, compiler_params=_cparams(len(grid), vmem_mb),
    )(*args, *[a for j in jobs for a in j.ins])
    res = list(res)
    main, rest, jres = res[:n_out], res[n_out:], []
    for k in jo:
        jres.append(rest[:k])
        rest = rest[k:]
    return main, jres


def _exchange(name, jobs):
    return _call(name, (), lambda ins, outs, scr: None, [], [], [], [], 16, [], jobs)[1]


def _small_rows(ka, kb):
    first_a = 8
    first_b = first_a + -(-ka // 8) * 8
    return first_a, first_b, first_b + -(-kb // 8) * 8


LOSS_ROW = 7


def _allreduce_small(vecs, taps_a, taps_b, loss_part):
    n = len(vecs)
    C = vecs[0].shape[1]
    NQ, KA, CB = taps_a.shape
    KB = taps_b.shape[1]
    first_a, first_b, R = _small_rows(KA, KB)
    assert n <= LOSS_ROW < first_a
    N = 8

    def body(*refs):
        vec_refs = refs[:n]
        ta_ref, tb_ref, loss_ref, out_ref, v_ref, gath, send_sems, recv_sems, local_sem = refs[n:]
        v_ref[...] = jnp.zeros_like(v_ref)
        v_ref[LOSS_ROW:LOSS_ROW + 1, 0:loss_ref.shape[1]] = loss_ref[0:1, :]
        for i, r in enumerate(vec_refs):
            v_ref[i:i + 1, :] = r[...]
        for q in range(NQ):
            v_ref[first_a:first_a + KA, q * CB:(q + 1) * CB] = ta_ref[q]
            v_ref[first_b:first_b + KB, q * CB:(q + 1) * CB] = tb_ref[q]
        x, y, c, chips = _place()
        me, sibling = (x, y, c), (x, y, 1 - c)

        def rows(px, py, pc):
            return gath.at[pl.ds((4 * px + 2 * py + pc) * R, R), :]

        def copy(k, block, to, src=None):
            return pltpu.make_async_remote_copy(src_ref=rows(*block) if src is None else src, dst_ref=rows(*block),
                                                send_sem=send_sems.at[k], recv_sem=recv_sems.at[k],
                                                device_id=to, device_id_type=MESH)

        mine = pltpu.make_async_copy(v_ref, rows(*me), local_sem)
        mine.start()
        first = [copy(0, me, sibling, src=v_ref)]
        first += [copy(1 + j, me, (*chip, c), src=v_ref) for j, chip in enumerate(chips)]
        for cp in first:
            cp.start()
        passed = [copy(4 + j, (*chip, c), sibling) for j, chip in enumerate(chips)]
        for j, chip in enumerate(chips):
            copy(1 + j, (*chip, c), me).wait_recv()
            passed[j].start()
        copy(0, sibling, me).wait_recv()
        for j, chip in enumerate(chips):
            copy(4 + j, (*chip, 1 - c), me).wait_recv()
        for cp in first + passed:
            cp.wait_send()
        mine.wait()
        acc = gath[0:R, :]
        for d in range(1, N):
            acc = acc + gath[d * R:(d + 1) * R, :]
        out_ref[...] = acc

    vmem = pl.BlockSpec(memory_space=pltpu.VMEM)
    return pl.pallas_call(
        body, name="allreduce_small",
        in_specs=[vmem] * (n + 3), out_specs=vmem,
        out_shape=jax.ShapeDtypeStruct((R, C), F32),
        scratch_shapes=[pltpu.VMEM((R, C), F32), pltpu.VMEM((N * R, C), F32), pltpu.SemaphoreType.DMA((7,)),
                        pltpu.SemaphoreType.DMA((7,)), pltpu.SemaphoreType.DMA],
    )(*vecs, taps_a, taps_b, loss_part)


def _adamw_math(w, g, m, v):
    c1 = 1.0 - ADAM_B1 ** ADAM_STEP
    c2 = 1.0 - ADAM_B2 ** ADAM_STEP
    mn = ADAM_B1 * m + (1.0 - ADAM_B1) * g
    vn = ADAM_B2 * v + (1.0 - ADAM_B2) * (g * g)
    return -ADAM_LR * ((mn / c1) / (jnp.sqrt(vn / c2) + ADAM_EPS) + ADAM_WD * w), mn, vn


def _small_adamw(place, small, vec_wmv, tap_wmv):
    n = len(vec_wmv)
    D = small.shape[1]
    CB = tap_wmv[0][0].shape[2]
    ks = [t[0].shape[1] for t in tap_wmv]
    firsts = _small_rows(*ks)[:2]

    def body(place_ref, small_ref, *refs):
        ins, outs = refs[:3 * (n + 2)], refs[3 * (n + 2):]
        chip = place_ref[0]
        for i in range(n):
            g = small_ref[i:i + 1, :]
            d, mn, vn = _adamw_math(ins[3 * i][...], g, ins[3 * i + 1][...], ins[3 * i + 2][...])
            for o, val in zip(outs[4 * i:4 * i + 4], (g, d, mn, vn)):
                o[...] = val
        for t, (row0, k) in enumerate(zip(firsts, ks)):
            g = jnp.zeros((k, CB), F32)
            for q in range(D // CB):
                g = g + jnp.where(chip == q, small_ref[row0:row0 + k, q * CB:(q + 1) * CB], 0.0)
            w_ref, m_ref, v_ref = ins[3 * (n + t):3 * (n + t) + 3]
            d, mn, vn = _adamw_math(w_ref[0], g, m_ref[0], v_ref[0])
            for o, val in zip(outs[4 * (n + t):4 * (n + t) + 4], (g, d, mn, vn)):
                o[0] = val

    flat = [a for wmv in list(vec_wmv) + list(tap_wmv) for a in wmv]
    shapes = [jax.ShapeDtypeStruct(wmv[0].shape, F32) for wmv in list(vec_wmv) + list(tap_wmv) for _ in range(4)]
    vmem = pl.BlockSpec(memory_space=pltpu.VMEM)
    res = pl.pallas_call(
        body, name="small_adamw",
        in_specs=[pl.BlockSpec(memory_space=pltpu.SMEM)] + [vmem] * (1 + len(flat)), out_specs=[vmem] * len(shapes),
        out_shape=shapes,
    )(place, small, *flat)
    return [res[4 * i:4 * i + 4] for i in range(n + 2)]


def _ffn_fwd(h, g, wg, wu, wd, name, jobs=(), head=None):
    S, D = h.shape
    F = wg.shape[0]
    ts = _tile(S, 512)
    fb = _tile(F, F // 2)
    nf = F // fb

    def compute(ins, outs, scr):
        h_ref, g_ref, wg_ref, wu_ref, wd_ref = ins[:5]
        n_ref, gp_ref, up_ref = outs[-3:]
        x = h_ref[...]
        n = _rms_fwd(x, g_ref[...]).astype(BF16)
        n_ref[...] = n
        acc = None
        for j in range(nf):
            cols = slice(j * fb, (j + 1) * fb)
            gp = _dot_nt(n, wg_ref[cols, :])
            up = _dot_nt(n, wu_ref[cols, :])
            gp_ref[:, cols] = gp.astype(BF16)
            up_ref[:, cols] = up.astype(BF16)
            part = _dot((gp * _sigmoid(gp) * up).astype(BF16), wd_ref[cols, :])
            acc = part if acc is None else acc + part
        ho = x + 0.5 * acc
        if head is None:
            outs[0][...] = ho
            return
        t_ref, gf_ref = ins[5:]
        dh_ref, do_ref, dgf_ref, loss_ref = outs[:4]

        @pl.when(pl.program_id(0) == 0)
        def _():
            dgf_ref[...] = jnp.zeros_like(dgf_ref)
            loss_ref[...] = jnp.zeros_like(loss_ref)

        err = _rms_fwd(ho, gf_ref[...]) - t_ref[...]
        loss_ref[...] += (0.5 / D) * jnp.sum(err * err)
        dx, dg = _rms_bwd(ho, gf_ref[...], err * (1.0 / D))
        dh_ref[...] = dx
        do_ref[...] = (0.5 * dx).astype(BF16)
        dgf_ref[...] += dg

    tok = pl.BlockSpec((ts, D), lambda i: (i, 0))
    row = pl.BlockSpec((1, D), lambda i: (0, 0))
    wsp = _resident((F, D))
    hid = pl.BlockSpec((ts, F), lambda i: (i, 0))
    saved = [jax.ShapeDtypeStruct((S, D), BF16), jax.ShapeDtypeStruct((S, F), BF16), jax.ShapeDtypeStruct((S, F), BF16)]
    if head is None:
        return _call(name, (S // ts,), compute, [tok, row, wsp, wsp, wsp], [tok, tok, hid, hid],
                     [jax.ShapeDtypeStruct((S, D), F32)] + saved, [], 56, [h, g, wg, wu, wd], jobs)
    return _call(name, (S // ts,), compute, [tok, row, wsp, wsp, wsp, tok, row],
                 [tok, tok, row, pl.BlockSpec((8, 128), lambda i: (0, 0)), tok, hid, hid],
                 [jax.ShapeDtypeStruct((S, D), F32), jax.ShapeDtypeStruct((S, D), BF16), jax.ShapeDtypeStruct((1, D), F32),
                  jax.ShapeDtypeStruct((8, 128), F32)] + saved, [], 60, [h, g, wg, wu, wd, *head], jobs)


def _ffn_bwd_hidden(do, gp, up, wd, name, jobs=()):
    S, D = do.shape
    F = wd.shape[0]
    ts = _tile(S, 1024)
    fb = _tile(F, F // 2)
    def compute(ins, outs, scr):
        do_ref, gp_ref, up_ref, wd_ref = ins
        dgp_ref, dup_ref, a_ref = outs
        parts = _row_parts(ts, 4)
        das = [_dot_nt(do_ref[rows, :], wd_ref[...]) for rows in parts]
        for rows, da in zip(parts, das):
            gf = gp_ref[rows, :].astype(F32)
            uf = up_ref[rows, :].astype(F32)
            sg = _sigmoid(gf)
            si = gf * sg
            dgp_ref[rows, :] = (da * uf * (sg * (1.0 + gf * (1.0 - sg)))).astype(BF16)
            dup_ref[rows, :] = (da * si).astype(BF16)
            a_ref[rows, :] = (si * uf).astype(BF16)

    tok = pl.BlockSpec((ts, D), lambda s, i: (i, 0))
    hid = pl.BlockSpec((ts, fb), lambda s, i: (i, s))
    return _call(name, (F // fb, S // ts), compute, [tok, hid, hid, pl.BlockSpec((fb, D), lambda s, i: (s, 0))],
                 [hid, hid, hid], [jax.ShapeDtypeStruct((S, F), BF16)] * 3, [], 56, [do, gp, up, wd], jobs)


def _ffn_bwd_input(dh, h, g, dgp, dup, wg, wu, name, jobs=()):
    S, D = h.shape
    F = wg.shape[0]
    ts = _tile(S, 512)

    def compute(ins, outs, scr):
        dh_ref, h_ref, g_ref, dgp_ref, dup_ref, wg_ref, wu_ref = ins
        dhi_ref, dg_ref = outs

        @pl.when(pl.program_id(0) == 0)
        def _():
            dg_ref[...] = jnp.zeros_like(dg_ref)

        dn = _dot(dgp_ref[...], wg_ref[...]) + _dot(dup_ref[...], wu_ref[...])
        dx, dg = _rms_bwd(h_ref[...], g_ref[...], dn)
        dhi_ref[...] = dh_ref[...] + dx
        dg_ref[...] += dg

    tok = pl.BlockSpec((ts, D), lambda i: (i, 0))
    hid = pl.BlockSpec((ts, F), lambda i: (i, 0))
    row = pl.BlockSpec((1, D), lambda i: (0, 0))
    return _call(name, (S // ts,), compute, [tok, tok, row, hid, hid, _resident((F, D)), _resident((F, D))], [tok, row],
                 [jax.ShapeDtypeStruct((S, D), F32), jax.ShapeDtypeStruct((1, D), F32)], [], 56,
                 [dh, h, g, dgp, dup, wg, wu], jobs)


def _ffn_wgrad(hid, tok, name, jobs=()):
    S, D = tok.shape
    F = hid.shape[1]
    fb = _tile(F, F // 2)

    def compute(ins, outs, scr):
        outs[0][...] = _dot_tn(ins[0][...], ins[1][...])

    main, jres = _call(name, (F // fb,), compute,
                       [pl.BlockSpec((S, fb), lambda j: (0, j)), _resident(tok.shape)],
                       [pl.BlockSpec((fb, D), lambda j: (j, 0))], [jax.ShapeDtypeStruct((F, D), F32)], [], 56,
                       [hid, tok], jobs)
    return main[0], jres


def _w_in_pieces(D, cq, ng):
    groups = []
    for k in range(ng):
        lo, hi, pieces = k * D, (k + 1) * D, []
        while lo < hi:
            q = lo // cq
            w = min(hi, (q + 1) * cq) - lo
            pieces.append((q, lo - q * cq, w, lo - k * D))
            lo += w
        groups.append(pieces)
    return groups


def _mix_in_fwd(h, g, win, jobs=()):
    S, D = h.shape
    NG = win.shape[0] * win.shape[2] // D
    pieces = _w_in_pieces(D, win.shape[2], NG)
    ts = _tile(S, 512)

    def compute(ins, outs, scr):
        h_ref, g_ref, w_ref = ins
        u_ref, z_ref = outs
        u = _rms_fwd(h_ref[...], g_ref[...]).astype(BF16)
        u_ref[...] = u
        for k in range(NG):
            for q, c0, w, d0 in pieces[k]:
                z_ref[k, :, d0:d0 + w] = _dot(u, w_ref[q, :, c0:c0 + w]).astype(BF16)

    return _call("mix_in_fwd", (S // ts,), compute,
                 [pl.BlockSpec((ts, D), lambda i: (i, 0)), pl.BlockSpec((1, D), lambda i: (0, 0)), _resident(win.shape)],
                 [pl.BlockSpec((ts, D), lambda i: (i, 0)), pl.BlockSpec((NG, ts, D), lambda i: (0, i, 0))],
                 [jax.ShapeDtypeStruct((S, D), BF16), jax.ShapeDtypeStruct((NG, S, D), BF16)],
                 [], 48, [h, g, win], jobs)


SUBLANES = 8


def _shifted_copies(s):
    n = s.shape[1] - SUBLANES
    for r in range(1, SUBLANES):
        s[r, 0:n, :] = s[0, r:r + n, :]


def _window(s, o, rows):
    r = o % SUBLANES
    return s[r, o - r:o - r + rows, :]


def _conv_fwd(z, wa, ba, wb, jobs=()):
    _, S, D = z.shape
    _, KA, CB = wa.shape
    KB = wb.shape[1]
    ts = _tile(S, 1024)
    r = ts // HALO
    CH = min(64, ts)

    def compute(ins, outs, scr):
        z_ref, zh_ref, wa_ref, ba_ref, wb_ref = ins
        a1_ref, q_ref = outs
        sa, sb = scr
        keep = (pl.program_id(1) > 0).astype(F32)
        sa[0, HALO:HALO + ts, :] = z_ref[0].astype(F32) * _sigmoid(z_ref[1].astype(F32))
        sa[0, 0:HALO, :] = zh_ref[0].astype(F32) * _sigmoid(zh_ref[1].astype(F32)) * keep
        _shifted_copies(sa)
        sb[HALO:HALO + ts, :] = z_ref[3].astype(F32) * z_ref[4].astype(F32)
        sb[0:HALO, :] = zh_ref[3].astype(F32) * zh_ref[4].astype(F32) * keep
        wak = [wa_ref[0, k:k + 1, :] for k in range(KA)]
        wbk = [wb_ref[0, k:k + 1, :] for k in range(KB)]
        for c0 in range(0, ts, CH):
            acc = jnp.broadcast_to(ba_ref[...], (CH, CB))
            for k in range(KA):
                acc = acc + wak[k] * _window(sa, c0 + HALO - (KA - 1) + k, CH)
            a1_ref[c0:c0 + CH, :] = acc
            v = jnp.zeros((CH, CB), F32)
            for k in range(KB):
                o = c0 + HALO - (KB - 1) + k
                v = v + wbk[k] * sb[o:o + CH, :]
            q_ref[c0:c0 + CH, :] = (z_ref[2, c0:c0 + CH, :].astype(F32) * v).astype(BF16)

    return _call("conv_fwd", (D // CB, S // ts), compute,
                 [pl.BlockSpec((5, ts, CB), lambda j, i: (0, i, j)),
                  pl.BlockSpec((5, HALO, CB), lambda j, i: (0, jnp.maximum(i * r - 1, 0), j)),
                  pl.BlockSpec((1, KA, CB), lambda j, i: (j, 0, 0)), pl.BlockSpec((1, CB), lambda j, i: (0, j)),
                  pl.BlockSpec((1, KB, CB), lambda j, i: (j, 0, 0))],
                 [pl.BlockSpec((ts, CB), lambda j, i: (i, j)), pl.BlockSpec((ts, CB), lambda j, i: (i, j))],
                 [jax.ShapeDtypeStruct((S, D), F32), jax.ShapeDtypeStruct((S, D), BF16)],
                 [pltpu.VMEM((SUBLANES, HALO + ts, CB), F32), pltpu.VMEM((HALO + ts, CB), F32)], 40, [z, z, wa, ba, wb], jobs)


def _ln_stats(a1):
    mu = jnp.mean(a1, axis=-1, keepdims=True)
    xc = a1 - mu
    rstd = lax.rsqrt(jnp.mean(xc * xc, axis=-1, keepdims=True) + EPS)
    return xc * rstd, rstd


def _mix_out_fwd(h1, a1, q, z, lng, lnb, wa, wb, wo, jobs=()):
    S, D = h1.shape
    ts = _tile(S, 512)

    def compute(ins, outs, scr):
        h_ref, a1_ref, q_ref, ga_ref, gb_ref, lng_ref, lnb_ref, wa_ref, wb_ref, wo_ref = ins
        h2_ref, a3_ref, m_ref, ya_ref, yb_ref = outs
        xhat, _ = _ln_stats(a1_ref[...])
        a2 = xhat * lng_ref[...] + lnb_ref[...]
        a3 = (a2 * _sigmoid(a2)).astype(BF16)
        a3_ref[...] = a3
        ya = _dot(a3, wa_ref[...])
        yb = _dot(q_ref[...], wb_ref[...])
        ya_ref[...] = ya.astype(BF16)
        yb_ref[...] = yb.astype(BF16)
        m = (_sigmoid(ga_ref[0].astype(F32)) * ya + _sigmoid(gb_ref[0].astype(F32)) * yb).astype(BF16)
        m_ref[...] = m
        h2_ref[...] = h_ref[...] + _dot(m, wo_ref[...])

    tok = pl.BlockSpec((ts, D), lambda i: (i, 0))
    row = pl.BlockSpec((1, D), lambda i: (0, 0))
    mat = _resident((D, D))
    return _call("mix_out_fwd", (S // ts,), compute,
                 [tok, tok, tok, pl.BlockSpec((1, ts, D), lambda i: (5, i, 0)), pl.BlockSpec((1, ts, D), lambda i: (6, i, 0)),
                  row, row, mat, mat, mat], [tok] * 5,
                 [jax.ShapeDtypeStruct((S, D), F32)] + [jax.ShapeDtypeStruct((S, D), BF16)] * 4,
                 [], 56, [h1, a1, q, z, z, lng, lnb, wa, wb, wo], jobs)


def _mix_out_bwd(dh2, a1, z, ya, yb, lng, lnb, wa, wb, wo, jobs=()):
    S, D = dh2.shape
    ts = _tile(S, 512)

    def compute(ins, outs, scr):
        dh_ref, a1_ref, ga_ref, gb_ref, ya_ref, yb_ref, lng_ref, lnb_ref, wa_ref, wb_ref, wo_ref = ins
        da1_ref, dq_ref, dga_ref, dgb_ref, dya_ref, dyb_ref, dhb_ref, dlg_ref, dlb_ref = outs

        @pl.when(pl.program_id(0) == 0)
        def _():
            dlg_ref[...] = jnp.zeros_like(dlg_ref)
            dlb_ref[...] = jnp.zeros_like(dlb_ref)

        for rows in _row_parts(ts):
            dhb = dh_ref[rows, :].astype(BF16)
            dhb_ref[rows, :] = dhb
            dm = _dot_nt(dhb, wo_ref[...])
            sa = _sigmoid(ga_ref[0, rows, :].astype(F32))
            sb = _sigmoid(gb_ref[0, rows, :].astype(F32))
            dga_ref[rows, :] = (dm * ya_ref[rows, :].astype(F32) * sa * (1.0 - sa)).astype(BF16)
            dgb_ref[rows, :] = (dm * yb_ref[rows, :].astype(F32) * sb * (1.0 - sb)).astype(BF16)
            dya = (sa * dm).astype(BF16)
            dyb = (sb * dm).astype(BF16)
            dya_ref[rows, :] = dya
            dyb_ref[rows, :] = dyb
            dq_ref[rows, :] = _dot_nt(dyb, wb_ref[...]).astype(BF16)
            da3 = _dot_nt(dya, wa_ref[...])
            xhat, rstd = _ln_stats(a1_ref[rows, :])
            a2 = xhat * lng_ref[...] + lnb_ref[...]
            sg = _sigmoid(a2)
            da2 = da3 * (sg * (1.0 + a2 * (1.0 - sg)))
            dlg_ref[...] += jnp.sum(da2 * xhat, axis=0, keepdims=True)
            dlb_ref[...] += jnp.sum(da2, axis=0, keepdims=True)
            dxh = da2 * lng_ref[...]
            da1_ref[rows, :] = (rstd * (dxh - jnp.mean(dxh, axis=-1, keepdims=True)
                                        - xhat * jnp.mean(dxh * xhat, axis=-1, keepdims=True))).astype(BF16)

    tok = pl.BlockSpec((ts, D), lambda i: (i, 0))
    row = pl.BlockSpec((1, D), lambda i: (0, 0))
    mat = _resident((D, D))
    return _call("mix_out_bwd", (S // ts,), compute,
                 [tok, tok, pl.BlockSpec((1, ts, D), lambda i: (5, i, 0)), pl.BlockSpec((1, ts, D), lambda i: (6, i, 0)),
                  tok, tok, row, row, mat, mat, mat], [tok] * 7 + [row, row],
                 [jax.ShapeDtypeStruct((S, D), BF16)] * 7 + [jax.ShapeDtypeStruct((1, D), F32)] * 2,
                 [], 56, [dh2, a1, z, z, ya, yb, lng, lnb, wa, wb, wo], jobs)


def _mixer_wgrads(a3, dya, q, dyb, mm, dhb, jobs=()):
    S, D = a3.shape
    tk = _tile(S, 512)

    def compute(ins, outs, scr):
        @pl.when(pl.program_id(0) == 0)
        def _():
            for o in outs:
                o[...] = jnp.zeros_like(o)

        for t in range(3):
            outs[t][...] += _dot_tn(ins[2 * t][...], ins[2 * t + 1][...])

    tok = pl.BlockSpec((tk, D), lambda k: (k, 0))
    return _call("mixer_wgrads", (S // tk,), compute, [tok] * 6, [pl.BlockSpec((D, D), lambda k: (0, 0))] * 3,
                 [jax.ShapeDtypeStruct((D, D), F32)] * 3, [], 56, [a3, dya, q, dyb, mm, dhb], jobs)


def _conv_bwd(z, da1, dq, dga, dgb, wa, wb, jobs=()):
    NG, S, D = z.shape
    _, KA, CB = wa.shape
    KB = wb.shape[1]
    ts = _tile(S, 1024)
    r = ts // HALO
    nt = S // ts
    CH = min(64, ts)
    last_halo = S // HALO - 1

    def compute(ins, outs, scr):
        z_ref, zp_ref, zn_ref, da1_ref, da1n_ref, dq_ref, dqn_ref, dga_ref, dgb_ref, wa_ref, wb_ref = ins
        dz_ref, dwa_ref, dba_ref, dwb_ref = outs
        sa0, sd, sp, sv, acca, accb = scr
        i = pl.program_id(1)
        prev = (i > 0).astype(F32)
        nxt = (i < nt - 1).astype(F32)

        @pl.when(i == 0)
        def _():
            acca[...] = jnp.zeros_like(acca)
            accb[...] = jnp.zeros_like(accb)
            dba_ref[...] = jnp.zeros_like(dba_ref)

        sa0[0, HALO:HALO + ts, :] = z_ref[0].astype(F32) * _sigmoid(z_ref[1].astype(F32))
        sa0[0, 0:HALO, :] = zp_ref[0].astype(F32) * _sigmoid(zp_ref[1].astype(F32)) * prev
        _shifted_copies(sa0)
        sp[HALO:HALO + ts, :] = z_ref[3].astype(F32) * z_ref[4].astype(F32)
        sp[0:HALO, :] = zp_ref[3].astype(F32) * zp_ref[4].astype(F32) * prev
        sd[0, 0:ts, :] = da1_ref[...].astype(F32)
        sd[0, ts:ts + HALO, :] = da1n_ref[...].astype(F32) * nxt
        _shifted_copies(sd)
        sv[0:ts, :] = dq_ref[...].astype(F32) * z_ref[2].astype(F32)
        sv[ts:ts + HALO, :] = dqn_ref[...].astype(F32) * zn_ref[2].astype(F32) * nxt
        dba_ref[...] += jnp.sum(sd[0, 0:ts, :], axis=0, keepdims=True)
        wak = [wa_ref[0, k:k + 1, :] for k in range(KA)]
        wbk = [wb_ref[0, k:k + 1, :] for k in range(KB)]
        for c0 in range(0, ts, CH):
            rows = slice(c0, c0 + CH)
            d1 = sd[0, rows, :]
            da0 = jnp.zeros((CH, CB), F32)
            for k in range(KA):
                da0 = da0 + wak[k] * _window(sd, c0 + (KA - 1) - k, CH)
                a0w = _window(sa0, c0 + HALO - (KA - 1) + k, CH)
                acca[k] += jnp.sum((d1 * a0w).reshape(CH // 8, 8, CB), axis=0)
            val = z_ref[0, rows, :].astype(F32)
            sg = _sigmoid(z_ref[1, rows, :].astype(F32))
            dz_ref[0, rows, :] = (da0 * sg).astype(BF16)
            dz_ref[1, rows, :] = (da0 * val * sg * (1.0 - sg)).astype(BF16)
            dv = sv[rows, :]
            v = jnp.zeros((CH, CB), F32)
            dp = jnp.zeros((CH, CB), F32)
            for k in range(KB):
                o = c0 + HALO - (KB - 1) + k
                pw = sp[o:o + CH, :]
                v = v + wbk[k] * pw
                accb[k] += jnp.sum((dv * pw).reshape(CH // 8, 8, CB), axis=0)
                o = c0 + (KB - 1) - k
                dp = dp + wbk[k] * sv[o:o + CH, :]
            dz_ref[2, rows, :] = (dq_ref[rows, :].astype(F32) * v).astype(BF16)
            dz_ref[3, rows, :] = (dp * z_ref[4, rows, :].astype(F32)).astype(BF16)
            dz_ref[4, rows, :] = (dp * z_ref[3, rows, :].astype(F32)).astype(BF16)
        dz_ref[5] = dga_ref[...]
        dz_ref[6] = dgb_ref[...]

        @pl.when(i == nt - 1)
        def _():
            dwa_ref[0] = jnp.sum(acca[...], axis=1)
            dwb_ref[0] = jnp.sum(accb[...], axis=1)

    zt = pl.BlockSpec((5, ts, CB), lambda j, i: (0, i, j))
    zp = pl.BlockSpec((5, HALO, CB), lambda j, i: (0, jnp.maximum(i * r - 1, 0), j))
    zn = pl.BlockSpec((5, HALO, CB), lambda j, i: (0, jnp.minimum((i + 1) * r, last_halo), j))
    tok = pl.BlockSpec((ts, CB), lambda j, i: (i, j))
    tokn = pl.BlockSpec((HALO, CB), lambda j, i: (jnp.minimum((i + 1) * r, last_halo), j))
    return _call("conv_bwd", (D // CB, nt), compute,
                 [zt, zp, zn, tok, tokn, tok, tokn, tok, tok,
                  pl.BlockSpec((1, KA, CB), lambda j, i: (j, 0, 0)), pl.BlockSpec((1, KB, CB), lambda j, i: (j, 0, 0))],
                 [pl.BlockSpec((NG, ts, CB), lambda j, i: (0, i, j)), pl.BlockSpec((1, KA, CB), lambda j, i: (j, 0, 0)),
                  pl.BlockSpec((1, CB), lambda j, i: (0, j)), pl.BlockSpec((1, KB, CB), lambda j, i: (j, 0, 0))],
                 [jax.ShapeDtypeStruct((NG, S, D), BF16), jax.ShapeDtypeStruct((D // CB, KA, CB), F32),
                  jax.ShapeDtypeStruct((1, D), F32), jax.ShapeDtypeStruct((D // CB, KB, CB), F32)],
                 [pltpu.VMEM((SUBLANES, HALO + ts, CB), F32), pltpu.VMEM((SUBLANES, ts + HALO, CB), F32),
                  pltpu.VMEM((HALO + ts, CB), F32), pltpu.VMEM((ts + HALO, CB), F32),
                  pltpu.VMEM((KA, 8, CB), F32), pltpu.VMEM((KB, 8, CB), F32)],
                 48, [z, z, z, da1, da1, dq, dq, dga, dgb, wa, wb], jobs)


def _mix_in_bwd(dh2, h1, g, dz, win, jobs=()):
    S, D = h1.shape
    NG = dz.shape[0]
    pieces = _w_in_pieces(D, win.shape[2], NG)
    ts = _tile(S, 512)

    def compute(ins, outs, scr):
        dh_ref, h_ref, g_ref, dz_ref, w_ref = ins
        dhi_ref, dg_ref, do_ref = outs

        @pl.when(pl.program_id(0) == 0)
        def _():
            dg_ref[...] = jnp.zeros_like(dg_ref)

        du = None
        for k in range(NG):
            for q, c0, w, d0 in pieces[k]:
                part = _dot_nt(dz_ref[k, :, d0:d0 + w], w_ref[q, :, c0:c0 + w])
                du = part if du is None else du + part
        dx, dg = _rms_bwd(h_ref[...], g_ref[...], du)
        dhi = dh_ref[...] + dx
        dhi_ref[...] = dhi
        do_ref[...] = (0.5 * dhi).astype(BF16)
        dg_ref[...] += dg

    tok = pl.BlockSpec((ts, D), lambda i: (i, 0))
    row = pl.BlockSpec((1, D), lambda i: (0, 0))
    return _call("mix_in_bwd", (S // ts,), compute,
                 [tok, tok, row, pl.BlockSpec((NG, ts, D), lambda i: (0, i, 0)), _resident(win.shape)],
                 [tok, row, tok],
                 [jax.ShapeDtypeStruct((S, D), F32), jax.ShapeDtypeStruct((1, D), F32), jax.ShapeDtypeStruct((S, D), BF16)],
                 [], 56, [dh2, h1, g, dz, win], jobs)


def _w_in_grad(u, dz, jobs=()):
    S, D = u.shape
    NG = dz.shape[0]

    def compute(ins, outs, scr):
        outs[0][...] = _dot_tn(ins[0][...], ins[1][0])

    return _call("w_in_grad", (NG,), compute,
                 [_resident(u.shape), pl.BlockSpec((1, S, D), lambda j: (j, 0, 0))],
                 [pl.BlockSpec((D, D), lambda j: (0, j))], [jax.ShapeDtypeStruct((D, NG * D), F32)], [], 48, [u, dz], jobs)


def _chip_sums(place, grads, got, kind, name):
    n = len(grads)
    qr, qc = _quarter_shape(grads[0].shape, kind)
    h = qr // 2
    tr = _row_tile(h)
    nr = h // tr

    def body(pc_ref, *refs):
        g_refs, got_refs, b_refs, f_refs = refs[:n], refs[n:2 * n], refs[2 * n:3 * n], refs[3 * n:]
        own = pl.program_id(1) == pc_ref[0]
        for a in range(n):
            s = g_refs[a][...] + got_refs[a][0]
            b_refs[a][0] = s.astype(BF16)

            @pl.when(own)
            def _():
                f_refs[a][...] = s

    if kind == "rows":
        gspec = pl.BlockSpec((tr, qc), lambda r, q, pc: (q * (2 * nr) + pc[1] * nr + r, 0))
    else:
        gspec = pl.BlockSpec((tr, qc), lambda r, q, pc: (pc[1] * nr + r, q))
    lspec = pl.BlockSpec((1, tr, qc), lambda r, q, pc: (q, r, 0))
    res = pl.pallas_call(
        body, name=name,
        grid_spec=pltpu.PrefetchScalarGridSpec(
            num_scalar_prefetch=1, grid=(nr, NS), in_specs=[gspec] * n + [lspec] * n,
            out_specs=[lspec] * n + [pl.BlockSpec((tr, qc), lambda r, q, pc: (r, 0))] * n),
        out_shape=[jax.ShapeDtypeStruct((NS, h, qc), BF16)] * n + [jax.ShapeDtypeStruct((h, qc), F32)] * n,
        compiler_params=_cparams(2, 48),
    )(place, *grads, *got)
    return res[:n], res[n:]


def _totals(place, own, got, name):
    n = len(own)
    h, qc = own[0].shape
    tr = _row_tile(h)
    nr = h // tr
    got = [list(g) if isinstance(g, (list, tuple)) else [g] for g in got]
    m = len(got[0])

    def body(pc_ref, *refs):
        own_refs, got_refs, o_refs = refs[:n], refs[n:n + n * m], refs[n + n * m:]
        for a in range(n):
            acc = own_refs[a][...]
            for g in got_refs[a * m:(a + 1) * m]:
                for k in range(g.shape[0]):
                    acc = acc + g[k].astype(F32)
            o_refs[a][...] = acc

    lands = [pl.BlockSpec((g.shape[0], tr, qc), lambda r, pc: (0, r, 0)) for gs in got for g in gs]
    return pl.pallas_call(
        body, name=name,
        grid_spec=pltpu.PrefetchScalarGridSpec(
            num_scalar_prefetch=1, grid=(nr,),
            in_specs=[pl.BlockSpec((tr, qc), lambda r, pc: (r, 0))] * n + lands,
            out_specs=[pl.BlockSpec((tr, qc), lambda r, pc: (pc[1] * nr + r, 0))] * n),
        out_shape=[jax.ShapeDtypeStruct((2 * h, qc), F32)] * n,
        compiler_params=_cparams(1, 48),
    )(place, *own, *[g for gs in got for g in gs])


def _adamw(ws, gs, ms, vs, name):
    n = len(ws)
    R, C = ws[0].shape
    tr = _row_tile(R, (36 << 20) // (7 * 2 * 4 * n * C))

    def body(*refs):
        w_refs, g_refs, m_refs, v_refs = refs[:n], refs[n:2 * n], refs[2 * n:3 * n], refs[3 * n:4 * n]
        d_refs, mo_refs, vo_refs = refs[4 * n:5 * n], refs[5 * n:6 * n], refs[6 * n:]
        for a in range(n):
            d_refs[a][...], mo_refs[a][...], vo_refs[a][...] = _adamw_math(w_refs[a][...], g_refs[a][...], m_refs[a][...],
                                                                         v_refs[a][...])

    blk = pl.BlockSpec((tr, C), lambda r: (r, 0))
    res = pl.pallas_call(
        body, name=name, grid=(R // tr,),
        in_specs=[blk] * (4 * n), out_specs=[blk] * (3 * n),
        out_shape=[jax.ShapeDtypeStruct((R, C), F32)] * (3 * n),
        compiler_params=_cparams(1, 56),
    )(*ws, *gs, *ms, *vs)
    return res[:n], res[n:2 * n], res[2 * n:]


def kernel(x, ffn1_norm, ffn1_w_gate, ffn1_w_up, ffn1_w_down, mix_norm, w_in, a_dw_w, a_dw_b, a_ln_g, a_ln_b, a_w_out, b_conv_w, b_w_out, w_o, ffn2_norm, ffn2_w_gate, ffn2_w_up, ffn2_w_down, final_norm, loss_target, m_ffn1_norm, m_ffn1_w_gate, m_ffn1_w_up, m_ffn1_w_down, m_mix_norm, m_w_in, m_a_dw_w, m_a_dw_b, m_a_ln_g, m_a_ln_b, m_a_w_out, m_b_conv_w, m_b_w_out, m_w_o, m_ffn2_norm, m_ffn2_w_gate, m_ffn2_w_up, m_ffn2_w_down, m_final_norm, v_ffn1_norm, v_ffn1_w_gate, v_ffn1_w_up, v_ffn1_w_down, v_mix_norm, v_w_in, v_a_dw_w, v_a_dw_b, v_a_ln_g, v_a_ln_b, v_a_w_out, v_b_conv_w, v_b_w_out, v_w_o, v_ffn2_norm, v_ffn2_w_gate, v_ffn2_w_up, v_ffn2_w_down, v_final_norm):
    names = ["ffn1_norm", "ffn1_w_gate", "ffn1_w_up", "ffn1_w_down", "mix_norm", "w_in", "a_dw_w", "a_dw_b", "a_ln_g",
             "a_ln_b", "a_w_out", "b_conv_w", "b_w_out", "w_o", "ffn2_norm", "ffn2_w_gate", "ffn2_w_up", "ffn2_w_down",
             "final_norm"]
    W = dict(zip(names, [ffn1_norm, ffn1_w_gate, ffn1_w_up, ffn1_w_down, mix_norm, w_in, a_dw_w, a_dw_b, a_ln_g, a_ln_b,
                         a_w_out, b_conv_w, b_w_out, w_o, ffn2_norm, ffn2_w_gate, ffn2_w_up, ffn2_w_down, final_norm]))
    M = dict(zip(names, [m_ffn1_norm, m_ffn1_w_gate, m_ffn1_w_up, m_ffn1_w_down, m_mix_norm, m_w_in, m_a_dw_w, m_a_dw_b,
                         m_a_ln_g, m_a_ln_b, m_a_w_out, m_b_conv_w, m_b_w_out, m_w_o, m_ffn2_norm, m_ffn2_w_gate,
                         m_ffn2_w_up, m_ffn2_w_down, m_final_norm]))
    V = dict(zip(names, [v_ffn1_norm, v_ffn1_w_gate, v_ffn1_w_up, v_ffn1_w_down, v_mix_norm, v_w_in, v_a_dw_w, v_a_dw_b,
                         v_a_ln_g, v_a_ln_b, v_a_w_out, v_b_conv_w, v_b_w_out, v_w_o, v_ffn2_norm, v_ffn2_w_gate,
                         v_ffn2_w_up, v_ffn2_w_down, v_final_norm]))
    transposed = ("ffn1_w_gate", "ffn1_w_up", "ffn2_w_gate", "ffn2_w_up")
    vecs = ["ffn1_norm", "mix_norm", "a_dw_b", "a_ln_g", "a_ln_b", "ffn2_norm", "final_norm"]
    ffn1 = ["ffn1_w_gate", "ffn1_w_up", "ffn1_w_down"]
    ffn2 = ["ffn2_w_gate", "ffn2_w_up", "ffn2_w_down"]
    outp = ["a_w_out", "b_w_out", "w_o"]

    S, D = x.shape[1], x.shape[2]
    CB = D // NS
    KA, KB = a_dw_w.shape[1], b_conv_w.shape[1]
    px, py, pc = lax.axis_index("x"), lax.axis_index("y"), lax.axis_index("c")
    chip = 2 * px + py
    place = jnp.stack([chip, pc]).astype(jnp.int32)
    h0 = x.reshape(S, D)
    tgt = loss_target.reshape(S, D)
    row = lambda n: pltpu.with_memory_space_constraint(W[n].reshape(1, D), pltpu.HBM)
    pad = lambda a, r: jnp.concatenate([a, jnp.zeros((r - a.shape[0], a.shape[1]), F32)], axis=0)

    def quarter(P, n):
        return jnp.transpose(P[n][0]) if n in transposed else P[n][0]

    def unquarter(a, n):
        return (jnp.transpose(a) if n in transposed else a).reshape(W[n].shape)

    wq = {n: quarter(W, n).astype(BF16) for n in ffn1 + ffn2 + outp + ["w_in"]}

    f1 = _exchange("gather_ffn1", [_Gather([wq[n] for n in ffn1], ["rows"] * 3)])[0]
    g_in = _Gather([wq["w_in"], pad(a_dw_w[0], 32), pad(b_conv_w[0], 16)], ["rows"] * 3)
    (h1, n1, gp1, up1), ((win, taps_a, taps_b),) = _ffn_fwd(h0, row("ffn1_norm"), *f1, "ffn1_fwd", [g_in])
    win = win.reshape(NS, D, -1)
    wa_taps = taps_a.reshape(NS, 32, CB)[:, :KA]
    wb_taps = taps_b.reshape(NS, 16, CB)[:, :KB]
    g_out = _Gather([wq[n] for n in outp], ["rows"] * 3)
    (u, z), ((wa_out, wb_out, wo),) = _mix_in_fwd(h1, row("mix_norm"), win, [g_out])
    g_f2 = _Gather([wq["ffn2_w_gate"], wq["ffn2_w_up"]], ["rows"] * 2)
    (a1, q), ((f2g, f2u),) = _conv_fwd(z, wa_taps, row("a_dw_b"), wb_taps, [g_f2])
    (h2, a3, mm, ya, yb), ((f2d,),) = _mix_out_fwd(h1, a1, q, z, row("a_ln_g"), row("a_ln_b"), wa_out, wb_out, wo,
                                                   [_Gather([wq["ffn2_w_down"]], ["rows"])])
    (dh3, do2, d_final, loss_part, n2, gp2, up2), _ = _ffn_fwd(h2, row("ffn2_norm"), f2g, f2u, f2d, "ffn2_fwd_loss",
                                                               head=(tgt, row("final_norm")))

    (dgp2, dup2, act2), _ = _ffn_bwd_hidden(do2, gp2, up2, f2d, "ffn2_bwd_hidden")
    (dh2, d_ffn2), _ = _ffn_bwd_input(dh3, h2, row("ffn2_norm"), dgp2, dup2, f2g, f2u, "ffn2_bwd_input")
    g2 = [_ffn_wgrad(dgp2, n2, "ffn2_dwg")[0], _ffn_wgrad(dup2, n2, "ffn2_dwu")[0], _ffn_wgrad(act2, do2, "ffn2_dwd")[0]]
    (da1, dq, dga, dgb, dya, dyb, dh2b, d_lng, d_lnb), (got,) = _mix_out_bwd(
        dh2, a1, z, ya, yb, row("a_ln_g"), row("a_ln_b"), wa_out, wb_out, wo, [_ToSibling(g2, ["rows"] * 3)])
    wire2, own2 = _chip_sums(place, g2, got, "rows", "ffn2_chip_sums")
    (dz, d_wa, d_ba, d_wb), (got,) = _conv_bwd(z, da1, dq, dga, dgb, wa_taps, wb_taps, [_ToChips(wire2)])
    half2 = _totals(place, own2, got, "ffn2_totals")
    go, (tot2,) = _mixer_wgrads(a3, dya, q, dyb, mm, dh2b, [_SwapHalves(half2)])
    (g_win,), (got_o,) = _w_in_grad(u, dz, [_ToSibling(go, ["rows"] * 3)])
    wire_o, own_o = _chip_sums(place, go, got_o, "rows", "mixer_chip_sums")
    (dh1, d_mix, do1), (got, land_o) = _mix_in_bwd(dh2, h1, row("mix_norm"), dz, win,
                                                   [_ToSibling([g_win], ["cols"]), _ToChips(wire_o)])
    wire_in, own_in = _chip_sums(place, [g_win], got, "cols", "w_in_chip_sum")
    half_o = _totals(place, own_o, land_o, "mixer_totals")
    (dgp1, dup1, act1), (near_in, tot_o) = _ffn_bwd_hidden(do1, gp1, up1, f1[2], "ffn1_bwd_hidden",
                                                           [_ToChips(wire_in, (0, 1)), _SwapHalves(half_o)])
    (dx, d_ffn1), _ = _ffn_bwd_input(dh1, h0, row("ffn1_norm"), dgp1, dup1, f1[0], f1[1], "ffn1_bwd_input")
    g1g, (far_in,) = _ffn_wgrad(dgp1, n1, "ffn1_dwg", [_ToChips(wire_in, (2,))])
    half_in = _totals(place, own_in, [[near_in[0], far_in[0]]], "w_in_total")
    g1u, (tot_in, got_g) = _ffn_wgrad(dup1, n1, "ffn1_dwu", [_SwapHalves(half_in), _ToSibling([g1g], ["rows"])])
    wire_g, own_g = _chip_sums(place, [g1g], got_g, "rows", "ffn1_dwg_chip_sum")
    g1d, (got_u, land_g) = _ffn_wgrad(act1, do1, "ffn1_dwd", [_ToSibling([g1u], ["rows"]), _ToChips(wire_g)])
    wire_u, own_u = _chip_sums(place, [g1u], got_u, "rows", "ffn1_dwu_chip_sum")
    half_g = _totals(place, own_g, land_g, "ffn1_dwg_total")
    got_d, land_u, tot_g = _exchange("tail_exchange_1", [_ToSibling([g1d], ["rows"]), _ToChips(wire_u), _SwapHalves(half_g)])
    wire_d, own_d = _chip_sums(place, [g1d], got_d, "rows", "ffn1_dwd_chip_sum")
    half_u = _totals(place, own_u, land_u, "ffn1_dwu_total")
    land_d, tot_u = _exchange("tail_exchange_2", [_ToChips(wire_d), _SwapHalves(half_u)])
    half_d = _totals(place, own_d, land_d, "ffn1_dwd_total")
    (tot_d,) = _exchange("tail_exchange_3", [_SwapHalves(half_d)])
    tot1 = [tot_g[0], tot_u[0], tot_d[0]]
    totals = dict(zip(ffn2 + ["w_in"] + ffn1 + outp, list(tot2) + list(tot_in) + tot1 + list(tot_o)))

    vec_grads = {"ffn1_norm": d_ffn1, "mix_norm": d_mix, "a_dw_b": d_ba, "a_ln_g": d_lng, "a_ln_b": d_lnb,
                 "ffn2_norm": d_ffn2, "final_norm": d_final}
    small = _allreduce_small([vec_grads[n] for n in vecs], d_wa, d_wb, loss_part)
    loss = small[LOSS_ROW, 0]
    taps = ["a_dw_w", "b_conv_w"]
    small_out = _small_adamw(place, small, [[P[n].reshape(1, D) for P in (W, M, V)] for n in vecs],
                             [[P[n] for P in (W, M, V)] for n in taps])

    grads, deltas, new_m, new_v = {}, {}, {}, {}
    for n, (g_, d_, m_, v_) in zip(vecs + taps, small_out):
        shp = W[n].shape
        grads[n], deltas[n], new_m[n], new_v[n] = g_.reshape(shp), d_.reshape(shp), m_.reshape(shp), v_.reshape(shp)
    for group, tag in ((ffn1 + ffn2, "ffn"), (["w_in"], "w_in"), (outp, "mixer")):
        ds, ms, vs = _adamw([quarter(W, n) for n in group], [totals[n] for n in group], [quarter(M, n) for n in group],
                            [quarter(V, n) for n in group], tag + "_adamw")
        for n, d_, m_, v_ in zip(group, ds, ms, vs):
            grads[n], deltas[n], new_m[n], new_v[n] = (unquarter(totals[n], n), unquarter(d_, n), unquarter(m_, n),
                                                       unquarter(v_, n))
    return (loss, dx.reshape(x.shape), *[grads[n] for n in names], *[deltas[n] for n in names],
            *[new_m[n] for n in names], *[new_v[n] for n in names])
```

```python
import functools

import jax
import jax.numpy as jnp
from jax import lax
from jax.experimental import pallas as pl
from jax.experimental.pallas import tpu as pltpu

F32 = jnp.float32
BF16 = jnp.bfloat16
EPS = 1e-6
NS = 4
HALO = 32
MESH = pl.DeviceIdType.MESH
IN_HBM = pl.BlockSpec(memory_space=pltpu.HBM)

ADAM_LR = 0.001
ADAM_B1 = 0.9
ADAM_B2 = 0.999
ADAM_EPS = 1e-08
ADAM_WD = 0.01
ADAM_STEP = 10


def _cparams(n_axes, vmem_mb):
    return pltpu.CompilerParams(dimension_semantics=("arbitrary",) * n_axes, vmem_limit_bytes=vmem_mb << 20)


def _tile(n, t):
    return t if n % t == 0 else n


def _row_parts(rows, n=2):
    if rows % (16 * n):
        return [slice(0, rows)]
    return [slice(p * (rows // n), (p + 1) * (rows // n)) for p in range(n)]


def _resident(shape):
    return pl.BlockSpec(shape, lambda *_: (0,) * len(shape), pipeline_mode=pl.Buffered(1))


def _row_tile(n, cap=256):
    for t in (256, 176, 128, 64, 32, 16, 8):
        if t <= cap and n % t == 0:
            return t
    return n


def _dot(a, b):
    return jnp.dot(a, b, preferred_element_type=F32)


def _dot_nt(a, b):
    return lax.dot_general(a, b, (((1,), (1,)), ((), ())), preferred_element_type=F32)


def _dot_tn(a, b):
    return lax.dot_general(a, b, (((0,), (0,)), ((), ())), preferred_element_type=F32)


def _sigmoid(x):
    return jax.nn.sigmoid(x)


def _rms_fwd(x, g):
    r = lax.rsqrt(jnp.mean(x * x, axis=-1, keepdims=True) + EPS)
    return x * r * g


def _rms_bwd(x, g, dn):
    r = lax.rsqrt(jnp.mean(x * x, axis=-1, keepdims=True) + EPS)
    xr = x * r
    dg = jnp.sum(dn * xr, axis=0, keepdims=True)
    w = dn * g
    dx = r * w - xr * (r * r) * jnp.mean(x * w, axis=-1, keepdims=True)
    return dx, dg


def _place():
    x, y, c = lax.axis_index("x"), lax.axis_index("y"), lax.axis_index("c")
    chips = [(1 - x, y), (x, 1 - y), (1 - x, 1 - y)]
    return x, y, c, chips


def _quarter_shape(full_shape, kind):
    r, c = full_shape
    return (r // NS, c) if kind == "rows" else (r, c // NS)


def _half_of_quarter(ref, kind, q, pc):
    qr, qc = _quarter_shape(ref.shape, kind)
    h = qr // 2
    if kind == "rows":
        return ref.at[pl.ds(q * qr + pc * h, h), :]
    return ref.at[pl.ds(pc * h, h), pl.ds(q * qc, qc)]


def _quarter(ref, kind, q):
    qr, qc = _quarter_shape(ref.shape, kind)
    if kind == "rows":
        return ref.at[pl.ds(q * qr, qr), :]
    return ref.at[:, pl.ds(q * qc, qc)]


def _rows_half(ref, pc):
    h = ref.shape[0] // 2
    return ref.at[pl.ds(pc * h, h)]


class _Gather:
    def __init__(self, quarters, kinds):
        self.ins = list(quarters)
        self.kinds = list(kinds)
        n = len(self.ins)
        self.out_shape = [jax.ShapeDtypeStruct((NS * a.shape[0], a.shape[1]) if k == "rows" else (a.shape[0], NS * a.shape[1]),
                                               a.dtype) for a, k in zip(self.ins, self.kinds)]
        self.scratch = [pltpu.SemaphoreType.DMA((n, 6)), pltpu.SemaphoreType.DMA((n, 6)), pltpu.SemaphoreType.DMA((n,))]
        self.aliases = {}

    def _copy(self, outs, sems, a, k, q, pc, to, src=None):
        dst = _half_of_quarter(outs[a], self.kinds[a], q, pc)
        return pltpu.make_async_remote_copy(src_ref=dst if src is None else src, dst_ref=dst,
                                            send_sem=sems[0].at[a, k], recv_sem=sems[1].at[a, k],
                                            device_id=to, device_id_type=MESH)

    def _mine(self, ins, outs, sems, a, p):
        return pltpu.make_async_copy(ins[a], _quarter(outs[a], self.kinds[a], p), sems[2].at[a])

    def start(self, ins, outs, sems):
        x, y, c, chips = _place()
        p = 2 * x + y
        for a in range(len(ins)):
            self._mine(ins, outs, sems, a, p).start()
            for j, chip in enumerate(chips):
                self._copy(outs, sems, a, j, p, c, (*chip, c), src=_rows_half(ins[a], c)).start()

    def relay(self, ins, outs, sems):
        x, y, c, chips = _place()
        sibling = (x, y, 1 - c)
        for a in range(len(ins)):
            for j, (qx, qy) in enumerate(chips):
                q = 2 * qx + qy
                self._copy(outs, sems, a, j, q, c, sibling).wait_recv()
                self._copy(outs, sems, a, 3 + j, q, c, sibling).start()

    def finish(self, ins, outs, sems):
        x, y, c, chips = _place()
        p = 2 * x + y
        sibling = (x, y, 1 - c)
        n = len(ins)
        for a in range(n):
            for j, (qx, qy) in enumerate(chips):
                q = 2 * qx + qy
                self._copy(outs, sems, a, 3 + j, q, 1 - c, sibling).wait_recv()
                self._copy(outs, sems, a, j, p, c, (qx, qy, c), src=_rows_half(ins[a], c)).wait_send()
                self._copy(outs, sems, a, 3 + j, q, c, sibling).wait_send()
            self._mine(ins, outs, sems, a, p).wait()


class _ToSibling:
    def __init__(self, grads, kinds):
        self.ins = list(grads)
        self.kinds = list(kinds)
        n = len(self.ins)
        self.out_shape = []
        for g, k in zip(self.ins, self.kinds):
            qr, qc = _quarter_shape(g.shape, k)
            self.out_shape.append(jax.ShapeDtypeStruct((NS, qr // 2, qc), g.dtype))
        self.scratch = [pltpu.SemaphoreType.DMA((n, NS)), pltpu.SemaphoreType.DMA((n, NS))]
        self.aliases = {}

    def _copies(self, ins, outs, sems):
        x, y, c, _ = _place()
        return [pltpu.make_async_remote_copy(src_ref=_half_of_quarter(ins[a], self.kinds[a], q, 1 - c), dst_ref=outs[a].at[q],
                                             send_sem=sems[0].at[a, q], recv_sem=sems[1].at[a, q],
                                             device_id=(x, y, 1 - c), device_id_type=MESH)
                for a in range(len(ins)) for q in range(NS)]

    def start(self, ins, outs, sems):
        for cp in self._copies(ins, outs, sems):
            cp.start()

    def finish(self, ins, outs, sems):
        for cp in self._copies(ins, outs, sems):
            cp.wait()


class _ToChips:
    def __init__(self, sums, which=(0, 1, 2)):
        self.ins = list(sums)
        self.which = tuple(which)
        n, m = len(self.ins), len(self.which)
        self.out_shape = [jax.ShapeDtypeStruct((m,) + s.shape[1:], s.dtype) for s in self.ins]
        self.scratch = [pltpu.SemaphoreType.DMA((n, m)), pltpu.SemaphoreType.DMA((n, m))]
        self.aliases = {}

    def _copies(self, ins, outs, sems):
        x, y, c, chips = _place()
        return [pltpu.make_async_remote_copy(src_ref=ins[a].at[2 * chips[j][0] + chips[j][1]], dst_ref=outs[a].at[k],
                                             send_sem=sems[0].at[a, k], recv_sem=sems[1].at[a, k],
                                             device_id=(*chips[j], c), device_id_type=MESH)
                for a in range(len(ins)) for k, j in enumerate(self.which)]

    def start(self, ins, outs, sems):
        for cp in self._copies(ins, outs, sems):
            cp.start()

    def finish(self, ins, outs, sems):
        for cp in self._copies(ins, outs, sems):
            cp.wait()


class _SwapHalves:
    def __init__(self, quarters):
        self.ins = list(quarters)
        n = len(self.ins)
        self.out_shape = [jax.ShapeDtypeStruct(g.shape, g.dtype) for g in self.ins]
        self.scratch = [pltpu.SemaphoreType.DMA((n,)), pltpu.SemaphoreType.DMA((n,))]
        self.aliases = {a: a for a in range(n)}

    def _copy(self, outs, sems, a, pc):
        x, y, c, _ = _place()
        rows = _rows_half(outs[a], pc)
        return pltpu.make_async_remote_copy(src_ref=rows, dst_ref=rows, send_sem=sems[0].at[a], recv_sem=sems[1].at[a],
                                            device_id=(x, y, 1 - c), device_id_type=MESH)

    def start(self, ins, outs, sems):
        c = lax.axis_index("c")
        for a in range(len(outs)):
            self._copy(outs, sems, a, c).start()

    def finish(self, ins, outs, sems):
        c = lax.axis_index("c")
        for a in range(len(outs)):
            self._copy(outs, sems, a, c).wait_send()
            self._copy(outs, sems, a, 1 - c).wait_recv()


def _call(name, grid, compute, in_specs, out_specs, out_shape, scratch, vmem_mb, args, jobs=(), own_aliases=None):
    n_in, n_out, n_scr = len(in_specs), len(out_specs), len(scratch)
    ji = [len(j.ins) for j in jobs]
    jo = [len(j.out_shape) for j in jobs]
    js = [len(j.scratch) for j in jobs]

    def body(*refs):
        pos = [0]

        def take(k):
            r = refs[pos[0]:pos[0] + k]
            pos[0] += k
            return r

        ins, jins = take(n_in), [take(k) for k in ji]
        outs, jouts = take(n_out), [take(k) for k in jo]
        scr, jscr = take(n_scr), [take(k) for k in js]
        if jobs and grid:
            ids = [pl.program_id(a) for a in range(len(grid))]
            first = functools.reduce(jnp.logical_and, [i == 0 for i in ids])
            last = functools.reduce(jnp.logical_and, [i == g - 1 for i, g in zip(ids, grid)])

            @pl.when(first)
            def _():
                for j, a, b, c in zip(jobs, jins, jouts, jscr):
                    j.start(a, b, c)

            @pl.when(last)
            def _():
                for j, a, b, c in zip(jobs, jins, jouts, jscr):
                    if hasattr(j, "relay"):
                        j.relay(a, b, c)
        elif jobs:
            for j, a, b, c in zip(jobs, jins, jouts, jscr):
                j.start(a, b, c)
            for j, a, b, c in zip(jobs, jins, jouts, jscr):
                if hasattr(j, "relay"):
                    j.relay(a, b, c)
        compute(ins, outs, scr)
        if jobs and grid:
            @pl.when(last)
            def _():
                for j, a, b, c in zip(jobs, jins, jouts, jscr):
                    j.finish(a, b, c)
        elif jobs:
            for j, a, b, c in zip(jobs, jins, jouts, jscr):
                j.finish(a, b, c)

    aliases = dict(own_aliases or {})
    in_off, out_off = n_in, n_out
    for j, a, b in zip(jobs, ji, jo):
        for s, d in j.aliases.items():
            aliases[in_off + s] = out_off + d
        in_off += a
        out_off += b
    res = pl.pallas_call(
        body, name=name, grid=grid,
        in_specs=list(in_specs) + [IN_HBM] * sum(ji), out_specs=list(out_specs) + [IN_HBM] * sum(jo),
        out_shape=list(out_shape) + [pltpu.HBM(s.shape, s.dtype) for j in jobs for s in j.out_shape],
        scratch_shapes=list(scratch) + [s for j in jobs for s in j.scratch],
        input_output_aliases=aliases, compiler_params=_cparams(len(grid), vmem_mb),
    )(*args, *[a for j in jobs for a in j.ins])
    res = list(res)
    main, rest, jres = res[:n_out], res[n_out:], []
    for k in jo:
        jres.append(rest[:k])
        rest = rest[k:]
    return main, jres


def _exchange(name, jobs):
    return _call(name, (), lambda ins, outs, scr: None, [], [], [], [], 16, [], jobs)[1]


def _small_rows(ka, kb):
    first_a = 8
    first_b = first_a + -(-ka // 8) * 8
    return first_a, first_b, first_b + -(-kb // 8) * 8


LOSS_ROW = 7


def _allreduce_small(vecs, taps_a, taps_b, loss_part):
    counts = [len(v) for v in vecs]
    flat = [r for v in vecs for r in v]
    n = len(flat)
    C = flat[0].shape[1]
    NQ, KA, CB = taps_a.shape
    KB = taps_b.shape[1]
    first_a, first_b, R = _small_rows(KA, KB)
    assert len(vecs) <= LOSS_ROW < first_a
    N = 8

    def body(*refs):
        vec_refs = list(refs[:n])
        ta_ref, tb_ref, loss_ref, out_ref, v_ref, gath, send_sems, recv_sems, local_sem = refs[n:]
        v_ref[...] = jnp.zeros_like(v_ref)
        v_ref[LOSS_ROW:LOSS_ROW + 1, 0:loss_ref.shape[1]] = loss_ref[0:1, :]
        for i, k in enumerate(counts):
            parts, vec_refs = vec_refs[:k], vec_refs[k:]
            v_ref[i:i + 1, :] = functools.reduce(lambda a, b: a + b, [r[...] for r in parts])
        for q in range(NQ):
            v_ref[first_a:first_a + KA, q * CB:(q + 1) * CB] = ta_ref[q]
            v_ref[first_b:first_b + KB, q * CB:(q + 1) * CB] = tb_ref[q]
        x, y, c, chips = _place()
        me, sibling = (x, y, c), (x, y, 1 - c)

        def rows(px, py, pc):
            return gath.at[pl.ds((4 * px + 2 * py + pc) * R, R), :]

        def copy(k, block, to, src=None):
            return pltpu.make_async_remote_copy(src_ref=rows(*block) if src is None else src, dst_ref=rows(*block),
                                                send_sem=send_sems.at[k], recv_sem=recv_sems.at[k],
                                                device_id=to, device_id_type=MESH)

        mine = pltpu.make_async_copy(v_ref, rows(*me), local_sem)
        mine.start()
        first = [copy(0, me, sibling, src=v_ref)]
        first += [copy(1 + j, me, (*chip, c), src=v_ref) for j, chip in enumerate(chips)]
        for cp in first:
            cp.start()
        passed = [copy(4 + j, (*chip, c), sibling) for j, chip in enumerate(chips)]
        for j, chip in enumerate(chips):
            copy(1 + j, (*chip, c), me).wait_recv()
            passed[j].start()
        copy(0, sibling, me).wait_recv()
        for j, chip in enumerate(chips):
            copy(4 + j, (*chip, 1 - c), me).wait_recv()
        for cp in first + passed:
            cp.wait_send()
        mine.wait()
        acc = gath[0:R, :]
        for d in range(1, N):
            acc = acc + gath[d * R:(d + 1) * R, :]
        out_ref[...] = acc

    vmem = pl.BlockSpec(memory_space=pltpu.VMEM)
    return pl.pallas_call(
        body, name="allreduce_small",
        in_specs=[vmem] * (n + 3), out_specs=vmem,
        out_shape=jax.ShapeDtypeStruct((R, C), F32),
        scratch_shapes=[pltpu.VMEM((R, C), F32), pltpu.VMEM((N * R, C), F32), pltpu.SemaphoreType.DMA((7,)),
                        pltpu.SemaphoreType.DMA((7,)), pltpu.SemaphoreType.DMA],
    )(*flat, taps_a, taps_b, loss_part)


def _adamw_math(w, g, m, v):
    c1 = 1.0 - ADAM_B1 ** ADAM_STEP
    c2 = 1.0 - ADAM_B2 ** ADAM_STEP
    mn = ADAM_B1 * m + (1.0 - ADAM_B1) * g
    vn = ADAM_B2 * v + (1.0 - ADAM_B2) * (g * g)
    return -ADAM_LR * ((mn / c1) / (jnp.sqrt(vn / c2) + ADAM_EPS) + ADAM_WD * w), mn, vn


def _small_adamw(place, small, vec_wmv, tap_wmv):
    n = len(vec_wmv)
    D = small.shape[1]
    CB = tap_wmv[0][0].shape[2]
    ks = [t[0].shape[1] for t in tap_wmv]
    firsts = _small_rows(*ks)[:2]

    def body(place_ref, small_ref, *refs):
        ins, outs = refs[:3 * (n + 2)], refs[3 * (n + 2):]
        chip = place_ref[0]
        for i in range(n):
            g = small_ref[i:i + 1, :]
            d, mn, vn = _adamw_math(ins[3 * i][...], g, ins[3 * i + 1][...], ins[3 * i + 2][...])
            for o, val in zip(outs[4 * i:4 * i + 4], (g, d, mn, vn)):
                o[...] = val
        for t, (row0, k) in enumerate(zip(firsts, ks)):
            g = jnp.zeros((k, CB), F32)
            for q in range(D // CB):
                g = g + jnp.where(chip == q, small_ref[row0:row0 + k, q * CB:(q + 1) * CB], 0.0)
            w_ref, m_ref, v_ref = ins[3 * (n + t):3 * (n + t) + 3]
            d, mn, vn = _adamw_math(w_ref[0], g, m_ref[0], v_ref[0])
            for o, val in zip(outs[4 * (n + t):4 * (n + t) + 4], (g, d, mn, vn)):
                o[0] = val

    flat = [a for wmv in list(vec_wmv) + list(tap_wmv) for a in wmv]
    shapes = [jax.ShapeDtypeStruct(wmv[0].shape, F32) for wmv in list(vec_wmv) + list(tap_wmv) for _ in range(4)]
    vmem = pl.BlockSpec(memory_space=pltpu.VMEM)
    res = pl.pallas_call(
        body, name="small_adamw",
        in_specs=[pl.BlockSpec(memory_space=pltpu.SMEM)] + [vmem] * (1 + len(flat)), out_specs=[vmem] * len(shapes),
        out_shape=shapes,
    )(place, small, *flat)
    return [res[4 * i:4 * i + 4] for i in range(n + 2)]


def _ffn_fwd(h, g, wg, wu, wd, name, jobs=(), head=None):
    S, D = h.shape
    F = wg.shape[0]
    ts = _tile(S, 512)
    fb = _tile(F, F // 2)
    nf = F // fb

    def compute(ins, outs, scr):
        h_ref, g_ref, wg_ref, wu_ref, wd_ref = ins[:5]
        n_ref, gp_ref, up_ref = outs[-3:]
        x = h_ref[...]
        n = _rms_fwd(x, g_ref[...]).astype(BF16)
        n_ref[...] = n
        acc = None
        for j in range(nf):
            cols = slice(j * fb, (j + 1) * fb)
            gp = _dot_nt(n, wg_ref[cols, :])
            up = _dot_nt(n, wu_ref[cols, :])
            gp_ref[:, cols] = gp.astype(BF16)
            up_ref[:, cols] = up.astype(BF16)
            part = _dot((gp * _sigmoid(gp) * up).astype(BF16), wd_ref[cols, :])
            acc = part if acc is None else acc + part
        ho = x + 0.5 * acc
        if head is None:
            outs[0][...] = ho
            return
        t_ref, gf_ref = ins[5:]
        dh_ref, do_ref, dgf_ref, loss_ref = outs[:4]

        @pl.when(pl.program_id(0) == 0)
        def _():
            dgf_ref[...] = jnp.zeros_like(dgf_ref)
            loss_ref[...] = jnp.zeros_like(loss_ref)

        err = _rms_fwd(ho, gf_ref[...]) - t_ref[...]
        loss_ref[...] += (0.5 / D) * jnp.sum(err * err)
        dx, dg = _rms_bwd(ho, gf_ref[...], err * (1.0 / D))
        dh_ref[...] = dx
        do_ref[...] = (0.5 * dx).astype(BF16)
        dgf_ref[...] += dg

    tok = pl.BlockSpec((ts, D), lambda i: (i, 0))
    row = pl.BlockSpec((1, D), lambda i: (0, 0))
    wsp = _resident((F, D))
    hid = pl.BlockSpec((ts, F), lambda i: (i, 0))
    saved = [jax.ShapeDtypeStruct((S, D), BF16), jax.ShapeDtypeStruct((S, F), BF16), jax.ShapeDtypeStruct((S, F), BF16)]
    if head is None:
        return _call(name, (S // ts,), compute, [tok, row, wsp, wsp, wsp], [tok, tok, hid, hid],
                     [jax.ShapeDtypeStruct((S, D), F32)] + saved, [], 56, [h, g, wg, wu, wd], jobs)
    return _call(name, (S // ts,), compute, [tok, row, wsp, wsp, wsp, tok, row],
                 [tok, tok, row, pl.BlockSpec((8, 128), lambda i: (0, 0)), tok, hid, hid],
                 [jax.ShapeDtypeStruct((S, D), F32), jax.ShapeDtypeStruct((S, D), BF16), jax.ShapeDtypeStruct((1, D), F32),
                  jax.ShapeDtypeStruct((8, 128), F32)] + saved, [], 60, [h, g, wg, wu, wd, *head], jobs)


def _ffn_bwd_hidden(do, gp, up, wd, name, jobs=()):
    S, D = do.shape
    F = wd.shape[0]
    ts = _tile(S, 1024)
    fb = _tile(F, F // 2)
    def compute(ins, outs, scr):
        do_ref, gp_ref, up_ref, wd_ref = ins
        dgp_ref, dup_ref, a_ref = outs
        parts = _row_parts(ts, 4)
        das = [_dot_nt(do_ref[rows, :], wd_ref[...]) for rows in parts]
        for rows, da in zip(parts, das):
            gf = gp_ref[rows, :].astype(F32)
            uf = up_ref[rows, :].astype(F32)
            sg = _sigmoid(gf)
            si = gf * sg
            dgp_ref[rows, :] = (da * uf * (sg * (1.0 + gf * (1.0 - sg)))).astype(BF16)
            dup_ref[rows, :] = (da * si).astype(BF16)
            a_ref[rows, :] = (si * uf).astype(BF16)

    tok = pl.BlockSpec((ts, D), lambda s, i: (i, 0))
    hid = pl.BlockSpec((ts, fb), lambda s, i: (i, s))
    return _call(name, (F // fb, S // ts), compute, [tok, hid, hid, pl.BlockSpec((fb, D), lambda s, i: (s, 0))],
                 [hid, hid, hid], [jax.ShapeDtypeStruct((S, F), BF16)] * 3, [], 56, [do, gp, up, wd], jobs)


def _ffn_bwd_input(dh, h, g, dgp, dup, wg, wu, name, jobs=(), part=(0, 1), into=None):
    S, D = h.shape
    F = wg.shape[0]
    ts = _tile(S, 512)
    steps = S // ts // part[1]
    first = part[0] * steps

    def compute(ins, outs, scr):
        dh_ref, h_ref, g_ref, dgp_ref, dup_ref, wg_ref, wu_ref = ins[:7]
        dhi_ref, dg_ref = outs

        @pl.when(pl.program_id(0) == 0)
        def _():
            dg_ref[...] = jnp.zeros_like(dg_ref)

        dn = _dot(dgp_ref[...], wg_ref[...]) + _dot(dup_ref[...], wu_ref[...])
        dx, dg = _rms_bwd(h_ref[...], g_ref[...], dn)
        dhi_ref[...] = dh_ref[...] + dx
        dg_ref[...] += dg

    tok = pl.BlockSpec((ts, D), lambda i: (first + i, 0))
    hid = pl.BlockSpec((ts, F), lambda i: (first + i, 0))
    row = pl.BlockSpec((1, D), lambda i: (0, 0))
    in_specs = [tok, tok, row, hid, hid, _resident((F, D)), _resident((F, D))]
    args = [dh, h, g, dgp, dup, wg, wu]
    if into is not None:
        in_specs, args = in_specs + [IN_HBM], args + [into]
    return _call(name, (steps,), compute, in_specs, [tok, row],
                 [jax.ShapeDtypeStruct((S, D), F32), jax.ShapeDtypeStruct((1, D), F32)], [], 56, args, jobs,
                 own_aliases=None if into is None else {7: 0})


def _ffn_wgrad(hid, tok, name, jobs=()):
    S, D = tok.shape
    F = hid.shape[1]
    fb = _tile(F, F // 2)

    def compute(ins, outs, scr):
        outs[0][...] = _dot_tn(ins[0][...], ins[1][...])

    main, jres = _call(name, (F // fb,), compute,
                       [pl.BlockSpec((S, fb), lambda j: (0, j)), _resident(tok.shape)],
                       [pl.BlockSpec((fb, D), lambda j: (j, 0))], [jax.ShapeDtypeStruct((F, D), F32)], [], 56,
                       [hid, tok], jobs)
    return main[0], jres


def _w_in_pieces(D, cq, ng):
    groups = []
    for k in range(ng):
        lo, hi, pieces = k * D, (k + 1) * D, []
        while lo < hi:
            q = lo // cq
            w = min(hi, (q + 1) * cq) - lo
            pieces.append((q, lo - q * cq, w, lo - k * D))
            lo += w
        groups.append(pieces)
    return groups


def _mix_in_fwd(h, g, win, jobs=()):
    S, D = h.shape
    NG = win.shape[0] * win.shape[2] // D
    pieces = _w_in_pieces(D, win.shape[2], NG)
    ts = _tile(S, 512)

    def compute(ins, outs, scr):
        h_ref, g_ref, w_ref = ins
        u_ref, z_ref = outs
        u = _rms_fwd(h_ref[...], g_ref[...]).astype(BF16)
        u_ref[...] = u
        for k in range(NG):
            for q, c0, w, d0 in pieces[k]:
                z_ref[k, :, d0:d0 + w] = _dot(u, w_ref[q, :, c0:c0 + w]).astype(BF16)

    return _call("mix_in_fwd", (S // ts,), compute,
                 [pl.BlockSpec((ts, D), lambda i: (i, 0)), pl.BlockSpec((1, D), lambda i: (0, 0)), _resident(win.shape)],
                 [pl.BlockSpec((ts, D), lambda i: (i, 0)), pl.BlockSpec((NG, ts, D), lambda i: (0, i, 0))],
                 [jax.ShapeDtypeStruct((S, D), BF16), jax.ShapeDtypeStruct((NG, S, D), BF16)],
                 [], 48, [h, g, win], jobs)


SUBLANES = 8


def _shifted_copies(s):
    n = s.shape[1] - SUBLANES
    for r in range(1, SUBLANES):
        s[r, 0:n, :] = s[0, r:r + n, :]


def _window(s, o, rows):
    r = o % SUBLANES
    return s[r, o - r:o - r + rows, :]


def _conv_fwd(z, wa, ba, wb, jobs=()):
    _, S, D = z.shape
    _, KA, CB = wa.shape
    KB = wb.shape[1]
    ts = _tile(S, 1024)
    r = ts // HALO
    CH = min(64, ts)

    def compute(ins, outs, scr):
        z_ref, zh_ref, wa_ref, ba_ref, wb_ref = ins
        a1_ref, q_ref = outs
        sa, sb = scr
        keep = (pl.program_id(1) > 0).astype(F32)
        sa[0, HALO:HALO + ts, :] = z_ref[0].astype(F32) * _sigmoid(z_ref[1].astype(F32))
        sa[0, 0:HALO, :] = zh_ref[0].astype(F32) * _sigmoid(zh_ref[1].astype(F32)) * keep
        _shifted_copies(sa)
        sb[HALO:HALO + ts, :] = z_ref[3].astype(F32) * z_ref[4].astype(F32)
        sb[0:HALO, :] = zh_ref[3].astype(F32) * zh_ref[4].astype(F32) * keep
        wak = [wa_ref[0, k:k + 1, :] for k in range(KA)]
        wbk = [wb_ref[0, k:k + 1, :] for k in range(KB)]
        for c0 in range(0, ts, CH):
            acc = jnp.broadcast_to(ba_ref[...], (CH, CB))
            for k in range(KA):
                acc = acc + wak[k] * _window(sa, c0 + HALO - (KA - 1) + k, CH)
            a1_ref[c0:c0 + CH, :] = acc
            v = jnp.zeros((CH, CB), F32)
            for k in range(KB):
                o = c0 + HALO - (KB - 1) + k
                v = v + wbk[k] * sb[o:o + CH, :]
            q_ref[c0:c0 + CH, :] = (z_ref[2, c0:c0 + CH, :].astype(F32) * v).astype(BF16)

    return _call("conv_fwd", (D // CB, S // ts), compute,
                 [pl.BlockSpec((5, ts, CB), lambda j, i: (0, i, j)),
                  pl.BlockSpec((5, HALO, CB), lambda j, i: (0, jnp.maximum(i * r - 1, 0), j)),
                  pl.BlockSpec((1, KA, CB), lambda j, i: (j, 0, 0)), pl.BlockSpec((1, CB), lambda j, i: (0, j)),
                  pl.BlockSpec((1, KB, CB), lambda j, i: (j, 0, 0))],
                 [pl.BlockSpec((ts, CB), lambda j, i: (i, j)), pl.BlockSpec((ts, CB), lambda j, i: (i, j))],
                 [jax.ShapeDtypeStruct((S, D), F32), jax.ShapeDtypeStruct((S, D), BF16)],
                 [pltpu.VMEM((SUBLANES, HALO + ts, CB), F32), pltpu.VMEM((HALO + ts, CB), F32)], 40, [z, z, wa, ba, wb], jobs)


def _ln_stats(a1):
    mu = jnp.mean(a1, axis=-1, keepdims=True)
    xc = a1 - mu
    rstd = lax.rsqrt(jnp.mean(xc * xc, axis=-1, keepdims=True) + EPS)
    return xc * rstd, rstd


def _mix_out_fwd(h1, a1, q, z, lng, lnb, wa, wb, wo, jobs=()):
    S, D = h1.shape
    ts = _tile(S, 512)

    def compute(ins, outs, scr):
        h_ref, a1_ref, q_ref, ga_ref, gb_ref, lng_ref, lnb_ref, wa_ref, wb_ref, wo_ref = ins
        h2_ref, a3_ref, m_ref, ya_ref, yb_ref = outs
        xhat, _ = _ln_stats(a1_ref[...])
        a2 = xhat * lng_ref[...] + lnb_ref[...]
        a3 = (a2 * _sigmoid(a2)).astype(BF16)
        a3_ref[...] = a3
        ya = _dot(a3, wa_ref[...])
        yb = _dot(q_ref[...], wb_ref[...])
        ya_ref[...] = ya.astype(BF16)
        yb_ref[...] = yb.astype(BF16)
        m = (_sigmoid(ga_ref[0].astype(F32)) * ya + _sigmoid(gb_ref[0].astype(F32)) * yb).astype(BF16)
        m_ref[...] = m
        h2_ref[...] = h_ref[...] + _dot(m, wo_ref[...])

    tok = pl.BlockSpec((ts, D), lambda i: (i, 0))
    row = pl.BlockSpec((1, D), lambda i: (0, 0))
    mat = _resident((D, D))
    return _call("mix_out_fwd", (S // ts,), compute,
                 [tok, tok, tok, pl.BlockSpec((1, ts, D), lambda i: (5, i, 0)), pl.BlockSpec((1, ts, D), lambda i: (6, i, 0)),
                  row, row, mat, mat, mat], [tok] * 5,
                 [jax.ShapeDtypeStruct((S, D), F32)] + [jax.ShapeDtypeStruct((S, D), BF16)] * 4,
                 [], 56, [h1, a1, q, z, z, lng, lnb, wa, wb, wo], jobs)


def _mix_out_bwd(dh2, a1, z, ya, yb, lng, lnb, wa, wb, wo, jobs=()):
    S, D = dh2.shape
    ts = _tile(S, 512)

    def compute(ins, outs, scr):
        dh_ref, a1_ref, ga_ref, gb_ref, ya_ref, yb_ref, lng_ref, lnb_ref, wa_ref, wb_ref, wo_ref = ins
        da1_ref, dq_ref, dga_ref, dgb_ref, dya_ref, dyb_ref, dhb_ref, dlg_ref, dlb_ref = outs

        @pl.when(pl.program_id(0) == 0)
        def _():
            dlg_ref[...] = jnp.zeros_like(dlg_ref)
            dlb_ref[...] = jnp.zeros_like(dlb_ref)

        for rows in _row_parts(ts):
            dhb = dh_ref[rows, :].astype(BF16)
            dhb_ref[rows, :] = dhb
            dm = _dot_nt(dhb, wo_ref[...])
            sa = _sigmoid(ga_ref[0, rows, :].astype(F32))
            sb = _sigmoid(gb_ref[0, rows, :].astype(F32))
            dga_ref[rows, :] = (dm * ya_ref[rows, :].astype(F32) * sa * (1.0 - sa)).astype(BF16)
            dgb_ref[rows, :] = (dm * yb_ref[rows, :].astype(F32) * sb * (1.0 - sb)).astype(BF16)
            dya = (sa * dm).astype(BF16)
            dyb = (sb * dm).astype(BF16)
            dya_ref[rows, :] = dya
            dyb_ref[rows, :] = dyb
            dq_ref[rows, :] = _dot_nt(dyb, wb_ref[...]).astype(BF16)
            da3 = _dot_nt(dya, wa_ref[...])
            xhat, rstd = _ln_stats(a1_ref[rows, :])
            a2 = xhat * lng_ref[...] + lnb_ref[...]
            sg = _sigmoid(a2)
            da2 = da3 * (sg * (1.0 + a2 * (1.0 - sg)))
            dlg_ref[...] += jnp.sum(da2 * xhat, axis=0, keepdims=True)
            dlb_ref[...] += jnp.sum(da2, axis=0, keepdims=True)
            dxh = da2 * lng_ref[...]
            da1_ref[rows, :] = (rstd * (dxh - jnp.mean(dxh, axis=-1, keepdims=True)
                                        - xhat * jnp.mean(dxh * xhat, axis=-1, keepdims=True))).astype(BF16)

    tok = pl.BlockSpec((ts, D), lambda i: (i, 0))
    row = pl.BlockSpec((1, D), lambda i: (0, 0))
    mat = _resident((D, D))
    return _call("mix_out_bwd", (S // ts,), compute,
                 [tok, tok, pl.BlockSpec((1, ts, D), lambda i: (5, i, 0)), pl.BlockSpec((1, ts, D), lambda i: (6, i, 0)),
                  tok, tok, row, row, mat, mat, mat], [tok] * 7 + [row, row],
                 [jax.ShapeDtypeStruct((S, D), BF16)] * 7 + [jax.ShapeDtypeStruct((1, D), F32)] * 2,
                 [], 56, [dh2, a1, z, z, ya, yb, lng, lnb, wa, wb, wo], jobs)


def _mixer_wgrads(a3, dya, q, dyb, mm, dhb, jobs=()):
    S, D = a3.shape
    tk = _tile(S, 512)

    def compute(ins, outs, scr):
        @pl.when(pl.program_id(0) == 0)
        def _():
            for o in outs:
                o[...] = jnp.zeros_like(o)

        for t in range(3):
            outs[t][...] += _dot_tn(ins[2 * t][...], ins[2 * t + 1][...])

    tok = pl.BlockSpec((tk, D), lambda k: (k, 0))
    return _call("mixer_wgrads", (S // tk,), compute, [tok] * 6, [pl.BlockSpec((D, D), lambda k: (0, 0))] * 3,
                 [jax.ShapeDtypeStruct((D, D), F32)] * 3, [], 56, [a3, dya, q, dyb, mm, dhb], jobs)


def _conv_bwd(z, da1, dq, dga, dgb, wa, wb, jobs=()):
    NG, S, D = z.shape
    _, KA, CB = wa.shape
    KB = wb.shape[1]
    ts = _tile(S, 1024)
    r = ts // HALO
    nt = S // ts
    CH = min(64, ts)
    last_halo = S // HALO - 1

    def compute(ins, outs, scr):
        z_ref, zp_ref, zn_ref, da1_ref, da1n_ref, dq_ref, dqn_ref, dga_ref, dgb_ref, wa_ref, wb_ref = ins
        dz_ref, dwa_ref, dba_ref, dwb_ref = outs
        sa0, sd, sp, sv, acca, accb = scr
        i = pl.program_id(1)
        prev = (i > 0).astype(F32)
        nxt = (i < nt - 1).astype(F32)

        @pl.when(i == 0)
        def _():
            acca[...] = jnp.zeros_like(acca)
            accb[...] = jnp.zeros_like(accb)
            dba_ref[...] = jnp.zeros_like(dba_ref)

        sa0[0, HALO:HALO + ts, :] = z_ref[0].astype(F32) * _sigmoid(z_ref[1].astype(F32))
        sa0[0, 0:HALO, :] = zp_ref[0].astype(F32) * _sigmoid(zp_ref[1].astype(F32)) * prev
        _shifted_copies(sa0)
        sp[HALO:HALO + ts, :] = z_ref[3].astype(F32) * z_ref[4].astype(F32)
        sp[0:HALO, :] = zp_ref[3].astype(F32) * zp_ref[4].astype(F32) * prev
        sd[0, 0:ts, :] = da1_ref[...].astype(F32)
        sd[0, ts:ts + HALO, :] = da1n_ref[...].astype(F32) * nxt
        _shifted_copies(sd)
        sv[0:ts, :] = dq_ref[...].astype(F32) * z_ref[2].astype(F32)
        sv[ts:ts + HALO, :] = dqn_ref[...].astype(F32) * zn_ref[2].astype(F32) * nxt
        dba_ref[...] += jnp.sum(sd[0, 0:ts, :], axis=0, keepdims=True)
        wak = [wa_ref[0, k:k + 1, :] for k in range(KA)]
        wbk = [wb_ref[0, k:k + 1, :] for k in range(KB)]
        for c0 in range(0, ts, CH):
            rows = slice(c0, c0 + CH)
            d1 = sd[0, rows, :]
            da0 = jnp.zeros((CH, CB), F32)
            for k in range(KA):
                da0 = da0 + wak[k] * _window(sd, c0 + (KA - 1) - k, CH)
                a0w = _window(sa0, c0 + HALO - (KA - 1) + k, CH)
                acca[k] += jnp.sum((d1 * a0w).reshape(CH // 8, 8, CB), axis=0)
            val = z_ref[0, rows, :].astype(F32)
            sg = _sigmoid(z_ref[1, rows, :].astype(F32))
            dz_ref[0, rows, :] = (da0 * sg).astype(BF16)
            dz_ref[1, rows, :] = (da0 * val * sg * (1.0 - sg)).astype(BF16)
            dv = sv[rows, :]
            v = jnp.zeros((CH, CB), F32)
            dp = jnp.zeros((CH, CB), F32)
            for k in range(KB):
                o = c0 + HALO - (KB - 1) + k
                pw = sp[o:o + CH, :]
                v = v + wbk[k] * pw
                accb[k] += jnp.sum((dv * pw).reshape(CH // 8, 8, CB), axis=0)
                o = c0 + (KB - 1) - k
                dp = dp + wbk[k] * sv[o:o + CH, :]
            dz_ref[2, rows, :] = (dq_ref[rows, :].astype(F32) * v).astype(BF16)
            dz_ref[3, rows, :] = (dp * z_ref[4, rows, :].astype(F32)).astype(BF16)
            dz_ref[4, rows, :] = (dp * z_ref[3, rows, :].astype(F32)).astype(BF16)
        dz_ref[5] = dga_ref[...]
        dz_ref[6] = dgb_ref[...]

        @pl.when(i == nt - 1)
        def _():
            dwa_ref[0] = jnp.sum(acca[...], axis=1)
            dwb_ref[0] = jnp.sum(accb[...], axis=1)

    zt = pl.BlockSpec((5, ts, CB), lambda j, i: (0, i, j))
    zp = pl.BlockSpec((5, HALO, CB), lambda j, i: (0, jnp.maximum(i * r - 1, 0), j))
    zn = pl.BlockSpec((5, HALO, CB), lambda j, i: (0, jnp.minimum((i + 1) * r, last_halo), j))
    tok = pl.BlockSpec((ts, CB), lambda j, i: (i, j))
    tokn = pl.BlockSpec((HALO, CB), lambda j, i: (jnp.minimum((i + 1) * r, last_halo), j))
    return _call("conv_bwd", (D // CB, nt), compute,
                 [zt, zp, zn, tok, tokn, tok, tokn, tok, tok,
                  pl.BlockSpec((1, KA, CB), lambda j, i: (j, 0, 0)), pl.BlockSpec((1, KB, CB), lambda j, i: (j, 0, 0))],
                 [pl.BlockSpec((NG, ts, CB), lambda j, i: (0, i, j)), pl.BlockSpec((1, KA, CB), lambda j, i: (j, 0, 0)),
                  pl.BlockSpec((1, CB), lambda j, i: (0, j)), pl.BlockSpec((1, KB, CB), lambda j, i: (j, 0, 0))],
                 [jax.ShapeDtypeStruct((NG, S, D), BF16), jax.ShapeDtypeStruct((D // CB, KA, CB), F32),
                  jax.ShapeDtypeStruct((1, D), F32), jax.ShapeDtypeStruct((D // CB, KB, CB), F32)],
                 [pltpu.VMEM((SUBLANES, HALO + ts, CB), F32), pltpu.VMEM((SUBLANES, ts + HALO, CB), F32),
                  pltpu.VMEM((HALO + ts, CB), F32), pltpu.VMEM((ts + HALO, CB), F32),
                  pltpu.VMEM((KA, 8, CB), F32), pltpu.VMEM((KB, 8, CB), F32)],
                 48, [z, z, z, da1, da1, dq, dq, dga, dgb, wa, wb], jobs)


def _mix_in_bwd(dh2, h1, g, dz, win, jobs=()):
    S, D = h1.shape
    NG = dz.shape[0]
    pieces = _w_in_pieces(D, win.shape[2], NG)
    ts = _tile(S, 512)

    def compute(ins, outs, scr):
        dh_ref, h_ref, g_ref, dz_ref, w_ref = ins
        dhi_ref, dg_ref, do_ref = outs

        @pl.when(pl.program_id(0) == 0)
        def _():
            dg_ref[...] = jnp.zeros_like(dg_ref)

        du = None
        for k in range(NG):
            for q, c0, w, d0 in pieces[k]:
                part = _dot_nt(dz_ref[k, :, d0:d0 + w], w_ref[q, :, c0:c0 + w])
                du = part if du is None else du + part
        dx, dg = _rms_bwd(h_ref[...], g_ref[...], du)
        dhi = dh_ref[...] + dx
        dhi_ref[...] = dhi
        do_ref[...] = (0.5 * dhi).astype(BF16)
        dg_ref[...] += dg

    tok = pl.BlockSpec((ts, D), lambda i: (i, 0))
    row = pl.BlockSpec((1, D), lambda i: (0, 0))
    return _call("mix_in_bwd", (S // ts,), compute,
                 [tok, tok, row, pl.BlockSpec((NG, ts, D), lambda i: (0, i, 0)), _resident(win.shape)],
                 [tok, row, tok],
                 [jax.ShapeDtypeStruct((S, D), F32), jax.ShapeDtypeStruct((1, D), F32), jax.ShapeDtypeStruct((S, D), BF16)],
                 [], 56, [dh2, h1, g, dz, win], jobs)


def _w_in_grad(u, dz, jobs=()):
    S, D = u.shape
    NG = dz.shape[0]

    def compute(ins, outs, scr):
        outs[0][...] = _dot_tn(ins[0][...], ins[1][0])

    return _call("w_in_grad", (NG,), compute,
                 [_resident(u.shape), pl.BlockSpec((1, S, D), lambda j: (j, 0, 0))],
                 [pl.BlockSpec((D, D), lambda j: (0, j))], [jax.ShapeDtypeStruct((D, NG * D), F32)], [], 48, [u, dz], jobs)


def _chip_sums(place, grads, got, kind, name):
    n = len(grads)
    qr, qc = _quarter_shape(grads[0].shape, kind)
    h = qr // 2
    tr = _row_tile(h)
    nr = h // tr

    def body(pc_ref, *refs):
        g_refs, got_refs, b_refs, f_refs = refs[:n], refs[n:2 * n], refs[2 * n:3 * n], refs[3 * n:]
        own = pl.program_id(1) == pc_ref[0]
        for a in range(n):
            s = g_refs[a][...] + got_refs[a][0]
            b_refs[a][0] = s.astype(BF16)

            @pl.when(own)
            def _():
                f_refs[a][...] = s

    if kind == "rows":
        gspec = pl.BlockSpec((tr, qc), lambda r, q, pc: (q * (2 * nr) + pc[1] * nr + r, 0))
    else:
        gspec = pl.BlockSpec((tr, qc), lambda r, q, pc: (pc[1] * nr + r, q))
    lspec = pl.BlockSpec((1, tr, qc), lambda r, q, pc: (q, r, 0))
    res = pl.pallas_call(
        body, name=name,
        grid_spec=pltpu.PrefetchScalarGridSpec(
            num_scalar_prefetch=1, grid=(nr, NS), in_specs=[gspec] * n + [lspec] * n,
            out_specs=[lspec] * n + [pl.BlockSpec((tr, qc), lambda r, q, pc: (r, 0))] * n),
        out_shape=[jax.ShapeDtypeStruct((NS, h, qc), BF16)] * n + [jax.ShapeDtypeStruct((h, qc), F32)] * n,
        compiler_params=_cparams(2, 48),
    )(place, *grads, *got)
    return res[:n], res[n:]


def _totals(place, own, got, name):
    n = len(own)
    h, qc = own[0].shape
    tr = _row_tile(h)
    nr = h // tr
    got = [list(g) if isinstance(g, (list, tuple)) else [g] for g in got]
    m = len(got[0])

    def body(pc_ref, *refs):
        own_refs, got_refs, o_refs = refs[:n], refs[n:n + n * m], refs[n + n * m:]
        for a in range(n):
            acc = own_refs[a][...]
            for g in got_refs[a * m:(a + 1) * m]:
                for k in range(g.shape[0]):
                    acc = acc + g[k].astype(F32)
            o_refs[a][...] = acc

    lands = [pl.BlockSpec((g.shape[0], tr, qc), lambda r, pc: (0, r, 0)) for gs in got for g in gs]
    return pl.pallas_call(
        body, name=name,
        grid_spec=pltpu.PrefetchScalarGridSpec(
            num_scalar_prefetch=1, grid=(nr,),
            in_specs=[pl.BlockSpec((tr, qc), lambda r, pc: (r, 0))] * n + lands,
            out_specs=[pl.BlockSpec((tr, qc), lambda r, pc: (pc[1] * nr + r, 0))] * n),
        out_shape=[jax.ShapeDtypeStruct((2 * h, qc), F32)] * n,
        compiler_params=_cparams(1, 48),
    )(place, *own, *[g for gs in got for g in gs])


def _adamw(ws, gs, ms, vs, name):
    n = len(ws)
    R, C = ws[0].shape
    tr = _row_tile(R, (36 << 20) // (7 * 2 * 4 * n * C))

    def body(*refs):
        w_refs, g_refs, m_refs, v_refs = refs[:n], refs[n:2 * n], refs[2 * n:3 * n], refs[3 * n:4 * n]
        d_refs, mo_refs, vo_refs = refs[4 * n:5 * n], refs[5 * n:6 * n], refs[6 * n:]
        for a in range(n):
            d_refs[a][...], mo_refs[a][...], vo_refs[a][...] = _adamw_math(w_refs[a][...], g_refs[a][...], m_refs[a][...],
                                                                         v_refs[a][...])

    blk = pl.BlockSpec((tr, C), lambda r: (r, 0))
    res = pl.pallas_call(
        body, name=name, grid=(R // tr,),
        in_specs=[blk] * (4 * n), out_specs=[blk] * (3 * n),
        out_shape=[jax.ShapeDtypeStruct((R, C), F32)] * (3 * n),
        compiler_params=_cparams(1, 56),
    )(*ws, *gs, *ms, *vs)
    return res[:n], res[n:2 * n], res[2 * n:]


def kernel(x, ffn1_norm, ffn1_w_gate, ffn1_w_up, ffn1_w_down, mix_norm, w_in, a_dw_w, a_dw_b, a_ln_g, a_ln_b, a_w_out, b_conv_w, b_w_out, w_o, ffn2_norm, ffn2_w_gate, ffn2_w_up, ffn2_w_down, final_norm, loss_target, m_ffn1_norm, m_ffn1_w_gate, m_ffn1_w_up, m_ffn1_w_down, m_mix_norm, m_w_in, m_a_dw_w, m_a_dw_b, m_a_ln_g, m_a_ln_b, m_a_w_out, m_b_conv_w, m_b_w_out, m_w_o, m_ffn2_norm, m_ffn2_w_gate, m_ffn2_w_up, m_ffn2_w_down, m_final_norm, v_ffn1_norm, v_ffn1_w_gate, v_ffn1_w_up, v_ffn1_w_down, v_mix_norm, v_w_in, v_a_dw_w, v_a_dw_b, v_a_ln_g, v_a_ln_b, v_a_w_out, v_b_conv_w, v_b_w_out, v_w_o, v_ffn2_norm, v_ffn2_w_gate, v_ffn2_w_up, v_ffn2_w_down, v_final_norm):
    names = ["ffn1_norm", "ffn1_w_gate", "ffn1_w_up", "ffn1_w_down", "mix_norm", "w_in", "a_dw_w", "a_dw_b", "a_ln_g",
             "a_ln_b", "a_w_out", "b_conv_w", "b_w_out", "w_o", "ffn2_norm", "ffn2_w_gate", "ffn2_w_up", "ffn2_w_down",
             "final_norm"]
    W = dict(zip(names, [ffn1_norm, ffn1_w_gate, ffn1_w_up, ffn1_w_down, mix_norm, w_in, a_dw_w, a_dw_b, a_ln_g, a_ln_b,
                         a_w_out, b_conv_w, b_w_out, w_o, ffn2_norm, ffn2_w_gate, ffn2_w_up, ffn2_w_down, final_norm]))
    M = dict(zip(names, [m_ffn1_norm, m_ffn1_w_gate, m_ffn1_w_up, m_ffn1_w_down, m_mix_norm, m_w_in, m_a_dw_w, m_a_dw_b,
                         m_a_ln_g, m_a_ln_b, m_a_w_out, m_b_conv_w, m_b_w_out, m_w_o, m_ffn2_norm, m_ffn2_w_gate,
                         m_ffn2_w_up, m_ffn2_w_down, m_final_norm]))
    V = dict(zip(names, [v_ffn1_norm, v_ffn1_w_gate, v_ffn1_w_up, v_ffn1_w_down, v_mix_norm, v_w_in, v_a_dw_w, v_a_dw_b,
                         v_a_ln_g, v_a_ln_b, v_a_w_out, v_b_conv_w, v_b_w_out, v_w_o, v_ffn2_norm, v_ffn2_w_gate,
                         v_ffn2_w_up, v_ffn2_w_down, v_final_norm]))
    transposed = ("ffn1_w_gate", "ffn1_w_up", "ffn2_w_gate", "ffn2_w_up")
    vecs = ["ffn1_norm", "mix_norm", "a_dw_b", "a_ln_g", "a_ln_b", "ffn2_norm", "final_norm"]
    ffn1 = ["ffn1_w_gate", "ffn1_w_up", "ffn1_w_down"]
    ffn2 = ["ffn2_w_gate", "ffn2_w_up", "ffn2_w_down"]
    outp = ["a_w_out", "b_w_out", "w_o"]

    S, D = x.shape[1], x.shape[2]
    CB = D // NS
    KA, KB = a_dw_w.shape[1], b_conv_w.shape[1]
    px, py, pc = lax.axis_index("x"), lax.axis_index("y"), lax.axis_index("c")
    chip = 2 * px + py
    place = jnp.stack([chip, pc]).astype(jnp.int32)
    h0 = x.reshape(S, D)
    tgt = loss_target.reshape(S, D)
    row = lambda n: pltpu.with_memory_space_constraint(W[n].reshape(1, D), pltpu.HBM)
    pad = lambda a, r: jnp.concatenate([a, jnp.zeros((r - a.shape[0], a.shape[1]), F32)], axis=0)

    def quarter(P, n):
        return jnp.transpose(P[n][0]) if n in transposed else P[n][0]

    def unquarter(a, n):
        return (jnp.transpose(a) if n in transposed else a).reshape(W[n].shape)

    wq = {n: quarter(W, n).astype(BF16) for n in ffn1 + ffn2 + outp + ["w_in"]}

    f1 = _exchange("gather_ffn1", [_Gather([wq[n] for n in ffn1], ["rows"] * 3)])[0]
    g_in = _Gather([wq["w_in"], pad(a_dw_w[0], 32), pad(b_conv_w[0], 16)], ["rows"] * 3)
    (h1, n1, gp1, up1), ((win, taps_a, taps_b),) = _ffn_fwd(h0, row("ffn1_norm"), *f1, "ffn1_fwd", [g_in])
    win = win.reshape(NS, D, -1)
    wa_taps = taps_a.reshape(NS, 32, CB)[:, :KA]
    wb_taps = taps_b.reshape(NS, 16, CB)[:, :KB]
    g_out = _Gather([wq[n] for n in outp], ["rows"] * 3)
    (u, z), ((wa_out, wb_out, wo),) = _mix_in_fwd(h1, row("mix_norm"), win, [g_out])
    g_f2 = _Gather([wq["ffn2_w_gate"], wq["ffn2_w_up"]], ["rows"] * 2)
    (a1, q), ((f2g, f2u),) = _conv_fwd(z, wa_taps, row("a_dw_b"), wb_taps, [g_f2])
    (h2, a3, mm, ya, yb), ((f2d,),) = _mix_out_fwd(h1, a1, q, z, row("a_ln_g"), row("a_ln_b"), wa_out, wb_out, wo,
                                                   [_Gather([wq["ffn2_w_down"]], ["rows"])])
    (dh3, do2, d_final, loss_part, n2, gp2, up2), _ = _ffn_fwd(h2, row("ffn2_norm"), f2g, f2u, f2d, "ffn2_fwd_loss",
                                                               head=(tgt, row("final_norm")))

    (dgp2, dup2, act2), _ = _ffn_bwd_hidden(do2, gp2, up2, f2d, "ffn2_bwd_hidden")
    (dh2, d_ffn2), _ = _ffn_bwd_input(dh3, h2, row("ffn2_norm"), dgp2, dup2, f2g, f2u, "ffn2_bwd_input")
    g2 = [_ffn_wgrad(dgp2, n2, "ffn2_dwg")[0], _ffn_wgrad(dup2, n2, "ffn2_dwu")[0], _ffn_wgrad(act2, do2, "ffn2_dwd")[0]]
    (da1, dq, dga, dgb, dya, dyb, dh2b, d_lng, d_lnb), (got,) = _mix_out_bwd(
        dh2, a1, z, ya, yb, row("a_ln_g"), row("a_ln_b"), wa_out, wb_out, wo, [_ToSibling(g2, ["rows"] * 3)])
    wire2, own2 = _chip_sums(place, g2, got, "rows", "ffn2_chip_sums")
    (dz, d_wa, d_ba, d_wb), (got,) = _conv_bwd(z, da1, dq, dga, dgb, wa_taps, wb_taps, [_ToChips(wire2)])
    half2 = _totals(place, own2, got, "ffn2_totals")
    go, (tot2,) = _mixer_wgrads(a3, dya, q, dyb, mm, dh2b, [_SwapHalves(half2)])
    (g_win,), (got_o,) = _w_in_grad(u, dz, [_ToSibling(go, ["rows"] * 3)])
    wire_o, own_o = _chip_sums(place, go, got_o, "rows", "mixer_chip_sums")
    (dh1, d_mix, do1), (got, land_o) = _mix_in_bwd(dh2, h1, row("mix_norm"), dz, win,
                                                   [_ToSibling([g_win], ["cols"]), _ToChips(wire_o)])
    wire_in, own_in = _chip_sums(place, [g_win], got, "cols", "w_in_chip_sum")
    half_o = _totals(place, own_o, land_o, "mixer_totals")
    (dgp1, dup1, act1), (near_in, tot_o) = _ffn_bwd_hidden(do1, gp1, up1, f1[2], "ffn1_bwd_hidden",
                                                           [_ToChips(wire_in, (0, 1)), _SwapHalves(half_o)])
    g1g, (far_in,) = _ffn_wgrad(dgp1, n1, "ffn1_dwg", [_ToChips(wire_in, (2,))])
    half_in = _totals(place, own_in, [[near_in[0], far_in[0]]], "w_in_total")
    g1u, (tot_in, got_g) = _ffn_wgrad(dup1, n1, "ffn1_dwu", [_SwapHalves(half_in), _ToSibling([g1g], ["rows"])])
    wire_g, own_g = _chip_sums(place, [g1g], got_g, "rows", "ffn1_dwg_chip_sum")
    g1d, (got_u, land_g) = _ffn_wgrad(act1, do1, "ffn1_dwd", [_ToSibling([g1u], ["rows"]), _ToChips(wire_g)])
    wire_u, own_u = _chip_sums(place, [g1u], got_u, "rows", "ffn1_dwu_chip_sum")
    half_g = _totals(place, own_g, land_g, "ffn1_dwg_total")
    ffn1_in = (dh1, h0, row("ffn1_norm"), dgp1, dup1, f1[0], f1[1])
    (dx, dg_a), (got_d, land_u, tot_g) = _ffn_bwd_input(
        *ffn1_in, "ffn1_bwd_input_a", [_ToSibling([g1d], ["rows"]), _ToChips(wire_u), _SwapHalves(half_g)], part=(0, 2))
    wire_d, own_d = _chip_sums(place, [g1d], got_d, "rows", "ffn1_dwd_chip_sum")
    half_u = _totals(place, own_u, land_u, "ffn1_dwu_total")
    (dx, dg_b), (land_d, tot_u) = _ffn_bwd_input(*ffn1_in, "ffn1_bwd_input_b", [_ToChips(wire_d), _SwapHalves(half_u)],
                                                 part=(1, 2), into=dx)
    half_d = _totals(place, own_d, land_d, "ffn1_dwd_total")
    (tot_d,) = _exchange("tail_exchange", [_SwapHalves(half_d)])
    tot1 = [tot_g[0], tot_u[0], tot_d[0]]
    totals = dict(zip(ffn2 + ["w_in"] + ffn1 + outp, list(tot2) + list(tot_in) + tot1 + list(tot_o)))

    vec_grads = {"ffn1_norm": [dg_a, dg_b], "mix_norm": [d_mix], "a_dw_b": [d_ba], "a_ln_g": [d_lng], "a_ln_b": [d_lnb],
                 "ffn2_norm": [d_ffn2], "final_norm": [d_final]}
    small = _allreduce_small([vec_grads[n] for n in vecs], d_wa, d_wb, loss_part)
    loss = small[LOSS_ROW, 0]
    taps = ["a_dw_w", "b_conv_w"]
    small_out = _small_adamw(place, small, [[P[n].reshape(1, D) for P in (W, M, V)] for n in vecs],
                             [[P[n] for P in (W, M, V)] for n in taps])

    grads, deltas, new_m, new_v = {}, {}, {}, {}
    for n, (g_, d_, m_, v_) in zip(vecs + taps, small_out):
        shp = W[n].shape
        grads[n], deltas[n], new_m[n], new_v[n] = g_.reshape(shp), d_.reshape(shp), m_.reshape(shp), v_.reshape(shp)
    for group, tag in ((ffn1 + ffn2, "ffn"), (["w_in"], "w_in"), (outp, "mixer")):
        ds, ms, vs = _adamw([quarter(W, n) for n in group], [totals[n] for n in group], [quarter(M, n) for n in group],
                            [quarter(V, n) for n in group], tag + "_adamw")
        for n, d_, m_, v_ in zip(group, ds, ms, vs):
            grads[n], deltas[n], new_m[n], new_v[n] = (unquarter(totals[n], n), unquarter(d_, n), unquarter(m_, n),
                                                       unquarter(v_, n))
    return (loss, dx.reshape(x.shape), *[grads[n] for n in names], *[deltas[n] for n in names],
            *[new_m[n] for n in names], *[new_v[n] for n in names])
```

```python
import functools

import jax
import jax.numpy as jnp
from jax import lax
from jax.experimental import pallas as pl
from jax.experimental.pallas import tpu as pltpu

F32 = jnp.float32
BF16 = jnp.bfloat16
EPS = 1e-6
NS = 4
HALO = 32
MESH = pl.DeviceIdType.MESH
IN_HBM = pl.BlockSpec(memory_space=pltpu.HBM)

ADAM_LR = 0.001
ADAM_B1 = 0.9
ADAM_B2 = 0.999
ADAM_EPS = 1e-08
ADAM_WD = 0.01
ADAM_STEP = 10


def _cparams(n_axes, vmem_mb):
    return pltpu.CompilerParams(dimension_semantics=("arbitrary",) * n_axes, vmem_limit_bytes=vmem_mb << 20)


def _tile(n, t):
    return t if n % t == 0 else n


def _row_parts(rows, n=2):
    if rows % (16 * n):
        return [slice(0, rows)]
    return [slice(p * (rows // n), (p + 1) * (rows // n)) for p in range(n)]


def _resident(shape):
    return pl.BlockSpec(shape, lambda *_: (0,) * len(shape), pipeline_mode=pl.Buffered(1))


def _row_tile(n, cap=256):
    for t in (256, 176, 128, 64, 32, 16, 8):
        if t <= cap and n % t == 0:
            return t
    return n


def _dot(a, b):
    return jnp.dot(a, b, preferred_element_type=F32)


def _dot_nt(a, b):
    return lax.dot_general(a, b, (((1,), (1,)), ((), ())), preferred_element_type=F32)


def _dot_tn(a, b):
    return lax.dot_general(a, b, (((0,), (0,)), ((), ())), preferred_element_type=F32)


def _sigmoid(x):
    return jax.nn.sigmoid(x)


def _rms_fwd(x, g):
    r = lax.rsqrt(jnp.mean(x * x, axis=-1, keepdims=True) + EPS)
    return x * r * g


def _rms_bwd(x, g, dn):
    r = lax.rsqrt(jnp.mean(x * x, axis=-1, keepdims=True) + EPS)
    xr = x * r
    dg = jnp.sum(dn * xr, axis=0, keepdims=True)
    w = dn * g
    dx = r * w - xr * (r * r) * jnp.mean(x * w, axis=-1, keepdims=True)
    return dx, dg


def _place():
    x, y, c = lax.axis_index("x"), lax.axis_index("y"), lax.axis_index("c")
    chips = [(1 - x, y), (x, 1 - y), (1 - x, 1 - y)]
    return x, y, c, chips


def _quarter_shape(full_shape, kind):
    r, c = full_shape
    return (r // NS, c) if kind == "rows" else (r, c // NS)


def _half_of_quarter(ref, kind, q, pc):
    qr, qc = _quarter_shape(ref.shape, kind)
    h = qr // 2
    if kind == "rows":
        return ref.at[pl.ds(q * qr + pc * h, h), :]
    return ref.at[pl.ds(pc * h, h), pl.ds(q * qc, qc)]


def _quarter(ref, kind, q):
    qr, qc = _quarter_shape(ref.shape, kind)
    if kind == "rows":
        return ref.at[pl.ds(q * qr, qr), :]
    return ref.at[:, pl.ds(q * qc, qc)]


def _rows_half(ref, pc):
    h = ref.shape[0] // 2
    return ref.at[pl.ds(pc * h, h)]


class _Gather:
    def __init__(self, quarters, kinds):
        self.ins = list(quarters)
        self.kinds = list(kinds)
        n = len(self.ins)
        self.out_shape = [jax.ShapeDtypeStruct((NS * a.shape[0], a.shape[1]) if k == "rows" else (a.shape[0], NS * a.shape[1]),
                                               a.dtype) for a, k in zip(self.ins, self.kinds)]
        self.scratch = [pltpu.SemaphoreType.DMA((n, 6)), pltpu.SemaphoreType.DMA((n, 6)), pltpu.SemaphoreType.DMA((n,))]
        self.aliases = {}

    def _copy(self, outs, sems, a, k, q, pc, to, src=None):
        dst = _half_of_quarter(outs[a], self.kinds[a], q, pc)
        return pltpu.make_async_remote_copy(src_ref=dst if src is None else src, dst_ref=dst,
                                            send_sem=sems[0].at[a, k], recv_sem=sems[1].at[a, k],
                                            device_id=to, device_id_type=MESH)

    def _mine(self, ins, outs, sems, a, p):
        return pltpu.make_async_copy(ins[a], _quarter(outs[a], self.kinds[a], p), sems[2].at[a])

    def start(self, ins, outs, sems):
        x, y, c, chips = _place()
        p = 2 * x + y
        for a in range(len(ins)):
            self._mine(ins, outs, sems, a, p).start()
            for j, chip in enumerate(chips):
                self._copy(outs, sems, a, j, p, c, (*chip, c), src=_rows_half(ins[a], c)).start()

    def relay(self, ins, outs, sems):
        x, y, c, chips = _place()
        sibling = (x, y, 1 - c)
        for a in range(len(ins)):
            for j, (qx, qy) in enumerate(chips):
                q = 2 * qx + qy
                self._copy(outs, sems, a, j, q, c, sibling).wait_recv()
                self._copy(outs, sems, a, 3 + j, q, c, sibling).start()

    def finish(self, ins, outs, sems):
        x, y, c, chips = _place()
        p = 2 * x + y
        sibling = (x, y, 1 - c)
        n = len(ins)
        for a in range(n):
            for j, (qx, qy) in enumerate(chips):
                q = 2 * qx + qy
                self._copy(outs, sems, a, 3 + j, q, 1 - c, sibling).wait_recv()
                self._copy(outs, sems, a, j, p, c, (qx, qy, c), src=_rows_half(ins[a], c)).wait_send()
                self._copy(outs, sems, a, 3 + j, q, c, sibling).wait_send()
            self._mine(ins, outs, sems, a, p).wait()


class _ToSibling:
    def __init__(self, grads, kinds):
        self.ins = list(grads)
        self.kinds = list(kinds)
        n = len(self.ins)
        self.out_shape = []
        for g, k in zip(self.ins, self.kinds):
            qr, qc = _quarter_shape(g.shape, k)
            self.out_shape.append(jax.ShapeDtypeStruct((NS, qr // 2, qc), g.dtype))
        self.scratch = [pltpu.SemaphoreType.DMA((n, NS)), pltpu.SemaphoreType.DMA((n, NS))]
        self.aliases = {}

    def _copies(self, ins, outs, sems):
        x, y, c, _ = _place()
        return [pltpu.make_async_remote_copy(src_ref=_half_of_quarter(ins[a], self.kinds[a], q, 1 - c), dst_ref=outs[a].at[q],
                                             send_sem=sems[0].at[a, q], recv_sem=sems[1].at[a, q],
                                             device_id=(x, y, 1 - c), device_id_type=MESH)
                for a in range(len(ins)) for q in range(NS)]

    def start(self, ins, outs, sems):
        for cp in self._copies(ins, outs, sems):
            cp.start()

    def finish(self, ins, outs, sems):
        for cp in self._copies(ins, outs, sems):
            cp.wait()


class _ToChips:
    def __init__(self, sums, which=(0, 1, 2)):
        self.ins = list(sums)
        self.which = tuple(which)
        n, m = len(self.ins), len(self.which)
        self.out_shape = [jax.ShapeDtypeStruct((m,) + s.shape[1:], s.dtype) for s in self.ins]
        self.scratch = [pltpu.SemaphoreType.DMA((n, m)), pltpu.SemaphoreType.DMA((n, m))]
        self.aliases = {}

    def _copies(self, ins, outs, sems):
        x, y, c, chips = _place()
        return [pltpu.make_async_remote_copy(src_ref=ins[a].at[2 * chips[j][0] + chips[j][1]], dst_ref=outs[a].at[k],
                                             send_sem=sems[0].at[a, k], recv_sem=sems[1].at[a, k],
                                             device_id=(*chips[j], c), device_id_type=MESH)
                for a in range(len(ins)) for k, j in enumerate(self.which)]

    def start(self, ins, outs, sems):
        for cp in self._copies(ins, outs, sems):
            cp.start()

    def finish(self, ins, outs, sems):
        for cp in self._copies(ins, outs, sems):
            cp.wait()


class _SwapHalves:
    def __init__(self, quarters):
        self.ins = list(quarters)
        n = len(self.ins)
        self.out_shape = [jax.ShapeDtypeStruct(g.shape, g.dtype) for g in self.ins]
        self.scratch = [pltpu.SemaphoreType.DMA((n,)), pltpu.SemaphoreType.DMA((n,))]
        self.aliases = {a: a for a in range(n)}

    def _copy(self, outs, sems, a, pc):
        x, y, c, _ = _place()
        rows = _rows_half(outs[a], pc)
        return pltpu.make_async_remote_copy(src_ref=rows, dst_ref=rows, send_sem=sems[0].at[a], recv_sem=sems[1].at[a],
                                            device_id=(x, y, 1 - c), device_id_type=MESH)

    def start(self, ins, outs, sems):
        c = lax.axis_index("c")
        for a in range(len(outs)):
            self._copy(outs, sems, a, c).start()

    def finish(self, ins, outs, sems):
        c = lax.axis_index("c")
        for a in range(len(outs)):
            self._copy(outs, sems, a, c).wait_send()
            self._copy(outs, sems, a, 1 - c).wait_recv()


def _call(name, grid, compute, in_specs, out_specs, out_shape, scratch, vmem_mb, args, jobs=(), own_aliases=None):
    n_in, n_out, n_scr = len(in_specs), len(out_specs), len(scratch)
    ji = [len(j.ins) for j in jobs]
    jo = [len(j.out_shape) for j in jobs]
    js = [len(j.scratch) for j in jobs]

    def body(*refs):
        pos = [0]

        def take(k):
            r = refs[pos[0]:pos[0] + k]
            pos[0] += k
            return r

        ins, jins = take(n_in), [take(k) for k in ji]
        outs, jouts = take(n_out), [take(k) for k in jo]
        scr, jscr = take(n_scr), [take(k) for k in js]
        if jobs and grid:
            ids = [pl.program_id(a) for a in range(len(grid))]
            first = functools.reduce(jnp.logical_and, [i == 0 for i in ids])
            last = functools.reduce(jnp.logical_and, [i == g - 1 for i, g in zip(ids, grid)])

            @pl.when(first)
            def _():
                for j, a, b, c in zip(jobs, jins, jouts, jscr):
                    j.start(a, b, c)

            @pl.when(last)
            def _():
                for j, a, b, c in zip(jobs, jins, jouts, jscr):
                    if hasattr(j, "relay"):
                        j.relay(a, b, c)
        elif jobs:
            for j, a, b, c in zip(jobs, jins, jouts, jscr):
                j.start(a, b, c)
            for j, a, b, c in zip(jobs, jins, jouts, jscr):
                if hasattr(j, "relay"):
                    j.relay(a, b, c)
        compute(ins, outs, scr)
        if jobs and grid:
            @pl.when(last)
            def _():
                for j, a, b, c in zip(jobs, jins, jouts, jscr):
                    j.finish(a, b, c)
        elif jobs:
            for j, a, b, c in zip(jobs, jins, jouts, jscr):
                j.finish(a, b, c)

    aliases = dict(own_aliases or {})
    in_off, out_off = n_in, n_out
    for j, a, b in zip(jobs, ji, jo):
        for s, d in j.aliases.items():
            aliases[in_off + s] = out_off + d
        in_off += a
        out_off += b
    res = pl.pallas_call(
        body, name=name, grid=grid,
        in_specs=list(in_specs) + [IN_HBM] * sum(ji), out_specs=list(out_specs) + [IN_HBM] * sum(jo),
        out_shape=list(out_shape) + [pltpu.HBM(s.shape, s.dtype) for j in jobs for s in j.out_shape],
        scratch_shapes=list(scratch) + [s for j in jobs for s in j.scratch],
        input_output_aliases=aliases, compiler_params=_cparams(len(grid), vmem_mb),
    )(*args, *[a for j in jobs for a in j.ins])
    res = list(res)
    main, rest, jres = res[:n_out], res[n_out:], []
    for k in jo:
        jres.append(rest[:k])
        rest = rest[k:]
    return main, jres


def _cast_bf16(arrays, name, jobs=()):
    n = len(arrays)

    def compute(ins, outs, scr):
        for i, o in zip(ins, outs):
            o[...] = i[...].astype(BF16)

    whole = [pl.BlockSpec(a.shape, lambda: (0, 0)) for a in arrays]
    return _call(name, (), compute, whole, whole, [jax.ShapeDtypeStruct(a.shape, BF16) for a in arrays], [], 48,
                 list(arrays), jobs)


def _exchange(name, jobs):
    return _call(name, (), lambda ins, outs, scr: None, [], [], [], [], 16, [], jobs)[1]


def _small_rows(ka, kb):
    first_a = 8
    first_b = first_a + -(-ka // 8) * 8
    return first_a, first_b, first_b + -(-kb // 8) * 8


LOSS_ROW = 7


def _allreduce_small(vecs, taps_a, taps_b, loss_part):
    counts = [len(v) for v in vecs]
    flat = [r for v in vecs for r in v]
    n = len(flat)
    C = flat[0].shape[1]
    NQ, KA, CB = taps_a.shape
    KB = taps_b.shape[1]
    first_a, first_b, R = _small_rows(KA, KB)
    assert len(vecs) <= LOSS_ROW < first_a
    N = 8

    def body(*refs):
        vec_refs = list(refs[:n])
        ta_ref, tb_ref, loss_ref, out_ref, v_ref, gath, send_sems, recv_sems, local_sem = refs[n:]
        v_ref[...] = jnp.zeros_like(v_ref)
        v_ref[LOSS_ROW:LOSS_ROW + 1, 0:loss_ref.shape[1]] = loss_ref[0:1, :]
        for i, k in enumerate(counts):
            parts, vec_refs = vec_refs[:k], vec_refs[k:]
            v_ref[i:i + 1, :] = functools.reduce(lambda a, b: a + b, [r[...] for r in parts])
        for q in range(NQ):
            v_ref[first_a:first_a + KA, q * CB:(q + 1) * CB] = ta_ref[q]
            v_ref[first_b:first_b + KB, q * CB:(q + 1) * CB] = tb_ref[q]
        x, y, c, chips = _place()
        me, sibling = (x, y, c), (x, y, 1 - c)

        def rows(px, py, pc):
            return gath.at[pl.ds((4 * px + 2 * py + pc) * R, R), :]

        def copy(k, block, to, src=None):
            return pltpu.make_async_remote_copy(src_ref=rows(*block) if src is None else src, dst_ref=rows(*block),
                                                send_sem=send_sems.at[k], recv_sem=recv_sems.at[k],
                                                device_id=to, device_id_type=MESH)

        mine = pltpu.make_async_copy(v_ref, rows(*me), local_sem)
        mine.start()
        first = [copy(0, me, sibling, src=v_ref)]
        first += [copy(1 + j, me, (*chip, c), src=v_ref) for j, chip in enumerate(chips)]
        for cp in first:
            cp.start()
        passed = [copy(4 + j, (*chip, c), sibling) for j, chip in enumerate(chips)]
        for j, chip in enumerate(chips):
            copy(1 + j, (*chip, c), me).wait_recv()
            passed[j].start()
        copy(0, sibling, me).wait_recv()
        for j, chip in enumerate(chips):
            copy(4 + j, (*chip, 1 - c), me).wait_recv()
        for cp in first + passed:
            cp.wait_send()
        mine.wait()
        acc = gath[0:R, :]
        for d in range(1, N):
            acc = acc + gath[d * R:(d + 1) * R, :]
        out_ref[...] = acc

    vmem = pl.BlockSpec(memory_space=pltpu.VMEM)
    return pl.pallas_call(
        body, name="allreduce_small",
        in_specs=[vmem] * (n + 3), out_specs=vmem,
        out_shape=jax.ShapeDtypeStruct((R, C), F32),
        scratch_shapes=[pltpu.VMEM((R, C), F32), pltpu.VMEM((N * R, C), F32), pltpu.SemaphoreType.DMA((7,)),
                        pltpu.SemaphoreType.DMA((7,)), pltpu.SemaphoreType.DMA],
    )(*flat, taps_a, taps_b, loss_part)


def _adamw_math(w, g, m, v):
    c1 = 1.0 - ADAM_B1 ** ADAM_STEP
    c2 = 1.0 - ADAM_B2 ** ADAM_STEP
    mn = ADAM_B1 * m + (1.0 - ADAM_B1) * g
    vn = ADAM_B2 * v + (1.0 - ADAM_B2) * (g * g)
    return -ADAM_LR * ((mn / c1) / (jnp.sqrt(vn / c2) + ADAM_EPS) + ADAM_WD * w), mn, vn


def _small_adamw(place, small, vec_wmv, tap_wmv):
    n = len(vec_wmv)
    D = small.shape[1]
    CB = tap_wmv[0][0].shape[2]
    ks = [t[0].shape[1] for t in tap_wmv]
    firsts = _small_rows(*ks)[:2]

    def body(place_ref, small_ref, *refs):
        ins, outs = refs[:3 * (n + 2)], refs[3 * (n + 2):]
        chip = place_ref[0]
        for i in range(n):
            g = small_ref[i:i + 1, :]
            d, mn, vn = _adamw_math(ins[3 * i][...], g, ins[3 * i + 1][...], ins[3 * i + 2][...])
            for o, val in zip(outs[4 * i:4 * i + 4], (g, d, mn, vn)):
                o[...] = val
        for t, (row0, k) in enumerate(zip(firsts, ks)):
            g = jnp.zeros((k, CB), F32)
            for q in range(D // CB):
                g = g + jnp.where(chip == q, small_ref[row0:row0 + k, q * CB:(q + 1) * CB], 0.0)
            w_ref, m_ref, v_ref = ins[3 * (n + t):3 * (n + t) + 3]
            d, mn, vn = _adamw_math(w_ref[0], g, m_ref[0], v_ref[0])
            for o, val in zip(outs[4 * (n + t):4 * (n + t) + 4], (g, d, mn, vn)):
                o[0] = val

    flat = [a for wmv in list(vec_wmv) + list(tap_wmv) for a in wmv]
    shapes = [jax.ShapeDtypeStruct(wmv[0].shape, F32) for wmv in list(vec_wmv) + list(tap_wmv) for _ in range(4)]
    vmem = pl.BlockSpec(memory_space=pltpu.VMEM)
    res = pl.pallas_call(
        body, name="small_adamw",
        in_specs=[pl.BlockSpec(memory_space=pltpu.SMEM)] + [vmem] * (1 + len(flat)), out_specs=[vmem] * len(shapes),
        out_shape=shapes,
    )(place, small, *flat)
    return [res[4 * i:4 * i + 4] for i in range(n + 2)]


def _ffn_fwd(h, g, wg, wu, wd, name, jobs=(), head=None):
    S, D = h.shape
    F = wg.shape[0]
    ts = _tile(S, 512)
    fb = _tile(F, F // 2)
    nf = F // fb

    def compute(ins, outs, scr):
        h_ref, g_ref, wg_ref, wu_ref, wd_ref = ins[:5]
        n_ref, gp_ref, up_ref = outs[-3:]
        x = h_ref[...]
        n = _rms_fwd(x, g_ref[...]).astype(BF16)
        n_ref[...] = n
        acc = None
        for j in range(nf):
            cols = slice(j * fb, (j + 1) * fb)
            gp = _dot_nt(n, wg_ref[cols, :])
            up = _dot_nt(n, wu_ref[cols, :])
            gp_ref[:, cols] = gp.astype(BF16)
            up_ref[:, cols] = up.astype(BF16)
            part = _dot((gp * _sigmoid(gp) * up).astype(BF16), wd_ref[cols, :])
            acc = part if acc is None else acc + part
        ho = x + 0.5 * acc
        if head is None:
            outs[0][...] = ho
            return
        t_ref, gf_ref = ins[5:]
        dh_ref, do_ref, dgf_ref, loss_ref = outs[:4]

        @pl.when(pl.program_id(0) == 0)
        def _():
            dgf_ref[...] = jnp.zeros_like(dgf_ref)
            loss_ref[...] = jnp.zeros_like(loss_ref)

        err = _rms_fwd(ho, gf_ref[...]) - t_ref[...]
        loss_ref[...] += (0.5 / D) * jnp.sum(err * err)
        dx, dg = _rms_bwd(ho, gf_ref[...], err * (1.0 / D))
        dh_ref[...] = dx
        do_ref[...] = (0.5 * dx).astype(BF16)
        dgf_ref[...] += dg

    tok = pl.BlockSpec((ts, D), lambda i: (i, 0))
    row = pl.BlockSpec((1, D), lambda i: (0, 0))
    wsp = _resident((F, D))
    hid = pl.BlockSpec((ts, F), lambda i: (i, 0))
    saved = [jax.ShapeDtypeStruct((S, D), BF16), jax.ShapeDtypeStruct((S, F), BF16), jax.ShapeDtypeStruct((S, F), BF16)]
    if head is None:
        return _call(name, (S // ts,), compute, [tok, row, wsp, wsp, wsp], [tok, tok, hid, hid],
                     [jax.ShapeDtypeStruct((S, D), F32)] + saved, [], 56, [h, g, wg, wu, wd], jobs)
    return _call(name, (S // ts,), compute, [tok, row, wsp, wsp, wsp, tok, row],
                 [tok, tok, row, pl.BlockSpec((8, 128), lambda i: (0, 0)), tok, hid, hid],
                 [jax.ShapeDtypeStruct((S, D), F32), jax.ShapeDtypeStruct((S, D), BF16), jax.ShapeDtypeStruct((1, D), F32),
                  jax.ShapeDtypeStruct((8, 128), F32)] + saved, [], 60, [h, g, wg, wu, wd, *head], jobs)


def _ffn_bwd_hidden(do, gp, up, wd, name, jobs=()):
    S, D = do.shape
    F = wd.shape[0]
    ts = _tile(S, 1024)
    fb = _tile(F, F // 2)
    def compute(ins, outs, scr):
        do_ref, gp_ref, up_ref, wd_ref = ins
        dgp_ref, dup_ref, a_ref = outs
        parts = _row_parts(ts, 4)
        das = [_dot_nt(do_ref[rows, :], wd_ref[...]) for rows in parts]
        for rows, da in zip(parts, das):
            gf = gp_ref[rows, :].astype(F32)
            uf = up_ref[rows, :].astype(F32)
            sg = _sigmoid(gf)
            si = gf * sg
            dgp_ref[rows, :] = (da * uf * (sg * (1.0 + gf * (1.0 - sg)))).astype(BF16)
            dup_ref[rows, :] = (da * si).astype(BF16)
            a_ref[rows, :] = (si * uf).astype(BF16)

    tok = pl.BlockSpec((ts, D), lambda s, i: (i, 0))
    hid = pl.BlockSpec((ts, fb), lambda s, i: (i, s))
    return _call(name, (F // fb, S // ts), compute, [tok, hid, hid, pl.BlockSpec((fb, D), lambda s, i: (s, 0))],
                 [hid, hid, hid], [jax.ShapeDtypeStruct((S, F), BF16)] * 3, [], 56, [do, gp, up, wd], jobs)


def _ffn_bwd_input(dh, h, g, dgp, dup, wg, wu, name, jobs=(), part=(0, 1), into=None):
    S, D = h.shape
    F = wg.shape[0]
    ts = _tile(S, 512)
    steps = S // ts // part[1]
    first = part[0] * steps

    def compute(ins, outs, scr):
        dh_ref, h_ref, g_ref, dgp_ref, dup_ref, wg_ref, wu_ref = ins[:7]
        dhi_ref, dg_ref = outs

        @pl.when(pl.program_id(0) == 0)
        def _():
            dg_ref[...] = jnp.zeros_like(dg_ref)

        dn = _dot(dgp_ref[...], wg_ref[...]) + _dot(dup_ref[...], wu_ref[...])
        dx, dg = _rms_bwd(h_ref[...], g_ref[...], dn)
        dhi_ref[...] = dh_ref[...] + dx
        dg_ref[...] += dg

    tok = pl.BlockSpec((ts, D), lambda i: (first + i, 0))
    hid = pl.BlockSpec((ts, F), lambda i: (first + i, 0))
    row = pl.BlockSpec((1, D), lambda i: (0, 0))
    in_specs = [tok, tok, row, hid, hid, _resident((F, D)), _resident((F, D))]
    args = [dh, h, g, dgp, dup, wg, wu]
    if into is not None:
        in_specs, args = in_specs + [IN_HBM], args + [into]
    return _call(name, (steps,), compute, in_specs, [tok, row],
                 [jax.ShapeDtypeStruct((S, D), F32), jax.ShapeDtypeStruct((1, D), F32)], [], 56, args, jobs,
                 own_aliases=None if into is None else {7: 0})


def _ffn_wgrad(hid, tok, name, jobs=()):
    S, D = tok.shape
    F = hid.shape[1]
    fb = _tile(F, F // 2)

    def compute(ins, outs, scr):
        outs[0][...] = _dot_tn(ins[0][...], ins[1][...])

    main, jres = _call(name, (F // fb,), compute,
                       [pl.BlockSpec((S, fb), lambda j: (0, j)), _resident(tok.shape)],
                       [pl.BlockSpec((fb, D), lambda j: (j, 0))], [jax.ShapeDtypeStruct((F, D), F32)], [], 56,
                       [hid, tok], jobs)
    return main[0], jres


def _w_in_pieces(D, cq, ng):
    groups = []
    for k in range(ng):
        lo, hi, pieces = k * D, (k + 1) * D, []
        while lo < hi:
            q = lo // cq
            w = min(hi, (q + 1) * cq) - lo
            pieces.append((q, lo - q * cq, w, lo - k * D))
            lo += w
        groups.append(pieces)
    return groups


def _mix_in_fwd(h, g, win, jobs=()):
    S, D = h.shape
    NG = win.shape[0] * win.shape[2] // D
    pieces = _w_in_pieces(D, win.shape[2], NG)
    ts = _tile(S, 512)

    def compute(ins, outs, scr):
        h_ref, g_ref, w_ref = ins
        u_ref, z_ref = outs
        u = _rms_fwd(h_ref[...], g_ref[...]).astype(BF16)
        u_ref[...] = u
        for k in range(NG):
            for q, c0, w, d0 in pieces[k]:
                z_ref[k, :, d0:d0 + w] = _dot(u, w_ref[q, :, c0:c0 + w]).astype(BF16)

    return _call("mix_in_fwd", (S // ts,), compute,
                 [pl.BlockSpec((ts, D), lambda i: (i, 0)), pl.BlockSpec((1, D), lambda i: (0, 0)), _resident(win.shape)],
                 [pl.BlockSpec((ts, D), lambda i: (i, 0)), pl.BlockSpec((NG, ts, D), lambda i: (0, i, 0))],
                 [jax.ShapeDtypeStruct((S, D), BF16), jax.ShapeDtypeStruct((NG, S, D), BF16)],
                 [], 48, [h, g, win], jobs)


SUBLANES = 8


def _shifted_copies(s):
    n = s.shape[1] - SUBLANES
    for r in range(1, SUBLANES):
        s[r, 0:n, :] = s[0, r:r + n, :]


def _window(s, o, rows):
    r = o % SUBLANES
    return s[r, o - r:o - r + rows, :]


def _conv_fwd(z, wa, ba, wb, jobs=()):
    _, S, D = z.shape
    _, KA, CB = wa.shape
    KB = wb.shape[1]
    ts = _tile(S, 1024)
    r = ts // HALO
    CH = min(64, ts)

    def compute(ins, outs, scr):
        z_ref, zh_ref, wa_ref, ba_ref, wb_ref = ins
        a1_ref, q_ref = outs
        sa, sb = scr
        keep = (pl.program_id(1) > 0).astype(F32)
        sa[0, HALO:HALO + ts, :] = z_ref[0].astype(F32) * _sigmoid(z_ref[1].astype(F32))
        sa[0, 0:HALO, :] = zh_ref[0].astype(F32) * _sigmoid(zh_ref[1].astype(F32)) * keep
        _shifted_copies(sa)
        sb[HALO:HALO + ts, :] = z_ref[3].astype(F32) * z_ref[4].astype(F32)
        sb[0:HALO, :] = zh_ref[3].astype(F32) * zh_ref[4].astype(F32) * keep
        wak = [wa_ref[0, k:k + 1, :] for k in range(KA)]
        wbk = [wb_ref[0, k:k + 1, :] for k in range(KB)]
        for c0 in range(0, ts, CH):
            acc = jnp.broadcast_to(ba_ref[...], (CH, CB))
            for k in range(KA):
                acc = acc + wak[k] * _window(sa, c0 + HALO - (KA - 1) + k, CH)
            a1_ref[c0:c0 + CH, :] = acc
            v = jnp.zeros((CH, CB), F32)
            for k in range(KB):
                o = c0 + HALO - (KB - 1) + k
                v = v + wbk[k] * sb[o:o + CH, :]
            q_ref[c0:c0 + CH, :] = (z_ref[2, c0:c0 + CH, :].astype(F32) * v).astype(BF16)

    return _call("conv_fwd", (D // CB, S // ts), compute,
                 [pl.BlockSpec((5, ts, CB), lambda j, i: (0, i, j)),
                  pl.BlockSpec((5, HALO, CB), lambda j, i: (0, jnp.maximum(i * r - 1, 0), j)),
                  pl.BlockSpec((1, KA, CB), lambda j, i: (j, 0, 0)), pl.BlockSpec((1, CB), lambda j, i: (0, j)),
                  pl.BlockSpec((1, KB, CB), lambda j, i: (j, 0, 0))],
                 [pl.BlockSpec((ts, CB), lambda j, i: (i, j)), pl.BlockSpec((ts, CB), lambda j, i: (i, j))],
                 [jax.ShapeDtypeStruct((S, D), F32), jax.ShapeDtypeStruct((S, D), BF16)],
                 [pltpu.VMEM((SUBLANES, HALO + ts, CB), F32), pltpu.VMEM((HALO + ts, CB), F32)], 40, [z, z, wa, ba, wb], jobs)


def _ln_stats(a1):
    mu = jnp.mean(a1, axis=-1, keepdims=True)
    xc = a1 - mu
    rstd = lax.rsqrt(jnp.mean(xc * xc, axis=-1, keepdims=True) + EPS)
    return xc * rstd, rstd


def _mix_out_fwd(h1, a1, q, z, lng, lnb, wa, wb, wo, jobs=()):
    S, D = h1.shape
    ts = _tile(S, 512)

    def compute(ins, outs, scr):
        h_ref, a1_ref, q_ref, ga_ref, gb_ref, lng_ref, lnb_ref, wa_ref, wb_ref, wo_ref = ins
        h2_ref, a3_ref, m_ref, ya_ref, yb_ref = outs
        xhat, _ = _ln_stats(a1_ref[...])
        a2 = xhat * lng_ref[...] + lnb_ref[...]
        a3 = (a2 * _sigmoid(a2)).astype(BF16)
        a3_ref[...] = a3
        ya = _dot(a3, wa_ref[...])
        yb = _dot(q_ref[...], wb_ref[...])
        ya_ref[...] = ya.astype(BF16)
        yb_ref[...] = yb.astype(BF16)
        m = (_sigmoid(ga_ref[0].astype(F32)) * ya + _sigmoid(gb_ref[0].astype(F32)) * yb).astype(BF16)
        m_ref[...] = m
        h2_ref[...] = h_ref[...] + _dot(m, wo_ref[...])

    tok = pl.BlockSpec((ts, D), lambda i: (i, 0))
    row = pl.BlockSpec((1, D), lambda i: (0, 0))
    mat = _resident((D, D))
    return _call("mix_out_fwd", (S // ts,), compute,
                 [tok, tok, tok, pl.BlockSpec((1, ts, D), lambda i: (5, i, 0)), pl.BlockSpec((1, ts, D), lambda i: (6, i, 0)),
                  row, row, mat, mat, mat], [tok] * 5,
                 [jax.ShapeDtypeStruct((S, D), F32)] + [jax.ShapeDtypeStruct((S, D), BF16)] * 4,
                 [], 56, [h1, a1, q, z, z, lng, lnb, wa, wb, wo], jobs)


def _mix_out_bwd(dh2, a1, z, ya, yb, lng, lnb, wa, wb, wo, jobs=()):
    S, D = dh2.shape
    ts = _tile(S, 512)

    def compute(ins, outs, scr):
        dh_ref, a1_ref, ga_ref, gb_ref, ya_ref, yb_ref, lng_ref, lnb_ref, wa_ref, wb_ref, wo_ref = ins
        da1_ref, dq_ref, dga_ref, dgb_ref, dya_ref, dyb_ref, dhb_ref, dlg_ref, dlb_ref = outs

        @pl.when(pl.program_id(0) == 0)
        def _():
            dlg_ref[...] = jnp.zeros_like(dlg_ref)
            dlb_ref[...] = jnp.zeros_like(dlb_ref)

        for rows in _row_parts(ts):
            dhb = dh_ref[rows, :].astype(BF16)
            dhb_ref[rows, :] = dhb
            dm = _dot_nt(dhb, wo_ref[...])
            sa = _sigmoid(ga_ref[0, rows, :].astype(F32))
            sb = _sigmoid(gb_ref[0, rows, :].astype(F32))
            dga_ref[rows, :] = (dm * ya_ref[rows, :].astype(F32) * sa * (1.0 - sa)).astype(BF16)
            dgb_ref[rows, :] = (dm * yb_ref[rows, :].astype(F32) * sb * (1.0 - sb)).astype(BF16)
            dya = (sa * dm).astype(BF16)
            dyb = (sb * dm).astype(BF16)
            dya_ref[rows, :] = dya
            dyb_ref[rows, :] = dyb
            dq_ref[rows, :] = _dot_nt(dyb, wb_ref[...]).astype(BF16)
            da3 = _dot_nt(dya, wa_ref[...])
            xhat, rstd = _ln_stats(a1_ref[rows, :])
            a2 = xhat * lng_ref[...] + lnb_ref[...]
            sg = _sigmoid(a2)
            da2 = da3 * (sg * (1.0 + a2 * (1.0 - sg)))
            dlg_ref[...] += jnp.sum(da2 * xhat, axis=0, keepdims=True)
            dlb_ref[...] += jnp.sum(da2, axis=0, keepdims=True)
            dxh = da2 * lng_ref[...]
            da1_ref[rows, :] = (rstd * (dxh - jnp.mean(dxh, axis=-1, keepdims=True)
                                        - xhat * jnp.mean(dxh * xhat, axis=-1, keepdims=True))).astype(BF16)

    tok = pl.BlockSpec((ts, D), lambda i: (i, 0))
    row = pl.BlockSpec((1, D), lambda i: (0, 0))
    mat = _resident((D, D))
    return _call("mix_out_bwd", (S // ts,), compute,
                 [tok, tok, pl.BlockSpec((1, ts, D), lambda i: (5, i, 0)), pl.BlockSpec((1, ts, D), lambda i: (6, i, 0)),
                  tok, tok, row, row, mat, mat, mat], [tok] * 7 + [row, row],
                 [jax.ShapeDtypeStruct((S, D), BF16)] * 7 + [jax.ShapeDtypeStruct((1, D), F32)] * 2,
                 [], 56, [dh2, a1, z, z, ya, yb, lng, lnb, wa, wb, wo], jobs)


def _mixer_wgrads(a3, dya, q, dyb, mm, dhb, jobs=()):
    S, D = a3.shape
    tk = _tile(S, 512)

    def compute(ins, outs, scr):
        @pl.when(pl.program_id(0) == 0)
        def _():
            for o in outs:
                o[...] = jnp.zeros_like(o)

        for t in range(3):
            outs[t][...] += _dot_tn(ins[2 * t][...], ins[2 * t + 1][...])

    tok = pl.BlockSpec((tk, D), lambda k: (k, 0))
    return _call("mixer_wgrads", (S // tk,), compute, [tok] * 6, [pl.BlockSpec((D, D), lambda k: (0, 0))] * 3,
                 [jax.ShapeDtypeStruct((D, D), F32)] * 3, [], 56, [a3, dya, q, dyb, mm, dhb], jobs)


def _conv_bwd(z, da1, dq, dga, dgb, wa, wb, jobs=()):
    NG, S, D = z.shape
    _, KA, CB = wa.shape
    KB = wb.shape[1]
    ts = _tile(S, 1024)
    r = ts // HALO
    nt = S // ts
    CH = min(64, ts)
    last_halo = S // HALO - 1

    def compute(ins, outs, scr):
        z_ref, zp_ref, zn_ref, da1_ref, da1n_ref, dq_ref, dqn_ref, dga_ref, dgb_ref, wa_ref, wb_ref = ins
        dz_ref, dwa_ref, dba_ref, dwb_ref = outs
        sa0, sd, sp, sv, acca, accb = scr
        i = pl.program_id(1)
        prev = (i > 0).astype(F32)
        nxt = (i < nt - 1).astype(F32)

        @pl.when(i == 0)
        def _():
            acca[...] = jnp.zeros_like(acca)
            accb[...] = jnp.zeros_like(accb)
            dba_ref[...] = jnp.zeros_like(dba_ref)

        sa0[0, HALO:HALO + ts, :] = z_ref[0].astype(F32) * _sigmoid(z_ref[1].astype(F32))
        sa0[0, 0:HALO, :] = zp_ref[0].astype(F32) * _sigmoid(zp_ref[1].astype(F32)) * prev
        _shifted_copies(sa0)
        sp[HALO:HALO + ts, :] = z_ref[3].astype(F32) * z_ref[4].astype(F32)
        sp[0:HALO, :] = zp_ref[3].astype(F32) * zp_ref[4].astype(F32) * prev
        sd[0, 0:ts, :] = da1_ref[...].astype(F32)
        sd[0, ts:ts + HALO, :] = da1n_ref[...].astype(F32) * nxt
        _shifted_copies(sd)
        sv[0:ts, :] = dq_ref[...].astype(F32) * z_ref[2].astype(F32)
        sv[ts:ts + HALO, :] = dqn_ref[...].astype(F32) * zn_ref[2].astype(F32) * nxt
        dba_ref[...] += jnp.sum(sd[0, 0:ts, :], axis=0, keepdims=True)
        wak = [wa_ref[0, k:k + 1, :] for k in range(KA)]
        wbk = [wb_ref[0, k:k + 1, :] for k in range(KB)]
        for c0 in range(0, ts, CH):
            rows = slice(c0, c0 + CH)
            d1 = sd[0, rows, :]
            da0 = jnp.zeros((CH, CB), F32)
            for k in range(KA):
                da0 = da0 + wak[k] * _window(sd, c0 + (KA - 1) - k, CH)
                a0w = _window(sa0, c0 + HALO - (KA - 1) + k, CH)
                acca[k] += jnp.sum((d1 * a0w).reshape(CH // 8, 8, CB), axis=0)
            val = z_ref[0, rows, :].astype(F32)
            sg = _sigmoid(z_ref[1, rows, :].astype(F32))
            dz_ref[0, rows, :] = (da0 * sg).astype(BF16)
            dz_ref[1, rows, :] = (da0 * val * sg * (1.0 - sg)).astype(BF16)
            dv = sv[rows, :]
            v = jnp.zeros((CH, CB), F32)
            dp = jnp.zeros((CH, CB), F32)
            for k in range(KB):
                o = c0 + HALO - (KB - 1) + k
                pw = sp[o:o + CH, :]
                v = v + wbk[k] * pw
                accb[k] += jnp.sum((dv * pw).reshape(CH // 8, 8, CB), axis=0)
                o = c0 + (KB - 1) - k
                dp = dp + wbk[k] * sv[o:o + CH, :]
            dz_ref[2, rows, :] = (dq_ref[rows, :].astype(F32) * v).astype(BF16)
            dz_ref[3, rows, :] = (dp * z_ref[4, rows, :].astype(F32)).astype(BF16)
            dz_ref[4, rows, :] = (dp * z_ref[3, rows, :].astype(F32)).astype(BF16)
        dz_ref[5] = dga_ref[...]
        dz_ref[6] = dgb_ref[...]

        @pl.when(i == nt - 1)
        def _():
            dwa_ref[0] = jnp.sum(acca[...], axis=1)
            dwb_ref[0] = jnp.sum(accb[...], axis=1)

    zt = pl.BlockSpec((5, ts, CB), lambda j, i: (0, i, j))
    zp = pl.BlockSpec((5, HALO, CB), lambda j, i: (0, jnp.maximum(i * r - 1, 0), j))
    zn = pl.BlockSpec((5, HALO, CB), lambda j, i: (0, jnp.minimum((i + 1) * r, last_halo), j))
    tok = pl.BlockSpec((ts, CB), lambda j, i: (i, j))
    tokn = pl.BlockSpec((HALO, CB), lambda j, i: (jnp.minimum((i + 1) * r, last_halo), j))
    return _call("conv_bwd", (D // CB, nt), compute,
                 [zt, zp, zn, tok, tokn, tok, tokn, tok, tok,
                  pl.BlockSpec((1, KA, CB), lambda j, i: (j, 0, 0)), pl.BlockSpec((1, KB, CB), lambda j, i: (j, 0, 0))],
                 [pl.BlockSpec((NG, ts, CB), lambda j, i: (0, i, j)), pl.BlockSpec((1, KA, CB), lambda j, i: (j, 0, 0)),
                  pl.BlockSpec((1, CB), lambda j, i: (0, j)), pl.BlockSpec((1, KB, CB), lambda j, i: (j, 0, 0))],
                 [jax.ShapeDtypeStruct((NG, S, D), BF16), jax.ShapeDtypeStruct((D // CB, KA, CB), F32),
                  jax.ShapeDtypeStruct((1, D), F32), jax.ShapeDtypeStruct((D // CB, KB, CB), F32)],
                 [pltpu.VMEM((SUBLANES, HALO + ts, CB), F32), pltpu.VMEM((SUBLANES, ts + HALO, CB), F32),
                  pltpu.VMEM((HALO + ts, CB), F32), pltpu.VMEM((ts + HALO, CB), F32),
                  pltpu.VMEM((KA, 8, CB), F32), pltpu.VMEM((KB, 8, CB), F32)],
                 48, [z, z, z, da1, da1, dq, dq, dga, dgb, wa, wb], jobs)


def _mix_in_bwd(dh2, h1, g, dz, win, jobs=()):
    S, D = h1.shape
    NG = dz.shape[0]
    pieces = _w_in_pieces(D, win.shape[2], NG)
    ts = _tile(S, 512)

    def compute(ins, outs, scr):
        dh_ref, h_ref, g_ref, dz_ref, w_ref = ins
        dhi_ref, dg_ref, do_ref = outs

        @pl.when(pl.program_id(0) == 0)
        def _():
            dg_ref[...] = jnp.zeros_like(dg_ref)

        du = None
        for k in range(NG):
            for q, c0, w, d0 in pieces[k]:
                part = _dot_nt(dz_ref[k, :, d0:d0 + w], w_ref[q, :, c0:c0 + w])
                du = part if du is None else du + part
        dx, dg = _rms_bwd(h_ref[...], g_ref[...], du)
        dhi = dh_ref[...] + dx
        dhi_ref[...] = dhi
        do_ref[...] = (0.5 * dhi).astype(BF16)
        dg_ref[...] += dg

    tok = pl.BlockSpec((ts, D), lambda i: (i, 0))
    row = pl.BlockSpec((1, D), lambda i: (0, 0))
    return _call("mix_in_bwd", (S // ts,), compute,
                 [tok, tok, row, pl.BlockSpec((NG, ts, D), lambda i: (0, i, 0)), _resident(win.shape)],
                 [tok, row, tok],
                 [jax.ShapeDtypeStruct((S, D), F32), jax.ShapeDtypeStruct((1, D), F32), jax.ShapeDtypeStruct((S, D), BF16)],
                 [], 56, [dh2, h1, g, dz, win], jobs)


def _w_in_grad(u, dz, jobs=()):
    S, D = u.shape
    NG = dz.shape[0]

    def compute(ins, outs, scr):
        outs[0][...] = _dot_tn(ins[0][...], ins[1][0])

    return _call("w_in_grad", (NG,), compute,
                 [_resident(u.shape), pl.BlockSpec((1, S, D), lambda j: (j, 0, 0))],
                 [pl.BlockSpec((D, D), lambda j: (0, j))], [jax.ShapeDtypeStruct((D, NG * D), F32)], [], 48, [u, dz], jobs)


def _chip_sums(place, grads, got, kind, name):
    n = len(grads)
    qr, qc = _quarter_shape(grads[0].shape, kind)
    h = qr // 2
    tr = _row_tile(h)
    nr = h // tr

    def body(pc_ref, *refs):
        g_refs, got_refs, b_refs, f_refs = refs[:n], refs[n:2 * n], refs[2 * n:3 * n], refs[3 * n:]
        own = pl.program_id(1) == pc_ref[0]
        for a in range(n):
            s = g_refs[a][...] + got_refs[a][0]
            b_refs[a][0] = s.astype(BF16)

            @pl.when(own)
            def _():
                f_refs[a][...] = s

    if kind == "rows":
        gspec = pl.BlockSpec((tr, qc), lambda r, q, pc: (q * (2 * nr) + pc[1] * nr + r, 0))
    else:
        gspec = pl.BlockSpec((tr, qc), lambda r, q, pc: (pc[1] * nr + r, q))
    lspec = pl.BlockSpec((1, tr, qc), lambda r, q, pc: (q, r, 0))
    res = pl.pallas_call(
        body, name=name,
        grid_spec=pltpu.PrefetchScalarGridSpec(
            num_scalar_prefetch=1, grid=(nr, NS), in_specs=[gspec] * n + [lspec] * n,
            out_specs=[lspec] * n + [pl.BlockSpec((tr, qc), lambda r, q, pc: (r, 0))] * n),
        out_shape=[jax.ShapeDtypeStruct((NS, h, qc), BF16)] * n + [jax.ShapeDtypeStruct((h, qc), F32)] * n,
        compiler_params=_cparams(2, 48),
    )(place, *grads, *got)
    return res[:n], res[n:]


def _totals(place, own, got, name):
    n = len(own)
    h, qc = own[0].shape
    tr = _row_tile(h)
    nr = h // tr
    got = [list(g) if isinstance(g, (list, tuple)) else [g] for g in got]
    m = len(got[0])

    def body(pc_ref, *refs):
        own_refs, got_refs, o_refs = refs[:n], refs[n:n + n * m], refs[n + n * m:]
        for a in range(n):
            acc = own_refs[a][...]
            for g in got_refs[a * m:(a + 1) * m]:
                for k in range(g.shape[0]):
                    acc = acc + g[k].astype(F32)
            o_refs[a][...] = acc

    lands = [pl.BlockSpec((g.shape[0], tr, qc), lambda r, pc: (0, r, 0)) for gs in got for g in gs]
    return pl.pallas_call(
        body, name=name,
        grid_spec=pltpu.PrefetchScalarGridSpec(
            num_scalar_prefetch=1, grid=(nr,),
            in_specs=[pl.BlockSpec((tr, qc), lambda r, pc: (r, 0))] * n + lands,
            out_specs=[pl.BlockSpec((tr, qc), lambda r, pc: (pc[1] * nr + r, 0))] * n),
        out_shape=[jax.ShapeDtypeStruct((2 * h, qc), F32)] * n,
        compiler_params=_cparams(1, 48),
    )(place, *own, *[g for gs in got for g in gs])


def _adamw(ws, gs, ms, vs, name):
    n = len(ws)
    R, C = ws[0].shape
    tr = _row_tile(R, (36 << 20) // (7 * 2 * 4 * n * C))

    def body(*refs):
        w_refs, g_refs, m_refs, v_refs = refs[:n], refs[n:2 * n], refs[2 * n:3 * n], refs[3 * n:4 * n]
        d_refs, mo_refs, vo_refs = refs[4 * n:5 * n], refs[5 * n:6 * n], refs[6 * n:]
        for a in range(n):
            d_refs[a][...], mo_refs[a][...], vo_refs[a][...] = _adamw_math(w_refs[a][...], g_refs[a][...], m_refs[a][...],
                                                                         v_refs[a][...])

    blk = pl.BlockSpec((tr, C), lambda r: (r, 0))
    res = pl.pallas_call(
        body, name=name, grid=(R // tr,),
        in_specs=[blk] * (4 * n), out_specs=[blk] * (3 * n),
        out_shape=[jax.ShapeDtypeStruct((R, C), F32)] * (3 * n),
        compiler_params=_cparams(1, 56),
    )(*ws, *gs, *ms, *vs)
    return res[:n], res[n:2 * n], res[2 * n:]


def kernel(x, ffn1_norm, ffn1_w_gate, ffn1_w_up, ffn1_w_down, mix_norm, w_in, a_dw_w, a_dw_b, a_ln_g, a_ln_b, a_w_out, b_conv_w, b_w_out, w_o, ffn2_norm, ffn2_w_gate, ffn2_w_up, ffn2_w_down, final_norm, loss_target, m_ffn1_norm, m_ffn1_w_gate, m_ffn1_w_up, m_ffn1_w_down, m_mix_norm, m_w_in, m_a_dw_w, m_a_dw_b, m_a_ln_g, m_a_ln_b, m_a_w_out, m_b_conv_w, m_b_w_out, m_w_o, m_ffn2_norm, m_ffn2_w_gate, m_ffn2_w_up, m_ffn2_w_down, m_final_norm, v_ffn1_norm, v_ffn1_w_gate, v_ffn1_w_up, v_ffn1_w_down, v_mix_norm, v_w_in, v_a_dw_w, v_a_dw_b, v_a_ln_g, v_a_ln_b, v_a_w_out, v_b_conv_w, v_b_w_out, v_w_o, v_ffn2_norm, v_ffn2_w_gate, v_ffn2_w_up, v_ffn2_w_down, v_final_norm):
    names = ["ffn1_norm", "ffn1_w_gate", "ffn1_w_up", "ffn1_w_down", "mix_norm", "w_in", "a_dw_w", "a_dw_b", "a_ln_g",
             "a_ln_b", "a_w_out", "b_conv_w", "b_w_out", "w_o", "ffn2_norm", "ffn2_w_gate", "ffn2_w_up", "ffn2_w_down",
             "final_norm"]
    W = dict(zip(names, [ffn1_norm, ffn1_w_gate, ffn1_w_up, ffn1_w_down, mix_norm, w_in, a_dw_w, a_dw_b, a_ln_g, a_ln_b,
                         a_w_out, b_conv_w, b_w_out, w_o, ffn2_norm, ffn2_w_gate, ffn2_w_up, ffn2_w_down, final_norm]))
    M = dict(zip(names, [m_ffn1_norm, m_ffn1_w_gate, m_ffn1_w_up, m_ffn1_w_down, m_mix_norm, m_w_in, m_a_dw_w, m_a_dw_b,
                         m_a_ln_g, m_a_ln_b, m_a_w_out, m_b_conv_w, m_b_w_out, m_w_o, m_ffn2_norm, m_ffn2_w_gate,
                         m_ffn2_w_up, m_ffn2_w_down, m_final_norm]))
    V = dict(zip(names, [v_ffn1_norm, v_ffn1_w_gate, v_ffn1_w_up, v_ffn1_w_down, v_mix_norm, v_w_in, v_a_dw_w, v_a_dw_b,
                         v_a_ln_g, v_a_ln_b, v_a_w_out, v_b_conv_w, v_b_w_out, v_w_o, v_ffn2_norm, v_ffn2_w_gate,
                         v_ffn2_w_up, v_ffn2_w_down, v_final_norm]))
    transposed = ("ffn1_w_gate", "ffn1_w_up", "ffn2_w_gate", "ffn2_w_up")
    vecs = ["ffn1_norm", "mix_norm", "a_dw_b", "a_ln_g", "a_ln_b", "ffn2_norm", "final_norm"]
    ffn1 = ["ffn1_w_gate", "ffn1_w_up", "ffn1_w_down"]
    ffn2 = ["ffn2_w_gate", "ffn2_w_up", "ffn2_w_down"]
    outp = ["a_w_out", "b_w_out", "w_o"]

    S, D = x.shape[1], x.shape[2]
    CB = D // NS
    KA, KB = a_dw_w.shape[1], b_conv_w.shape[1]
    px, py, pc = lax.axis_index("x"), lax.axis_index("y"), lax.axis_index("c")
    chip = 2 * px + py
    place = jnp.stack([chip, pc]).astype(jnp.int32)
    h0 = x.reshape(S, D)
    tgt = loss_target.reshape(S, D)
    row = lambda n: pltpu.with_memory_space_constraint(W[n].reshape(1, D), pltpu.HBM)
    pad = lambda a, r: jnp.concatenate([a, jnp.zeros((r - a.shape[0], a.shape[1]), F32)], axis=0)

    def quarter(P, n):
        return jnp.transpose(P[n][0]) if n in transposed else P[n][0]

    def unquarter(a, n):
        return (jnp.transpose(a) if n in transposed else a).reshape(W[n].shape)

    wq = {n: quarter(W, n).astype(BF16) for n in ffn1}

    later = ffn2 + outp + ["w_in"]
    casts, (f1,) = _cast_bf16([quarter(W, n) for n in later], "cast_weights", [_Gather([wq[n] for n in ffn1], ["rows"] * 3)])
    wq.update(zip(later, casts))
    g_in = _Gather([wq["w_in"], pad(a_dw_w[0], 32), pad(b_conv_w[0], 16)], ["rows"] * 3)
    (h1, n1, gp1, up1), ((win, taps_a, taps_b),) = _ffn_fwd(h0, row("ffn1_norm"), *f1, "ffn1_fwd", [g_in])
    win = win.reshape(NS, D, -1)
    wa_taps = taps_a.reshape(NS, 32, CB)[:, :KA]
    wb_taps = taps_b.reshape(NS, 16, CB)[:, :KB]
    g_out = _Gather([wq[n] for n in outp], ["rows"] * 3)
    (u, z), ((wa_out, wb_out, wo),) = _mix_in_fwd(h1, row("mix_norm"), win, [g_out])
    g_f2 = _Gather([wq["ffn2_w_gate"], wq["ffn2_w_up"]], ["rows"] * 2)
    (a1, q), ((f2g, f2u),) = _conv_fwd(z, wa_taps, row("a_dw_b"), wb_taps, [g_f2])
    (h2, a3, mm, ya, yb), ((f2d,),) = _mix_out_fwd(h1, a1, q, z, row("a_ln_g"), row("a_ln_b"), wa_out, wb_out, wo,
                                                   [_Gather([wq["ffn2_w_down"]], ["rows"])])
    (dh3, do2, d_final, loss_part, n2, gp2, up2), _ = _ffn_fwd(h2, row("ffn2_norm"), f2g, f2u, f2d, "ffn2_fwd_loss",
                                                               head=(tgt, row("final_norm")))

    (dgp2, dup2, act2), _ = _ffn_bwd_hidden(do2, gp2, up2, f2d, "ffn2_bwd_hidden")
    (dh2, d_ffn2), _ = _ffn_bwd_input(dh3, h2, row("ffn2_norm"), dgp2, dup2, f2g, f2u, "ffn2_bwd_input")
    g2 = [_ffn_wgrad(dgp2, n2, "ffn2_dwg")[0], _ffn_wgrad(dup2, n2, "ffn2_dwu")[0], _ffn_wgrad(act2, do2, "ffn2_dwd")[0]]
    (da1, dq, dga, dgb, dya, dyb, dh2b, d_lng, d_lnb), (got,) = _mix_out_bwd(
        dh2, a1, z, ya, yb, row("a_ln_g"), row("a_ln_b"), wa_out, wb_out, wo, [_ToSibling(g2, ["rows"] * 3)])
    wire2, own2 = _chip_sums(place, g2, got, "rows", "ffn2_chip_sums")
    (dz, d_wa, d_ba, d_wb), (got,) = _conv_bwd(z, da1, dq, dga, dgb, wa_taps, wb_taps, [_ToChips(wire2)])
    half2 = _totals(place, own2, got, "ffn2_totals")
    go, (tot2,) = _mixer_wgrads(a3, dya, q, dyb, mm, dh2b, [_SwapHalves(half2)])
    (g_win,), (got_o,) = _w_in_grad(u, dz, [_ToSibling(go, ["rows"] * 3)])
    wire_o, own_o = _chip_sums(place, go, got_o, "rows", "mixer_chip_sums")
    (dh1, d_mix, do1), (got, land_o) = _mix_in_bwd(dh2, h1, row("mix_norm"), dz, win,
                                                   [_ToSibling([g_win], ["cols"]), _ToChips(wire_o)])
    wire_in, own_in = _chip_sums(place, [g_win], got, "cols", "w_in_chip_sum")
    half_o = _totals(place, own_o, land_o, "mixer_totals")
    (dgp1, dup1, act1), (near_in, tot_o) = _ffn_bwd_hidden(do1, gp1, up1, f1[2], "ffn1_bwd_hidden",
                                                           [_ToChips(wire_in, (0, 1)), _SwapHalves(half_o)])
    g1g, (far_in,) = _ffn_wgrad(dgp1, n1, "ffn1_dwg", [_ToChips(wire_in, (2,))])
    half_in = _totals(place, own_in, [[near_in[0], far_in[0]]], "w_in_total")
    g1u, (tot_in, got_g) = _ffn_wgrad(dup1, n1, "ffn1_dwu", [_SwapHalves(half_in), _ToSibling([g1g], ["rows"])])
    wire_g, own_g = _chip_sums(place, [g1g], got_g, "rows", "ffn1_dwg_chip_sum")
    g1d, (got_u, land_g) = _ffn_wgrad(act1, do1, "ffn1_dwd", [_ToSibling([g1u], ["rows"]), _ToChips(wire_g)])
    wire_u, own_u = _chip_sums(place, [g1u], got_u, "rows", "ffn1_dwu_chip_sum")
    half_g = _totals(place, own_g, land_g, "ffn1_dwg_total")
    ffn1_in = (dh1, h0, row("ffn1_norm"), dgp1, dup1, f1[0], f1[1])
    (dx, dg_a), (got_d, land_u, tot_g) = _ffn_bwd_input(
        *ffn1_in, "ffn1_bwd_input_a", [_ToSibling([g1d], ["rows"]), _ToChips(wire_u), _SwapHalves(half_g)], part=(0, 2))
    wire_d, own_d = _chip_sums(place, [g1d], got_d, "rows", "ffn1_dwd_chip_sum")
    half_u = _totals(place, own_u, land_u, "ffn1_dwu_total")
    (dx, dg_b), (land_d, tot_u) = _ffn_bwd_input(*ffn1_in, "ffn1_bwd_input_b", [_ToChips(wire_d), _SwapHalves(half_u)],
                                                 part=(1, 2), into=dx)
    half_d = _totals(place, own_d, land_d, "ffn1_dwd_total")
    (tot_d,) = _exchange("tail_exchange", [_SwapHalves(half_d)])
    tot1 = [tot_g[0], tot_u[0], tot_d[0]]
    totals = dict(zip(ffn2 + ["w_in"] + ffn1 + outp, list(tot2) + list(tot_in) + tot1 + list(tot_o)))

    vec_grads = {"ffn1_norm": [dg_a, dg_b], "mix_norm": [d_mix], "a_dw_b": [d_ba], "a_ln_g": [d_lng], "a_ln_b": [d_lnb],
                 "ffn2_norm": [d_ffn2], "final_norm": [d_final]}
    small = _allreduce_small([vec_grads[n] for n in vecs], d_wa, d_wb, loss_part)
    loss = small[LOSS_ROW, 0]
    taps = ["a_dw_w", "b_conv_w"]
    small_out = _small_adamw(place, small, [[P[n].reshape(1, D) for P in (W, M, V)] for n in vecs],
                             [[P[n] for P in (W, M, V)] for n in taps])

    grads, deltas, new_m, new_v = {}, {}, {}, {}
    for n, (g_, d_, m_, v_) in zip(vecs + taps, small_out):
        shp = W[n].shape
        grads[n], deltas[n], new_m[n], new_v[n] = g_.reshape(shp), d_.reshape(shp), m_.reshape(shp), v_.reshape(shp)
    for group, tag in ((ffn1 + ffn2, "ffn"), (["w_in"], "w_in"), (outp, "mixer")):
        ds, ms, vs = _adamw([quarter(W, n) for n in group], [totals[n] for n in group], [quarter(M, n) for n in group],
                            [quarter(V, n) for n in group], tag + "_adamw")
        for n, d_, m_, v_ in zip(group, ds, ms, vs):
            grads[n], deltas[n], new_m[n], new_v[n] = (unquarter(totals[n], n), unquarter(d_, n), unquarter(m_, n),
                                                       unquarter(v_, n))
    return (loss, dx.reshape(x.shape), *[grads[n] for n in names], *[deltas[n] for n in names],
            *[new_m[n] for n in names], *[new_v[n] for n in names])
```

```python
import functools

import jax
import jax.numpy as jnp
from jax import lax
from jax.experimental import pallas as pl
from jax.experimental.pallas import tpu as pltpu

F32 = jnp.float32
BF16 = jnp.bfloat16
EPS = 1e-6
NS = 4
HALO = 32
MESH = pl.DeviceIdType.MESH
IN_HBM = pl.BlockSpec(memory_space=pltpu.HBM)

ADAM_LR = 0.001
ADAM_B1 = 0.9
ADAM_B2 = 0.999
ADAM_EPS = 1e-08
ADAM_WD = 0.01
ADAM_STEP = 10


def _cparams(n_axes, vmem_mb):
    return pltpu.CompilerParams(dimension_semantics=("arbitrary",) * n_axes, vmem_limit_bytes=vmem_mb << 20)


def _tile(n, t):
    return t if n % t == 0 else n


def _row_parts(rows, n=2):
    if rows % (16 * n):
        return [slice(0, rows)]
    return [slice(p * (rows // n), (p + 1) * (rows // n)) for p in range(n)]


def _resident(shape):
    return pl.BlockSpec(shape, lambda *_: (0,) * len(shape), pipeline_mode=pl.Buffered(1))


def _row_tile(n, cap=256):
    for t in (256, 176, 128, 64, 32, 16, 8):
        if t <= cap and n % t == 0:
            return t
    return n


def _dot(a, b):
    return jnp.dot(a, b, preferred_element_type=F32)


def _dot_nt(a, b):
    return lax.dot_general(a, b, (((1,), (1,)), ((), ())), preferred_element_type=F32)


def _dot_tn(a, b):
    return lax.dot_general(a, b, (((0,), (0,)), ((), ())), preferred_element_type=F32)


def _sigmoid(x):
    return jax.nn.sigmoid(x)


def _rms_fwd(x, g):
    r = lax.rsqrt(jnp.mean(x * x, axis=-1, keepdims=True) + EPS)
    return x * r * g


def _rms_bwd(x, g, dn):
    r = lax.rsqrt(jnp.mean(x * x, axis=-1, keepdims=True) + EPS)
    xr = x * r
    dg = jnp.sum(dn * xr, axis=0, keepdims=True)
    w = dn * g
    dx = r * w - xr * (r * r) * jnp.mean(x * w, axis=-1, keepdims=True)
    return dx, dg


def _place():
    x, y, c = lax.axis_index("x"), lax.axis_index("y"), lax.axis_index("c")
    chips = [(1 - x, y), (x, 1 - y), (1 - x, 1 - y)]
    return x, y, c, chips


def _quarter_shape(full_shape, kind):
    r, c = full_shape
    return (r // NS, c) if kind == "rows" else (r, c // NS)


def _half_of_quarter(ref, kind, q, pc):
    qr, qc = _quarter_shape(ref.shape, kind)
    h = qr // 2
    if kind == "rows":
        return ref.at[pl.ds(q * qr + pc * h, h), :]
    return ref.at[pl.ds(pc * h, h), pl.ds(q * qc, qc)]


def _quarter(ref, kind, q):
    qr, qc = _quarter_shape(ref.shape, kind)
    if kind == "rows":
        return ref.at[pl.ds(q * qr, qr), :]
    return ref.at[:, pl.ds(q * qc, qc)]


def _rows_half(ref, pc):
    h = ref.shape[0] // 2
    return ref.at[pl.ds(pc * h, h)]


class _Gather:
    def __init__(self, quarters, kinds):
        self.ins = list(quarters)
        self.kinds = list(kinds)
        n = len(self.ins)
        self.out_shape = [jax.ShapeDtypeStruct((NS * a.shape[0], a.shape[1]) if k == "rows" else (a.shape[0], NS * a.shape[1]),
                                               a.dtype) for a, k in zip(self.ins, self.kinds)]
        self.scratch = [pltpu.SemaphoreType.DMA((n, 6)), pltpu.SemaphoreType.DMA((n, 6)), pltpu.SemaphoreType.DMA((n,))]
        self.aliases = {}

    def _copy(self, outs, sems, a, k, q, pc, to, src=None):
        dst = _half_of_quarter(outs[a], self.kinds[a], q, pc)
        return pltpu.make_async_remote_copy(src_ref=dst if src is None else src, dst_ref=dst,
                                            send_sem=sems[0].at[a, k], recv_sem=sems[1].at[a, k],
                                            device_id=to, device_id_type=MESH)

    def _mine(self, ins, outs, sems, a, p):
        return pltpu.make_async_copy(ins[a], _quarter(outs[a], self.kinds[a], p), sems[2].at[a])

    def start(self, ins, outs, sems):
        x, y, c, chips = _place()
        p = 2 * x + y
        for a in range(len(ins)):
            self._mine(ins, outs, sems, a, p).start()
            for j, chip in enumerate(chips):
                self._copy(outs, sems, a, j, p, c, (*chip, c), src=_rows_half(ins[a], c)).start()

    def relay(self, ins, outs, sems):
        x, y, c, chips = _place()
        sibling = (x, y, 1 - c)
        for a in range(len(ins)):
            for j, (qx, qy) in enumerate(chips):
                q = 2 * qx + qy
                self._copy(outs, sems, a, j, q, c, sibling).wait_recv()
                self._copy(outs, sems, a, 3 + j, q, c, sibling).start()

    def finish(self, ins, outs, sems):
        x, y, c, chips = _place()
        p = 2 * x + y
        sibling = (x, y, 1 - c)
        n = len(ins)
        for a in range(n):
            for j, (qx, qy) in enumerate(chips):
                q = 2 * qx + qy
                self._copy(outs, sems, a, 3 + j, q, 1 - c, sibling).wait_recv()
                self._copy(outs, sems, a, j, p, c, (qx, qy, c), src=_rows_half(ins[a], c)).wait_send()
                self._copy(outs, sems, a, 3 + j, q, c, sibling).wait_send()
            self._mine(ins, outs, sems, a, p).wait()


class _ToSibling:
    def __init__(self, grads, kinds):
        self.ins = list(grads)
        self.kinds = list(kinds)
        n = len(self.ins)
        self.out_shape = []
        for g, k in zip(self.ins, self.kinds):
            qr, qc = _quarter_shape(g.shape, k)
            self.out_shape.append(jax.ShapeDtypeStruct((NS, qr // 2, qc), g.dtype))
        self.scratch = [pltpu.SemaphoreType.DMA((n, NS)), pltpu.SemaphoreType.DMA((n, NS))]
        self.aliases = {}

    def _copies(self, ins, outs, sems):
        x, y, c, _ = _place()
        return [pltpu.make_async_remote_copy(src_ref=_half_of_quarter(ins[a], self.kinds[a], q, 1 - c), dst_ref=outs[a].at[q],
                                             send_sem=sems[0].at[a, q], recv_sem=sems[1].at[a, q],
                                             device_id=(x, y, 1 - c), device_id_type=MESH)
                for a in range(len(ins)) for q in range(NS)]

    def start(self, ins, outs, sems):
        for cp in self._copies(ins, outs, sems):
            cp.start()

    def finish(self, ins, outs, sems):
        for cp in self._copies(ins, outs, sems):
            cp.wait()


class _ToChips:
    def __init__(self, sums, which=(0, 1, 2)):
        self.ins = list(sums)
        self.which = tuple(which)
        n, m = len(self.ins), len(self.which)
        self.out_shape = [jax.ShapeDtypeStruct((m,) + s.shape[1:], s.dtype) for s in self.ins]
        self.scratch = [pltpu.SemaphoreType.DMA((n, m)), pltpu.SemaphoreType.DMA((n, m))]
        self.aliases = {}

    def _copies(self, ins, outs, sems):
        x, y, c, chips = _place()
        return [pltpu.make_async_remote_copy(src_ref=ins[a].at[2 * chips[j][0] + chips[j][1]], dst_ref=outs[a].at[k],
                                             send_sem=sems[0].at[a, k], recv_sem=sems[1].at[a, k],
                                             device_id=(*chips[j], c), device_id_type=MESH)
                for a in range(len(ins)) for k, j in enumerate(self.which)]

    def start(self, ins, outs, sems):
        for cp in self._copies(ins, outs, sems):
            cp.start()

    def finish(self, ins, outs, sems):
        for cp in self._copies(ins, outs, sems):
            cp.wait()


class _SwapHalves:
    def __init__(self, quarters):
        self.ins = list(quarters)
        n = len(self.ins)
        self.out_shape = [jax.ShapeDtypeStruct(g.shape, g.dtype) for g in self.ins]
        self.scratch = [pltpu.SemaphoreType.DMA((n,)), pltpu.SemaphoreType.DMA((n,))]
        self.aliases = {a: a for a in range(n)}

    def _copy(self, outs, sems, a, pc):
        x, y, c, _ = _place()
        rows = _rows_half(outs[a], pc)
        return pltpu.make_async_remote_copy(src_ref=rows, dst_ref=rows, send_sem=sems[0].at[a], recv_sem=sems[1].at[a],
                                            device_id=(x, y, 1 - c), device_id_type=MESH)

    def start(self, ins, outs, sems):
        c = lax.axis_index("c")
        for a in range(len(outs)):
            self._copy(outs, sems, a, c).start()

    def finish(self, ins, outs, sems):
        c = lax.axis_index("c")
        for a in range(len(outs)):
            self._copy(outs, sems, a, c).wait_send()
            self._copy(outs, sems, a, 1 - c).wait_recv()


def _call(name, grid, compute, in_specs, out_specs, out_shape, scratch, vmem_mb, args, jobs=(), own_aliases=None):
    n_in, n_out, n_scr = len(in_specs), len(out_specs), len(scratch)
    ji = [len(j.ins) for j in jobs]
    jo = [len(j.out_shape) for j in jobs]
    js = [len(j.scratch) for j in jobs]

    def body(*refs):
        pos = [0]

        def take(k):
            r = refs[pos[0]:pos[0] + k]
            pos[0] += k
            return r

        ins, jins = take(n_in), [take(k) for k in ji]
        outs, jouts = take(n_out), [take(k) for k in jo]
        scr, jscr = take(n_scr), [take(k) for k in js]
        if jobs and grid:
            ids = [pl.program_id(a) for a in range(len(grid))]
            first = functools.reduce(jnp.logical_and, [i == 0 for i in ids])
            last = functools.reduce(jnp.logical_and, [i == g - 1 for i, g in zip(ids, grid)])

            @pl.when(first)
            def _():
                for j, a, b, c in zip(jobs, jins, jouts, jscr):
                    j.start(a, b, c)

            @pl.when(last)
            def _():
                for j, a, b, c in zip(jobs, jins, jouts, jscr):
                    if hasattr(j, "relay"):
                        j.relay(a, b, c)
        elif jobs:
            for j, a, b, c in zip(jobs, jins, jouts, jscr):
                j.start(a, b, c)
        compute(ins, outs, scr)
        if jobs and not grid:
            for j, a, b, c in zip(jobs, jins, jouts, jscr):
                if hasattr(j, "relay"):
                    j.relay(a, b, c)
        if jobs and grid:
            @pl.when(last)
            def _():
                for j, a, b, c in zip(jobs, jins, jouts, jscr):
                    j.finish(a, b, c)
        elif jobs:
            for j, a, b, c in zip(jobs, jins, jouts, jscr):
                j.finish(a, b, c)

    aliases = dict(own_aliases or {})
    in_off, out_off = n_in, n_out
    for j, a, b in zip(jobs, ji, jo):
        for s, d in j.aliases.items():
            aliases[in_off + s] = out_off + d
        in_off += a
        out_off += b
    res = pl.pallas_call(
        body, name=name, grid=grid,
        in_specs=list(in_specs) + [IN_HBM] * sum(ji), out_specs=list(out_specs) + [IN_HBM] * sum(jo),
        out_shape=list(out_shape) + [pltpu.HBM(s.shape, s.dtype) for j in jobs for s in j.out_shape],
        scratch_shapes=list(scratch) + [s for j in jobs for s in j.scratch],
        input_output_aliases=aliases, compiler_params=_cparams(len(grid), vmem_mb),
    )(*args, *[a for j in jobs for a in j.ins])
    res = list(res)
    main, rest, jres = res[:n_out], res[n_out:], []
    for k in jo:
        jres.append(rest[:k])
        rest = rest[k:]
    return main, jres


def _cast_bf16(arrays, name, jobs=()):
    n = len(arrays)

    def compute(ins, outs, scr):
        for i, o in zip(ins, outs):
            o[...] = i[...].astype(BF16)

    whole = [pl.BlockSpec(a.shape, lambda: (0, 0)) for a in arrays]
    return _call(name, (), compute, whole, whole, [jax.ShapeDtypeStruct(a.shape, BF16) for a in arrays], [], 48,
                 list(arrays), jobs)


def _exchange(name, jobs):
    return _call(name, (), lambda ins, outs, scr: None, [], [], [], [], 16, [], jobs)[1]


def _small_rows(ka, kb):
    first_a = 8
    first_b = first_a + -(-ka // 8) * 8
    return first_a, first_b, first_b + -(-kb // 8) * 8


LOSS_ROW = 7


def _allreduce_small(vecs, taps_a, taps_b, loss_part):
    counts = [len(v) for v in vecs]
    flat = [r for v in vecs for r in v]
    n = len(flat)
    C = flat[0].shape[1]
    NQ, KA, CB = taps_a.shape
    KB = taps_b.shape[1]
    first_a, first_b, R = _small_rows(KA, KB)
    assert len(vecs) <= LOSS_ROW < first_a
    N = 8

    def body(*refs):
        vec_refs = list(refs[:n])
        ta_ref, tb_ref, loss_ref, out_ref, v_ref, gath, send_sems, recv_sems, local_sem = refs[n:]
        v_ref[...] = jnp.zeros_like(v_ref)
        v_ref[LOSS_ROW:LOSS_ROW + 1, 0:loss_ref.shape[1]] = loss_ref[0:1, :]
        for i, k in enumerate(counts):
            parts, vec_refs = vec_refs[:k], vec_refs[k:]
            v_ref[i:i + 1, :] = functools.reduce(lambda a, b: a + b, [r[...] for r in parts])
        for q in range(NQ):
            v_ref[first_a:first_a + KA, q * CB:(q + 1) * CB] = ta_ref[q]
            v_ref[first_b:first_b + KB, q * CB:(q + 1) * CB] = tb_ref[q]
        x, y, c, chips = _place()
        me, sibling = (x, y, c), (x, y, 1 - c)

        def rows(px, py, pc):
            return gath.at[pl.ds((4 * px + 2 * py + pc) * R, R), :]

        def copy(k, block, to, src=None):
            return pltpu.make_async_remote_copy(src_ref=rows(*block) if src is None else src, dst_ref=rows(*block),
                                                send_sem=send_sems.at[k], recv_sem=recv_sems.at[k],
                                                device_id=to, device_id_type=MESH)

        mine = pltpu.make_async_copy(v_ref, rows(*me), local_sem)
        mine.start()
        first = [copy(0, me, sibling, src=v_ref)]
        first += [copy(1 + j, me, (*chip, c), src=v_ref) for j, chip in enumerate(chips)]
        for cp in first:
            cp.start()
        passed = [copy(4 + j, (*chip, c), sibling) for j, chip in enumerate(chips)]
        for j, chip in enumerate(chips):
            copy(1 + j, (*chip, c), me).wait_recv()
            passed[j].start()
        copy(0, sibling, me).wait_recv()
        for j, chip in enumerate(chips):
            copy(4 + j, (*chip, 1 - c), me).wait_recv()
        for cp in first + passed:
            cp.wait_send()
        mine.wait()
        acc = gath[0:R, :]
        for d in range(1, N):
            acc = acc + gath[d * R:(d + 1) * R, :]
        out_ref[...] = acc

    vmem = pl.BlockSpec(memory_space=pltpu.VMEM)
    return pl.pallas_call(
        body, name="allreduce_small",
        in_specs=[vmem] * (n + 3), out_specs=vmem,
        out_shape=jax.ShapeDtypeStruct((R, C), F32),
        scratch_shapes=[pltpu.VMEM((R, C), F32), pltpu.VMEM((N * R, C), F32), pltpu.SemaphoreType.DMA((7,)),
                        pltpu.SemaphoreType.DMA((7,)), pltpu.SemaphoreType.DMA],
    )(*flat, taps_a, taps_b, loss_part)


def _adamw_math(w, g, m, v):
    c1 = 1.0 - ADAM_B1 ** ADAM_STEP
    c2 = 1.0 - ADAM_B2 ** ADAM_STEP
    mn = ADAM_B1 * m + (1.0 - ADAM_B1) * g
    vn = ADAM_B2 * v + (1.0 - ADAM_B2) * (g * g)
    return -ADAM_LR * ((mn / c1) / (jnp.sqrt(vn / c2) + ADAM_EPS) + ADAM_WD * w), mn, vn


def _small_adamw(place, small, vec_wmv, tap_wmv):
    n = len(vec_wmv)
    D = small.shape[1]
    CB = tap_wmv[0][0].shape[2]
    ks = [t[0].shape[1] for t in tap_wmv]
    firsts = _small_rows(*ks)[:2]

    def body(place_ref, small_ref, *refs):
        ins, outs = refs[:3 * (n + 2)], refs[3 * (n + 2):]
        chip = place_ref[0]
        for i in range(n):
            g = small_ref[i:i + 1, :]
            d, mn, vn = _adamw_math(ins[3 * i][...], g, ins[3 * i + 1][...], ins[3 * i + 2][...])
            for o, val in zip(outs[4 * i:4 * i + 4], (g, d, mn, vn)):
                o[...] = val
        for t, (row0, k) in enumerate(zip(firsts, ks)):
            g = jnp.zeros((k, CB), F32)
            for q in range(D // CB):
                g = g + jnp.where(chip == q, small_ref[row0:row0 + k, q * CB:(q + 1) * CB], 0.0)
            w_ref, m_ref, v_ref = ins[3 * (n + t):3 * (n + t) + 3]
            d, mn, vn = _adamw_math(w_ref[0], g, m_ref[0], v_ref[0])
            for o, val in zip(outs[4 * (n + t):4 * (n + t) + 4], (g, d, mn, vn)):
                o[0] = val

    flat = [a for wmv in list(vec_wmv) + list(tap_wmv) for a in wmv]
    shapes = [jax.ShapeDtypeStruct(wmv[0].shape, F32) for wmv in list(vec_wmv) + list(tap_wmv) for _ in range(4)]
    vmem = pl.BlockSpec(memory_space=pltpu.VMEM)
    res = pl.pallas_call(
        body, name="small_adamw",
        in_specs=[pl.BlockSpec(memory_space=pltpu.SMEM)] + [vmem] * (1 + len(flat)), out_specs=[vmem] * len(shapes),
        out_shape=shapes,
    )(place, small, *flat)
    return [res[4 * i:4 * i + 4] for i in range(n + 2)]


def _ffn_fwd(h, g, wg, wu, wd, name, jobs=(), head=None):
    S, D = h.shape
    F = wg.shape[0]
    ts = _tile(S, 512)
    fb = _tile(F, F // 2)
    nf = F // fb

    def compute(ins, outs, scr):
        h_ref, g_ref, wg_ref, wu_ref, wd_ref = ins[:5]
        n_ref, gp_ref, up_ref = outs[-3:]
        x = h_ref[...]
        n = _rms_fwd(x, g_ref[...]).astype(BF16)
        n_ref[...] = n
        acc = None
        for j in range(nf):
            cols = slice(j * fb, (j + 1) * fb)
            gp = _dot_nt(n, wg_ref[cols, :])
            up = _dot_nt(n, wu_ref[cols, :])
            gp_ref[:, cols] = gp.astype(BF16)
            up_ref[:, cols] = up.astype(BF16)
            part = _dot((gp * _sigmoid(gp) * up).astype(BF16), wd_ref[cols, :])
            acc = part if acc is None else acc + part
        ho = x + 0.5 * acc
        if head is None:
            outs[0][...] = ho
            return
        t_ref, gf_ref = ins[5:]
        dh_ref, do_ref, dgf_ref, loss_ref = outs[:4]

        @pl.when(pl.program_id(0) == 0)
        def _():
            dgf_ref[...] = jnp.zeros_like(dgf_ref)
            loss_ref[...] = jnp.zeros_like(loss_ref)

        err = _rms_fwd(ho, gf_ref[...]) - t_ref[...]
        loss_ref[...] += (0.5 / D) * jnp.sum(err * err)
        dx, dg = _rms_bwd(ho, gf_ref[...], err * (1.0 / D))
        dh_ref[...] = dx
        do_ref[...] = (0.5 * dx).astype(BF16)
        dgf_ref[...] += dg

    tok = pl.BlockSpec((ts, D), lambda i: (i, 0))
    row = pl.BlockSpec((1, D), lambda i: (0, 0))
    wsp = _resident((F, D))
    hid = pl.BlockSpec((ts, F), lambda i: (i, 0))
    saved = [jax.ShapeDtypeStruct((S, D), BF16), jax.ShapeDtypeStruct((S, F), BF16), jax.ShapeDtypeStruct((S, F), BF16)]
    if head is None:
        return _call(name, (S // ts,), compute, [tok, row, wsp, wsp, wsp], [tok, tok, hid, hid],
                     [jax.ShapeDtypeStruct((S, D), F32)] + saved, [], 56, [h, g, wg, wu, wd], jobs)
    return _call(name, (S // ts,), compute, [tok, row, wsp, wsp, wsp, tok, row],
                 [tok, tok, row, pl.BlockSpec((8, 128), lambda i: (0, 0)), tok, hid, hid],
                 [jax.ShapeDtypeStruct((S, D), F32), jax.ShapeDtypeStruct((S, D), BF16), jax.ShapeDtypeStruct((1, D), F32),
                  jax.ShapeDtypeStruct((8, 128), F32)] + saved, [], 60, [h, g, wg, wu, wd, *head], jobs)


def _ffn_bwd_hidden(do, gp, up, wd, name, jobs=()):
    S, D = do.shape
    F = wd.shape[0]
    ts = _tile(S, 1024)
    fb = _tile(F, F // 2)
    def compute(ins, outs, scr):
        do_ref, gp_ref, up_ref, wd_ref = ins
        dgp_ref, dup_ref, a_ref = outs
        parts = _row_parts(ts, 4)
        das = [_dot_nt(do_ref[rows, :], wd_ref[...]) for rows in parts]
        for rows, da in zip(parts, das):
            gf = gp_ref[rows, :].astype(F32)
            uf = up_ref[rows, :].astype(F32)
            sg = _sigmoid(gf)
            si = gf * sg
            dgp_ref[rows, :] = (da * uf * (sg * (1.0 + gf * (1.0 - sg)))).astype(BF16)
            dup_ref[rows, :] = (da * si).astype(BF16)
            a_ref[rows, :] = (si * uf).astype(BF16)

    tok = pl.BlockSpec((ts, D), lambda s, i: (i, 0))
    hid = pl.BlockSpec((ts, fb), lambda s, i: (i, s))
    return _call(name, (F // fb, S // ts), compute, [tok, hid, hid, pl.BlockSpec((fb, D), lambda s, i: (s, 0))],
                 [hid, hid, hid], [jax.ShapeDtypeStruct((S, F), BF16)] * 3, [], 56, [do, gp, up, wd], jobs)


def _ffn_bwd_input(dh, h, g, dgp, dup, wg, wu, name, jobs=(), part=(0, 1), into=None):
    S, D = h.shape
    F = wg.shape[0]
    ts = _tile(S, 512)
    steps = S // ts // part[1]
    first = part[0] * steps

    def compute(ins, outs, scr):
        dh_ref, h_ref, g_ref, dgp_ref, dup_ref, wg_ref, wu_ref = ins[:7]
        dhi_ref, dg_ref = outs

        @pl.when(pl.program_id(0) == 0)
        def _():
            dg_ref[...] = jnp.zeros_like(dg_ref)

        dn = _dot(dgp_ref[...], wg_ref[...]) + _dot(dup_ref[...], wu_ref[...])
        dx, dg = _rms_bwd(h_ref[...], g_ref[...], dn)
        dhi_ref[...] = dh_ref[...] + dx
        dg_ref[...] += dg

    tok = pl.BlockSpec((ts, D), lambda i: (first + i, 0))
    hid = pl.BlockSpec((ts, F), lambda i: (first + i, 0))
    row = pl.BlockSpec((1, D), lambda i: (0, 0))
    in_specs = [tok, tok, row, hid, hid, _resident((F, D)), _resident((F, D))]
    args = [dh, h, g, dgp, dup, wg, wu]
    if into is not None:
        in_specs, args = in_specs + [IN_HBM], args + [into]
    return _call(name, (steps,), compute, in_specs, [tok, row],
                 [jax.ShapeDtypeStruct((S, D), F32), jax.ShapeDtypeStruct((1, D), F32)], [], 56, args, jobs,
                 own_aliases=None if into is None else {7: 0})


def _ffn_wgrad(hid, tok, name, jobs=()):
    S, D = tok.shape
    F = hid.shape[1]
    fb = _tile(F, F // 2)

    def compute(ins, outs, scr):
        outs[0][...] = _dot_tn(ins[0][...], ins[1][...])

    main, jres = _call(name, (F // fb,), compute,
                       [pl.BlockSpec((S, fb), lambda j: (0, j)), _resident(tok.shape)],
                       [pl.BlockSpec((fb, D), lambda j: (j, 0))], [jax.ShapeDtypeStruct((F, D), F32)], [], 56,
                       [hid, tok], jobs)
    return main[0], jres


def _w_in_pieces(D, cq, ng):
    groups = []
    for k in range(ng):
        lo, hi, pieces = k * D, (k + 1) * D, []
        while lo < hi:
            q = lo // cq
            w = min(hi, (q + 1) * cq) - lo
            pieces.append((q, lo - q * cq, w, lo - k * D))
            lo += w
        groups.append(pieces)
    return groups


def _mix_in_fwd(h, g, win, jobs=()):
    S, D = h.shape
    NG = win.shape[0] * win.shape[2] // D
    pieces = _w_in_pieces(D, win.shape[2], NG)
    ts = _tile(S, 512)

    def compute(ins, outs, scr):
        h_ref, g_ref, w_ref = ins
        u_ref, z_ref = outs
        u = _rms_fwd(h_ref[...], g_ref[...]).astype(BF16)
        u_ref[...] = u
        for k in range(NG):
            for q, c0, w, d0 in pieces[k]:
                z_ref[k, :, d0:d0 + w] = _dot(u, w_ref[q, :, c0:c0 + w]).astype(BF16)

    return _call("mix_in_fwd", (S // ts,), compute,
                 [pl.BlockSpec((ts, D), lambda i: (i, 0)), pl.BlockSpec((1, D), lambda i: (0, 0)), _resident(win.shape)],
                 [pl.BlockSpec((ts, D), lambda i: (i, 0)), pl.BlockSpec((NG, ts, D), lambda i: (0, i, 0))],
                 [jax.ShapeDtypeStruct((S, D), BF16), jax.ShapeDtypeStruct((NG, S, D), BF16)],
                 [], 48, [h, g, win], jobs)


SUBLANES = 8


def _shifted_copies(s):
    n = s.shape[1] - SUBLANES
    for r in range(1, SUBLANES):
        s[r, 0:n, :] = s[0, r:r + n, :]


def _window(s, o, rows):
    r = o % SUBLANES
    return s[r, o - r:o - r + rows, :]


def _conv_fwd(z, wa, ba, wb, jobs=()):
    _, S, D = z.shape
    _, KA, CB = wa.shape
    KB = wb.shape[1]
    ts = _tile(S, 1024)
    r = ts // HALO
    CH = min(64, ts)

    def compute(ins, outs, scr):
        z_ref, zh_ref, wa_ref, ba_ref, wb_ref = ins
        a1_ref, q_ref = outs
        sa, sb = scr
        keep = (pl.program_id(1) > 0).astype(F32)
        sa[0, HALO:HALO + ts, :] = z_ref[0].astype(F32) * _sigmoid(z_ref[1].astype(F32))
        sa[0, 0:HALO, :] = zh_ref[0].astype(F32) * _sigmoid(zh_ref[1].astype(F32)) * keep
        _shifted_copies(sa)
        sb[HALO:HALO + ts, :] = z_ref[3].astype(F32) * z_ref[4].astype(F32)
        sb[0:HALO, :] = zh_ref[3].astype(F32) * zh_ref[4].astype(F32) * keep
        wak = [wa_ref[0, k:k + 1, :] for k in range(KA)]
        wbk = [wb_ref[0, k:k + 1, :] for k in range(KB)]
        for c0 in range(0, ts, CH):
            acc = jnp.broadcast_to(ba_ref[...], (CH, CB))
            for k in range(KA):
                acc = acc + wak[k] * _window(sa, c0 + HALO - (KA - 1) + k, CH)
            a1_ref[c0:c0 + CH, :] = acc
            v = jnp.zeros((CH, CB), F32)
            for k in range(KB):
                o = c0 + HALO - (KB - 1) + k
                v = v + wbk[k] * sb[o:o + CH, :]
            q_ref[c0:c0 + CH, :] = (z_ref[2, c0:c0 + CH, :].astype(F32) * v).astype(BF16)

    return _call("conv_fwd", (D // CB, S // ts), compute,
                 [pl.BlockSpec((5, ts, CB), lambda j, i: (0, i, j)),
                  pl.BlockSpec((5, HALO, CB), lambda j, i: (0, jnp.maximum(i * r - 1, 0), j)),
                  pl.BlockSpec((1, KA, CB), lambda j, i: (j, 0, 0)), pl.BlockSpec((1, CB), lambda j, i: (0, j)),
                  pl.BlockSpec((1, KB, CB), lambda j, i: (j, 0, 0))],
                 [pl.BlockSpec((ts, CB), lambda j, i: (i, j)), pl.BlockSpec((ts, CB), lambda j, i: (i, j))],
                 [jax.ShapeDtypeStruct((S, D), F32), jax.ShapeDtypeStruct((S, D), BF16)],
                 [pltpu.VMEM((SUBLANES, HALO + ts, CB), F32), pltpu.VMEM((HALO + ts, CB), F32)], 40, [z, z, wa, ba, wb], jobs)


def _ln_stats(a1):
    mu = jnp.mean(a1, axis=-1, keepdims=True)
    xc = a1 - mu
    rstd = lax.rsqrt(jnp.mean(xc * xc, axis=-1, keepdims=True) + EPS)
    return xc * rstd, rstd


def _mix_out_fwd(h1, a1, q, z, lng, lnb, wa, wb, wo, jobs=()):
    S, D = h1.shape
    ts = _tile(S, 512)

    def compute(ins, outs, scr):
        h_ref, a1_ref, q_ref, ga_ref, gb_ref, lng_ref, lnb_ref, wa_ref, wb_ref, wo_ref = ins
        h2_ref, a3_ref, m_ref, ya_ref, yb_ref = outs
        xhat, _ = _ln_stats(a1_ref[...])
        a2 = xhat * lng_ref[...] + lnb_ref[...]
        a3 = (a2 * _sigmoid(a2)).astype(BF16)
        a3_ref[...] = a3
        ya = _dot(a3, wa_ref[...])
        yb = _dot(q_ref[...], wb_ref[...])
        ya_ref[...] = ya.astype(BF16)
        yb_ref[...] = yb.astype(BF16)
        m = (_sigmoid(ga_ref[0].astype(F32)) * ya + _sigmoid(gb_ref[0].astype(F32)) * yb).astype(BF16)
        m_ref[...] = m
        h2_ref[...] = h_ref[...] + _dot(m, wo_ref[...])

    tok = pl.BlockSpec((ts, D), lambda i: (i, 0))
    row = pl.BlockSpec((1, D), lambda i: (0, 0))
    mat = _resident((D, D))
    return _call("mix_out_fwd", (S // ts,), compute,
                 [tok, tok, tok, pl.BlockSpec((1, ts, D), lambda i: (5, i, 0)), pl.BlockSpec((1, ts, D), lambda i: (6, i, 0)),
                  row, row, mat, mat, mat], [tok] * 5,
                 [jax.ShapeDtypeStruct((S, D), F32)] + [jax.ShapeDtypeStruct((S, D), BF16)] * 4,
                 [], 56, [h1, a1, q, z, z, lng, lnb, wa, wb, wo], jobs)


def _mix_out_bwd(dh2, a1, z, ya, yb, lng, lnb, wa, wb, wo, jobs=()):
    S, D = dh2.shape
    ts = _tile(S, 512)

    def compute(ins, outs, scr):
        dh_ref, a1_ref, ga_ref, gb_ref, ya_ref, yb_ref, lng_ref, lnb_ref, wa_ref, wb_ref, wo_ref = ins
        da1_ref, dq_ref, dga_ref, dgb_ref, dya_ref, dyb_ref, dhb_ref, dlg_ref, dlb_ref = outs

        @pl.when(pl.program_id(0) == 0)
        def _():
            dlg_ref[...] = jnp.zeros_like(dlg_ref)
            dlb_ref[...] = jnp.zeros_like(dlb_ref)

        for rows in _row_parts(ts):
            dhb = dh_ref[rows, :].astype(BF16)
            dhb_ref[rows, :] = dhb
            dm = _dot_nt(dhb, wo_ref[...])
            sa = _sigmoid(ga_ref[0, rows, :].astype(F32))
            sb = _sigmoid(gb_ref[0, rows, :].astype(F32))
            dga_ref[rows, :] = (dm * ya_ref[rows, :].astype(F32) * sa * (1.0 - sa)).astype(BF16)
            dgb_ref[rows, :] = (dm * yb_ref[rows, :].astype(F32) * sb * (1.0 - sb)).astype(BF16)
            dya = (sa * dm).astype(BF16)
            dyb = (sb * dm).astype(BF16)
            dya_ref[rows, :] = dya
            dyb_ref[rows, :] = dyb
            dq_ref[rows, :] = _dot_nt(dyb, wb_ref[...]).astype(BF16)
            da3 = _dot_nt(dya, wa_ref[...])
            xhat, rstd = _ln_stats(a1_ref[rows, :])
            a2 = xhat * lng_ref[...] + lnb_ref[...]
            sg = _sigmoid(a2)
            da2 = da3 * (sg * (1.0 + a2 * (1.0 - sg)))
            dlg_ref[...] += jnp.sum(da2 * xhat, axis=0, keepdims=True)
            dlb_ref[...] += jnp.sum(da2, axis=0, keepdims=True)
            dxh = da2 * lng_ref[...]
            da1_ref[rows, :] = (rstd * (dxh - jnp.mean(dxh, axis=-1, keepdims=True)
                                        - xhat * jnp.mean(dxh * xhat, axis=-1, keepdims=True))).astype(BF16)

    tok = pl.BlockSpec((ts, D), lambda i: (i, 0))
    row = pl.BlockSpec((1, D), lambda i: (0, 0))
    mat = _resident((D, D))
    return _call("mix_out_bwd", (S // ts,), compute,
                 [tok, tok, pl.BlockSpec((1, ts, D), lambda i: (5, i, 0)), pl.BlockSpec((1, ts, D), lambda i: (6, i, 0)),
                  tok, tok, row, row, mat, mat, mat], [tok] * 7 + [row, row],
                 [jax.ShapeDtypeStruct((S, D), BF16)] * 7 + [jax.ShapeDtypeStruct((1, D), F32)] * 2,
                 [], 56, [dh2, a1, z, z, ya, yb, lng, lnb, wa, wb, wo], jobs)


def _mixer_wgrads(a3, dya, q, dyb, mm, dhb, jobs=()):
    S, D = a3.shape
    tk = _tile(S, 512)

    def compute(ins, outs, scr):
        @pl.when(pl.program_id(0) == 0)
        def _():
            for o in outs:
                o[...] = jnp.zeros_like(o)

        for t in range(3):
            outs[t][...] += _dot_tn(ins[2 * t][...], ins[2 * t + 1][...])

    tok = pl.BlockSpec((tk, D), lambda k: (k, 0))
    return _call("mixer_wgrads", (S // tk,), compute, [tok] * 6, [pl.BlockSpec((D, D), lambda k: (0, 0))] * 3,
                 [jax.ShapeDtypeStruct((D, D), F32)] * 3, [], 56, [a3, dya, q, dyb, mm, dhb], jobs)


def _conv_bwd(z, da1, dq, dga, dgb, wa, wb, jobs=()):
    NG, S, D = z.shape
    _, KA, CB = wa.shape
    KB = wb.shape[1]
    ts = _tile(S, 1024)
    r = ts // HALO
    nt = S // ts
    CH = min(64, ts)
    last_halo = S // HALO - 1

    def compute(ins, outs, scr):
        z_ref, zp_ref, zn_ref, da1_ref, da1n_ref, dq_ref, dqn_ref, dga_ref, dgb_ref, wa_ref, wb_ref = ins
        dz_ref, dwa_ref, dba_ref, dwb_ref = outs
        sa0, sd, sp, sv, acca, accb = scr
        i = pl.program_id(1)
        prev = (i > 0).astype(F32)
        nxt = (i < nt - 1).astype(F32)

        @pl.when(i == 0)
        def _():
            acca[...] = jnp.zeros_like(acca)
            accb[...] = jnp.zeros_like(accb)
            dba_ref[...] = jnp.zeros_like(dba_ref)

        sa0[0, HALO:HALO + ts, :] = z_ref[0].astype(F32) * _sigmoid(z_ref[1].astype(F32))
        sa0[0, 0:HALO, :] = zp_ref[0].astype(F32) * _sigmoid(zp_ref[1].astype(F32)) * prev
        _shifted_copies(sa0)
        sp[HALO:HALO + ts, :] = z_ref[3].astype(F32) * z_ref[4].astype(F32)
        sp[0:HALO, :] = zp_ref[3].astype(F32) * zp_ref[4].astype(F32) * prev
        sd[0, 0:ts, :] = da1_ref[...].astype(F32)
        sd[0, ts:ts + HALO, :] = da1n_ref[...].astype(F32) * nxt
        _shifted_copies(sd)
        sv[0:ts, :] = dq_ref[...].astype(F32) * z_ref[2].astype(F32)
        sv[ts:ts + HALO, :] = dqn_ref[...].astype(F32) * zn_ref[2].astype(F32) * nxt
        dba_ref[...] += jnp.sum(sd[0, 0:ts, :], axis=0, keepdims=True)
        wak = [wa_ref[0, k:k + 1, :] for k in range(KA)]
        wbk = [wb_ref[0, k:k + 1, :] for k in range(KB)]
        for c0 in range(0, ts, CH):
            rows = slice(c0, c0 + CH)
            d1 = sd[0, rows, :]
            da0 = jnp.zeros((CH, CB), F32)
            for k in range(KA):
                da0 = da0 + wak[k] * _window(sd, c0 + (KA - 1) - k, CH)
                a0w = _window(sa0, c0 + HALO - (KA - 1) + k, CH)
                acca[k] += jnp.sum((d1 * a0w).reshape(CH // 8, 8, CB), axis=0)
            val = z_ref[0, rows, :].astype(F32)
            sg = _sigmoid(z_ref[1, rows, :].astype(F32))
            dz_ref[0, rows, :] = (da0 * sg).astype(BF16)
            dz_ref[1, rows, :] = (da0 * val * sg * (1.0 - sg)).astype(BF16)
            dv = sv[rows, :]
            v = jnp.zeros((CH, CB), F32)
            dp = jnp.zeros((CH, CB), F32)
            for k in range(KB):
                o = c0 + HALO - (KB - 1) + k
                pw = sp[o:o + CH, :]
                v = v + wbk[k] * pw
                accb[k] += jnp.sum((dv * pw).reshape(CH // 8, 8, CB), axis=0)
                o = c0 + (KB - 1) - k
                dp = dp + wbk[k] * sv[o:o + CH, :]
            dz_ref[2, rows, :] = (dq_ref[rows, :].astype(F32) * v).astype(BF16)
            dz_ref[3, rows, :] = (dp * z_ref[4, rows, :].astype(F32)).astype(BF16)
            dz_ref[4, rows, :] = (dp * z_ref[3, rows, :].astype(F32)).astype(BF16)
        dz_ref[5] = dga_ref[...]
        dz_ref[6] = dgb_ref[...]

        @pl.when(i == nt - 1)
        def _():
            dwa_ref[0] = jnp.sum(acca[...], axis=1)
            dwb_ref[0] = jnp.sum(accb[...], axis=1)

    zt = pl.BlockSpec((5, ts, CB), lambda j, i: (0, i, j))
    zp = pl.BlockSpec((5, HALO, CB), lambda j, i: (0, jnp.maximum(i * r - 1, 0), j))
    zn = pl.BlockSpec((5, HALO, CB), lambda j, i: (0, jnp.minimum((i + 1) * r, last_halo), j))
    tok = pl.BlockSpec((ts, CB), lambda j, i: (i, j))
    tokn = pl.BlockSpec((HALO, CB), lambda j, i: (jnp.minimum((i + 1) * r, last_halo), j))
    return _call("conv_bwd", (D // CB, nt), compute,
                 [zt, zp, zn, tok, tokn, tok, tokn, tok, tok,
                  pl.BlockSpec((1, KA, CB), lambda j, i: (j, 0, 0)), pl.BlockSpec((1, KB, CB), lambda j, i: (j, 0, 0))],
                 [pl.BlockSpec((NG, ts, CB), lambda j, i: (0, i, j)), pl.BlockSpec((1, KA, CB), lambda j, i: (j, 0, 0)),
                  pl.BlockSpec((1, CB), lambda j, i: (0, j)), pl.BlockSpec((1, KB, CB), lambda j, i: (j, 0, 0))],
                 [jax.ShapeDtypeStruct((NG, S, D), BF16), jax.ShapeDtypeStruct((D // CB, KA, CB), F32),
                  jax.ShapeDtypeStruct((1, D), F32), jax.ShapeDtypeStruct((D // CB, KB, CB), F32)],
                 [pltpu.VMEM((SUBLANES, HALO + ts, CB), F32), pltpu.VMEM((SUBLANES, ts + HALO, CB), F32),
                  pltpu.VMEM((HALO + ts, CB), F32), pltpu.VMEM((ts + HALO, CB), F32),
                  pltpu.VMEM((KA, 8, CB), F32), pltpu.VMEM((KB, 8, CB), F32)],
                 48, [z, z, z, da1, da1, dq, dq, dga, dgb, wa, wb], jobs)


def _mix_in_bwd(dh2, h1, g, dz, win, jobs=()):
    S, D = h1.shape
    NG = dz.shape[0]
    pieces = _w_in_pieces(D, win.shape[2], NG)
    ts = _tile(S, 512)

    def compute(ins, outs, scr):
        dh_ref, h_ref, g_ref, dz_ref, w_ref = ins
        dhi_ref, dg_ref, do_ref = outs

        @pl.when(pl.program_id(0) == 0)
        def _():
            dg_ref[...] = jnp.zeros_like(dg_ref)

        du = None
        for k in range(NG):
            for q, c0, w, d0 in pieces[k]:
                part = _dot_nt(dz_ref[k, :, d0:d0 + w], w_ref[q, :, c0:c0 + w])
                du = part if du is None else du + part
        dx, dg = _rms_bwd(h_ref[...], g_ref[...], du)
        dhi = dh_ref[...] + dx
        dhi_ref[...] = dhi
        do_ref[...] = (0.5 * dhi).astype(BF16)
        dg_ref[...] += dg

    tok = pl.BlockSpec((ts, D), lambda i: (i, 0))
    row = pl.BlockSpec((1, D), lambda i: (0, 0))
    return _call("mix_in_bwd", (S // ts,), compute,
                 [tok, tok, row, pl.BlockSpec((NG, ts, D), lambda i: (0, i, 0)), _resident(win.shape)],
                 [tok, row, tok],
                 [jax.ShapeDtypeStruct((S, D), F32), jax.ShapeDtypeStruct((1, D), F32), jax.ShapeDtypeStruct((S, D), BF16)],
                 [], 56, [dh2, h1, g, dz, win], jobs)


def _w_in_grad(u, dz, jobs=()):
    S, D = u.shape
    NG = dz.shape[0]

    def compute(ins, outs, scr):
        outs[0][...] = _dot_tn(ins[0][...], ins[1][0])

    return _call("w_in_grad", (NG,), compute,
                 [_resident(u.shape), pl.BlockSpec((1, S, D), lambda j: (j, 0, 0))],
                 [pl.BlockSpec((D, D), lambda j: (0, j))], [jax.ShapeDtypeStruct((D, NG * D), F32)], [], 48, [u, dz], jobs)


def _chip_sums(place, grads, got, kind, name):
    n = len(grads)
    qr, qc = _quarter_shape(grads[0].shape, kind)
    h = qr // 2
    tr = _row_tile(h)
    nr = h // tr

    def body(pc_ref, *refs):
        g_refs, got_refs, b_refs, f_refs = refs[:n], refs[n:2 * n], refs[2 * n:3 * n], refs[3 * n:]
        own = pl.program_id(1) == pc_ref[0]
        for a in range(n):
            s = g_refs[a][...] + got_refs[a][0]
            b_refs[a][0] = s.astype(BF16)

            @pl.when(own)
            def _():
                f_refs[a][...] = s

    if kind == "rows":
        gspec = pl.BlockSpec((tr, qc), lambda r, q, pc: (q * (2 * nr) + pc[1] * nr + r, 0))
    else:
        gspec = pl.BlockSpec((tr, qc), lambda r, q, pc: (pc[1] * nr + r, q))
    lspec = pl.BlockSpec((1, tr, qc), lambda r, q, pc: (q, r, 0))
    res = pl.pallas_call(
        body, name=name,
        grid_spec=pltpu.PrefetchScalarGridSpec(
            num_scalar_prefetch=1, grid=(nr, NS), in_specs=[gspec] * n + [lspec] * n,
            out_specs=[lspec] * n + [pl.BlockSpec((tr, qc), lambda r, q, pc: (r, 0))] * n),
        out_shape=[jax.ShapeDtypeStruct((NS, h, qc), BF16)] * n + [jax.ShapeDtypeStruct((h, qc), F32)] * n,
        compiler_params=_cparams(2, 48),
    )(place, *grads, *got)
    return res[:n], res[n:]


def _totals(place, own, got, name):
    n = len(own)
    h, qc = own[0].shape
    tr = _row_tile(h)
    nr = h // tr
    got = [list(g) if isinstance(g, (list, tuple)) else [g] for g in got]
    m = len(got[0])

    def body(pc_ref, *refs):
        own_refs, got_refs, o_refs = refs[:n], refs[n:n + n * m], refs[n + n * m:]
        for a in range(n):
            acc = own_refs[a][...]
            for g in got_refs[a * m:(a + 1) * m]:
                for k in range(g.shape[0]):
                    acc = acc + g[k].astype(F32)
            o_refs[a][...] = acc

    lands = [pl.BlockSpec((g.shape[0], tr, qc), lambda r, pc: (0, r, 0)) for gs in got for g in gs]
    return pl.pallas_call(
        body, name=name,
        grid_spec=pltpu.PrefetchScalarGridSpec(
            num_scalar_prefetch=1, grid=(nr,),
            in_specs=[pl.BlockSpec((tr, qc), lambda r, pc: (r, 0))] * n + lands,
            out_specs=[pl.BlockSpec((tr, qc), lambda r, pc: (pc[1] * nr + r, 0))] * n),
        out_shape=[jax.ShapeDtypeStruct((2 * h, qc), F32)] * n,
        compiler_params=_cparams(1, 48),
    )(place, *own, *[g for gs in got for g in gs])


def _adamw(ws, gs, ms, vs, name):
    n = len(ws)
    R, C = ws[0].shape
    tr = _row_tile(R, (36 << 20) // (7 * 2 * 4 * n * C))

    def body(*refs):
        w_refs, g_refs, m_refs, v_refs = refs[:n], refs[n:2 * n], refs[2 * n:3 * n], refs[3 * n:4 * n]
        d_refs, mo_refs, vo_refs = refs[4 * n:5 * n], refs[5 * n:6 * n], refs[6 * n:]
        for a in range(n):
            d_refs[a][...], mo_refs[a][...], vo_refs[a][...] = _adamw_math(w_refs[a][...], g_refs[a][...], m_refs[a][...],
                                                                         v_refs[a][...])

    blk = pl.BlockSpec((tr, C), lambda r: (r, 0))
    res = pl.pallas_call(
        body, name=name, grid=(R // tr,),
        in_specs=[blk] * (4 * n), out_specs=[blk] * (3 * n),
        out_shape=[jax.ShapeDtypeStruct((R, C), F32)] * (3 * n),
        compiler_params=_cparams(1, 56),
    )(*ws, *gs, *ms, *vs)
    return res[:n], res[n:2 * n], res[2 * n:]


def kernel(x, ffn1_norm, ffn1_w_gate, ffn1_w_up, ffn1_w_down, mix_norm, w_in, a_dw_w, a_dw_b, a_ln_g, a_ln_b, a_w_out, b_conv_w, b_w_out, w_o, ffn2_norm, ffn2_w_gate, ffn2_w_up, ffn2_w_down, final_norm, loss_target, m_ffn1_norm, m_ffn1_w_gate, m_ffn1_w_up, m_ffn1_w_down, m_mix_norm, m_w_in, m_a_dw_w, m_a_dw_b, m_a_ln_g, m_a_ln_b, m_a_w_out, m_b_conv_w, m_b_w_out, m_w_o, m_ffn2_norm, m_ffn2_w_gate, m_ffn2_w_up, m_ffn2_w_down, m_final_norm, v_ffn1_norm, v_ffn1_w_gate, v_ffn1_w_up, v_ffn1_w_down, v_mix_norm, v_w_in, v_a_dw_w, v_a_dw_b, v_a_ln_g, v_a_ln_b, v_a_w_out, v_b_conv_w, v_b_w_out, v_w_o, v_ffn2_norm, v_ffn2_w_gate, v_ffn2_w_up, v_ffn2_w_down, v_final_norm):
    names = ["ffn1_norm", "ffn1_w_gate", "ffn1_w_up", "ffn1_w_down", "mix_norm", "w_in", "a_dw_w", "a_dw_b", "a_ln_g",
             "a_ln_b", "a_w_out", "b_conv_w", "b_w_out", "w_o", "ffn2_norm", "ffn2_w_gate", "ffn2_w_up", "ffn2_w_down",
             "final_norm"]
    W = dict(zip(names, [ffn1_norm, ffn1_w_gate, ffn1_w_up, ffn1_w_down, mix_norm, w_in, a_dw_w, a_dw_b, a_ln_g, a_ln_b,
                         a_w_out, b_conv_w, b_w_out, w_o, ffn2_norm, ffn2_w_gate, ffn2_w_up, ffn2_w_down, final_norm]))
    M = dict(zip(names, [m_ffn1_norm, m_ffn1_w_gate, m_ffn1_w_up, m_ffn1_w_down, m_mix_norm, m_w_in, m_a_dw_w, m_a_dw_b,
                         m_a_ln_g, m_a_ln_b, m_a_w_out, m_b_conv_w, m_b_w_out, m_w_o, m_ffn2_norm, m_ffn2_w_gate,
                         m_ffn2_w_up, m_ffn2_w_down, m_final_norm]))
    V = dict(zip(names, [v_ffn1_norm, v_ffn1_w_gate, v_ffn1_w_up, v_ffn1_w_down, v_mix_norm, v_w_in, v_a_dw_w, v_a_dw_b,
                         v_a_ln_g, v_a_ln_b, v_a_w_out, v_b_conv_w, v_b_w_out, v_w_o, v_ffn2_norm, v_ffn2_w_gate,
                         v_ffn2_w_up, v_ffn2_w_down, v_final_norm]))
    transposed = ("ffn1_w_gate", "ffn1_w_up", "ffn2_w_gate", "ffn2_w_up")
    vecs = ["ffn1_norm", "mix_norm", "a_dw_b", "a_ln_g", "a_ln_b", "ffn2_norm", "final_norm"]
    ffn1 = ["ffn1_w_gate", "ffn1_w_up", "ffn1_w_down"]
    ffn2 = ["ffn2_w_gate", "ffn2_w_up", "ffn2_w_down"]
    outp = ["a_w_out", "b_w_out", "w_o"]

    S, D = x.shape[1], x.shape[2]
    CB = D // NS
    KA, KB = a_dw_w.shape[1], b_conv_w.shape[1]
    px, py, pc = lax.axis_index("x"), lax.axis_index("y"), lax.axis_index("c")
    chip = 2 * px + py
    place = jnp.stack([chip, pc]).astype(jnp.int32)
    h0 = x.reshape(S, D)
    tgt = loss_target.reshape(S, D)
    row = lambda n: pltpu.with_memory_space_constraint(W[n].reshape(1, D), pltpu.HBM)
    pad = lambda a, r: jnp.concatenate([a, jnp.zeros((r - a.shape[0], a.shape[1]), F32)], axis=0)

    def quarter(P, n):
        return jnp.transpose(P[n][0]) if n in transposed else P[n][0]

    def unquarter(a, n):
        return (jnp.transpose(a) if n in transposed else a).reshape(W[n].shape)

    wq = {n: quarter(W, n).astype(BF16) for n in ffn1}

    later = ffn2 + outp + ["w_in"]
    casts, (f1,) = _cast_bf16([quarter(W, n) for n in later], "cast_weights", [_Gather([wq[n] for n in ffn1], ["rows"] * 3)])
    wq.update(zip(later, casts))
    g_in = _Gather([wq["w_in"], pad(a_dw_w[0], 32), pad(b_conv_w[0], 16)], ["rows"] * 3)
    (h1, n1, gp1, up1), ((win, taps_a, taps_b),) = _ffn_fwd(h0, row("ffn1_norm"), *f1, "ffn1_fwd", [g_in])
    win = win.reshape(NS, D, -1)
    wa_taps = taps_a.reshape(NS, 32, CB)[:, :KA]
    wb_taps = taps_b.reshape(NS, 16, CB)[:, :KB]
    g_out = _Gather([wq[n] for n in outp], ["rows"] * 3)
    (u, z), ((wa_out, wb_out, wo),) = _mix_in_fwd(h1, row("mix_norm"), win, [g_out])
    g_f2 = _Gather([wq["ffn2_w_gate"], wq["ffn2_w_up"]], ["rows"] * 2)
    (a1, q), ((f2g, f2u),) = _conv_fwd(z, wa_taps, row("a_dw_b"), wb_taps, [g_f2])
    (h2, a3, mm, ya, yb), ((f2d,),) = _mix_out_fwd(h1, a1, q, z, row("a_ln_g"), row("a_ln_b"), wa_out, wb_out, wo,
                                                   [_Gather([wq["ffn2_w_down"]], ["rows"])])
    (dh3, do2, d_final, loss_part, n2, gp2, up2), _ = _ffn_fwd(h2, row("ffn2_norm"), f2g, f2u, f2d, "ffn2_fwd_loss",
                                                               head=(tgt, row("final_norm")))

    (dgp2, dup2, act2), _ = _ffn_bwd_hidden(do2, gp2, up2, f2d, "ffn2_bwd_hidden")
    (dh2, d_ffn2), _ = _ffn_bwd_input(dh3, h2, row("ffn2_norm"), dgp2, dup2, f2g, f2u, "ffn2_bwd_input")
    g2 = [_ffn_wgrad(dgp2, n2, "ffn2_dwg")[0], _ffn_wgrad(dup2, n2, "ffn2_dwu")[0], _ffn_wgrad(act2, do2, "ffn2_dwd")[0]]
    (da1, dq, dga, dgb, dya, dyb, dh2b, d_lng, d_lnb), (got,) = _mix_out_bwd(
        dh2, a1, z, ya, yb, row("a_ln_g"), row("a_ln_b"), wa_out, wb_out, wo, [_ToSibling(g2, ["rows"] * 3)])
    wire2, own2 = _chip_sums(place, g2, got, "rows", "ffn2_chip_sums")
    (dz, d_wa, d_ba, d_wb), (got,) = _conv_bwd(z, da1, dq, dga, dgb, wa_taps, wb_taps, [_ToChips(wire2)])
    half2 = _totals(place, own2, got, "ffn2_totals")
    go, (tot2,) = _mixer_wgrads(a3, dya, q, dyb, mm, dh2b, [_SwapHalves(half2)])
    (g_win,), (got_o,) = _w_in_grad(u, dz, [_ToSibling(go, ["rows"] * 3)])
    wire_o, own_o = _chip_sums(place, go, got_o, "rows", "mixer_chip_sums")
    (dh1, d_mix, do1), (got, land_o) = _mix_in_bwd(dh2, h1, row("mix_norm"), dz, win,
                                                   [_ToSibling([g_win], ["cols"]), _ToChips(wire_o)])
    wire_in, own_in = _chip_sums(place, [g_win], got, "cols", "w_in_chip_sum")
    half_o = _totals(place, own_o, land_o, "mixer_totals")
    (dgp1, dup1, act1), (near_in, tot_o) = _ffn_bwd_hidden(do1, gp1, up1, f1[2], "ffn1_bwd_hidden",
                                                           [_ToChips(wire_in, (0, 1)), _SwapHalves(half_o)])
    g1g, (far_in,) = _ffn_wgrad(dgp1, n1, "ffn1_dwg", [_ToChips(wire_in, (2,))])
    half_in = _totals(place, own_in, [[near_in[0], far_in[0]]], "w_in_total")
    g1u, (tot_in, got_g) = _ffn_wgrad(dup1, n1, "ffn1_dwu", [_SwapHalves(half_in), _ToSibling([g1g], ["rows"])])
    wire_g, own_g = _chip_sums(place, [g1g], got_g, "rows", "ffn1_dwg_chip_sum")
    g1d, (got_u, land_g) = _ffn_wgrad(act1, do1, "ffn1_dwd", [_ToSibling([g1u], ["rows"]), _ToChips(wire_g)])
    wire_u, own_u = _chip_sums(place, [g1u], got_u, "rows", "ffn1_dwu_chip_sum")
    half_g = _totals(place, own_g, land_g, "ffn1_dwg_total")
    ffn1_in = (dh1, h0, row("ffn1_norm"), dgp1, dup1, f1[0], f1[1])
    (dx, dg_a), (got_d, land_u, tot_g) = _ffn_bwd_input(
        *ffn1_in, "ffn1_bwd_input_a", [_ToSibling([g1d], ["rows"]), _ToChips(wire_u), _SwapHalves(half_g)], part=(0, 2))
    wire_d, own_d = _chip_sums(place, [g1d], got_d, "rows", "ffn1_dwd_chip_sum")
    half_u = _totals(place, own_u, land_u, "ffn1_dwu_total")
    (dx, dg_b), (land_d, tot_u) = _ffn_bwd_input(*ffn1_in, "ffn1_bwd_input_b", [_ToChips(wire_d), _SwapHalves(half_u)],
                                                 part=(1, 2), into=dx)
    half_d = _totals(place, own_d, land_d, "ffn1_dwd_total")
    (tot_d,) = _exchange("tail_exchange", [_SwapHalves(half_d)])
    tot1 = [tot_g[0], tot_u[0], tot_d[0]]
    totals = dict(zip(ffn2 + ["w_in"] + ffn1 + outp, list(tot2) + list(tot_in) + tot1 + list(tot_o)))

    vec_grads = {"ffn1_norm": [dg_a, dg_b], "mix_norm": [d_mix], "a_dw_b": [d_ba], "a_ln_g": [d_lng], "a_ln_b": [d_lnb],
                 "ffn2_norm": [d_ffn2], "final_norm": [d_final]}
    small = _allreduce_small([vec_grads[n] for n in vecs], d_wa, d_wb, loss_part)
    loss = small[LOSS_ROW, 0]
    taps = ["a_dw_w", "b_conv_w"]
    small_out = _small_adamw(place, small, [[P[n].reshape(1, D) for P in (W, M, V)] for n in vecs],
                             [[P[n] for P in (W, M, V)] for n in taps])

    grads, deltas, new_m, new_v = {}, {}, {}, {}
    for n, (g_, d_, m_, v_) in zip(vecs + taps, small_out):
        shp = W[n].shape
        grads[n], deltas[n], new_m[n], new_v[n] = g_.reshape(shp), d_.reshape(shp), m_.reshape(shp), v_.reshape(shp)
    for group, tag in ((ffn1 + ffn2, "ffn"), (["w_in"], "w_in"), (outp, "mixer")):
        ds, ms, vs = _adamw([quarter(W, n) for n in group], [totals[n] for n in group], [quarter(M, n) for n in group],
                            [quarter(V, n) for n in group], tag + "_adamw")
        for n, d_, m_, v_ in zip(group, ds, ms, vs):
            grads[n], deltas[n], new_m[n], new_v[n] = (unquarter(totals[n], n), unquarter(d_, n), unquarter(m_, n),
                                                       unquarter(v_, n))
    return (loss, dx.reshape(x.shape), *[grads[n] for n in names], *[deltas[n] for n in names],
            *[new_m[n] for n in names], *[new_v[n] for n in names])
```

```python
import functools

import jax
import jax.numpy as jnp
from jax import lax
from jax.experimental import pallas as pl
from jax.experimental.pallas import tpu as pltpu

F32 = jnp.float32
BF16 = jnp.bfloat16
EPS = 1e-6
NS = 4
HALO = 32
MESH = pl.DeviceIdType.MESH
IN_HBM = pl.BlockSpec(memory_space=pltpu.HBM)

ADAM_LR = 0.001
ADAM_B1 = 0.9
ADAM_B2 = 0.999
ADAM_EPS = 1e-08
ADAM_WD = 0.01
ADAM_STEP = 10


def _cparams(n_axes, vmem_mb):
    return pltpu.CompilerParams(dimension_semantics=("arbitrary",) * n_axes, vmem_limit_bytes=vmem_mb << 20)


def _tile(n, t):
    return t if n % t == 0 else n


def _row_parts(rows, n=2):
    if rows % (16 * n):
        return [slice(0, rows)]
    return [slice(p * (rows // n), (p + 1) * (rows // n)) for p in range(n)]


def _resident(shape):
    return pl.BlockSpec(shape, lambda *_: (0,) * len(shape), pipeline_mode=pl.Buffered(1))


def _row_tile(n, cap=256):
    for t in (256, 176, 128, 64, 32, 16, 8):
        if t <= cap and n % t == 0:
            return t
    return n


def _dot(a, b):
    return jnp.dot(a, b, preferred_element_type=F32)


def _dot_nt(a, b):
    return lax.dot_general(a, b, (((1,), (1,)), ((), ())), preferred_element_type=F32)


def _dot_tn(a, b):
    return lax.dot_general(a, b, (((0,), (0,)), ((), ())), preferred_element_type=F32)


def _sigmoid(x):
    return jax.nn.sigmoid(x)


def _rms_fwd(x, g):
    r = lax.rsqrt(jnp.mean(x * x, axis=-1, keepdims=True) + EPS)
    return x * r * g


def _rms_bwd(x, g, dn):
    r = lax.rsqrt(jnp.mean(x * x, axis=-1, keepdims=True) + EPS)
    xr = x * r
    dg = jnp.sum(dn * xr, axis=0, keepdims=True)
    w = dn * g
    dx = r * w - xr * (r * r) * jnp.mean(x * w, axis=-1, keepdims=True)
    return dx, dg


def _place():
    x, y, c = lax.axis_index("x"), lax.axis_index("y"), lax.axis_index("c")
    chips = [(1 - x, y), (x, 1 - y), (1 - x, 1 - y)]
    return x, y, c, chips


def _quarter_shape(full_shape, kind):
    r, c = full_shape
    return (r // NS, c) if kind == "rows" else (r, c // NS)


def _half_of_quarter(ref, kind, q, pc):
    qr, qc = _quarter_shape(ref.shape, kind)
    h = qr // 2
    if kind == "rows":
        return ref.at[pl.ds(q * qr + pc * h, h), :]
    return ref.at[pl.ds(pc * h, h), pl.ds(q * qc, qc)]


def _quarter(ref, kind, q):
    qr, qc = _quarter_shape(ref.shape, kind)
    if kind == "rows":
        return ref.at[pl.ds(q * qr, qr), :]
    return ref.at[:, pl.ds(q * qc, qc)]


def _rows_half(ref, pc):
    h = ref.shape[0] // 2
    return ref.at[pl.ds(pc * h, h)]


class _Gather:
    def __init__(self, quarters, kinds):
        self.ins = list(quarters)
        self.kinds = list(kinds)
        n = len(self.ins)
        self.out_shape = [jax.ShapeDtypeStruct((NS * a.shape[0], a.shape[1]) if k == "rows" else (a.shape[0], NS * a.shape[1]),
                                               a.dtype) for a, k in zip(self.ins, self.kinds)]
        self.scratch = [pltpu.SemaphoreType.DMA((n, 6)), pltpu.SemaphoreType.DMA((n, 6)), pltpu.SemaphoreType.DMA((n,))]
        self.aliases = {}

    def _copy(self, outs, sems, a, k, q, pc, to, src=None):
        dst = _half_of_quarter(outs[a], self.kinds[a], q, pc)
        return pltpu.make_async_remote_copy(src_ref=dst if src is None else src, dst_ref=dst,
                                            send_sem=sems[0].at[a, k], recv_sem=sems[1].at[a, k],
                                            device_id=to, device_id_type=MESH)

    def _mine(self, ins, outs, sems, a, p):
        return pltpu.make_async_copy(ins[a], _quarter(outs[a], self.kinds[a], p), sems[2].at[a])

    def start(self, ins, outs, sems):
        x, y, c, chips = _place()
        p = 2 * x + y
        for a in range(len(ins)):
            self._mine(ins, outs, sems, a, p).start()
            for j, chip in enumerate(chips):
                self._copy(outs, sems, a, j, p, c, (*chip, c), src=_rows_half(ins[a], c)).start()

    def relay(self, ins, outs, sems):
        x, y, c, chips = _place()
        sibling = (x, y, 1 - c)
        for a in range(len(ins)):
            for j, (qx, qy) in enumerate(chips):
                q = 2 * qx + qy
                self._copy(outs, sems, a, j, q, c, sibling).wait_recv()
                self._copy(outs, sems, a, 3 + j, q, c, sibling).start()

    def finish(self, ins, outs, sems):
        x, y, c, chips = _place()
        p = 2 * x + y
        sibling = (x, y, 1 - c)
        n = len(ins)
        for a in range(n):
            for j, (qx, qy) in enumerate(chips):
                q = 2 * qx + qy
                self._copy(outs, sems, a, 3 + j, q, 1 - c, sibling).wait_recv()
                self._copy(outs, sems, a, j, p, c, (qx, qy, c), src=_rows_half(ins[a], c)).wait_send()
                self._copy(outs, sems, a, 3 + j, q, c, sibling).wait_send()
            self._mine(ins, outs, sems, a, p).wait()


class _ToSibling:
    def __init__(self, grads, kinds):
        self.ins = list(grads)
        self.kinds = list(kinds)
        n = len(self.ins)
        self.out_shape = []
        for g, k in zip(self.ins, self.kinds):
            qr, qc = _quarter_shape(g.shape, k)
            self.out_shape.append(jax.ShapeDtypeStruct((NS, qr // 2, qc), g.dtype))
        self.scratch = [pltpu.SemaphoreType.DMA((n, NS)), pltpu.SemaphoreType.DMA((n, NS))]
        self.aliases = {}

    def _copies(self, ins, outs, sems):
        x, y, c, _ = _place()
        return [pltpu.make_async_remote_copy(src_ref=_half_of_quarter(ins[a], self.kinds[a], q, 1 - c), dst_ref=outs[a].at[q],
                                             send_sem=sems[0].at[a, q], recv_sem=sems[1].at[a, q],
                                             device_id=(x, y, 1 - c), device_id_type=MESH)
                for a in range(len(ins)) for q in range(NS)]

    def start(self, ins, outs, sems):
        for cp in self._copies(ins, outs, sems):
            cp.start()

    def finish(self, ins, outs, sems):
        for cp in self._copies(ins, outs, sems):
            cp.wait()


class _ToChips:
    def __init__(self, sums, which=(0, 1, 2)):
        self.ins = list(sums)
        self.which = tuple(which)
        n, m = len(self.ins), len(self.which)
        self.out_shape = [jax.ShapeDtypeStruct((m,) + s.shape[1:], s.dtype) for s in self.ins]
        self.scratch = [pltpu.SemaphoreType.DMA((n, m)), pltpu.SemaphoreType.DMA((n, m))]
        self.aliases = {}

    def _copies(self, ins, outs, sems):
        x, y, c, chips = _place()
        return [pltpu.make_async_remote_copy(src_ref=ins[a].at[2 * chips[j][0] + chips[j][1]], dst_ref=outs[a].at[k],
                                             send_sem=sems[0].at[a, k], recv_sem=sems[1].at[a, k],
                                             device_id=(*chips[j], c), device_id_type=MESH)
                for a in range(len(ins)) for k, j in enumerate(self.which)]

    def start(self, ins, outs, sems):
        for cp in self._copies(ins, outs, sems):
            cp.start()

    def finish(self, ins, outs, sems):
        for cp in self._copies(ins, outs, sems):
            cp.wait()


class _SwapHalves:
    def __init__(self, quarters):
        self.ins = list(quarters)
        n = len(self.ins)
        self.out_shape = [jax.ShapeDtypeStruct(g.shape, g.dtype) for g in self.ins]
        self.scratch = [pltpu.SemaphoreType.DMA((n,)), pltpu.SemaphoreType.DMA((n,))]
        self.aliases = {a: a for a in range(n)}

    def _copy(self, outs, sems, a, pc):
        x, y, c, _ = _place()
        rows = _rows_half(outs[a], pc)
        return pltpu.make_async_remote_copy(src_ref=rows, dst_ref=rows, send_sem=sems[0].at[a], recv_sem=sems[1].at[a],
                                            device_id=(x, y, 1 - c), device_id_type=MESH)

    def start(self, ins, outs, sems):
        c = lax.axis_index("c")
        for a in range(len(outs)):
            self._copy(outs, sems, a, c).start()

    def finish(self, ins, outs, sems):
        c = lax.axis_index("c")
        for a in range(len(outs)):
            self._copy(outs, sems, a, c).wait_send()
            self._copy(outs, sems, a, 1 - c).wait_recv()


def _call(name, grid, compute, in_specs, out_specs, out_shape, scratch, vmem_mb, args, jobs=(), own_aliases=None):
    n_in, n_out, n_scr = len(in_specs), len(out_specs), len(scratch)
    ji = [len(j.ins) for j in jobs]
    jo = [len(j.out_shape) for j in jobs]
    js = [len(j.scratch) for j in jobs]

    def body(*refs):
        pos = [0]

        def take(k):
            r = refs[pos[0]:pos[0] + k]
            pos[0] += k
            return r

        ins, jins = take(n_in), [take(k) for k in ji]
        outs, jouts = take(n_out), [take(k) for k in jo]
        scr, jscr = take(n_scr), [take(k) for k in js]
        if jobs and grid:
            ids = [pl.program_id(a) for a in range(len(grid))]
            first = functools.reduce(jnp.logical_and, [i == 0 for i in ids])
            last = functools.reduce(jnp.logical_and, [i == g - 1 for i, g in zip(ids, grid)])

            @pl.when(first)
            def _():
                for j, a, b, c in zip(jobs, jins, jouts, jscr):
                    j.start(a, b, c)

            @pl.when(last)
            def _():
                for j, a, b, c in zip(jobs, jins, jouts, jscr):
                    if hasattr(j, "relay"):
                        j.relay(a, b, c)
        elif jobs:
            for j, a, b, c in zip(jobs, jins, jouts, jscr):
                j.start(a, b, c)
            for j, a, b, c in zip(jobs, jins, jouts, jscr):
                if hasattr(j, "relay"):
                    j.relay(a, b, c)
        compute(ins, outs, scr)
        if jobs and grid:
            @pl.when(last)
            def _():
                for j, a, b, c in zip(jobs, jins, jouts, jscr):
                    j.finish(a, b, c)
        elif jobs:
            for j, a, b, c in zip(jobs, jins, jouts, jscr):
                j.finish(a, b, c)

    aliases = dict(own_aliases or {})
    in_off, out_off = n_in, n_out
    for j, a, b in zip(jobs, ji, jo):
        for s, d in j.aliases.items():
            aliases[in_off + s] = out_off + d
        in_off += a
        out_off += b
    res = pl.pallas_call(
        body, name=name, grid=grid,
        in_specs=list(in_specs) + [IN_HBM] * sum(ji), out_specs=list(out_specs) + [IN_HBM] * sum(jo),
        out_shape=list(out_shape) + [pltpu.HBM(s.shape, s.dtype) for j in jobs for s in j.out_shape],
        scratch_shapes=list(scratch) + [s for j in jobs for s in j.scratch],
        input_output_aliases=aliases, compiler_params=_cparams(len(grid), vmem_mb),
    )(*args, *[a for j in jobs for a in j.ins])
    res = list(res)
    main, rest, jres = res[:n_out], res[n_out:], []
    for k in jo:
        jres.append(rest[:k])
        rest = rest[k:]
    return main, jres


def _exchange(name, jobs):
    return _call(name, (), lambda ins, outs, scr: None, [], [], [], [], 16, [], jobs)[1]


def _small_rows(ka, kb):
    first_a = 8
    first_b = first_a + -(-ka // 8) * 8
    return first_a, first_b, first_b + -(-kb // 8) * 8


LOSS_ROW = 7


def _allreduce_small(vecs, taps_a, taps_b, loss_part):
    counts = [len(v) for v in vecs]
    flat = [r for v in vecs for r in v]
    n = len(flat)
    C = flat[0].shape[1]
    NQ, KA, CB = taps_a.shape
    KB = taps_b.shape[1]
    first_a, first_b, R = _small_rows(KA, KB)
    assert len(vecs) <= LOSS_ROW < first_a
    N = 8

    def body(*refs):
        vec_refs = list(refs[:n])
        ta_ref, tb_ref, loss_ref, out_ref, v_ref, gath, send_sems, recv_sems, local_sem = refs[n:]
        v_ref[...] = jnp.zeros_like(v_ref)
        v_ref[LOSS_ROW:LOSS_ROW + 1, 0:loss_ref.shape[1]] = loss_ref[0:1, :]
        for i, k in enumerate(counts):
            parts, vec_refs = vec_refs[:k], vec_refs[k:]
            v_ref[i:i + 1, :] = functools.reduce(lambda a, b: a + b, [r[...] for r in parts])
        for q in range(NQ):
            v_ref[first_a:first_a + KA, q * CB:(q + 1) * CB] = ta_ref[q]
            v_ref[first_b:first_b + KB, q * CB:(q + 1) * CB] = tb_ref[q]
        x, y, c, chips = _place()
        me, sibling = (x, y, c), (x, y, 1 - c)

        def rows(px, py, pc):
            return gath.at[pl.ds((4 * px + 2 * py + pc) * R, R), :]

        def copy(k, block, to, src=None):
            return pltpu.make_async_remote_copy(src_ref=rows(*block) if src is None else src, dst_ref=rows(*block),
                                                send_sem=send_sems.at[k], recv_sem=recv_sems.at[k],
                                                device_id=to, device_id_type=MESH)

        mine = pltpu.make_async_copy(v_ref, rows(*me), local_sem)
        mine.start()
        first = [copy(0, me, sibling, src=v_ref)]
        first += [copy(1 + j, me, (*chip, c), src=v_ref) for j, chip in enumerate(chips)]
        for cp in first:
            cp.start()
        passed = [copy(4 + j, (*chip, c), sibling) for j, chip in enumerate(chips)]
        for j, chip in enumerate(chips):
            copy(1 + j, (*chip, c), me).wait_recv()
            passed[j].start()
        copy(0, sibling, me).wait_recv()
        for j, chip in enumerate(chips):
            copy(4 + j, (*chip, 1 - c), me).wait_recv()
        for cp in first + passed:
            cp.wait_send()
        mine.wait()
        acc = gath[0:R, :]
        for d in range(1, N):
            acc = acc + gath[d * R:(d + 1) * R, :]
        out_ref[...] = acc

    vmem = pl.BlockSpec(memory_space=pltpu.VMEM)
    return pl.pallas_call(
        body, name="allreduce_small",
        in_specs=[vmem] * (n + 3), out_specs=vmem,
        out_shape=jax.ShapeDtypeStruct((R, C), F32),
        scratch_shapes=[pltpu.VMEM((R, C), F32), pltpu.VMEM((N * R, C), F32), pltpu.SemaphoreType.DMA((7,)),
                        pltpu.SemaphoreType.DMA((7,)), pltpu.SemaphoreType.DMA],
    )(*flat, taps_a, taps_b, loss_part)


def _adamw_math(w, g, m, v):
    c1 = 1.0 - ADAM_B1 ** ADAM_STEP
    c2 = 1.0 - ADAM_B2 ** ADAM_STEP
    mn = ADAM_B1 * m + (1.0 - ADAM_B1) * g
    vn = ADAM_B2 * v + (1.0 - ADAM_B2) * (g * g)
    return -ADAM_LR * ((mn / c1) / (jnp.sqrt(vn / c2) + ADAM_EPS) + ADAM_WD * w), mn, vn


def _small_adamw(place, small, vec_wmv, tap_wmv):
    n = len(vec_wmv)
    D = small.shape[1]
    CB = tap_wmv[0][0].shape[2]
    ks = [t[0].shape[1] for t in tap_wmv]
    firsts = _small_rows(*ks)[:2]

    def body(place_ref, small_ref, *refs):
        ins, outs = refs[:3 * (n + 2)], refs[3 * (n + 2):]
        chip = place_ref[0]
        for i in range(n):
            g = small_ref[i:i + 1, :]
            d, mn, vn = _adamw_math(ins[3 * i][...], g, ins[3 * i + 1][...], ins[3 * i + 2][...])
            for o, val in zip(outs[4 * i:4 * i + 4], (g, d, mn, vn)):
                o[...] = val
        for t, (row0, k) in enumerate(zip(firsts, ks)):
            g = jnp.zeros((k, CB), F32)
            for q in range(D // CB):
                g = g + jnp.where(chip == q, small_ref[row0:row0 + k, q * CB:(q + 1) * CB], 0.0)
            w_ref, m_ref, v_ref = ins[3 * (n + t):3 * (n + t) + 3]
            d, mn, vn = _adamw_math(w_ref[0], g, m_ref[0], v_ref[0])
            for o, val in zip(outs[4 * (n + t):4 * (n + t) + 4], (g, d, mn, vn)):
                o[0] = val

    flat = [a for wmv in list(vec_wmv) + list(tap_wmv) for a in wmv]
    shapes = [jax.ShapeDtypeStruct(wmv[0].shape, F32) for wmv in list(vec_wmv) + list(tap_wmv) for _ in range(4)]
    vmem = pl.BlockSpec(memory_space=pltpu.VMEM)
    res = pl.pallas_call(
        body, name="small_adamw",
        in_specs=[pl.BlockSpec(memory_space=pltpu.SMEM)] + [vmem] * (1 + len(flat)), out_specs=[vmem] * len(shapes),
        out_shape=shapes,
    )(place, small, *flat)
    return [res[4 * i:4 * i + 4] for i in range(n + 2)]


def _ffn_fwd(h, g, wg, wu, wd, name, jobs=(), head=None):
    S, D = h.shape
    F = wg.shape[0]
    ts = _tile(S, 512)
    fb = _tile(F, F // 2)
    nf = F // fb

    def compute(ins, outs, scr):
        h_ref, g_ref, wg_ref, wu_ref, wd_ref = ins[:5]
        n_ref, gp_ref, up_ref = outs[-3:]
        x = h_ref[...]
        n = _rms_fwd(x, g_ref[...]).astype(BF16)
        n_ref[...] = n
        acc = None
        for j in range(nf):
            cols = slice(j * fb, (j + 1) * fb)
            gp = _dot_nt(n, wg_ref[cols, :])
            up = _dot_nt(n, wu_ref[cols, :])
            gp_ref[:, cols] = gp.astype(BF16)
            up_ref[:, cols] = up.astype(BF16)
            part = _dot((gp * _sigmoid(gp) * up).astype(BF16), wd_ref[cols, :])
            acc = part if acc is None else acc + part
        ho = x + 0.5 * acc
        if head is None:
            outs[0][...] = ho
            return
        t_ref, gf_ref = ins[5:]
        dh_ref, do_ref, dgf_ref, loss_ref = outs[:4]

        @pl.when(pl.program_id(0) == 0)
        def _():
            dgf_ref[...] = jnp.zeros_like(dgf_ref)
            loss_ref[...] = jnp.zeros_like(loss_ref)

        err = _rms_fwd(ho, gf_ref[...]) - t_ref[...]
        loss_ref[...] += (0.5 / D) * jnp.sum(err * err)
        dx, dg = _rms_bwd(ho, gf_ref[...], err * (1.0 / D))
        dh_ref[...] = dx
        do_ref[...] = (0.5 * dx).astype(BF16)
        dgf_ref[...] += dg

    tok = pl.BlockSpec((ts, D), lambda i: (i, 0))
    row = pl.BlockSpec((1, D), lambda i: (0, 0))
    wsp = _resident((F, D))
    hid = pl.BlockSpec((ts, F), lambda i: (i, 0))
    saved = [jax.ShapeDtypeStruct((S, D), BF16), jax.ShapeDtypeStruct((S, F), BF16), jax.ShapeDtypeStruct((S, F), BF16)]
    if head is None:
        return _call(name, (S // ts,), compute, [tok, row, wsp, wsp, wsp], [tok, tok, hid, hid],
                     [jax.ShapeDtypeStruct((S, D), F32)] + saved, [], 56, [h, g, wg, wu, wd], jobs)
    return _call(name, (S // ts,), compute, [tok, row, wsp, wsp, wsp, tok, row],
                 [tok, tok, row, pl.BlockSpec((8, 128), lambda i: (0, 0)), tok, hid, hid],
                 [jax.ShapeDtypeStruct((S, D), F32), jax.ShapeDtypeStruct((S, D), BF16), jax.ShapeDtypeStruct((1, D), F32),
                  jax.ShapeDtypeStruct((8, 128), F32)] + saved, [], 60, [h, g, wg, wu, wd, *head], jobs)


def _ffn_bwd_hidden(do, gp, up, wd, name, jobs=()):
    S, D = do.shape
    F = wd.shape[0]
    ts = _tile(S, 1024)
    fb = _tile(F, F // 2)
    def compute(ins, outs, scr):
        do_ref, gp_ref, up_ref, wd_ref = ins
        dgp_ref, dup_ref, a_ref = outs
        parts = _row_parts(ts, 4)
        das = [_dot_nt(do_ref[rows, :], wd_ref[...]) for rows in parts]
        for rows, da in zip(parts, das):
            gf = gp_ref[rows, :].astype(F32)
            uf = up_ref[rows, :].astype(F32)
            sg = _sigmoid(gf)
            si = gf * sg
            dgp_ref[rows, :] = (da * uf * (sg * (1.0 + gf * (1.0 - sg)))).astype(BF16)
            dup_ref[rows, :] = (da * si).astype(BF16)
            a_ref[rows, :] = (si * uf).astype(BF16)

    tok = pl.BlockSpec((ts, D), lambda s, i: (i, 0))
    hid = pl.BlockSpec((ts, fb), lambda s, i: (i, s))
    return _call(name, (F // fb, S // ts), compute, [tok, hid, hid, pl.BlockSpec((fb, D), lambda s, i: (s, 0))],
                 [hid, hid, hid], [jax.ShapeDtypeStruct((S, F), BF16)] * 3, [], 56, [do, gp, up, wd], jobs)


def _ffn_bwd_input(dh, h, g, dgp, dup, wg, wu, name, jobs=(), part=(0, 1), into=None):
    S, D = h.shape
    F = wg.shape[0]
    ts = _tile(S, 512)
    steps = S // ts // part[1]
    first = part[0] * steps

    def compute(ins, outs, scr):
        dh_ref, h_ref, g_ref, dgp_ref, dup_ref, wg_ref, wu_ref = ins[:7]
        dhi_ref, dg_ref = outs

        @pl.when(pl.program_id(0) == 0)
        def _():
            dg_ref[...] = jnp.zeros_like(dg_ref)

        dn = _dot(dgp_ref[...], wg_ref[...]) + _dot(dup_ref[...], wu_ref[...])
        dx, dg = _rms_bwd(h_ref[...], g_ref[...], dn)
        dhi_ref[...] = dh_ref[...] + dx
        dg_ref[...] += dg

    tok = pl.BlockSpec((ts, D), lambda i: (first + i, 0))
    hid = pl.BlockSpec((ts, F), lambda i: (first + i, 0))
    row = pl.BlockSpec((1, D), lambda i: (0, 0))
    in_specs = [tok, tok, row, hid, hid, _resident((F, D)), _resident((F, D))]
    args = [dh, h, g, dgp, dup, wg, wu]
    if into is not None:
        in_specs, args = in_specs + [IN_HBM], args + [into]
    return _call(name, (steps,), compute, in_specs, [tok, row],
                 [jax.ShapeDtypeStruct((S, D), F32), jax.ShapeDtypeStruct((1, D), F32)], [], 56, args, jobs,
                 own_aliases=None if into is None else {7: 0})


def _ffn_wgrad(hid, tok, name, jobs=()):
    S, D = tok.shape
    F = hid.shape[1]
    fb = _tile(F, F // 2)

    def compute(ins, outs, scr):
        outs[0][...] = _dot_tn(ins[0][...], ins[1][...])

    main, jres = _call(name, (F // fb,), compute,
                       [pl.BlockSpec((S, fb), lambda j: (0, j)), _resident(tok.shape)],
                       [pl.BlockSpec((fb, D), lambda j: (j, 0))], [jax.ShapeDtypeStruct((F, D), F32)], [], 56,
                       [hid, tok], jobs)
    return main[0], jres


def _w_in_pieces(D, cq, ng):
    groups = []
    for k in range(ng):
        lo, hi, pieces = k * D, (k + 1) * D, []
        while lo < hi:
            q = lo // cq
            w = min(hi, (q + 1) * cq) - lo
            pieces.append((q, lo - q * cq, w, lo - k * D))
            lo += w
        groups.append(pieces)
    return groups


def _mix_in_fwd(h, g, win, jobs=()):
    S, D = h.shape
    NG = win.shape[0] * win.shape[2] // D
    pieces = _w_in_pieces(D, win.shape[2], NG)
    ts = _tile(S, 512)

    def compute(ins, outs, scr):
        h_ref, g_ref, w_ref = ins
        u_ref, z_ref = outs
        u = _rms_fwd(h_ref[...], g_ref[...]).astype(BF16)
        u_ref[...] = u
        for k in range(NG):
            for q, c0, w, d0 in pieces[k]:
                z_ref[k, :, d0:d0 + w] = _dot(u, w_ref[q, :, c0:c0 + w]).astype(BF16)

    return _call("mix_in_fwd", (S // ts,), compute,
                 [pl.BlockSpec((ts, D), lambda i: (i, 0)), pl.BlockSpec((1, D), lambda i: (0, 0)), _resident(win.shape)],
                 [pl.BlockSpec((ts, D), lambda i: (i, 0)), pl.BlockSpec((NG, ts, D), lambda i: (0, i, 0))],
                 [jax.ShapeDtypeStruct((S, D), BF16), jax.ShapeDtypeStruct((NG, S, D), BF16)],
                 [], 48, [h, g, win], jobs)


SUBLANES = 8


def _shifted_copies(s):
    n = s.shape[1] - SUBLANES
    for r in range(1, SUBLANES):
        s[r, 0:n, :] = s[0, r:r + n, :]


def _window(s, o, rows):
    r = o % SUBLANES
    return s[r, o - r:o - r + rows, :]


def _conv_fwd(z, wa, ba, wb, jobs=()):
    _, S, D = z.shape
    _, KA, CB = wa.shape
    KB = wb.shape[1]
    ts = _tile(S, 1024)
    r = ts // HALO
    CH = min(64, ts)

    def compute(ins, outs, scr):
        z_ref, zh_ref, wa_ref, ba_ref, wb_ref = ins
        a1_ref, q_ref = outs
        sa, sb = scr
        keep = (pl.program_id(1) > 0).astype(F32)
        sa[0, HALO:HALO + ts, :] = z_ref[0].astype(F32) * _sigmoid(z_ref[1].astype(F32))
        sa[0, 0:HALO, :] = zh_ref[0].astype(F32) * _sigmoid(zh_ref[1].astype(F32)) * keep
        _shifted_copies(sa)
        sb[HALO:HALO + ts, :] = z_ref[3].astype(F32) * z_ref[4].astype(F32)
        sb[0:HALO, :] = zh_ref[3].astype(F32) * zh_ref[4].astype(F32) * keep
        wak = [wa_ref[0, k:k + 1, :] for k in range(KA)]
        wbk = [wb_ref[0, k:k + 1, :] for k in range(KB)]
        for c0 in range(0, ts, CH):
            acc = jnp.broadcast_to(ba_ref[...], (CH, CB))
            for k in range(KA):
                acc = acc + wak[k] * _window(sa, c0 + HALO - (KA - 1) + k, CH)
            a1_ref[c0:c0 + CH, :] = acc
            v = jnp.zeros((CH, CB), F32)
            for k in range(KB):
                o = c0 + HALO - (KB - 1) + k
                v = v + wbk[k] * sb[o:o + CH, :]
            q_ref[c0:c0 + CH, :] = (z_ref[2, c0:c0 + CH, :].astype(F32) * v).astype(BF16)

    return _call("conv_fwd", (D // CB, S // ts), compute,
                 [pl.BlockSpec((5, ts, CB), lambda j, i: (0, i, j)),
                  pl.BlockSpec((5, HALO, CB), lambda j, i: (0, jnp.maximum(i * r - 1, 0), j)),
                  pl.BlockSpec((1, KA, CB), lambda j, i: (j, 0, 0)), pl.BlockSpec((1, CB), lambda j, i: (0, j)),
                  pl.BlockSpec((1, KB, CB), lambda j, i: (j, 0, 0))],
                 [pl.BlockSpec((ts, CB), lambda j, i: (i, j)), pl.BlockSpec((ts, CB), lambda j, i: (i, j))],
                 [jax.ShapeDtypeStruct((S, D), F32), jax.ShapeDtypeStruct((S, D), BF16)],
                 [pltpu.VMEM((SUBLANES, HALO + ts, CB), F32), pltpu.VMEM((HALO + ts, CB), F32)], 40, [z, z, wa, ba, wb], jobs)


def _ln_stats(a1):
    mu = jnp.mean(a1, axis=-1, keepdims=True)
    xc = a1 - mu
    rstd = lax.rsqrt(jnp.mean(xc * xc, axis=-1, keepdims=True) + EPS)
    return xc * rstd, rstd


def _mix_out_fwd(h1, a1, q, z, lng, lnb, wa, wb, wo, jobs=()):
    S, D = h1.shape
    ts = _tile(S, 512)

    def compute(ins, outs, scr):
        h_ref, a1_ref, q_ref, ga_ref, gb_ref, lng_ref, lnb_ref, wa_ref, wb_ref, wo_ref = ins
        h2_ref, a3_ref, m_ref, ya_ref, yb_ref = outs
        xhat, _ = _ln_stats(a1_ref[...])
        a2 = xhat * lng_ref[...] + lnb_ref[...]
        a3 = (a2 * _sigmoid(a2)).astype(BF16)
        a3_ref[...] = a3
        ya = _dot(a3, wa_ref[...])
        yb = _dot(q_ref[...], wb_ref[...])
        ya_ref[...] = ya.astype(BF16)
        yb_ref[...] = yb.astype(BF16)
        m = (_sigmoid(ga_ref[0].astype(F32)) * ya + _sigmoid(gb_ref[0].astype(F32)) * yb).astype(BF16)
        m_ref[...] = m
        h2_ref[...] = h_ref[...] + _dot(m, wo_ref[...])

    tok = pl.BlockSpec((ts, D), lambda i: (i, 0))
    row = pl.BlockSpec((1, D), lambda i: (0, 0))
    mat = _resident((D, D))
    return _call("mix_out_fwd", (S // ts,), compute,
                 [tok, tok, tok, pl.BlockSpec((1, ts, D), lambda i: (5, i, 0)), pl.BlockSpec((1, ts, D), lambda i: (6, i, 0)),
                  row, row, mat, mat, mat], [tok] * 5,
                 [jax.ShapeDtypeStruct((S, D), F32)] + [jax.ShapeDtypeStruct((S, D), BF16)] * 4,
                 [], 56, [h1, a1, q, z, z, lng, lnb, wa, wb, wo], jobs)


def _mix_out_bwd(dh2, a1, z, ya, yb, lng, lnb, wa, wb, wo, jobs=()):
    S, D = dh2.shape
    ts = _tile(S, 512)

    def compute(ins, outs, scr):
        dh_ref, a1_ref, ga_ref, gb_ref, ya_ref, yb_ref, lng_ref, lnb_ref, wa_ref, wb_ref, wo_ref = ins
        da1_ref, dq_ref, dga_ref, dgb_ref, dya_ref, dyb_ref, dhb_ref, dlg_ref, dlb_ref = outs

        @pl.when(pl.program_id(0) == 0)
        def _():
            dlg_ref[...] = jnp.zeros_like(dlg_ref)
            dlb_ref[...] = jnp.zeros_like(dlb_ref)

        for rows in _row_parts(ts):
            dhb = dh_ref[rows, :].astype(BF16)
            dhb_ref[rows, :] = dhb
            dm = _dot_nt(dhb, wo_ref[...])
            sa = _sigmoid(ga_ref[0, rows, :].astype(F32))
            sb = _sigmoid(gb_ref[0, rows, :].astype(F32))
            dga_ref[rows, :] = (dm * ya_ref[rows, :].astype(F32) * sa * (1.0 - sa)).astype(BF16)
            dgb_ref[rows, :] = (dm * yb_ref[rows, :].astype(F32) * sb * (1.0 - sb)).astype(BF16)
            dya = (sa * dm).astype(BF16)
            dyb = (sb * dm).astype(BF16)
            dya_ref[rows, :] = dya
            dyb_ref[rows, :] = dyb
            dq_ref[rows, :] = _dot_nt(dyb, wb_ref[...]).astype(BF16)
            da3 = _dot_nt(dya, wa_ref[...])
            xhat, rstd = _ln_stats(a1_ref[rows, :])
            a2 = xhat * lng_ref[...] + lnb_ref[...]
            sg = _sigmoid(a2)
            da2 = da3 * (sg * (1.0 + a2 * (1.0 - sg)))
            dlg_ref[...] += jnp.sum(da2 * xhat, axis=0, keepdims=True)
            dlb_ref[...] += jnp.sum(da2, axis=0, keepdims=True)
            dxh = da2 * lng_ref[...]
            da1_ref[rows, :] = (rstd * (dxh - jnp.mean(dxh, axis=-1, keepdims=True)
                                        - xhat * jnp.mean(dxh * xhat, axis=-1, keepdims=True))).astype(BF16)

    tok = pl.BlockSpec((ts, D), lambda i: (i, 0))
    row = pl.BlockSpec((1, D), lambda i: (0, 0))
    mat = _resident((D, D))
    return _call("mix_out_bwd", (S // ts,), compute,
                 [tok, tok, pl.BlockSpec((1, ts, D), lambda i: (5, i, 0)), pl.BlockSpec((1, ts, D), lambda i: (6, i, 0)),
                  tok, tok, row, row, mat, mat, mat], [tok] * 7 + [row, row],
                 [jax.ShapeDtypeStruct((S, D), BF16)] * 7 + [jax.ShapeDtypeStruct((1, D), F32)] * 2,
                 [], 56, [dh2, a1, z, z, ya, yb, lng, lnb, wa, wb, wo], jobs)


def _mixer_wgrads(a3, dya, q, dyb, mm, dhb, jobs=()):
    S, D = a3.shape
    tk = _tile(S, 512)

    def compute(ins, outs, scr):
        @pl.when(pl.program_id(0) == 0)
        def _():
            for o in outs:
                o[...] = jnp.zeros_like(o)

        for t in range(3):
            outs[t][...] += _dot_tn(ins[2 * t][...], ins[2 * t + 1][...])

    tok = pl.BlockSpec((tk, D), lambda k: (k, 0))
    return _call("mixer_wgrads", (S // tk,), compute, [tok] * 6, [pl.BlockSpec((D, D), lambda k: (0, 0))] * 3,
                 [jax.ShapeDtypeStruct((D, D), F32)] * 3, [], 56, [a3, dya, q, dyb, mm, dhb], jobs)


def _conv_bwd(z, da1, dq, dga, dgb, wa, wb, jobs=()):
    NG, S, D = z.shape
    _, KA, CB = wa.shape
    KB = wb.shape[1]
    ts = _tile(S, 1024)
    r = ts // HALO
    nt = S // ts
    CH = min(64, ts)
    last_halo = S // HALO - 1

    def compute(ins, outs, scr):
        z_ref, zp_ref, zn_ref, da1_ref, da1n_ref, dq_ref, dqn_ref, dga_ref, dgb_ref, wa_ref, wb_ref = ins
        dz_ref, dwa_ref, dba_ref, dwb_ref = outs
        sa0, sd, sp, sv, acca, accb = scr
        i = pl.program_id(1)
        prev = (i > 0).astype(F32)
        nxt = (i < nt - 1).astype(F32)

        @pl.when(i == 0)
        def _():
            acca[...] = jnp.zeros_like(acca)
            accb[...] = jnp.zeros_like(accb)
            dba_ref[...] = jnp.zeros_like(dba_ref)

        sa0[0, HALO:HALO + ts, :] = z_ref[0].astype(F32) * _sigmoid(z_ref[1].astype(F32))
        sa0[0, 0:HALO, :] = zp_ref[0].astype(F32) * _sigmoid(zp_ref[1].astype(F32)) * prev
        _shifted_copies(sa0)
        sp[HALO:HALO + ts, :] = z_ref[3].astype(F32) * z_ref[4].astype(F32)
        sp[0:HALO, :] = zp_ref[3].astype(F32) * zp_ref[4].astype(F32) * prev
        sd[0, 0:ts, :] = da1_ref[...].astype(F32)
        sd[0, ts:ts + HALO, :] = da1n_ref[...].astype(F32) * nxt
        _shifted_copies(sd)
        sv[0:ts, :] = dq_ref[...].astype(F32) * z_ref[2].astype(F32)
        sv[ts:ts + HALO, :] = dqn_ref[...].astype(F32) * zn_ref[2].astype(F32) * nxt
        dba_ref[...] += jnp.sum(sd[0, 0:ts, :], axis=0, keepdims=True)
        wak = [wa_ref[0, k:k + 1, :] for k in range(KA)]
        wbk = [wb_ref[0, k:k + 1, :] for k in range(KB)]
        for c0 in range(0, ts, CH):
            rows = slice(c0, c0 + CH)
            d1 = sd[0, rows, :]
            da0 = jnp.zeros((CH, CB), F32)
            for k in range(KA):
                da0 = da0 + wak[k] * _window(sd, c0 + (KA - 1) - k, CH)
                a0w = _window(sa0, c0 + HALO - (KA - 1) + k, CH)
                acca[k] += jnp.sum((d1 * a0w).reshape(CH // 8, 8, CB), axis=0)
            val = z_ref[0, rows, :].astype(F32)
            sg = _sigmoid(z_ref[1, rows, :].astype(F32))
            dz_ref[0, rows, :] = (da0 * sg).astype(BF16)
            dz_ref[1, rows, :] = (da0 * val * sg * (1.0 - sg)).astype(BF16)
            dv = sv[rows, :]
            v = jnp.zeros((CH, CB), F32)
            dp = jnp.zeros((CH, CB), F32)
            for k in range(KB):
                o = c0 + HALO - (KB - 1) + k
                pw = sp[o:o + CH, :]
                v = v + wbk[k] * pw
                accb[k] += jnp.sum((dv * pw).reshape(CH // 8, 8, CB), axis=0)
                o = c0 + (KB - 1) - k
                dp = dp + wbk[k] * sv[o:o + CH, :]
            dz_ref[2, rows, :] = (dq_ref[rows, :].astype(F32) * v).astype(BF16)
            dz_ref[3, rows, :] = (dp * z_ref[4, rows, :].astype(F32)).astype(BF16)
            dz_ref[4, rows, :] = (dp * z_ref[3, rows, :].astype(F32)).astype(BF16)
        dz_ref[5] = dga_ref[...]
        dz_ref[6] = dgb_ref[...]

        @pl.when(i == nt - 1)
        def _():
            dwa_ref[0] = jnp.sum(acca[...], axis=1)
            dwb_ref[0] = jnp.sum(accb[...], axis=1)

    zt = pl.BlockSpec((5, ts, CB), lambda j, i: (0, i, j))
    zp = pl.BlockSpec((5, HALO, CB), lambda j, i: (0, jnp.maximum(i * r - 1, 0), j))
    zn = pl.BlockSpec((5, HALO, CB), lambda j, i: (0, jnp.minimum((i + 1) * r, last_halo), j))
    tok = pl.BlockSpec((ts, CB), lambda j, i: (i, j))
    tokn = pl.BlockSpec((HALO, CB), lambda j, i: (jnp.minimum((i + 1) * r, last_halo), j))
    return _call("conv_bwd", (D // CB, nt), compute,
                 [zt, zp, zn, tok, tokn, tok, tokn, tok, tok,
                  pl.BlockSpec((1, KA, CB), lambda j, i: (j, 0, 0)), pl.BlockSpec((1, KB, CB), lambda j, i: (j, 0, 0))],
                 [pl.BlockSpec((NG, ts, CB), lambda j, i: (0, i, j)), pl.BlockSpec((1, KA, CB), lambda j, i: (j, 0, 0)),
                  pl.BlockSpec((1, CB), lambda j, i: (0, j)), pl.BlockSpec((1, KB, CB), lambda j, i: (j, 0, 0))],
                 [jax.ShapeDtypeStruct((NG, S, D), BF16), jax.ShapeDtypeStruct((D // CB, KA, CB), F32),
                  jax.ShapeDtypeStruct((1, D), F32), jax.ShapeDtypeStruct((D // CB, KB, CB), F32)],
                 [pltpu.VMEM((SUBLANES, HALO + ts, CB), F32), pltpu.VMEM((SUBLANES, ts + HALO, CB), F32),
                  pltpu.VMEM((HALO + ts, CB), F32), pltpu.VMEM((ts + HALO, CB), F32),
                  pltpu.VMEM((KA, 8, CB), F32), pltpu.VMEM((KB, 8, CB), F32)],
                 48, [z, z, z, da1, da1, dq, dq, dga, dgb, wa, wb], jobs)


def _mix_in_bwd(dh2, h1, g, dz, win, jobs=()):
    S, D = h1.shape
    NG = dz.shape[0]
    pieces = _w_in_pieces(D, win.shape[2], NG)
    ts = _tile(S, 512)

    def compute(ins, outs, scr):
        dh_ref, h_ref, g_ref, dz_ref, w_ref = ins
        dhi_ref, dg_ref, do_ref = outs

        @pl.when(pl.program_id(0) == 0)
        def _():
            dg_ref[...] = jnp.zeros_like(dg_ref)

        du = None
        for k in range(NG):
            for q, c0, w, d0 in pieces[k]:
                part = _dot_nt(dz_ref[k, :, d0:d0 + w], w_ref[q, :, c0:c0 + w])
                du = part if du is None else du + part
        dx, dg = _rms_bwd(h_ref[...], g_ref[...], du)
        dhi = dh_ref[...] + dx
        dhi_ref[...] = dhi
        do_ref[...] = (0.5 * dhi).astype(BF16)
        dg_ref[...] += dg

    tok = pl.BlockSpec((ts, D), lambda i: (i, 0))
    row = pl.BlockSpec((1, D), lambda i: (0, 0))
    return _call("mix_in_bwd", (S // ts,), compute,
                 [tok, tok, row, pl.BlockSpec((NG, ts, D), lambda i: (0, i, 0)), _resident(win.shape)],
                 [tok, row, tok],
                 [jax.ShapeDtypeStruct((S, D), F32), jax.ShapeDtypeStruct((1, D), F32), jax.ShapeDtypeStruct((S, D), BF16)],
                 [], 56, [dh2, h1, g, dz, win], jobs)


def _w_in_grad(u, dz, jobs=()):
    S, D = u.shape
    NG = dz.shape[0]

    def compute(ins, outs, scr):
        outs[0][...] = _dot_tn(ins[0][...], ins[1][0])

    return _call("w_in_grad", (NG,), compute,
                 [_resident(u.shape), pl.BlockSpec((1, S, D), lambda j: (j, 0, 0))],
                 [pl.BlockSpec((D, D), lambda j: (0, j))], [jax.ShapeDtypeStruct((D, NG * D), F32)], [], 48, [u, dz], jobs)


def _chip_sums(place, grads, got, kind, name):
    n = len(grads)
    qr, qc = _quarter_shape(grads[0].shape, kind)
    h = qr // 2
    tr = _row_tile(h)
    nr = h // tr

    def body(pc_ref, *refs):
        g_refs, got_refs, b_refs, f_refs = refs[:n], refs[n:2 * n], refs[2 * n:3 * n], refs[3 * n:]
        own = pl.program_id(1) == pc_ref[0]
        for a in range(n):
            s = g_refs[a][...] + got_refs[a][0]
            b_refs[a][0] = s.astype(BF16)

            @pl.when(own)
            def _():
                f_refs[a][...] = s

    if kind == "rows":
        gspec = pl.BlockSpec((tr, qc), lambda r, q, pc: (q * (2 * nr) + pc[1] * nr + r, 0))
    else:
        gspec = pl.BlockSpec((tr, qc), lambda r, q, pc: (pc[1] * nr + r, q))
    lspec = pl.BlockSpec((1, tr, qc), lambda r, q, pc: (q, r, 0))
    res = pl.pallas_call(
        body, name=name,
        grid_spec=pltpu.PrefetchScalarGridSpec(
            num_scalar_prefetch=1, grid=(nr, NS), in_specs=[gspec] * n + [lspec] * n,
            out_specs=[lspec] * n + [pl.BlockSpec((tr, qc), lambda r, q, pc: (r, 0))] * n),
        out_shape=[jax.ShapeDtypeStruct((NS, h, qc), BF16)] * n + [jax.ShapeDtypeStruct((h, qc), F32)] * n,
        compiler_params=_cparams(2, 48),
    )(place, *grads, *got)
    return res[:n], res[n:]


def _totals(place, own, got, name):
    n = len(own)
    h, qc = own[0].shape
    tr = _row_tile(h)
    nr = h // tr
    got = [list(g) if isinstance(g, (list, tuple)) else [g] for g in got]
    m = len(got[0])

    def body(pc_ref, *refs):
        own_refs, got_refs, o_refs = refs[:n], refs[n:n + n * m], refs[n + n * m:]
        for a in range(n):
            acc = own_refs[a][...]
            for g in got_refs[a * m:(a + 1) * m]:
                for k in range(g.shape[0]):
                    acc = acc + g[k].astype(F32)
            o_refs[a][...] = acc

    lands = [pl.BlockSpec((g.shape[0], tr, qc), lambda r, pc: (0, r, 0)) for gs in got for g in gs]
    return pl.pallas_call(
        body, name=name,
        grid_spec=pltpu.PrefetchScalarGridSpec(
            num_scalar_prefetch=1, grid=(nr,),
            in_specs=[pl.BlockSpec((tr, qc), lambda r, pc: (r, 0))] * n + lands,
            out_specs=[pl.BlockSpec((tr, qc), lambda r, pc: (pc[1] * nr + r, 0))] * n),
        out_shape=[jax.ShapeDtypeStruct((2 * h, qc), F32)] * n,
        compiler_params=_cparams(1, 48),
    )(place, *own, *[g for gs in got for g in gs])


def _adamw(ws, gs, ms, vs, name):
    n = len(ws)
    R, C = ws[0].shape
    tr = _row_tile(R, (36 << 20) // (7 * 2 * 4 * n * C))

    def body(*refs):
        w_refs, g_refs, m_refs, v_refs = refs[:n], refs[n:2 * n], refs[2 * n:3 * n], refs[3 * n:4 * n]
        d_refs, mo_refs, vo_refs = refs[4 * n:5 * n], refs[5 * n:6 * n], refs[6 * n:]
        for a in range(n):
            d_refs[a][...], mo_refs[a][...], vo_refs[a][...] = _adamw_math(w_refs[a][...], g_refs[a][...], m_refs[a][...],
                                                                         v_refs[a][...])

    blk = pl.BlockSpec((tr, C), lambda r: (r, 0))
    res = pl.pallas_call(
        body, name=name, grid=(R // tr,),
        in_specs=[blk] * (4 * n), out_specs=[blk] * (3 * n),
        out_shape=[jax.ShapeDtypeStruct((R, C), F32)] * (3 * n),
        compiler_params=_cparams(1, 56),
    )(*ws, *gs, *ms, *vs)
    return res[:n], res[n:2 * n], res[2 * n:]


def kernel(x, ffn1_norm, ffn1_w_gate, ffn1_w_up, ffn1_w_down, mix_norm, w_in, a_dw_w, a_dw_b, a_ln_g, a_ln_b, a_w_out, b_conv_w, b_w_out, w_o, ffn2_norm, ffn2_w_gate, ffn2_w_up, ffn2_w_down, final_norm, loss_target, m_ffn1_norm, m_ffn1_w_gate, m_ffn1_w_up, m_ffn1_w_down, m_mix_norm, m_w_in, m_a_dw_w, m_a_dw_b, m_a_ln_g, m_a_ln_b, m_a_w_out, m_b_conv_w, m_b_w_out, m_w_o, m_ffn2_norm, m_ffn2_w_gate, m_ffn2_w_up, m_ffn2_w_down, m_final_norm, v_ffn1_norm, v_ffn1_w_gate, v_ffn1_w_up, v_ffn1_w_down, v_mix_norm, v_w_in, v_a_dw_w, v_a_dw_b, v_a_ln_g, v_a_ln_b, v_a_w_out, v_b_conv_w, v_b_w_out, v_w_o, v_ffn2_norm, v_ffn2_w_gate, v_ffn2_w_up, v_ffn2_w_down, v_final_norm):
    names = ["ffn1_norm", "ffn1_w_gate", "ffn1_w_up", "ffn1_w_down", "mix_norm", "w_in", "a_dw_w", "a_dw_b", "a_ln_g",
             "a_ln_b", "a_w_out", "b_conv_w", "b_w_out", "w_o", "ffn2_norm", "ffn2_w_gate", "ffn2_w_up", "ffn2_w_down",
             "final_norm"]
    W = dict(zip(names, [ffn1_norm, ffn1_w_gate, ffn1_w_up, ffn1_w_down, mix_norm, w_in, a_dw_w, a_dw_b, a_ln_g, a_ln_b,
                         a_w_out, b_conv_w, b_w_out, w_o, ffn2_norm, ffn2_w_gate, ffn2_w_up, ffn2_w_down, final_norm]))
    M = dict(zip(names, [m_ffn1_norm, m_ffn1_w_gate, m_ffn1_w_up, m_ffn1_w_down, m_mix_norm, m_w_in, m_a_dw_w, m_a_dw_b,
                         m_a_ln_g, m_a_ln_b, m_a_w_out, m_b_conv_w, m_b_w_out, m_w_o, m_ffn2_norm, m_ffn2_w_gate,
                         m_ffn2_w_up, m_ffn2_w_down, m_final_norm]))
    V = dict(zip(names, [v_ffn1_norm, v_ffn1_w_gate, v_ffn1_w_up, v_ffn1_w_down, v_mix_norm, v_w_in, v_a_dw_w, v_a_dw_b,
                         v_a_ln_g, v_a_ln_b, v_a_w_out, v_b_conv_w, v_b_w_out, v_w_o, v_ffn2_norm, v_ffn2_w_gate,
                         v_ffn2_w_up, v_ffn2_w_down, v_final_norm]))
    transposed = ("ffn1_w_gate", "ffn1_w_up", "ffn2_w_gate", "ffn2_w_up")
    vecs = ["ffn1_norm", "mix_norm", "a_dw_b", "a_ln_g", "a_ln_b", "ffn2_norm", "final_norm"]
    ffn1 = ["ffn1_w_gate", "ffn1_w_up", "ffn1_w_down"]
    ffn2 = ["ffn2_w_gate", "ffn2_w_up", "ffn2_w_down"]
    outp = ["a_w_out", "b_w_out", "w_o"]

    S, D = x.shape[1], x.shape[2]
    CB = D // NS
    KA, KB = a_dw_w.shape[1], b_conv_w.shape[1]
    px, py, pc = lax.axis_index("x"), lax.axis_index("y"), lax.axis_index("c")
    chip = 2 * px + py
    place = jnp.stack([chip, pc]).astype(jnp.int32)
    h0 = x.reshape(S, D)
    tgt = loss_target.reshape(S, D)
    row = lambda n: pltpu.with_memory_space_constraint(W[n].reshape(1, D), pltpu.HBM)
    pad = lambda a, r: jnp.concatenate([a, jnp.zeros((r - a.shape[0], a.shape[1]), F32)], axis=0)

    def quarter(P, n):
        return jnp.transpose(P[n][0]) if n in transposed else P[n][0]

    def unquarter(a, n):
        return (jnp.transpose(a) if n in transposed else a).reshape(W[n].shape)

    wq = {n: quarter(W, n).astype(BF16) for n in ffn1 + ffn2 + outp + ["w_in"]}

    f1 = _exchange("gather_ffn1", [_Gather([wq[n] for n in ffn1], ["rows"] * 3)])[0]
    g_in = _Gather([wq["w_in"], pad(a_dw_w[0], 32), pad(b_conv_w[0], 16)], ["rows"] * 3)
    (h1, n1, gp1, up1), ((win, taps_a, taps_b),) = _ffn_fwd(h0, row("ffn1_norm"), *f1, "ffn1_fwd", [g_in])
    win = win.reshape(NS, D, -1)
    wa_taps = taps_a.reshape(NS, 32, CB)[:, :KA]
    wb_taps = taps_b.reshape(NS, 16, CB)[:, :KB]
    g_out = _Gather([wq[n] for n in outp], ["rows"] * 3)
    (u, z), ((wa_out, wb_out, wo),) = _mix_in_fwd(h1, row("mix_norm"), win, [g_out])
    g_f2 = _Gather([wq["ffn2_w_gate"], wq["ffn2_w_up"]], ["rows"] * 2)
    (a1, q), ((f2g, f2u),) = _conv_fwd(z, wa_taps, row("a_dw_b"), wb_taps, [g_f2])
    (h2, a3, mm, ya, yb), ((f2d,),) = _mix_out_fwd(h1, a1, q, z, row("a_ln_g"), row("a_ln_b"), wa_out, wb_out, wo,
                                                   [_Gather([wq["ffn2_w_down"]], ["rows"])])
    (dh3, do2, d_final, loss_part, n2, gp2, up2), _ = _ffn_fwd(h2, row("ffn2_norm"), f2g, f2u, f2d, "ffn2_fwd_loss",
                                                               head=(tgt, row("final_norm")))

    (dgp2, dup2, act2), _ = _ffn_bwd_hidden(do2, gp2, up2, f2d, "ffn2_bwd_hidden")
    (dh2, d_ffn2), _ = _ffn_bwd_input(dh3, h2, row("ffn2_norm"), dgp2, dup2, f2g, f2u, "ffn2_bwd_input")
    g2 = [_ffn_wgrad(dgp2, n2, "ffn2_dwg")[0], _ffn_wgrad(dup2, n2, "ffn2_dwu")[0], _ffn_wgrad(act2, do2, "ffn2_dwd")[0]]
    (da1, dq, dga, dgb, dya, dyb, dh2b, d_lng, d_lnb), (got,) = _mix_out_bwd(
        dh2, a1, z, ya, yb, row("a_ln_g"), row("a_ln_b"), wa_out, wb_out, wo, [_ToSibling(g2, ["rows"] * 3)])
    wire2, own2 = _chip_sums(place, g2, got, "rows", "ffn2_chip_sums")
    (dz, d_wa, d_ba, d_wb), (got,) = _conv_bwd(z, da1, dq, dga, dgb, wa_taps, wb_taps, [_ToChips(wire2)])
    half2 = _totals(place, own2, got, "ffn2_totals")
    (g_win,), (tot2,) = _w_in_grad(u, dz, [_SwapHalves(half2)])
    go, (got,) = _mixer_wgrads(a3, dya, q, dyb, mm, dh2b, [_ToSibling([g_win], ["cols"])])
    wire_in, own_in = _chip_sums(place, [g_win], got, "cols", "w_in_chip_sum")
    (dh1, d_mix, do1), (land_in, got_o) = _mix_in_bwd(dh2, h1, row("mix_norm"), dz, win,
                                                      [_ToChips(wire_in), _ToSibling(go, ["rows"] * 3)])
    half_in = _totals(place, own_in, land_in, "w_in_total")
    wire_o, own_o = _chip_sums(place, go, got_o, "rows", "mixer_chip_sums")
    (dgp1, dup1, act1), (tot_in, land_o) = _ffn_bwd_hidden(do1, gp1, up1, f1[2], "ffn1_bwd_hidden",
                                                           [_SwapHalves(half_in), _ToChips(wire_o)])
    half_o = _totals(place, own_o, land_o, "mixer_totals")
    g1g, (tot_o,) = _ffn_wgrad(dgp1, n1, "ffn1_dwg", [_SwapHalves(half_o)])
    g1u, (got_g,) = _ffn_wgrad(dup1, n1, "ffn1_dwu", [_ToSibling([g1g], ["rows"])])
    wire_g, own_g = _chip_sums(place, [g1g], got_g, "rows", "ffn1_dwg_chip_sum")
    g1d, (got_u, land_g) = _ffn_wgrad(act1, do1, "ffn1_dwd", [_ToSibling([g1u], ["rows"]), _ToChips(wire_g)])
    wire_u, own_u = _chip_sums(place, [g1u], got_u, "rows", "ffn1_dwu_chip_sum")
    half_g = _totals(place, own_g, land_g, "ffn1_dwg_total")
    ffn1_in = (dh1, h0, row("ffn1_norm"), dgp1, dup1, f1[0], f1[1])
    (dx, dg_a), (got_d, land_u, tot_g) = _ffn_bwd_input(
        *ffn1_in, "ffn1_bwd_input_a", [_ToSibling([g1d], ["rows"]), _ToChips(wire_u), _SwapHalves(half_g)], part=(0, 2))
    wire_d, own_d = _chip_sums(place, [g1d], got_d, "rows", "ffn1_dwd_chip_sum")
    half_u = _totals(place, own_u, land_u, "ffn1_dwu_total")
    (dx, dg_b), (land_d, tot_u) = _ffn_bwd_input(*ffn1_in, "ffn1_bwd_input_b", [_ToChips(wire_d), _SwapHalves(half_u)],
                                                 part=(1, 2), into=dx)
    half_d = _totals(place, own_d, land_d, "ffn1_dwd_total")
    (tot_d,) = _exchange("tail_exchange", [_SwapHalves(half_d)])
    tot1 = [tot_g[0], tot_u[0], tot_d[0]]
    totals = dict(zip(ffn2 + ["w_in"] + ffn1 + outp, list(tot2) + list(tot_in) + tot1 + list(tot_o)))

    vec_grads = {"ffn1_norm": [dg_a, dg_b], "mix_norm": [d_mix], "a_dw_b": [d_ba], "a_ln_g": [d_lng], "a_ln_b": [d_lnb],
                 "ffn2_norm": [d_ffn2], "final_norm": [d_final]}
    small = _allreduce_small([vec_grads[n] for n in vecs], d_wa, d_wb, loss_part)
    loss = small[LOSS_ROW, 0]
    taps = ["a_dw_w", "b_conv_w"]
    small_out = _small_adamw(place, small, [[P[n].reshape(1, D) for P in (W, M, V)] for n in vecs],
                             [[P[n] for P in (W, M, V)] for n in taps])

    grads, deltas, new_m, new_v = {}, {}, {}, {}
    for n, (g_, d_, m_, v_) in zip(vecs + taps, small_out):
        shp = W[n].shape
        grads[n], deltas[n], new_m[n], new_v[n] = g_.reshape(shp), d_.reshape(shp), m_.reshape(shp), v_.reshape(shp)
    for group, tag in ((ffn1 + ffn2, "ffn"), (["w_in"], "w_in"), (outp, "mixer")):
        ds, ms, vs = _adamw([quarter(W, n) for n in group], [totals[n] for n in group], [quarter(M, n) for n in group],
                            [quarter(V, n) for n in group], tag + "_adamw")
        for n, d_, m_, v_ in zip(group, ds, ms, vs):
            grads[n], deltas[n], new_m[n], new_v[n] = (unquarter(totals[n], n), unquarter(d_, n), unquarter(m_, n),
                                                       unquarter(v_, n))
    return (loss, dx.reshape(x.shape), *[grads[n] for n in names], *[deltas[n] for n in names],
            *[new_m[n] for n in names], *[new_v[n] for n in names])
```

```python
import functools

import jax
import jax.numpy as jnp
from jax import lax
from jax.experimental import pallas as pl
from jax.experimental.pallas import tpu as pltpu

F32 = jnp.float32
BF16 = jnp.bfloat16
EPS = 1e-6
NS = 4
HALO = 32
MESH = pl.DeviceIdType.MESH
IN_HBM = pl.BlockSpec(memory_space=pltpu.HBM)

ADAM_LR = 0.001
ADAM_B1 = 0.9
ADAM_B2 = 0.999
ADAM_EPS = 1e-08
ADAM_WD = 0.01
ADAM_STEP = 10


def _cparams(n_axes, vmem_mb):
    return pltpu.CompilerParams(dimension_semantics=("arbitrary",) * n_axes, vmem_limit_bytes=vmem_mb << 20)


def _tile(n, t):
    return t if n % t == 0 else n


def _row_parts(rows, n=2):
    if rows % (16 * n):
        return [slice(0, rows)]
    return [slice(p * (rows // n), (p + 1) * (rows // n)) for p in range(n)]


def _resident(shape):
    return pl.BlockSpec(shape, lambda *_: (0,) * len(shape), pipeline_mode=pl.Buffered(1))


def _row_tile(n, cap=256):
    for t in (256, 176, 128, 64, 32, 16, 8):
        if t <= cap and n % t == 0:
            return t
    return n


def _dot(a, b):
    return jnp.dot(a, b, preferred_element_type=F32)


def _dot_nt(a, b):
    return lax.dot_general(a, b, (((1,), (1,)), ((), ())), preferred_element_type=F32)


def _dot_tn(a, b):
    return lax.dot_general(a, b, (((0,), (0,)), ((), ())), preferred_element_type=F32)


def _sigmoid(x):
    return jax.nn.sigmoid(x)


def _rms_fwd(x, g):
    r = lax.rsqrt(jnp.mean(x * x, axis=-1, keepdims=True) + EPS)
    return x * r * g


def _rms_bwd(x, g, dn):
    r = lax.rsqrt(jnp.mean(x * x, axis=-1, keepdims=True) + EPS)
    xr = x * r
    dg = jnp.sum(dn * xr, axis=0, keepdims=True)
    w = dn * g
    dx = r * w - xr * (r * r) * jnp.mean(x * w, axis=-1, keepdims=True)
    return dx, dg


def _place():
    x, y, c = lax.axis_index("x"), lax.axis_index("y"), lax.axis_index("c")
    chips = [(1 - x, y), (x, 1 - y), (1 - x, 1 - y)]
    return x, y, c, chips


def _quarter_shape(full_shape, kind):
    r, c = full_shape
    return (r // NS, c) if kind == "rows" else (r, c // NS)


def _half_of_quarter(ref, kind, q, pc):
    qr, qc = _quarter_shape(ref.shape, kind)
    h = qr // 2
    if kind == "rows":
        return ref.at[pl.ds(q * qr + pc * h, h), :]
    return ref.at[pl.ds(pc * h, h), pl.ds(q * qc, qc)]


def _quarter(ref, kind, q):
    qr, qc = _quarter_shape(ref.shape, kind)
    if kind == "rows":
        return ref.at[pl.ds(q * qr, qr), :]
    return ref.at[:, pl.ds(q * qc, qc)]


def _rows_half(ref, pc):
    h = ref.shape[0] // 2
    return ref.at[pl.ds(pc * h, h)]


class _Gather:
    def __init__(self, quarters, kinds):
        self.ins = list(quarters)
        self.kinds = list(kinds)
        n = len(self.ins)
        self.out_shape = [jax.ShapeDtypeStruct((NS * a.shape[0], a.shape[1]) if k == "rows" else (a.shape[0], NS * a.shape[1]),
                                               a.dtype) for a, k in zip(self.ins, self.kinds)]
        self.scratch = [pltpu.SemaphoreType.DMA((n, 6)), pltpu.SemaphoreType.DMA((n, 6)), pltpu.SemaphoreType.DMA((n,))]
        self.aliases = {}

    def _copy(self, outs, sems, a, k, q, pc, to, src=None):
        dst = _half_of_quarter(outs[a], self.kinds[a], q, pc)
        return pltpu.make_async_remote_copy(src_ref=dst if src is None else src, dst_ref=dst,
                                            send_sem=sems[0].at[a, k], recv_sem=sems[1].at[a, k],
                                            device_id=to, device_id_type=MESH)

    def _mine(self, ins, outs, sems, a, p):
        return pltpu.make_async_copy(ins[a], _quarter(outs[a], self.kinds[a], p), sems[2].at[a])

    def start(self, ins, outs, sems):
        x, y, c, chips = _place()
        p = 2 * x + y
        for a in range(len(ins)):
            self._mine(ins, outs, sems, a, p).start()
            for j, chip in enumerate(chips):
                self._copy(outs, sems, a, j, p, c, (*chip, c), src=_rows_half(ins[a], c)).start()

    def relay(self, ins, outs, sems):
        x, y, c, chips = _place()
        sibling = (x, y, 1 - c)
        for a in range(len(ins)):
            for j, (qx, qy) in enumerate(chips):
                q = 2 * qx + qy
                self._copy(outs, sems, a, j, q, c, sibling).wait_recv()
                self._copy(outs, sems, a, 3 + j, q, c, sibling).start()

    def finish(self, ins, outs, sems):
        x, y, c, chips = _place()
        p = 2 * x + y
        sibling = (x, y, 1 - c)
        n = len(ins)
        for a in range(n):
            for j, (qx, qy) in enumerate(chips):
                q = 2 * qx + qy
                self._copy(outs, sems, a, 3 + j, q, 1 - c, sibling).wait_recv()
                self._copy(outs, sems, a, j, p, c, (qx, qy, c), src=_rows_half(ins[a], c)).wait_send()
                self._copy(outs, sems, a, 3 + j, q, c, sibling).wait_send()
            self._mine(ins, outs, sems, a, p).wait()


class _ToSibling:
    def __init__(self, grads, kinds):
        self.ins = list(grads)
        self.kinds = list(kinds)
        n = len(self.ins)
        self.out_shape = []
        for g, k in zip(self.ins, self.kinds):
            qr, qc = _quarter_shape(g.shape, k)
            self.out_shape.append(jax.ShapeDtypeStruct((NS, qr // 2, qc), g.dtype))
        self.scratch = [pltpu.SemaphoreType.DMA((n, NS)), pltpu.SemaphoreType.DMA((n, NS))]
        self.aliases = {}

    def _copies(self, ins, outs, sems):
        x, y, c, _ = _place()
        return [pltpu.make_async_remote_copy(src_ref=_half_of_quarter(ins[a], self.kinds[a], q, 1 - c), dst_ref=outs[a].at[q],
                                             send_sem=sems[0].at[a, q], recv_sem=sems[1].at[a, q],
                                             device_id=(x, y, 1 - c), device_id_type=MESH)
                for a in range(len(ins)) for q in range(NS)]

    def start(self, ins, outs, sems):
        for cp in self._copies(ins, outs, sems):
            cp.start()

    def finish(self, ins, outs, sems):
        for cp in self._copies(ins, outs, sems):
            cp.wait()


class _ToChips:
    def __init__(self, sums, which=(0, 1, 2)):
        self.ins = list(sums)
        self.which = tuple(which)
        n, m = len(self.ins), len(self.which)
        self.out_shape = [jax.ShapeDtypeStruct((m,) + s.shape[1:], s.dtype) for s in self.ins]
        self.scratch = [pltpu.SemaphoreType.DMA((n, m)), pltpu.SemaphoreType.DMA((n, m))]
        self.aliases = {}

    def _copies(self, ins, outs, sems):
        x, y, c, chips = _place()
        return [pltpu.make_async_remote_copy(src_ref=ins[a].at[2 * chips[j][0] + chips[j][1]], dst_ref=outs[a].at[k],
                                             send_sem=sems[0].at[a, k], recv_sem=sems[1].at[a, k],
                                             device_id=(*chips[j], c), device_id_type=MESH)
                for a in range(len(ins)) for k, j in enumerate(self.which)]

    def start(self, ins, outs, sems):
        for cp in self._copies(ins, outs, sems):
            cp.start()

    def finish(self, ins, outs, sems):
        for cp in self._copies(ins, outs, sems):
            cp.wait()


class _SwapHalves:
    def __init__(self, quarters):
        self.ins = list(quarters)
        n = len(self.ins)
        self.out_shape = [jax.ShapeDtypeStruct(g.shape, g.dtype) for g in self.ins]
        self.scratch = [pltpu.SemaphoreType.DMA((n,)), pltpu.SemaphoreType.DMA((n,))]
        self.aliases = {a: a for a in range(n)}

    def _copy(self, outs, sems, a, pc):
        x, y, c, _ = _place()
        rows = _rows_half(outs[a], pc)
        return pltpu.make_async_remote_copy(src_ref=rows, dst_ref=rows, send_sem=sems[0].at[a], recv_sem=sems[1].at[a],
                                            device_id=(x, y, 1 - c), device_id_type=MESH)

    def start(self, ins, outs, sems):
        c = lax.axis_index("c")
        for a in range(len(outs)):
            self._copy(outs, sems, a, c).start()

    def finish(self, ins, outs, sems):
        c = lax.axis_index("c")
        for a in range(len(outs)):
            self._copy(outs, sems, a, c).wait_send()
            self._copy(outs, sems, a, 1 - c).wait_recv()


def _call(name, grid, compute, in_specs, out_specs, out_shape, scratch, vmem_mb, args, jobs=(), own_aliases=None):
    n_in, n_out, n_scr = len(in_specs), len(out_specs), len(scratch)
    ji = [len(j.ins) for j in jobs]
    jo = [len(j.out_shape) for j in jobs]
    js = [len(j.scratch) for j in jobs]

    def body(*refs):
        pos = [0]

        def take(k):
            r = refs[pos[0]:pos[0] + k]
            pos[0] += k
            return r

        ins, jins = take(n_in), [take(k) for k in ji]
        outs, jouts = take(n_out), [take(k) for k in jo]
        scr, jscr = take(n_scr), [take(k) for k in js]
        if jobs and grid:
            ids = [pl.program_id(a) for a in range(len(grid))]
            first = functools.reduce(jnp.logical_and, [i == 0 for i in ids])
            last = functools.reduce(jnp.logical_and, [i == g - 1 for i, g in zip(ids, grid)])

            @pl.when(first)
            def _():
                for j, a, b, c in zip(jobs, jins, jouts, jscr):
                    j.start(a, b, c)

            @pl.when(last)
            def _():
                for j, a, b, c in zip(jobs, jins, jouts, jscr):
                    if hasattr(j, "relay"):
                        j.relay(a, b, c)
        elif jobs:
            for j, a, b, c in zip(jobs, jins, jouts, jscr):
                j.start(a, b, c)
            for j, a, b, c in zip(jobs, jins, jouts, jscr):
                if hasattr(j, "relay"):
                    j.relay(a, b, c)
        compute(ins, outs, scr)
        if jobs and grid:
            @pl.when(last)
            def _():
                for j, a, b, c in zip(jobs, jins, jouts, jscr):
                    j.finish(a, b, c)
        elif jobs:
            for j, a, b, c in zip(jobs, jins, jouts, jscr):
                j.finish(a, b, c)

    aliases = dict(own_aliases or {})
    in_off, out_off = n_in, n_out
    for j, a, b in zip(jobs, ji, jo):
        for s, d in j.aliases.items():
            aliases[in_off + s] = out_off + d
        in_off += a
        out_off += b
    res = pl.pallas_call(
        body, name=name, grid=grid,
        in_specs=list(in_specs) + [IN_HBM] * sum(ji), out_specs=list(out_specs) + [IN_HBM] * sum(jo),
        out_shape=list(out_shape) + [pltpu.HBM(s.shape, s.dtype) for j in jobs for s in j.out_shape],
        scratch_shapes=list(scratch) + [s for j in jobs for s in j.scratch],
        input_output_aliases=aliases, compiler_params=_cparams(len(grid), vmem_mb),
    )(*args, *[a for j in jobs for a in j.ins])
    res = list(res)
    main, rest, jres = res[:n_out], res[n_out:], []
    for k in jo:
        jres.append(rest[:k])
        rest = rest[k:]
    return main, jres


def _exchange(name, jobs):
    return _call(name, (), lambda ins, outs, scr: None, [], [], [], [], 16, [], jobs)[1]


def _small_rows(ka, kb):
    first_a = 8
    first_b = first_a + -(-ka // 8) * 8
    return first_a, first_b, first_b + -(-kb // 8) * 8


LOSS_ROW = 7


def _allreduce_small(vecs, taps_a, taps_b, loss_part):
    counts = [len(v) for v in vecs]
    flat = [r for v in vecs for r in v]
    n = len(flat)
    C = flat[0].shape[1]
    NQ, KA, CB = taps_a.shape
    KB = taps_b.shape[1]
    first_a, first_b, R = _small_rows(KA, KB)
    assert len(vecs) <= LOSS_ROW < first_a
    N = 8

    def body(*refs):
        vec_refs = list(refs[:n])
        ta_ref, tb_ref, loss_ref, out_ref, v_ref, gath, send_sems, recv_sems, local_sem = refs[n:]
        v_ref[...] = jnp.zeros_like(v_ref)
        v_ref[LOSS_ROW:LOSS_ROW + 1, 0:loss_ref.shape[1]] = loss_ref[0:1, :]
        for i, k in enumerate(counts):
            parts, vec_refs = vec_refs[:k], vec_refs[k:]
            v_ref[i:i + 1, :] = functools.reduce(lambda a, b: a + b, [r[...] for r in parts])
        for q in range(NQ):
            v_ref[first_a:first_a + KA, q * CB:(q + 1) * CB] = ta_ref[q]
            v_ref[first_b:first_b + KB, q * CB:(q + 1) * CB] = tb_ref[q]
        x, y, c, chips = _place()
        me, sibling = (x, y, c), (x, y, 1 - c)

        def rows(px, py, pc):
            return gath.at[pl.ds((4 * px + 2 * py + pc) * R, R), :]

        def copy(k, block, to, src=None):
            return pltpu.make_async_remote_copy(src_ref=rows(*block) if src is None else src, dst_ref=rows(*block),
                                                send_sem=send_sems.at[k], recv_sem=recv_sems.at[k],
                                                device_id=to, device_id_type=MESH)

        mine = pltpu.make_async_copy(v_ref, rows(*me), local_sem)
        mine.start()
        first = [copy(0, me, sibling, src=v_ref)]
        first += [copy(1 + j, me, (*chip, c), src=v_ref) for j, chip in enumerate(chips)]
        for cp in first:
            cp.start()
        passed = [copy(4 + j, (*chip, c), sibling) for j, chip in enumerate(chips)]
        for j, chip in enumerate(chips):
            copy(1 + j, (*chip, c), me).wait_recv()
            passed[j].start()
        copy(0, sibling, me).wait_recv()
        for j, chip in enumerate(chips):
            copy(4 + j, (*chip, 1 - c), me).wait_recv()
        for cp in first + passed:
            cp.wait_send()
        mine.wait()
        acc = gath[0:R, :]
        for d in range(1, N):
            acc = acc + gath[d * R:(d + 1) * R, :]
        out_ref[...] = acc

    vmem = pl.BlockSpec(memory_space=pltpu.VMEM)
    return pl.pallas_call(
        body, name="allreduce_small",
        in_specs=[vmem] * (n + 3), out_specs=vmem,
        out_shape=jax.ShapeDtypeStruct((R, C), F32),
        scratch_shapes=[pltpu.VMEM((R, C), F32), pltpu.VMEM((N * R, C), F32), pltpu.SemaphoreType.DMA((7,)),
                        pltpu.SemaphoreType.DMA((7,)), pltpu.SemaphoreType.DMA],
    )(*flat, taps_a, taps_b, loss_part)


def _adamw_math(w, g, m, v):
    c1 = 1.0 - ADAM_B1 ** ADAM_STEP
    c2 = 1.0 - ADAM_B2 ** ADAM_STEP
    mn = ADAM_B1 * m + (1.0 - ADAM_B1) * g
    vn = ADAM_B2 * v + (1.0 - ADAM_B2) * (g * g)
    return -ADAM_LR * ((mn / c1) / (jnp.sqrt(vn / c2) + ADAM_EPS) + ADAM_WD * w), mn, vn


def _small_adamw(place, small, vec_wmv, tap_wmv):
    n = len(vec_wmv)
    D = small.shape[1]
    CB = tap_wmv[0][0].shape[2]
    ks = [t[0].shape[1] for t in tap_wmv]
    firsts = _small_rows(*ks)[:2]

    def body(place_ref, small_ref, *refs):
        ins, outs = refs[:3 * (n + 2)], refs[3 * (n + 2):]
        chip = place_ref[0]
        for i in range(n):
            g = small_ref[i:i + 1, :]
            d, mn, vn = _adamw_math(ins[3 * i][...], g, ins[3 * i + 1][...], ins[3 * i + 2][...])
            for o, val in zip(outs[4 * i:4 * i + 4], (g, d, mn, vn)):
                o[...] = val
        for t, (row0, k) in enumerate(zip(firsts, ks)):
            g = jnp.zeros((k, CB), F32)
            for q in range(D // CB):
                g = g + jnp.where(chip == q, small_ref[row0:row0 + k, q * CB:(q + 1) * CB], 0.0)
            w_ref, m_ref, v_ref = ins[3 * (n + t):3 * (n + t) + 3]
            d, mn, vn = _adamw_math(w_ref[0], g, m_ref[0], v_ref[0])
            for o, val in zip(outs[4 * (n + t):4 * (n + t) + 4], (g, d, mn, vn)):
                o[0] = val

    flat = [a for wmv in list(vec_wmv) + list(tap_wmv) for a in wmv]
    shapes = [jax.ShapeDtypeStruct(wmv[0].shape, F32) for wmv in list(vec_wmv) + list(tap_wmv) for _ in range(4)]
    vmem = pl.BlockSpec(memory_space=pltpu.VMEM)
    res = pl.pallas_call(
        body, name="small_adamw",
        in_specs=[pl.BlockSpec(memory_space=pltpu.SMEM)] + [vmem] * (1 + len(flat)), out_specs=[vmem] * len(shapes),
        out_shape=shapes,
    )(place, small, *flat)
    return [res[4 * i:4 * i + 4] for i in range(n + 2)]


def _ffn_fwd(h, g, wg, wu, wd, name, jobs=(), head=None):
    S, D = h.shape
    F = wg.shape[0]
    ts = _tile(S, 512)
    fb = _tile(F, F // 2)
    nf = F // fb

    def compute(ins, outs, scr):
        h_ref, g_ref, wg_ref, wu_ref, wd_ref = ins[:5]
        n_ref, gp_ref, up_ref = outs[-3:]
        x = h_ref[...]
        n = _rms_fwd(x, g_ref[...]).astype(BF16)
        n_ref[...] = n
        acc = None
        for j in range(nf):
            cols = slice(j * fb, (j + 1) * fb)
            gp = _dot_nt(n, wg_ref[cols, :])
            up = _dot_nt(n, wu_ref[cols, :])
            gp_ref[:, cols] = gp.astype(BF16)
            up_ref[:, cols] = up.astype(BF16)
            part = _dot((gp * _sigmoid(gp) * up).astype(BF16), wd_ref[cols, :])
            acc = part if acc is None else acc + part
        ho = x + 0.5 * acc
        if head is None:
            outs[0][...] = ho
            return
        t_ref, gf_ref = ins[5:]
        dh_ref, do_ref, dgf_ref, loss_ref = outs[:4]

        @pl.when(pl.program_id(0) == 0)
        def _():
            dgf_ref[...] = jnp.zeros_like(dgf_ref)
            loss_ref[...] = jnp.zeros_like(loss_ref)

        err = _rms_fwd(ho, gf_ref[...]) - t_ref[...]
        loss_ref[...] += (0.5 / D) * jnp.sum(err * err)
        dx, dg = _rms_bwd(ho, gf_ref[...], err * (1.0 / D))
        dh_ref[...] = dx
        do_ref[...] = (0.5 * dx).astype(BF16)
        dgf_ref[...] += dg

    tok = pl.BlockSpec((ts, D), lambda i: (i, 0))
    row = pl.BlockSpec((1, D), lambda i: (0, 0))
    wsp = _resident((F, D))
    hid = pl.BlockSpec((ts, F), lambda i: (i, 0))
    saved = [jax.ShapeDtypeStruct((S, D), BF16), jax.ShapeDtypeStruct((S, F), BF16), jax.ShapeDtypeStruct((S, F), BF16)]
    if head is None:
        return _call(name, (S // ts,), compute, [tok, row, wsp, wsp, wsp], [tok, tok, hid, hid],
                     [jax.ShapeDtypeStruct((S, D), F32)] + saved, [], 56, [h, g, wg, wu, wd], jobs)
    return _call(name, (S // ts,), compute, [tok, row, wsp, wsp, wsp, tok, row],
                 [tok, tok, row, pl.BlockSpec((8, 128), lambda i: (0, 0)), tok, hid, hid],
                 [jax.ShapeDtypeStruct((S, D), F32), jax.ShapeDtypeStruct((S, D), BF16), jax.ShapeDtypeStruct((1, D), F32),
                  jax.ShapeDtypeStruct((8, 128), F32)] + saved, [], 60, [h, g, wg, wu, wd, *head], jobs)


def _ffn_bwd_hidden(do, gp, up, wd, name, jobs=()):
    S, D = do.shape
    F = wd.shape[0]
    ts = _tile(S, 1024)
    fb = _tile(F, F // 2)
    nt, total = S // ts, (F // fb) * (S // ts)
    depth = 3

    def compute(ins, outs, scr):
        do_ref, gp_hbm, up_hbm, wd_ref = ins
        dgp_ref, dup_ref, a_ref = outs
        gring, uring, sems = scr
        t = pl.program_id(0) * nt + pl.program_id(1)

        def fetch(step):
            slot = lax.rem(step, depth)
            src = (pl.ds(lax.rem(step, nt) * ts, ts), pl.ds(lax.div(step, nt) * fb, fb))
            return (pltpu.make_async_copy(gp_hbm.at[src], gring.at[slot], sems.at[0, slot]),
                    pltpu.make_async_copy(up_hbm.at[src], uring.at[slot], sems.at[1, slot]))

        @pl.when(t == 0)
        def _():
            for k in range(min(depth - 1, total)):
                for cp in fetch(jnp.int32(k)):
                    cp.start()

        @pl.when(t + (depth - 1) < total)
        def _():
            for cp in fetch(t + (depth - 1)):
                cp.start()

        for cp in fetch(t):
            cp.wait()
        slot = lax.rem(t, depth)
        parts = _row_parts(ts, 4)
        das = [_dot_nt(do_ref[rows, :], wd_ref[...]) for rows in parts]
        for rows, da in zip(parts, das):
            gf = gring[slot, rows, :].astype(F32)
            uf = uring[slot, rows, :].astype(F32)
            sg = _sigmoid(gf)
            si = gf * sg
            dgp_ref[rows, :] = (da * uf * (sg * (1.0 + gf * (1.0 - sg)))).astype(BF16)
            dup_ref[rows, :] = (da * si).astype(BF16)
            a_ref[rows, :] = (si * uf).astype(BF16)

    tok = pl.BlockSpec((ts, D), lambda s, i: (i, 0))
    hid = pl.BlockSpec((ts, fb), lambda s, i: (i, s))
    return _call(name, (F // fb, nt), compute, [tok, IN_HBM, IN_HBM, pl.BlockSpec((fb, D), lambda s, i: (s, 0))],
                 [hid, hid, hid], [jax.ShapeDtypeStruct((S, F), BF16)] * 3,
                 [pltpu.VMEM((depth, ts, fb), BF16), pltpu.VMEM((depth, ts, fb), BF16), pltpu.SemaphoreType.DMA((2, depth))],
                 60, [do, gp, up, wd], jobs)


def _ffn_bwd_input(dh, h, g, dgp, dup, wg, wu, name, jobs=(), part=(0, 1), into=None):
    S, D = h.shape
    F = wg.shape[0]
    ts = _tile(S, 512)
    steps = S // ts // part[1]
    first = part[0] * steps

    def compute(ins, outs, scr):
        dh_ref, h_ref, g_ref, dgp_ref, dup_ref, wg_ref, wu_ref = ins[:7]
        dhi_ref, dg_ref = outs

        @pl.when(pl.program_id(0) == 0)
        def _():
            dg_ref[...] = jnp.zeros_like(dg_ref)

        dn = _dot(dgp_ref[...], wg_ref[...]) + _dot(dup_ref[...], wu_ref[...])
        dx, dg = _rms_bwd(h_ref[...], g_ref[...], dn)
        dhi_ref[...] = dh_ref[...] + dx
        dg_ref[...] += dg

    tok = pl.BlockSpec((ts, D), lambda i: (first + i, 0))
    hid = pl.BlockSpec((ts, F), lambda i: (first + i, 0))
    row = pl.BlockSpec((1, D), lambda i: (0, 0))
    in_specs = [tok, tok, row, hid, hid, _resident((F, D)), _resident((F, D))]
    args = [dh, h, g, dgp, dup, wg, wu]
    if into is not None:
        in_specs, args = in_specs + [IN_HBM], args + [into]
    return _call(name, (steps,), compute, in_specs, [tok, row],
                 [jax.ShapeDtypeStruct((S, D), F32), jax.ShapeDtypeStruct((1, D), F32)], [], 56, args, jobs,
                 own_aliases=None if into is None else {7: 0})


def _ffn_wgrad(hid, tok, name, jobs=()):
    S, D = tok.shape
    F = hid.shape[1]
    fb = _tile(F, F // 2)

    def compute(ins, outs, scr):
        outs[0][...] = _dot_tn(ins[0][...], ins[1][...])

    main, jres = _call(name, (F // fb,), compute,
                       [pl.BlockSpec((S, fb), lambda j: (0, j)), _resident(tok.shape)],
                       [pl.BlockSpec((fb, D), lambda j: (j, 0))], [jax.ShapeDtypeStruct((F, D), F32)], [], 56,
                       [hid, tok], jobs)
    return main[0], jres


def _w_in_pieces(D, cq, ng):
    groups = []
    for k in range(ng):
        lo, hi, pieces = k * D, (k + 1) * D, []
        while lo < hi:
            q = lo // cq
            w = min(hi, (q + 1) * cq) - lo
            pieces.append((q, lo - q * cq, w, lo - k * D))
            lo += w
        groups.append(pieces)
    return groups


def _mix_in_fwd(h, g, win, jobs=()):
    S, D = h.shape
    NG = win.shape[0] * win.shape[2] // D
    pieces = _w_in_pieces(D, win.shape[2], NG)
    ts = _tile(S, 512)

    def compute(ins, outs, scr):
        h_ref, g_ref, w_ref = ins
        u_ref, z_ref = outs
        u = _rms_fwd(h_ref[...], g_ref[...]).astype(BF16)
        u_ref[...] = u
        for k in range(NG):
            for q, c0, w, d0 in pieces[k]:
                z_ref[k, :, d0:d0 + w] = _dot(u, w_ref[q, :, c0:c0 + w]).astype(BF16)

    return _call("mix_in_fwd", (S // ts,), compute,
                 [pl.BlockSpec((ts, D), lambda i: (i, 0)), pl.BlockSpec((1, D), lambda i: (0, 0)), _resident(win.shape)],
                 [pl.BlockSpec((ts, D), lambda i: (i, 0)), pl.BlockSpec((NG, ts, D), lambda i: (0, i, 0))],
                 [jax.ShapeDtypeStruct((S, D), BF16), jax.ShapeDtypeStruct((NG, S, D), BF16)],
                 [], 48, [h, g, win], jobs)


SUBLANES = 8


def _shifted_copies(s):
    n = s.shape[1] - SUBLANES
    for r in range(1, SUBLANES):
        s[r, 0:n, :] = s[0, r:r + n, :]


def _window(s, o, rows):
    r = o % SUBLANES
    return s[r, o - r:o - r + rows, :]


def _conv_fwd(z, wa, ba, wb, jobs=()):
    _, S, D = z.shape
    _, KA, CB = wa.shape
    KB = wb.shape[1]
    ts = _tile(S, 1024)
    r = ts // HALO
    CH = min(64, ts)

    def compute(ins, outs, scr):
        z_ref, zh_ref, wa_ref, ba_ref, wb_ref = ins
        a1_ref, q_ref = outs
        sa, sb = scr
        keep = (pl.program_id(1) > 0).astype(F32)
        sa[0, HALO:HALO + ts, :] = z_ref[0].astype(F32) * _sigmoid(z_ref[1].astype(F32))
        sa[0, 0:HALO, :] = zh_ref[0].astype(F32) * _sigmoid(zh_ref[1].astype(F32)) * keep
        _shifted_copies(sa)
        sb[HALO:HALO + ts, :] = z_ref[3].astype(F32) * z_ref[4].astype(F32)
        sb[0:HALO, :] = zh_ref[3].astype(F32) * zh_ref[4].astype(F32) * keep
        wak = [wa_ref[0, k:k + 1, :] for k in range(KA)]
        wbk = [wb_ref[0, k:k + 1, :] for k in range(KB)]
        for c0 in range(0, ts, CH):
            acc = jnp.broadcast_to(ba_ref[...], (CH, CB))
            for k in range(KA):
                acc = acc + wak[k] * _window(sa, c0 + HALO - (KA - 1) + k, CH)
            a1_ref[c0:c0 + CH, :] = acc
            v = jnp.zeros((CH, CB), F32)
            for k in range(KB):
                o = c0 + HALO - (KB - 1) + k
                v = v + wbk[k] * sb[o:o + CH, :]
            q_ref[c0:c0 + CH, :] = (z_ref[2, c0:c0 + CH, :].astype(F32) * v).astype(BF16)

    return _call("conv_fwd", (D // CB, S // ts), compute,
                 [pl.BlockSpec((5, ts, CB), lambda j, i: (0, i, j)),
                  pl.BlockSpec((5, HALO, CB), lambda j, i: (0, jnp.maximum(i * r - 1, 0), j)),
                  pl.BlockSpec((1, KA, CB), lambda j, i: (j, 0, 0)), pl.BlockSpec((1, CB), lambda j, i: (0, j)),
                  pl.BlockSpec((1, KB, CB), lambda j, i: (j, 0, 0))],
                 [pl.BlockSpec((ts, CB), lambda j, i: (i, j)), pl.BlockSpec((ts, CB), lambda j, i: (i, j))],
                 [jax.ShapeDtypeStruct((S, D), F32), jax.ShapeDtypeStruct((S, D), BF16)],
                 [pltpu.VMEM((SUBLANES, HALO + ts, CB), F32), pltpu.VMEM((HALO + ts, CB), F32)], 40, [z, z, wa, ba, wb], jobs)


def _ln_stats(a1):
    mu = jnp.mean(a1, axis=-1, keepdims=True)
    xc = a1 - mu
    rstd = lax.rsqrt(jnp.mean(xc * xc, axis=-1, keepdims=True) + EPS)
    return xc * rstd, rstd


def _mix_out_fwd(h1, a1, q, z, lng, lnb, wa, wb, wo, jobs=()):
    S, D = h1.shape
    ts = _tile(S, 512)

    def compute(ins, outs, scr):
        h_ref, a1_ref, q_ref, ga_ref, gb_ref, lng_ref, lnb_ref, wa_ref, wb_ref, wo_ref = ins
        h2_ref, a3_ref, m_ref, ya_ref, yb_ref = outs
        xhat, _ = _ln_stats(a1_ref[...])
        a2 = xhat * lng_ref[...] + lnb_ref[...]
        a3 = (a2 * _sigmoid(a2)).astype(BF16)
        a3_ref[...] = a3
        ya = _dot(a3, wa_ref[...])
        yb = _dot(q_ref[...], wb_ref[...])
        ya_ref[...] = ya.astype(BF16)
        yb_ref[...] = yb.astype(BF16)
        m = (_sigmoid(ga_ref[0].astype(F32)) * ya + _sigmoid(gb_ref[0].astype(F32)) * yb).astype(BF16)
        m_ref[...] = m
        h2_ref[...] = h_ref[...] + _dot(m, wo_ref[...])

    tok = pl.BlockSpec((ts, D), lambda i: (i, 0))
    row = pl.BlockSpec((1, D), lambda i: (0, 0))
    mat = _resident((D, D))
    return _call("mix_out_fwd", (S // ts,), compute,
                 [tok, tok, tok, pl.BlockSpec((1, ts, D), lambda i: (5, i, 0)), pl.BlockSpec((1, ts, D), lambda i: (6, i, 0)),
                  row, row, mat, mat, mat], [tok] * 5,
                 [jax.ShapeDtypeStruct((S, D), F32)] + [jax.ShapeDtypeStruct((S, D), BF16)] * 4,
                 [], 56, [h1, a1, q, z, z, lng, lnb, wa, wb, wo], jobs)


def _mix_out_bwd(dh2, a1, z, ya, yb, lng, lnb, wa, wb, wo, jobs=()):
    S, D = dh2.shape
    ts = _tile(S, 512)

    def compute(ins, outs, scr):
        dh_ref, a1_ref, ga_ref, gb_ref, ya_ref, yb_ref, lng_ref, lnb_ref, wa_ref, wb_ref, wo_ref = ins
        da1_ref, dq_ref, dga_ref, dgb_ref, dya_ref, dyb_ref, dhb_ref, dlg_ref, dlb_ref = outs

        @pl.when(pl.program_id(0) == 0)
        def _():
            dlg_ref[...] = jnp.zeros_like(dlg_ref)
            dlb_ref[...] = jnp.zeros_like(dlb_ref)

        for rows in _row_parts(ts):
            dhb = dh_ref[rows, :].astype(BF16)
            dhb_ref[rows, :] = dhb
            dm = _dot_nt(dhb, wo_ref[...])
            sa = _sigmoid(ga_ref[0, rows, :].astype(F32))
            sb = _sigmoid(gb_ref[0, rows, :].astype(F32))
            dga_ref[rows, :] = (dm * ya_ref[rows, :].astype(F32) * sa * (1.0 - sa)).astype(BF16)
            dgb_ref[rows, :] = (dm * yb_ref[rows, :].astype(F32) * sb * (1.0 - sb)).astype(BF16)
            dya = (sa * dm).astype(BF16)
            dyb = (sb * dm).astype(BF16)
            dya_ref[rows, :] = dya
            dyb_ref[rows, :] = dyb
            dq_ref[rows, :] = _dot_nt(dyb, wb_ref[...]).astype(BF16)
            da3 = _dot_nt(dya, wa_ref[...])
            xhat, rstd = _ln_stats(a1_ref[rows, :])
            a2 = xhat * lng_ref[...] + lnb_ref[...]
            sg = _sigmoid(a2)
            da2 = da3 * (sg * (1.0 + a2 * (1.0 - sg)))
            dlg_ref[...] += jnp.sum(da2 * xhat, axis=0, keepdims=True)
            dlb_ref[...] += jnp.sum(da2, axis=0, keepdims=True)
            dxh = da2 * lng_ref[...]
            da1_ref[rows, :] = (rstd * (dxh - jnp.mean(dxh, axis=-1, keepdims=True)
                                        - xhat * jnp.mean(dxh * xhat, axis=-1, keepdims=True))).astype(BF16)

    tok = pl.BlockSpec((ts, D), lambda i: (i, 0))
    row = pl.BlockSpec((1, D), lambda i: (0, 0))
    mat = _resident((D, D))
    return _call("mix_out_bwd", (S // ts,), compute,
                 [tok, tok, pl.BlockSpec((1, ts, D), lambda i: (5, i, 0)), pl.BlockSpec((1, ts, D), lambda i: (6, i, 0)),
                  tok, tok, row, row, mat, mat, mat], [tok] * 7 + [row, row],
                 [jax.ShapeDtypeStruct((S, D), BF16)] * 7 + [jax.ShapeDtypeStruct((1, D), F32)] * 2,
                 [], 56, [dh2, a1, z, z, ya, yb, lng, lnb, wa, wb, wo], jobs)


def _mixer_wgrads(a3, dya, q, dyb, mm, dhb, jobs=()):
    S, D = a3.shape
    tk = _tile(S, 512)

    def compute(ins, outs, scr):
        @pl.when(pl.program_id(0) == 0)
        def _():
            for o in outs:
                o[...] = jnp.zeros_like(o)

        for t in range(3):
            outs[t][...] += _dot_tn(ins[2 * t][...], ins[2 * t + 1][...])

    tok = pl.BlockSpec((tk, D), lambda k: (k, 0))
    return _call("mixer_wgrads", (S // tk,), compute, [tok] * 6, [pl.BlockSpec((D, D), lambda k: (0, 0))] * 3,
                 [jax.ShapeDtypeStruct((D, D), F32)] * 3, [], 56, [a3, dya, q, dyb, mm, dhb], jobs)


def _conv_bwd(z, da1, dq, dga, dgb, wa, wb, jobs=()):
    NG, S, D = z.shape
    _, KA, CB = wa.shape
    KB = wb.shape[1]
    ts = _tile(S, 1024)
    r = ts // HALO
    nt = S // ts
    CH = min(64, ts)
    last_halo = S // HALO - 1

    def compute(ins, outs, scr):
        z_ref, zp_ref, zn_ref, da1_ref, da1n_ref, dq_ref, dqn_ref, dga_ref, dgb_ref, wa_ref, wb_ref = ins
        dz_ref, dwa_ref, dba_ref, dwb_ref = outs
        sa0, sd, sp, sv, acca, accb = scr
        i = pl.program_id(1)
        prev = (i > 0).astype(F32)
        nxt = (i < nt - 1).astype(F32)

        @pl.when(i == 0)
        def _():
            acca[...] = jnp.zeros_like(acca)
            accb[...] = jnp.zeros_like(accb)
            dba_ref[...] = jnp.zeros_like(dba_ref)

        sa0[0, HALO:HALO + ts, :] = z_ref[0].astype(F32) * _sigmoid(z_ref[1].astype(F32))
        sa0[0, 0:HALO, :] = zp_ref[0].astype(F32) * _sigmoid(zp_ref[1].astype(F32)) * prev
        _shifted_copies(sa0)
        sp[HALO:HALO + ts, :] = z_ref[3].astype(F32) * z_ref[4].astype(F32)
        sp[0:HALO, :] = zp_ref[3].astype(F32) * zp_ref[4].astype(F32) * prev
        sd[0, 0:ts, :] = da1_ref[...].astype(F32)
        sd[0, ts:ts + HALO, :] = da1n_ref[...].astype(F32) * nxt
        _shifted_copies(sd)
        sv[0:ts, :] = dq_ref[...].astype(F32) * z_ref[2].astype(F32)
        sv[ts:ts + HALO, :] = dqn_ref[...].astype(F32) * zn_ref[2].astype(F32) * nxt
        dba_ref[...] += jnp.sum(sd[0, 0:ts, :], axis=0, keepdims=True)
        wak = [wa_ref[0, k:k + 1, :] for k in range(KA)]
        wbk = [wb_ref[0, k:k + 1, :] for k in range(KB)]
        for c0 in range(0, ts, CH):
            rows = slice(c0, c0 + CH)
            d1 = sd[0, rows, :]
            da0 = jnp.zeros((CH, CB), F32)
            for k in range(KA):
                da0 = da0 + wak[k] * _window(sd, c0 + (KA - 1) - k, CH)
                a0w = _window(sa0, c0 + HALO - (KA - 1) + k, CH)
                acca[k] += jnp.sum((d1 * a0w).reshape(CH // 8, 8, CB), axis=0)
            val = z_ref[0, rows, :].astype(F32)
            sg = _sigmoid(z_ref[1, rows, :].astype(F32))
            dz_ref[0, rows, :] = (da0 * sg).astype(BF16)
            dz_ref[1, rows, :] = (da0 * val * sg * (1.0 - sg)).astype(BF16)
            dv = sv[rows, :]
            v = jnp.zeros((CH, CB), F32)
            dp = jnp.zeros((CH, CB), F32)
            for k in range(KB):
                o = c0 + HALO - (KB - 1) + k
                pw = sp[o:o + CH, :]
                v = v + wbk[k] * pw
                accb[k] += jnp.sum((dv * pw).reshape(CH // 8, 8, CB), axis=0)
                o = c0 + (KB - 1) - k
                dp = dp + wbk[k] * sv[o:o + CH, :]
            dz_ref[2, rows, :] = (dq_ref[rows, :].astype(F32) * v).astype(BF16)
            dz_ref[3, rows, :] = (dp * z_ref[4, rows, :].astype(F32)).astype(BF16)
            dz_ref[4, rows, :] = (dp * z_ref[3, rows, :].astype(F32)).astype(BF16)
        dz_ref[5] = dga_ref[...]
        dz_ref[6] = dgb_ref[...]

        @pl.when(i == nt - 1)
        def _():
            dwa_ref[0] = jnp.sum(acca[...], axis=1)
            dwb_ref[0] = jnp.sum(accb[...], axis=1)

    zt = pl.BlockSpec((5, ts, CB), lambda j, i: (0, i, j))
    zp = pl.BlockSpec((5, HALO, CB), lambda j, i: (0, jnp.maximum(i * r - 1, 0), j))
    zn = pl.BlockSpec((5, HALO, CB), lambda j, i: (0, jnp.minimum((i + 1) * r, last_halo), j))
    tok = pl.BlockSpec((ts, CB), lambda j, i: (i, j))
    tokn = pl.BlockSpec((HALO, CB), lambda j, i: (jnp.minimum((i + 1) * r, last_halo), j))
    return _call("conv_bwd", (D // CB, nt), compute,
                 [zt, zp, zn, tok, tokn, tok, tokn, tok, tok,
                  pl.BlockSpec((1, KA, CB), lambda j, i: (j, 0, 0)), pl.BlockSpec((1, KB, CB), lambda j, i: (j, 0, 0))],
                 [pl.BlockSpec((NG, ts, CB), lambda j, i: (0, i, j)), pl.BlockSpec((1, KA, CB), lambda j, i: (j, 0, 0)),
                  pl.BlockSpec((1, CB), lambda j, i: (0, j)), pl.BlockSpec((1, KB, CB), lambda j, i: (j, 0, 0))],
                 [jax.ShapeDtypeStruct((NG, S, D), BF16), jax.ShapeDtypeStruct((D // CB, KA, CB), F32),
                  jax.ShapeDtypeStruct((1, D), F32), jax.ShapeDtypeStruct((D // CB, KB, CB), F32)],
                 [pltpu.VMEM((SUBLANES, HALO + ts, CB), F32), pltpu.VMEM((SUBLANES, ts + HALO, CB), F32),
                  pltpu.VMEM((HALO + ts, CB), F32), pltpu.VMEM((ts + HALO, CB), F32),
                  pltpu.VMEM((KA, 8, CB), F32), pltpu.VMEM((KB, 8, CB), F32)],
                 48, [z, z, z, da1, da1, dq, dq, dga, dgb, wa, wb], jobs)


def _mix_in_bwd(dh2, h1, g, dz, win, jobs=()):
    S, D = h1.shape
    NG = dz.shape[0]
    pieces = _w_in_pieces(D, win.shape[2], NG)
    ts = _tile(S, 512)

    def compute(ins, outs, scr):
        dh_ref, h_ref, g_ref, dz_ref, w_ref = ins
        dhi_ref, dg_ref, do_ref = outs

        @pl.when(pl.program_id(0) == 0)
        def _():
            dg_ref[...] = jnp.zeros_like(dg_ref)

        du = None
        for k in range(NG):
            for q, c0, w, d0 in pieces[k]:
                part = _dot_nt(dz_ref[k, :, d0:d0 + w], w_ref[q, :, c0:c0 + w])
                du = part if du is None else du + part
        dx, dg = _rms_bwd(h_ref[...], g_ref[...], du)
        dhi = dh_ref[...] + dx
        dhi_ref[...] = dhi
        do_ref[...] = (0.5 * dhi).astype(BF16)
        dg_ref[...] += dg

    tok = pl.BlockSpec((ts, D), lambda i: (i, 0))
    row = pl.BlockSpec((1, D), lambda i: (0, 0))
    return _call("mix_in_bwd", (S // ts,), compute,
                 [tok, tok, row, pl.BlockSpec((NG, ts, D), lambda i: (0, i, 0)), _resident(win.shape)],
                 [tok, row, tok],
                 [jax.ShapeDtypeStruct((S, D), F32), jax.ShapeDtypeStruct((1, D), F32), jax.ShapeDtypeStruct((S, D), BF16)],
                 [], 56, [dh2, h1, g, dz, win], jobs)


def _w_in_grad(u, dz, jobs=()):
    S, D = u.shape
    NG = dz.shape[0]

    def compute(ins, outs, scr):
        outs[0][...] = _dot_tn(ins[0][...], ins[1][0])

    return _call("w_in_grad", (NG,), compute,
                 [_resident(u.shape), pl.BlockSpec((1, S, D), lambda j: (j, 0, 0))],
                 [pl.BlockSpec((D, D), lambda j: (0, j))], [jax.ShapeDtypeStruct((D, NG * D), F32)], [], 48, [u, dz], jobs)


def _chip_sums(place, grads, got, kind, name):
    n = len(grads)
    qr, qc = _quarter_shape(grads[0].shape, kind)
    h = qr // 2
    tr = _row_tile(h)
    nr = h // tr

    def body(pc_ref, *refs):
        g_refs, got_refs, b_refs, f_refs = refs[:n], refs[n:2 * n], refs[2 * n:3 * n], refs[3 * n:]
        own = pl.program_id(1) == pc_ref[0]
        for a in range(n):
            s = g_refs[a][...] + got_refs[a][0]
            b_refs[a][0] = s.astype(BF16)

            @pl.when(own)
            def _():
                f_refs[a][...] = s

    if kind == "rows":
        gspec = pl.BlockSpec((tr, qc), lambda r, q, pc: (q * (2 * nr) + pc[1] * nr + r, 0))
    else:
        gspec = pl.BlockSpec((tr, qc), lambda r, q, pc: (pc[1] * nr + r, q))
    lspec = pl.BlockSpec((1, tr, qc), lambda r, q, pc: (q, r, 0))
    res = pl.pallas_call(
        body, name=name,
        grid_spec=pltpu.PrefetchScalarGridSpec(
            num_scalar_prefetch=1, grid=(nr, NS), in_specs=[gspec] * n + [lspec] * n,
            out_specs=[lspec] * n + [pl.BlockSpec((tr, qc), lambda r, q, pc: (r, 0))] * n),
        out_shape=[jax.ShapeDtypeStruct((NS, h, qc), BF16)] * n + [jax.ShapeDtypeStruct((h, qc), F32)] * n,
        compiler_params=_cparams(2, 48),
    )(place, *grads, *got)
    return res[:n], res[n:]


def _totals(place, own, got, name):
    n = len(own)
    h, qc = own[0].shape
    tr = _row_tile(h)
    nr = h // tr
    got = [list(g) if isinstance(g, (list, tuple)) else [g] for g in got]
    m = len(got[0])

    def body(pc_ref, *refs):
        own_refs, got_refs, o_refs = refs[:n], refs[n:n + n * m], refs[n + n * m:]
        for a in range(n):
            acc = own_refs[a][...]
            for g in got_refs[a * m:(a + 1) * m]:
                for k in range(g.shape[0]):
                    acc = acc + g[k].astype(F32)
            o_refs[a][...] = acc

    lands = [pl.BlockSpec((g.shape[0], tr, qc), lambda r, pc: (0, r, 0)) for gs in got for g in gs]
    return pl.pallas_call(
        body, name=name,
        grid_spec=pltpu.PrefetchScalarGridSpec(
            num_scalar_prefetch=1, grid=(nr,),
            in_specs=[pl.BlockSpec((tr, qc), lambda r, pc: (r, 0))] * n + lands,
            out_specs=[pl.BlockSpec((tr, qc), lambda r, pc: (pc[1] * nr + r, 0))] * n),
        out_shape=[jax.ShapeDtypeStruct((2 * h, qc), F32)] * n,
        compiler_params=_cparams(1, 48),
    )(place, *own, *[g for gs in got for g in gs])


def _adamw(ws, gs, ms, vs, name):
    n = len(ws)
    R, C = ws[0].shape
    tr = _row_tile(R, (36 << 20) // (7 * 2 * 4 * n * C))

    def body(*refs):
        w_refs, g_refs, m_refs, v_refs = refs[:n], refs[n:2 * n], refs[2 * n:3 * n], refs[3 * n:4 * n]
        d_refs, mo_refs, vo_refs = refs[4 * n:5 * n], refs[5 * n:6 * n], refs[6 * n:]
        for a in range(n):
            d_refs[a][...], mo_refs[a][...], vo_refs[a][...] = _adamw_math(w_refs[a][...], g_refs[a][...], m_refs[a][...],
                                                                         v_refs[a][...])

    blk = pl.BlockSpec((tr, C), lambda r: (r, 0))
    res = pl.pallas_call(
        body, name=name, grid=(R // tr,),
        in_specs=[blk] * (4 * n), out_specs=[blk] * (3 * n),
        out_shape=[jax.ShapeDtypeStruct((R, C), F32)] * (3 * n),
        compiler_params=_cparams(1, 56),
    )(*ws, *gs, *ms, *vs)
    return res[:n], res[n:2 * n], res[2 * n:]


def kernel(x, ffn1_norm, ffn1_w_gate, ffn1_w_up, ffn1_w_down, mix_norm, w_in, a_dw_w, a_dw_b, a_ln_g, a_ln_b, a_w_out, b_conv_w, b_w_out, w_o, ffn2_norm, ffn2_w_gate, ffn2_w_up, ffn2_w_down, final_norm, loss_target, m_ffn1_norm, m_ffn1_w_gate, m_ffn1_w_up, m_ffn1_w_down, m_mix_norm, m_w_in, m_a_dw_w, m_a_dw_b, m_a_ln_g, m_a_ln_b, m_a_w_out, m_b_conv_w, m_b_w_out, m_w_o, m_ffn2_norm, m_ffn2_w_gate, m_ffn2_w_up, m_ffn2_w_down, m_final_norm, v_ffn1_norm, v_ffn1_w_gate, v_ffn1_w_up, v_ffn1_w_down, v_mix_norm, v_w_in, v_a_dw_w, v_a_dw_b, v_a_ln_g, v_a_ln_b, v_a_w_out, v_b_conv_w, v_b_w_out, v_w_o, v_ffn2_norm, v_ffn2_w_gate, v_ffn2_w_up, v_ffn2_w_down, v_final_norm):
    names = ["ffn1_norm", "ffn1_w_gate", "ffn1_w_up", "ffn1_w_down", "mix_norm", "w_in", "a_dw_w", "a_dw_b", "a_ln_g",
             "a_ln_b", "a_w_out", "b_conv_w", "b_w_out", "w_o", "ffn2_norm", "ffn2_w_gate", "ffn2_w_up", "ffn2_w_down",
             "final_norm"]
    W = dict(zip(names, [ffn1_norm, ffn1_w_gate, ffn1_w_up, ffn1_w_down, mix_norm, w_in, a_dw_w, a_dw_b, a_ln_g, a_ln_b,
                         a_w_out, b_conv_w, b_w_out, w_o, ffn2_norm, ffn2_w_gate, ffn2_w_up, ffn2_w_down, final_norm]))
    M = dict(zip(names, [m_ffn1_norm, m_ffn1_w_gate, m_ffn1_w_up, m_ffn1_w_down, m_mix_norm, m_w_in, m_a_dw_w, m_a_dw_b,
                         m_a_ln_g, m_a_ln_b, m_a_w_out, m_b_conv_w, m_b_w_out, m_w_o, m_ffn2_norm, m_ffn2_w_gate,
                         m_ffn2_w_up, m_ffn2_w_down, m_final_norm]))
    V = dict(zip(names, [v_ffn1_norm, v_ffn1_w_gate, v_ffn1_w_up, v_ffn1_w_down, v_mix_norm, v_w_in, v_a_dw_w, v_a_dw_b,
                         v_a_ln_g, v_a_ln_b, v_a_w_out, v_b_conv_w, v_b_w_out, v_w_o, v_ffn2_norm, v_ffn2_w_gate,
                         v_ffn2_w_up, v_ffn2_w_down, v_final_norm]))
    transposed = ("ffn1_w_gate", "ffn1_w_up", "ffn2_w_gate", "ffn2_w_up")
    vecs = ["ffn1_norm", "mix_norm", "a_dw_b", "a_ln_g", "a_ln_b", "ffn2_norm", "final_norm"]
    ffn1 = ["ffn1_w_gate", "ffn1_w_up", "ffn1_w_down"]
    ffn2 = ["ffn2_w_gate", "ffn2_w_up", "ffn2_w_down"]
    outp = ["a_w_out", "b_w_out", "w_o"]

    S, D = x.shape[1], x.shape[2]
    CB = D // NS
    KA, KB = a_dw_w.shape[1], b_conv_w.shape[1]
    px, py, pc = lax.axis_index("x"), lax.axis_index("y"), lax.axis_index("c")
    chip = 2 * px + py
    place = jnp.stack([chip, pc]).astype(jnp.int32)
    h0 = x.reshape(S, D)
    tgt = loss_target.reshape(S, D)
    row = lambda n: pltpu.with_memory_space_constraint(W[n].reshape(1, D), pltpu.HBM)
    pad = lambda a, r: jnp.concatenate([a, jnp.zeros((r - a.shape[0], a.shape[1]), F32)], axis=0)

    def quarter(P, n):
        return jnp.transpose(P[n][0]) if n in transposed else P[n][0]

    def unquarter(a, n):
        return (jnp.transpose(a) if n in transposed else a).reshape(W[n].shape)

    wq = {n: quarter(W, n).astype(BF16) for n in ffn1 + ffn2 + outp + ["w_in"]}

    f1 = _exchange("gather_ffn1", [_Gather([wq[n] for n in ffn1], ["rows"] * 3)])[0]
    g_in = _Gather([wq["w_in"], pad(a_dw_w[0], 32), pad(b_conv_w[0], 16)], ["rows"] * 3)
    (h1, n1, gp1, up1), ((win, taps_a, taps_b),) = _ffn_fwd(h0, row("ffn1_norm"), *f1, "ffn1_fwd", [g_in])
    win = win.reshape(NS, D, -1)
    wa_taps = taps_a.reshape(NS, 32, CB)[:, :KA]
    wb_taps = taps_b.reshape(NS, 16, CB)[:, :KB]
    g_out = _Gather([wq[n] for n in outp], ["rows"] * 3)
    (u, z), ((wa_out, wb_out, wo),) = _mix_in_fwd(h1, row("mix_norm"), win, [g_out])
    g_f2 = _Gather([wq["ffn2_w_gate"], wq["ffn2_w_up"]], ["rows"] * 2)
    (a1, q), ((f2g, f2u),) = _conv_fwd(z, wa_taps, row("a_dw_b"), wb_taps, [g_f2])
    (h2, a3, mm, ya, yb), ((f2d,),) = _mix_out_fwd(h1, a1, q, z, row("a_ln_g"), row("a_ln_b"), wa_out, wb_out, wo,
                                                   [_Gather([wq["ffn2_w_down"]], ["rows"])])
    (dh3, do2, d_final, loss_part, n2, gp2, up2), _ = _ffn_fwd(h2, row("ffn2_norm"), f2g, f2u, f2d, "ffn2_fwd_loss",
                                                               head=(tgt, row("final_norm")))

    (dgp2, dup2, act2), _ = _ffn_bwd_hidden(do2, gp2, up2, f2d, "ffn2_bwd_hidden")
    (dh2, d_ffn2), _ = _ffn_bwd_input(dh3, h2, row("ffn2_norm"), dgp2, dup2, f2g, f2u, "ffn2_bwd_input")
    g2 = [_ffn_wgrad(dgp2, n2, "ffn2_dwg")[0], _ffn_wgrad(dup2, n2, "ffn2_dwu")[0], _ffn_wgrad(act2, do2, "ffn2_dwd")[0]]
    (da1, dq, dga, dgb, dya, dyb, dh2b, d_lng, d_lnb), (got,) = _mix_out_bwd(
        dh2, a1, z, ya, yb, row("a_ln_g"), row("a_ln_b"), wa_out, wb_out, wo, [_ToSibling(g2, ["rows"] * 3)])
    wire2, own2 = _chip_sums(place, g2, got, "rows", "ffn2_chip_sums")
    (dz, d_wa, d_ba, d_wb), (got,) = _conv_bwd(z, da1, dq, dga, dgb, wa_taps, wb_taps, [_ToChips(wire2)])
    half2 = _totals(place, own2, got, "ffn2_totals")
    (g_win,), (tot2,) = _w_in_grad(u, dz, [_SwapHalves(half2)])
    go, (got,) = _mixer_wgrads(a3, dya, q, dyb, mm, dh2b, [_ToSibling([g_win], ["cols"])])
    wire_in, own_in = _chip_sums(place, [g_win], got, "cols", "w_in_chip_sum")
    (dh1, d_mix, do1), (land_in, got_o) = _mix_in_bwd(dh2, h1, row("mix_norm"), dz, win,
                                                      [_ToChips(wire_in), _ToSibling(go, ["rows"] * 3)])
    half_in = _totals(place, own_in, land_in, "w_in_total")
    wire_o, own_o = _chip_sums(place, go, got_o, "rows", "mixer_chip_sums")
    (dgp1, dup1, act1), (tot_in, land_o) = _ffn_bwd_hidden(do1, gp1, up1, f1[2], "ffn1_bwd_hidden",
                                                           [_SwapHalves(half_in), _ToChips(wire_o)])
    half_o = _totals(place, own_o, land_o, "mixer_totals")
    g1g, (tot_o,) = _ffn_wgrad(dgp1, n1, "ffn1_dwg", [_SwapHalves(half_o)])
    g1u, (got_g,) = _ffn_wgrad(dup1, n1, "ffn1_dwu", [_ToSibling([g1g], ["rows"])])
    wire_g, own_g = _chip_sums(place, [g1g], got_g, "rows", "ffn1_dwg_chip_sum")
    g1d, (got_u, land_g) = _ffn_wgrad(act1, do1, "ffn1_dwd", [_ToSibling([g1u], ["rows"]), _ToChips(wire_g)])
    wire_u, own_u = _chip_sums(place, [g1u], got_u, "rows", "ffn1_dwu_chip_sum")
    half_g = _totals(place, own_g, land_g, "ffn1_dwg_total")
    ffn1_in = (dh1, h0, row("ffn1_norm"), dgp1, dup1, f1[0], f1[1])
    (dx, dg_a), (got_d, land_u, tot_g) = _ffn_bwd_input(
        *ffn1_in, "ffn1_bwd_input_a", [_ToSibling([g1d], ["rows"]), _ToChips(wire_u), _SwapHalves(half_g)], part=(0, 2))
    wire_d, own_d = _chip_sums(place, [g1d], got_d, "rows", "ffn1_dwd_chip_sum")
    half_u = _totals(place, own_u, land_u, "ffn1_dwu_total")
    (dx, dg_b), (land_d, tot_u) = _ffn_bwd_input(*ffn1_in, "ffn1_bwd_input_b", [_ToChips(wire_d), _SwapHalves(half_u)],
                                                 part=(1, 2), into=dx)
    half_d = _totals(place, own_d, land_d, "ffn1_dwd_total")
    (tot_d,) = _exchange("tail_exchange", [_SwapHalves(half_d)])
    tot1 = [tot_g[0], tot_u[0], tot_d[0]]
    totals = dict(zip(ffn2 + ["w_in"] + ffn1 + outp, list(tot2) + list(tot_in) + tot1 + list(tot_o)))

    vec_grads = {"ffn1_norm": [dg_a, dg_b], "mix_norm": [d_mix], "a_dw_b": [d_ba], "a_ln_g": [d_lng], "a_ln_b": [d_lnb],
                 "ffn2_norm": [d_ffn2], "final_norm": [d_final]}
    small = _allreduce_small([vec_grads[n] for n in vecs], d_wa, d_wb, loss_part)
    loss = small[LOSS_ROW, 0]
    taps = ["a_dw_w", "b_conv_w"]
    small_out = _small_adamw(place, small, [[P[n].reshape(1, D) for P in (W, M, V)] for n in vecs],
                             [[P[n] for P in (W, M, V)] for n in taps])

    grads, deltas, new_m, new_v = {}, {}, {}, {}
    for n, (g_, d_, m_, v_) in zip(vecs + taps, small_out):
        shp = W[n].shape
        grads[n], deltas[n], new_m[n], new_v[n] = g_.reshape(shp), d_.reshape(shp), m_.reshape(shp), v_.reshape(shp)
    for group, tag in ((ffn1 + ffn2, "ffn"), (["w_in"], "w_in"), (outp, "mixer")):
        ds, ms, vs = _adamw([quarter(W, n) for n in group], [totals[n] for n in group], [quarter(M, n) for n in group],
                            [quarter(V, n) for n in group], tag + "_adamw")
        for n, d_, m_, v_ in zip(group, ds, ms, vs):
            grads[n], deltas[n], new_m[n], new_v[n] = (unquarter(totals[n], n), unquarter(d_, n), unquarter(m_, n),
                                                       unquarter(v_, n))
    return (loss, dx.reshape(x.shape), *[grads[n] for n in names], *[deltas[n] for n in names],
            *[new_m[n] for n in names], *[new_v[n] for n in names])
```

```python
import functools

import jax
import jax.numpy as jnp
from jax import lax
from jax.experimental import pallas as pl
from jax.experimental.pallas import tpu as pltpu

F32 = jnp.float32
BF16 = jnp.bfloat16
EPS = 1e-6
NS = 4
HALO = 32
MESH = pl.DeviceIdType.MESH
IN_HBM = pl.BlockSpec(memory_space=pltpu.HBM)

ADAM_LR = 0.001
ADAM_B1 = 0.9
ADAM_B2 = 0.999
ADAM_EPS = 1e-08
ADAM_WD = 0.01
ADAM_STEP = 10


def _cparams(n_axes, vmem_mb):
    return pltpu.CompilerParams(dimension_semantics=("arbitrary",) * n_axes, vmem_limit_bytes=vmem_mb << 20)


def _tile(n, t):
    return t if n % t == 0 else n


def _row_parts(rows, n=2):
    if rows % (16 * n):
        return [slice(0, rows)]
    return [slice(p * (rows // n), (p + 1) * (rows // n)) for p in range(n)]


def _resident(shape):
    return pl.BlockSpec(shape, lambda *_: (0,) * len(shape), pipeline_mode=pl.Buffered(1))


def _row_tile(n, cap=256):
    for t in (256, 176, 128, 64, 32, 16, 8):
        if t <= cap and n % t == 0:
            return t
    return n


def _dot(a, b):
    return jnp.dot(a, b, preferred_element_type=F32)


def _dot_nt(a, b):
    return lax.dot_general(a, b, (((1,), (1,)), ((), ())), preferred_element_type=F32)


def _dot_tn(a, b):
    return lax.dot_general(a, b, (((0,), (0,)), ((), ())), preferred_element_type=F32)


def _sigmoid(x):
    return jax.nn.sigmoid(x)


def _rms_fwd(x, g):
    r = lax.rsqrt(jnp.mean(x * x, axis=-1, keepdims=True) + EPS)
    return x * r * g


def _rms_bwd(x, g, dn):
    r = lax.rsqrt(jnp.mean(x * x, axis=-1, keepdims=True) + EPS)
    xr = x * r
    dg = jnp.sum(dn * xr, axis=0, keepdims=True)
    w = dn * g
    dx = r * w - xr * (r * r) * jnp.mean(x * w, axis=-1, keepdims=True)
    return dx, dg


def _place():
    x, y, c = lax.axis_index("x"), lax.axis_index("y"), lax.axis_index("c")
    chips = [(1 - x, y), (x, 1 - y), (1 - x, 1 - y)]
    return x, y, c, chips


def _quarter_shape(full_shape, kind):
    r, c = full_shape
    return (r // NS, c) if kind == "rows" else (r, c // NS)


def _half_of_quarter(ref, kind, q, pc):
    qr, qc = _quarter_shape(ref.shape, kind)
    h = qr // 2
    if kind == "rows":
        return ref.at[pl.ds(q * qr + pc * h, h), :]
    return ref.at[pl.ds(pc * h, h), pl.ds(q * qc, qc)]


def _quarter(ref, kind, q):
    qr, qc = _quarter_shape(ref.shape, kind)
    if kind == "rows":
        return ref.at[pl.ds(q * qr, qr), :]
    return ref.at[:, pl.ds(q * qc, qc)]


def _rows_half(ref, pc):
    h = ref.shape[0] // 2
    return ref.at[pl.ds(pc * h, h)]


class _Gather:
    def __init__(self, quarters, kinds):
        self.ins = list(quarters)
        self.kinds = list(kinds)
        n = len(self.ins)
        self.out_shape = [jax.ShapeDtypeStruct((NS * a.shape[0], a.shape[1]) if k == "rows" else (a.shape[0], NS * a.shape[1]),
                                               a.dtype) for a, k in zip(self.ins, self.kinds)]
        self.scratch = [pltpu.SemaphoreType.DMA((n, 6)), pltpu.SemaphoreType.DMA((n, 6)), pltpu.SemaphoreType.DMA((n,))]
        self.aliases = {}

    def _copy(self, outs, sems, a, k, q, pc, to, src=None):
        dst = _half_of_quarter(outs[a], self.kinds[a], q, pc)
        return pltpu.make_async_remote_copy(src_ref=dst if src is None else src, dst_ref=dst,
                                            send_sem=sems[0].at[a, k], recv_sem=sems[1].at[a, k],
                                            device_id=to, device_id_type=MESH)

    def _mine(self, ins, outs, sems, a, p):
        return pltpu.make_async_copy(ins[a], _quarter(outs[a], self.kinds[a], p), sems[2].at[a])

    def start(self, ins, outs, sems):
        x, y, c, chips = _place()
        p = 2 * x + y
        for a in range(len(ins)):
            self._mine(ins, outs, sems, a, p).start()
            for j, chip in enumerate(chips):
                self._copy(outs, sems, a, j, p, c, (*chip, c), src=_rows_half(ins[a], c)).start()

    def relay(self, ins, outs, sems):
        x, y, c, chips = _place()
        sibling = (x, y, 1 - c)
        for a in range(len(ins)):
            for j, (qx, qy) in enumerate(chips):
                q = 2 * qx + qy
                self._copy(outs, sems, a, j, q, c, sibling).wait_recv()
                self._copy(outs, sems, a, 3 + j, q, c, sibling).start()

    def finish(self, ins, outs, sems):
        x, y, c, chips = _place()
        p = 2 * x + y
        sibling = (x, y, 1 - c)
        n = len(ins)
        for a in range(n):
            for j, (qx, qy) in enumerate(chips):
                q = 2 * qx + qy
                self._copy(outs, sems, a, 3 + j, q, 1 - c, sibling).wait_recv()
                self._copy(outs, sems, a, j, p, c, (qx, qy, c), src=_rows_half(ins[a], c)).wait_send()
                self._copy(outs, sems, a, 3 + j, q, c, sibling).wait_send()
            self._mine(ins, outs, sems, a, p).wait()


class _ToSibling:
    def __init__(self, grads, kinds):
        self.ins = list(grads)
        self.kinds = list(kinds)
        n = len(self.ins)
        self.out_shape = []
        for g, k in zip(self.ins, self.kinds):
            qr, qc = _quarter_shape(g.shape, k)
            self.out_shape.append(jax.ShapeDtypeStruct((NS, qr // 2, qc), g.dtype))
        self.scratch = [pltpu.SemaphoreType.DMA((n, NS)), pltpu.SemaphoreType.DMA((n, NS))]
        self.aliases = {}

    def _copies(self, ins, outs, sems):
        x, y, c, _ = _place()
        return [pltpu.make_async_remote_copy(src_ref=_half_of_quarter(ins[a], self.kinds[a], q, 1 - c), dst_ref=outs[a].at[q],
                                             send_sem=sems[0].at[a, q], recv_sem=sems[1].at[a, q],
                                             device_id=(x, y, 1 - c), device_id_type=MESH)
                for a in range(len(ins)) for q in range(NS)]

    def start(self, ins, outs, sems):
        for cp in self._copies(ins, outs, sems):
            cp.start()

    def finish(self, ins, outs, sems):
        for cp in self._copies(ins, outs, sems):
            cp.wait()


class _ToChips:
    def __init__(self, sums, which=(0, 1, 2)):
        self.ins = list(sums)
        self.which = tuple(which)
        n, m = len(self.ins), len(self.which)
        self.out_shape = [jax.ShapeDtypeStruct((m,) + s.shape[1:], s.dtype) for s in self.ins]
        self.scratch = [pltpu.SemaphoreType.DMA((n, m)), pltpu.SemaphoreType.DMA((n, m))]
        self.aliases = {}

    def _copies(self, ins, outs, sems):
        x, y, c, chips = _place()
        return [pltpu.make_async_remote_copy(src_ref=ins[a].at[2 * chips[j][0] + chips[j][1]], dst_ref=outs[a].at[k],
                                             send_sem=sems[0].at[a, k], recv_sem=sems[1].at[a, k],
                                             device_id=(*chips[j], c), device_id_type=MESH)
                for a in range(len(ins)) for k, j in enumerate(self.which)]

    def start(self, ins, outs, sems):
        for cp in self._copies(ins, outs, sems):
            cp.start()

    def finish(self, ins, outs, sems):
        for cp in self._copies(ins, outs, sems):
            cp.wait()


class _SwapHalves:
    def __init__(self, quarters):
        self.ins = list(quarters)
        n = len(self.ins)
        self.out_shape = [jax.ShapeDtypeStruct(g.shape, g.dtype) for g in self.ins]
        self.scratch = [pltpu.SemaphoreType.DMA((n,)), pltpu.SemaphoreType.DMA((n,))]
        self.aliases = {a: a for a in range(n)}

    def _copy(self, outs, sems, a, pc):
        x, y, c, _ = _place()
        rows = _rows_half(outs[a], pc)
        return pltpu.make_async_remote_copy(src_ref=rows, dst_ref=rows, send_sem=sems[0].at[a], recv_sem=sems[1].at[a],
                                            device_id=(x, y, 1 - c), device_id_type=MESH)

    def start(self, ins, outs, sems):
        c = lax.axis_index("c")
        for a in range(len(outs)):
            self._copy(outs, sems, a, c).start()

    def finish(self, ins, outs, sems):
        c = lax.axis_index("c")
        for a in range(len(outs)):
            self._copy(outs, sems, a, c).wait_send()
            self._copy(outs, sems, a, 1 - c).wait_recv()


def _call(name, grid, compute, in_specs, out_specs, out_shape, scratch, vmem_mb, args, jobs=(), own_aliases=None):
    n_in, n_out, n_scr = len(in_specs), len(out_specs), len(scratch)
    ji = [len(j.ins) for j in jobs]
    jo = [len(j.out_shape) for j in jobs]
    js = [len(j.scratch) for j in jobs]

    def body(*refs):
        pos = [0]

        def take(k):
            r = refs[pos[0]:pos[0] + k]
            pos[0] += k
            return r

        ins, jins = take(n_in), [take(k) for k in ji]
        outs, jouts = take(n_out), [take(k) for k in jo]
        scr, jscr = take(n_scr), [take(k) for k in js]
        if jobs and grid:
            ids = [pl.program_id(a) for a in range(len(grid))]
            first = functools.reduce(jnp.logical_and, [i == 0 for i in ids])
            last = functools.reduce(jnp.logical_and, [i == g - 1 for i, g in zip(ids, grid)])

            @pl.when(first)
            def _():
                for j, a, b, c in zip(jobs, jins, jouts, jscr):
                    j.start(a, b, c)

            @pl.when(last)
            def _():
                for j, a, b, c in zip(jobs, jins, jouts, jscr):
                    if hasattr(j, "relay"):
                        j.relay(a, b, c)
        elif jobs:
            for j, a, b, c in zip(jobs, jins, jouts, jscr):
                j.start(a, b, c)
            for j, a, b, c in zip(jobs, jins, jouts, jscr):
                if hasattr(j, "relay"):
                    j.relay(a, b, c)
        compute(ins, outs, scr)
        if jobs and grid:
            @pl.when(last)
            def _():
                for j, a, b, c in zip(jobs, jins, jouts, jscr):
                    j.finish(a, b, c)
        elif jobs:
            for j, a, b, c in zip(jobs, jins, jouts, jscr):
                j.finish(a, b, c)

    aliases = dict(own_aliases or {})
    in_off, out_off = n_in, n_out
    for j, a, b in zip(jobs, ji, jo):
        for s, d in j.aliases.items():
            aliases[in_off + s] = out_off + d
        in_off += a
        out_off += b
    res = pl.pallas_call(
        body, name=name, grid=grid,
        in_specs=list(in_specs) + [IN_HBM] * sum(ji), out_specs=list(out_specs) + [IN_HBM] * sum(jo),
        out_shape=list(out_shape) + [pltpu.HBM(s.shape, s.dtype) for j in jobs for s in j.out_shape],
        scratch_shapes=list(scratch) + [s for j in jobs for s in j.scratch],
        input_output_aliases=aliases, compiler_params=_cparams(len(grid), vmem_mb),
    )(*args, *[a for j in jobs for a in j.ins])
    res = list(res)
    main, rest, jres = res[:n_out], res[n_out:], []
    for k in jo:
        jres.append(rest[:k])
        rest = rest[k:]
    return main, jres


def _exchange(name, jobs):
    return _call(name, (), lambda ins, outs, scr: None, [], [], [], [], 16, [], jobs)[1]


def _small_rows(ka, kb):
    first_a = 8
    first_b = first_a + -(-ka // 8) * 8
    return first_a, first_b, first_b + -(-kb // 8) * 8


LOSS_ROW = 7


def _allreduce_small(vecs, taps_a, taps_b, loss_part):
    counts = [len(v) for v in vecs]
    flat = [r for v in vecs for r in v]
    n = len(flat)
    C = flat[0].shape[1]
    NQ, KA, CB = taps_a.shape
    KB = taps_b.shape[1]
    first_a, first_b, R = _small_rows(KA, KB)
    assert len(vecs) <= LOSS_ROW < first_a
    N = 8

    def body(*refs):
        vec_refs = list(refs[:n])
        ta_ref, tb_ref, loss_ref, out_ref, v_ref, gath, send_sems, recv_sems, local_sem = refs[n:]
        v_ref[...] = jnp.zeros_like(v_ref)
        v_ref[LOSS_ROW:LOSS_ROW + 1, 0:loss_ref.shape[1]] = loss_ref[0:1, :]
        for i, k in enumerate(counts):
            parts, vec_refs = vec_refs[:k], vec_refs[k:]
            v_ref[i:i + 1, :] = functools.reduce(lambda a, b: a + b, [r[...] for r in parts])
        for q in range(NQ):
            v_ref[first_a:first_a + KA, q * CB:(q + 1) * CB] = ta_ref[q]
            v_ref[first_b:first_b + KB, q * CB:(q + 1) * CB] = tb_ref[q]
        x, y, c, chips = _place()
        me, sibling = (x, y, c), (x, y, 1 - c)

        def rows(px, py, pc):
            return gath.at[pl.ds((4 * px + 2 * py + pc) * R, R), :]

        def copy(k, block, to, src=None):
            return pltpu.make_async_remote_copy(src_ref=rows(*block) if src is None else src, dst_ref=rows(*block),
                                                send_sem=send_sems.at[k], recv_sem=recv_sems.at[k],
                                                device_id=to, device_id_type=MESH)

        mine = pltpu.make_async_copy(v_ref, rows(*me), local_sem)
        mine.start()
        first = [copy(0, me, sibling, src=v_ref)]
        first += [copy(1 + j, me, (*chip, c), src=v_ref) for j, chip in enumerate(chips)]
        for cp in first:
            cp.start()
        passed = [copy(4 + j, (*chip, c), sibling) for j, chip in enumerate(chips)]
        for j, chip in enumerate(chips):
            copy(1 + j, (*chip, c), me).wait_recv()
            passed[j].start()
        copy(0, sibling, me).wait_recv()
        for j, chip in enumerate(chips):
            copy(4 + j, (*chip, 1 - c), me).wait_recv()
        for cp in first + passed:
            cp.wait_send()
        mine.wait()
        acc = gath[0:R, :]
        for d in range(1, N):
            acc = acc + gath[d * R:(d + 1) * R, :]
        out_ref[...] = acc

    vmem = pl.BlockSpec(memory_space=pltpu.VMEM)
    return pl.pallas_call(
        body, name="allreduce_small",
        in_specs=[vmem] * (n + 3), out_specs=vmem,
        out_shape=jax.ShapeDtypeStruct((R, C), F32),
        scratch_shapes=[pltpu.VMEM((R, C), F32), pltpu.VMEM((N * R, C), F32), pltpu.SemaphoreType.DMA((7,)),
                        pltpu.SemaphoreType.DMA((7,)), pltpu.SemaphoreType.DMA],
    )(*flat, taps_a, taps_b, loss_part)


def _adamw_math(w, g, m, v):
    c1 = 1.0 - ADAM_B1 ** ADAM_STEP
    c2 = 1.0 - ADAM_B2 ** ADAM_STEP
    mn = ADAM_B1 * m + (1.0 - ADAM_B1) * g
    vn = ADAM_B2 * v + (1.0 - ADAM_B2) * (g * g)
    return -ADAM_LR * ((mn / c1) / (jnp.sqrt(vn / c2) + ADAM_EPS) + ADAM_WD * w), mn, vn


def _small_adamw(place, small, vec_wmv, tap_wmv):
    n = len(vec_wmv)
    D = small.shape[1]
    CB = tap_wmv[0][0].shape[2]
    ks = [t[0].shape[1] for t in tap_wmv]
    firsts = _small_rows(*ks)[:2]

    def body(place_ref, small_ref, *refs):
        ins, outs = refs[:3 * (n + 2)], refs[3 * (n + 2):]
        chip = place_ref[0]
        for i in range(n):
            g = small_ref[i:i + 1, :]
            d, mn, vn = _adamw_math(ins[3 * i][...], g, ins[3 * i + 1][...], ins[3 * i + 2][...])
            for o, val in zip(outs[4 * i:4 * i + 4], (g, d, mn, vn)):
                o[...] = val
        for t, (row0, k) in enumerate(zip(firsts, ks)):
            g = jnp.zeros((k, CB), F32)
            for q in range(D // CB):
                g = g + jnp.where(chip == q, small_ref[row0:row0 + k, q * CB:(q + 1) * CB], 0.0)
            w_ref, m_ref, v_ref = ins[3 * (n + t):3 * (n + t) + 3]
            d, mn, vn = _adamw_math(w_ref[0], g, m_ref[0], v_ref[0])
            for o, val in zip(outs[4 * (n + t):4 * (n + t) + 4], (g, d, mn, vn)):
                o[0] = val

    flat = [a for wmv in list(vec_wmv) + list(tap_wmv) for a in wmv]
    shapes = [jax.ShapeDtypeStruct(wmv[0].shape, F32) for wmv in list(vec_wmv) + list(tap_wmv) for _ in range(4)]
    vmem = pl.BlockSpec(memory_space=pltpu.VMEM)
    res = pl.pallas_call(
        body, name="small_adamw",
        in_specs=[pl.BlockSpec(memory_space=pltpu.SMEM)] + [vmem] * (1 + len(flat)), out_specs=[vmem] * len(shapes),
        out_shape=shapes,
    )(place, small, *flat)
    return [res[4 * i:4 * i + 4] for i in range(n + 2)]


def _ffn_fwd(h, g, wg, wu, wd, name, jobs=(), head=None):
    S, D = h.shape
    F = wg.shape[0]
    ts = _tile(S, 512)
    fb = _tile(F, F // 2)
    nf = F // fb

    def compute(ins, outs, scr):
        h_ref, g_ref, wg_ref, wu_ref, wd_ref = ins[:5]
        n_ref, gp_ref, up_ref = outs[-3:]
        x = h_ref[...]
        n = _rms_fwd(x, g_ref[...]).astype(BF16)
        n_ref[...] = n
        acc = None
        for j in range(nf):
            cols = slice(j * fb, (j + 1) * fb)
            gp = _dot_nt(n, wg_ref[cols, :])
            up = _dot_nt(n, wu_ref[cols, :])
            gp_ref[:, cols] = gp.astype(BF16)
            up_ref[:, cols] = up.astype(BF16)
            part = _dot((gp * _sigmoid(gp) * up).astype(BF16), wd_ref[cols, :])
            acc = part if acc is None else acc + part
        ho = x + 0.5 * acc
        if head is None:
            outs[0][...] = ho
            return
        t_ref, gf_ref = ins[5:]
        dh_ref, do_ref, dgf_ref, loss_ref = outs[:4]

        @pl.when(pl.program_id(0) == 0)
        def _():
            dgf_ref[...] = jnp.zeros_like(dgf_ref)
            loss_ref[...] = jnp.zeros_like(loss_ref)

        err = _rms_fwd(ho, gf_ref[...]) - t_ref[...]
        loss_ref[...] += (0.5 / D) * jnp.sum(err * err)
        dx, dg = _rms_bwd(ho, gf_ref[...], err * (1.0 / D))
        dh_ref[...] = dx
        do_ref[...] = (0.5 * dx).astype(BF16)
        dgf_ref[...] += dg

    tok = pl.BlockSpec((ts, D), lambda i: (i, 0))
    row = pl.BlockSpec((1, D), lambda i: (0, 0))
    wsp = _resident((F, D))
    hid = pl.BlockSpec((ts, F), lambda i: (i, 0))
    saved = [jax.ShapeDtypeStruct((S, D), BF16), jax.ShapeDtypeStruct((S, F), BF16), jax.ShapeDtypeStruct((S, F), BF16)]
    if head is None:
        return _call(name, (S // ts,), compute, [tok, row, wsp, wsp, wsp], [tok, tok, hid, hid],
                     [jax.ShapeDtypeStruct((S, D), F32)] + saved, [], 56, [h, g, wg, wu, wd], jobs)
    return _call(name, (S // ts,), compute, [tok, row, wsp, wsp, wsp, tok, row],
                 [tok, tok, row, pl.BlockSpec((8, 128), lambda i: (0, 0)), tok, hid, hid],
                 [jax.ShapeDtypeStruct((S, D), F32), jax.ShapeDtypeStruct((S, D), BF16), jax.ShapeDtypeStruct((1, D), F32),
                  jax.ShapeDtypeStruct((8, 128), F32)] + saved, [], 60, [h, g, wg, wu, wd, *head], jobs)


def _ffn_bwd_hidden(do, gp, up, wd, name, jobs=()):
    S, D = do.shape
    F = wd.shape[0]
    ts = _tile(S, 1024)
    fb = _tile(F, F // 2)
    nt, total = S // ts, (F // fb) * (S // ts)
    depth = 3

    def compute(ins, outs, scr):
        do_ref, gp_hbm, up_hbm, wd_ref = ins
        dgp_ref, dup_ref, a_ref = outs
        gring, uring, sems = scr
        t = pl.program_id(0) * nt + pl.program_id(1)

        def fetch(step):
            slot = lax.rem(step, depth)
            src = (pl.ds(lax.rem(step, nt) * ts, ts), pl.ds(lax.div(step, nt) * fb, fb))
            return (pltpu.make_async_copy(gp_hbm.at[src], gring.at[slot], sems.at[0, slot]),
                    pltpu.make_async_copy(up_hbm.at[src], uring.at[slot], sems.at[1, slot]))

        @pl.when(t == 0)
        def _():
            for k in range(min(depth - 1, total)):
                for cp in fetch(jnp.int32(k)):
                    cp.start()

        @pl.when(t + (depth - 1) < total)
        def _():
            for cp in fetch(t + (depth - 1)):
                cp.start()

        for cp in fetch(t):
            cp.wait()
        slot = lax.rem(t, depth)
        parts = _row_parts(ts, 4)
        das = [_dot_nt(do_ref[rows, :], wd_ref[...]) for rows in parts]
        for rows, da in zip(parts, das):
            gf = gring[slot, rows, :].astype(F32)
            uf = uring[slot, rows, :].astype(F32)
            sg = _sigmoid(gf)
            si = gf * sg
            dgp_ref[rows, :] = (da * uf * (sg * (1.0 + gf * (1.0 - sg)))).astype(BF16)
            dup_ref[rows, :] = (da * si).astype(BF16)
            a_ref[rows, :] = (si * uf).astype(BF16)

    tok = pl.BlockSpec((ts, D), lambda s, i: (i, 0))
    hid = pl.BlockSpec((ts, fb), lambda s, i: (i, s))
    return _call(name, (F // fb, nt), compute, [tok, IN_HBM, IN_HBM, pl.BlockSpec((fb, D), lambda s, i: (s, 0))],
                 [hid, hid, hid], [jax.ShapeDtypeStruct((S, F), BF16)] * 3,
                 [pltpu.VMEM((depth, ts, fb), BF16), pltpu.VMEM((depth, ts, fb), BF16), pltpu.SemaphoreType.DMA((2, depth))],
                 60, [do, gp, up, wd], jobs)


def _ffn_bwd_input(dh, h, g, dgp, dup, wg, wu, name, jobs=(), part=(0, 1), into=None):
    S, D = h.shape
    F = wg.shape[0]
    ts = _tile(S, 512)
    steps = S // ts // part[1]
    first = part[0] * steps

    def compute(ins, outs, scr):
        dh_ref, h_ref, g_ref, dgp_ref, dup_ref, wg_ref, wu_ref = ins[:7]
        dhi_ref, dg_ref = outs

        @pl.when(pl.program_id(0) == 0)
        def _():
            dg_ref[...] = jnp.zeros_like(dg_ref)

        dn = _dot(dgp_ref[...], wg_ref[...]) + _dot(dup_ref[...], wu_ref[...])
        dx, dg = _rms_bwd(h_ref[...], g_ref[...], dn)
        dhi_ref[...] = dh_ref[...] + dx
        dg_ref[...] += dg

    tok = pl.BlockSpec((ts, D), lambda i: (first + i, 0))
    hid = pl.BlockSpec((ts, F), lambda i: (first + i, 0))
    row = pl.BlockSpec((1, D), lambda i: (0, 0))
    in_specs = [tok, tok, row, hid, hid, _resident((F, D)), _resident((F, D))]
    args = [dh, h, g, dgp, dup, wg, wu]
    if into is not None:
        in_specs, args = in_specs + [IN_HBM], args + [into]
    return _call(name, (steps,), compute, in_specs, [tok, row],
                 [jax.ShapeDtypeStruct((S, D), F32), jax.ShapeDtypeStruct((1, D), F32)], [], 56, args, jobs,
                 own_aliases=None if into is None else {7: 0})


def _ffn_wgrad(hid, tok, name, jobs=()):
    S, D = tok.shape
    F = hid.shape[1]
    fb = _tile(F, F // 2)

    def compute(ins, outs, scr):
        outs[0][...] = _dot_tn(ins[0][...], ins[1][...])

    main, jres = _call(name, (F // fb,), compute,
                       [pl.BlockSpec((S, fb), lambda j: (0, j)), _resident(tok.shape)],
                       [pl.BlockSpec((fb, D), lambda j: (j, 0))], [jax.ShapeDtypeStruct((F, D), F32)], [], 56,
                       [hid, tok], jobs)
    return main[0], jres


def _w_in_pieces(D, cq, ng):
    groups = []
    for k in range(ng):
        lo, hi, pieces = k * D, (k + 1) * D, []
        while lo < hi:
            q = lo // cq
            w = min(hi, (q + 1) * cq) - lo
            pieces.append((q, lo - q * cq, w, lo - k * D))
            lo += w
        groups.append(pieces)
    return groups


def _mix_in_fwd(h, g, win, jobs=()):
    S, D = h.shape
    NG = win.shape[0] * win.shape[2] // D
    pieces = _w_in_pieces(D, win.shape[2], NG)
    ts = _tile(S, 512)

    def compute(ins, outs, scr):
        h_ref, g_ref, w_ref = ins
        u_ref, z_ref = outs
        u = _rms_fwd(h_ref[...], g_ref[...]).astype(BF16)
        u_ref[...] = u
        for k in range(NG):
            for q, c0, w, d0 in pieces[k]:
                z_ref[k, :, d0:d0 + w] = _dot(u, w_ref[q, :, c0:c0 + w]).astype(BF16)

    return _call("mix_in_fwd", (S // ts,), compute,
                 [pl.BlockSpec((ts, D), lambda i: (i, 0)), pl.BlockSpec((1, D), lambda i: (0, 0)), _resident(win.shape)],
                 [pl.BlockSpec((ts, D), lambda i: (i, 0)), pl.BlockSpec((NG, ts, D), lambda i: (0, i, 0))],
                 [jax.ShapeDtypeStruct((S, D), BF16), jax.ShapeDtypeStruct((NG, S, D), BF16)],
                 [], 48, [h, g, win], jobs)


SUBLANES = 8


def _shifted_copies(s):
    n = s.shape[1] - SUBLANES
    for r in range(1, SUBLANES):
        s[r, 0:n, :] = s[0, r:r + n, :]


def _window(s, o, rows):
    r = o % SUBLANES
    return s[r, o - r:o - r + rows, :]


def _conv_fwd(z, wa, ba, wb, jobs=()):
    _, S, D = z.shape
    _, KA, CB = wa.shape
    KB = wb.shape[1]
    ts = _tile(S, 1024)
    r = ts // HALO
    CH = min(64, ts)

    def compute(ins, outs, scr):
        z_ref, zh_ref, wa_ref, ba_ref, wb_ref = ins
        a1_ref, q_ref = outs
        sa, sb = scr
        keep = (pl.program_id(1) > 0).astype(F32)
        sa[0, HALO:HALO + ts, :] = z_ref[0].astype(F32) * _sigmoid(z_ref[1].astype(F32))
        sa[0, 0:HALO, :] = zh_ref[0].astype(F32) * _sigmoid(zh_ref[1].astype(F32)) * keep
        _shifted_copies(sa)
        sb[HALO:HALO + ts, :] = z_ref[3].astype(F32) * z_ref[4].astype(F32)
        sb[0:HALO, :] = zh_ref[3].astype(F32) * zh_ref[4].astype(F32) * keep
        wak = [wa_ref[0, k:k + 1, :] for k in range(KA)]
        wbk = [wb_ref[0, k:k + 1, :] for k in range(KB)]
        for c0 in range(0, ts, CH):
            acc = jnp.broadcast_to(ba_ref[...], (CH, CB))
            for k in range(KA):
                acc = acc + wak[k] * _window(sa, c0 + HALO - (KA - 1) + k, CH)
            a1_ref[c0:c0 + CH, :] = acc
            v = jnp.zeros((CH, CB), F32)
            for k in range(KB):
                o = c0 + HALO - (KB - 1) + k
                v = v + wbk[k] * sb[o:o + CH, :]
            q_ref[c0:c0 + CH, :] = (z_ref[2, c0:c0 + CH, :].astype(F32) * v).astype(BF16)

    return _call("conv_fwd", (D // CB, S // ts), compute,
                 [pl.BlockSpec((5, ts, CB), lambda j, i: (0, i, j)),
                  pl.BlockSpec((5, HALO, CB), lambda j, i: (0, jnp.maximum(i * r - 1, 0), j)),
                  pl.BlockSpec((1, KA, CB), lambda j, i: (j, 0, 0)), pl.BlockSpec((1, CB), lambda j, i: (0, j)),
                  pl.BlockSpec((1, KB, CB), lambda j, i: (j, 0, 0))],
                 [pl.BlockSpec((ts, CB), lambda j, i: (i, j)), pl.BlockSpec((ts, CB), lambda j, i: (i, j))],
                 [jax.ShapeDtypeStruct((S, D), F32), jax.ShapeDtypeStruct((S, D), BF16)],
                 [pltpu.VMEM((SUBLANES, HALO + ts, CB), F32), pltpu.VMEM((HALO + ts, CB), F32)], 40, [z, z, wa, ba, wb], jobs)


def _ln_stats(a1):
    mu = jnp.mean(a1, axis=-1, keepdims=True)
    xc = a1 - mu
    rstd = lax.rsqrt(jnp.mean(xc * xc, axis=-1, keepdims=True) + EPS)
    return xc * rstd, rstd


def _mix_out_fwd(h1, a1, q, z, lng, lnb, wa, wb, wo, jobs=()):
    S, D = h1.shape
    ts = _tile(S, 512)

    def compute(ins, outs, scr):
        h_ref, a1_ref, q_ref, ga_ref, gb_ref, lng_ref, lnb_ref, wa_ref, wb_ref, wo_ref = ins
        h2_ref, a3_ref, m_ref, ya_ref, yb_ref = outs
        xhat, _ = _ln_stats(a1_ref[...])
        a2 = xhat * lng_ref[...] + lnb_ref[...]
        a3 = (a2 * _sigmoid(a2)).astype(BF16)
        a3_ref[...] = a3
        ya = _dot(a3, wa_ref[...])
        yb = _dot(q_ref[...], wb_ref[...])
        ya_ref[...] = ya.astype(BF16)
        yb_ref[...] = yb.astype(BF16)
        m = (_sigmoid(ga_ref[0].astype(F32)) * ya + _sigmoid(gb_ref[0].astype(F32)) * yb).astype(BF16)
        m_ref[...] = m
        h2_ref[...] = h_ref[...] + _dot(m, wo_ref[...])

    tok = pl.BlockSpec((ts, D), lambda i: (i, 0))
    row = pl.BlockSpec((1, D), lambda i: (0, 0))
    mat = _resident((D, D))
    return _call("mix_out_fwd", (S // ts,), compute,
                 [tok, tok, tok, pl.BlockSpec((1, ts, D), lambda i: (5, i, 0)), pl.BlockSpec((1, ts, D), lambda i: (6, i, 0)),
                  row, row, mat, mat, mat], [tok] * 5,
                 [jax.ShapeDtypeStruct((S, D), F32)] + [jax.ShapeDtypeStruct((S, D), BF16)] * 4,
                 [], 56, [h1, a1, q, z, z, lng, lnb, wa, wb, wo], jobs)


def _mix_out_bwd(dh2, a1, z, ya, yb, lng, lnb, wa, wb, wo, jobs=()):
    S, D = dh2.shape
    ts = _tile(S, 512)
    total = S // ts
    depth = 3

    def compute(ins, outs, scr):
        dh_hbm, a1_hbm, ga_ref, gb_ref, ya_ref, yb_ref, lng_ref, lnb_ref, wa_ref, wb_ref, wo_ref = ins
        da1_ref, dq_ref, dga_ref, dgb_ref, dya_ref, dyb_ref, dhb_ref, dlg_ref, dlb_ref = outs
        dh_ring, a1_ring, sems = scr
        t = pl.program_id(0)

        def fetch(step):
            slot = lax.rem(step, depth)
            src = pl.ds(step * ts, ts)
            return (pltpu.make_async_copy(dh_hbm.at[src], dh_ring.at[slot], sems.at[0, slot]),
                    pltpu.make_async_copy(a1_hbm.at[src], a1_ring.at[slot], sems.at[1, slot]))

        @pl.when(t == 0)
        def _():
            dlg_ref[...] = jnp.zeros_like(dlg_ref)
            dlb_ref[...] = jnp.zeros_like(dlb_ref)
            for k in range(min(depth - 1, total)):
                for cp in fetch(jnp.int32(k)):
                    cp.start()

        @pl.when(t + (depth - 1) < total)
        def _():
            for cp in fetch(t + (depth - 1)):
                cp.start()

        for cp in fetch(t):
            cp.wait()
        dh_ref, a1_ref = dh_ring.at[lax.rem(t, depth)], a1_ring.at[lax.rem(t, depth)]
        for rows in _row_parts(ts):
            dhb = dh_ref[rows, :].astype(BF16)
            dhb_ref[rows, :] = dhb
            dm = _dot_nt(dhb, wo_ref[...])
            sa = _sigmoid(ga_ref[0, rows, :].astype(F32))
            sb = _sigmoid(gb_ref[0, rows, :].astype(F32))
            dga_ref[rows, :] = (dm * ya_ref[rows, :].astype(F32) * sa * (1.0 - sa)).astype(BF16)
            dgb_ref[rows, :] = (dm * yb_ref[rows, :].astype(F32) * sb * (1.0 - sb)).astype(BF16)
            dya = (sa * dm).astype(BF16)
            dyb = (sb * dm).astype(BF16)
            dya_ref[rows, :] = dya
            dyb_ref[rows, :] = dyb
            dq_ref[rows, :] = _dot_nt(dyb, wb_ref[...]).astype(BF16)
            da3 = _dot_nt(dya, wa_ref[...])
            xhat, rstd = _ln_stats(a1_ref[rows, :])
            a2 = xhat * lng_ref[...] + lnb_ref[...]
            sg = _sigmoid(a2)
            da2 = da3 * (sg * (1.0 + a2 * (1.0 - sg)))
            dlg_ref[...] += jnp.sum(da2 * xhat, axis=0, keepdims=True)
            dlb_ref[...] += jnp.sum(da2, axis=0, keepdims=True)
            dxh = da2 * lng_ref[...]
            da1_ref[rows, :] = (rstd * (dxh - jnp.mean(dxh, axis=-1, keepdims=True)
                                        - xhat * jnp.mean(dxh * xhat, axis=-1, keepdims=True))).astype(BF16)

    tok = pl.BlockSpec((ts, D), lambda i: (i, 0))
    row = pl.BlockSpec((1, D), lambda i: (0, 0))
    mat = _resident((D, D))
    return _call("mix_out_bwd", (S // ts,), compute,
                 [IN_HBM, IN_HBM, pl.BlockSpec((1, ts, D), lambda i: (5, i, 0)), pl.BlockSpec((1, ts, D), lambda i: (6, i, 0)),
                  tok, tok, row, row, mat, mat, mat], [tok] * 7 + [row, row],
                 [jax.ShapeDtypeStruct((S, D), BF16)] * 7 + [jax.ShapeDtypeStruct((1, D), F32)] * 2,
                 [pltpu.VMEM((depth, ts, D), F32), pltpu.VMEM((depth, ts, D), F32), pltpu.SemaphoreType.DMA((2, depth))],
                 60, [dh2, a1, z, z, ya, yb, lng, lnb, wa, wb, wo], jobs)


def _mixer_wgrads(a3, dya, q, dyb, mm, dhb, jobs=()):
    S, D = a3.shape
    tk = _tile(S, 512)

    def compute(ins, outs, scr):
        @pl.when(pl.program_id(0) == 0)
        def _():
            for o in outs:
                o[...] = jnp.zeros_like(o)

        for t in range(3):
            outs[t][...] += _dot_tn(ins[2 * t][...], ins[2 * t + 1][...])

    tok = pl.BlockSpec((tk, D), lambda k: (k, 0))
    return _call("mixer_wgrads", (S // tk,), compute, [tok] * 6, [pl.BlockSpec((D, D), lambda k: (0, 0))] * 3,
                 [jax.ShapeDtypeStruct((D, D), F32)] * 3, [], 56, [a3, dya, q, dyb, mm, dhb], jobs)


def _conv_bwd(z, da1, dq, dga, dgb, wa, wb, jobs=()):
    NG, S, D = z.shape
    _, KA, CB = wa.shape
    KB = wb.shape[1]
    ts = _tile(S, 1024)
    r = ts // HALO
    nt = S // ts
    CH = min(64, ts)
    last_halo = S // HALO - 1

    def compute(ins, outs, scr):
        z_ref, zp_ref, zn_ref, da1_ref, da1n_ref, dq_ref, dqn_ref, dga_ref, dgb_ref, wa_ref, wb_ref = ins
        dz_ref, dwa_ref, dba_ref, dwb_ref = outs
        sa0, sd, sp, sv, acca, accb = scr
        i = pl.program_id(1)
        prev = (i > 0).astype(F32)
        nxt = (i < nt - 1).astype(F32)

        @pl.when(i == 0)
        def _():
            acca[...] = jnp.zeros_like(acca)
            accb[...] = jnp.zeros_like(accb)
            dba_ref[...] = jnp.zeros_like(dba_ref)

        sa0[0, HALO:HALO + ts, :] = z_ref[0].astype(F32) * _sigmoid(z_ref[1].astype(F32))
        sa0[0, 0:HALO, :] = zp_ref[0].astype(F32) * _sigmoid(zp_ref[1].astype(F32)) * prev
        _shifted_copies(sa0)
        sp[HALO:HALO + ts, :] = z_ref[3].astype(F32) * z_ref[4].astype(F32)
        sp[0:HALO, :] = zp_ref[3].astype(F32) * zp_ref[4].astype(F32) * prev
        sd[0, 0:ts, :] = da1_ref[...].astype(F32)
        sd[0, ts:ts + HALO, :] = da1n_ref[...].astype(F32) * nxt
        _shifted_copies(sd)
        sv[0:ts, :] = dq_ref[...].astype(F32) * z_ref[2].astype(F32)
        sv[ts:ts + HALO, :] = dqn_ref[...].astype(F32) * zn_ref[2].astype(F32) * nxt
        dba_ref[...] += jnp.sum(sd[0, 0:ts, :], axis=0, keepdims=True)
        wak = [wa_ref[0, k:k + 1, :] for k in range(KA)]
        wbk = [wb_ref[0, k:k + 1, :] for k in range(KB)]
        for c0 in range(0, ts, CH):
            rows = slice(c0, c0 + CH)
            d1 = sd[0, rows, :]
            da0 = jnp.zeros((CH, CB), F32)
            for k in range(KA):
                da0 = da0 + wak[k] * _window(sd, c0 + (KA - 1) - k, CH)
                a0w = _window(sa0, c0 + HALO - (KA - 1) + k, CH)
                acca[k] += jnp.sum((d1 * a0w).reshape(CH // 8, 8, CB), axis=0)
            val = z_ref[0, rows, :].astype(F32)
            sg = _sigmoid(z_ref[1, rows, :].astype(F32))
            dz_ref[0, rows, :] = (da0 * sg).astype(BF16)
            dz_ref[1, rows, :] = (da0 * val * sg * (1.0 - sg)).astype(BF16)
            dv = sv[rows, :]
            v = jnp.zeros((CH, CB), F32)
            dp = jnp.zeros((CH, CB), F32)
            for k in range(KB):
                o = c0 + HALO - (KB - 1) + k
                pw = sp[o:o + CH, :]
                v = v + wbk[k] * pw
                accb[k] += jnp.sum((dv * pw).reshape(CH // 8, 8, CB), axis=0)
                o = c0 + (KB - 1) - k
                dp = dp + wbk[k] * sv[o:o + CH, :]
            dz_ref[2, rows, :] = (dq_ref[rows, :].astype(F32) * v).astype(BF16)
            dz_ref[3, rows, :] = (dp * z_ref[4, rows, :].astype(F32)).astype(BF16)
            dz_ref[4, rows, :] = (dp * z_ref[3, rows, :].astype(F32)).astype(BF16)
        dz_ref[5] = dga_ref[...]
        dz_ref[6] = dgb_ref[...]

        @pl.when(i == nt - 1)
        def _():
            dwa_ref[0] = jnp.sum(acca[...], axis=1)
            dwb_ref[0] = jnp.sum(accb[...], axis=1)

    zt = pl.BlockSpec((5, ts, CB), lambda j, i: (0, i, j))
    zp = pl.BlockSpec((5, HALO, CB), lambda j, i: (0, jnp.maximum(i * r - 1, 0), j))
    zn = pl.BlockSpec((5, HALO, CB), lambda j, i: (0, jnp.minimum((i + 1) * r, last_halo), j))
    tok = pl.BlockSpec((ts, CB), lambda j, i: (i, j))
    tokn = pl.BlockSpec((HALO, CB), lambda j, i: (jnp.minimum((i + 1) * r, last_halo), j))
    return _call("conv_bwd", (D // CB, nt), compute,
                 [zt, zp, zn, tok, tokn, tok, tokn, tok, tok,
                  pl.BlockSpec((1, KA, CB), lambda j, i: (j, 0, 0)), pl.BlockSpec((1, KB, CB), lambda j, i: (j, 0, 0))],
                 [pl.BlockSpec((NG, ts, CB), lambda j, i: (0, i, j)), pl.BlockSpec((1, KA, CB), lambda j, i: (j, 0, 0)),
                  pl.BlockSpec((1, CB), lambda j, i: (0, j)), pl.BlockSpec((1, KB, CB), lambda j, i: (j, 0, 0))],
                 [jax.ShapeDtypeStruct((NG, S, D), BF16), jax.ShapeDtypeStruct((D // CB, KA, CB), F32),
                  jax.ShapeDtypeStruct((1, D), F32), jax.ShapeDtypeStruct((D // CB, KB, CB), F32)],
                 [pltpu.VMEM((SUBLANES, HALO + ts, CB), F32), pltpu.VMEM((SUBLANES, ts + HALO, CB), F32),
                  pltpu.VMEM((HALO + ts, CB), F32), pltpu.VMEM((ts + HALO, CB), F32),
                  pltpu.VMEM((KA, 8, CB), F32), pltpu.VMEM((KB, 8, CB), F32)],
                 48, [z, z, z, da1, da1, dq, dq, dga, dgb, wa, wb], jobs)


def _mix_in_bwd(dh2, h1, g, dz, win, jobs=()):
    S, D = h1.shape
    NG = dz.shape[0]
    pieces = _w_in_pieces(D, win.shape[2], NG)
    ts = _tile(S, 512)

    def compute(ins, outs, scr):
        dh_ref, h_ref, g_ref, dz_ref, w_ref = ins
        dhi_ref, dg_ref, do_ref = outs

        @pl.when(pl.program_id(0) == 0)
        def _():
            dg_ref[...] = jnp.zeros_like(dg_ref)

        du = None
        for k in range(NG):
            for q, c0, w, d0 in pieces[k]:
                part = _dot_nt(dz_ref[k, :, d0:d0 + w], w_ref[q, :, c0:c0 + w])
                du = part if du is None else du + part
        dx, dg = _rms_bwd(h_ref[...], g_ref[...], du)
        dhi = dh_ref[...] + dx
        dhi_ref[...] = dhi
        do_ref[...] = (0.5 * dhi).astype(BF16)
        dg_ref[...] += dg

    tok = pl.BlockSpec((ts, D), lambda i: (i, 0))
    row = pl.BlockSpec((1, D), lambda i: (0, 0))
    return _call("mix_in_bwd", (S // ts,), compute,
                 [tok, tok, row, pl.BlockSpec((NG, ts, D), lambda i: (0, i, 0)), _resident(win.shape)],
                 [tok, row, tok],
                 [jax.ShapeDtypeStruct((S, D), F32), jax.ShapeDtypeStruct((1, D), F32), jax.ShapeDtypeStruct((S, D), BF16)],
                 [], 56, [dh2, h1, g, dz, win], jobs)


def _w_in_grad(u, dz, jobs=()):
    S, D = u.shape
    NG = dz.shape[0]

    def compute(ins, outs, scr):
        outs[0][...] = _dot_tn(ins[0][...], ins[1][0])

    return _call("w_in_grad", (NG,), compute,
                 [_resident(u.shape), pl.BlockSpec((1, S, D), lambda j: (j, 0, 0))],
                 [pl.BlockSpec((D, D), lambda j: (0, j))], [jax.ShapeDtypeStruct((D, NG * D), F32)], [], 48, [u, dz], jobs)


def _chip_sums(place, grads, got, kind, name):
    n = len(grads)
    qr, qc = _quarter_shape(grads[0].shape, kind)
    h = qr // 2
    tr = _row_tile(h)
    nr = h // tr

    def body(pc_ref, *refs):
        g_refs, got_refs, b_refs, f_refs = refs[:n], refs[n:2 * n], refs[2 * n:3 * n], refs[3 * n:]
        own = pl.program_id(1) == pc_ref[0]
        for a in range(n):
            s = g_refs[a][...] + got_refs[a][0]
            b_refs[a][0] = s.astype(BF16)

            @pl.when(own)
            def _():
                f_refs[a][...] = s

    if kind == "rows":
        gspec = pl.BlockSpec((tr, qc), lambda r, q, pc: (q * (2 * nr) + pc[1] * nr + r, 0))
    else:
        gspec = pl.BlockSpec((tr, qc), lambda r, q, pc: (pc[1] * nr + r, q))
    lspec = pl.BlockSpec((1, tr, qc), lambda r, q, pc: (q, r, 0))
    res = pl.pallas_call(
        body, name=name,
        grid_spec=pltpu.PrefetchScalarGridSpec(
            num_scalar_prefetch=1, grid=(nr, NS), in_specs=[gspec] * n + [lspec] * n,
            out_specs=[lspec] * n + [pl.BlockSpec((tr, qc), lambda r, q, pc: (r, 0))] * n),
        out_shape=[jax.ShapeDtypeStruct((NS, h, qc), BF16)] * n + [jax.ShapeDtypeStruct((h, qc), F32)] * n,
        compiler_params=_cparams(2, 48),
    )(place, *grads, *got)
    return res[:n], res[n:]


def _totals(place, own, got, name):
    n = len(own)
    h, qc = own[0].shape
    tr = _row_tile(h)
    nr = h // tr
    got = [list(g) if isinstance(g, (list, tuple)) else [g] for g in got]
    m = len(got[0])

    def body(pc_ref, *refs):
        own_refs, got_refs, o_refs = refs[:n], refs[n:n + n * m], refs[n + n * m:]
        for a in range(n):
            acc = own_refs[a][...]
            for g in got_refs[a * m:(a + 1) * m]:
                for k in range(g.shape[0]):
                    acc = acc + g[k].astype(F32)
            o_refs[a][...] = acc

    lands = [pl.BlockSpec((g.shape[0], tr, qc), lambda r, pc: (0, r, 0)) for gs in got for g in gs]
    return pl.pallas_call(
        body, name=name,
        grid_spec=pltpu.PrefetchScalarGridSpec(
            num_scalar_prefetch=1, grid=(nr,),
            in_specs=[pl.BlockSpec((tr, qc), lambda r, pc: (r, 0))] * n + lands,
            out_specs=[pl.BlockSpec((tr, qc), lambda r, pc: (pc[1] * nr + r, 0))] * n),
        out_shape=[jax.ShapeDtypeStruct((2 * h, qc), F32)] * n,
        compiler_params=_cparams(1, 48),
    )(place, *own, *[g for gs in got for g in gs])


def _adamw(ws, gs, ms, vs, name):
    n = len(ws)
    R, C = ws[0].shape
    tr = _row_tile(R, (36 << 20) // (7 * 2 * 4 * n * C))

    def body(*refs):
        w_refs, g_refs, m_refs, v_refs = refs[:n], refs[n:2 * n], refs[2 * n:3 * n], refs[3 * n:4 * n]
        d_refs, mo_refs, vo_refs = refs[4 * n:5 * n], refs[5 * n:6 * n], refs[6 * n:]
        for a in range(n):
            d_refs[a][...], mo_refs[a][...], vo_refs[a][...] = _adamw_math(w_refs[a][...], g_refs[a][...], m_refs[a][...],
                                                                         v_refs[a][...])

    blk = pl.BlockSpec((tr, C), lambda r: (r, 0))
    res = pl.pallas_call(
        body, name=name, grid=(R // tr,),
        in_specs=[blk] * (4 * n), out_specs=[blk] * (3 * n),
        out_shape=[jax.ShapeDtypeStruct((R, C), F32)] * (3 * n),
        compiler_params=_cparams(1, 56),
    )(*ws, *gs, *ms, *vs)
    return res[:n], res[n:2 * n], res[2 * n:]


def kernel(x, ffn1_norm, ffn1_w_gate, ffn1_w_up, ffn1_w_down, mix_norm, w_in, a_dw_w, a_dw_b, a_ln_g, a_ln_b, a_w_out, b_conv_w, b_w_out, w_o, ffn2_norm, ffn2_w_gate, ffn2_w_up, ffn2_w_down, final_norm, loss_target, m_ffn1_norm, m_ffn1_w_gate, m_ffn1_w_up, m_ffn1_w_down, m_mix_norm, m_w_in, m_a_dw_w, m_a_dw_b, m_a_ln_g, m_a_ln_b, m_a_w_out, m_b_conv_w, m_b_w_out, m_w_o, m_ffn2_norm, m_ffn2_w_gate, m_ffn2_w_up, m_ffn2_w_down, m_final_norm, v_ffn1_norm, v_ffn1_w_gate, v_ffn1_w_up, v_ffn1_w_down, v_mix_norm, v_w_in, v_a_dw_w, v_a_dw_b, v_a_ln_g, v_a_ln_b, v_a_w_out, v_b_conv_w, v_b_w_out, v_w_o, v_ffn2_norm, v_ffn2_w_gate, v_ffn2_w_up, v_ffn2_w_down, v_final_norm):
    names = ["ffn1_norm", "ffn1_w_gate", "ffn1_w_up", "ffn1_w_down", "mix_norm", "w_in", "a_dw_w", "a_dw_b", "a_ln_g",
             "a_ln_b", "a_w_out", "b_conv_w", "b_w_out", "w_o", "ffn2_norm", "ffn2_w_gate", "ffn2_w_up", "ffn2_w_down",
             "final_norm"]
    W = dict(zip(names, [ffn1_norm, ffn1_w_gate, ffn1_w_up, ffn1_w_down, mix_norm, w_in, a_dw_w, a_dw_b, a_ln_g, a_ln_b,
                         a_w_out, b_conv_w, b_w_out, w_o, ffn2_norm, ffn2_w_gate, ffn2_w_up, ffn2_w_down, final_norm]))
    M = dict(zip(names, [m_ffn1_norm, m_ffn1_w_gate, m_ffn1_w_up, m_ffn1_w_down, m_mix_norm, m_w_in, m_a_dw_w, m_a_dw_b,
                         m_a_ln_g, m_a_ln_b, m_a_w_out, m_b_conv_w, m_b_w_out, m_w_o, m_ffn2_norm, m_ffn2_w_gate,
                         m_ffn2_w_up, m_ffn2_w_down, m_final_norm]))
    V = dict(zip(names, [v_ffn1_norm, v_ffn1_w_gate, v_ffn1_w_up, v_ffn1_w_down, v_mix_norm, v_w_in, v_a_dw_w, v_a_dw_b,
                         v_a_ln_g, v_a_ln_b, v_a_w_out, v_b_conv_w, v_b_w_out, v_w_o, v_ffn2_norm, v_ffn2_w_gate,
                         v_ffn2_w_up, v_ffn2_w_down, v_final_norm]))
    transposed = ("ffn1_w_gate", "ffn1_w_up", "ffn2_w_gate", "ffn2_w_up")
    vecs = ["ffn1_norm", "mix_norm", "a_dw_b", "a_ln_g", "a_ln_b", "ffn2_norm", "final_norm"]
    ffn1 = ["ffn1_w_gate", "ffn1_w_up", "ffn1_w_down"]
    ffn2 = ["ffn2_w_gate", "ffn2_w_up", "ffn2_w_down"]
    outp = ["a_w_out", "b_w_out", "w_o"]

    S, D = x.shape[1], x.shape[2]
    CB = D // NS
    KA, KB = a_dw_w.shape[1], b_conv_w.shape[1]
    px, py, pc = lax.axis_index("x"), lax.axis_index("y"), lax.axis_index("c")
    chip = 2 * px + py
    place = jnp.stack([chip, pc]).astype(jnp.int32)
    h0 = x.reshape(S, D)
    tgt = loss_target.reshape(S, D)
    row = lambda n: pltpu.with_memory_space_constraint(W[n].reshape(1, D), pltpu.HBM)
    pad = lambda a, r: jnp.concatenate([a, jnp.zeros((r - a.shape[0], a.shape[1]), F32)], axis=0)

    def quarter(P, n):
        return jnp.transpose(P[n][0]) if n in transposed else P[n][0]

    def unquarter(a, n):
        return (jnp.transpose(a) if n in transposed else a).reshape(W[n].shape)

    wq = {n: quarter(W, n).astype(BF16) for n in ffn1 + ffn2 + outp + ["w_in"]}

    f1 = _exchange("gather_ffn1", [_Gather([wq[n] for n in ffn1], ["rows"] * 3)])[0]
    g_in = _Gather([wq["w_in"], pad(a_dw_w[0], 32), pad(b_conv_w[0], 16)], ["rows"] * 3)
    (h1, n1, gp1, up1), ((win, taps_a, taps_b),) = _ffn_fwd(h0, row("ffn1_norm"), *f1, "ffn1_fwd", [g_in])
    win = win.reshape(NS, D, -1)
    wa_taps = taps_a.reshape(NS, 32, CB)[:, :KA]
    wb_taps = taps_b.reshape(NS, 16, CB)[:, :KB]
    g_out = _Gather([wq[n] for n in outp], ["rows"] * 3)
    (u, z), ((wa_out, wb_out, wo),) = _mix_in_fwd(h1, row("mix_norm"), win, [g_out])
    g_f2 = _Gather([wq["ffn2_w_gate"], wq["ffn2_w_up"]], ["rows"] * 2)
    (a1, q), ((f2g, f2u),) = _conv_fwd(z, wa_taps, row("a_dw_b"), wb_taps, [g_f2])
    (h2, a3, mm, ya, yb), ((f2d,),) = _mix_out_fwd(h1, a1, q, z, row("a_ln_g"), row("a_ln_b"), wa_out, wb_out, wo,
                                                   [_Gather([wq["ffn2_w_down"]], ["rows"])])
    (dh3, do2, d_final, loss_part, n2, gp2, up2), _ = _ffn_fwd(h2, row("ffn2_norm"), f2g, f2u, f2d, "ffn2_fwd_loss",
                                                               head=(tgt, row("final_norm")))

    (dgp2, dup2, act2), _ = _ffn_bwd_hidden(do2, gp2, up2, f2d, "ffn2_bwd_hidden")
    (dh2, d_ffn2), _ = _ffn_bwd_input(dh3, h2, row("ffn2_norm"), dgp2, dup2, f2g, f2u, "ffn2_bwd_input")
    g2 = [_ffn_wgrad(dgp2, n2, "ffn2_dwg")[0], _ffn_wgrad(dup2, n2, "ffn2_dwu")[0], _ffn_wgrad(act2, do2, "ffn2_dwd")[0]]
    (da1, dq, dga, dgb, dya, dyb, dh2b, d_lng, d_lnb), (got,) = _mix_out_bwd(
        dh2, a1, z, ya, yb, row("a_ln_g"), row("a_ln_b"), wa_out, wb_out, wo, [_ToSibling(g2, ["rows"] * 3)])
    wire2, own2 = _chip_sums(place, g2, got, "rows", "ffn2_chip_sums")
    (dz, d_wa, d_ba, d_wb), (got,) = _conv_bwd(z, da1, dq, dga, dgb, wa_taps, wb_taps, [_ToChips(wire2)])
    half2 = _totals(place, own2, got, "ffn2_totals")
    (g_win,), (tot2,) = _w_in_grad(u, dz, [_SwapHalves(half2)])
    go, (got,) = _mixer_wgrads(a3, dya, q, dyb, mm, dh2b, [_ToSibling([g_win], ["cols"])])
    wire_in, own_in = _chip_sums(place, [g_win], got, "cols", "w_in_chip_sum")
    (dh1, d_mix, do1), (land_in, got_o) = _mix_in_bwd(dh2, h1, row("mix_norm"), dz, win,
                                                      [_ToChips(wire_in), _ToSibling(go, ["rows"] * 3)])
    half_in = _totals(place, own_in, land_in, "w_in_total")
    wire_o, own_o = _chip_sums(place, go, got_o, "rows", "mixer_chip_sums")
    (dgp1, dup1, act1), (tot_in, land_o) = _ffn_bwd_hidden(do1, gp1, up1, f1[2], "ffn1_bwd_hidden",
                                                           [_SwapHalves(half_in), _ToChips(wire_o)])
    half_o = _totals(place, own_o, land_o, "mixer_totals")
    g1g, (tot_o,) = _ffn_wgrad(dgp1, n1, "ffn1_dwg", [_SwapHalves(half_o)])
    g1u, (got_g,) = _ffn_wgrad(dup1, n1, "ffn1_dwu", [_ToSibling([g1g], ["rows"])])
    wire_g, own_g = _chip_sums(place, [g1g], got_g, "rows", "ffn1_dwg_chip_sum")
    g1d, (got_u, land_g) = _ffn_wgrad(act1, do1, "ffn1_dwd", [_ToSibling([g1u], ["rows"]), _ToChips(wire_g)])
    wire_u, own_u = _chip_sums(place, [g1u], got_u, "rows", "ffn1_dwu_chip_sum")
    half_g = _totals(place, own_g, land_g, "ffn1_dwg_total")
    ffn1_in = (dh1, h0, row("ffn1_norm"), dgp1, dup1, f1[0], f1[1])
    (dx, dg_a), (got_d, land_u, tot_g) = _ffn_bwd_input(
        *ffn1_in, "ffn1_bwd_input_a", [_ToSibling([g1d], ["rows"]), _ToChips(wire_u), _SwapHalves(half_g)], part=(0, 2))
    wire_d, own_d = _chip_sums(place, [g1d], got_d, "rows", "ffn1_dwd_chip_sum")
    half_u = _totals(place, own_u, land_u, "ffn1_dwu_total")
    (dx, dg_b), (land_d, tot_u) = _ffn_bwd_input(*ffn1_in, "ffn1_bwd_input_b", [_ToChips(wire_d), _SwapHalves(half_u)],
                                                 part=(1, 2), into=dx)
    half_d = _totals(place, own_d, land_d, "ffn1_dwd_total")
    (tot_d,) = _exchange("tail_exchange", [_SwapHalves(half_d)])
    tot1 = [tot_g[0], tot_u[0], tot_d[0]]
    totals = dict(zip(ffn2 + ["w_in"] + ffn1 + outp, list(tot2) + list(tot_in) + tot1 + list(tot_o)))

    vec_grads = {"ffn1_norm": [dg_a, dg_b], "mix_norm": [d_mix], "a_dw_b": [d_ba], "a_ln_g": [d_lng], "a_ln_b": [d_lnb],
                 "ffn2_norm": [d_ffn2], "final_norm": [d_final]}
    small = _allreduce_small([vec_grads[n] for n in vecs], d_wa, d_wb, loss_part)
    loss = small[LOSS_ROW, 0]
    taps = ["a_dw_w", "b_conv_w"]
    small_out = _small_adamw(place, small, [[P[n].reshape(1, D) for P in (W, M, V)] for n in vecs],
                             [[P[n] for P in (W, M, V)] for n in taps])

    grads, deltas, new_m, new_v = {}, {}, {}, {}
    for n, (g_, d_, m_, v_) in zip(vecs + taps, small_out):
        shp = W[n].shape
        grads[n], deltas[n], new_m[n], new_v[n] = g_.reshape(shp), d_.reshape(shp), m_.reshape(shp), v_.reshape(shp)
    for group, tag in ((ffn1 + ffn2, "ffn"), (["w_in"], "w_in"), (outp, "mixer")):
        ds, ms, vs = _adamw([quarter(W, n) for n in group], [totals[n] for n in group], [quarter(M, n) for n in group],
                            [quarter(V, n) for n in group], tag + "_adamw")
        for n, d_, m_, v_ in zip(group, ds, ms, vs):
            grads[n], deltas[n], new_m[n], new_v[n] = (unquarter(totals[n], n), unquarter(d_, n), unquarter(m_, n),
                                                       unquarter(v_, n))
    return (loss, dx.reshape(x.shape), *[grads[n] for n in names], *[deltas[n] for n in names],
            *[new_m[n] for n in names], *[new_v[n] for n in names])
```
